```python
import jax, jax.numpy as jnp
from jax import lax
import numpy as np

D_MODEL = 2048
BATCH = 2
SEQ = 8192
DEPTH = 1

CHUNK = 64
Q_BLOCK = 128
HEAD_DIM = 128
N_FOX_HEADS = D_MODEL // (2 * HEAD_DIM)
N_RET_HEADS = D_MODEL // (2 * HEAD_DIM)
D_FOX = N_FOX_HEADS * HEAD_DIM
D_RET = N_RET_HEADS * HEAD_DIM
D_MIX = D_FOX + D_RET
D_IN = 3 * D_FOX + N_FOX_HEADS + 4 * D_RET
N_GROUPS = 4
EXPERTS_PER_GROUP = 8
N_EXPERTS = N_GROUPS * EXPERTS_PER_GROUP
TOP_K = 2
D_EXPERT = D_MODEL // 4
MOE_BLOCK = 128
ROPE_BASE = 10000.0
EPS = 1e-6
FORGET_BIAS_INIT = 3.0

kernel_name = "hybrid_fox_retention_hmoe_adaln"


def rms_norm(x, w):
    xf = x.astype(jnp.float32)
    y = xf * lax.rsqrt(jnp.mean(xf * xf, axis=-1, keepdims=True) + EPS)
    return (y * w.astype(jnp.float32)).astype(x.dtype)


def rotate(x, pos):
    half = HEAD_DIM // 2
    theta = ROPE_BASE ** (-jnp.arange(half, dtype=jnp.float32) / half)
    ang = pos[:, None] * theta[None, :]
    cos = jnp.cos(ang)[None, :, None, :]
    sin = jnp.sin(ang)[None, :, None, :]
    x1, x2 = x[..., :half], x[..., half:]
    return jnp.concatenate([x1 * cos - x2 * sin, x1 * sin + x2 * cos], axis=-1).astype(x.dtype)


def forgetting_attention(q, k, v, log_f):
    B, H, S, d = q.shape
    nb = S // Q_BLOCK
    cum = jnp.cumsum(log_f, axis=-1)
    key_pos = jnp.arange(S)
    scale = HEAD_DIM ** -0.5
    qb = q.reshape(B, H, nb, Q_BLOCK, d).transpose(2, 0, 1, 3, 4)
    cb = cum.reshape(B, H, nb, Q_BLOCK).transpose(2, 0, 1, 3)

    def block(args):
        i, q_i, c_i = args
        s = jnp.einsum('bhqd,bhkd->bhqk', q_i, k).astype(jnp.float32) * scale
        s = s + c_i[..., :, None] - cum[..., None, :]
        q_pos = i * Q_BLOCK + jnp.arange(Q_BLOCK)
        s = jnp.where(key_pos[None, :] <= q_pos[:, None], s, -jnp.inf)
        p = jax.nn.softmax(s, axis=-1).astype(v.dtype)
        return jnp.einsum('bhqk,bhkd->bhqd', p, v)

    out = lax.map(block, (jnp.arange(nb), qb, cb))
    return out.transpose(1, 2, 0, 3, 4).reshape(B, H, S, d)


def retention(q, k, v):
    B, H, S, d = q.shape
    nc = S // CHUNK
    log_g = jnp.log(1.0 - 2.0 ** (-5.0 - jnp.arange(H, dtype=jnp.float32)))
    idx = jnp.arange(CHUNK, dtype=jnp.float32)
    diff = idx[:, None] - idx[None, :]
    decay = jnp.where(diff >= 0, jnp.exp(log_g[:, None, None] * jnp.maximum(diff, 0.0)), 0.0)
    q_decay = jnp.exp(log_g[:, None] * (idx + 1.0))
    k_decay = jnp.exp(log_g[:, None] * (CHUNK - 1.0 - idx))
    chunk_decay = jnp.exp(log_g * CHUNK)
    qc = q.reshape(B, H, nc, CHUNK, d)
    kc = k.reshape(B, H, nc, CHUNK, d) * (HEAD_DIM ** -0.5)
    vc = v.reshape(B, H, nc, CHUNK, d)
    scores = jnp.einsum('bhnid,bhnjd->bhnij', qc, kc) * decay[None, :, None]
    intra = jnp.einsum('bhnij,bhnje->bhnie', scores, vc)
    kv = jnp.einsum('bhnjd,bhnje->bhnde', kc * k_decay[None, :, None, :, None], vc)

    def step(state, kv_i):
        return state * chunk_decay[None, :, None, None] + kv_i, state

    _, prev = lax.scan(step, jnp.zeros((B, H, d, d), kv.dtype), kv.transpose(2, 0, 1, 3, 4))
    prev = prev.transpose(1, 2, 0, 3, 4)
    inter = jnp.einsum('bhnid,bhnde->bhnie', qc * q_decay[None, :, None, :, None], prev)
    return (intra + inter).reshape(B, H, S, d).astype(v.dtype)


def hierarchical_moe(h, w_coarse, b_coarse, w_fine, b_fine, w1, w3, w2):
    N, D = h.shape
    hf = h.astype(jnp.float32)
    p_group = jax.nn.softmax(hf @ w_coarse.astype(jnp.float32) + b_coarse.astype(jnp.float32), axis=-1)
    g_sel = jnp.argmax(p_group, axis=-1)
    p_g = jnp.take_along_axis(p_group, g_sel[:, None], axis=-1)
    fine_all = jnp.einsum('nd,gde->nge', hf, w_fine.astype(jnp.float32)) + b_fine.astype(jnp.float32)
    fine = jnp.take_along_axis(fine_all, g_sel[:, None, None], axis=1)[:, 0]
    top_p, top_e = lax.top_k(jax.nn.softmax(fine, axis=-1), TOP_K)
    gate = p_g * top_p / jnp.sum(top_p, axis=-1, keepdims=True)
    eid = g_sel[:, None] * EXPERTS_PER_GROUP + top_e
    NK = N * TOP_K
    flat_e = eid.reshape(NK).astype(jnp.int32)
    flat_w = gate.reshape(NK)
    flat_tok = jnp.repeat(jnp.arange(N, dtype=jnp.int32), TOP_K)
    order = jnp.argsort(flat_e)
    e_s, tok_s, w_s = flat_e[order], flat_tok[order], flat_w[order]
    counts = jnp.zeros((N_EXPERTS,), jnp.int32).at[flat_e].add(1)
    padded = (counts + MOE_BLOCK - 1) // MOE_BLOCK * MOE_BLOCK
    start = jnp.cumsum(counts) - counts
    pstart = jnp.cumsum(padded) - padded
    pend = pstart + padded
    dest = pstart[e_s] + (jnp.arange(NK, dtype=jnp.int32) - start[e_s])
    nb = NK // MOE_BLOCK + N_EXPERTS
    slot_tok = jnp.full((nb * MOE_BLOCK,), N, jnp.int32).at[dest].set(tok_s)
    slot_w = jnp.zeros((nb * MOE_BLOCK,), h.dtype).at[dest].set(w_s.astype(h.dtype))
    block_start = jnp.arange(nb, dtype=jnp.int32) * MOE_BLOCK
    block_e = jnp.minimum(jnp.searchsorted(pend, block_start, side='right'), N_EXPERTS - 1)
    xpad = jnp.concatenate([h, jnp.zeros((1, D), h.dtype)], axis=0)
    xb = xpad[slot_tok].reshape(nb, MOE_BLOCK, D)

    def expert_block(args):
        x_i, e = args
        return (jax.nn.silu(x_i @ w1[e]) * (x_i @ w3[e])) @ w2[e]

    yb = lax.map(expert_block, (xb, block_e))
    y = jnp.zeros((N + 1, D), yb.dtype).at[slot_tok].add(yb.reshape(-1, D) * slot_w[:, None])
    return y[:N]


def setup_inputs(seed: int = 0) -> dict:
    key = jax.random.key(seed)
    ks = jax.random.split(key, 20)
    L, D = DEPTH, D_MODEL
    nrm = jax.random.normal
    return {
        "x": nrm(ks[0], (BATCH, SEQ, D), jnp.float32),
        "c": nrm(ks[1], (BATCH, D), jnp.float32),
        "w_ada": nrm(ks[2], (L, D, 6 * D), jnp.float32) * D ** -0.5,
        "b_ada": 0.02 * nrm(ks[3], (L, 6 * D), jnp.float32),
        "norm1_w": 1.0 + 0.01 * nrm(ks[4], (L, D), jnp.float32),
        "w_in": nrm(ks[5], (L, D, D_IN), jnp.float32) * D ** -0.5,
        "forget_bias": FORGET_BIAS_INIT + 0.1 * nrm(ks[6], (L, N_FOX_HEADS), jnp.float32),
        "q_norm_w": 1.0 + 0.01 * nrm(ks[7], (L, HEAD_DIM), jnp.float32),
        "k_norm_w": 1.0 + 0.01 * nrm(ks[8], (L, HEAD_DIM), jnp.float32),
        "ret_norm_w": 1.0 + 0.01 * nrm(ks[9], (L, D_RET), jnp.float32),
        "w_out": nrm(ks[10], (L, D_MIX, D), jnp.float32) * D_MIX ** -0.5,
        "norm2_w": 1.0 + 0.01 * nrm(ks[11], (L, D), jnp.float32),
        "w_coarse": nrm(ks[12], (L, D, N_GROUPS), jnp.float32) * D ** -0.5,
        "b_coarse": 0.01 * nrm(ks[13], (L, N_GROUPS), jnp.float32),
        "w_fine": nrm(ks[14], (L, N_GROUPS, D, EXPERTS_PER_GROUP), jnp.float32) * D ** -0.5,
        "b_fine": 0.01 * nrm(ks[15], (L, N_GROUPS, EXPERTS_PER_GROUP), jnp.float32),
        "w1": nrm(ks[16], (L, N_EXPERTS, D, D_EXPERT), jnp.float32) * D ** -0.5,
        "w3": nrm(ks[17], (L, N_EXPERTS, D, D_EXPERT), jnp.float32) * D ** -0.5,
        "w2": nrm(ks[18], (L, N_EXPERTS, D_EXPERT, D), jnp.float32) * D_EXPERT ** -0.5,
    }


def reference(x, c, w_ada, b_ada, norm1_w, w_in, forget_bias, q_norm_w, k_norm_w, ret_norm_w,
              w_out, norm2_w, w_coarse, b_coarse, w_fine, b_fine, w1, w3, w2):
    B, S, D = x.shape
    H, d = N_FOX_HEADS, HEAD_DIM
    Hr = N_RET_HEADS
    splits = [D_FOX, 2 * D_FOX, 3 * D_FOX, 3 * D_FOX + H,
              3 * D_FOX + H + D_RET, 3 * D_FOX + H + 2 * D_RET, 3 * D_FOX + H + 3 * D_RET]
    pos = jnp.arange(S, dtype=jnp.float32)
    c_act = jax.nn.silu(c)
    for l in range(DEPTH):
        mod = c_act @ w_ada[l] + b_ada[l]
        sh1, sc1, g1, sh2, sc2, g2 = jnp.split(mod[:, None, :], 6, axis=-1)
        h = rms_norm(x, norm1_w[l]) * (1.0 + sc1) + sh1
        z = h @ w_in[l]
        qa, ka, va, fa, qb, kb, vb, gb = jnp.split(z, splits, axis=-1)
        qa = rms_norm(qa.reshape(B, S, H, d), q_norm_w[l]).transpose(0, 2, 1, 3)
        ka = rms_norm(ka.reshape(B, S, H, d), k_norm_w[l]).transpose(0, 2, 1, 3)
        va = va.reshape(B, S, H, d).transpose(0, 2, 1, 3)
        log_f = jax.nn.log_sigmoid(fa.astype(jnp.float32) + forget_bias[l].astype(jnp.float32))
        o_a = forgetting_attention(qa, ka, va, log_f.transpose(0, 2, 1))
        o_a = o_a.transpose(0, 2, 1, 3).reshape(B, S, D_FOX)
        qb = rotate(qb.reshape(B, S, Hr, d), pos).transpose(0, 2, 1, 3)
        kb = rotate(kb.reshape(B, S, Hr, d), pos).transpose(0, 2, 1, 3)
        vb = vb.reshape(B, S, Hr, d).transpose(0, 2, 1, 3)
        o_b = retention(qb, kb, vb).transpose(0, 2, 1, 3)
        o_b = rms_norm(o_b, ret_norm_w[l].reshape(Hr, d)).reshape(B, S, D_RET)
        o_b = o_b * jax.nn.silu(gb)
        mix = jnp.concatenate([o_a, o_b], axis=-1) @ w_out[l]
        x = x + g1 * mix
        h2 = rms_norm(x, norm2_w[l]) * (1.0 + sc2) + sh2
        y = hierarchical_moe(h2.reshape(B * S, D), w_coarse[l], b_coarse[l], w_fine[l], b_fine[l],
                             w1[l], w3[l], w2[l])
        x = x + g2 * y.reshape(B, S, D).astype(x.dtype)
    return x
```

```python
import functools

import jax
import jax.numpy as jnp
from jax import lax
from jax.experimental import pallas as pl
from jax.experimental.pallas import tpu as pltpu

HEAD_DIM = 128
N_GROUPS = 4
EXPERTS_PER_GROUP = 8
N_EXPERTS = N_GROUPS * EXPERTS_PER_GROUP
TOP_K = 2
ROPE_BASE = 10000.0
EPS = 1e-6

LANES = 128
VMEM_LIMIT = 56 * 1024 * 1024
NEG_BIG = -1e30

F32 = jnp.float32
BF16 = jnp.bfloat16
U32 = jnp.uint32


def _params(*sem):
    return pltpu.CompilerParams(dimension_semantics=sem, vmem_limit_bytes=VMEM_LIMIT)


def _silu(v):
    return v * (1.0 / (1.0 + jnp.exp(-v)))


def _pack_halves(y):
    w = y.shape[1] // 2
    lo = pltpu.bitcast(y[:, :w].astype(BF16).astype(F32), U32)
    hi = pltpu.bitcast(y[:, w:].astype(BF16).astype(F32), U32)
    return (hi & jnp.uint32(0xFFFF0000)) | (lo >> 16)


def _unpack_halves(p):
    lo = pltpu.bitcast(p << 16, F32)
    hi = pltpu.bitcast(p & jnp.uint32(0xFFFF0000), F32)
    return lo, hi


def _ada_kernel(ct_ref, w_ref, b_ref, o_ref):
    w = w_ref[...]
    rows = []
    for b in range(o_ref.shape[0]):
        if b < 2:
            cb = _silu(ct_ref[:, b:b + 1])
            rows.append(jnp.sum(cb * w, axis=0, keepdims=True) + b_ref[...])
        else:
            rows.append(jnp.zeros_like(b_ref[...]))
    o_ref[...] = jnp.concatenate(rows, axis=0)


def _ada_modulation(c, w_ada, b_ada):
    bsz, d = c.shape
    n = w_ada.shape[1]
    tn = 1024
    ct = jnp.zeros((d, LANES), F32).at[:, :bsz].set(c.T)
    out = pl.pallas_call(
        _ada_kernel,
        grid=(n // tn,),
        in_specs=[pl.BlockSpec((d, LANES), lambda j: (0, 0)),
                  pl.BlockSpec((d, tn), lambda j: (0, j)),
                  pl.BlockSpec((1, tn), lambda j: (0, j))],
        out_specs=pl.BlockSpec((8, tn), lambda j: (0, j)),
        out_shape=jax.ShapeDtypeStruct((8, n), F32),
        compiler_params=_params("arbitrary"),
    )(ct, w_ada, b_ada.reshape(1, n))
    return out[:bsz]


def _inproj_kernel(n_fox_tiles, n_main_tiles, x_ref, nw_ref, sc_ref, sh_ref, w_ref, cos_ref, sin_ref,
                   qw_ref, kw_ref, fb_ref, z_ref, f_ref, h_ref):
    j = pl.program_id(1)

    @pl.when(j == 0)
    def _():
        x = x_ref[...]
        ms = jnp.mean(x * x, axis=-1, keepdims=True)
        y = x * lax.rsqrt(ms + EPS) * nw_ref[...]
        h_ref[...] = (y * (1.0 + sc_ref[0]) + sh_ref[0]).astype(BF16)

    acc = jnp.dot(h_ref[...], w_ref[...], preferred_element_type=F32)
    tn = acc.shape[1]
    heads = tn // HEAD_DIM
    q_t = n_fox_tiles // 3
    r0 = n_fox_tiles
    r_t = (n_main_tiles - n_fox_tiles) // 4

    def head_norm(w_row):
        outs = []
        for hh in range(heads):
            a = acc[:, hh * HEAD_DIM:(hh + 1) * HEAD_DIM]
            ms = jnp.mean(a * a, axis=-1, keepdims=True)
            outs.append(a * lax.rsqrt(ms + EPS) * w_row)
        return jnp.concatenate(outs, axis=-1)

    def rotate(scale):
        outs = []
        cs = cos_ref[...] * scale
        sn = sin_ref[...] * scale
        for hh in range(heads):
            a = acc[:, hh * HEAD_DIM:(hh + 1) * HEAD_DIM]
            outs.append(a * cs + pltpu.roll(a, HEAD_DIM // 2, 1) * sn)
        return jnp.concatenate(outs, axis=-1)

    @pl.when(j < q_t)
    def _():
        z_ref[...] = head_norm(qw_ref[...] * (HEAD_DIM ** -0.5)).astype(BF16)

    @pl.when((j >= q_t) & (j < 2 * q_t))
    def _():
        z_ref[...] = head_norm(kw_ref[...]).astype(BF16)

    @pl.when((j >= r0) & (j < r0 + r_t))
    def _():
        z_ref[...] = rotate(1.0).astype(BF16)

    @pl.when((j >= r0 + r_t) & (j < r0 + 2 * r_t))
    def _():
        z_ref[...] = rotate(HEAD_DIM ** -0.5).astype(BF16)

    @pl.when(((j >= 2 * q_t) & (j < r0)) | ((j >= r0 + 2 * r_t) & (j < n_main_tiles)))
    def _():
        z_ref[...] = acc.astype(BF16)

    @pl.when(j == n_main_tiles)
    def _():
        t = acc[:, :LANES] + fb_ref[...]
        f_ref[...] = jnp.minimum(t, 0.0) - jnp.log(1.0 + jnp.exp(-jnp.abs(t)))


def _input_projection(x2d, seq, norm_w, sc1, sh1, w_all, cos_t, sin_t, qw, kw, fb, d_fox, d_main):
    n, d = x2d.shape
    tm, tn = 1024, 512
    tm = min(tm, seq)
    n_main_tiles = d_main // tn
    n_fox_tiles = 3 * d_fox // tn
    tiles_per_seq = seq // tm
    kern = functools.partial(_inproj_kernel, n_fox_tiles, n_main_tiles)
    bsel = lambda i, j: (i // tiles_per_seq, 0, 0)
    return pl.pallas_call(
        kern,
        grid=(n // tm, n_main_tiles + 1),
        in_specs=[pl.BlockSpec((tm, d), lambda i, j: (i, 0)),
                  pl.BlockSpec((1, d), lambda i, j: (0, 0)),
                  pl.BlockSpec((1, 1, d), bsel),
                  pl.BlockSpec((1, 1, d), bsel),
                  pl.BlockSpec((d, tn), lambda i, j: (0, j)),
                  pl.BlockSpec((tm, HEAD_DIM), lambda i, j: (i % tiles_per_seq, 0)),
                  pl.BlockSpec((tm, HEAD_DIM), lambda i, j: (i % tiles_per_seq, 0)),
                  pl.BlockSpec((1, HEAD_DIM), lambda i, j: (0, 0)),
                  pl.BlockSpec((1, HEAD_DIM), lambda i, j: (0, 0)),
                  pl.BlockSpec((1, LANES), lambda i, j: (0, 0))],
        out_specs=[pl.BlockSpec((tm, tn), lambda i, j: (i, jnp.minimum(j, n_main_tiles - 1))),
                   pl.BlockSpec((tm, LANES), lambda i, j: (i, 0))],
        out_shape=[jax.ShapeDtypeStruct((n, d_main), BF16),
                   jax.ShapeDtypeStruct((n, LANES), F32)],
        scratch_shapes=[pltpu.VMEM((tm, d), BF16)],
        compiler_params=_params("arbitrary", "arbitrary"),
    )(x2d, norm_w, sc1, sh1, w_all, cos_t, sin_t, qw, kw, fb)


def _cumsum_kernel(x_ref, o_ref):
    x = x_ref[0]
    r = x.shape[0]
    a = lax.broadcasted_iota(jnp.int32, (LANES, LANES), 0)
    b = lax.broadcasted_iota(jnp.int32, (LANES, LANES), 1)
    upper = (a <= b).astype(F32)
    within = jnp.dot(x, upper, precision=lax.Precision.HIGHEST, preferred_element_type=F32)
    tot = jnp.broadcast_to(within[:, LANES - 1:LANES], (r, LANES))
    ra = lax.broadcasted_iota(jnp.int32, (r, r), 0)
    rb = lax.broadcasted_iota(jnp.int32, (r, r), 1)
    strict = (rb < ra).astype(F32)
    before = jnp.dot(strict, tot, precision=lax.Precision.HIGHEST, preferred_element_type=F32)
    o_ref[0] = within + before


def _cumsum_rows(x):
    g, s = x.shape
    r = s // LANES
    out = pl.pallas_call(
        _cumsum_kernel,
        grid=(g,),
        in_specs=[pl.BlockSpec((1, r, LANES), lambda i: (i, 0, 0))],
        out_specs=pl.BlockSpec((1, r, LANES), lambda i: (i, 0, 0)),
        out_shape=jax.ShapeDtypeStruct((g, r, LANES), F32),
        compiler_params=_params("arbitrary"),
    )(x.reshape(g, r, LANES))
    return out.reshape(g, 1, s)


def _fox_kernel(tq, q_ref, k_ref, v_ref, cum_ref, o_ref, m_ref, l_ref, acc_ref):
    qi = pl.program_id(2)
    q = q_ref[...]
    q_start = pl.multiple_of(qi * tq, tq)
    c0 = cum_ref[0, :, pl.ds(q_start, LANES)][:, 0:1]

    m_ref[...] = jnp.full(m_ref.shape, NEG_BIG, F32)
    l_ref[...] = jnp.zeros(l_ref.shape, F32)
    acc_ref[...] = jnp.zeros(acc_ref.shape, F32)

    def block(start, masked):
        k = k_ref[pl.ds(start, tq), :]
        v = v_ref[pl.ds(start, tq), :]
        s = lax.dot_general(q, k, (((1,), (1,)), ((), ())), preferred_element_type=F32)
        s = s + (c0 - cum_ref[0, :, pl.ds(start, tq)])
        if masked:
            row = lax.broadcasted_iota(jnp.int32, s.shape, 0)
            col = lax.broadcasted_iota(jnp.int32, s.shape, 1)
            s = jnp.where(col <= row, s, NEG_BIG)
        m_prev = m_ref[...]
        m_new = jnp.maximum(m_prev, jnp.max(s, axis=-1, keepdims=True))
        p = jnp.exp(s - m_new)
        alpha = jnp.exp(m_prev - m_new)
        l_ref[...] = alpha * l_ref[...] + jnp.sum(p, axis=-1, keepdims=True)
        acc_ref[...] = alpha * acc_ref[...] + jnp.dot(p.astype(BF16), v, preferred_element_type=F32)
        m_ref[...] = m_new

    def body(kb, carry):
        block(pl.multiple_of(kb * tq, tq), False)
        return carry

    lax.fori_loop(0, qi, body, 0)
    block(q_start, True)
    o_ref[...] = (acc_ref[...] / l_ref[...]).astype(BF16)


def _fox_attention(z, cum, bsz, seq, n_heads):
    tq = min(512, seq)
    nq = seq // tq
    kern = functools.partial(_fox_kernel, tq)
    return pl.pallas_call(
        kern,
        grid=(bsz, n_heads, nq),
        in_specs=[pl.BlockSpec((tq, HEAD_DIM), lambda b, h, i: (b * nq + i, h)),
                  pl.BlockSpec((seq, HEAD_DIM), lambda b, h, i: (b, n_heads + h)),
                  pl.BlockSpec((seq, HEAD_DIM), lambda b, h, i: (b, 2 * n_heads + h)),
                  pl.BlockSpec((1, 1, seq), lambda b, h, i: (b * n_heads + h, 0, 0))],
        out_specs=pl.BlockSpec((tq, HEAD_DIM), lambda b, h, i: (b * nq + i, h)),
        out_shape=jax.ShapeDtypeStruct((bsz * seq, n_heads * HEAD_DIM), BF16),
        scratch_shapes=[pltpu.VMEM((tq, 1), F32), pltpu.VMEM((tq, 1), F32),
                        pltpu.VMEM((tq, HEAD_DIM), F32)],
        compiler_params=_params("arbitrary", "arbitrary", "arbitrary"),
    )(z, z, z, cum)


def _ret_kernel(chunk, n_chunks, lg_ref, q_ref, k_ref, v_ref, g_ref, nw_ref, o_ref, state_ref, decay_ref):
    h = pl.program_id(1)
    t = pl.program_id(2)
    log_g = lg_ref[h]

    @pl.when(t == 0)
    def _():
        state_ref[...] = jnp.zeros(state_ref.shape, F32)
        i = lax.broadcasted_iota(jnp.int32, (chunk, chunk), 0)
        jj = lax.broadcasted_iota(jnp.int32, (chunk, chunk), 1)
        diff = (i - jj).astype(F32)
        decay_ref[...] = jnp.where(diff >= 0, jnp.exp(log_g * jnp.maximum(diff, 0.0)), 0.0)

    pos = lax.broadcasted_iota(jnp.int32, (chunk, HEAD_DIM), 0).astype(F32)
    q_decay = jnp.exp(log_g * (pos + 1.0))
    k_decay = jnp.exp(log_g * (chunk - 1.0 - pos))
    chunk_decay = jnp.exp(jnp.full((1, HEAD_DIM), chunk, F32) * log_g)

    for c in range(n_chunks):
        rows = pl.ds(c * chunk, chunk)
        q = q_ref[rows, :]
        k = k_ref[rows, :]
        v = v_ref[rows, :]
        scores = lax.dot_general(q, k, (((1,), (1,)), ((), ())), preferred_element_type=F32)
        scores = scores * decay_ref[...]
        intra = jnp.dot(scores.astype(BF16), v, preferred_element_type=F32)
        state = state_ref[...]
        inter = jnp.dot(q, state.astype(BF16), preferred_element_type=F32) * q_decay
        kd = (k.astype(F32) * k_decay).astype(BF16)
        kv = lax.dot_general(kd, v, (((0,), (0,)), ((), ())), preferred_element_type=F32)
        state_ref[...] = state * chunk_decay + kv
        o = intra + inter
        ms = jnp.mean(o * o, axis=-1, keepdims=True)
        o = o * lax.rsqrt(ms + EPS) * nw_ref[...]
        o_ref[rows, :] = (o * _silu(g_ref[rows, :].astype(F32))).astype(BF16)


def _retention(z, log_g, norm_w, bsz, seq, n_heads, col0):
    chunk = min(256, seq)
    n_chunks = min(2, seq // chunk)
    tt = chunk * n_chunks
    nt = seq // tt
    c0 = col0 // HEAD_DIM
    kern = functools.partial(_ret_kernel, chunk, n_chunks)

    def sec(s):
        return pl.BlockSpec((tt, HEAD_DIM), lambda b, h, t, lg: (b * nt + t, c0 + s * n_heads + h))

    grid_spec = pltpu.PrefetchScalarGridSpec(
        num_scalar_prefetch=1,
        grid=(bsz, n_heads, nt),
        in_specs=[sec(0), sec(1), sec(2), sec(3),
                  pl.BlockSpec((1, HEAD_DIM), lambda b, h, t, lg: (0, h))],
        out_specs=pl.BlockSpec((tt, HEAD_DIM), lambda b, h, t, lg: (b * nt + t, h)),
        scratch_shapes=[pltpu.VMEM((HEAD_DIM, HEAD_DIM), F32), pltpu.VMEM((chunk, chunk), F32)],
    )
    return pl.pallas_call(
        kern,
        grid_spec=grid_spec,
        out_shape=jax.ShapeDtypeStruct((bsz * seq, n_heads * HEAD_DIM), BF16),
        compiler_params=_params("arbitrary", "arbitrary", "arbitrary"),
    )(log_g, z, z, z, z, norm_w)


def _outproj_kernel(oa_ref, ob_ref, wa_ref, wb_ref, x_ref, g1_ref, nw_ref, sc_ref, sh_ref, wr_ref, br_ref,
                    x1_ref, hp_ref, lg_ref):
    mix = jnp.dot(oa_ref[...], wa_ref[...], preferred_element_type=F32)
    mix = mix + jnp.dot(ob_ref[...], wb_ref[...], preferred_element_type=F32)
    x1 = x_ref[...] + g1_ref[0] * mix
    x1_ref[...] = x1
    ms = jnp.mean(x1 * x1, axis=-1, keepdims=True)
    h2 = x1 * lax.rsqrt(ms + EPS) * nw_ref[...] * (1.0 + sc_ref[0]) + sh_ref[0]
    hp_ref[...] = _pack_halves(h2)
    lg_ref[...] = jnp.dot(h2, wr_ref[...], precision=lax.Precision.HIGHEST,
                          preferred_element_type=F32) + br_ref[...]


def _output_projection(o_a, o_b, w_out, x2d, seq, g1, norm_w, sc2, sh2, w_router, b_router):
    n, d = x2d.shape
    da = o_a.shape[1]
    tm = min(256, seq)
    tiles_per_seq = seq // tm
    bsel = lambda i: (i // tiles_per_seq, 0, 0)
    return pl.pallas_call(
        _outproj_kernel,
        grid=(n // tm,),
        in_specs=[pl.BlockSpec((tm, da), lambda i: (i, 0)),
                  pl.BlockSpec((tm, da), lambda i: (i, 0)),
                  pl.BlockSpec((da, d), lambda i: (0, 0)),
                  pl.BlockSpec((da, d), lambda i: (1, 0)),
                  pl.BlockSpec((tm, d), lambda i: (i, 0)),
                  pl.BlockSpec((1, 1, d), bsel),
                  pl.BlockSpec((1, d), lambda i: (0, 0)),
                  pl.BlockSpec((1, 1, d), bsel),
                  pl.BlockSpec((1, 1, d), bsel),
                  pl.BlockSpec((d, LANES), lambda i: (0, 0)),
                  pl.BlockSpec((1, LANES), lambda i: (0, 0))],
        out_specs=[pl.BlockSpec((tm, d), lambda i: (i, 0)),
                   pl.BlockSpec((tm, d // 2), lambda i: (i, 0)),
                   pl.BlockSpec((tm, LANES), lambda i: (i, 0))],
        out_shape=[jax.ShapeDtypeStruct((n, d), F32),
                   jax.ShapeDtypeStruct((n, d // 2), U32),
                   jax.ShapeDtypeStruct((n, LANES), F32)],
        compiler_params=_params("arbitrary"),
    )(o_a, o_b, w_out, w_out, x2d, g1, norm_w, sc2, sh2, w_router, b_router)


def _route_kernel(lg_ref, gate_ref, eid_ref, rank_ref, cnt_ref, run_ref):
    i = pl.program_id(0)

    @pl.when(i == 0)
    def _():
        run_ref[...] = jnp.zeros(run_ref.shape, F32)

    lg = lg_ref[...]
    tt = lg.shape[0]
    lane = lax.broadcasted_iota(jnp.int32, lg.shape, 1).astype(F32)
    big = 1e6

    def rmax(v):
        return jnp.max(v, axis=-1, keepdims=True)

    def rmin(v):
        return jnp.min(v, axis=-1, keepdims=True)

    def rsum(v):
        return jnp.sum(v, axis=-1, keepdims=True)

    cmask = lane < N_GROUPS
    cm = jnp.where(cmask, lg, NEG_BIG)
    ce = jnp.where(cmask, jnp.exp(cm - rmax(cm)), 0.0)
    pgrp = ce / rsum(ce)
    p_g = rmax(pgrp)
    g_sel = rmin(jnp.where(cmask & (pgrp == p_g), lane, big))

    lo = N_GROUPS + EXPERTS_PER_GROUP * g_sel
    fmask = (lane >= lo) & (lane < lo + EXPERTS_PER_GROUP)
    fm = jnp.where(fmask, lg, NEG_BIG)
    fe = jnp.where(fmask, jnp.exp(fm - rmax(fm)), 0.0)
    fp = fe / rsum(fe)
    fp = jnp.where(fmask, fp, -1.0)
    p1 = rmax(fp)
    i1 = rmin(jnp.where(fp == p1, lane, big))
    fp2 = jnp.where(lane == i1, -1.0, fp)
    p2 = rmax(fp2)
    i2 = rmin(jnp.where(fp2 == p2, lane, big))
    denom = p1 + p2
    w1 = p_g * p1 / denom
    w2 = p_g * p2 / denom
    e1 = i1 - N_GROUPS
    e2 = i2 - N_GROUPS

    gate_ref[...] = jnp.where(lane == 0, w1, jnp.where(lane == 1, w2, 0.0))
    eid_ref[...] = jnp.where(lane == 0, e1, jnp.where(lane == 1, e2, 0.0)).astype(jnp.int32)

    oh1 = (lane == e1).astype(F32)
    oh2 = (lane == e2).astype(F32)
    both = oh1 + oh2
    ra = lax.broadcasted_iota(jnp.int32, (tt, tt), 0)
    rb = lax.broadcasted_iota(jnp.int32, (tt, tt), 1)
    strict = (rb < ra).astype(BF16)
    prefix = jnp.dot(strict, both.astype(BF16), preferred_element_type=F32) + run_ref[...]
    r1 = rsum(prefix * oh1)
    r2 = rsum(prefix * oh2)
    rank_ref[...] = jnp.where(lane == 0, r1, jnp.where(lane == 1, r2, 0.0)).astype(jnp.int32)
    run_ref[...] = run_ref[...] + jnp.sum(both, axis=0, keepdims=True)
    cnt_ref[...] = run_ref[...].astype(jnp.int32)


def _route(logits):
    n = logits.shape[0]
    tt = min(512, n)
    blk = lambda: pl.BlockSpec((tt, LANES), lambda i: (i, 0))
    return pl.pallas_call(
        _route_kernel,
        grid=(n // tt,),
        in_specs=[blk()],
        out_specs=[blk(), blk(), blk(), pl.BlockSpec((1, LANES), lambda i: (0, 0))],
        out_shape=[jax.ShapeDtypeStruct((n, LANES), F32),
                   jax.ShapeDtypeStruct((n, LANES), jnp.int32),
                   jax.ShapeDtypeStruct((n, LANES), jnp.int32),
                   jax.ShapeDtypeStruct((1, LANES), jnp.int32)],
        scratch_shapes=[pltpu.VMEM((1, LANES), F32)],
        compiler_params=_params("arbitrary"),
    )(logits)


def _dispatch_kernel(tt, blk, n_fill, dest_ref, fill_ref, h_ref, xs_ref, zero_ref, sem, zsem):
    i = pl.program_id(0)
    base = i * (tt * TOP_K)

    @pl.when(i == 0)
    def _():
        zero_ref[...] = jnp.zeros(zero_ref.shape, U32)

        def zcopy(z):
            row = pl.multiple_of(jnp.maximum(fill_ref[z], 0), blk)
            return pltpu.make_async_copy(zero_ref, xs_ref.at[pl.ds(row, blk), :], zsem)

        def zissue(z, carry):
            @pl.when(fill_ref[z] >= 0)
            def _():
                zcopy(z).start()
            return carry

        def zdrain(z, carry):
            @pl.when(fill_ref[z] >= 0)
            def _():
                zcopy(z).wait()
            return carry

        lax.fori_loop(0, n_fill, zissue, 0)
        lax.fori_loop(0, n_fill, zdrain, 0)

    def copy(r, kk):
        d = dest_ref[base + r * TOP_K + kk]
        return pltpu.make_async_copy(h_ref.at[pl.ds(r, 1), :], xs_ref.at[pl.ds(d, 1), :], sem)

    def issue(r, carry):
        for kk in range(TOP_K):
            copy(r, kk).start()
        return carry

    def drain(r, carry):
        for kk in range(TOP_K):
            copy(r, kk).wait()
        return carry

    lax.fori_loop(0, tt, issue, 0)
    lax.fori_loop(0, tt, drain, 0)


def _dispatch(h_packed, dest_flat, fill_rows, n_slots, blk):
    n, w = h_packed.shape
    tt = min(128, n)
    n_fill = fill_rows.shape[0]
    grid_spec = pltpu.PrefetchScalarGridSpec(
        num_scalar_prefetch=2,
        grid=(n // tt,),
        in_specs=[pl.BlockSpec((tt, w), lambda i, d, f: (i, 0))],
        out_specs=pl.BlockSpec(memory_space=pl.ANY),
        scratch_shapes=[pltpu.VMEM((blk, w), U32), pltpu.SemaphoreType.DMA(()), pltpu.SemaphoreType.DMA(())],
    )
    return pl.pallas_call(
        functools.partial(_dispatch_kernel, tt, blk, n_fill),
        grid_spec=grid_spec,
        out_shape=jax.ShapeDtypeStruct((n_slots, w), U32),
        compiler_params=_params("arbitrary"),
    )(dest_flat, fill_rows, h_packed)


def _expert_kernel(be_ref, nv_ref, xs_ref, w1_ref, w3_ref, w2_ref, y_ref, w1b, w3b, w2b):
    i = pl.program_id(0)
    prev = be_ref[jnp.maximum(i - 1, 0)]

    @pl.when((i == 0) | (be_ref[i] != prev))
    def _():
        w1b[...] = w1_ref[0].astype(BF16)
        w3b[...] = w3_ref[0].astype(BF16)
        w2b[...] = w2_ref[0].astype(BF16)

    nvalid = nv_ref[i]

    @pl.when(nvalid > 0)
    def _():
        lo, hi = _unpack_halves(xs_ref[...])
        lo = lo.astype(BF16)
        hi = hi.astype(BF16)
        half = lo.shape[1]
        a = jnp.dot(lo, w1b[:half, :], preferred_element_type=F32)
        a = a + jnp.dot(hi, w1b[half:, :], preferred_element_type=F32)
        b = jnp.dot(lo, w3b[:half, :], preferred_element_type=F32)
        b = b + jnp.dot(hi, w3b[half:, :], preferred_element_type=F32)
        mid = (_silu(a) * b).astype(BF16)
        y_ref[...] = _pack_halves(jnp.dot(mid, w2b[...], preferred_element_type=F32))

    @pl.when(nvalid <= 0)
    def _():
        y_ref[...] = jnp.zeros(y_ref.shape, U32)


def _expert_blocks(xs, block_e, nvalid, w1, w3, w2, blk):
    n_slots, w = xs.shape
    d = w1.shape[1]
    de = w1.shape[2]
    grid_spec = pltpu.PrefetchScalarGridSpec(
        num_scalar_prefetch=2,
        grid=(n_slots // blk,),
        in_specs=[pl.BlockSpec((blk, w), lambda i, be, nv: (i, 0)),
                  pl.BlockSpec((1, d, de), lambda i, be, nv: (be[i], 0, 0)),
                  pl.BlockSpec((1, d, de), lambda i, be, nv: (be[i], 0, 0)),
                  pl.BlockSpec((1, de, d), lambda i, be, nv: (be[i], 0, 0))],
        out_specs=pl.BlockSpec((blk, w), lambda i, be, nv: (i, 0)),
        scratch_shapes=[pltpu.VMEM((d, de), BF16), pltpu.VMEM((d, de), BF16), pltpu.VMEM((de, d), BF16)],
    )
    return pl.pallas_call(
        _expert_kernel,
        grid_spec=grid_spec,
        out_shape=jax.ShapeDtypeStruct((n_slots, w), U32),
        compiler_params=_params("arbitrary"),
    )(block_e, nvalid, xs, w1, w3, w2)


def _combine_kernel(tt, n_tiles, dest_ref, x1_ref, g2_ref, gate_ref, yb_ref, o_ref, buf, sems):
    i = pl.program_id(0)

    def copy(tile, slot, r, kk):
        d = dest_ref[(tile * tt + r) * TOP_K + kk]
        return pltpu.make_async_copy(yb_ref.at[pl.ds(d, 1), :], buf.at[slot, kk, pl.ds(r, 1), :], sems.at[slot])

    def issue_tile(tile, slot):
        def body(r, carry):
            for kk in range(TOP_K):
                copy(tile, slot, r, kk).start()
            return carry
        lax.fori_loop(0, tt, body, 0)

    def wait_tile(tile, slot):
        def body(r, carry):
            for kk in range(TOP_K):
                copy(tile, slot, r, kk).wait()
            return carry
        lax.fori_loop(0, tt, body, 0)

    slot = i % 2

    @pl.when(i == 0)
    def _():
        issue_tile(0, 0)

    @pl.when(i + 1 < n_tiles)
    def _():
        issue_tile(i + 1, 1 - slot)

    wait_tile(i, slot)

    gate = gate_ref[...]
    wa = gate[:, 0:1]
    wb = gate[:, 1:2]
    lo_a, hi_a = _unpack_halves(buf[slot, 0])
    lo_b, hi_b = _unpack_halves(buf[slot, 1])
    y = jnp.concatenate([wa * lo_a + wb * lo_b, wa * hi_a + wb * hi_b], axis=-1)
    o_ref[...] = x1_ref[...] + g2_ref[0] * y


def _combine(x1, seq, g2, gates, dest_flat, yb):
    n, d = x1.shape
    w = yb.shape[1]
    tt = min(128, seq)
    n_tiles = n // tt
    tiles_per_seq = seq // tt
    grid_spec = pltpu.PrefetchScalarGridSpec(
        num_scalar_prefetch=1,
        grid=(n_tiles,),
        in_specs=[pl.BlockSpec((tt, d), lambda i, dr: (i, 0)),
                  pl.BlockSpec((1, 1, d), lambda i, dr: (i // tiles_per_seq, 0, 0)),
                  pl.BlockSpec((tt, LANES), lambda i, dr: (i, 0)),
                  pl.BlockSpec(memory_space=pl.ANY)],
        out_specs=pl.BlockSpec((tt, d), lambda i, dr: (i, 0)),
        scratch_shapes=[pltpu.VMEM((2, TOP_K, tt, w), U32), pltpu.SemaphoreType.DMA((2,))],
    )
    return pl.pallas_call(
        functools.partial(_combine_kernel, tt, n_tiles),
        grid_spec=grid_spec,
        out_shape=jax.ShapeDtypeStruct((n, d), F32),
        compiler_params=_params("arbitrary"),
    )(dest_flat, x1, g2, gates, yb)


def _layer(x, c, w_ada, b_ada, norm1_w, w_in, forget_bias, q_norm_w, k_norm_w, ret_norm_w, w_out, norm2_w,
           w_coarse, b_coarse, w_fine, b_fine, w1, w3, w2):
    bsz, seq, d = x.shape
    n = bsz * seq
    d_fox = d // 2
    d_ret = d // 2
    n_heads = d_fox // HEAD_DIM
    d_main = 3 * d_fox + 4 * d_ret

    mod = _ada_modulation(c, w_ada, b_ada)
    sh1, sc1, g1, sh2, sc2, g2 = [m.reshape(bsz, 1, d) for m in jnp.split(mod, 6, axis=-1)]

    f0 = 3 * d_fox
    tn = 512
    w_all = jnp.concatenate(
        [w_in[:, :f0], w_in[:, f0 + n_heads:], w_in[:, f0:f0 + n_heads],
         jnp.zeros((d, tn - n_heads), w_in.dtype)], axis=1).astype(BF16)
    fb = jnp.zeros((1, LANES), F32).at[0, :n_heads].set(forget_bias)

    half = HEAD_DIM // 2
    theta = ROPE_BASE ** (-jnp.arange(half, dtype=F32) / half)
    ang = jnp.arange(seq, dtype=F32)[:, None] * theta[None, :]
    cos_t = jnp.concatenate([jnp.cos(ang), jnp.cos(ang)], axis=-1)
    sin_t = jnp.concatenate([-jnp.sin(ang), jnp.sin(ang)], axis=-1)

    x2d = x.reshape(n, d)
    z, log_f = _input_projection(x2d, seq, norm1_w.reshape(1, d), sc1, sh1, w_all, cos_t, sin_t,
                                 q_norm_w.reshape(1, HEAD_DIM), k_norm_w.reshape(1, HEAD_DIM), fb,
                                 d_fox, d_main)

    lf = log_f[:, :n_heads].reshape(bsz, seq, n_heads).transpose(0, 2, 1).reshape(bsz * n_heads, seq)
    cum = _cumsum_rows(lf)

    o_a = _fox_attention(z, cum, bsz, seq, n_heads)
    log_g = jnp.log(1.0 - 2.0 ** (-5.0 - jnp.arange(n_heads, dtype=F32)))
    o_b = _retention(z, log_g, ret_norm_w.reshape(1, d_ret), bsz, seq, n_heads, 3 * d_fox)

    w_router = jnp.zeros((d, LANES), F32)
    w_router = w_router.at[:, :N_GROUPS].set(w_coarse)
    w_router = w_router.at[:, N_GROUPS:N_GROUPS + N_EXPERTS].set(
        w_fine.transpose(1, 0, 2).reshape(d, N_EXPERTS))
    b_router = jnp.zeros((1, LANES), F32)
    b_router = b_router.at[0, :N_GROUPS].set(b_coarse)
    b_router = b_router.at[0, N_GROUPS:N_GROUPS + N_EXPERTS].set(b_fine.reshape(N_EXPERTS))

    x1, h_packed, logits = _output_projection(o_a, o_b, w_out.astype(BF16), x2d, seq, g1,
                                              norm2_w.reshape(1, d), sc2, sh2, w_router, b_router)

    gates, eids, ranks, counts = _route(logits)

    blk = 256
    nk = n * TOP_K
    n_blocks = nk // blk + N_EXPERTS
    cnt = counts[0, :N_EXPERTS]
    padded = (cnt + blk - 1) // blk * blk
    pend = jnp.cumsum(padded)
    pstart = pend - padded
    eid2 = eids[:, :TOP_K]
    dest = (pstart[eid2] + ranks[:, :TOP_K]).reshape(nk).astype(jnp.int32)
    block_start = jnp.arange(n_blocks, dtype=jnp.int32) * blk
    block_e = jnp.minimum(jnp.searchsorted(pend, block_start, side='right'), N_EXPERTS - 1).astype(jnp.int32)
    nvalid = jnp.clip(pstart[block_e] + cnt[block_e] - block_start, 0, blk).astype(jnp.int32)

    tail_rows = pend[-1] + jnp.arange(N_EXPERTS, dtype=jnp.int32) * blk
    fill_rows = jnp.concatenate([jnp.where(padded > 0, pend - blk, -1),
                                 jnp.where(tail_rows < n_blocks * blk, tail_rows, -1)]).astype(jnp.int32)

    xs = _dispatch(h_packed, dest, fill_rows, n_blocks * blk, blk)
    yb = _expert_blocks(xs, block_e, nvalid, w1, w3, w2, blk)
    out = _combine(x1, seq, g2, gates, dest, yb)
    return out.reshape(bsz, seq, d)


def kernel(x, c, w_ada, b_ada, norm1_w, w_in, forget_bias, q_norm_w, k_norm_w, ret_norm_w, w_out, norm2_w,
           w_coarse, b_coarse, w_fine, b_fine, w1, w3, w2):
    c_in = c
    for l in range(w_ada.shape[0]):
        x = _layer(x, c_in, w_ada[l], b_ada[l], norm1_w[l], w_in[l], forget_bias[l], q_norm_w[l],
                   k_norm_w[l], ret_norm_w[l], w_out[l], norm2_w[l], w_coarse[l], b_coarse[l],
                   w_fine[l], b_fine[l], w1[l], w3[l], w2[l])
    return x
```

```python
import functools

import jax
import jax.numpy as jnp
from jax import lax
from jax.experimental import pallas as pl
from jax.experimental.pallas import tpu as pltpu

HEAD_DIM = 128
N_GROUPS = 4
EXPERTS_PER_GROUP = 8
N_EXPERTS = N_GROUPS * EXPERTS_PER_GROUP
TOP_K = 2
ROPE_BASE = 10000.0
EPS = 1e-6

LANES = 128
VMEM_LIMIT = 56 * 1024 * 1024
NEG_BIG = -1e30
LOG2E = 1.4426950408889634

F32 = jnp.float32
BF16 = jnp.bfloat16
U32 = jnp.uint32


def _params(*sem):
    return pltpu.CompilerParams(dimension_semantics=sem, vmem_limit_bytes=VMEM_LIMIT)


def _silu(v):
    return v * (1.0 / (1.0 + jnp.exp(-v)))


def _pack_halves(y):
    w = y.shape[1] // 2
    lo = pltpu.bitcast(y[:, :w].astype(BF16).astype(F32), U32)
    hi = pltpu.bitcast(y[:, w:].astype(BF16).astype(F32), U32)
    return (hi & jnp.uint32(0xFFFF0000)) | (lo >> 16)


def _unpack_halves(p):
    lo = pltpu.bitcast(p << 16, F32)
    hi = pltpu.bitcast(p & jnp.uint32(0xFFFF0000), F32)
    return lo, hi


def _ada_kernel(ct_ref, w_ref, b_ref, o_ref):
    w = w_ref[...]
    rows = []
    for b in range(o_ref.shape[0]):
        if b < 2:
            cb = _silu(ct_ref[:, b:b + 1])
            rows.append(jnp.sum(cb * w, axis=0, keepdims=True) + b_ref[...])
        else:
            rows.append(jnp.zeros_like(b_ref[...]))
    o_ref[...] = jnp.concatenate(rows, axis=0)


def _ada_modulation(c, w_ada, b_ada):
    bsz, d = c.shape
    n = w_ada.shape[1]
    tn = 1024
    ct = jnp.zeros((d, LANES), F32).at[:, :bsz].set(c.T)
    out = pl.pallas_call(
        _ada_kernel,
        grid=(n // tn,),
        in_specs=[pl.BlockSpec((d, LANES), lambda j: (0, 0)),
                  pl.BlockSpec((d, tn), lambda j: (0, j)),
                  pl.BlockSpec((1, tn), lambda j: (0, j))],
        out_specs=pl.BlockSpec((8, tn), lambda j: (0, j)),
        out_shape=jax.ShapeDtypeStruct((8, n), F32),
        compiler_params=_params("arbitrary"),
    )(ct, w_ada, b_ada.reshape(1, n))
    return out[:bsz]


def _inproj_kernel(n_fox_tiles, n_main_tiles, x_ref, nw_ref, sc_ref, sh_ref, w_ref, cos_ref, sin_ref,
                   qw_ref, kw_ref, fb_ref, z_ref, f_ref, h_ref):
    j = pl.program_id(1)

    @pl.when(j == 0)
    def _():
        x = x_ref[...]
        ms = jnp.mean(x * x, axis=-1, keepdims=True)
        y = x * lax.rsqrt(ms + EPS) * nw_ref[...]
        h_ref[...] = (y * (1.0 + sc_ref[0]) + sh_ref[0]).astype(BF16)

    acc = jnp.dot(h_ref[...], w_ref[...], preferred_element_type=F32)
    tn = acc.shape[1]
    heads = tn // HEAD_DIM
    q_t = n_fox_tiles // 3
    r0 = n_fox_tiles
    r_t = (n_main_tiles - n_fox_tiles) // 4

    def head_norm(w_row):
        outs = []
        for hh in range(heads):
            a = acc[:, hh * HEAD_DIM:(hh + 1) * HEAD_DIM]
            ms = jnp.mean(a * a, axis=-1, keepdims=True)
            outs.append(a * lax.rsqrt(ms + EPS) * w_row)
        return jnp.concatenate(outs, axis=-1)

    def rotate(scale):
        outs = []
        cs = cos_ref[...] * scale
        sn = sin_ref[...] * scale
        for hh in range(heads):
            a = acc[:, hh * HEAD_DIM:(hh + 1) * HEAD_DIM]
            outs.append(a * cs + pltpu.roll(a, HEAD_DIM // 2, 1) * sn)
        return jnp.concatenate(outs, axis=-1)

    @pl.when(j < q_t)
    def _():
        z_ref[...] = head_norm(qw_ref[...] * (LOG2E * HEAD_DIM ** -0.5)).astype(BF16)

    @pl.when((j >= q_t) & (j < 2 * q_t))
    def _():
        z_ref[...] = head_norm(kw_ref[...]).astype(BF16)

    @pl.when((j >= r0) & (j < r0 + r_t))
    def _():
        z_ref[...] = rotate(1.0).astype(BF16)

    @pl.when((j >= r0 + r_t) & (j < r0 + 2 * r_t))
    def _():
        z_ref[...] = rotate(HEAD_DIM ** -0.5).astype(BF16)

    @pl.when(((j >= 2 * q_t) & (j < r0)) | ((j >= r0 + 2 * r_t) & (j < n_main_tiles)))
    def _():
        z_ref[...] = acc.astype(BF16)

    @pl.when(j == n_main_tiles)
    def _():
        t = acc[:, :LANES] + fb_ref[...]
        f_ref[...] = jnp.minimum(t, 0.0) - jnp.log(1.0 + jnp.exp(-jnp.abs(t)))


def _input_projection(x2d, seq, norm_w, sc1, sh1, w_all, cos_t, sin_t, qw, kw, fb, d_fox, d_main):
    n, d = x2d.shape
    tm, tn = 1024, 512
    tm = min(tm, seq)
    n_main_tiles = d_main // tn
    n_fox_tiles = 3 * d_fox // tn
    tiles_per_seq = seq // tm
    kern = functools.partial(_inproj_kernel, n_fox_tiles, n_main_tiles)
    bsel = lambda i, j: (i // tiles_per_seq, 0, 0)
    return pl.pallas_call(
        kern,
        grid=(n // tm, n_main_tiles + 1),
        in_specs=[pl.BlockSpec((tm, d), lambda i, j: (i, 0)),
                  pl.BlockSpec((1, d), lambda i, j: (0, 0)),
                  pl.BlockSpec((1, 1, d), bsel),
                  pl.BlockSpec((1, 1, d), bsel),
                  pl.BlockSpec((d, tn), lambda i, j: (0, j)),
                  pl.BlockSpec((tm, HEAD_DIM), lambda i, j: (i % tiles_per_seq, 0)),
                  pl.BlockSpec((tm, HEAD_DIM), lambda i, j: (i % tiles_per_seq, 0)),
                  pl.BlockSpec((1, HEAD_DIM), lambda i, j: (0, 0)),
                  pl.BlockSpec((1, HEAD_DIM), lambda i, j: (0, 0)),
                  pl.BlockSpec((1, LANES), lambda i, j: (0, 0))],
        out_specs=[pl.BlockSpec((tm, tn), lambda i, j: (i, jnp.minimum(j, n_main_tiles - 1))),
                   pl.BlockSpec((tm, LANES), lambda i, j: (i, 0))],
        out_shape=[jax.ShapeDtypeStruct((n, d_main), BF16),
                   jax.ShapeDtypeStruct((n, LANES), F32)],
        scratch_shapes=[pltpu.VMEM((tm, d), BF16)],
        compiler_params=_params("arbitrary", "arbitrary"),
    )(x2d, norm_w, sc1, sh1, w_all, cos_t, sin_t, qw, kw, fb)


def _cumsum_kernel(x_ref, o_ref):
    x = x_ref[0]
    r = x.shape[0]
    a = lax.broadcasted_iota(jnp.int32, (LANES, LANES), 0)
    b = lax.broadcasted_iota(jnp.int32, (LANES, LANES), 1)
    upper = (a <= b).astype(F32)
    within = jnp.dot(x, upper, precision=lax.Precision.HIGHEST, preferred_element_type=F32)
    tot = jnp.broadcast_to(within[:, LANES - 1:LANES], (r, LANES))
    ra = lax.broadcasted_iota(jnp.int32, (r, r), 0)
    rb = lax.broadcasted_iota(jnp.int32, (r, r), 1)
    strict = (rb < ra).astype(F32)
    before = jnp.dot(strict, tot, precision=lax.Precision.HIGHEST, preferred_element_type=F32)
    o_ref[0] = within + before


def _cumsum_rows(x):
    g, s = x.shape
    r = s // LANES
    out = pl.pallas_call(
        _cumsum_kernel,
        grid=(g,),
        in_specs=[pl.BlockSpec((1, r, LANES), lambda i: (i, 0, 0))],
        out_specs=pl.BlockSpec((1, r, LANES), lambda i: (i, 0, 0)),
        out_shape=jax.ShapeDtypeStruct((g, r, LANES), F32),
        compiler_params=_params("arbitrary"),
    )(x.reshape(g, r, LANES))
    return out.reshape(g, 1, s)


def _fox_kernel(tq, q_ref, k_ref, v_ref, cum_ref, o_ref, sa_ref, sb_ref, m_ref, l_ref, acc_ref):
    qi = pl.program_id(2)
    q_start = pl.multiple_of(qi * tq, tq)
    c0 = cum_ref[0, :, pl.ds(q_start, LANES)][:, 0:1]

    m_ref[...] = jnp.full(m_ref.shape, NEG_BIG, F32)
    l_ref[...] = jnp.zeros(l_ref.shape, F32)
    acc_ref[...] = jnp.zeros(acc_ref.shape, F32)
    n_slabs = tq // LANES

    def scores(kb, s_ref):
        start = pl.multiple_of(kb * tq, tq)
        k = k_ref[pl.ds(start, tq), :]
        bias = (c0 - cum_ref[0, :, pl.ds(start, tq)]) * LOG2E
        s_ref[...] = lax.dot_general(q_ref[...], k, (((1,), (1,)), ((), ())),
                                     preferred_element_type=F32) + bias

    def softmax_pv(kb, s_ref, masked):
        start = pl.multiple_of(kb * tq, tq)
        v = v_ref[pl.ds(start, tq), :]
        slabs = []
        for j in range(n_slabs):
            t = s_ref[:, j * LANES:(j + 1) * LANES]
            if masked:
                row = lax.broadcasted_iota(jnp.int32, t.shape, 0)
                col = lax.broadcasted_iota(jnp.int32, t.shape, 1) + j * LANES
                t = jnp.where(col <= row, t, NEG_BIG)
            slabs.append(t)
        mx = slabs[0]
        for t in slabs[1:]:
            mx = jnp.maximum(mx, t)
        m_prev = m_ref[...]
        m_new = jnp.maximum(m_prev, jnp.max(mx, axis=-1, keepdims=True))
        alpha = jnp.exp2(m_prev - m_new)
        probs = [jnp.exp2(t - m_new) for t in slabs]
        psum = probs[0]
        for t in probs[1:]:
            psum = psum + t
        l_ref[...] = alpha * l_ref[...] + psum
        p = jnp.concatenate([t.astype(BF16) for t in probs], axis=-1)
        acc_ref[...] = alpha * acc_ref[...] + jnp.dot(p, v, preferred_element_type=F32)
        m_ref[...] = m_new

    scores(0, sa_ref)

    def body(i, carry):
        kb = 2 * i
        scores(kb + 1, sb_ref)
        softmax_pv(kb, sa_ref, False)
        scores(kb + 2, sa_ref)
        softmax_pv(kb + 1, sb_ref, False)
        return carry

    lax.fori_loop(0, qi // 2, body, 0)

    @pl.when(qi % 2 == 0)
    def _():
        softmax_pv(qi, sa_ref, True)

    @pl.when(qi % 2 == 1)
    def _():
        scores(qi, sb_ref)
        softmax_pv(qi - 1, sa_ref, False)
        softmax_pv(qi, sb_ref, True)

    o_ref[...] = (acc_ref[...] / jnp.sum(l_ref[...], axis=-1, keepdims=True)).astype(BF16)


def _fox_attention(z, cum, bsz, seq, n_heads):
    tq = min(512, seq)
    nq = seq // tq
    kern = functools.partial(_fox_kernel, tq)
    return pl.pallas_call(
        kern,
        grid=(bsz, n_heads, nq),
        in_specs=[pl.BlockSpec((tq, HEAD_DIM), lambda b, h, i: (b * nq + i, h)),
                  pl.BlockSpec((seq, HEAD_DIM), lambda b, h, i: (b, n_heads + h)),
                  pl.BlockSpec((seq, HEAD_DIM), lambda b, h, i: (b, 2 * n_heads + h)),
                  pl.BlockSpec((1, 1, seq), lambda b, h, i: (b * n_heads + h, 0, 0))],
        out_specs=pl.BlockSpec((tq, HEAD_DIM), lambda b, h, i: (b * nq + i, h)),
        out_shape=jax.ShapeDtypeStruct((bsz * seq, n_heads * HEAD_DIM), BF16),
        scratch_shapes=[pltpu.VMEM((tq, tq), F32), pltpu.VMEM((tq, tq), F32),
                        pltpu.VMEM((tq, LANES), F32), pltpu.VMEM((tq, LANES), F32),
                        pltpu.VMEM((tq, HEAD_DIM), F32)],
        compiler_params=_params("arbitrary", "arbitrary", "arbitrary"),
    )(z, z, z, cum)


def _ret_kernel(chunk, n_chunks, lg_ref, q_ref, k_ref, v_ref, g_ref, nw_ref, o_ref, state_ref, decay_ref):
    h = pl.program_id(1)
    t = pl.program_id(2)
    log_g = lg_ref[h]

    @pl.when(t == 0)
    def _():
        state_ref[...] = jnp.zeros(state_ref.shape, F32)
        i = lax.broadcasted_iota(jnp.int32, (chunk, chunk), 0)
        jj = lax.broadcasted_iota(jnp.int32, (chunk, chunk), 1)
        diff = (i - jj).astype(F32)
        decay_ref[...] = jnp.where(diff >= 0, jnp.exp(log_g * jnp.maximum(diff, 0.0)), 0.0)

    pos = lax.broadcasted_iota(jnp.int32, (chunk, HEAD_DIM), 0).astype(F32)
    q_decay = jnp.exp(log_g * (pos + 1.0))
    k_decay = jnp.exp(log_g * (chunk - 1.0 - pos))
    chunk_decay = jnp.exp(jnp.full((1, HEAD_DIM), chunk, F32) * log_g)

    for c in range(n_chunks):
        rows = pl.ds(c * chunk, chunk)
        q = q_ref[rows, :]
        k = k_ref[rows, :]
        v = v_ref[rows, :]
        scores = lax.dot_general(q, k, (((1,), (1,)), ((), ())), preferred_element_type=F32)
        scores = scores * decay_ref[...]
        intra = jnp.dot(scores.astype(BF16), v, preferred_element_type=F32)
        state = state_ref[...]
        inter = jnp.dot(q, state.astype(BF16), preferred_element_type=F32) * q_decay
        kd = (k.astype(F32) * k_decay).astype(BF16)
        kv = lax.dot_general(kd, v, (((0,), (0,)), ((), ())), preferred_element_type=F32)
        state_ref[...] = state * chunk_decay + kv
        o = intra + inter
        ms = jnp.mean(o * o, axis=-1, keepdims=True)
        o = o * lax.rsqrt(ms + EPS) * nw_ref[...]
        o_ref[rows, :] = (o * _silu(g_ref[rows, :].astype(F32))).astype(BF16)


def _retention(z, log_g, norm_w, bsz, seq, n_heads, col0):
    chunk = min(256, seq)
    n_chunks = min(2, seq // chunk)
    tt = chunk * n_chunks
    nt = seq // tt
    c0 = col0 // HEAD_DIM
    kern = functools.partial(_ret_kernel, chunk, n_chunks)

    def sec(s):
        return pl.BlockSpec((tt, HEAD_DIM), lambda b, h, t, lg: (b * nt + t, c0 + s * n_heads + h))

    grid_spec = pltpu.PrefetchScalarGridSpec(
        num_scalar_prefetch=1,
        grid=(bsz, n_heads, nt),
        in_specs=[sec(0), sec(1), sec(2), sec(3),
                  pl.BlockSpec((1, HEAD_DIM), lambda b, h, t, lg: (0, h))],
        out_specs=pl.BlockSpec((tt, HEAD_DIM), lambda b, h, t, lg: (b * nt + t, h)),
        scratch_shapes=[pltpu.VMEM((HEAD_DIM, HEAD_DIM), F32), pltpu.VMEM((chunk, chunk), F32)],
    )
    return pl.pallas_call(
        kern,
        grid_spec=grid_spec,
        out_shape=jax.ShapeDtypeStruct((bsz * seq, n_heads * HEAD_DIM), BF16),
        compiler_params=_params("arbitrary", "arbitrary", "arbitrary"),
    )(log_g, z, z, z, z, norm_w)


def _outproj_kernel(oa_ref, ob_ref, wa_ref, wb_ref, x_ref, g1_ref, nw_ref, sc_ref, sh_ref, wr_ref, br_ref,
                    x1_ref, hp_ref, lg_ref):
    mix = jnp.dot(oa_ref[...], wa_ref[...], preferred_element_type=F32)
    mix = mix + jnp.dot(ob_ref[...], wb_ref[...], preferred_element_type=F32)
    x1 = x_ref[...] + g1_ref[0] * mix
    x1_ref[...] = x1
    ms = jnp.mean(x1 * x1, axis=-1, keepdims=True)
    h2 = x1 * lax.rsqrt(ms + EPS) * nw_ref[...] * (1.0 + sc_ref[0]) + sh_ref[0]
    hp_ref[...] = _pack_halves(h2)
    h_hi = h2.astype(BF16)
    h_lo = (h2 - h_hi.astype(F32)).astype(BF16)
    both = jnp.dot(h_hi, wr_ref[...], preferred_element_type=F32)
    cross = jnp.dot(h_lo, wr_ref[:, :LANES], preferred_element_type=F32)
    lg_ref[...] = both[:, :LANES] + both[:, LANES:] + cross + br_ref[...]


def _output_projection(o_a, o_b, w_out, x2d, seq, g1, norm_w, sc2, sh2, w_router, b_router):
    n, d = x2d.shape
    da = o_a.shape[1]
    tm = min(256, seq)
    tiles_per_seq = seq // tm
    bsel = lambda i: (i // tiles_per_seq, 0, 0)
    return pl.pallas_call(
        _outproj_kernel,
        grid=(n // tm,),
        in_specs=[pl.BlockSpec((tm, da), lambda i: (i, 0)),
                  pl.BlockSpec((tm, da), lambda i: (i, 0)),
                  pl.BlockSpec((da, d), lambda i: (0, 0)),
                  pl.BlockSpec((da, d), lambda i: (1, 0)),
                  pl.BlockSpec((tm, d), lambda i: (i, 0)),
                  pl.BlockSpec((1, 1, d), bsel),
                  pl.BlockSpec((1, d), lambda i: (0, 0)),
                  pl.BlockSpec((1, 1, d), bsel),
                  pl.BlockSpec((1, 1, d), bsel),
                  pl.BlockSpec((d, 2 * LANES), lambda i: (0, 0)),
                  pl.BlockSpec((1, LANES), lambda i: (0, 0))],
        out_specs=[pl.BlockSpec((tm, d), lambda i: (i, 0)),
                   pl.BlockSpec((tm, d // 2), lambda i: (i, 0)),
                   pl.BlockSpec((tm, LANES), lambda i: (i, 0))],
        out_shape=[jax.ShapeDtypeStruct((n, d), F32),
                   jax.ShapeDtypeStruct((n, d // 2), U32),
                   jax.ShapeDtypeStruct((n, LANES), F32)],
        compiler_params=_params("arbitrary"),
    )(o_a, o_b, w_out, w_out, x2d, g1, norm_w, sc2, sh2, w_router, b_router)


def _route_kernel(lg_ref, gate_ref, eid_ref, rank_ref, cnt_ref, run_ref):
    i = pl.program_id(0)

    @pl.when(i == 0)
    def _():
        run_ref[...] = jnp.zeros(run_ref.shape, F32)

    lg = lg_ref[...]
    tt = lg.shape[0]
    lane = lax.broadcasted_iota(jnp.int32, lg.shape, 1).astype(F32)
    big = 1e6

    def rmax(v):
        return jnp.max(v, axis=-1, keepdims=True)

    def rmin(v):
        return jnp.min(v, axis=-1, keepdims=True)

    def rsum(v):
        return jnp.sum(v, axis=-1, keepdims=True)

    cmask = lane < N_GROUPS
    cm = jnp.where(cmask, lg, NEG_BIG)
    ce = jnp.where(cmask, jnp.exp(cm - rmax(cm)), 0.0)
    pgrp = ce / rsum(ce)
    p_g = rmax(pgrp)
    g_sel = rmin(jnp.where(cmask & (pgrp == p_g), lane, big))

    lo = N_GROUPS + EXPERTS_PER_GROUP * g_sel
    fmask = (lane >= lo) & (lane < lo + EXPERTS_PER_GROUP)
    fm = jnp.where(fmask, lg, NEG_BIG)
    fe = jnp.where(fmask, jnp.exp(fm - rmax(fm)), 0.0)
    fp = fe / rsum(fe)
    fp = jnp.where(fmask, fp, -1.0)
    p1 = rmax(fp)
    i1 = rmin(jnp.where(fp == p1, lane, big))
    fp2 = jnp.where(lane == i1, -1.0, fp)
    p2 = rmax(fp2)
    i2 = rmin(jnp.where(fp2 == p2, lane, big))
    denom = p1 + p2
    w1 = p_g * p1 / denom
    w2 = p_g * p2 / denom
    e1 = i1 - N_GROUPS
    e2 = i2 - N_GROUPS

    gate_ref[...] = jnp.where(lane == 0, w1, jnp.where(lane == 1, w2, 0.0))
    eid_ref[...] = jnp.where(lane == 0, e1, jnp.where(lane == 1, e2, 0.0)).astype(jnp.int32)

    oh1 = (lane == e1).astype(F32)
    oh2 = (lane == e2).astype(F32)
    both = oh1 + oh2
    ra = lax.broadcasted_iota(jnp.int32, (tt, tt), 0)
    rb = lax.broadcasted_iota(jnp.int32, (tt, tt), 1)
    strict = (rb < ra).astype(BF16)
    prefix = jnp.dot(strict, both.astype(BF16), preferred_element_type=F32) + run_ref[...]
    r1 = rsum(prefix * oh1)
    r2 = rsum(prefix * oh2)
    rank_ref[...] = jnp.where(lane == 0, r1, jnp.where(lane == 1, r2, 0.0)).astype(jnp.int32)
    run_ref[...] = run_ref[...] + jnp.sum(both, axis=0, keepdims=True)
    cnt_ref[...] = run_ref[...].astype(jnp.int32)


def _route(logits):
    n = logits.shape[0]
    tt = min(512, n)
    blk = lambda: pl.BlockSpec((tt, LANES), lambda i: (i, 0))
    return pl.pallas_call(
        _route_kernel,
        grid=(n // tt,),
        in_specs=[blk()],
        out_specs=[blk(), blk(), blk(), pl.BlockSpec((1, LANES), lambda i: (0, 0))],
        out_shape=[jax.ShapeDtypeStruct((n, LANES), F32),
                   jax.ShapeDtypeStruct((n, LANES), jnp.int32),
                   jax.ShapeDtypeStruct((n, LANES), jnp.int32),
                   jax.ShapeDtypeStruct((1, LANES), jnp.int32)],
        scratch_shapes=[pltpu.VMEM((1, LANES), F32)],
        compiler_params=_params("arbitrary"),
    )(logits)


def _dispatch_kernel(tt, blk, n_fill, dest_ref, fill_ref, h_ref, xs_ref, zero_ref, sem, zsem):
    i = pl.program_id(0)
    base = i * (tt * TOP_K)

    @pl.when(i == 0)
    def _():
        zero_ref[...] = jnp.zeros(zero_ref.shape, U32)

        def zcopy(z):
            row = pl.multiple_of(jnp.maximum(fill_ref[z], 0), blk)
            return pltpu.make_async_copy(zero_ref, xs_ref.at[pl.ds(row, blk), :], zsem)

        def zissue(z, carry):
            @pl.when(fill_ref[z] >= 0)
            def _():
                zcopy(z).start()
            return carry

        def zdrain(z, carry):
            @pl.when(fill_ref[z] >= 0)
            def _():
                zcopy(z).wait()
            return carry

        lax.fori_loop(0, n_fill, zissue, 0)
        lax.fori_loop(0, n_fill, zdrain, 0)

    def copy(r, kk):
        d = dest_ref[base + r * TOP_K + kk]
        return pltpu.make_async_copy(h_ref.at[pl.ds(r, 1), :], xs_ref.at[pl.ds(d, 1), :], sem)

    def issue(r, carry):
        for kk in range(TOP_K):
            copy(r, kk).start()
        return carry

    lax.fori_loop(0, tt, issue, 0, unroll=8)
    for _ in range(TOP_K):
        pltpu.make_async_copy(h_ref, xs_ref.at[pl.ds(0, tt), :], sem).wait()


def _dispatch(h_packed, dest_flat, fill_rows, n_slots, blk):
    n, w = h_packed.shape
    tt = min(128, n)
    n_fill = fill_rows.shape[0]
    grid_spec = pltpu.PrefetchScalarGridSpec(
        num_scalar_prefetch=2,
        grid=(n // tt,),
        in_specs=[pl.BlockSpec((tt, w), lambda i, d, f: (i, 0))],
        out_specs=pl.BlockSpec(memory_space=pl.ANY),
        scratch_shapes=[pltpu.VMEM((blk, w), U32), pltpu.SemaphoreType.DMA(()), pltpu.SemaphoreType.DMA(())],
    )
    return pl.pallas_call(
        functools.partial(_dispatch_kernel, tt, blk, n_fill),
        grid_spec=grid_spec,
        out_shape=jax.ShapeDtypeStruct((n_slots, w), U32),
        compiler_params=_params("arbitrary"),
    )(dest_flat, fill_rows, h_packed)


def _expert_kernel(be_ref, nv_ref, xs_ref, w1_ref, w3_ref, w2_ref, y_ref, w1b, w3b, w2b):
    i = pl.program_id(0)
    prev = be_ref[jnp.maximum(i - 1, 0)]

    @pl.when((i == 0) | (be_ref[i] != prev))
    def _():
        w1b[...] = w1_ref[0].astype(BF16)
        w3b[...] = w3_ref[0].astype(BF16)
        w2b[...] = w2_ref[0].astype(BF16)

    nvalid = nv_ref[i]

    @pl.when(nvalid > 0)
    def _():
        lo, hi = _unpack_halves(xs_ref[...])
        lo = lo.astype(BF16)
        hi = hi.astype(BF16)
        half = lo.shape[1]
        a = jnp.dot(lo, w1b[:half, :], preferred_element_type=F32)
        a = a + jnp.dot(hi, w1b[half:, :], preferred_element_type=F32)
        b = jnp.dot(lo, w3b[:half, :], preferred_element_type=F32)
        b = b + jnp.dot(hi, w3b[half:, :], preferred_element_type=F32)
        mid = (_silu(a) * b).astype(BF16)
        y_ref[...] = _pack_halves(jnp.dot(mid, w2b[...], preferred_element_type=F32))

    @pl.when(nvalid <= 0)
    def _():
        y_ref[...] = jnp.zeros(y_ref.shape, U32)


def _expert_blocks(xs, block_e, nvalid, w1, w3, w2, blk):
    n_slots, w = xs.shape
    d = w1.shape[1]
    de = w1.shape[2]
    grid_spec = pltpu.PrefetchScalarGridSpec(
        num_scalar_prefetch=2,
        grid=(n_slots // blk,),
        in_specs=[pl.BlockSpec((blk, w), lambda i, be, nv: (i, 0)),
                  pl.BlockSpec((1, d, de), lambda i, be, nv: (be[i], 0, 0)),
                  pl.BlockSpec((1, d, de), lambda i, be, nv: (be[i], 0, 0)),
                  pl.BlockSpec((1, de, d), lambda i, be, nv: (be[i], 0, 0))],
        out_specs=pl.BlockSpec((blk, w), lambda i, be, nv: (i, 0)),
        scratch_shapes=[pltpu.VMEM((d, de), BF16), pltpu.VMEM((d, de), BF16), pltpu.VMEM((de, d), BF16)],
    )
    return pl.pallas_call(
        _expert_kernel,
        grid_spec=grid_spec,
        out_shape=jax.ShapeDtypeStruct((n_slots, w), U32),
        compiler_params=_params("arbitrary"),
    )(block_e, nvalid, xs, w1, w3, w2)


def _combine_kernel(tt, n_tiles, dest_ref, x1_ref, g2_ref, gate_ref, yb_ref, o_ref, buf, sems):
    i = pl.program_id(0)

    def copy(tile, slot, r, kk):
        d = dest_ref[(tile * tt + r) * TOP_K + kk]
        return pltpu.make_async_copy(yb_ref.at[pl.ds(d, 1), :], buf.at[slot, kk, pl.ds(r, 1), :], sems.at[slot])

    def issue_tile(tile, slot):
        def body(r, carry):
            for kk in range(TOP_K):
                copy(tile, slot, r, kk).start()
            return carry
        lax.fori_loop(0, tt, body, 0, unroll=8)

    def wait_tile(tile, slot):
        for kk in range(TOP_K):
            pltpu.make_async_copy(yb_ref.at[pl.ds(0, tt), :], buf.at[slot, kk], sems.at[slot]).wait()

    slot = i % 2

    @pl.when(i == 0)
    def _():
        issue_tile(0, 0)

    @pl.when(i + 1 < n_tiles)
    def _():
        issue_tile(i + 1, 1 - slot)

    wait_tile(i, slot)

    gate = gate_ref[...]
    wa = gate[:, 0:1]
    wb = gate[:, 1:2]
    lo_a, hi_a = _unpack_halves(buf[slot, 0])
    lo_b, hi_b = _unpack_halves(buf[slot, 1])
    y = jnp.concatenate([wa * lo_a + wb * lo_b, wa * hi_a + wb * hi_b], axis=-1)
    o_ref[...] = x1_ref[...] + g2_ref[0] * y


def _combine(x1, seq, g2, gates, dest_flat, yb):
    n, d = x1.shape
    w = yb.shape[1]
    tt = min(128, seq)
    n_tiles = n // tt
    tiles_per_seq = seq // tt
    grid_spec = pltpu.PrefetchScalarGridSpec(
        num_scalar_prefetch=1,
        grid=(n_tiles,),
        in_specs=[pl.BlockSpec((tt, d), lambda i, dr: (i, 0)),
                  pl.BlockSpec((1, 1, d), lambda i, dr: (i // tiles_per_seq, 0, 0)),
                  pl.BlockSpec((tt, LANES), lambda i, dr: (i, 0)),
                  pl.BlockSpec(memory_space=pl.ANY)],
        out_specs=pl.BlockSpec((tt, d), lambda i, dr: (i, 0)),
        scratch_shapes=[pltpu.VMEM((2, TOP_K, tt, w), U32), pltpu.SemaphoreType.DMA((2,))],
    )
    return pl.pallas_call(
        functools.partial(_combine_kernel, tt, n_tiles),
        grid_spec=grid_spec,
        out_shape=jax.ShapeDtypeStruct((n, d), F32),
        compiler_params=_params("arbitrary"),
    )(dest_flat, x1, g2, gates, yb)


def _layer(x, c, w_ada, b_ada, norm1_w, w_in, forget_bias, q_norm_w, k_norm_w, ret_norm_w, w_out, norm2_w,
           w_coarse, b_coarse, w_fine, b_fine, w1, w3, w2):
    bsz, seq, d = x.shape
    n = bsz * seq
    d_fox = d // 2
    d_ret = d // 2
    n_heads = d_fox // HEAD_DIM
    d_main = 3 * d_fox + 4 * d_ret

    mod = _ada_modulation(c, w_ada, b_ada)
    sh1, sc1, g1, sh2, sc2, g2 = [m.reshape(bsz, 1, d) for m in jnp.split(mod, 6, axis=-1)]

    f0 = 3 * d_fox
    tn = 512
    w_all = jnp.concatenate(
        [w_in[:, :f0], w_in[:, f0 + n_heads:], w_in[:, f0:f0 + n_heads],
         jnp.zeros((d, tn - n_heads), w_in.dtype)], axis=1).astype(BF16)
    fb = jnp.zeros((1, LANES), F32).at[0, :n_heads].set(forget_bias)

    half = HEAD_DIM // 2
    theta = ROPE_BASE ** (-jnp.arange(half, dtype=F32) / half)
    ang = jnp.arange(seq, dtype=F32)[:, None] * theta[None, :]
    cos_t = jnp.concatenate([jnp.cos(ang), jnp.cos(ang)], axis=-1)
    sin_t = jnp.concatenate([-jnp.sin(ang), jnp.sin(ang)], axis=-1)

    x2d = x.reshape(n, d)
    z, log_f = _input_projection(x2d, seq, norm1_w.reshape(1, d), sc1, sh1, w_all, cos_t, sin_t,
                                 q_norm_w.reshape(1, HEAD_DIM), k_norm_w.reshape(1, HEAD_DIM), fb,
                                 d_fox, d_main)

    lf = log_f[:, :n_heads].reshape(bsz, seq, n_heads).transpose(0, 2, 1).reshape(bsz * n_heads, seq)
    cum = _cumsum_rows(lf)

    o_a = _fox_attention(z, cum, bsz, seq, n_heads)
    log_g = jnp.log(1.0 - 2.0 ** (-5.0 - jnp.arange(n_heads, dtype=F32)))
    o_b = _retention(z, log_g, ret_norm_w.reshape(1, d_ret), bsz, seq, n_heads, 3 * d_fox)

    w_router = jnp.zeros((d, LANES), F32)
    w_router = w_router.at[:, :N_GROUPS].set(w_coarse)
    w_router = w_router.at[:, N_GROUPS:N_GROUPS + N_EXPERTS].set(
        w_fine.transpose(1, 0, 2).reshape(d, N_EXPERTS))
    b_router = jnp.zeros((1, LANES), F32)
    b_router = b_router.at[0, :N_GROUPS].set(b_coarse)
    b_router = b_router.at[0, N_GROUPS:N_GROUPS + N_EXPERTS].set(b_fine.reshape(N_EXPERTS))

    wr_hi = w_router.astype(BF16)
    wr_lo = (w_router - wr_hi.astype(F32)).astype(BF16)
    x1, h_packed, logits = _output_projection(o_a, o_b, w_out.astype(BF16), x2d, seq, g1,
                                              norm2_w.reshape(1, d), sc2, sh2,
                                              jnp.concatenate([wr_hi, wr_lo], axis=1), b_router)

    gates, eids, ranks, counts = _route(logits)

    blk = 256
    nk = n * TOP_K
    n_blocks = nk // blk + N_EXPERTS
    cnt = counts[0, :N_EXPERTS]
    padded = (cnt + blk - 1) // blk * blk
    pend = jnp.cumsum(padded)
    pstart = pend - padded
    eid2 = eids[:, :TOP_K]
    dest = (pstart[eid2] + ranks[:, :TOP_K]).reshape(nk).astype(jnp.int32)
    block_start = jnp.arange(n_blocks, dtype=jnp.int32) * blk
    block_e = jnp.minimum(jnp.sum(pend[None, :] <= block_start[:, None], axis=1), N_EXPERTS - 1).astype(jnp.int32)
    nvalid = jnp.clip(pstart[block_e] + cnt[block_e] - block_start, 0, blk).astype(jnp.int32)

    tail_rows = pend[-1] + jnp.arange(N_EXPERTS, dtype=jnp.int32) * blk
    fill_rows = jnp.concatenate([jnp.where(padded > 0, pend - blk, -1),
                                 jnp.where(tail_rows < n_blocks * blk, tail_rows, -1)]).astype(jnp.int32)

    xs = _dispatch(h_packed, dest, fill_rows, n_blocks * blk, blk)
    yb = _expert_blocks(xs, block_e, nvalid, w1, w3, w2, blk)
    out = _combine(x1, seq, g2, gates, dest, yb)
    return out.reshape(bsz, seq, d)


def kernel(x, c, w_ada, b_ada, norm1_w, w_in, forget_bias, q_norm_w, k_norm_w, ret_norm_w, w_out, norm2_w,
           w_coarse, b_coarse, w_fine, b_fine, w1, w3, w2):
    c_in = c
    for l in range(w_ada.shape[0]):
        x = _layer(x, c_in, w_ada[l], b_ada[l], norm1_w[l], w_in[l], forget_bias[l], q_norm_w[l],
                   k_norm_w[l], ret_norm_w[l], w_out[l], norm2_w[l], w_coarse[l], b_coarse[l],
                   w_fine[l], b_fine[l], w1[l], w3[l], w2[l])
    return x
```

```python
import functools

import jax
import jax.numpy as jnp
from jax import lax
from jax.experimental import pallas as pl
from jax.experimental.pallas import tpu as pltpu

HEAD_DIM = 128
N_GROUPS = 4
EXPERTS_PER_GROUP = 8
N_EXPERTS = N_GROUPS * EXPERTS_PER_GROUP
TOP_K = 2
ROPE_BASE = 10000.0
EPS = 1e-6

LANES = 128
VMEM_LIMIT = 56 * 1024 * 1024
NEG_BIG = -1e30
LOG2E = 1.4426950408889634

F32 = jnp.float32
BF16 = jnp.bfloat16
U32 = jnp.uint32


def _params(*sem):
    return pltpu.CompilerParams(dimension_semantics=sem, vmem_limit_bytes=VMEM_LIMIT)


def _silu(v):
    return v * (1.0 / (1.0 + jnp.exp(-v)))


def _pack_halves(y):
    w = y.shape[1] // 2
    lo = pltpu.bitcast(y[:, :w].astype(BF16).astype(F32), U32)
    hi = pltpu.bitcast(y[:, w:].astype(BF16).astype(F32), U32)
    return (hi & jnp.uint32(0xFFFF0000)) | (lo >> 16)


def _unpack_halves(p):
    lo = pltpu.bitcast(p << 16, F32)
    hi = pltpu.bitcast(p & jnp.uint32(0xFFFF0000), F32)
    return lo, hi


def _ada_kernel(ct_ref, w_ref, b_ref, o_ref):
    w = w_ref[...]
    rows = []
    for b in range(o_ref.shape[0]):
        if b < 2:
            cb = _silu(ct_ref[:, b:b + 1])
            rows.append(jnp.sum(cb * w, axis=0, keepdims=True) + b_ref[...])
        else:
            rows.append(jnp.zeros_like(b_ref[...]))
    o_ref[...] = jnp.concatenate(rows, axis=0)


def _ada_modulation(c, w_ada, b_ada):
    bsz, d = c.shape
    n = w_ada.shape[1]
    tn = 1024
    ct = jnp.zeros((d, LANES), F32).at[:, :bsz].set(c.T)
    out = pl.pallas_call(
        _ada_kernel,
        grid=(n // tn,),
        in_specs=[pl.BlockSpec((d, LANES), lambda j: (0, 0)),
                  pl.BlockSpec((d, tn), lambda j: (0, j)),
                  pl.BlockSpec((1, tn), lambda j: (0, j))],
        out_specs=pl.BlockSpec((8, tn), lambda j: (0, j)),
        out_shape=jax.ShapeDtypeStruct((8, n), F32),
        compiler_params=_params("arbitrary"),
    )(ct, w_ada, b_ada.reshape(1, n))
    return out[:bsz]


def _inproj_kernel(q_t, r_t, x_ref, nw_ref, sc_ref, sh_ref, wa_ref, wb_ref, wf_ref, cos_ref, sin_ref,
                   qw_ref, kw_ref, fb_ref, z_ref, f_ref, h_ref):
    j = pl.program_id(1)
    r0 = 3 * q_t

    @pl.when(j == 0)
    def _():
        x = x_ref[...]
        ms = jnp.mean(x * x, axis=-1, keepdims=True)
        y = x * lax.rsqrt(ms + EPS) * nw_ref[...]
        h = (y * (1.0 + sc_ref[0]) + sh_ref[0]).astype(BF16)
        h_ref[...] = h
        t = jnp.dot(h, wf_ref[...], preferred_element_type=F32) + fb_ref[...]
        f_ref[...] = jnp.minimum(t, 0.0) - jnp.log(1.0 + jnp.exp(-jnp.abs(t)))

    def heads_of(acc):
        return [acc[:, hh * HEAD_DIM:(hh + 1) * HEAD_DIM] for hh in range(acc.shape[1] // HEAD_DIM)]

    def head_norm(acc, w_row):
        outs = []
        for a in heads_of(acc):
            ms = jnp.mean(a * a, axis=-1, keepdims=True)
            outs.append(a * lax.rsqrt(ms + EPS) * w_row)
        return jnp.concatenate(outs, axis=-1).astype(BF16)

    def rotate(acc, scale):
        cs = cos_ref[...] * scale
        sn = sin_ref[...] * scale
        outs = [a * cs + pltpu.roll(a, HEAD_DIM // 2, 1) * sn for a in heads_of(acc)]
        return jnp.concatenate(outs, axis=-1).astype(BF16)

    def fox():
        return jnp.dot(h_ref[...], wa_ref[...], preferred_element_type=F32)

    def ret():
        return jnp.dot(h_ref[...], wb_ref[...], preferred_element_type=F32)

    @pl.when(j < q_t)
    def _():
        z_ref[...] = head_norm(fox(), qw_ref[...] * (LOG2E * HEAD_DIM ** -0.5))

    @pl.when((j >= q_t) & (j < 2 * q_t))
    def _():
        z_ref[...] = head_norm(fox(), kw_ref[...])

    @pl.when((j >= 2 * q_t) & (j < r0))
    def _():
        z_ref[...] = fox().astype(BF16)

    @pl.when((j >= r0) & (j < r0 + r_t))
    def _():
        z_ref[...] = rotate(ret(), 1.0)

    @pl.when((j >= r0 + r_t) & (j < r0 + 2 * r_t))
    def _():
        z_ref[...] = rotate(ret(), HEAD_DIM ** -0.5)

    @pl.when(j >= r0 + 2 * r_t)
    def _():
        z_ref[...] = ret().astype(BF16)


def _input_projection(x2d, seq, norm_w, sc1, sh1, w_fox, w_ret, w_f, cos_t, sin_t, qw, kw, fb):
    n, d = x2d.shape
    tm, tn = min(1024, seq), 512
    fox_tiles = w_fox.shape[1] // tn
    ret_tiles = w_ret.shape[1] // tn
    tiles_per_seq = seq // tm
    kern = functools.partial(_inproj_kernel, fox_tiles // 3, ret_tiles // 4)
    bsel = lambda i, j: (i // tiles_per_seq, 0, 0)
    const = lambda i, j: (0, 0)
    return pl.pallas_call(
        kern,
        grid=(n // tm, fox_tiles + ret_tiles),
        in_specs=[pl.BlockSpec((tm, d), lambda i, j: (i, 0)),
                  pl.BlockSpec((1, d), const),
                  pl.BlockSpec((1, 1, d), bsel),
                  pl.BlockSpec((1, 1, d), bsel),
                  pl.BlockSpec((d, tn), lambda i, j: (0, jnp.minimum(j, fox_tiles - 1))),
                  pl.BlockSpec((d, tn), lambda i, j: (0, jnp.maximum(j - fox_tiles, 0))),
                  pl.BlockSpec((d, LANES), const),
                  pl.BlockSpec((tm, HEAD_DIM), lambda i, j: (i % tiles_per_seq, 0)),
                  pl.BlockSpec((tm, HEAD_DIM), lambda i, j: (i % tiles_per_seq, 0)),
                  pl.BlockSpec((1, HEAD_DIM), const),
                  pl.BlockSpec((1, HEAD_DIM), const),
                  pl.BlockSpec((1, LANES), const)],
        out_specs=[pl.BlockSpec((tm, tn), lambda i, j: (i, j)),
                   pl.BlockSpec((tm, LANES), lambda i, j: (i, 0))],
        out_shape=[jax.ShapeDtypeStruct((n, w_fox.shape[1] + w_ret.shape[1]), BF16),
                   jax.ShapeDtypeStruct((n, LANES), F32)],
        scratch_shapes=[pltpu.VMEM((tm, d), BF16)],
        compiler_params=_params("arbitrary", "arbitrary"),
    )(x2d, norm_w, sc1, sh1, w_fox, w_ret, w_f, cos_t, sin_t, qw, kw, fb)


def _cumsum_kernel(x_ref, o_ref):
    x = x_ref[0]
    r = x.shape[0]
    a = lax.broadcasted_iota(jnp.int32, (LANES, LANES), 0)
    b = lax.broadcasted_iota(jnp.int32, (LANES, LANES), 1)
    upper = (a <= b).astype(F32)
    within = jnp.dot(x, upper, precision=lax.Precision.HIGHEST, preferred_element_type=F32)
    tot = jnp.broadcast_to(within[:, LANES - 1:LANES], (r, LANES))
    ra = lax.broadcasted_iota(jnp.int32, (r, r), 0)
    rb = lax.broadcasted_iota(jnp.int32, (r, r), 1)
    strict = (rb < ra).astype(F32)
    before = jnp.dot(strict, tot, precision=lax.Precision.HIGHEST, preferred_element_type=F32)
    o_ref[0] = within + before


def _cumsum_rows(x):
    g, s = x.shape
    r = s // LANES
    out = pl.pallas_call(
        _cumsum_kernel,
        grid=(g,),
        in_specs=[pl.BlockSpec((1, r, LANES), lambda i: (i, 0, 0))],
        out_specs=pl.BlockSpec((1, r, LANES), lambda i: (i, 0, 0)),
        out_shape=jax.ShapeDtypeStruct((g, r, LANES), F32),
        compiler_params=_params("arbitrary"),
    )(x.reshape(g, r, LANES))
    return out.reshape(g, 1, s)


def _fox_kernel(tq, q_ref, k_ref, v_ref, cum_ref, o_ref, sa_ref, sb_ref, m_ref, l_ref, acc_ref):
    qi = pl.program_id(2)
    q_start = pl.multiple_of(qi * tq, tq)
    c0 = cum_ref[0, :, pl.ds(q_start, LANES)][:, 0:1]

    m_ref[...] = jnp.full(m_ref.shape, NEG_BIG, F32)
    l_ref[...] = jnp.zeros(l_ref.shape, F32)
    acc_ref[...] = jnp.zeros(acc_ref.shape, F32)
    n_slabs = tq // LANES

    def scores(kb, s_ref):
        start = pl.multiple_of(kb * tq, tq)
        k = k_ref[pl.ds(start, tq), :]
        bias = (c0 - cum_ref[0, :, pl.ds(start, tq)]) * LOG2E
        s_ref[...] = lax.dot_general(q_ref[...], k, (((1,), (1,)), ((), ())),
                                     preferred_element_type=F32) + bias

    def softmax_pv(kb, s_ref, masked):
        start = pl.multiple_of(kb * tq, tq)
        v = v_ref[pl.ds(start, tq), :]
        slabs = []
        for j in range(n_slabs):
            t = s_ref[:, j * LANES:(j + 1) * LANES]
            if masked:
                row = lax.broadcasted_iota(jnp.int32, t.shape, 0)
                col = lax.broadcasted_iota(jnp.int32, t.shape, 1) + j * LANES
                t = jnp.where(col <= row, t, NEG_BIG)
            slabs.append(t)
        mx = slabs[0]
        for t in slabs[1:]:
            mx = jnp.maximum(mx, t)
        m_prev = m_ref[...]
        m_new = jnp.maximum(m_prev, jnp.max(mx, axis=-1, keepdims=True))
        alpha = jnp.exp2(m_prev - m_new)
        probs = [jnp.exp2(t - m_new) for t in slabs]
        psum = probs[0]
        for t in probs[1:]:
            psum = psum + t
        l_ref[...] = alpha * l_ref[...] + psum
        p = jnp.concatenate([t.astype(BF16) for t in probs], axis=-1)
        acc_ref[...] = alpha * acc_ref[...] + jnp.dot(p, v, preferred_element_type=F32)
        m_ref[...] = m_new

    scores(0, sa_ref)

    def pair(kb):
        scores(kb + 1, sb_ref)
        softmax_pv(kb, sa_ref, False)
        scores(kb + 2, sa_ref)
        softmax_pv(kb + 1, sb_ref, False)

    def body4(i, carry):
        pair(4 * i)
        pair(4 * i + 2)
        return carry

    def body2(i, carry):
        pair(2 * i)
        return carry

    n4 = qi // 4
    lax.fori_loop(0, n4, body4, 0)
    lax.fori_loop(2 * n4, qi // 2, body2, 0)

    @pl.when(qi % 2 == 0)
    def _():
        softmax_pv(qi, sa_ref, True)

    @pl.when(qi % 2 == 1)
    def _():
        scores(qi, sb_ref)
        softmax_pv(qi - 1, sa_ref, False)
        softmax_pv(qi, sb_ref, True)

    o_ref[...] = (acc_ref[...] / jnp.sum(l_ref[...], axis=-1, keepdims=True)).astype(BF16)


def _fox_attention(z, cum, bsz, seq, n_heads):
    tq = min(512, seq)
    nq = seq // tq
    kern = functools.partial(_fox_kernel, tq)
    return pl.pallas_call(
        kern,
        grid=(bsz, n_heads, nq),
        in_specs=[pl.BlockSpec((tq, HEAD_DIM), lambda b, h, i: (b * nq + i, h)),
                  pl.BlockSpec((seq, HEAD_DIM), lambda b, h, i: (b, n_heads + h)),
                  pl.BlockSpec((seq, HEAD_DIM), lambda b, h, i: (b, 2 * n_heads + h)),
                  pl.BlockSpec((1, 1, seq), lambda b, h, i: (b * n_heads + h, 0, 0))],
        out_specs=pl.BlockSpec((tq, HEAD_DIM), lambda b, h, i: (b * nq + i, h)),
        out_shape=jax.ShapeDtypeStruct((bsz * seq, n_heads * HEAD_DIM), BF16),
        scratch_shapes=[pltpu.VMEM((tq, tq), F32), pltpu.VMEM((tq, tq), F32),
                        pltpu.VMEM((tq, LANES), F32), pltpu.VMEM((tq, LANES), F32),
                        pltpu.VMEM((tq, HEAD_DIM), F32)],
        compiler_params=_params("arbitrary", "arbitrary", "arbitrary"),
    )(z, z, z, cum)


def _ret_kernel(chunk, n_heads, lg_ref, q_ref, k_ref, v_ref, g_ref, nw_ref, o_ref, state_ref, decay_ref):
    first = (pl.program_id(0) == 0) & (pl.program_id(1) == 0)

    @pl.when(first)
    def _():
        i = lax.broadcasted_iota(jnp.int32, (chunk, chunk), 0)
        jj = lax.broadcasted_iota(jnp.int32, (chunk, chunk), 1)
        diff = (i - jj).astype(F32)
        for h in range(n_heads):
            decay_ref[h] = jnp.where(diff >= 0, jnp.exp(lg_ref[h] * jnp.maximum(diff, 0.0)), 0.0)

    @pl.when(pl.program_id(1) == 0)
    def _():
        state_ref[...] = jnp.zeros(state_ref.shape, F32)

    pos = lax.broadcasted_iota(jnp.int32, (chunk, HEAD_DIM), 0).astype(F32)
    for h in range(n_heads):
        log_g = lg_ref[h]
        cols = slice(h * HEAD_DIM, (h + 1) * HEAD_DIM)
        q = q_ref[:, cols]
        k = k_ref[:, cols]
        v = v_ref[:, cols]
        scores = lax.dot_general(q, k, (((1,), (1,)), ((), ())), preferred_element_type=F32)
        scores = scores * decay_ref[h]
        intra = jnp.dot(scores.astype(BF16), v, preferred_element_type=F32)
        state = state_ref[h]
        inter = jnp.dot(q, state.astype(BF16), preferred_element_type=F32) * jnp.exp(log_g * (pos + 1.0))
        kd = (k.astype(F32) * jnp.exp(log_g * (chunk - 1.0 - pos))).astype(BF16)
        kv = lax.dot_general(kd, v, (((0,), (0,)), ((), ())), preferred_element_type=F32)
        state_ref[h] = state * jnp.exp(jnp.full((1, HEAD_DIM), chunk, F32) * log_g) + kv
        o = intra + inter
        ms = jnp.mean(o * o, axis=-1, keepdims=True)
        o = o * lax.rsqrt(ms + EPS) * nw_ref[:, cols]
        o_ref[:, cols] = (o * _silu(g_ref[:, cols].astype(F32))).astype(BF16)


def _retention(z, log_g, norm_w, bsz, seq, n_heads, col0):
    chunk = min(256, seq)
    nt = seq // chunk
    width = n_heads * HEAD_DIM
    c0 = col0 // width
    kern = functools.partial(_ret_kernel, chunk, n_heads)

    def sec(s):
        return pl.BlockSpec((chunk, width), lambda b, t, lg: (b * nt + t, c0 + s))

    grid_spec = pltpu.PrefetchScalarGridSpec(
        num_scalar_prefetch=1,
        grid=(bsz, nt),
        in_specs=[sec(0), sec(1), sec(2), sec(3), pl.BlockSpec((1, width), lambda b, t, lg: (0, 0))],
        out_specs=pl.BlockSpec((chunk, width), lambda b, t, lg: (b * nt + t, 0)),
        scratch_shapes=[pltpu.VMEM((n_heads, HEAD_DIM, HEAD_DIM), F32),
                        pltpu.VMEM((n_heads, chunk, chunk), F32)],
    )
    return pl.pallas_call(
        kern,
        grid_spec=grid_spec,
        out_shape=jax.ShapeDtypeStruct((bsz * seq, width), BF16),
        compiler_params=_params("arbitrary", "arbitrary"),
    )(log_g, z, z, z, z, norm_w)


def _outproj_kernel(oa_ref, ob_ref, wa_ref, wb_ref, x_ref, g1_ref, nw_ref, sc_ref, sh_ref, wr_ref, br_ref,
                    x1_ref, hp_ref, lg_ref):
    mix = jnp.dot(oa_ref[...], wa_ref[...], preferred_element_type=F32)
    mix = mix + jnp.dot(ob_ref[...], wb_ref[...], preferred_element_type=F32)
    x1 = x_ref[...] + g1_ref[0] * mix
    x1_ref[...] = x1
    ms = jnp.mean(x1 * x1, axis=-1, keepdims=True)
    h2 = x1 * lax.rsqrt(ms + EPS) * nw_ref[...] * (1.0 + sc_ref[0]) + sh_ref[0]
    hp_ref[...] = _pack_halves(h2)
    h_hi = h2.astype(BF16)
    h_lo = (h2 - h_hi.astype(F32)).astype(BF16)
    both = jnp.dot(h_hi, wr_ref[...], preferred_element_type=F32)
    cross = jnp.dot(h_lo, wr_ref[:, :LANES], preferred_element_type=F32)
    lg_ref[...] = both[:, :LANES] + both[:, LANES:] + cross + br_ref[...]


def _output_projection(o_a, o_b, w_out, x2d, seq, g1, norm_w, sc2, sh2, w_router, b_router):
    n, d = x2d.shape
    da = o_a.shape[1]
    tm = min(256, seq)
    tiles_per_seq = seq // tm
    bsel = lambda i: (i // tiles_per_seq, 0, 0)
    return pl.pallas_call(
        _outproj_kernel,
        grid=(n // tm,),
        in_specs=[pl.BlockSpec((tm, da), lambda i: (i, 0)),
                  pl.BlockSpec((tm, da), lambda i: (i, 0)),
                  pl.BlockSpec((da, d), lambda i: (0, 0)),
                  pl.BlockSpec((da, d), lambda i: (1, 0)),
                  pl.BlockSpec((tm, d), lambda i: (i, 0)),
                  pl.BlockSpec((1, 1, d), bsel),
                  pl.BlockSpec((1, d), lambda i: (0, 0)),
                  pl.BlockSpec((1, 1, d), bsel),
                  pl.BlockSpec((1, 1, d), bsel),
                  pl.BlockSpec((d, 2 * LANES), lambda i: (0, 0)),
                  pl.BlockSpec((1, LANES), lambda i: (0, 0))],
        out_specs=[pl.BlockSpec((tm, d), lambda i: (i, 0)),
                   pl.BlockSpec((tm, d // 2), lambda i: (i, 0)),
                   pl.BlockSpec((tm, LANES), lambda i: (i, 0))],
        out_shape=[jax.ShapeDtypeStruct((n, d), F32),
                   jax.ShapeDtypeStruct((n, d // 2), U32),
                   jax.ShapeDtypeStruct((n, LANES), F32)],
        compiler_params=_params("arbitrary"),
    )(o_a, o_b, w_out, w_out, x2d, g1, norm_w, sc2, sh2, w_router, b_router)


def _route_kernel(blk, n_blocks, lg_ref, gate_ref, ids_ref, plan_ref, blocks_ref, run_ref):
    i = pl.program_id(0)

    @pl.when(i == 0)
    def _():
        run_ref[...] = jnp.zeros(run_ref.shape, F32)

    lg = lg_ref[...]
    tt = lg.shape[0]
    lane = lax.broadcasted_iota(jnp.int32, lg.shape, 1).astype(F32)
    big = 1e6

    def rmax(v):
        return jnp.max(v, axis=-1, keepdims=True)

    def rmin(v):
        return jnp.min(v, axis=-1, keepdims=True)

    def rsum(v):
        return jnp.sum(v, axis=-1, keepdims=True)

    cmask = lane < N_GROUPS
    cm = jnp.where(cmask, lg, NEG_BIG)
    ce = jnp.where(cmask, jnp.exp(cm - rmax(cm)), 0.0)
    pgrp = ce / rsum(ce)
    p_g = rmax(pgrp)
    g_sel = rmin(jnp.where(cmask & (pgrp == p_g), lane, big))

    lo = N_GROUPS + EXPERTS_PER_GROUP * g_sel
    fmask = (lane >= lo) & (lane < lo + EXPERTS_PER_GROUP)
    fm = jnp.where(fmask, lg, NEG_BIG)
    fe = jnp.where(fmask, jnp.exp(fm - rmax(fm)), 0.0)
    fp = fe / rsum(fe)
    fp = jnp.where(fmask, fp, -1.0)
    p1 = rmax(fp)
    i1 = rmin(jnp.where(fp == p1, lane, big))
    fp2 = jnp.where(lane == i1, -1.0, fp)
    p2 = rmax(fp2)
    i2 = rmin(jnp.where(fp2 == p2, lane, big))
    denom = p1 + p2
    w1 = p_g * p1 / denom
    w2 = p_g * p2 / denom
    e1 = i1 - N_GROUPS
    e2 = i2 - N_GROUPS

    gate_ref[...] = jnp.where(lane == 0, w1, jnp.where(lane == 1, w2, 0.0))

    oh1 = (lane == e1).astype(F32)
    oh2 = (lane == e2).astype(F32)
    both = oh1 + oh2
    ra = lax.broadcasted_iota(jnp.int32, (tt, tt), 0)
    rb = lax.broadcasted_iota(jnp.int32, (tt, tt), 1)
    strict = (rb < ra).astype(BF16)
    prefix = jnp.dot(strict, both.astype(BF16), preferred_element_type=F32) + run_ref[...]
    r1 = rsum(prefix * oh1)
    r2 = rsum(prefix * oh2)
    run_ref[...] = run_ref[...] + jnp.sum(both, axis=0, keepdims=True)

    packed = jnp.where(lane == 0, e1, jnp.where(lane == 1, e2, jnp.where(lane == 2, r1,
                                                                        jnp.where(lane == 3, r2, 0.0))))
    ids_ref[...] = jnp.transpose(packed)[:8, :].astype(jnp.int32)

    @pl.when(i == pl.num_programs(0) - 1)
    def _():
        cnt = jnp.broadcast_to(run_ref[...], (8, LANES))
        lane8 = lax.broadcasted_iota(jnp.int32, (8, LANES), 1)
        padded = jnp.floor((cnt + (blk - 1.0)) * (1.0 / blk)) * blk
        pend = padded
        for sh in (1, 2, 4, 8, 16, 32, 64):
            pend = pend + jnp.where(lane8 >= sh, pltpu.roll(pend, sh, 1), 0.0)
        pstart = pend - padded
        total = jnp.max(pend, axis=-1, keepdims=True)
        tail = total + (lane8 - N_EXPERTS).astype(F32) * blk
        fill = jnp.where(lane8 < N_EXPERTS, jnp.where(padded > 0, pend - blk, -1.0),
                         jnp.where((lane8 < 2 * N_EXPERTS) & (tail < n_blocks * blk), tail, -1.0))
        row8 = lax.broadcasted_iota(jnp.int32, (8, LANES), 0)
        plan_ref[...] = jnp.where(row8 == 0, pstart, jnp.where(row8 == 1, fill, 0.0)).astype(jnp.int32)

        nrows = blocks_ref.shape[0]
        brow = lax.broadcasted_iota(jnp.int32, (nrows, LANES), 0).astype(F32) * blk
        blane = lax.broadcasted_iota(jnp.int32, (nrows, LANES), 1)
        pend_m = jnp.where(blane < N_EXPERTS, pend[0:1, :], 1e9)
        be = jnp.minimum(rsum((pend_m <= brow).astype(F32)), N_EXPERTS - 1.0)
        hit = blane.astype(F32) == be
        endv = rsum(jnp.where(hit, pstart[0:1, :] + cnt[0:1, :], 0.0))
        nvalid = jnp.clip(endv - brow[:, 0:1], 0.0, float(blk))
        blocks_ref[...] = jnp.where(blane == 0, be, jnp.where(blane == 1, nvalid, 0.0)).astype(jnp.int32)


def _route(logits, blk, n_blocks):
    n = logits.shape[0]
    tt = min(512, n)
    nrows = -(-n_blocks // 8) * 8
    blkspec = lambda: pl.BlockSpec((tt, LANES), lambda i: (i, 0))
    return pl.pallas_call(
        functools.partial(_route_kernel, blk, n_blocks),
        grid=(n // tt,),
        in_specs=[blkspec()],
        out_specs=[blkspec(),
                   pl.BlockSpec((8, tt), lambda i: (0, i)),
                   pl.BlockSpec((8, LANES), lambda i: (0, 0)),
                   pl.BlockSpec((nrows, LANES), lambda i: (0, 0))],
        out_shape=[jax.ShapeDtypeStruct((n, LANES), F32),
                   jax.ShapeDtypeStruct((8, n), jnp.int32),
                   jax.ShapeDtypeStruct((8, LANES), jnp.int32),
                   jax.ShapeDtypeStruct((nrows, LANES), jnp.int32)],
        scratch_shapes=[pltpu.VMEM((1, LANES), F32)],
        compiler_params=_params("arbitrary"),
    )(logits)


def _dispatch_kernel(tt, blk, n_fill, dest_ref, fill_ref, h_ref, xs_ref, zero_ref, sem, zsem):
    i = pl.program_id(0)
    base = i * (tt * TOP_K)

    @pl.when(i == 0)
    def _():
        zero_ref[...] = jnp.zeros(zero_ref.shape, U32)

        def zcopy(z):
            row = pl.multiple_of(jnp.maximum(fill_ref[z], 0), blk)
            return pltpu.make_async_copy(zero_ref, xs_ref.at[pl.ds(row, blk), :], zsem)

        def zissue(z, carry):
            @pl.when(fill_ref[z] >= 0)
            def _():
                zcopy(z).start()
            return carry

        def zdrain(z, carry):
            @pl.when(fill_ref[z] >= 0)
            def _():
                zcopy(z).wait()
            return carry

        lax.fori_loop(0, n_fill, zissue, 0)
        lax.fori_loop(0, n_fill, zdrain, 0)

    def copy(r, kk):
        d = dest_ref[base + r * TOP_K + kk]
        return pltpu.make_async_copy(h_ref.at[pl.ds(r, 1), :], xs_ref.at[pl.ds(d, 1), :], sem)

    def issue(r, carry):
        for kk in range(TOP_K):
            copy(r, kk).start()
        return carry

    lax.fori_loop(0, tt, issue, 0, unroll=8)
    for _ in range(TOP_K):
        pltpu.make_async_copy(h_ref, xs_ref.at[pl.ds(0, tt), :], sem).wait()


def _dispatch(h_packed, dest_flat, fill_rows, n_slots, blk):
    n, w = h_packed.shape
    tt = min(128, n)
    n_fill = fill_rows.shape[0]
    grid_spec = pltpu.PrefetchScalarGridSpec(
        num_scalar_prefetch=2,
        grid=(n // tt,),
        in_specs=[pl.BlockSpec((tt, w), lambda i, d, f: (i, 0))],
        out_specs=pl.BlockSpec(memory_space=pl.ANY),
        scratch_shapes=[pltpu.VMEM((blk, w), U32), pltpu.SemaphoreType.DMA(()), pltpu.SemaphoreType.DMA(())],
    )
    return pl.pallas_call(
        functools.partial(_dispatch_kernel, tt, blk, n_fill),
        grid_spec=grid_spec,
        out_shape=jax.ShapeDtypeStruct((n_slots, w), U32),
        compiler_params=_params("arbitrary"),
    )(dest_flat, fill_rows, h_packed)


def _expert_kernel(be_ref, nv_ref, xs_ref, w1_ref, w3_ref, w2_ref, y_ref, w1b, w3b, w2b):
    i = pl.program_id(0)
    prev = be_ref[jnp.maximum(i - 1, 0)]

    @pl.when((i == 0) | (be_ref[i] != prev))
    def _():
        w1b[...] = w1_ref[0].astype(BF16)
        w3b[...] = w3_ref[0].astype(BF16)
        w2b[...] = w2_ref[0].astype(BF16)

    nvalid = nv_ref[i]

    @pl.when(nvalid > 0)
    def _():
        lo, hi = _unpack_halves(xs_ref[...])
        lo = lo.astype(BF16)
        hi = hi.astype(BF16)
        half = lo.shape[1]
        a = jnp.dot(lo, w1b[:half, :], preferred_element_type=F32)
        a = a + jnp.dot(hi, w1b[half:, :], preferred_element_type=F32)
        b = jnp.dot(lo, w3b[:half, :], preferred_element_type=F32)
        b = b + jnp.dot(hi, w3b[half:, :], preferred_element_type=F32)
        mid = (_silu(a) * b).astype(BF16)
        y_ref[...] = _pack_halves(jnp.dot(mid, w2b[...], preferred_element_type=F32))

    @pl.when(nvalid <= 0)
    def _():
        y_ref[...] = jnp.zeros(y_ref.shape, U32)


def _expert_blocks(xs, block_e, nvalid, w1, w3, w2, blk):
    n_slots, w = xs.shape
    d = w1.shape[1]
    de = w1.shape[2]
    grid_spec = pltpu.PrefetchScalarGridSpec(
        num_scalar_prefetch=2,
        grid=(n_slots // blk,),
        in_specs=[pl.BlockSpec((blk, w), lambda i, be, nv: (i, 0)),
                  pl.BlockSpec((1, d, de), lambda i, be, nv: (be[i], 0, 0)),
                  pl.BlockSpec((1, d, de), lambda i, be, nv: (be[i], 0, 0)),
                  pl.BlockSpec((1, de, d), lambda i, be, nv: (be[i], 0, 0))],
        out_specs=pl.BlockSpec((blk, w), lambda i, be, nv: (i, 0)),
        scratch_shapes=[pltpu.VMEM((d, de), BF16), pltpu.VMEM((d, de), BF16), pltpu.VMEM((de, d), BF16)],
    )
    return pl.pallas_call(
        _expert_kernel,
        grid_spec=grid_spec,
        out_shape=jax.ShapeDtypeStruct((n_slots, w), U32),
        compiler_params=_params("arbitrary"),
    )(block_e, nvalid, xs, w1, w3, w2)


def _combine_kernel(tt, n_tiles, dest_ref, x1_ref, g2_ref, gate_ref, yb_ref, o_ref, buf, sems):
    i = pl.program_id(0)

    def copy(tile, slot, r, kk):
        d = dest_ref[(tile * tt + r) * TOP_K + kk]
        return pltpu.make_async_copy(yb_ref.at[pl.ds(d, 1), :], buf.at[slot, kk, pl.ds(r, 1), :], sems.at[slot])

    def issue_tile(tile, slot):
        def body(r, carry):
            for kk in range(TOP_K):
                copy(tile, slot, r, kk).start()
            return carry
        lax.fori_loop(0, tt, body, 0, unroll=8)

    def wait_tile(tile, slot):
        for kk in range(TOP_K):
            pltpu.make_async_copy(yb_ref.at[pl.ds(0, tt), :], buf.at[slot, kk], sems.at[slot]).wait()

    slot = i % 2

    @pl.when(i == 0)
    def _():
        issue_tile(0, 0)

    @pl.when(i + 1 < n_tiles)
    def _():
        issue_tile(i + 1, 1 - slot)

    wait_tile(i, slot)

    gate = gate_ref[...]
    wa = gate[:, 0:1]
    wb = gate[:, 1:2]
    lo_a, hi_a = _unpack_halves(buf[slot, 0])
    lo_b, hi_b = _unpack_halves(buf[slot, 1])
    y = jnp.concatenate([wa * lo_a + wb * lo_b, wa * hi_a + wb * hi_b], axis=-1)
    o_ref[...] = x1_ref[...] + g2_ref[0] * y


def _combine(x1, seq, g2, gates, dest_flat, yb):
    n, d = x1.shape
    w = yb.shape[1]
    tt = min(128, seq)
    n_tiles = n // tt
    tiles_per_seq = seq // tt
    grid_spec = pltpu.PrefetchScalarGridSpec(
        num_scalar_prefetch=1,
        grid=(n_tiles,),
        in_specs=[pl.BlockSpec((tt, d), lambda i, dr: (i, 0)),
                  pl.BlockSpec((1, 1, d), lambda i, dr: (i // tiles_per_seq, 0, 0)),
                  pl.BlockSpec((tt, LANES), lambda i, dr: (i, 0)),
                  pl.BlockSpec(memory_space=pl.ANY)],
        out_specs=pl.BlockSpec((tt, d), lambda i, dr: (i, 0)),
        scratch_shapes=[pltpu.VMEM((2, TOP_K, tt, w), U32), pltpu.SemaphoreType.DMA((2,))],
    )
    return pl.pallas_call(
        functools.partial(_combine_kernel, tt, n_tiles),
        grid_spec=grid_spec,
        out_shape=jax.ShapeDtypeStruct((n, d), F32),
        compiler_params=_params("arbitrary"),
    )(dest_flat, x1, g2, gates, yb)


def _layer(x, c, w_ada, b_ada, norm1_w, w_in, forget_bias, q_norm_w, k_norm_w, ret_norm_w, w_out, norm2_w,
           w_coarse, b_coarse, w_fine, b_fine, w1, w3, w2):
    bsz, seq, d = x.shape
    n = bsz * seq
    d_fox = d // 2
    d_ret = d // 2
    n_heads = d_fox // HEAD_DIM

    mod = _ada_modulation(c, w_ada, b_ada)
    sh1, sc1, g1, sh2, sc2, g2 = [m.reshape(bsz, 1, d) for m in jnp.split(mod, 6, axis=-1)]

    f0 = 3 * d_fox
    w_fox = w_in[:, :f0].astype(BF16)
    w_ret = w_in[:, f0 + n_heads:].astype(BF16)
    w_f = jnp.zeros((d, LANES), BF16).at[:, :n_heads].set(w_in[:, f0:f0 + n_heads].astype(BF16))
    fb = jnp.zeros((1, LANES), F32).at[0, :n_heads].set(forget_bias)

    half = HEAD_DIM // 2
    theta = ROPE_BASE ** (-jnp.arange(half, dtype=F32) / half)
    ang = jnp.arange(seq, dtype=F32)[:, None] * theta[None, :]
    cos_t = jnp.concatenate([jnp.cos(ang), jnp.cos(ang)], axis=-1)
    sin_t = jnp.concatenate([-jnp.sin(ang), jnp.sin(ang)], axis=-1)

    x2d = x.reshape(n, d)
    z, log_f = _input_projection(x2d, seq, norm1_w.reshape(1, d), sc1, sh1, w_fox, w_ret, w_f, cos_t, sin_t,
                                 q_norm_w.reshape(1, HEAD_DIM), k_norm_w.reshape(1, HEAD_DIM), fb)

    lf = log_f[:, :n_heads].reshape(bsz, seq, n_heads).transpose(0, 2, 1).reshape(bsz * n_heads, seq)
    cum = _cumsum_rows(lf)

    o_a = _fox_attention(z, cum, bsz, seq, n_heads)
    log_g = jnp.log(1.0 - 2.0 ** (-5.0 - jnp.arange(n_heads, dtype=F32)))
    o_b = _retention(z, log_g, ret_norm_w.reshape(1, d_ret), bsz, seq, n_heads, 3 * d_fox)

    w_router = jnp.zeros((d, LANES), F32)
    w_router = w_router.at[:, :N_GROUPS].set(w_coarse)
    w_router = w_router.at[:, N_GROUPS:N_GROUPS + N_EXPERTS].set(
        w_fine.transpose(1, 0, 2).reshape(d, N_EXPERTS))
    b_router = jnp.zeros((1, LANES), F32)
    b_router = b_router.at[0, :N_GROUPS].set(b_coarse)
    b_router = b_router.at[0, N_GROUPS:N_GROUPS + N_EXPERTS].set(b_fine.reshape(N_EXPERTS))

    wr_hi = w_router.astype(BF16)
    wr_lo = (w_router - wr_hi.astype(F32)).astype(BF16)
    x1, h_packed, logits = _output_projection(o_a, o_b, w_out.astype(BF16), x2d, seq, g1,
                                              norm2_w.reshape(1, d), sc2, sh2,
                                              jnp.concatenate([wr_hi, wr_lo], axis=1), b_router)

    blk = 256
    nk = n * TOP_K
    n_blocks = nk // blk + N_EXPERTS
    gates, ids, plan, blocks = _route(logits, blk, n_blocks)
    pstart = plan[0, :N_EXPERTS]
    fill_rows = plan[1, :2 * N_EXPERTS]
    dest = (pstart[ids[0:TOP_K]] + ids[TOP_K:2 * TOP_K]).T.reshape(nk)
    block_e = blocks[:n_blocks, 0]
    nvalid = blocks[:n_blocks, 1]

    xs = _dispatch(h_packed, dest, fill_rows, n_blocks * blk, blk)
    yb = _expert_blocks(xs, block_e, nvalid, w1, w3, w2, blk)
    out = _combine(x1, seq, g2, gates, dest, yb)
    return out.reshape(bsz, seq, d)


def kernel(x, c, w_ada, b_ada, norm1_w, w_in, forget_bias, q_norm_w, k_norm_w, ret_norm_w, w_out, norm2_w,
           w_coarse, b_coarse, w_fine, b_fine, w1, w3, w2):
    c_in = c
    for l in range(w_ada.shape[0]):
        x = _layer(x, c_in, w_ada[l], b_ada[l], norm1_w[l], w_in[l], forget_bias[l], q_norm_w[l],
                   k_norm_w[l], ret_norm_w[l], w_out[l], norm2_w[l], w_coarse[l], b_coarse[l],
                   w_fine[l], b_fine[l], w1[l], w3[l], w2[l])
    return x
```

```python
import functools

import jax
import jax.numpy as jnp
import numpy as np
from jax import lax
from jax.experimental import pallas as pl
from jax.experimental.pallas import tpu as pltpu

HEAD_DIM = 128
N_GROUPS = 4
EXPERTS_PER_GROUP = 8
N_EXPERTS = N_GROUPS * EXPERTS_PER_GROUP
TOP_K = 2
ROPE_BASE = 10000.0
EPS = 1e-6

LANES = 128
VMEM_LIMIT = 56 * 1024 * 1024
NEG_BIG = -1e30
LOG2E = 1.4426950408889634

F32 = jnp.float32
BF16 = jnp.bfloat16
U32 = jnp.uint32


def _params(*sem):
    return pltpu.CompilerParams(dimension_semantics=sem, vmem_limit_bytes=VMEM_LIMIT)


def _silu(v):
    return v * (1.0 / (1.0 + jnp.exp(-v)))


def _pack_halves(y):
    w = y.shape[1] // 2
    lo = pltpu.bitcast(y[:, :w].astype(BF16).astype(F32), U32)
    hi = pltpu.bitcast(y[:, w:].astype(BF16).astype(F32), U32)
    return (hi & jnp.uint32(0xFFFF0000)) | (lo >> 16)


def _unpack_halves(p):
    lo = pltpu.bitcast(p << 16, F32)
    hi = pltpu.bitcast(p & jnp.uint32(0xFFFF0000), F32)
    return lo, hi


def _ada_kernel(ct_ref, w_ref, b_ref, o_ref):
    w = w_ref[...]
    rows = []
    for b in range(o_ref.shape[0]):
        if b < 2:
            cb = _silu(ct_ref[:, b:b + 1])
            rows.append(jnp.sum(cb * w, axis=0, keepdims=True) + b_ref[...])
        else:
            rows.append(jnp.zeros_like(b_ref[...]))
    o_ref[...] = jnp.concatenate(rows, axis=0)


def _ada_modulation(c, w_ada, b_ada):
    bsz, d = c.shape
    n = w_ada.shape[1]
    tn = 1024
    ct = jnp.zeros((d, LANES), F32).at[:, :bsz].set(c.T)
    out = pl.pallas_call(
        _ada_kernel,
        grid=(n // tn,),
        in_specs=[pl.BlockSpec((d, LANES), lambda j: (0, 0)),
                  pl.BlockSpec((d, tn), lambda j: (0, j)),
                  pl.BlockSpec((1, tn), lambda j: (0, j))],
        out_specs=pl.BlockSpec((8, tn), lambda j: (0, j)),
        out_shape=jax.ShapeDtypeStruct((8, n), F32),
        compiler_params=_params("arbitrary"),
    )(ct, w_ada, b_ada.reshape(1, n))
    return out[:bsz]


def _inproj_kernel(q_t, r_t, x_ref, nw_ref, sc_ref, sh_ref, wa_ref, wb_ref, wf_ref, cos_ref, sin_ref,
                   qw_ref, kw_ref, fb_ref, z_ref, f_ref, h_ref):
    j = pl.program_id(1)
    r0 = 3 * q_t

    @pl.when(j == 0)
    def _():
        x = x_ref[...]
        ms = jnp.mean(x * x, axis=-1, keepdims=True)
        y = x * lax.rsqrt(ms + EPS) * nw_ref[...]
        h = (y * (1.0 + sc_ref[0]) + sh_ref[0]).astype(BF16)
        h_ref[...] = h
        t = jnp.dot(h, wf_ref[...], preferred_element_type=F32) + fb_ref[...]
        f_ref[...] = jnp.minimum(t, 0.0) - jnp.log(1.0 + jnp.exp(-jnp.abs(t)))

    def heads_of(acc):
        return [acc[:, hh * HEAD_DIM:(hh + 1) * HEAD_DIM] for hh in range(acc.shape[1] // HEAD_DIM)]

    def head_norm(acc, w_row):
        outs = []
        for a in heads_of(acc):
            ms = jnp.mean(a * a, axis=-1, keepdims=True)
            outs.append(a * lax.rsqrt(ms + EPS) * w_row)
        return jnp.concatenate(outs, axis=-1).astype(BF16)

    def rotate(acc, scale):
        cs = cos_ref[...] * scale
        sn = sin_ref[...] * scale
        outs = [a * cs + pltpu.roll(a, HEAD_DIM // 2, 1) * sn for a in heads_of(acc)]
        return jnp.concatenate(outs, axis=-1).astype(BF16)

    def fox():
        return jnp.dot(h_ref[...], wa_ref[...], preferred_element_type=F32)

    def ret():
        return jnp.dot(h_ref[...], wb_ref[...], preferred_element_type=F32)

    @pl.when(j < q_t)
    def _():
        z_ref[...] = head_norm(fox(), qw_ref[...] * (LOG2E * HEAD_DIM ** -0.5))

    @pl.when((j >= q_t) & (j < 2 * q_t))
    def _():
        z_ref[...] = head_norm(fox(), kw_ref[...])

    @pl.when((j >= 2 * q_t) & (j < r0))
    def _():
        z_ref[...] = fox().astype(BF16)

    @pl.when((j >= r0) & (j < r0 + r_t))
    def _():
        z_ref[...] = rotate(ret(), 1.0)

    @pl.when((j >= r0 + r_t) & (j < r0 + 2 * r_t))
    def _():
        z_ref[...] = rotate(ret(), HEAD_DIM ** -0.5)

    @pl.when(j >= r0 + 2 * r_t)
    def _():
        z_ref[...] = ret().astype(BF16)


def _input_projection(x2d, seq, norm_w, sc1, sh1, w_fox, w_ret, w_f, cos_t, sin_t, qw, kw, fb):
    n, d = x2d.shape
    tm, tn = min(1024, seq), 512
    fox_tiles = w_fox.shape[1] // tn
    ret_tiles = w_ret.shape[1] // tn
    tiles_per_seq = seq // tm
    kern = functools.partial(_inproj_kernel, fox_tiles // 3, ret_tiles // 4)
    bsel = lambda i, j: (i // tiles_per_seq, 0, 0)
    const = lambda i, j: (0, 0)
    return pl.pallas_call(
        kern,
        grid=(n // tm, fox_tiles + ret_tiles),
        in_specs=[pl.BlockSpec((tm, d), lambda i, j: (i, 0)),
                  pl.BlockSpec((1, d), const),
                  pl.BlockSpec((1, 1, d), bsel),
                  pl.BlockSpec((1, 1, d), bsel),
                  pl.BlockSpec((d, tn), lambda i, j: (0, jnp.minimum(j, fox_tiles - 1))),
                  pl.BlockSpec((d, tn), lambda i, j: (0, jnp.maximum(j - fox_tiles, 0))),
                  pl.BlockSpec((d, LANES), const),
                  pl.BlockSpec((tm, HEAD_DIM), lambda i, j: (i % tiles_per_seq, 0)),
                  pl.BlockSpec((tm, HEAD_DIM), lambda i, j: (i % tiles_per_seq, 0)),
                  pl.BlockSpec((1, HEAD_DIM), const),
                  pl.BlockSpec((1, HEAD_DIM), const),
                  pl.BlockSpec((1, LANES), const)],
        out_specs=[pl.BlockSpec((tm, tn), lambda i, j: (i, j)),
                   pl.BlockSpec((tm, LANES), lambda i, j: (i, 0))],
        out_shape=[jax.ShapeDtypeStruct((n, w_fox.shape[1] + w_ret.shape[1]), BF16),
                   jax.ShapeDtypeStruct((n, LANES), F32)],
        scratch_shapes=[pltpu.VMEM((tm, d), BF16)],
        compiler_params=_params("arbitrary", "arbitrary"),
    )(x2d, norm_w, sc1, sh1, w_fox, w_ret, w_f, cos_t, sin_t, qw, kw, fb)


def _cumsum_kernel(x_ref, o_ref):
    x = x_ref[0]
    r = x.shape[0]
    a = lax.broadcasted_iota(jnp.int32, (LANES, LANES), 0)
    b = lax.broadcasted_iota(jnp.int32, (LANES, LANES), 1)
    upper = (a <= b).astype(F32)
    within = jnp.dot(x, upper, precision=lax.Precision.HIGHEST, preferred_element_type=F32)
    tot = jnp.broadcast_to(within[:, LANES - 1:LANES], (r, LANES))
    ra = lax.broadcasted_iota(jnp.int32, (r, r), 0)
    rb = lax.broadcasted_iota(jnp.int32, (r, r), 1)
    strict = (rb < ra).astype(F32)
    before = jnp.dot(strict, tot, precision=lax.Precision.HIGHEST, preferred_element_type=F32)
    o_ref[0] = within + before


def _cumsum_rows(x):
    g, s = x.shape
    r = s // LANES
    out = pl.pallas_call(
        _cumsum_kernel,
        grid=(g,),
        in_specs=[pl.BlockSpec((1, r, LANES), lambda i: (i, 0, 0))],
        out_specs=pl.BlockSpec((1, r, LANES), lambda i: (i, 0, 0)),
        out_shape=jax.ShapeDtypeStruct((g, r, LANES), F32),
        compiler_params=_params("arbitrary"),
    )(x.reshape(g, r, LANES))
    return out.reshape(g, 1, s)


def _fox_kernel(tq, q_ref, k_ref, v_ref, cum_ref, o_ref, sa_ref, sb_ref, m_ref, l_ref, acc_ref):
    qi = pl.program_id(2)
    q_start = pl.multiple_of(qi * tq, tq)
    c0 = cum_ref[0, :, pl.ds(q_start, LANES)][:, 0:1]

    m_ref[...] = jnp.full(m_ref.shape, NEG_BIG, F32)
    l_ref[...] = jnp.zeros(l_ref.shape, F32)
    acc_ref[...] = jnp.zeros(acc_ref.shape, F32)
    n_slabs = tq // LANES

    def scores(kb, s_ref):
        start = pl.multiple_of(kb * tq, tq)
        k = k_ref[pl.ds(start, tq), :]
        bias = (c0 - cum_ref[0, :, pl.ds(start, tq)]) * LOG2E
        s_ref[...] = lax.dot_general(q_ref[...], k, (((1,), (1,)), ((), ())),
                                     preferred_element_type=F32) + bias

    def softmax_pv(kb, s_ref, masked):
        start = pl.multiple_of(kb * tq, tq)
        v = v_ref[pl.ds(start, tq), :]
        slabs = []
        for j in range(n_slabs):
            t = s_ref[:, j * LANES:(j + 1) * LANES]
            if masked:
                row = lax.broadcasted_iota(jnp.int32, t.shape, 0)
                col = lax.broadcasted_iota(jnp.int32, t.shape, 1) + j * LANES
                t = jnp.where(col <= row, t, NEG_BIG)
            slabs.append(t)
        mx = slabs[0]
        for t in slabs[1:]:
            mx = jnp.maximum(mx, t)
        m_prev = m_ref[...]
        m_new = jnp.maximum(m_prev, jnp.max(mx, axis=-1, keepdims=True))
        alpha = jnp.exp2(m_prev - m_new)
        probs = [jnp.exp2(t - m_new) for t in slabs]
        psum = probs[0]
        for t in probs[1:]:
            psum = psum + t
        l_ref[...] = alpha * l_ref[...] + psum
        p = jnp.concatenate([t.astype(BF16) for t in probs], axis=-1)
        acc_ref[...] = alpha * acc_ref[...] + jnp.dot(p, v, preferred_element_type=F32)
        m_ref[...] = m_new

    scores(0, sa_ref)

    def pair(kb):
        scores(kb + 1, sb_ref)
        softmax_pv(kb, sa_ref, False)
        scores(kb + 2, sa_ref)
        softmax_pv(kb + 1, sb_ref, False)

    def body4(i, carry):
        pair(4 * i)
        pair(4 * i + 2)
        return carry

    def body2(i, carry):
        pair(2 * i)
        return carry

    n4 = qi // 4
    lax.fori_loop(0, n4, body4, 0)
    lax.fori_loop(2 * n4, qi // 2, body2, 0)

    @pl.when(qi % 2 == 0)
    def _():
        softmax_pv(qi, sa_ref, True)

    @pl.when(qi % 2 == 1)
    def _():
        scores(qi, sb_ref)
        softmax_pv(qi - 1, sa_ref, False)
        softmax_pv(qi, sb_ref, True)

    o_ref[...] = (acc_ref[...] / jnp.sum(l_ref[...], axis=-1, keepdims=True)).astype(BF16)


def _fox_attention(z, cum, bsz, seq, n_heads):
    tq = min(512, seq)
    nq = seq // tq
    kern = functools.partial(_fox_kernel, tq)
    return pl.pallas_call(
        kern,
        grid=(bsz, n_heads, nq),
        in_specs=[pl.BlockSpec((tq, HEAD_DIM), lambda b, h, i: (b * nq + i, h)),
                  pl.BlockSpec((seq, HEAD_DIM), lambda b, h, i: (b, n_heads + h)),
                  pl.BlockSpec((seq, HEAD_DIM), lambda b, h, i: (b, 2 * n_heads + h)),
                  pl.BlockSpec((1, 1, seq), lambda b, h, i: (b * n_heads + h, 0, 0))],
        out_specs=pl.BlockSpec((tq, HEAD_DIM), lambda b, h, i: (b * nq + i, h)),
        out_shape=jax.ShapeDtypeStruct((bsz * seq, n_heads * HEAD_DIM), BF16),
        scratch_shapes=[pltpu.VMEM((tq, tq), F32), pltpu.VMEM((tq, tq), F32),
                        pltpu.VMEM((tq, LANES), F32), pltpu.VMEM((tq, LANES), F32),
                        pltpu.VMEM((tq, HEAD_DIM), F32)],
        compiler_params=_params("arbitrary", "arbitrary", "arbitrary"),
    )(z, z, z, cum)


def _ret_kernel(chunk, n_heads, lg_ref, q_ref, k_ref, v_ref, g_ref, nw_ref, o_ref, state_ref, decay_ref):
    first = (pl.program_id(0) == 0) & (pl.program_id(1) == 0)

    @pl.when(first)
    def _():
        i = lax.broadcasted_iota(jnp.int32, (chunk, chunk), 0)
        jj = lax.broadcasted_iota(jnp.int32, (chunk, chunk), 1)
        diff = (i - jj).astype(F32)
        for h in range(n_heads):
            decay_ref[h] = jnp.where(diff >= 0, jnp.exp(lg_ref[h] * jnp.maximum(diff, 0.0)), 0.0)

    @pl.when(pl.program_id(1) == 0)
    def _():
        state_ref[...] = jnp.zeros(state_ref.shape, F32)

    pos = lax.broadcasted_iota(jnp.int32, (chunk, HEAD_DIM), 0).astype(F32)
    for h in range(n_heads):
        log_g = lg_ref[h]
        cols = slice(h * HEAD_DIM, (h + 1) * HEAD_DIM)
        q = q_ref[:, cols]
        k = k_ref[:, cols]
        v = v_ref[:, cols]
        scores = lax.dot_general(q, k, (((1,), (1,)), ((), ())), preferred_element_type=F32)
        scores = scores * decay_ref[h]
        intra = jnp.dot(scores.astype(BF16), v, preferred_element_type=F32)
        state = state_ref[h]
        inter = jnp.dot(q, state.astype(BF16), preferred_element_type=F32) * jnp.exp(log_g * (pos + 1.0))
        kd = (k.astype(F32) * jnp.exp(log_g * (chunk - 1.0 - pos))).astype(BF16)
        kv = lax.dot_general(kd, v, (((0,), (0,)), ((), ())), preferred_element_type=F32)
        state_ref[h] = state * jnp.exp(jnp.full((1, HEAD_DIM), chunk, F32) * log_g) + kv
        o = intra + inter
        ms = jnp.mean(o * o, axis=-1, keepdims=True)
        o = o * lax.rsqrt(ms + EPS) * nw_ref[:, cols]
        o_ref[:, cols] = (o * _silu(g_ref[:, cols].astype(F32))).astype(BF16)


def _retention(z, log_g, norm_w, bsz, seq, n_heads, col0):
    chunk = min(256, seq)
    nt = seq // chunk
    width = n_heads * HEAD_DIM
    c0 = col0 // width
    kern = functools.partial(_ret_kernel, chunk, n_heads)

    def sec(s):
        return pl.BlockSpec((chunk, width), lambda b, t, lg: (b * nt + t, c0 + s))

    grid_spec = pltpu.PrefetchScalarGridSpec(
        num_scalar_prefetch=1,
        grid=(bsz, nt),
        in_specs=[sec(0), sec(1), sec(2), sec(3), pl.BlockSpec((1, width), lambda b, t, lg: (0, 0))],
        out_specs=pl.BlockSpec((chunk, width), lambda b, t, lg: (b * nt + t, 0)),
        scratch_shapes=[pltpu.VMEM((n_heads, HEAD_DIM, HEAD_DIM), F32),
                        pltpu.VMEM((n_heads, chunk, chunk), F32)],
    )
    return pl.pallas_call(
        kern,
        grid_spec=grid_spec,
        out_shape=jax.ShapeDtypeStruct((bsz * seq, width), BF16),
        compiler_params=_params("arbitrary", "arbitrary"),
    )(log_g, z, z, z, z, norm_w)


def _outproj_kernel(oa_ref, ob_ref, wa_ref, wb_ref, x_ref, g1_ref, nw_ref, sc_ref, sh_ref, wr_ref, br_ref,
                    x1_ref, hp_ref, lg_ref):
    mix = jnp.dot(oa_ref[...], wa_ref[...], preferred_element_type=F32)
    mix = mix + jnp.dot(ob_ref[...], wb_ref[...], preferred_element_type=F32)
    x1 = x_ref[...] + g1_ref[0] * mix
    x1_ref[...] = x1
    ms = jnp.mean(x1 * x1, axis=-1, keepdims=True)
    h2 = x1 * lax.rsqrt(ms + EPS) * nw_ref[...] * (1.0 + sc_ref[0]) + sh_ref[0]
    hp_ref[...] = _pack_halves(h2)
    h_hi = h2.astype(BF16)
    h_lo = (h2 - h_hi.astype(F32)).astype(BF16)
    both = jnp.dot(h_hi, wr_ref[...], preferred_element_type=F32)
    cross = jnp.dot(h_lo, wr_ref[:, :LANES], preferred_element_type=F32)
    lg_ref[...] = both[:, :LANES] + both[:, LANES:] + cross + br_ref[...]


def _output_projection(o_a, o_b, w_out, x2d, seq, g1, norm_w, sc2, sh2, w_router, b_router):
    n, d = x2d.shape
    da = o_a.shape[1]
    tm = min(256, seq)
    tiles_per_seq = seq // tm
    bsel = lambda i: (i // tiles_per_seq, 0, 0)
    return pl.pallas_call(
        _outproj_kernel,
        grid=(n // tm,),
        in_specs=[pl.BlockSpec((tm, da), lambda i: (i, 0)),
                  pl.BlockSpec((tm, da), lambda i: (i, 0)),
                  pl.BlockSpec((da, d), lambda i: (0, 0)),
                  pl.BlockSpec((da, d), lambda i: (1, 0)),
                  pl.BlockSpec((tm, d), lambda i: (i, 0)),
                  pl.BlockSpec((1, 1, d), bsel),
                  pl.BlockSpec((1, d), lambda i: (0, 0)),
                  pl.BlockSpec((1, 1, d), bsel),
                  pl.BlockSpec((1, 1, d), bsel),
                  pl.BlockSpec((d, 2 * LANES), lambda i: (0, 0)),
                  pl.BlockSpec((1, LANES), lambda i: (0, 0))],
        out_specs=[pl.BlockSpec((tm, d), lambda i: (i, 0)),
                   pl.BlockSpec((tm, d // 2), lambda i: (i, 0)),
                   pl.BlockSpec((tm, LANES), lambda i: (i, 0))],
        out_shape=[jax.ShapeDtypeStruct((n, d), F32),
                   jax.ShapeDtypeStruct((n, d // 2), U32),
                   jax.ShapeDtypeStruct((n, LANES), F32)],
        compiler_params=_params("arbitrary"),
    )(o_a, o_b, w_out, w_out, x2d, g1, norm_w, sc2, sh2, w_router, b_router)


def _route_kernel(blk, n_blocks, lg_ref, gate_ref, ids_ref, plan_ref, run_ref):
    i = pl.program_id(0)

    @pl.when(i == 0)
    def _():
        run_ref[...] = jnp.zeros(run_ref.shape, F32)

    lg = lg_ref[...]
    tt = lg.shape[0]
    lane = lax.broadcasted_iota(jnp.int32, lg.shape, 1).astype(F32)
    big = 1e6

    def rmax(v):
        return jnp.max(v, axis=-1, keepdims=True)

    def rmin(v):
        return jnp.min(v, axis=-1, keepdims=True)

    def rsum(v):
        return jnp.sum(v, axis=-1, keepdims=True)

    cmask = lane < N_GROUPS
    cm = jnp.where(cmask, lg, NEG_BIG)
    ce = jnp.where(cmask, jnp.exp(cm - rmax(cm)), 0.0)
    pgrp = ce / rsum(ce)
    p_g = rmax(pgrp)
    g_sel = rmin(jnp.where(cmask & (pgrp == p_g), lane, big))

    lo = N_GROUPS + EXPERTS_PER_GROUP * g_sel
    fmask = (lane >= lo) & (lane < lo + EXPERTS_PER_GROUP)
    fm = jnp.where(fmask, lg, NEG_BIG)
    fe = jnp.where(fmask, jnp.exp(fm - rmax(fm)), 0.0)
    fp = fe / rsum(fe)
    fp = jnp.where(fmask, fp, -1.0)
    p1 = rmax(fp)
    i1 = rmin(jnp.where(fp == p1, lane, big))
    fp2 = jnp.where(lane == i1, -1.0, fp)
    p2 = rmax(fp2)
    i2 = rmin(jnp.where(fp2 == p2, lane, big))
    denom = p1 + p2
    w1 = p_g * p1 / denom
    w2 = p_g * p2 / denom
    e1 = i1 - N_GROUPS
    e2 = i2 - N_GROUPS

    gate_ref[...] = jnp.where(lane == 0, w1, jnp.where(lane == 1, w2, 0.0))

    oh1 = (lane == e1).astype(F32)
    oh2 = (lane == e2).astype(F32)
    both = oh1 + oh2
    ra = lax.broadcasted_iota(jnp.int32, (tt, tt), 0)
    rb = lax.broadcasted_iota(jnp.int32, (tt, tt), 1)
    strict = (rb < ra).astype(BF16)
    prefix = jnp.dot(strict, both.astype(BF16), preferred_element_type=F32) + run_ref[...]
    r1 = rsum(prefix * oh1)
    r2 = rsum(prefix * oh2)
    run_ref[...] = run_ref[...] + jnp.sum(both, axis=0, keepdims=True)

    packed = jnp.where(lane == 0, e1, jnp.where(lane == 1, e2, jnp.where(lane == 2, r1,
                                                                        jnp.where(lane == 3, r2, 0.0))))
    ids_ref[...] = jnp.transpose(packed)[:8, :].astype(jnp.int32)

    @pl.when(i == pl.num_programs(0) - 1)
    def _():
        cnt = jnp.broadcast_to(run_ref[...], (8, LANES))
        lane8 = lax.broadcasted_iota(jnp.int32, (8, LANES), 1)
        padded = jnp.floor((cnt + (blk - 1.0)) * (1.0 / blk)) * blk
        pend = padded
        for sh in (1, 2, 4, 8, 16, 32, 64):
            pend = pend + jnp.where(lane8 >= sh, pltpu.roll(pend, sh, 1), 0.0)
        pstart = pend - padded
        total = jnp.max(pend, axis=-1, keepdims=True)
        tail = total + (lane8 - N_EXPERTS).astype(F32) * blk
        fill = jnp.where(lane8 < N_EXPERTS, jnp.where(padded > 0, pend - blk, -1.0),
                         jnp.where((lane8 < 2 * N_EXPERTS) & (tail < n_blocks * blk), tail, -1.0))
        row8 = lax.broadcasted_iota(jnp.int32, (8, LANES), 0)
        plan_ref[...] = jnp.where(row8 == 0, pstart, jnp.where(row8 == 1, fill,
                                                               jnp.where(row8 == 2, cnt, 0.0))).astype(jnp.int32)


def _route(logits, blk, n_blocks):
    n = logits.shape[0]
    tt = min(512, n)
    blkspec = lambda: pl.BlockSpec((tt, LANES), lambda i: (i, 0))
    return pl.pallas_call(
        functools.partial(_route_kernel, blk, n_blocks),
        grid=(n // tt,),
        in_specs=[blkspec()],
        out_specs=[blkspec(),
                   pl.BlockSpec((8, tt), lambda i: (0, i)),
                   pl.BlockSpec((8, LANES), lambda i: (0, 0))],
        out_shape=[jax.ShapeDtypeStruct((n, LANES), F32),
                   jax.ShapeDtypeStruct((8, n), jnp.int32),
                   jax.ShapeDtypeStruct((8, LANES), jnp.int32)],
        scratch_shapes=[pltpu.VMEM((1, LANES), F32)],
        compiler_params=_params("arbitrary"),
    )(logits)


def _dispatch_kernel(tt, blk, n_fill, dest_ref, fill_ref, h_ref, xs_ref, zero_ref, sem, zsem):
    i = pl.program_id(0)
    base = i * (tt * TOP_K)

    @pl.when(i == 0)
    def _():
        zero_ref[...] = jnp.zeros(zero_ref.shape, U32)

        def zcopy(z):
            row = pl.multiple_of(jnp.maximum(fill_ref[z], 0), blk)
            return pltpu.make_async_copy(zero_ref, xs_ref.at[pl.ds(row, blk), :], zsem)

        def zissue(z, carry):
            @pl.when(fill_ref[z] >= 0)
            def _():
                zcopy(z).start()
            return carry

        def zdrain(z, carry):
            @pl.when(fill_ref[z] >= 0)
            def _():
                zcopy(z).wait()
            return carry

        lax.fori_loop(0, n_fill, zissue, 0)
        lax.fori_loop(0, n_fill, zdrain, 0)

    def copy(r, kk):
        d = dest_ref[base + r * TOP_K + kk]
        return pltpu.make_async_copy(h_ref.at[pl.ds(r, 1), :], xs_ref.at[pl.ds(d, 1), :], sem)

    def issue(r, carry):
        for kk in range(TOP_K):
            copy(r, kk).start()
        return carry

    lax.fori_loop(0, tt, issue, 0, unroll=8)
    for _ in range(TOP_K):
        pltpu.make_async_copy(h_ref, xs_ref.at[pl.ds(0, tt), :], sem).wait()


def _dispatch(h_packed, dest_flat, fill_rows, n_slots, blk):
    n, w = h_packed.shape
    tt = min(128, n)
    n_fill = fill_rows.shape[0]
    grid_spec = pltpu.PrefetchScalarGridSpec(
        num_scalar_prefetch=2,
        grid=(n // tt,),
        in_specs=[pl.BlockSpec((tt, w), lambda i, d, f: (i, 0))],
        out_specs=pl.BlockSpec(memory_space=pl.ANY),
        scratch_shapes=[pltpu.VMEM((blk, w), U32), pltpu.SemaphoreType.DMA(()), pltpu.SemaphoreType.DMA(())],
    )
    return pl.pallas_call(
        functools.partial(_dispatch_kernel, tt, blk, n_fill),
        grid_spec=grid_spec,
        out_shape=jax.ShapeDtypeStruct((n_slots, w), U32),
        compiler_params=_params("arbitrary"),
    )(dest_flat, fill_rows, h_packed)


def _expert_kernel(blk, cnt_ref, pstart_ref, fill_ref, xs_ref, w1_ref, w3_ref, w2_ref, y_ref,
                   w1b, w3b, w2b, xbuf, ybuf, in_sem, out_sem):
    e = pl.program_id(0)
    w1b[...] = w1_ref[0].astype(BF16)
    w3b[...] = w3_ref[0].astype(BF16)
    w2b[...] = w2_ref[0].astype(BF16)
    n_blk = (cnt_ref[e] + (blk - 1)) // blk
    base = pstart_ref[e]

    def rows(b):
        return pl.ds(pl.multiple_of(base + b * blk, blk), blk)

    def in_copy(b, slot):
        return pltpu.make_async_copy(xs_ref.at[rows(b), :], xbuf.at[slot], in_sem.at[slot])

    def out_copy(b, slot):
        return pltpu.make_async_copy(ybuf.at[slot], y_ref.at[rows(b), :], out_sem.at[slot])

    @pl.when(n_blk > 0)
    def _():
        in_copy(0, 0).start()

    def body(b, carry):
        slot = b % 2

        @pl.when(b + 1 < n_blk)
        def _():
            in_copy(b + 1, 1 - slot).start()

        in_copy(b, slot).wait()

        @pl.when(b >= 2)
        def _():
            out_copy(b - 2, slot).wait()

        lo, hi = _unpack_halves(xbuf[slot])
        lo = lo.astype(BF16)
        hi = hi.astype(BF16)
        half = lo.shape[1]
        a = jnp.dot(lo, w1b[:half, :], preferred_element_type=F32)
        a = a + jnp.dot(hi, w1b[half:, :], preferred_element_type=F32)
        g = jnp.dot(lo, w3b[:half, :], preferred_element_type=F32)
        g = g + jnp.dot(hi, w3b[half:, :], preferred_element_type=F32)
        mid = (_silu(a) * g).astype(BF16)
        ybuf[slot] = _pack_halves(jnp.dot(mid, w2b[...], preferred_element_type=F32))
        out_copy(b, slot).start()
        return carry

    lax.fori_loop(0, n_blk, body, 0)

    @pl.when(n_blk >= 2)
    def _():
        out_copy(n_blk - 2, n_blk % 2).wait()

    @pl.when(n_blk >= 1)
    def _():
        out_copy(n_blk - 1, (n_blk - 1) % 2).wait()

    @pl.when(e == pl.num_programs(0) - 1)
    def _():
        ybuf[0] = jnp.zeros(ybuf.shape[1:], U32)

        def zcopy(t):
            row = pl.multiple_of(jnp.maximum(fill_ref[N_EXPERTS + t], 0), blk)
            return pltpu.make_async_copy(ybuf.at[0], y_ref.at[pl.ds(row, blk), :], out_sem.at[0])

        def zissue(t, carry):
            @pl.when(fill_ref[N_EXPERTS + t] >= 0)
            def _():
                zcopy(t).start()
            return carry

        def zdrain(t, carry):
            @pl.when(fill_ref[N_EXPERTS + t] >= 0)
            def _():
                zcopy(t).wait()
            return carry

        lax.fori_loop(0, N_EXPERTS, zissue, 0)
        lax.fori_loop(0, N_EXPERTS, zdrain, 0)


def _expert_blocks(xs, counts, pstart, fill_rows, w1, w3, w2, blk):
    n_slots, w = xs.shape
    n_exp, d, de = w1.shape
    grid_spec = pltpu.PrefetchScalarGridSpec(
        num_scalar_prefetch=3,
        grid=(n_exp,),
        in_specs=[pl.BlockSpec(memory_space=pl.ANY),
                  pl.BlockSpec((1, d, de), lambda e, c, p, f: (e, 0, 0)),
                  pl.BlockSpec((1, d, de), lambda e, c, p, f: (e, 0, 0)),
                  pl.BlockSpec((1, de, d), lambda e, c, p, f: (e, 0, 0))],
        out_specs=pl.BlockSpec(memory_space=pl.ANY),
        scratch_shapes=[pltpu.VMEM((d, de), BF16), pltpu.VMEM((d, de), BF16), pltpu.VMEM((de, d), BF16),
                        pltpu.VMEM((2, blk, w), U32), pltpu.VMEM((2, blk, w), U32),
                        pltpu.SemaphoreType.DMA((2,)), pltpu.SemaphoreType.DMA((2,))],
    )
    return pl.pallas_call(
        functools.partial(_expert_kernel, blk),
        grid_spec=grid_spec,
        out_shape=jax.ShapeDtypeStruct((n_slots, w), U32),
        compiler_params=_params("arbitrary"),
    )(counts, pstart, fill_rows, xs, w1, w3, w2)


def _combine_kernel(tt, n_tiles, dest_ref, x1_ref, g2_ref, gate_ref, yb_ref, o_ref, buf, sems):
    i = pl.program_id(0)

    def copy(tile, slot, r, kk):
        d = dest_ref[(tile * tt + r) * TOP_K + kk]
        return pltpu.make_async_copy(yb_ref.at[pl.ds(d, 1), :], buf.at[slot, kk, pl.ds(r, 1), :], sems.at[slot])

    def issue_tile(tile, slot):
        def body(r, carry):
            for kk in range(TOP_K):
                copy(tile, slot, r, kk).start()
            return carry
        lax.fori_loop(0, tt, body, 0, unroll=8)

    def wait_tile(tile, slot):
        for kk in range(TOP_K):
            pltpu.make_async_copy(yb_ref.at[pl.ds(0, tt), :], buf.at[slot, kk], sems.at[slot]).wait()

    slot = i % 2

    @pl.when(i == 0)
    def _():
        issue_tile(0, 0)

    @pl.when(i + 1 < n_tiles)
    def _():
        issue_tile(i + 1, 1 - slot)

    wait_tile(i, slot)

    gate = gate_ref[...]
    wa = gate[:, 0:1]
    wb = gate[:, 1:2]
    lo_a, hi_a = _unpack_halves(buf[slot, 0])
    lo_b, hi_b = _unpack_halves(buf[slot, 1])
    y = jnp.concatenate([wa * lo_a + wb * lo_b, wa * hi_a + wb * hi_b], axis=-1)
    o_ref[...] = x1_ref[...] + g2_ref[0] * y


def _combine(x1, seq, g2, gates, dest_flat, yb):
    n, d = x1.shape
    w = yb.shape[1]
    tt = min(128, seq)
    n_tiles = n // tt
    tiles_per_seq = seq // tt
    grid_spec = pltpu.PrefetchScalarGridSpec(
        num_scalar_prefetch=1,
        grid=(n_tiles,),
        in_specs=[pl.BlockSpec((tt, d), lambda i, dr: (i, 0)),
                  pl.BlockSpec((1, 1, d), lambda i, dr: (i // tiles_per_seq, 0, 0)),
                  pl.BlockSpec((tt, LANES), lambda i, dr: (i, 0)),
                  pl.BlockSpec(memory_space=pl.ANY)],
        out_specs=pl.BlockSpec((tt, d), lambda i, dr: (i, 0)),
        scratch_shapes=[pltpu.VMEM((2, TOP_K, tt, w), U32), pltpu.SemaphoreType.DMA((2,))],
    )
    return pl.pallas_call(
        functools.partial(_combine_kernel, tt, n_tiles),
        grid_spec=grid_spec,
        out_shape=jax.ShapeDtypeStruct((n, d), F32),
        compiler_params=_params("arbitrary"),
    )(dest_flat, x1, g2, gates, yb)


def _rotation_tables(seq):
    half = HEAD_DIM // 2
    theta = ROPE_BASE ** (-np.arange(half, dtype=np.float64) / half)
    ang = np.arange(seq, dtype=np.float64)[:, None] * theta[None, :]
    cos_t = np.concatenate([np.cos(ang), np.cos(ang)], axis=-1).astype(np.float32)
    sin_t = np.concatenate([-np.sin(ang), np.sin(ang)], axis=-1).astype(np.float32)
    return jnp.asarray(cos_t), jnp.asarray(sin_t)


def _layer(x, c, w_ada, b_ada, norm1_w, w_in, forget_bias, q_norm_w, k_norm_w, ret_norm_w, w_out, norm2_w,
           w_coarse, b_coarse, w_fine, b_fine, w1, w3, w2):
    bsz, seq, d = x.shape
    n = bsz * seq
    d_fox = d // 2
    d_ret = d // 2
    n_heads = d_fox // HEAD_DIM

    mod = _ada_modulation(c, w_ada, b_ada)
    sh1, sc1, g1, sh2, sc2, g2 = [m.reshape(bsz, 1, d) for m in jnp.split(mod, 6, axis=-1)]

    f0 = 3 * d_fox
    w_fox = w_in[:, :f0].astype(BF16)
    w_ret = w_in[:, f0 + n_heads:].astype(BF16)
    w_f = jnp.zeros((d, LANES), BF16).at[:, :n_heads].set(w_in[:, f0:f0 + n_heads].astype(BF16))
    fb = jnp.zeros((1, LANES), F32).at[0, :n_heads].set(forget_bias)

    cos_t, sin_t = _rotation_tables(seq)

    x2d = x.reshape(n, d)
    z, log_f = _input_projection(x2d, seq, norm1_w.reshape(1, d), sc1, sh1, w_fox, w_ret, w_f, cos_t, sin_t,
                                 q_norm_w.reshape(1, HEAD_DIM), k_norm_w.reshape(1, HEAD_DIM), fb)

    lf = log_f[:, :n_heads].reshape(bsz, seq, n_heads).transpose(0, 2, 1).reshape(bsz * n_heads, seq)
    cum = _cumsum_rows(lf)

    o_a = _fox_attention(z, cum, bsz, seq, n_heads)
    log_g = jnp.log(1.0 - 2.0 ** (-5.0 - jnp.arange(n_heads, dtype=F32)))
    o_b = _retention(z, log_g, ret_norm_w.reshape(1, d_ret), bsz, seq, n_heads, 3 * d_fox)

    w_router = jnp.zeros((d, LANES), F32)
    w_router = w_router.at[:, :N_GROUPS].set(w_coarse)
    w_router = w_router.at[:, N_GROUPS:N_GROUPS + N_EXPERTS].set(
        w_fine.transpose(1, 0, 2).reshape(d, N_EXPERTS))
    b_router = jnp.zeros((1, LANES), F32)
    b_router = b_router.at[0, :N_GROUPS].set(b_coarse)
    b_router = b_router.at[0, N_GROUPS:N_GROUPS + N_EXPERTS].set(b_fine.reshape(N_EXPERTS))

    wr_hi = w_router.astype(BF16)
    wr_lo = (w_router - wr_hi.astype(F32)).astype(BF16)
    x1, h_packed, logits = _output_projection(o_a, o_b, w_out.astype(BF16), x2d, seq, g1,
                                              norm2_w.reshape(1, d), sc2, sh2,
                                              jnp.concatenate([wr_hi, wr_lo], axis=1), b_router)

    blk = 256
    nk = n * TOP_K
    n_blocks = nk // blk + N_EXPERTS
    gates, ids, plan = _route(logits, blk, n_blocks)
    pstart = plan[0, :N_EXPERTS]
    fill_rows = plan[1, :2 * N_EXPERTS]
    counts = plan[2, :N_EXPERTS]
    eid = ids[0:TOP_K]
    hit = eid[None] == jnp.arange(N_EXPERTS, dtype=jnp.int32)[:, None, None]
    dest = (jnp.sum(jnp.where(hit, pstart[:, None, None], 0), axis=0) + ids[TOP_K:2 * TOP_K]).T.reshape(nk)

    xs = _dispatch(h_packed, dest, fill_rows, n_blocks * blk, blk)
    yb = _expert_blocks(xs, counts, pstart, fill_rows, w1, w3, w2, blk)
    out = _combine(x1, seq, g2, gates, dest, yb)
    return out.reshape(bsz, seq, d)


def kernel(x, c, w_ada, b_ada, norm1_w, w_in, forget_bias, q_norm_w, k_norm_w, ret_norm_w, w_out, norm2_w,
           w_coarse, b_coarse, w_fine, b_fine, w1, w3, w2):
    c_in = c
    for l in range(w_ada.shape[0]):
        x = _layer(x, c_in, w_ada[l], b_ada[l], norm1_w[l], w_in[l], forget_bias[l], q_norm_w[l],
                   k_norm_w[l], ret_norm_w[l], w_out[l], norm2_w[l], w_coarse[l], b_coarse[l],
                   w_fine[l], b_fine[l], w1[l], w3[l], w2[l])
    return x
```

```python
import functools

import jax
import jax.numpy as jnp
import numpy as np
from jax import lax
from jax.experimental import pallas as pl
from jax.experimental.pallas import tpu as pltpu

HEAD_DIM = 128
N_GROUPS = 4
EXPERTS_PER_GROUP = 8
N_EXPERTS = N_GROUPS * EXPERTS_PER_GROUP
TOP_K = 2
ROPE_BASE = 10000.0
EPS = 1e-6

LANES = 128
VMEM_LIMIT = 56 * 1024 * 1024
NEG_BIG = -1e30
LOG2E = 1.4426950408889634

F32 = jnp.float32
BF16 = jnp.bfloat16
U32 = jnp.uint32


def _params(*sem):
    return pltpu.CompilerParams(dimension_semantics=sem, vmem_limit_bytes=VMEM_LIMIT)


def _silu(v):
    return v * (1.0 / (1.0 + jnp.exp(-v)))


def _pack_halves(y):
    w = y.shape[1] // 2
    lo = pltpu.bitcast(y[:, :w].astype(BF16).astype(F32), U32)
    hi = pltpu.bitcast(y[:, w:].astype(BF16).astype(F32), U32)
    return (hi & jnp.uint32(0xFFFF0000)) | (lo >> 16)


def _unpack_halves(p):
    lo = pltpu.bitcast(p << 16, F32)
    hi = pltpu.bitcast(p & jnp.uint32(0xFFFF0000), F32)
    return lo, hi


def _ada_kernel(ct_ref, w_ref, b_ref, o_ref):
    w = w_ref[...]
    rows = []
    for b in range(o_ref.shape[0]):
        if b < 2:
            cb = _silu(ct_ref[:, b:b + 1])
            rows.append(jnp.sum(cb * w, axis=0, keepdims=True) + b_ref[...])
        else:
            rows.append(jnp.zeros_like(b_ref[...]))
    o_ref[...] = jnp.concatenate(rows, axis=0)


def _ada_modulation(c, w_ada, b_ada):
    bsz, d = c.shape
    n = w_ada.shape[1]
    tn = 1024
    ct = jnp.zeros((d, LANES), F32).at[:, :bsz].set(c.T)
    out = pl.pallas_call(
        _ada_kernel,
        grid=(n // tn,),
        in_specs=[pl.BlockSpec((d, LANES), lambda j: (0, 0)),
                  pl.BlockSpec((d, tn), lambda j: (0, j)),
                  pl.BlockSpec((1, tn), lambda j: (0, j))],
        out_specs=pl.BlockSpec((8, tn), lambda j: (0, j)),
        out_shape=jax.ShapeDtypeStruct((8, n), F32),
        compiler_params=_params("arbitrary"),
    )(ct, w_ada, b_ada.reshape(1, n))
    return out[:bsz]


def _inproj_kernel(q_t, r_t, x_ref, nw_ref, sc_ref, sh_ref, wa_ref, wb_ref, wf_ref, cos_ref, sin_ref,
                   qw_ref, kw_ref, fb_ref, z_ref, f_ref, h_ref):
    j = pl.program_id(1)
    r0 = 3 * q_t

    @pl.when(j == 0)
    def _():
        x = x_ref[...]
        ms = jnp.mean(x * x, axis=-1, keepdims=True)
        y = x * lax.rsqrt(ms + EPS) * nw_ref[...]
        h = (y * (1.0 + sc_ref[0]) + sh_ref[0]).astype(BF16)
        h_ref[...] = h
        t = jnp.dot(h, wf_ref[...], preferred_element_type=F32) + fb_ref[...]
        f_ref[...] = jnp.minimum(t, 0.0) - jnp.log(1.0 + jnp.exp(-jnp.abs(t)))

    def heads_of(acc):
        return [acc[:, hh * HEAD_DIM:(hh + 1) * HEAD_DIM] for hh in range(acc.shape[1] // HEAD_DIM)]

    def head_norm(acc, w_row):
        outs = []
        for a in heads_of(acc):
            ms = jnp.mean(a * a, axis=-1, keepdims=True)
            outs.append(a * lax.rsqrt(ms + EPS) * w_row)
        return jnp.concatenate(outs, axis=-1).astype(BF16)

    def rotate(acc, scale):
        cs = cos_ref[...] * scale
        sn = sin_ref[...] * scale
        outs = [a * cs + pltpu.roll(a, HEAD_DIM // 2, 1) * sn for a in heads_of(acc)]
        return jnp.concatenate(outs, axis=-1).astype(BF16)

    def fox():
        return jnp.dot(h_ref[...], wa_ref[...], preferred_element_type=F32)

    def ret():
        return jnp.dot(h_ref[...], wb_ref[...], preferred_element_type=F32)

    @pl.when(j < q_t)
    def _():
        z_ref[...] = head_norm(fox(), qw_ref[...] * (LOG2E * HEAD_DIM ** -0.5))

    @pl.when((j >= q_t) & (j < 2 * q_t))
    def _():
        z_ref[...] = head_norm(fox(), kw_ref[...])

    @pl.when((j >= 2 * q_t) & (j < r0))
    def _():
        z_ref[...] = fox().astype(BF16)

    @pl.when((j >= r0) & (j < r0 + r_t))
    def _():
        z_ref[...] = rotate(ret(), 1.0)

    @pl.when((j >= r0 + r_t) & (j < r0 + 2 * r_t))
    def _():
        z_ref[...] = rotate(ret(), HEAD_DIM ** -0.5)

    @pl.when(j >= r0 + 2 * r_t)
    def _():
        z_ref[...] = ret().astype(BF16)


def _input_projection(x2d, seq, norm_w, sc1, sh1, w_fox, w_ret, w_f, cos_t, sin_t, qw, kw, fb):
    n, d = x2d.shape
    tm, tn = min(1024, seq), 512
    fox_tiles = w_fox.shape[1] // tn
    ret_tiles = w_ret.shape[1] // tn
    tiles_per_seq = seq // tm
    kern = functools.partial(_inproj_kernel, fox_tiles // 3, ret_tiles // 4)
    bsel = lambda i, j: (i // tiles_per_seq, 0, 0)
    const = lambda i, j: (0, 0)
    return pl.pallas_call(
        kern,
        grid=(n // tm, fox_tiles + ret_tiles),
        in_specs=[pl.BlockSpec((tm, d), lambda i, j: (i, 0)),
                  pl.BlockSpec((1, d), const),
                  pl.BlockSpec((1, 1, d), bsel),
                  pl.BlockSpec((1, 1, d), bsel),
                  pl.BlockSpec((d, tn), lambda i, j: (0, jnp.minimum(j, fox_tiles - 1))),
                  pl.BlockSpec((d, tn), lambda i, j: (0, jnp.maximum(j - fox_tiles, 0))),
                  pl.BlockSpec((d, LANES), const),
                  pl.BlockSpec((tm, HEAD_DIM), lambda i, j: (i % tiles_per_seq, 0)),
                  pl.BlockSpec((tm, HEAD_DIM), lambda i, j: (i % tiles_per_seq, 0)),
                  pl.BlockSpec((1, HEAD_DIM), const),
                  pl.BlockSpec((1, HEAD_DIM), const),
                  pl.BlockSpec((1, LANES), const)],
        out_specs=[pl.BlockSpec((tm, tn), lambda i, j: (i, j)),
                   pl.BlockSpec((tm, LANES), lambda i, j: (i, 0))],
        out_shape=[jax.ShapeDtypeStruct((n, w_fox.shape[1] + w_ret.shape[1]), BF16),
                   jax.ShapeDtypeStruct((n, LANES), F32)],
        scratch_shapes=[pltpu.VMEM((tm, d), BF16)],
        compiler_params=_params("arbitrary", "arbitrary"),
    )(x2d, norm_w, sc1, sh1, w_fox, w_ret, w_f, cos_t, sin_t, qw, kw, fb)


def _cumsum_kernel(x_ref, o_ref):
    x = x_ref[0]
    r = x.shape[0]
    a = lax.broadcasted_iota(jnp.int32, (LANES, LANES), 0)
    b = lax.broadcasted_iota(jnp.int32, (LANES, LANES), 1)
    upper = (a <= b).astype(F32)
    within = jnp.dot(x, upper, precision=lax.Precision.HIGHEST, preferred_element_type=F32)
    tot = jnp.broadcast_to(within[:, LANES - 1:LANES], (r, LANES))
    ra = lax.broadcasted_iota(jnp.int32, (r, r), 0)
    rb = lax.broadcasted_iota(jnp.int32, (r, r), 1)
    strict = (rb < ra).astype(F32)
    before = jnp.dot(strict, tot, precision=lax.Precision.HIGHEST, preferred_element_type=F32)
    o_ref[0] = within + before


def _cumsum_rows(x):
    g, s = x.shape
    r = s // LANES
    out = pl.pallas_call(
        _cumsum_kernel,
        grid=(g,),
        in_specs=[pl.BlockSpec((1, r, LANES), lambda i: (i, 0, 0))],
        out_specs=pl.BlockSpec((1, r, LANES), lambda i: (i, 0, 0)),
        out_shape=jax.ShapeDtypeStruct((g, r, LANES), F32),
        compiler_params=_params("arbitrary"),
    )(x.reshape(g, r, LANES))
    return out.reshape(g, 1, s)


def _fox_kernel(tq, q_ref, k_ref, v_ref, cum_ref, o_ref, sa_ref, sb_ref, m_ref, l_ref, acc_ref):
    qi = pl.program_id(2)
    q_start = pl.multiple_of(qi * tq, tq)
    c0 = cum_ref[0, :, pl.ds(q_start, LANES)][:, 0:1]

    m_ref[...] = jnp.full(m_ref.shape, NEG_BIG, F32)
    l_ref[...] = jnp.zeros(l_ref.shape, F32)
    acc_ref[...] = jnp.zeros(acc_ref.shape, F32)
    n_slabs = tq // LANES

    def scores(kb, s_ref):
        start = pl.multiple_of(kb * tq, tq)
        k = k_ref[pl.ds(start, tq), :]
        bias = (c0 - cum_ref[0, :, pl.ds(start, tq)]) * LOG2E
        s_ref[...] = lax.dot_general(q_ref[...], k, (((1,), (1,)), ((), ())),
                                     preferred_element_type=F32) + bias

    def softmax_pv(kb, s_ref, masked):
        start = pl.multiple_of(kb * tq, tq)
        v = v_ref[pl.ds(start, tq), :]
        slabs = []
        for j in range(n_slabs):
            t = s_ref[:, j * LANES:(j + 1) * LANES]
            if masked:
                row = lax.broadcasted_iota(jnp.int32, t.shape, 0)
                col = lax.broadcasted_iota(jnp.int32, t.shape, 1) + j * LANES
                t = jnp.where(col <= row, t, NEG_BIG)
            slabs.append(t)
        mx = slabs[0]
        for t in slabs[1:]:
            mx = jnp.maximum(mx, t)
        m_prev = m_ref[...]
        m_new = jnp.maximum(m_prev, jnp.max(mx, axis=-1, keepdims=True))
        alpha = jnp.exp2(m_prev - m_new)
        probs = [jnp.exp2(t - m_new) for t in slabs]
        psum = probs[0]
        for t in probs[1:]:
            psum = psum + t
        l_ref[...] = alpha * l_ref[...] + psum
        p = jnp.concatenate([t.astype(BF16) for t in probs], axis=-1)
        acc_ref[...] = alpha * acc_ref[...] + jnp.dot(p, v, preferred_element_type=F32)
        m_ref[...] = m_new

    scores(0, sa_ref)

    def pair(kb):
        scores(kb + 1, sb_ref)
        softmax_pv(kb, sa_ref, False)
        scores(kb + 2, sa_ref)
        softmax_pv(kb + 1, sb_ref, False)

    def body4(i, carry):
        pair(4 * i)
        pair(4 * i + 2)
        return carry

    def body2(i, carry):
        pair(2 * i)
        return carry

    n4 = qi // 4
    lax.fori_loop(0, n4, body4, 0)
    lax.fori_loop(2 * n4, qi // 2, body2, 0)

    @pl.when(qi % 2 == 0)
    def _():
        softmax_pv(qi, sa_ref, True)

    @pl.when(qi % 2 == 1)
    def _():
        scores(qi, sb_ref)
        softmax_pv(qi - 1, sa_ref, False)
        softmax_pv(qi, sb_ref, True)

    o_ref[...] = (acc_ref[...] / jnp.sum(l_ref[...], axis=-1, keepdims=True)).astype(BF16)


def _fox_attention(z, cum, bsz, seq, n_heads):
    tq = min(512, seq)
    nq = seq // tq
    kern = functools.partial(_fox_kernel, tq)
    return pl.pallas_call(
        kern,
        grid=(bsz, n_heads, nq),
        in_specs=[pl.BlockSpec((tq, HEAD_DIM), lambda b, h, i: (b * nq + i, h)),
                  pl.BlockSpec((seq, HEAD_DIM), lambda b, h, i: (b, n_heads + h)),
                  pl.BlockSpec((seq, HEAD_DIM), lambda b, h, i: (b, 2 * n_heads + h)),
                  pl.BlockSpec((1, 1, seq), lambda b, h, i: (b * n_heads + h, 0, 0))],
        out_specs=pl.BlockSpec((tq, HEAD_DIM), lambda b, h, i: (b * nq + i, h)),
        out_shape=jax.ShapeDtypeStruct((bsz * seq, n_heads * HEAD_DIM), BF16),
        scratch_shapes=[pltpu.VMEM((tq, tq), F32), pltpu.VMEM((tq, tq), F32),
                        pltpu.VMEM((tq, LANES), F32), pltpu.VMEM((tq, LANES), F32),
                        pltpu.VMEM((tq, HEAD_DIM), F32)],
        compiler_params=_params("arbitrary", "arbitrary", "arbitrary"),
    )(z, z, z, cum)


def _ret_kernel(chunk, n_heads, lg_ref, q_ref, k_ref, v_ref, g_ref, nw_ref, o_ref, state_ref, decay_ref):
    first = (pl.program_id(0) == 0) & (pl.program_id(1) == 0)

    @pl.when(first)
    def _():
        i = lax.broadcasted_iota(jnp.int32, (chunk, chunk), 0)
        jj = lax.broadcasted_iota(jnp.int32, (chunk, chunk), 1)
        diff = (i - jj).astype(F32)
        for h in range(n_heads):
            decay_ref[h] = jnp.where(diff >= 0, jnp.exp(lg_ref[h] * jnp.maximum(diff, 0.0)), 0.0)

    @pl.when(pl.program_id(1) == 0)
    def _():
        state_ref[...] = jnp.zeros(state_ref.shape, F32)

    pos = lax.broadcasted_iota(jnp.int32, (chunk, HEAD_DIM), 0).astype(F32)
    for h in range(n_heads):
        log_g = lg_ref[h]
        cols = slice(h * HEAD_DIM, (h + 1) * HEAD_DIM)
        q = q_ref[:, cols]
        k = k_ref[:, cols]
        v = v_ref[:, cols]
        scores = lax.dot_general(q, k, (((1,), (1,)), ((), ())), preferred_element_type=F32)
        scores = scores * decay_ref[h]
        intra = jnp.dot(scores.astype(BF16), v, preferred_element_type=F32)
        state = state_ref[h]
        inter = jnp.dot(q, state.astype(BF16), preferred_element_type=F32) * jnp.exp(log_g * (pos + 1.0))
        kd = (k.astype(F32) * jnp.exp(log_g * (chunk - 1.0 - pos))).astype(BF16)
        kv = lax.dot_general(kd, v, (((0,), (0,)), ((), ())), preferred_element_type=F32)
        state_ref[h] = state * jnp.exp(jnp.full((1, HEAD_DIM), chunk, F32) * log_g) + kv
        o = intra + inter
        ms = jnp.mean(o * o, axis=-1, keepdims=True)
        o = o * lax.rsqrt(ms + EPS) * nw_ref[:, cols]
        o_ref[:, cols] = (o * _silu(g_ref[:, cols].astype(F32))).astype(BF16)


def _retention(z, log_g, norm_w, bsz, seq, n_heads, col0):
    chunk = min(256, seq)
    nt = seq // chunk
    width = n_heads * HEAD_DIM
    c0 = col0 // width
    kern = functools.partial(_ret_kernel, chunk, n_heads)

    def sec(s):
        return pl.BlockSpec((chunk, width), lambda b, t, lg: (b * nt + t, c0 + s))

    grid_spec = pltpu.PrefetchScalarGridSpec(
        num_scalar_prefetch=1,
        grid=(bsz, nt),
        in_specs=[sec(0), sec(1), sec(2), sec(3), pl.BlockSpec((1, width), lambda b, t, lg: (0, 0))],
        out_specs=pl.BlockSpec((chunk, width), lambda b, t, lg: (b * nt + t, 0)),
        scratch_shapes=[pltpu.VMEM((n_heads, HEAD_DIM, HEAD_DIM), F32),
                        pltpu.VMEM((n_heads, chunk, chunk), F32)],
    )
    return pl.pallas_call(
        kern,
        grid_spec=grid_spec,
        out_shape=jax.ShapeDtypeStruct((bsz * seq, width), BF16),
        compiler_params=_params("arbitrary", "arbitrary"),
    )(log_g, z, z, z, z, norm_w)


def _outproj_kernel(oa_ref, ob_ref, wa_ref, wb_ref, x_ref, g1_ref, nw_ref, sc_ref, sh_ref, wr_ref, br_ref,
                    x1_ref, hp_ref, lg_ref):
    mix = jnp.dot(oa_ref[...], wa_ref[...], preferred_element_type=F32)
    mix = mix + jnp.dot(ob_ref[...], wb_ref[...], preferred_element_type=F32)
    x1 = x_ref[...] + g1_ref[0] * mix
    x1_ref[...] = x1
    ms = jnp.mean(x1 * x1, axis=-1, keepdims=True)
    h2 = x1 * lax.rsqrt(ms + EPS) * nw_ref[...] * (1.0 + sc_ref[0]) + sh_ref[0]
    hp_ref[...] = _pack_halves(h2)
    h_hi = h2.astype(BF16)
    h_lo = (h2 - h_hi.astype(F32)).astype(BF16)
    both = jnp.dot(h_hi, wr_ref[...], preferred_element_type=F32)
    cross = jnp.dot(h_lo, wr_ref[:, :LANES], preferred_element_type=F32)
    lg_ref[...] = both[:, :LANES] + both[:, LANES:] + cross + br_ref[...]


def _output_projection(o_a, o_b, w_out, x2d, seq, g1, norm_w, sc2, sh2, w_router, b_router):
    n, d = x2d.shape
    da = o_a.shape[1]
    tm = min(256, seq)
    tiles_per_seq = seq // tm
    bsel = lambda i: (i // tiles_per_seq, 0, 0)
    return pl.pallas_call(
        _outproj_kernel,
        grid=(n // tm,),
        in_specs=[pl.BlockSpec((tm, da), lambda i: (i, 0)),
                  pl.BlockSpec((tm, da), lambda i: (i, 0)),
                  pl.BlockSpec((da, d), lambda i: (0, 0)),
                  pl.BlockSpec((da, d), lambda i: (1, 0)),
                  pl.BlockSpec((tm, d), lambda i: (i, 0)),
                  pl.BlockSpec((1, 1, d), bsel),
                  pl.BlockSpec((1, d), lambda i: (0, 0)),
                  pl.BlockSpec((1, 1, d), bsel),
                  pl.BlockSpec((1, 1, d), bsel),
                  pl.BlockSpec((d, 2 * LANES), lambda i: (0, 0)),
                  pl.BlockSpec((1, LANES), lambda i: (0, 0))],
        out_specs=[pl.BlockSpec((tm, d), lambda i: (i, 0)),
                   pl.BlockSpec((tm, d // 2), lambda i: (i, 0)),
                   pl.BlockSpec((tm, LANES), lambda i: (i, 0))],
        out_shape=[jax.ShapeDtypeStruct((n, d), F32),
                   jax.ShapeDtypeStruct((n, d // 2), U32),
                   jax.ShapeDtypeStruct((n, LANES), F32)],
        compiler_params=_params("arbitrary"),
    )(o_a, o_b, w_out, w_out, x2d, g1, norm_w, sc2, sh2, w_router, b_router)


def _route_kernel(blk, n_blocks, lg_ref, gate_ref, ids_ref, plan_ref, run_ref):
    i = pl.program_id(0)

    @pl.when(i == 0)
    def _():
        run_ref[...] = jnp.zeros(run_ref.shape, F32)

    lg = lg_ref[...]
    tt = lg.shape[0]
    lane = lax.broadcasted_iota(jnp.int32, lg.shape, 1).astype(F32)
    big = 1e6

    def rmax(v):
        return jnp.max(v, axis=-1, keepdims=True)

    def rmin(v):
        return jnp.min(v, axis=-1, keepdims=True)

    def rsum(v):
        return jnp.sum(v, axis=-1, keepdims=True)

    cmask = lane < N_GROUPS
    cm = jnp.where(cmask, lg, NEG_BIG)
    ce = jnp.where(cmask, jnp.exp(cm - rmax(cm)), 0.0)
    pgrp = ce / rsum(ce)
    p_g = rmax(pgrp)
    g_sel = rmin(jnp.where(cmask & (pgrp == p_g), lane, big))

    lo = N_GROUPS + EXPERTS_PER_GROUP * g_sel
    fmask = (lane >= lo) & (lane < lo + EXPERTS_PER_GROUP)
    fm = jnp.where(fmask, lg, NEG_BIG)
    fe = jnp.where(fmask, jnp.exp(fm - rmax(fm)), 0.0)
    fp = fe / rsum(fe)
    fp = jnp.where(fmask, fp, -1.0)
    p1 = rmax(fp)
    i1 = rmin(jnp.where(fp == p1, lane, big))
    fp2 = jnp.where(lane == i1, -1.0, fp)
    p2 = rmax(fp2)
    i2 = rmin(jnp.where(fp2 == p2, lane, big))
    denom = p1 + p2
    w1 = p_g * p1 / denom
    w2 = p_g * p2 / denom
    e1 = i1 - N_GROUPS
    e2 = i2 - N_GROUPS

    gate_ref[...] = jnp.where(lane == 0, w1, jnp.where(lane == 1, w2, 0.0))

    oh1 = (lane == e1).astype(F32)
    oh2 = (lane == e2).astype(F32)
    both = oh1 + oh2
    ra = lax.broadcasted_iota(jnp.int32, (tt, tt), 0)
    rb = lax.broadcasted_iota(jnp.int32, (tt, tt), 1)
    strict = (rb < ra).astype(BF16)
    prefix = jnp.dot(strict, both.astype(BF16), preferred_element_type=F32) + run_ref[...]
    r1 = rsum(prefix * oh1)
    r2 = rsum(prefix * oh2)
    run_ref[...] = run_ref[...] + jnp.sum(both, axis=0, keepdims=True)

    packed = jnp.where(lane == 0, e1, jnp.where(lane == 1, e2, jnp.where(lane == 2, r1,
                                                                        jnp.where(lane == 3, r2, 0.0))))
    ids_ref[...] = jnp.transpose(packed)[:8, :].astype(jnp.int32)

    @pl.when(i == pl.num_programs(0) - 1)
    def _():
        cnt = jnp.broadcast_to(run_ref[...], (8, LANES))
        lane8 = lax.broadcasted_iota(jnp.int32, (8, LANES), 1)
        padded = jnp.floor((cnt + (blk - 1.0)) * (1.0 / blk)) * blk
        pend = padded
        for sh in (1, 2, 4, 8, 16, 32, 64):
            pend = pend + jnp.where(lane8 >= sh, pltpu.roll(pend, sh, 1), 0.0)
        pstart = pend - padded
        total = jnp.max(pend, axis=-1, keepdims=True)
        tail = total + (lane8 - N_EXPERTS).astype(F32) * blk
        fill = jnp.where(lane8 < N_EXPERTS, jnp.where(padded > 0, pend - blk, -1.0),
                         jnp.where((lane8 < 2 * N_EXPERTS) & (tail < n_blocks * blk), tail, -1.0))
        row8 = lax.broadcasted_iota(jnp.int32, (8, LANES), 0)
        plan_ref[...] = jnp.where(row8 == 0, pstart, jnp.where(row8 == 1, fill,
                                                               jnp.where(row8 == 2, cnt, 0.0))).astype(jnp.int32)


def _route(logits, blk, n_blocks):
    n = logits.shape[0]
    tt = min(512, n)
    blkspec = lambda: pl.BlockSpec((tt, LANES), lambda i: (i, 0))
    return pl.pallas_call(
        functools.partial(_route_kernel, blk, n_blocks),
        grid=(n // tt,),
        in_specs=[blkspec()],
        out_specs=[blkspec(),
                   pl.BlockSpec((8, tt), lambda i: (0, i)),
                   pl.BlockSpec((8, LANES), lambda i: (0, 0))],
        out_shape=[jax.ShapeDtypeStruct((n, LANES), F32),
                   jax.ShapeDtypeStruct((8, n), jnp.int32),
                   jax.ShapeDtypeStruct((8, LANES), jnp.int32)],
        scratch_shapes=[pltpu.VMEM((1, LANES), F32)],
        compiler_params=_params("arbitrary"),
    )(logits)


def _dispatch_kernel(tt, blk, n_fill, dest_ref, fill_ref, h_ref, xs_ref, zero_ref, sem, zsem):
    i = pl.program_id(0)
    base = i * (tt * TOP_K)

    @pl.when(i == 0)
    def _():
        zero_ref[...] = jnp.zeros(zero_ref.shape, U32)

        def zcopy(z):
            row = pl.multiple_of(jnp.maximum(fill_ref[z], 0), blk)
            return pltpu.make_async_copy(zero_ref, xs_ref.at[pl.ds(row, blk), :], zsem)

        def zissue(z, carry):
            @pl.when(fill_ref[z] >= 0)
            def _():
                zcopy(z).start()
            return carry

        def zdrain(z, carry):
            @pl.when(fill_ref[z] >= 0)
            def _():
                zcopy(z).wait()
            return carry

        lax.fori_loop(0, n_fill, zissue, 0)
        lax.fori_loop(0, n_fill, zdrain, 0)

    def copy(r, kk):
        d = dest_ref[base + r * TOP_K + kk]
        return pltpu.make_async_copy(h_ref.at[pl.ds(r, 1), :], xs_ref.at[pl.ds(d, 1), :], sem)

    def issue(r, carry):
        for kk in range(TOP_K):
            copy(r, kk).start(priority=kk % 2)
        return carry

    lax.fori_loop(0, tt, issue, 0, unroll=8)
    for _ in range(TOP_K):
        pltpu.make_async_copy(h_ref, xs_ref.at[pl.ds(0, tt), :], sem).wait()


def _dispatch(h_packed, dest_flat, fill_rows, n_slots, blk):
    n, w = h_packed.shape
    tt = min(128, n)
    n_fill = fill_rows.shape[0]
    grid_spec = pltpu.PrefetchScalarGridSpec(
        num_scalar_prefetch=2,
        grid=(n // tt,),
        in_specs=[pl.BlockSpec((tt, w), lambda i, d, f: (i, 0))],
        out_specs=pl.BlockSpec(memory_space=pl.ANY),
        scratch_shapes=[pltpu.VMEM((blk, w), U32), pltpu.SemaphoreType.DMA(()), pltpu.SemaphoreType.DMA(())],
    )
    return pl.pallas_call(
        functools.partial(_dispatch_kernel, tt, blk, n_fill),
        grid_spec=grid_spec,
        out_shape=jax.ShapeDtypeStruct((n_slots, w), U32),
        compiler_params=_params("arbitrary"),
    )(dest_flat, fill_rows, h_packed)


def _expert_kernel(blk, cnt_ref, pstart_ref, fill_ref, xs_ref, w1_ref, w3_ref, w2_ref, y_ref,
                   w1b, w3b, w2b, xbuf, ybuf, in_sem, out_sem):
    e = pl.program_id(0)
    w1b[...] = w1_ref[0].astype(BF16)
    w3b[...] = w3_ref[0].astype(BF16)
    w2b[...] = w2_ref[0].astype(BF16)
    n_blk = (cnt_ref[e] + (blk - 1)) // blk
    base = pstart_ref[e]

    def rows(b):
        return pl.ds(pl.multiple_of(base + b * blk, blk), blk)

    def in_copy(b, slot):
        return pltpu.make_async_copy(xs_ref.at[rows(b), :], xbuf.at[slot], in_sem.at[slot])

    def out_copy(b, slot):
        return pltpu.make_async_copy(ybuf.at[slot], y_ref.at[rows(b), :], out_sem.at[slot])

    @pl.when(n_blk > 0)
    def _():
        in_copy(0, 0).start(priority=1)

    def body(b, carry):
        slot = b % 2

        @pl.when(b + 1 < n_blk)
        def _():
            in_copy(b + 1, 1 - slot).start(priority=1)

        in_copy(b, slot).wait()

        @pl.when(b >= 2)
        def _():
            out_copy(b - 2, slot).wait()

        lo, hi = _unpack_halves(xbuf[slot])
        lo = lo.astype(BF16)
        hi = hi.astype(BF16)
        half = lo.shape[1]
        a = jnp.dot(lo, w1b[:half, :], preferred_element_type=F32)
        a = a + jnp.dot(hi, w1b[half:, :], preferred_element_type=F32)
        g = jnp.dot(lo, w3b[:half, :], preferred_element_type=F32)
        g = g + jnp.dot(hi, w3b[half:, :], preferred_element_type=F32)
        mid = (_silu(a) * g).astype(BF16)
        ybuf[slot] = _pack_halves(jnp.dot(mid, w2b[...], preferred_element_type=F32))
        out_copy(b, slot).start(priority=1)
        return carry

    lax.fori_loop(0, n_blk, body, 0)

    @pl.when(n_blk >= 2)
    def _():
        out_copy(n_blk - 2, n_blk % 2).wait()

    @pl.when(n_blk >= 1)
    def _():
        out_copy(n_blk - 1, (n_blk - 1) % 2).wait()

    @pl.when(e == pl.num_programs(0) - 1)
    def _():
        ybuf[0] = jnp.zeros(ybuf.shape[1:], U32)

        def zcopy(t):
            row = pl.multiple_of(jnp.maximum(fill_ref[N_EXPERTS + t], 0), blk)
            return pltpu.make_async_copy(ybuf.at[0], y_ref.at[pl.ds(row, blk), :], out_sem.at[0])

        def zissue(t, carry):
            @pl.when(fill_ref[N_EXPERTS + t] >= 0)
            def _():
                zcopy(t).start()
            return carry

        def zdrain(t, carry):
            @pl.when(fill_ref[N_EXPERTS + t] >= 0)
            def _():
                zcopy(t).wait()
            return carry

        lax.fori_loop(0, N_EXPERTS, zissue, 0)
        lax.fori_loop(0, N_EXPERTS, zdrain, 0)


def _expert_blocks(xs, counts, pstart, fill_rows, w1, w3, w2, blk):
    n_slots, w = xs.shape
    n_exp, d, de = w1.shape
    grid_spec = pltpu.PrefetchScalarGridSpec(
        num_scalar_prefetch=3,
        grid=(n_exp,),
        in_specs=[pl.BlockSpec(memory_space=pl.ANY),
                  pl.BlockSpec((1, d, de), lambda e, c, p, f: (e, 0, 0)),
                  pl.BlockSpec((1, d, de), lambda e, c, p, f: (e, 0, 0)),
                  pl.BlockSpec((1, de, d), lambda e, c, p, f: (e, 0, 0))],
        out_specs=pl.BlockSpec(memory_space=pl.ANY),
        scratch_shapes=[pltpu.VMEM((d, de), BF16), pltpu.VMEM((d, de), BF16), pltpu.VMEM((de, d), BF16),
                        pltpu.VMEM((2, blk, w), U32), pltpu.VMEM((2, blk, w), U32),
                        pltpu.SemaphoreType.DMA((2,)), pltpu.SemaphoreType.DMA((2,))],
    )
    return pl.pallas_call(
        functools.partial(_expert_kernel, blk),
        grid_spec=grid_spec,
        out_shape=jax.ShapeDtypeStruct((n_slots, w), U32),
        compiler_params=_params("arbitrary"),
    )(counts, pstart, fill_rows, xs, w1, w3, w2)


def _combine_kernel(tt, n_tiles, dest_ref, x1_ref, g2_ref, gate_ref, yb_ref, o_ref, buf, sems):
    i = pl.program_id(0)

    def copy(tile, slot, r, kk):
        d = dest_ref[(tile * tt + r) * TOP_K + kk]
        return pltpu.make_async_copy(yb_ref.at[pl.ds(d, 1), :], buf.at[slot, kk, pl.ds(r, 1), :], sems.at[slot])

    def issue_tile(tile, slot):
        def body(r, carry):
            for kk in range(TOP_K):
                copy(tile, slot, r, kk).start(priority=kk % 2)
            return carry
        lax.fori_loop(0, tt, body, 0, unroll=8)

    def wait_tile(tile, slot):
        for kk in range(TOP_K):
            pltpu.make_async_copy(yb_ref.at[pl.ds(0, tt), :], buf.at[slot, kk], sems.at[slot]).wait()

    slot = i % 2

    @pl.when(i == 0)
    def _():
        issue_tile(0, 0)

    @pl.when(i + 1 < n_tiles)
    def _():
        issue_tile(i + 1, 1 - slot)

    wait_tile(i, slot)

    gate = gate_ref[...]
    wa = gate[:, 0:1]
    wb = gate[:, 1:2]
    lo_a, hi_a = _unpack_halves(buf[slot, 0])
    lo_b, hi_b = _unpack_halves(buf[slot, 1])
    y = jnp.concatenate([wa * lo_a + wb * lo_b, wa * hi_a + wb * hi_b], axis=-1)
    o_ref[...] = x1_ref[...] + g2_ref[0] * y


def _combine(x1, seq, g2, gates, dest_flat, yb):
    n, d = x1.shape
    w = yb.shape[1]
    tt = min(128, seq)
    n_tiles = n // tt
    tiles_per_seq = seq // tt
    grid_spec = pltpu.PrefetchScalarGridSpec(
        num_scalar_prefetch=1,
        grid=(n_tiles,),
        in_specs=[pl.BlockSpec((tt, d), lambda i, dr: (i, 0)),
                  pl.BlockSpec((1, 1, d), lambda i, dr: (i // tiles_per_seq, 0, 0)),
                  pl.BlockSpec((tt, LANES), lambda i, dr: (i, 0)),
                  pl.BlockSpec(memory_space=pl.ANY)],
        out_specs=pl.BlockSpec((tt, d), lambda i, dr: (i, 0)),
        scratch_shapes=[pltpu.VMEM((2, TOP_K, tt, w), U32), pltpu.SemaphoreType.DMA((2,))],
    )
    return pl.pallas_call(
        functools.partial(_combine_kernel, tt, n_tiles),
        grid_spec=grid_spec,
        out_shape=jax.ShapeDtypeStruct((n, d), F32),
        compiler_params=_params("arbitrary"),
    )(dest_flat, x1, g2, gates, yb)


def _rotation_tables(seq):
    half = HEAD_DIM // 2
    theta = ROPE_BASE ** (-np.arange(half, dtype=np.float64) / half)
    ang = np.arange(seq, dtype=np.float64)[:, None] * theta[None, :]
    cos_t = np.concatenate([np.cos(ang), np.cos(ang)], axis=-1).astype(np.float32)
    sin_t = np.concatenate([-np.sin(ang), np.sin(ang)], axis=-1).astype(np.float32)
    return jnp.asarray(cos_t), jnp.asarray(sin_t)


def _layer(x, c, w_ada, b_ada, norm1_w, w_in, forget_bias, q_norm_w, k_norm_w, ret_norm_w, w_out, norm2_w,
           w_coarse, b_coarse, w_fine, b_fine, w1, w3, w2):
    bsz, seq, d = x.shape
    n = bsz * seq
    d_fox = d // 2
    d_ret = d // 2
    n_heads = d_fox // HEAD_DIM

    mod = _ada_modulation(c, w_ada, b_ada)
    sh1, sc1, g1, sh2, sc2, g2 = [m.reshape(bsz, 1, d) for m in jnp.split(mod, 6, axis=-1)]

    f0 = 3 * d_fox
    w_fox = w_in[:, :f0].astype(BF16)
    w_ret = w_in[:, f0 + n_heads:].astype(BF16)
    w_f = jnp.zeros((d, LANES), BF16).at[:, :n_heads].set(w_in[:, f0:f0 + n_heads].astype(BF16))
    fb = jnp.zeros((1, LANES), F32).at[0, :n_heads].set(forget_bias)

    cos_t, sin_t = _rotation_tables(seq)

    x2d = x.reshape(n, d)
    z, log_f = _input_projection(x2d, seq, norm1_w.reshape(1, d), sc1, sh1, w_fox, w_ret, w_f, cos_t, sin_t,
                                 q_norm_w.reshape(1, HEAD_DIM), k_norm_w.reshape(1, HEAD_DIM), fb)

    lf = log_f[:, :n_heads].reshape(bsz, seq, n_heads).transpose(0, 2, 1).reshape(bsz * n_heads, seq)
    cum = _cumsum_rows(lf)

    o_a = _fox_attention(z, cum, bsz, seq, n_heads)
    log_g = jnp.log(1.0 - 2.0 ** (-5.0 - jnp.arange(n_heads, dtype=F32)))
    o_b = _retention(z, log_g, ret_norm_w.reshape(1, d_ret), bsz, seq, n_heads, 3 * d_fox)

    w_router = jnp.zeros((d, LANES), F32)
    w_router = w_router.at[:, :N_GROUPS].set(w_coarse)
    w_router = w_router.at[:, N_GROUPS:N_GROUPS + N_EXPERTS].set(
        w_fine.transpose(1, 0, 2).reshape(d, N_EXPERTS))
    b_router = jnp.zeros((1, LANES), F32)
    b_router = b_router.at[0, :N_GROUPS].set(b_coarse)
    b_router = b_router.at[0, N_GROUPS:N_GROUPS + N_EXPERTS].set(b_fine.reshape(N_EXPERTS))

    wr_hi = w_router.astype(BF16)
    wr_lo = (w_router - wr_hi.astype(F32)).astype(BF16)
    x1, h_packed, logits = _output_projection(o_a, o_b, w_out.astype(BF16), x2d, seq, g1,
                                              norm2_w.reshape(1, d), sc2, sh2,
                                              jnp.concatenate([wr_hi, wr_lo], axis=1), b_router)

    blk = 256
    nk = n * TOP_K
    n_blocks = nk // blk + N_EXPERTS
    gates, ids, plan = _route(logits, blk, n_blocks)
    pstart = plan[0, :N_EXPERTS]
    fill_rows = plan[1, :2 * N_EXPERTS]
    counts = plan[2, :N_EXPERTS]
    eid = ids[0:TOP_K]
    hit = eid[None] == jnp.arange(N_EXPERTS, dtype=jnp.int32)[:, None, None]
    dest = (jnp.sum(jnp.where(hit, pstart[:, None, None], 0), axis=0) + ids[TOP_K:2 * TOP_K]).T.reshape(nk)

    xs = _dispatch(h_packed, dest, fill_rows, n_blocks * blk, blk)
    yb = _expert_blocks(xs, counts, pstart, fill_rows, w1, w3, w2, blk)
    out = _combine(x1, seq, g2, gates, dest, yb)
    return out.reshape(bsz, seq, d)


def kernel(x, c, w_ada, b_ada, norm1_w, w_in, forget_bias, q_norm_w, k_norm_w, ret_norm_w, w_out, norm2_w,
           w_coarse, b_coarse, w_fine, b_fine, w1, w3, w2):
    c_in = c
    for l in range(w_ada.shape[0]):
        x = _layer(x, c_in, w_ada[l], b_ada[l], norm1_w[l], w_in[l], forget_bias[l], q_norm_w[l],
                   k_norm_w[l], ret_norm_w[l], w_out[l], norm2_w[l], w_coarse[l], b_coarse[l],
                   w_fine[l], b_fine[l], w1[l], w3[l], w2[l])
    return x
```

```python
import functools

import jax
import jax.numpy as jnp
import numpy as np
from jax import lax
from jax.experimental import pallas as pl
from jax.experimental.pallas import tpu as pltpu

HEAD_DIM = 128
N_GROUPS = 4
EXPERTS_PER_GROUP = 8
N_EXPERTS = N_GROUPS * EXPERTS_PER_GROUP
TOP_K = 2
ROPE_BASE = 10000.0
EPS = 1e-6

LANES = 128
VMEM_LIMIT = 56 * 1024 * 1024
NEG_BIG = -1e30
LOG2E = 1.4426950408889634

F32 = jnp.float32
BF16 = jnp.bfloat16
U32 = jnp.uint32


def _params(*sem):
    return pltpu.CompilerParams(dimension_semantics=sem, vmem_limit_bytes=VMEM_LIMIT)


def _silu(v):
    return v * (1.0 / (1.0 + jnp.exp(-v)))


def _pack_halves(y):
    w = y.shape[1] // 2
    lo = pltpu.bitcast(y[:, :w].astype(BF16).astype(F32), U32)
    hi = pltpu.bitcast(y[:, w:].astype(BF16).astype(F32), U32)
    return (hi & jnp.uint32(0xFFFF0000)) | (lo >> 16)


def _unpack_halves(p):
    lo = pltpu.bitcast(p << 16, F32)
    hi = pltpu.bitcast(p & jnp.uint32(0xFFFF0000), F32)
    return lo, hi


def _ada_kernel(ct_ref, w_ref, b_ref, o_ref):
    w = w_ref[...]
    rows = []
    for b in range(o_ref.shape[0]):
        if b < 2:
            cb = _silu(ct_ref[:, b:b + 1])
            rows.append(jnp.sum(cb * w, axis=0, keepdims=True) + b_ref[...])
        else:
            rows.append(jnp.zeros_like(b_ref[...]))
    o_ref[...] = jnp.concatenate(rows, axis=0)


def _ada_modulation(c, w_ada, b_ada):
    bsz, d = c.shape
    n = w_ada.shape[1]
    tn = 1024
    ct = jnp.zeros((d, LANES), F32).at[:, :bsz].set(c.T)
    out = pl.pallas_call(
        _ada_kernel,
        grid=(n // tn,),
        in_specs=[pl.BlockSpec((d, LANES), lambda j: (0, 0)),
                  pl.BlockSpec((d, tn), lambda j: (0, j)),
                  pl.BlockSpec((1, tn), lambda j: (0, j))],
        out_specs=pl.BlockSpec((8, tn), lambda j: (0, j)),
        out_shape=jax.ShapeDtypeStruct((8, n), F32),
        compiler_params=_params("arbitrary"),
    )(ct, w_ada, b_ada.reshape(1, n))
    return out[:bsz]


def _inproj_kernel(q_t, r_t, x_ref, nw_ref, sc_ref, sh_ref, wa_ref, wb_ref, wf_ref, cos_ref, sin_ref,
                   qw_ref, kw_ref, fb_ref, z_ref, f_ref, h_ref):
    j = pl.program_id(1)
    r0 = 3 * q_t

    @pl.when(j == 0)
    def _():
        x = x_ref[...]
        ms = jnp.mean(x * x, axis=-1, keepdims=True)
        y = x * lax.rsqrt(ms + EPS) * nw_ref[...]
        h = (y * (1.0 + sc_ref[0]) + sh_ref[0]).astype(BF16)
        h_ref[...] = h
        t = jnp.dot(h, wf_ref[...], preferred_element_type=F32) + fb_ref[...]
        f_ref[...] = jnp.minimum(t, 0.0) - jnp.log(1.0 + jnp.exp(-jnp.abs(t)))

    def heads_of(acc):
        return [acc[:, hh * HEAD_DIM:(hh + 1) * HEAD_DIM] for hh in range(acc.shape[1] // HEAD_DIM)]

    def head_norm(acc, w_row):
        outs = []
        for a in heads_of(acc):
            ms = jnp.mean(a * a, axis=-1, keepdims=True)
            outs.append(a * lax.rsqrt(ms + EPS) * w_row)
        return jnp.concatenate(outs, axis=-1).astype(BF16)

    def rotate(acc, scale):
        cs = cos_ref[...] * scale
        sn = sin_ref[...] * scale
        outs = [a * cs + pltpu.roll(a, HEAD_DIM // 2, 1) * sn for a in heads_of(acc)]
        return jnp.concatenate(outs, axis=-1).astype(BF16)

    def fox():
        return jnp.dot(h_ref[...], wa_ref[...], preferred_element_type=F32)

    def ret():
        return jnp.dot(h_ref[...], wb_ref[...], preferred_element_type=F32)

    @pl.when(j < q_t)
    def _():
        z_ref[...] = head_norm(fox(), qw_ref[...] * (LOG2E * HEAD_DIM ** -0.5))

    @pl.when((j >= q_t) & (j < 2 * q_t))
    def _():
        z_ref[...] = head_norm(fox(), kw_ref[...])

    @pl.when((j >= 2 * q_t) & (j < r0))
    def _():
        z_ref[...] = fox().astype(BF16)

    @pl.when((j >= r0) & (j < r0 + r_t))
    def _():
        z_ref[...] = rotate(ret(), 1.0)

    @pl.when((j >= r0 + r_t) & (j < r0 + 2 * r_t))
    def _():
        z_ref[...] = rotate(ret(), HEAD_DIM ** -0.5)

    @pl.when(j >= r0 + 2 * r_t)
    def _():
        z_ref[...] = ret().astype(BF16)


def _input_projection(x2d, seq, norm_w, sc1, sh1, w_fox, w_ret, w_f, cos_t, sin_t, qw, kw, fb):
    n, d = x2d.shape
    tm, tn = min(1024, seq), 512
    fox_tiles = w_fox.shape[1] // tn
    ret_tiles = w_ret.shape[1] // tn
    tiles_per_seq = seq // tm
    kern = functools.partial(_inproj_kernel, fox_tiles // 3, ret_tiles // 4)
    bsel = lambda i, j: (i // tiles_per_seq, 0, 0)
    const = lambda i, j: (0, 0)
    return pl.pallas_call(
        kern,
        grid=(n // tm, fox_tiles + ret_tiles),
        in_specs=[pl.BlockSpec((tm, d), lambda i, j: (i, 0)),
                  pl.BlockSpec((1, d), const),
                  pl.BlockSpec((1, 1, d), bsel),
                  pl.BlockSpec((1, 1, d), bsel),
                  pl.BlockSpec((d, tn), lambda i, j: (0, jnp.minimum(j, fox_tiles - 1))),
                  pl.BlockSpec((d, tn), lambda i, j: (0, jnp.maximum(j - fox_tiles, 0))),
                  pl.BlockSpec((d, LANES), const),
                  pl.BlockSpec((tm, HEAD_DIM), lambda i, j: (i % tiles_per_seq, 0)),
                  pl.BlockSpec((tm, HEAD_DIM), lambda i, j: (i % tiles_per_seq, 0)),
                  pl.BlockSpec((1, HEAD_DIM), const),
                  pl.BlockSpec((1, HEAD_DIM), const),
                  pl.BlockSpec((1, LANES), const)],
        out_specs=[pl.BlockSpec((tm, tn), lambda i, j: (i, j)),
                   pl.BlockSpec((tm, LANES), lambda i, j: (i, 0))],
        out_shape=[jax.ShapeDtypeStruct((n, w_fox.shape[1] + w_ret.shape[1]), BF16),
                   jax.ShapeDtypeStruct((n, LANES), F32)],
        scratch_shapes=[pltpu.VMEM((tm, d), BF16)],
        compiler_params=_params("arbitrary", "arbitrary"),
    )(x2d, norm_w, sc1, sh1, w_fox, w_ret, w_f, cos_t, sin_t, qw, kw, fb)


def _cumsum_kernel(x_ref, o_ref):
    x = x_ref[0]
    r = x.shape[0]
    a = lax.broadcasted_iota(jnp.int32, (LANES, LANES), 0)
    b = lax.broadcasted_iota(jnp.int32, (LANES, LANES), 1)
    upper = (a <= b).astype(F32)
    within = jnp.dot(x, upper, precision=lax.Precision.HIGHEST, preferred_element_type=F32)
    tot = jnp.broadcast_to(within[:, LANES - 1:LANES], (r, LANES))
    ra = lax.broadcasted_iota(jnp.int32, (r, r), 0)
    rb = lax.broadcasted_iota(jnp.int32, (r, r), 1)
    strict = (rb < ra).astype(F32)
    before = jnp.dot(strict, tot, precision=lax.Precision.HIGHEST, preferred_element_type=F32)
    o_ref[0] = within + before


def _cumsum_rows(x):
    g, s = x.shape
    r = s // LANES
    out = pl.pallas_call(
        _cumsum_kernel,
        grid=(g,),
        in_specs=[pl.BlockSpec((1, r, LANES), lambda i: (i, 0, 0))],
        out_specs=pl.BlockSpec((1, r, LANES), lambda i: (i, 0, 0)),
        out_shape=jax.ShapeDtypeStruct((g, r, LANES), F32),
        compiler_params=_params("arbitrary"),
    )(x.reshape(g, r, LANES))
    return out.reshape(g, 1, s)


def _fox_kernel(tq, q_ref, k_ref, v_ref, cum_ref, o_ref, sa_ref, sb_ref, m_ref, l_ref, acc_ref):
    qi = pl.program_id(2)
    q_start = pl.multiple_of(qi * tq, tq)
    c0 = cum_ref[0, :, pl.ds(q_start, LANES)][:, 0:1]

    m_ref[...] = jnp.full(m_ref.shape, NEG_BIG, F32)
    l_ref[...] = jnp.zeros(l_ref.shape, F32)
    acc_ref[...] = jnp.zeros(acc_ref.shape, F32)
    n_slabs = tq // LANES

    def scores(kb, s_ref):
        start = pl.multiple_of(kb * tq, tq)
        k = k_ref[pl.ds(start, tq), :]
        bias = (c0 - cum_ref[0, :, pl.ds(start, tq)]) * LOG2E
        s_ref[...] = lax.dot_general(q_ref[...], k, (((1,), (1,)), ((), ())),
                                     preferred_element_type=F32) + bias

    def softmax_pv(kb, s_ref, masked):
        start = pl.multiple_of(kb * tq, tq)
        v = v_ref[pl.ds(start, tq), :]
        slabs = []
        for j in range(n_slabs):
            t = s_ref[:, j * LANES:(j + 1) * LANES]
            if masked:
                row = lax.broadcasted_iota(jnp.int32, t.shape, 0)
                col = lax.broadcasted_iota(jnp.int32, t.shape, 1) + j * LANES
                t = jnp.where(col <= row, t, NEG_BIG)
            slabs.append(t)
        mx = slabs[0]
        for t in slabs[1:]:
            mx = jnp.maximum(mx, t)
        m_prev = m_ref[...]
        m_new = jnp.maximum(m_prev, jnp.max(mx, axis=-1, keepdims=True))
        alpha = jnp.exp2(m_prev - m_new)
        probs = [jnp.exp2(t - m_new) for t in slabs]
        psum = probs[0]
        for t in probs[1:]:
            psum = psum + t
        l_ref[...] = alpha * l_ref[...] + psum
        p = jnp.concatenate([t.astype(BF16) for t in probs], axis=-1)
        acc_ref[...] = alpha * acc_ref[...] + jnp.dot(p, v, preferred_element_type=F32)
        m_ref[...] = m_new

    scores(0, sa_ref)

    def pair(kb):
        scores(kb + 1, sb_ref)
        softmax_pv(kb, sa_ref, False)
        scores(kb + 2, sa_ref)
        softmax_pv(kb + 1, sb_ref, False)

    def body4(i, carry):
        pair(4 * i)
        pair(4 * i + 2)
        return carry

    def body2(i, carry):
        pair(2 * i)
        return carry

    n4 = qi // 4
    lax.fori_loop(0, n4, body4, 0)
    lax.fori_loop(2 * n4, qi // 2, body2, 0)

    @pl.when(qi % 2 == 0)
    def _():
        softmax_pv(qi, sa_ref, True)

    @pl.when(qi % 2 == 1)
    def _():
        scores(qi, sb_ref)
        softmax_pv(qi - 1, sa_ref, False)
        softmax_pv(qi, sb_ref, True)

    o_ref[...] = (acc_ref[...] / jnp.sum(l_ref[...], axis=-1, keepdims=True)).astype(BF16)


def _fox_attention(z, cum, bsz, seq, n_heads):
    tq = min(512, seq)
    nq = seq // tq
    kern = functools.partial(_fox_kernel, tq)
    return pl.pallas_call(
        kern,
        grid=(bsz, n_heads, nq),
        in_specs=[pl.BlockSpec((tq, HEAD_DIM), lambda b, h, i: (b * nq + i, h)),
                  pl.BlockSpec((seq, HEAD_DIM), lambda b, h, i: (b, n_heads + h)),
                  pl.BlockSpec((seq, HEAD_DIM), lambda b, h, i: (b, 2 * n_heads + h)),
                  pl.BlockSpec((1, 1, seq), lambda b, h, i: (b * n_heads + h, 0, 0))],
        out_specs=pl.BlockSpec((tq, HEAD_DIM), lambda b, h, i: (b * nq + i, h)),
        out_shape=jax.ShapeDtypeStruct((bsz * seq, n_heads * HEAD_DIM), BF16),
        scratch_shapes=[pltpu.VMEM((tq, tq), F32), pltpu.VMEM((tq, tq), F32),
                        pltpu.VMEM((tq, LANES), F32), pltpu.VMEM((tq, LANES), F32),
                        pltpu.VMEM((tq, HEAD_DIM), F32)],
        compiler_params=_params("arbitrary", "arbitrary", "arbitrary"),
    )(z, z, z, cum)


def _ret_kernel(chunk, n_heads, lg_ref, q_ref, k_ref, v_ref, g_ref, nw_ref, o_ref, state_ref, decay_ref):
    first = (pl.program_id(0) == 0) & (pl.program_id(1) == 0)

    @pl.when(first)
    def _():
        i = lax.broadcasted_iota(jnp.int32, (chunk, chunk), 0)
        jj = lax.broadcasted_iota(jnp.int32, (chunk, chunk), 1)
        diff = (i - jj).astype(F32)
        for h in range(n_heads):
            decay_ref[h] = jnp.where(diff >= 0, jnp.exp(lg_ref[h] * jnp.maximum(diff, 0.0)), 0.0)

    @pl.when(pl.program_id(1) == 0)
    def _():
        state_ref[...] = jnp.zeros(state_ref.shape, F32)

    pos = lax.broadcasted_iota(jnp.int32, (chunk, HEAD_DIM), 0).astype(F32)
    for h in range(n_heads):
        log_g = lg_ref[h]
        cols = slice(h * HEAD_DIM, (h + 1) * HEAD_DIM)
        q = q_ref[:, cols]
        k = k_ref[:, cols]
        v = v_ref[:, cols]
        scores = lax.dot_general(q, k, (((1,), (1,)), ((), ())), preferred_element_type=F32)
        scores = scores * decay_ref[h]
        intra = jnp.dot(scores.astype(BF16), v, preferred_element_type=F32)
        state = state_ref[h]
        inter = jnp.dot(q, state.astype(BF16), preferred_element_type=F32) * jnp.exp(log_g * (pos + 1.0))
        kd = (k.astype(F32) * jnp.exp(log_g * (chunk - 1.0 - pos))).astype(BF16)
        kv = lax.dot_general(kd, v, (((0,), (0,)), ((), ())), preferred_element_type=F32)
        state_ref[h] = state * jnp.exp(jnp.full((1, HEAD_DIM), chunk, F32) * log_g) + kv
        o = intra + inter
        ms = jnp.mean(o * o, axis=-1, keepdims=True)
        o = o * lax.rsqrt(ms + EPS) * nw_ref[:, cols]
        o_ref[:, cols] = (o * _silu(g_ref[:, cols].astype(F32))).astype(BF16)


def _retention(z, log_g, norm_w, bsz, seq, n_heads, col0):
    chunk = min(256, seq)
    nt = seq // chunk
    width = n_heads * HEAD_DIM
    c0 = col0 // width
    kern = functools.partial(_ret_kernel, chunk, n_heads)

    def sec(s):
        return pl.BlockSpec((chunk, width), lambda b, t, lg: (b * nt + t, c0 + s))

    grid_spec = pltpu.PrefetchScalarGridSpec(
        num_scalar_prefetch=1,
        grid=(bsz, nt),
        in_specs=[sec(0), sec(1), sec(2), sec(3), pl.BlockSpec((1, width), lambda b, t, lg: (0, 0))],
        out_specs=pl.BlockSpec((chunk, width), lambda b, t, lg: (b * nt + t, 0)),
        scratch_shapes=[pltpu.VMEM((n_heads, HEAD_DIM, HEAD_DIM), F32),
                        pltpu.VMEM((n_heads, chunk, chunk), F32)],
    )
    return pl.pallas_call(
        kern,
        grid_spec=grid_spec,
        out_shape=jax.ShapeDtypeStruct((bsz * seq, width), BF16),
        compiler_params=_params("arbitrary", "arbitrary"),
    )(log_g, z, z, z, z, norm_w)


def _outproj_kernel(oa_ref, ob_ref, wa_ref, wb_ref, x_ref, g1_ref, nw_ref, sc_ref, sh_ref, wr_ref, br_ref,
                    x1_ref, hp_ref, lg_ref):
    mix = jnp.dot(oa_ref[...], wa_ref[...], preferred_element_type=F32)
    mix = mix + jnp.dot(ob_ref[...], wb_ref[...], preferred_element_type=F32)
    x1 = x_ref[...] + g1_ref[0] * mix
    x1_ref[...] = x1
    ms = jnp.mean(x1 * x1, axis=-1, keepdims=True)
    h2 = x1 * lax.rsqrt(ms + EPS) * nw_ref[...] * (1.0 + sc_ref[0]) + sh_ref[0]
    hp_ref[...] = _pack_halves(h2)
    h_hi = h2.astype(BF16)
    h_lo = (h2 - h_hi.astype(F32)).astype(BF16)
    both = jnp.dot(h_hi, wr_ref[...], preferred_element_type=F32)
    cross = jnp.dot(h_lo, wr_ref[:, :LANES], preferred_element_type=F32)
    lg_ref[...] = both[:, :LANES] + both[:, LANES:] + cross + br_ref[...]


def _output_projection(o_a, o_b, w_out, x2d, seq, g1, norm_w, sc2, sh2, w_router, b_router):
    n, d = x2d.shape
    da = o_a.shape[1]
    tm = min(256, seq)
    tiles_per_seq = seq // tm
    bsel = lambda i: (i // tiles_per_seq, 0, 0)
    return pl.pallas_call(
        _outproj_kernel,
        grid=(n // tm,),
        in_specs=[pl.BlockSpec((tm, da), lambda i: (i, 0)),
                  pl.BlockSpec((tm, da), lambda i: (i, 0)),
                  pl.BlockSpec((da, d), lambda i: (0, 0)),
                  pl.BlockSpec((da, d), lambda i: (1, 0)),
                  pl.BlockSpec((tm, d), lambda i: (i, 0)),
                  pl.BlockSpec((1, 1, d), bsel),
                  pl.BlockSpec((1, d), lambda i: (0, 0)),
                  pl.BlockSpec((1, 1, d), bsel),
                  pl.BlockSpec((1, 1, d), bsel),
                  pl.BlockSpec((d, 2 * LANES), lambda i: (0, 0)),
                  pl.BlockSpec((1, LANES), lambda i: (0, 0))],
        out_specs=[pl.BlockSpec((tm, d), lambda i: (i, 0)),
                   pl.BlockSpec((tm, d // 2), lambda i: (i, 0)),
                   pl.BlockSpec((tm, LANES), lambda i: (i, 0))],
        out_shape=[jax.ShapeDtypeStruct((n, d), F32),
                   jax.ShapeDtypeStruct((n, d // 2), U32),
                   jax.ShapeDtypeStruct((n, LANES), F32)],
        compiler_params=_params("arbitrary"),
    )(o_a, o_b, w_out, w_out, x2d, g1, norm_w, sc2, sh2, w_router, b_router)


def _route_kernel(blk, n_blocks, lg_ref, gate_ref, ids_ref, plan_ref, run_ref):
    i = pl.program_id(0)

    @pl.when(i == 0)
    def _():
        run_ref[...] = jnp.zeros(run_ref.shape, F32)

    lg = lg_ref[...]
    tt = lg.shape[0]
    lane = lax.broadcasted_iota(jnp.int32, lg.shape, 1).astype(F32)
    big = 1e6

    def rmax(v):
        return jnp.max(v, axis=-1, keepdims=True)

    def rmin(v):
        return jnp.min(v, axis=-1, keepdims=True)

    def rsum(v):
        return jnp.sum(v, axis=-1, keepdims=True)

    cmask = lane < N_GROUPS
    cm = jnp.where(cmask, lg, NEG_BIG)
    ce = jnp.where(cmask, jnp.exp(cm - rmax(cm)), 0.0)
    pgrp = ce / rsum(ce)
    p_g = rmax(pgrp)
    g_sel = rmin(jnp.where(cmask & (pgrp == p_g), lane, big))

    lo = N_GROUPS + EXPERTS_PER_GROUP * g_sel
    fmask = (lane >= lo) & (lane < lo + EXPERTS_PER_GROUP)
    fm = jnp.where(fmask, lg, NEG_BIG)
    fe = jnp.where(fmask, jnp.exp(fm - rmax(fm)), 0.0)
    fp = fe / rsum(fe)
    fp = jnp.where(fmask, fp, -1.0)
    p1 = rmax(fp)
    i1 = rmin(jnp.where(fp == p1, lane, big))
    fp2 = jnp.where(lane == i1, -1.0, fp)
    p2 = rmax(fp2)
    i2 = rmin(jnp.where(fp2 == p2, lane, big))
    denom = p1 + p2
    w1 = p_g * p1 / denom
    w2 = p_g * p2 / denom
    e1 = i1 - N_GROUPS
    e2 = i2 - N_GROUPS

    gate_ref[...] = jnp.where(lane == 0, w1, jnp.where(lane == 1, w2, 0.0))

    oh1 = (lane == e1).astype(F32)
    oh2 = (lane == e2).astype(F32)
    both = oh1 + oh2
    ra = lax.broadcasted_iota(jnp.int32, (tt, tt), 0)
    rb = lax.broadcasted_iota(jnp.int32, (tt, tt), 1)
    strict = (rb < ra).astype(BF16)
    prefix = jnp.dot(strict, both.astype(BF16), preferred_element_type=F32) + run_ref[...]
    r1 = rsum(prefix * oh1)
    r2 = rsum(prefix * oh2)
    run_ref[...] = run_ref[...] + jnp.sum(both, axis=0, keepdims=True)

    packed = jnp.where(lane == 0, e1, jnp.where(lane == 1, e2, jnp.where(lane == 2, r1,
                                                                        jnp.where(lane == 3, r2, 0.0))))
    ids_ref[...] = jnp.transpose(packed)[:8, :].astype(jnp.int32)

    @pl.when(i == pl.num_programs(0) - 1)
    def _():
        cnt = jnp.broadcast_to(run_ref[...], (8, LANES))
        lane8 = lax.broadcasted_iota(jnp.int32, (8, LANES), 1)
        padded = jnp.floor((cnt + (blk - 1.0)) * (1.0 / blk)) * blk
        pend = padded
        for sh in (1, 2, 4, 8, 16, 32, 64):
            pend = pend + jnp.where(lane8 >= sh, pltpu.roll(pend, sh, 1), 0.0)
        pstart = pend - padded
        total = jnp.max(pend, axis=-1, keepdims=True)
        tail = total + (lane8 - N_EXPERTS).astype(F32) * blk
        fill = jnp.where(lane8 < N_EXPERTS, jnp.where(padded > 0, pend - blk, -1.0),
                         jnp.where((lane8 < 2 * N_EXPERTS) & (tail < n_blocks * blk), tail, -1.0))
        row8 = lax.broadcasted_iota(jnp.int32, (8, LANES), 0)
        plan_ref[...] = jnp.where(row8 == 0, pstart, jnp.where(row8 == 1, fill,
                                                               jnp.where(row8 == 2, cnt, 0.0))).astype(jnp.int32)


def _route(logits, blk, n_blocks):
    n = logits.shape[0]
    tt = min(512, n)
    blkspec = lambda: pl.BlockSpec((tt, LANES), lambda i: (i, 0))
    return pl.pallas_call(
        functools.partial(_route_kernel, blk, n_blocks),
        grid=(n // tt,),
        in_specs=[blkspec()],
        out_specs=[blkspec(),
                   pl.BlockSpec((8, tt), lambda i: (0, i)),
                   pl.BlockSpec((8, LANES), lambda i: (0, 0))],
        out_shape=[jax.ShapeDtypeStruct((n, LANES), F32),
                   jax.ShapeDtypeStruct((8, n), jnp.int32),
                   jax.ShapeDtypeStruct((8, LANES), jnp.int32)],
        scratch_shapes=[pltpu.VMEM((1, LANES), F32)],
        compiler_params=_params("arbitrary"),
    )(logits)


def _dispatch_kernel(tt, blk, n_fill, dest_ref, fill_ref, h_ref, xs_ref, zero_ref, sem, zsem):
    i = pl.program_id(0)
    base = i * (tt * TOP_K)

    @pl.when(i == 0)
    def _():
        zero_ref[...] = jnp.zeros(zero_ref.shape, U32)

        def zcopy(z):
            row = pl.multiple_of(jnp.maximum(fill_ref[z], 0), blk)
            return pltpu.make_async_copy(zero_ref, xs_ref.at[pl.ds(row, blk), :], zsem)

        def zissue(z, carry):
            @pl.when(fill_ref[z] >= 0)
            def _():
                zcopy(z).start()
            return carry

        def zdrain(z, carry):
            @pl.when(fill_ref[z] >= 0)
            def _():
                zcopy(z).wait()
            return carry

        lax.fori_loop(0, n_fill, zissue, 0)
        lax.fori_loop(0, n_fill, zdrain, 0)

    def copy(r, kk):
        d = dest_ref[base + r * TOP_K + kk]
        return pltpu.make_async_copy(h_ref.at[pl.ds(r, 1), :], xs_ref.at[pl.ds(d, 1), :], sem)

    def issue(r, carry):
        for kk in range(TOP_K):
            copy(r, kk).start()
        return carry

    lax.fori_loop(0, tt, issue, 0, unroll=8)
    for _ in range(TOP_K):
        pltpu.make_async_copy(h_ref, xs_ref.at[pl.ds(0, tt), :], sem).wait()


def _dispatch(h_packed, dest_flat, fill_rows, n_slots, blk):
    n, w = h_packed.shape
    tt = min(128, n)
    n_fill = fill_rows.shape[0]
    grid_spec = pltpu.PrefetchScalarGridSpec(
        num_scalar_prefetch=2,
        grid=(n // tt,),
        in_specs=[pl.BlockSpec((tt, w), lambda i, d, f: (i, 0))],
        out_specs=pl.BlockSpec(memory_space=pl.ANY),
        scratch_shapes=[pltpu.VMEM((blk, w), U32), pltpu.SemaphoreType.DMA(()), pltpu.SemaphoreType.DMA(())],
    )
    return pl.pallas_call(
        functools.partial(_dispatch_kernel, tt, blk, n_fill),
        grid_spec=grid_spec,
        out_shape=jax.ShapeDtypeStruct((n_slots, w), U32),
        compiler_params=_params("arbitrary"),
    )(dest_flat, fill_rows, h_packed)


def _expert_kernel(blk, ahead, cnt_ref, pstart_ref, fill_ref, xs_ref, w1_ref, w3_ref, w2_ref, y_ref,
                   w1f, w3f, w2f, w1b, w3b, w2b, xbuf, ybuf, w_sem, in_sem, out_sem):
    e = pl.program_id(0)
    n_exp = pl.num_programs(0)
    wslot = e % 2
    n_blk = (cnt_ref[e] + (blk - 1)) // blk
    base = pstart_ref[e]
    n_x = xbuf.shape[0]

    def weight_copies(ex, slot):
        return [pltpu.make_async_copy(src.at[ex], dst.at[slot], w_sem.at[slot])
                for src, dst in ((w1_ref, w1f), (w3_ref, w3f), (w2_ref, w2f))]

    def rows(b):
        return pl.ds(pl.multiple_of(base + b * blk, blk), blk)

    def in_copy(b, slot):
        return pltpu.make_async_copy(xs_ref.at[rows(b), :], xbuf.at[slot], in_sem.at[slot])

    def out_copy(b, slot):
        return pltpu.make_async_copy(ybuf.at[slot], y_ref.at[rows(b), :], out_sem.at[slot])

    @pl.when(e == 0)
    def _():
        for c in weight_copies(0, 0):
            c.start()

    for p in range(ahead):
        @pl.when(p < n_blk)
        def _():
            in_copy(p, p).start()

    @pl.when(e + 1 < n_exp)
    def _():
        for c in weight_copies(e + 1, 1 - wslot):
            c.start()

    for c in weight_copies(e, wslot):
        c.wait()
    w1b[...] = w1f[wslot].astype(BF16)
    w3b[...] = w3f[wslot].astype(BF16)
    w2b[...] = w2f[wslot].astype(BF16)

    def body(b, carry):
        slot = b % 2

        @pl.when(b + ahead < n_blk)
        def _():
            in_copy(b + ahead, (b + ahead) % n_x).start()

        in_copy(b, b % n_x).wait()

        @pl.when(b >= 2)
        def _():
            out_copy(b - 2, slot).wait()

        lo, hi = _unpack_halves(xbuf[b % n_x])
        lo = lo.astype(BF16)
        hi = hi.astype(BF16)
        half = lo.shape[1]
        a = jnp.dot(lo, w1b[:half, :], preferred_element_type=F32)
        a = a + jnp.dot(hi, w1b[half:, :], preferred_element_type=F32)
        g = jnp.dot(lo, w3b[:half, :], preferred_element_type=F32)
        g = g + jnp.dot(hi, w3b[half:, :], preferred_element_type=F32)
        mid = (_silu(a) * g).astype(BF16)
        ybuf[slot] = _pack_halves(jnp.dot(mid, w2b[...], preferred_element_type=F32))
        out_copy(b, slot).start(priority=1)
        return carry

    lax.fori_loop(0, n_blk, body, 0)

    @pl.when(n_blk >= 2)
    def _():
        out_copy(n_blk - 2, n_blk % 2).wait()

    @pl.when(n_blk >= 1)
    def _():
        out_copy(n_blk - 1, (n_blk - 1) % 2).wait()

    @pl.when(e == pl.num_programs(0) - 1)
    def _():
        ybuf[0] = jnp.zeros(ybuf.shape[1:], U32)

        def zcopy(t):
            row = pl.multiple_of(jnp.maximum(fill_ref[N_EXPERTS + t], 0), blk)
            return pltpu.make_async_copy(ybuf.at[0], y_ref.at[pl.ds(row, blk), :], out_sem.at[0])

        def zissue(t, carry):
            @pl.when(fill_ref[N_EXPERTS + t] >= 0)
            def _():
                zcopy(t).start()
            return carry

        def zdrain(t, carry):
            @pl.when(fill_ref[N_EXPERTS + t] >= 0)
            def _():
                zcopy(t).wait()
            return carry

        lax.fori_loop(0, N_EXPERTS, zissue, 0)
        lax.fori_loop(0, N_EXPERTS, zdrain, 0)


def _expert_blocks(xs, counts, pstart, fill_rows, w1, w3, w2, blk):
    n_slots, w = xs.shape
    n_exp, d, de = w1.shape
    ahead = 3
    hbm = pl.BlockSpec(memory_space=pl.ANY)
    grid_spec = pltpu.PrefetchScalarGridSpec(
        num_scalar_prefetch=3,
        grid=(n_exp,),
        in_specs=[hbm, hbm, hbm, hbm],
        out_specs=hbm,
        scratch_shapes=[pltpu.VMEM((2, d, de), F32), pltpu.VMEM((2, d, de), F32), pltpu.VMEM((2, de, d), F32),
                        pltpu.VMEM((d, de), BF16), pltpu.VMEM((d, de), BF16), pltpu.VMEM((de, d), BF16),
                        pltpu.VMEM((ahead + 1, blk, w), U32), pltpu.VMEM((2, blk, w), U32),
                        pltpu.SemaphoreType.DMA((2,)), pltpu.SemaphoreType.DMA((ahead + 1,)),
                        pltpu.SemaphoreType.DMA((2,))],
    )
    return pl.pallas_call(
        functools.partial(_expert_kernel, blk, ahead),
        grid_spec=grid_spec,
        out_shape=jax.ShapeDtypeStruct((n_slots, w), U32),
        compiler_params=_params("arbitrary"),
    )(counts, pstart, fill_rows, xs, w1, w3, w2)


def _combine_kernel(tt, n_tiles, dest_ref, x1_ref, g2_ref, gate_ref, yb_ref, o_ref, buf, sems):
    i = pl.program_id(0)

    def copy(tile, slot, r, kk):
        d = dest_ref[(tile * tt + r) * TOP_K + kk]
        return pltpu.make_async_copy(yb_ref.at[pl.ds(d, 1), :], buf.at[slot, kk, pl.ds(r, 1), :], sems.at[slot])

    def issue_tile(tile, slot):
        def body(r, carry):
            for kk in range(TOP_K):
                copy(tile, slot, r, kk).start()
            return carry
        lax.fori_loop(0, tt, body, 0, unroll=8)

    def wait_tile(tile, slot):
        for kk in range(TOP_K):
            pltpu.make_async_copy(yb_ref.at[pl.ds(0, tt), :], buf.at[slot, kk], sems.at[slot]).wait()

    slot = i % 2

    @pl.when(i == 0)
    def _():
        issue_tile(0, 0)

    @pl.when(i + 1 < n_tiles)
    def _():
        issue_tile(i + 1, 1 - slot)

    wait_tile(i, slot)

    gate = gate_ref[...]
    wa = gate[:, 0:1]
    wb = gate[:, 1:2]
    lo_a, hi_a = _unpack_halves(buf[slot, 0])
    lo_b, hi_b = _unpack_halves(buf[slot, 1])
    y = jnp.concatenate([wa * lo_a + wb * lo_b, wa * hi_a + wb * hi_b], axis=-1)
    o_ref[...] = x1_ref[...] + g2_ref[0] * y


def _combine(x1, seq, g2, gates, dest_flat, yb):
    n, d = x1.shape
    w = yb.shape[1]
    tt = min(128, seq)
    n_tiles = n // tt
    tiles_per_seq = seq // tt
    grid_spec = pltpu.PrefetchScalarGridSpec(
        num_scalar_prefetch=1,
        grid=(n_tiles,),
        in_specs=[pl.BlockSpec((tt, d), lambda i, dr: (i, 0)),
                  pl.BlockSpec((1, 1, d), lambda i, dr: (i // tiles_per_seq, 0, 0)),
                  pl.BlockSpec((tt, LANES), lambda i, dr: (i, 0)),
                  pl.BlockSpec(memory_space=pl.ANY)],
        out_specs=pl.BlockSpec((tt, d), lambda i, dr: (i, 0)),
        scratch_shapes=[pltpu.VMEM((2, TOP_K, tt, w), U32), pltpu.SemaphoreType.DMA((2,))],
    )
    return pl.pallas_call(
        functools.partial(_combine_kernel, tt, n_tiles),
        grid_spec=grid_spec,
        out_shape=jax.ShapeDtypeStruct((n, d), F32),
        compiler_params=_params("arbitrary"),
    )(dest_flat, x1, g2, gates, yb)


def _rotation_tables(seq):
    half = HEAD_DIM // 2
    theta = ROPE_BASE ** (-np.arange(half, dtype=np.float64) / half)
    ang = np.arange(seq, dtype=np.float64)[:, None] * theta[None, :]
    cos_t = np.concatenate([np.cos(ang), np.cos(ang)], axis=-1).astype(np.float32)
    sin_t = np.concatenate([-np.sin(ang), np.sin(ang)], axis=-1).astype(np.float32)
    return jnp.asarray(cos_t), jnp.asarray(sin_t)


def _layer(x, c, w_ada, b_ada, norm1_w, w_in, forget_bias, q_norm_w, k_norm_w, ret_norm_w, w_out, norm2_w,
           w_coarse, b_coarse, w_fine, b_fine, w1, w3, w2):
    bsz, seq, d = x.shape
    n = bsz * seq
    d_fox = d // 2
    d_ret = d // 2
    n_heads = d_fox // HEAD_DIM

    mod = _ada_modulation(c, w_ada, b_ada)
    sh1, sc1, g1, sh2, sc2, g2 = [m.reshape(bsz, 1, d) for m in jnp.split(mod, 6, axis=-1)]

    f0 = 3 * d_fox
    w_fox = w_in[:, :f0].astype(BF16)
    w_ret = w_in[:, f0 + n_heads:].astype(BF16)
    w_f = jnp.zeros((d, LANES), BF16).at[:, :n_heads].set(w_in[:, f0:f0 + n_heads].astype(BF16))
    fb = jnp.zeros((1, LANES), F32).at[0, :n_heads].set(forget_bias)

    cos_t, sin_t = _rotation_tables(seq)

    x2d = x.reshape(n, d)
    z, log_f = _input_projection(x2d, seq, norm1_w.reshape(1, d), sc1, sh1, w_fox, w_ret, w_f, cos_t, sin_t,
                                 q_norm_w.reshape(1, HEAD_DIM), k_norm_w.reshape(1, HEAD_DIM), fb)

    lf = log_f[:, :n_heads].reshape(bsz, seq, n_heads).transpose(0, 2, 1).reshape(bsz * n_heads, seq)
    cum = _cumsum_rows(lf)

    o_a = _fox_attention(z, cum, bsz, seq, n_heads)
    log_g = jnp.log(1.0 - 2.0 ** (-5.0 - jnp.arange(n_heads, dtype=F32)))
    o_b = _retention(z, log_g, ret_norm_w.reshape(1, d_ret), bsz, seq, n_heads, 3 * d_fox)

    w_router = jnp.zeros((d, LANES), F32)
    w_router = w_router.at[:, :N_GROUPS].set(w_coarse)
    w_router = w_router.at[:, N_GROUPS:N_GROUPS + N_EXPERTS].set(
        w_fine.transpose(1, 0, 2).reshape(d, N_EXPERTS))
    b_router = jnp.zeros((1, LANES), F32)
    b_router = b_router.at[0, :N_GROUPS].set(b_coarse)
    b_router = b_router.at[0, N_GROUPS:N_GROUPS + N_EXPERTS].set(b_fine.reshape(N_EXPERTS))

    wr_hi = w_router.astype(BF16)
    wr_lo = (w_router - wr_hi.astype(F32)).astype(BF16)
    x1, h_packed, logits = _output_projection(o_a, o_b, w_out.astype(BF16), x2d, seq, g1,
                                              norm2_w.reshape(1, d), sc2, sh2,
                                              jnp.concatenate([wr_hi, wr_lo], axis=1), b_router)

    blk = 256
    nk = n * TOP_K
    n_blocks = nk // blk + N_EXPERTS
    gates, ids, plan = _route(logits, blk, n_blocks)
    pstart = plan[0, :N_EXPERTS]
    fill_rows = plan[1, :2 * N_EXPERTS]
    counts = plan[2, :N_EXPERTS]
    eid = ids[0:TOP_K]
    hit = eid[None] == jnp.arange(N_EXPERTS, dtype=jnp.int32)[:, None, None]
    dest = (jnp.sum(jnp.where(hit, pstart[:, None, None], 0), axis=0) + ids[TOP_K:2 * TOP_K]).T.reshape(nk)

    xs = _dispatch(h_packed, dest, fill_rows, n_blocks * blk, blk)
    yb = _expert_blocks(xs, counts, pstart, fill_rows, w1, w3, w2, blk)
    out = _combine(x1, seq, g2, gates, dest, yb)
    return out.reshape(bsz, seq, d)


def kernel(x, c, w_ada, b_ada, norm1_w, w_in, forget_bias, q_norm_w, k_norm_w, ret_norm_w, w_out, norm2_w,
           w_coarse, b_coarse, w_fine, b_fine, w1, w3, w2):
    c_in = c
    for l in range(w_ada.shape[0]):
        x = _layer(x, c_in, w_ada[l], b_ada[l], norm1_w[l], w_in[l], forget_bias[l], q_norm_w[l],
                   k_norm_w[l], ret_norm_w[l], w_out[l], norm2_w[l], w_coarse[l], b_coarse[l],
                   w_fine[l], b_fine[l], w1[l], w3[l], w2[l])
    return x
```

```python
import functools

import jax
import jax.numpy as jnp
import numpy as np
from jax import lax
from jax.experimental import pallas as pl
from jax.experimental.pallas import tpu as pltpu

HEAD_DIM = 128
N_GROUPS = 4
EXPERTS_PER_GROUP = 8
N_EXPERTS = N_GROUPS * EXPERTS_PER_GROUP
TOP_K = 2
ROPE_BASE = 10000.0
EPS = 1e-6

LANES = 128
VMEM_LIMIT = 56 * 1024 * 1024
NEG_BIG = -1e30
LOG2E = 1.4426950408889634

F32 = jnp.float32
BF16 = jnp.bfloat16
U32 = jnp.uint32


def _params(*sem):
    return pltpu.CompilerParams(dimension_semantics=sem, vmem_limit_bytes=VMEM_LIMIT)


def _silu(v):
    return v * (1.0 / (1.0 + jnp.exp(-v)))


def _pack_halves(y):
    w = y.shape[1] // 2
    lo = pltpu.bitcast(y[:, :w].astype(BF16).astype(F32), U32)
    hi = pltpu.bitcast(y[:, w:].astype(BF16).astype(F32), U32)
    return (hi & jnp.uint32(0xFFFF0000)) | (lo >> 16)


def _rows_to_tiles(p):
    return pltpu.einshape("m(ck)->mck", p, c=8, k=LANES)


def _tiles_to_rows(t):
    return pltpu.einshape("mck->m(ck)", t)


def _unpack_halves(p):
    lo = pltpu.bitcast(p << 16, F32)
    hi = pltpu.bitcast(p & jnp.uint32(0xFFFF0000), F32)
    return lo, hi


def _ada_kernel(ct_ref, w_ref, b_ref, o_ref):
    w = w_ref[...]
    rows = []
    for b in range(o_ref.shape[0]):
        if b < 2:
            cb = _silu(ct_ref[:, b:b + 1])
            rows.append(jnp.sum(cb * w, axis=0, keepdims=True) + b_ref[...])
        else:
            rows.append(jnp.zeros_like(b_ref[...]))
    o_ref[...] = jnp.concatenate(rows, axis=0)


def _ada_modulation(c, w_ada, b_ada):
    bsz, d = c.shape
    n = w_ada.shape[1]
    tn = 1024
    ct = jnp.zeros((d, LANES), F32).at[:, :bsz].set(c.T)
    out = pl.pallas_call(
        _ada_kernel,
        grid=(n // tn,),
        in_specs=[pl.BlockSpec((d, LANES), lambda j: (0, 0)),
                  pl.BlockSpec((d, tn), lambda j: (0, j)),
                  pl.BlockSpec((1, tn), lambda j: (0, j))],
        out_specs=pl.BlockSpec((8, tn), lambda j: (0, j)),
        out_shape=jax.ShapeDtypeStruct((8, n), F32),
        compiler_params=_params("arbitrary"),
    )(ct, w_ada, b_ada.reshape(1, n))
    return out[:bsz]


def _inproj_kernel(q_t, r_t, x_ref, nw_ref, sc_ref, sh_ref, wa_ref, wb_ref, wf_ref, cos_ref, sin_ref,
                   qw_ref, kw_ref, fb_ref, z_ref, f_ref, h_ref):
    j = pl.program_id(1)
    r0 = 3 * q_t

    @pl.when(j == 0)
    def _():
        x = x_ref[...]
        ms = jnp.mean(x * x, axis=-1, keepdims=True)
        y = x * lax.rsqrt(ms + EPS) * nw_ref[...]
        h = (y * (1.0 + sc_ref[0]) + sh_ref[0]).astype(BF16)
        h_ref[...] = h
        t = jnp.dot(h, wf_ref[...], preferred_element_type=F32) + fb_ref[...]
        f_ref[...] = jnp.minimum(t, 0.0) - jnp.log(1.0 + jnp.exp(-jnp.abs(t)))

    def heads_of(acc):
        return [acc[:, hh * HEAD_DIM:(hh + 1) * HEAD_DIM] for hh in range(acc.shape[1] // HEAD_DIM)]

    def head_norm(acc, w_row):
        outs = []
        for a in heads_of(acc):
            ms = jnp.mean(a * a, axis=-1, keepdims=True)
            outs.append(a * lax.rsqrt(ms + EPS) * w_row)
        return jnp.concatenate(outs, axis=-1).astype(BF16)

    def rotate(acc, scale):
        cs = cos_ref[...] * scale
        sn = sin_ref[...] * scale
        outs = [a * cs + pltpu.roll(a, HEAD_DIM // 2, 1) * sn for a in heads_of(acc)]
        return jnp.concatenate(outs, axis=-1).astype(BF16)

    def fox():
        return jnp.dot(h_ref[...], wa_ref[...], preferred_element_type=F32)

    def ret():
        return jnp.dot(h_ref[...], wb_ref[...], preferred_element_type=F32)

    @pl.when(j < q_t)
    def _():
        z_ref[...] = head_norm(fox(), qw_ref[...] * (LOG2E * HEAD_DIM ** -0.5))

    @pl.when((j >= q_t) & (j < 2 * q_t))
    def _():
        z_ref[...] = head_norm(fox(), kw_ref[...])

    @pl.when((j >= 2 * q_t) & (j < r0))
    def _():
        z_ref[...] = fox().astype(BF16)

    @pl.when((j >= r0) & (j < r0 + r_t))
    def _():
        z_ref[...] = rotate(ret(), 1.0)

    @pl.when((j >= r0 + r_t) & (j < r0 + 2 * r_t))
    def _():
        z_ref[...] = rotate(ret(), HEAD_DIM ** -0.5)

    @pl.when(j >= r0 + 2 * r_t)
    def _():
        z_ref[...] = ret().astype(BF16)


def _input_projection(x2d, seq, norm_w, sc1, sh1, w_fox, w_ret, w_f, cos_t, sin_t, qw, kw, fb):
    n, d = x2d.shape
    tm, tn = min(1024, seq), 1024
    fox_tiles = w_fox.shape[1] // tn
    ret_tiles = w_ret.shape[1] // tn
    tiles_per_seq = seq // tm
    kern = functools.partial(_inproj_kernel, fox_tiles // 3, ret_tiles // 4)
    bsel = lambda i, j: (i // tiles_per_seq, 0, 0)
    const = lambda i, j: (0, 0)
    return pl.pallas_call(
        kern,
        grid=(n // tm, fox_tiles + ret_tiles),
        in_specs=[pl.BlockSpec((tm, d), lambda i, j: (i, 0)),
                  pl.BlockSpec((1, d), const),
                  pl.BlockSpec((1, 1, d), bsel),
                  pl.BlockSpec((1, 1, d), bsel),
                  pl.BlockSpec((d, tn), lambda i, j: (0, jnp.minimum(j, fox_tiles - 1))),
                  pl.BlockSpec((d, tn), lambda i, j: (0, jnp.maximum(j - fox_tiles, 0))),
                  pl.BlockSpec((d, LANES), const),
                  pl.BlockSpec((tm, HEAD_DIM), lambda i, j: (i % tiles_per_seq, 0)),
                  pl.BlockSpec((tm, HEAD_DIM), lambda i, j: (i % tiles_per_seq, 0)),
                  pl.BlockSpec((1, HEAD_DIM), const),
                  pl.BlockSpec((1, HEAD_DIM), const),
                  pl.BlockSpec((1, LANES), const)],
        out_specs=[pl.BlockSpec((tm, tn), lambda i, j: (i, j)),
                   pl.BlockSpec((tm, LANES), lambda i, j: (i, 0))],
        out_shape=[jax.ShapeDtypeStruct((n, w_fox.shape[1] + w_ret.shape[1]), BF16),
                   jax.ShapeDtypeStruct((n, LANES), F32)],
        scratch_shapes=[pltpu.VMEM((tm, d), BF16)],
        compiler_params=_params("arbitrary", "arbitrary"),
    )(x2d, norm_w, sc1, sh1, w_fox, w_ret, w_f, cos_t, sin_t, qw, kw, fb)


def _cumsum_kernel(x_ref, o_ref):
    x = x_ref[0]
    r = x.shape[0]
    a = lax.broadcasted_iota(jnp.int32, (LANES, LANES), 0)
    b = lax.broadcasted_iota(jnp.int32, (LANES, LANES), 1)
    upper = (a <= b).astype(F32)
    within = jnp.dot(x, upper, precision=lax.Precision.HIGHEST, preferred_element_type=F32)
    tot = jnp.broadcast_to(within[:, LANES - 1:LANES], (r, LANES))
    ra = lax.broadcasted_iota(jnp.int32, (r, r), 0)
    rb = lax.broadcasted_iota(jnp.int32, (r, r), 1)
    strict = (rb < ra).astype(F32)
    before = jnp.dot(strict, tot, precision=lax.Precision.HIGHEST, preferred_element_type=F32)
    o_ref[0] = within + before


def _cumsum_rows(x):
    g, s = x.shape
    r = s // LANES
    out = pl.pallas_call(
        _cumsum_kernel,
        grid=(g,),
        in_specs=[pl.BlockSpec((1, r, LANES), lambda i: (i, 0, 0))],
        out_specs=pl.BlockSpec((1, r, LANES), lambda i: (i, 0, 0)),
        out_shape=jax.ShapeDtypeStruct((g, r, LANES), F32),
        compiler_params=_params("arbitrary"),
    )(x.reshape(g, r, LANES))
    return out.reshape(g, 1, s)


def _fox_kernel(tq, q_ref, k_ref, v_ref, cum_ref, o_ref, sa_ref, sb_ref, m_ref, l_ref, acc_ref):
    qi = pl.program_id(2)
    q_start = pl.multiple_of(qi * tq, tq)
    c0 = cum_ref[0, :, pl.ds(q_start, LANES)][:, 0:1]

    m_ref[...] = jnp.full(m_ref.shape, NEG_BIG, F32)
    l_ref[...] = jnp.zeros(l_ref.shape, F32)
    acc_ref[...] = jnp.zeros(acc_ref.shape, F32)
    n_slabs = tq // LANES

    def scores(kb, s_ref):
        start = pl.multiple_of(kb * tq, tq)
        k = k_ref[pl.ds(start, tq), :]
        bias = (c0 - cum_ref[0, :, pl.ds(start, tq)]) * LOG2E
        s_ref[...] = lax.dot_general(q_ref[...], k, (((1,), (1,)), ((), ())),
                                     preferred_element_type=F32) + bias

    def softmax_pv(kb, s_ref, masked):
        start = pl.multiple_of(kb * tq, tq)
        v = v_ref[pl.ds(start, tq), :]
        slabs = []
        for j in range(n_slabs):
            t = s_ref[:, j * LANES:(j + 1) * LANES]
            if masked:
                row = lax.broadcasted_iota(jnp.int32, t.shape, 0)
                col = lax.broadcasted_iota(jnp.int32, t.shape, 1) + j * LANES
                t = jnp.where(col <= row, t, NEG_BIG)
            slabs.append(t)
        mx = slabs[0]
        for t in slabs[1:]:
            mx = jnp.maximum(mx, t)
        m_prev = m_ref[...]
        m_new = jnp.maximum(m_prev, jnp.max(mx, axis=-1, keepdims=True))
        alpha = jnp.exp2(m_prev - m_new)
        probs = [jnp.exp2(t - m_new) for t in slabs]
        psum = probs[0]
        for t in probs[1:]:
            psum = psum + t
        l_ref[...] = alpha * l_ref[...] + psum
        p = jnp.concatenate([t.astype(BF16) for t in probs], axis=-1)
        acc_ref[...] = alpha * acc_ref[...] + jnp.dot(p, v, preferred_element_type=F32)
        m_ref[...] = m_new

    scores(0, sa_ref)

    def pair(kb):
        scores(kb + 1, sb_ref)
        softmax_pv(kb, sa_ref, False)
        scores(kb + 2, sa_ref)
        softmax_pv(kb + 1, sb_ref, False)

    def body4(i, carry):
        pair(4 * i)
        pair(4 * i + 2)
        return carry

    def body2(i, carry):
        pair(2 * i)
        return carry

    n4 = qi // 4
    lax.fori_loop(0, n4, body4, 0)
    lax.fori_loop(2 * n4, qi // 2, body2, 0)

    @pl.when(qi % 2 == 0)
    def _():
        softmax_pv(qi, sa_ref, True)

    @pl.when(qi % 2 == 1)
    def _():
        scores(qi, sb_ref)
        softmax_pv(qi - 1, sa_ref, False)
        softmax_pv(qi, sb_ref, True)

    o_ref[...] = (acc_ref[...] / jnp.sum(l_ref[...], axis=-1, keepdims=True)).astype(BF16)


def _fox_attention(z, cum, bsz, seq, n_heads):
    tq = min(512, seq)
    nq = seq // tq
    kern = functools.partial(_fox_kernel, tq)
    return pl.pallas_call(
        kern,
        grid=(bsz, n_heads, nq),
        in_specs=[pl.BlockSpec((tq, HEAD_DIM), lambda b, h, i: (b * nq + i, h)),
                  pl.BlockSpec((seq, HEAD_DIM), lambda b, h, i: (b, n_heads + h)),
                  pl.BlockSpec((seq, HEAD_DIM), lambda b, h, i: (b, 2 * n_heads + h)),
                  pl.BlockSpec((1, 1, seq), lambda b, h, i: (b * n_heads + h, 0, 0))],
        out_specs=pl.BlockSpec((tq, HEAD_DIM), lambda b, h, i: (b * nq + i, h)),
        out_shape=jax.ShapeDtypeStruct((bsz * seq, n_heads * HEAD_DIM), BF16),
        scratch_shapes=[pltpu.VMEM((tq, tq), F32), pltpu.VMEM((tq, tq), F32),
                        pltpu.VMEM((tq, LANES), F32), pltpu.VMEM((tq, LANES), F32),
                        pltpu.VMEM((tq, HEAD_DIM), F32)],
        compiler_params=_params("arbitrary", "arbitrary", "arbitrary"),
    )(z, z, z, cum)


def _ret_kernel(chunk, n_heads, lg_ref, q_ref, k_ref, v_ref, g_ref, nw_ref, o_ref, state_ref, decay_ref):
    first = (pl.program_id(0) == 0) & (pl.program_id(1) == 0)

    @pl.when(first)
    def _():
        i = lax.broadcasted_iota(jnp.int32, (chunk, chunk), 0)
        jj = lax.broadcasted_iota(jnp.int32, (chunk, chunk), 1)
        diff = (i - jj).astype(F32)
        for h in range(n_heads):
            decay_ref[h] = jnp.where(diff >= 0, jnp.exp(lg_ref[h] * jnp.maximum(diff, 0.0)), 0.0)

    @pl.when(pl.program_id(1) == 0)
    def _():
        state_ref[...] = jnp.zeros(state_ref.shape, F32)

    pos = lax.broadcasted_iota(jnp.int32, (chunk, HEAD_DIM), 0).astype(F32)
    for h in range(n_heads):
        log_g = lg_ref[h]
        cols = slice(h * HEAD_DIM, (h + 1) * HEAD_DIM)
        q = q_ref[:, cols]
        k = k_ref[:, cols]
        v = v_ref[:, cols]
        scores = lax.dot_general(q, k, (((1,), (1,)), ((), ())), preferred_element_type=F32)
        scores = scores * decay_ref[h]
        intra = jnp.dot(scores.astype(BF16), v, preferred_element_type=F32)
        state = state_ref[h]
        inter = jnp.dot(q, state.astype(BF16), preferred_element_type=F32) * jnp.exp(log_g * (pos + 1.0))
        kd = (k.astype(F32) * jnp.exp(log_g * (chunk - 1.0 - pos))).astype(BF16)
        kv = lax.dot_general(kd, v, (((0,), (0,)), ((), ())), preferred_element_type=F32)
        state_ref[h] = state * jnp.exp(jnp.full((1, HEAD_DIM), chunk, F32) * log_g) + kv
        o = intra + inter
        ms = jnp.mean(o * o, axis=-1, keepdims=True)
        o = o * lax.rsqrt(ms + EPS) * nw_ref[:, cols]
        o_ref[:, cols] = (o * _silu(g_ref[:, cols].astype(F32))).astype(BF16)


def _retention(z, log_g, norm_w, bsz, seq, n_heads, col0):
    chunk = min(256, seq)
    nt = seq // chunk
    width = n_heads * HEAD_DIM
    c0 = col0 // width
    kern = functools.partial(_ret_kernel, chunk, n_heads)

    def sec(s):
        return pl.BlockSpec((chunk, width), lambda b, t, lg: (b * nt + t, c0 + s))

    grid_spec = pltpu.PrefetchScalarGridSpec(
        num_scalar_prefetch=1,
        grid=(bsz, nt),
        in_specs=[sec(0), sec(1), sec(2), sec(3), pl.BlockSpec((1, width), lambda b, t, lg: (0, 0))],
        out_specs=pl.BlockSpec((chunk, width), lambda b, t, lg: (b * nt + t, 0)),
        scratch_shapes=[pltpu.VMEM((n_heads, HEAD_DIM, HEAD_DIM), F32),
                        pltpu.VMEM((n_heads, chunk, chunk), F32)],
    )
    return pl.pallas_call(
        kern,
        grid_spec=grid_spec,
        out_shape=jax.ShapeDtypeStruct((bsz * seq, width), BF16),
        compiler_params=_params("arbitrary", "arbitrary"),
    )(log_g, z, z, z, z, norm_w)


def _outproj_kernel(oa_ref, ob_ref, wa_ref, wb_ref, x_ref, g1_ref, nw_ref, sc_ref, sh_ref, wr_ref, br_ref,
                    x1_ref, hp_ref, lg_ref):
    mix = jnp.dot(oa_ref[...], wa_ref[...], preferred_element_type=F32)
    mix = mix + jnp.dot(ob_ref[...], wb_ref[...], preferred_element_type=F32)
    x1 = x_ref[...] + g1_ref[0] * mix
    x1_ref[...] = x1
    ms = jnp.mean(x1 * x1, axis=-1, keepdims=True)
    h2 = x1 * lax.rsqrt(ms + EPS) * nw_ref[...] * (1.0 + sc_ref[0]) + sh_ref[0]
    hp_ref[...] = _rows_to_tiles(_pack_halves(h2))
    h_hi = h2.astype(BF16)
    h_lo = (h2 - h_hi.astype(F32)).astype(BF16)
    both = jnp.dot(h_hi, wr_ref[...], preferred_element_type=F32)
    cross = jnp.dot(h_lo, wr_ref[:, :LANES], preferred_element_type=F32)
    lg_ref[...] = both[:, :LANES] + both[:, LANES:] + cross + br_ref[...]


def _output_projection(o_a, o_b, w_out, x2d, seq, g1, norm_w, sc2, sh2, w_router, b_router):
    n, d = x2d.shape
    da = o_a.shape[1]
    tm = min(256, seq)
    tiles_per_seq = seq // tm
    bsel = lambda i: (i // tiles_per_seq, 0, 0)
    return pl.pallas_call(
        _outproj_kernel,
        grid=(n // tm,),
        in_specs=[pl.BlockSpec((tm, da), lambda i: (i, 0)),
                  pl.BlockSpec((tm, da), lambda i: (i, 0)),
                  pl.BlockSpec((da, d), lambda i: (0, 0)),
                  pl.BlockSpec((da, d), lambda i: (1, 0)),
                  pl.BlockSpec((tm, d), lambda i: (i, 0)),
                  pl.BlockSpec((1, 1, d), bsel),
                  pl.BlockSpec((1, d), lambda i: (0, 0)),
                  pl.BlockSpec((1, 1, d), bsel),
                  pl.BlockSpec((1, 1, d), bsel),
                  pl.BlockSpec((d, 2 * LANES), lambda i: (0, 0)),
                  pl.BlockSpec((1, LANES), lambda i: (0, 0))],
        out_specs=[pl.BlockSpec((tm, d), lambda i: (i, 0)),
                   pl.BlockSpec((tm, d // 2 // LANES, LANES), lambda i: (i, 0, 0)),
                   pl.BlockSpec((tm, LANES), lambda i: (i, 0))],
        out_shape=[jax.ShapeDtypeStruct((n, d), F32),
                   jax.ShapeDtypeStruct((n, d // 2 // LANES, LANES), U32),
                   jax.ShapeDtypeStruct((n, LANES), F32)],
        compiler_params=_params("arbitrary"),
    )(o_a, o_b, w_out, w_out, x2d, g1, norm_w, sc2, sh2, w_router, b_router)


def _route_kernel(blk, n_blocks, lg_ref, gate_ref, ids_ref, plan_ref, run_ref):
    i = pl.program_id(0)

    @pl.when(i == 0)
    def _():
        run_ref[...] = jnp.zeros(run_ref.shape, F32)

    lg = lg_ref[...]
    tt = lg.shape[0]
    lane = lax.broadcasted_iota(jnp.int32, lg.shape, 1).astype(F32)
    big = 1e6

    def rmax(v):
        return jnp.max(v, axis=-1, keepdims=True)

    def rmin(v):
        return jnp.min(v, axis=-1, keepdims=True)

    def rsum(v):
        return jnp.sum(v, axis=-1, keepdims=True)

    cmask = lane < N_GROUPS
    cm = jnp.where(cmask, lg, NEG_BIG)
    ce = jnp.where(cmask, jnp.exp(cm - rmax(cm)), 0.0)
    pgrp = ce / rsum(ce)
    p_g = rmax(pgrp)
    g_sel = rmin(jnp.where(cmask & (pgrp == p_g), lane, big))

    lo = N_GROUPS + EXPERTS_PER_GROUP * g_sel
    fmask = (lane >= lo) & (lane < lo + EXPERTS_PER_GROUP)
    fm = jnp.where(fmask, lg, NEG_BIG)
    fe = jnp.where(fmask, jnp.exp(fm - rmax(fm)), 0.0)
    fp = fe / rsum(fe)
    fp = jnp.where(fmask, fp, -1.0)
    p1 = rmax(fp)
    i1 = rmin(jnp.where(fp == p1, lane, big))
    fp2 = jnp.where(lane == i1, -1.0, fp)
    p2 = rmax(fp2)
    i2 = rmin(jnp.where(fp2 == p2, lane, big))
    denom = p1 + p2
    w1 = p_g * p1 / denom
    w2 = p_g * p2 / denom
    e1 = i1 - N_GROUPS
    e2 = i2 - N_GROUPS

    gate_ref[...] = jnp.where(lane == 0, w1, jnp.where(lane == 1, w2, 0.0))

    oh1 = (lane == e1).astype(F32)
    oh2 = (lane == e2).astype(F32)
    both = oh1 + oh2
    ra = lax.broadcasted_iota(jnp.int32, (tt, tt), 0)
    rb = lax.broadcasted_iota(jnp.int32, (tt, tt), 1)
    strict = (rb < ra).astype(BF16)
    prefix = jnp.dot(strict, both.astype(BF16), preferred_element_type=F32) + run_ref[...]
    r1 = rsum(prefix * oh1)
    r2 = rsum(prefix * oh2)
    run_ref[...] = run_ref[...] + jnp.sum(both, axis=0, keepdims=True)

    packed = jnp.where(lane == 0, e1, jnp.where(lane == 1, e2, jnp.where(lane == 2, r1,
                                                                        jnp.where(lane == 3, r2, 0.0))))
    ids_ref[...] = jnp.transpose(packed)[:8, :].astype(jnp.int32)

    @pl.when(i == pl.num_programs(0) - 1)
    def _():
        cnt = jnp.broadcast_to(run_ref[...], (8, LANES))
        lane8 = lax.broadcasted_iota(jnp.int32, (8, LANES), 1)
        padded = jnp.floor((cnt + (blk - 1.0)) * (1.0 / blk)) * blk
        pend = padded
        for sh in (1, 2, 4, 8, 16, 32, 64):
            pend = pend + jnp.where(lane8 >= sh, pltpu.roll(pend, sh, 1), 0.0)
        pstart = pend - padded
        total = jnp.max(pend, axis=-1, keepdims=True)
        tail = total + (lane8 - N_EXPERTS).astype(F32) * blk
        fill = jnp.where(lane8 < N_EXPERTS, jnp.where(padded > 0, pend - blk, -1.0),
                         jnp.where((lane8 < 2 * N_EXPERTS) & (tail < n_blocks * blk), tail, -1.0))
        row8 = lax.broadcasted_iota(jnp.int32, (8, LANES), 0)
        plan_ref[...] = jnp.where(row8 == 0, pstart, jnp.where(row8 == 1, fill,
                                                               jnp.where(row8 == 2, cnt, 0.0))).astype(jnp.int32)


def _route(logits, blk, n_blocks):
    n = logits.shape[0]
    tt = min(512, n)
    blkspec = lambda: pl.BlockSpec((tt, LANES), lambda i: (i, 0))
    return pl.pallas_call(
        functools.partial(_route_kernel, blk, n_blocks),
        grid=(n // tt,),
        in_specs=[blkspec()],
        out_specs=[blkspec(),
                   pl.BlockSpec((8, tt), lambda i: (0, i)),
                   pl.BlockSpec((8, LANES), lambda i: (0, 0))],
        out_shape=[jax.ShapeDtypeStruct((n, LANES), F32),
                   jax.ShapeDtypeStruct((8, n), jnp.int32),
                   jax.ShapeDtypeStruct((8, LANES), jnp.int32)],
        scratch_shapes=[pltpu.VMEM((1, LANES), F32)],
        compiler_params=_params("arbitrary"),
    )(logits)


def _dispatch_kernel(tt, blk, n_fill, dest_ref, fill_ref, h_ref, xs_ref, zero_ref, sem, zsem):
    i = pl.program_id(0)
    base = i * (tt * TOP_K)

    @pl.when(i == 0)
    def _():
        zero_ref[...] = jnp.zeros(zero_ref.shape, U32)

        def zcopy(z):
            row = pl.multiple_of(jnp.maximum(fill_ref[z], 0), blk)
            return pltpu.make_async_copy(zero_ref, xs_ref.at[pl.ds(row, blk)], zsem)

        def zissue(z, carry):
            @pl.when(fill_ref[z] >= 0)
            def _():
                zcopy(z).start()
            return carry

        def zdrain(z, carry):
            @pl.when(fill_ref[z] >= 0)
            def _():
                zcopy(z).wait()
            return carry

        lax.fori_loop(0, n_fill, zissue, 0)
        lax.fori_loop(0, n_fill, zdrain, 0)

    def copy(r, kk):
        d = dest_ref[base + r * TOP_K + kk]
        return pltpu.make_async_copy(h_ref.at[r], xs_ref.at[d], sem)

    def issue(r, carry):
        for kk in range(TOP_K):
            copy(r, kk).start()
        return carry

    lax.fori_loop(0, tt, issue, 0, unroll=8)
    for _ in range(TOP_K):
        pltpu.make_async_copy(h_ref, xs_ref.at[pl.ds(0, tt)], sem).wait()


def _dispatch(h_packed, dest_flat, fill_rows, n_slots, blk):
    n = h_packed.shape[0]
    tile = h_packed.shape[1:]
    tt = min(128, n)
    n_fill = fill_rows.shape[0]
    grid_spec = pltpu.PrefetchScalarGridSpec(
        num_scalar_prefetch=2,
        grid=(n // tt,),
        in_specs=[pl.BlockSpec((tt,) + tile, lambda i, d, f: (i, 0, 0))],
        out_specs=pl.BlockSpec(memory_space=pl.ANY),
        scratch_shapes=[pltpu.VMEM((blk,) + tile, U32), pltpu.SemaphoreType.DMA(()), pltpu.SemaphoreType.DMA(())],
    )
    return pl.pallas_call(
        functools.partial(_dispatch_kernel, tt, blk, n_fill),
        grid_spec=grid_spec,
        out_shape=jax.ShapeDtypeStruct((n_slots,) + tile, U32),
        compiler_params=_params("arbitrary"),
    )(dest_flat, fill_rows, h_packed)


def _expert_kernel(blk, ahead, cnt_ref, pstart_ref, fill_ref, xs_ref, w1_ref, w3_ref, w2_ref, y_ref,
                   w1f, w3f, w2f, w1b, w3b, w2b, xbuf, ybuf, w_sem, in_sem, out_sem):
    e = pl.program_id(0)
    n_exp = pl.num_programs(0)
    wslot = e % 2
    n_blk = (cnt_ref[e] + (blk - 1)) // blk
    base = pstart_ref[e]
    n_x = xbuf.shape[0]

    def weight_copies(ex, slot):
        return [pltpu.make_async_copy(src.at[ex], dst.at[slot], w_sem.at[slot])
                for src, dst in ((w1_ref, w1f), (w3_ref, w3f), (w2_ref, w2f))]

    def rows(b):
        return pl.ds(pl.multiple_of(base + b * blk, blk), blk)

    def in_copy(b, slot):
        return pltpu.make_async_copy(xs_ref.at[rows(b)], xbuf.at[slot], in_sem.at[slot])

    def out_copy(b, slot):
        return pltpu.make_async_copy(ybuf.at[slot], y_ref.at[rows(b)], out_sem.at[slot])

    @pl.when(e == 0)
    def _():
        for c in weight_copies(0, 0):
            c.start()

    for p in range(ahead):
        @pl.when(p < n_blk)
        def _():
            in_copy(p, p).start()

    @pl.when(e + 1 < n_exp)
    def _():
        for c in weight_copies(e + 1, 1 - wslot):
            c.start()

    for c in weight_copies(e, wslot):
        c.wait()
    w1b[...] = w1f[wslot].astype(BF16)
    w3b[...] = w3f[wslot].astype(BF16)
    w2b[...] = w2f[wslot].astype(BF16)

    def body(b, carry):
        slot = b % 2

        @pl.when(b + ahead < n_blk)
        def _():
            in_copy(b + ahead, (b + ahead) % n_x).start()

        in_copy(b, b % n_x).wait()

        @pl.when(b >= 2)
        def _():
            out_copy(b - 2, slot).wait()

        lo, hi = _unpack_halves(_tiles_to_rows(xbuf[b % n_x]))
        lo = lo.astype(BF16)
        hi = hi.astype(BF16)
        half = lo.shape[1]
        a = jnp.dot(lo, w1b[:half, :], preferred_element_type=F32)
        a = a + jnp.dot(hi, w1b[half:, :], preferred_element_type=F32)
        g = jnp.dot(lo, w3b[:half, :], preferred_element_type=F32)
        g = g + jnp.dot(hi, w3b[half:, :], preferred_element_type=F32)
        mid = (_silu(a) * g).astype(BF16)
        ybuf[slot] = _rows_to_tiles(_pack_halves(jnp.dot(mid, w2b[...], preferred_element_type=F32)))
        out_copy(b, slot).start(priority=1)
        return carry

    lax.fori_loop(0, n_blk, body, 0)

    @pl.when(n_blk >= 2)
    def _():
        out_copy(n_blk - 2, n_blk % 2).wait()

    @pl.when(n_blk >= 1)
    def _():
        out_copy(n_blk - 1, (n_blk - 1) % 2).wait()

    @pl.when(e == pl.num_programs(0) - 1)
    def _():
        ybuf[0] = jnp.zeros(ybuf.shape[1:], U32)

        def zcopy(t):
            row = pl.multiple_of(jnp.maximum(fill_ref[N_EXPERTS + t], 0), blk)
            return pltpu.make_async_copy(ybuf.at[0], y_ref.at[pl.ds(row, blk)], out_sem.at[0])

        def zissue(t, carry):
            @pl.when(fill_ref[N_EXPERTS + t] >= 0)
            def _():
                zcopy(t).start()
            return carry

        def zdrain(t, carry):
            @pl.when(fill_ref[N_EXPERTS + t] >= 0)
            def _():
                zcopy(t).wait()
            return carry

        lax.fori_loop(0, N_EXPERTS, zissue, 0)
        lax.fori_loop(0, N_EXPERTS, zdrain, 0)


def _expert_blocks(xs, counts, pstart, fill_rows, w1, w3, w2, blk):
    n_slots = xs.shape[0]
    tile = xs.shape[1:]
    n_exp, d, de = w1.shape
    ahead = 3
    hbm = pl.BlockSpec(memory_space=pl.ANY)
    grid_spec = pltpu.PrefetchScalarGridSpec(
        num_scalar_prefetch=3,
        grid=(n_exp,),
        in_specs=[hbm, hbm, hbm, hbm],
        out_specs=hbm,
        scratch_shapes=[pltpu.VMEM((2, d, de), F32), pltpu.VMEM((2, d, de), F32), pltpu.VMEM((2, de, d), F32),
                        pltpu.VMEM((d, de), BF16), pltpu.VMEM((d, de), BF16), pltpu.VMEM((de, d), BF16),
                        pltpu.VMEM((ahead + 1, blk) + tile, U32), pltpu.VMEM((2, blk) + tile, U32),
                        pltpu.SemaphoreType.DMA((2,)), pltpu.SemaphoreType.DMA((ahead + 1,)),
                        pltpu.SemaphoreType.DMA((2,))],
    )
    return pl.pallas_call(
        functools.partial(_expert_kernel, blk, ahead),
        grid_spec=grid_spec,
        out_shape=jax.ShapeDtypeStruct((n_slots,) + tile, U32),
        compiler_params=_params("arbitrary"),
    )(counts, pstart, fill_rows, xs, w1, w3, w2)


def _combine_kernel(tt, n_tiles, dest_ref, x1_ref, g2_ref, gate_ref, yb_ref, o_ref, buf, sems):
    i = pl.program_id(0)

    def copy(tile, slot, r, kk):
        d = dest_ref[(tile * tt + r) * TOP_K + kk]
        return pltpu.make_async_copy(yb_ref.at[d], buf.at[slot, kk, r], sems.at[slot])

    def issue_tile(tile, slot):
        def body(r, carry):
            for kk in range(TOP_K):
                copy(tile, slot, r, kk).start()
            return carry
        lax.fori_loop(0, tt, body, 0, unroll=8)

    def wait_tile(tile, slot):
        for kk in range(TOP_K):
            pltpu.make_async_copy(yb_ref.at[pl.ds(0, tt)], buf.at[slot, kk], sems.at[slot]).wait()

    slot = i % 2

    @pl.when(i == 0)
    def _():
        issue_tile(0, 0)

    @pl.when(i + 1 < n_tiles)
    def _():
        issue_tile(i + 1, 1 - slot)

    wait_tile(i, slot)

    gate = gate_ref[...]
    wa = gate[:, 0:1]
    wb = gate[:, 1:2]
    lo_a, hi_a = _unpack_halves(_tiles_to_rows(buf[slot, 0]))
    lo_b, hi_b = _unpack_halves(_tiles_to_rows(buf[slot, 1]))
    y = jnp.concatenate([wa * lo_a + wb * lo_b, wa * hi_a + wb * hi_b], axis=-1)
    o_ref[...] = x1_ref[...] + g2_ref[0] * y


def _combine(x1, seq, g2, gates, dest_flat, yb):
    n, d = x1.shape
    tile = yb.shape[1:]
    tt = min(128, seq)
    n_tiles = n // tt
    tiles_per_seq = seq // tt
    grid_spec = pltpu.PrefetchScalarGridSpec(
        num_scalar_prefetch=1,
        grid=(n_tiles,),
        in_specs=[pl.BlockSpec((tt, d), lambda i, dr: (i, 0)),
                  pl.BlockSpec((1, 1, d), lambda i, dr: (i // tiles_per_seq, 0, 0)),
                  pl.BlockSpec((tt, LANES), lambda i, dr: (i, 0)),
                  pl.BlockSpec(memory_space=pl.ANY)],
        out_specs=pl.BlockSpec((tt, d), lambda i, dr: (i, 0)),
        scratch_shapes=[pltpu.VMEM((2, TOP_K, tt) + tile, U32), pltpu.SemaphoreType.DMA((2,))],
    )
    return pl.pallas_call(
        functools.partial(_combine_kernel, tt, n_tiles),
        grid_spec=grid_spec,
        out_shape=jax.ShapeDtypeStruct((n, d), F32),
        compiler_params=_params("arbitrary"),
    )(dest_flat, x1, g2, gates, yb)


def _rotation_tables(seq):
    half = HEAD_DIM // 2
    theta = ROPE_BASE ** (-np.arange(half, dtype=np.float64) / half)
    ang = np.arange(seq, dtype=np.float64)[:, None] * theta[None, :]
    cos_t = np.concatenate([np.cos(ang), np.cos(ang)], axis=-1).astype(np.float32)
    sin_t = np.concatenate([-np.sin(ang), np.sin(ang)], axis=-1).astype(np.float32)
    return jnp.asarray(cos_t), jnp.asarray(sin_t)


def _layer(x, c, w_ada, b_ada, norm1_w, w_in, forget_bias, q_norm_w, k_norm_w, ret_norm_w, w_out, norm2_w,
           w_coarse, b_coarse, w_fine, b_fine, w1, w3, w2):
    bsz, seq, d = x.shape
    n = bsz * seq
    d_fox = d // 2
    d_ret = d // 2
    n_heads = d_fox // HEAD_DIM

    mod = _ada_modulation(c, w_ada, b_ada)
    sh1, sc1, g1, sh2, sc2, g2 = [m.reshape(bsz, 1, d) for m in jnp.split(mod, 6, axis=-1)]

    f0 = 3 * d_fox
    w_fox = w_in[:, :f0].astype(BF16)
    w_ret = w_in[:, f0 + n_heads:].astype(BF16)
    w_f = jnp.zeros((d, LANES), BF16).at[:, :n_heads].set(w_in[:, f0:f0 + n_heads].astype(BF16))
    fb = jnp.zeros((1, LANES), F32).at[0, :n_heads].set(forget_bias)

    cos_t, sin_t = _rotation_tables(seq)

    x2d = x.reshape(n, d)
    z, log_f = _input_projection(x2d, seq, norm1_w.reshape(1, d), sc1, sh1, w_fox, w_ret, w_f, cos_t, sin_t,
                                 q_norm_w.reshape(1, HEAD_DIM), k_norm_w.reshape(1, HEAD_DIM), fb)

    lf = log_f[:, :n_heads].reshape(bsz, seq, n_heads).transpose(0, 2, 1).reshape(bsz * n_heads, seq)
    cum = _cumsum_rows(lf)

    o_a = _fox_attention(z, cum, bsz, seq, n_heads)
    log_g = jnp.log(1.0 - 2.0 ** (-5.0 - jnp.arange(n_heads, dtype=F32)))
    o_b = _retention(z, log_g, ret_norm_w.reshape(1, d_ret), bsz, seq, n_heads, 3 * d_fox)

    w_router = jnp.zeros((d, LANES), F32)
    w_router = w_router.at[:, :N_GROUPS].set(w_coarse)
    w_router = w_router.at[:, N_GROUPS:N_GROUPS + N_EXPERTS].set(
        w_fine.transpose(1, 0, 2).reshape(d, N_EXPERTS))
    b_router = jnp.zeros((1, LANES), F32)
    b_router = b_router.at[0, :N_GROUPS].set(b_coarse)
    b_router = b_router.at[0, N_GROUPS:N_GROUPS + N_EXPERTS].set(b_fine.reshape(N_EXPERTS))

    wr_hi = w_router.astype(BF16)
    wr_lo = (w_router - wr_hi.astype(F32)).astype(BF16)
    x1, h_packed, logits = _output_projection(o_a, o_b, w_out.astype(BF16), x2d, seq, g1,
                                              norm2_w.reshape(1, d), sc2, sh2,
                                              jnp.concatenate([wr_hi, wr_lo], axis=1), b_router)

    blk = 256
    nk = n * TOP_K
    n_blocks = nk // blk + N_EXPERTS
    gates, ids, plan = _route(logits, blk, n_blocks)
    pstart = plan[0, :N_EXPERTS]
    fill_rows = plan[1, :2 * N_EXPERTS]
    counts = plan[2, :N_EXPERTS]
    eid = ids[0:TOP_K]
    hit = eid[None] == jnp.arange(N_EXPERTS, dtype=jnp.int32)[:, None, None]
    dest = (jnp.sum(jnp.where(hit, pstart[:, None, None], 0), axis=0) + ids[TOP_K:2 * TOP_K]).T.reshape(nk)

    xs = _dispatch(h_packed, dest, fill_rows, n_blocks * blk, blk)
    yb = _expert_blocks(xs, counts, pstart, fill_rows, w1, w3, w2, blk)
    out = _combine(x1, seq, g2, gates, dest, yb)
    return out.reshape(bsz, seq, d)


def kernel(x, c, w_ada, b_ada, norm1_w, w_in, forget_bias, q_norm_w, k_norm_w, ret_norm_w, w_out, norm2_w,
           w_coarse, b_coarse, w_fine, b_fine, w1, w3, w2):
    c_in = c
    for l in range(w_ada.shape[0]):
        x = _layer(x, c_in, w_ada[l], b_ada[l], norm1_w[l], w_in[l], forget_bias[l], q_norm_w[l],
                   k_norm_w[l], ret_norm_w[l], w_out[l], norm2_w[l], w_coarse[l], b_coarse[l],
                   w_fine[l], b_fine[l], w1[l], w3[l], w2[l])
    return x
```

```python
import functools

import jax
import jax.numpy as jnp
import numpy as np
from jax import lax
from jax.experimental import pallas as pl
from jax.experimental.pallas import tpu as pltpu

HEAD_DIM = 128
N_GROUPS = 4
EXPERTS_PER_GROUP = 8
N_EXPERTS = N_GROUPS * EXPERTS_PER_GROUP
TOP_K = 2
ROPE_BASE = 10000.0
EPS = 1e-6

LANES = 128
VMEM_LIMIT = 56 * 1024 * 1024
NEG_BIG = -1e30
LOG2E = 1.4426950408889634

F32 = jnp.float32
BF16 = jnp.bfloat16
U32 = jnp.uint32


def _params(*sem):
    return pltpu.CompilerParams(dimension_semantics=sem, vmem_limit_bytes=VMEM_LIMIT)


def _silu(v):
    return v * (1.0 / (1.0 + jnp.exp(-v)))


def _pack_halves(y):
    w = y.shape[1] // 2
    lo = pltpu.bitcast(y[:, :w].astype(BF16).astype(F32), U32)
    hi = pltpu.bitcast(y[:, w:].astype(BF16).astype(F32), U32)
    return (hi & jnp.uint32(0xFFFF0000)) | (lo >> 16)


def _rows_to_tiles(p):
    return pltpu.einshape("m(ck)->mck", p, c=8, k=LANES)


def _tiles_to_rows(t):
    return pltpu.einshape("mck->m(ck)", t)


def _unpack_halves(p):
    lo = pltpu.bitcast(p << 16, F32)
    hi = pltpu.bitcast(p & jnp.uint32(0xFFFF0000), F32)
    return lo, hi


def _ada_kernel(ct_ref, w_ref, b_ref, o_ref):
    w = w_ref[...]
    rows = []
    for b in range(o_ref.shape[0]):
        if b < 2:
            cb = _silu(ct_ref[:, b:b + 1])
            rows.append(jnp.sum(cb * w, axis=0, keepdims=True) + b_ref[...])
        else:
            rows.append(jnp.zeros_like(b_ref[...]))
    o_ref[...] = jnp.concatenate(rows, axis=0)


def _ada_modulation(c, w_ada, b_ada):
    bsz, d = c.shape
    n = w_ada.shape[1]
    tn = 1024
    ct = jnp.zeros((d, LANES), F32).at[:, :bsz].set(c.T)
    out = pl.pallas_call(
        _ada_kernel,
        grid=(n // tn,),
        in_specs=[pl.BlockSpec((d, LANES), lambda j: (0, 0)),
                  pl.BlockSpec((d, tn), lambda j: (0, j)),
                  pl.BlockSpec((1, tn), lambda j: (0, j))],
        out_specs=pl.BlockSpec((8, tn), lambda j: (0, j)),
        out_shape=jax.ShapeDtypeStruct((8, n), F32),
        compiler_params=_params("arbitrary"),
    )(ct, w_ada, b_ada.reshape(1, n))
    return out[:bsz]


def _inproj_kernel(q_t, r_t, x_ref, nw_ref, sc_ref, sh_ref, wa_ref, wb_ref, wf_ref, cos_ref, sin_ref,
                   qw_ref, kw_ref, fb_ref, z_ref, f_ref, h_ref):
    j = pl.program_id(1)
    r0 = 3 * q_t

    @pl.when(j == 0)
    def _():
        x = x_ref[...]
        ms = jnp.mean(x * x, axis=-1, keepdims=True)
        y = x * lax.rsqrt(ms + EPS) * nw_ref[...]
        h = (y * (1.0 + sc_ref[0]) + sh_ref[0]).astype(BF16)
        h_ref[...] = h
        t = jnp.dot(h, wf_ref[...], preferred_element_type=F32) + fb_ref[...]
        f_ref[...] = jnp.minimum(t, 0.0) - jnp.log(1.0 + jnp.exp(-jnp.abs(t)))

    def heads_of(acc):
        return [acc[:, hh * HEAD_DIM:(hh + 1) * HEAD_DIM] for hh in range(acc.shape[1] // HEAD_DIM)]

    def head_norm(acc, w_row):
        outs = []
        for a in heads_of(acc):
            ms = jnp.mean(a * a, axis=-1, keepdims=True)
            outs.append(a * lax.rsqrt(ms + EPS) * w_row)
        return jnp.concatenate(outs, axis=-1).astype(BF16)

    def rotate(acc, scale):
        cs = cos_ref[...] * scale
        sn = sin_ref[...] * scale
        outs = [a * cs + pltpu.roll(a, HEAD_DIM // 2, 1) * sn for a in heads_of(acc)]
        return jnp.concatenate(outs, axis=-1).astype(BF16)

    def fox():
        return jnp.dot(h_ref[...], wa_ref[...], preferred_element_type=F32)

    def ret():
        return jnp.dot(h_ref[...], wb_ref[...], preferred_element_type=F32)

    @pl.when(j < q_t)
    def _():
        z_ref[...] = head_norm(fox(), qw_ref[...] * (LOG2E * HEAD_DIM ** -0.5))

    @pl.when((j >= q_t) & (j < 2 * q_t))
    def _():
        z_ref[...] = head_norm(fox(), kw_ref[...])

    @pl.when((j >= 2 * q_t) & (j < r0))
    def _():
        z_ref[...] = fox().astype(BF16)

    @pl.when((j >= r0) & (j < r0 + r_t))
    def _():
        z_ref[...] = rotate(ret(), 1.0)

    @pl.when((j >= r0 + r_t) & (j < r0 + 2 * r_t))
    def _():
        z_ref[...] = rotate(ret(), HEAD_DIM ** -0.5)

    @pl.when(j >= r0 + 2 * r_t)
    def _():
        z_ref[...] = ret().astype(BF16)


def _input_projection(x2d, seq, norm_w, sc1, sh1, w_fox, w_ret, w_f, cos_t, sin_t, qw, kw, fb):
    n, d = x2d.shape
    tm, tn = min(1024, seq), 1024
    fox_tiles = w_fox.shape[1] // tn
    ret_tiles = w_ret.shape[1] // tn
    tiles_per_seq = seq // tm
    kern = functools.partial(_inproj_kernel, fox_tiles // 3, ret_tiles // 4)
    bsel = lambda i, j: (i // tiles_per_seq, 0, 0)
    const = lambda i, j: (0, 0)
    return pl.pallas_call(
        kern,
        grid=(n // tm, fox_tiles + ret_tiles),
        in_specs=[pl.BlockSpec((tm, d), lambda i, j: (i, 0)),
                  pl.BlockSpec((1, d), const),
                  pl.BlockSpec((1, 1, d), bsel),
                  pl.BlockSpec((1, 1, d), bsel),
                  pl.BlockSpec((d, tn), lambda i, j: (0, jnp.minimum(j, fox_tiles - 1))),
                  pl.BlockSpec((d, tn), lambda i, j: (0, jnp.maximum(j - fox_tiles, 0))),
                  pl.BlockSpec((d, LANES), const),
                  pl.BlockSpec((tm, HEAD_DIM), lambda i, j: (i % tiles_per_seq, 0)),
                  pl.BlockSpec((tm, HEAD_DIM), lambda i, j: (i % tiles_per_seq, 0)),
                  pl.BlockSpec((1, HEAD_DIM), const),
                  pl.BlockSpec((1, HEAD_DIM), const),
                  pl.BlockSpec((1, LANES), const)],
        out_specs=[pl.BlockSpec((tm, tn), lambda i, j: (i, j)),
                   pl.BlockSpec((tm, LANES), lambda i, j: (i, 0))],
        out_shape=[jax.ShapeDtypeStruct((n, w_fox.shape[1] + w_ret.shape[1]), BF16),
                   jax.ShapeDtypeStruct((n, LANES), F32)],
        scratch_shapes=[pltpu.VMEM((tm, d), BF16)],
        compiler_params=_params("arbitrary", "arbitrary"),
    )(x2d, norm_w, sc1, sh1, w_fox, w_ret, w_f, cos_t, sin_t, qw, kw, fb)


def _cumsum_kernel(x_ref, o_ref):
    x = x_ref[0]
    r = x.shape[0]
    a = lax.broadcasted_iota(jnp.int32, (LANES, LANES), 0)
    b = lax.broadcasted_iota(jnp.int32, (LANES, LANES), 1)
    upper = (a <= b).astype(F32)
    within = jnp.dot(x, upper, precision=lax.Precision.HIGHEST, preferred_element_type=F32)
    tot = jnp.broadcast_to(within[:, LANES - 1:LANES], (r, LANES))
    ra = lax.broadcasted_iota(jnp.int32, (r, r), 0)
    rb = lax.broadcasted_iota(jnp.int32, (r, r), 1)
    strict = (rb < ra).astype(F32)
    before = jnp.dot(strict, tot, precision=lax.Precision.HIGHEST, preferred_element_type=F32)
    o_ref[0] = within + before


def _cumsum_rows(x):
    g, s = x.shape
    r = s // LANES
    out = pl.pallas_call(
        _cumsum_kernel,
        grid=(g,),
        in_specs=[pl.BlockSpec((1, r, LANES), lambda i: (i, 0, 0))],
        out_specs=pl.BlockSpec((1, r, LANES), lambda i: (i, 0, 0)),
        out_shape=jax.ShapeDtypeStruct((g, r, LANES), F32),
        compiler_params=_params("arbitrary"),
    )(x.reshape(g, r, LANES))
    return out.reshape(g, 1, s)


def _fox_kernel(tq, q_ref, k_ref, v_ref, cum_ref, o_ref, sa_ref, sb_ref, m_ref, l_ref, acc_ref):
    qi = pl.program_id(2)
    q_start = pl.multiple_of(qi * tq, tq)
    c0 = cum_ref[0, :, pl.ds(q_start, LANES)][:, 0:1]

    m_ref[...] = jnp.full(m_ref.shape, NEG_BIG, F32)
    l_ref[...] = jnp.zeros(l_ref.shape, F32)
    acc_ref[...] = jnp.zeros(acc_ref.shape, F32)
    n_slabs = tq // LANES

    def scores(kb, s_ref):
        start = pl.multiple_of(kb * tq, tq)
        k = k_ref[pl.ds(start, tq), :]
        bias = (c0 - cum_ref[0, :, pl.ds(start, tq)]) * LOG2E
        s_ref[...] = lax.dot_general(q_ref[...], k, (((1,), (1,)), ((), ())),
                                     preferred_element_type=F32) + bias

    def softmax_pv(kb, s_ref, masked):
        start = pl.multiple_of(kb * tq, tq)
        v = v_ref[pl.ds(start, tq), :]
        slabs = []
        for j in range(n_slabs):
            t = s_ref[:, j * LANES:(j + 1) * LANES]
            if masked:
                row = lax.broadcasted_iota(jnp.int32, t.shape, 0)
                col = lax.broadcasted_iota(jnp.int32, t.shape, 1) + j * LANES
                t = jnp.where(col <= row, t, NEG_BIG)
            slabs.append(t)
        mx = slabs[0]
        for t in slabs[1:]:
            mx = jnp.maximum(mx, t)
        m_prev = m_ref[...]
        m_new = jnp.maximum(m_prev, jnp.max(mx, axis=-1, keepdims=True))
        alpha = jnp.exp2(m_prev - m_new)
        probs = [jnp.exp2(t - m_new) for t in slabs]
        psum = probs[0]
        for t in probs[1:]:
            psum = psum + t
        l_ref[...] = alpha * l_ref[...] + psum
        p = jnp.concatenate([t.astype(BF16) for t in probs], axis=-1)
        acc_ref[...] = alpha * acc_ref[...] + jnp.dot(p, v, preferred_element_type=F32)
        m_ref[...] = m_new

    scores(0, sa_ref)

    def pair(kb):
        scores(kb + 1, sb_ref)
        softmax_pv(kb, sa_ref, False)
        scores(kb + 2, sa_ref)
        softmax_pv(kb + 1, sb_ref, False)

    def body4(i, carry):
        pair(4 * i)
        pair(4 * i + 2)
        return carry

    def body2(i, carry):
        pair(2 * i)
        return carry

    n4 = qi // 4
    lax.fori_loop(0, n4, body4, 0)
    lax.fori_loop(2 * n4, qi // 2, body2, 0)

    @pl.when(qi % 2 == 0)
    def _():
        softmax_pv(qi, sa_ref, True)

    @pl.when(qi % 2 == 1)
    def _():
        scores(qi, sb_ref)
        softmax_pv(qi - 1, sa_ref, False)
        softmax_pv(qi, sb_ref, True)

    o_ref[...] = (acc_ref[...] / jnp.sum(l_ref[...], axis=-1, keepdims=True)).astype(BF16)


def _fox_attention(z, cum, bsz, seq, n_heads):
    tq = min(512, seq)
    nq = seq // tq
    kern = functools.partial(_fox_kernel, tq)
    return pl.pallas_call(
        kern,
        grid=(bsz, n_heads, nq),
        in_specs=[pl.BlockSpec((tq, HEAD_DIM), lambda b, h, i: (b * nq + i, h)),
                  pl.BlockSpec((seq, HEAD_DIM), lambda b, h, i: (b, n_heads + h)),
                  pl.BlockSpec((seq, HEAD_DIM), lambda b, h, i: (b, 2 * n_heads + h)),
                  pl.BlockSpec((1, 1, seq), lambda b, h, i: (b * n_heads + h, 0, 0))],
        out_specs=pl.BlockSpec((tq, HEAD_DIM), lambda b, h, i: (b * nq + i, h)),
        out_shape=jax.ShapeDtypeStruct((bsz * seq, n_heads * HEAD_DIM), BF16),
        scratch_shapes=[pltpu.VMEM((tq, tq), F32), pltpu.VMEM((tq, tq), F32),
                        pltpu.VMEM((tq, LANES), F32), pltpu.VMEM((tq, LANES), F32),
                        pltpu.VMEM((tq, HEAD_DIM), F32)],
        compiler_params=_params("arbitrary", "arbitrary", "arbitrary"),
    )(z, z, z, cum)


def _ret_kernel(chunk, n_heads, lg_ref, q_ref, k_ref, v_ref, g_ref, nw_ref, o_ref, state_ref, decay_ref):
    first = (pl.program_id(0) == 0) & (pl.program_id(1) == 0)

    @pl.when(first)
    def _():
        i = lax.broadcasted_iota(jnp.int32, (chunk, chunk), 0)
        jj = lax.broadcasted_iota(jnp.int32, (chunk, chunk), 1)
        diff = (i - jj).astype(F32)
        for h in range(n_heads):
            decay_ref[h] = jnp.where(diff >= 0, jnp.exp(lg_ref[h] * jnp.maximum(diff, 0.0)), 0.0)

    @pl.when(pl.program_id(1) == 0)
    def _():
        state_ref[...] = jnp.zeros(state_ref.shape, F32)

    pos = lax.broadcasted_iota(jnp.int32, (chunk, HEAD_DIM), 0).astype(F32)
    for h in range(n_heads):
        log_g = lg_ref[h]
        cols = slice(h * HEAD_DIM, (h + 1) * HEAD_DIM)
        q = q_ref[:, cols]
        k = k_ref[:, cols]
        v = v_ref[:, cols]
        scores = lax.dot_general(q, k, (((1,), (1,)), ((), ())), preferred_element_type=F32)
        scores = scores * decay_ref[h]
        intra = jnp.dot(scores.astype(BF16), v, preferred_element_type=F32)
        state = state_ref[h]
        inter = jnp.dot(q, state.astype(BF16), preferred_element_type=F32) * jnp.exp(log_g * (pos + 1.0))
        kd = (k.astype(F32) * jnp.exp(log_g * (chunk - 1.0 - pos))).astype(BF16)
        kv = lax.dot_general(kd, v, (((0,), (0,)), ((), ())), preferred_element_type=F32)
        state_ref[h] = state * jnp.exp(jnp.full((1, HEAD_DIM), chunk, F32) * log_g) + kv
        o = intra + inter
        ms = jnp.mean(o * o, axis=-1, keepdims=True)
        o = o * lax.rsqrt(ms + EPS) * nw_ref[:, cols]
        o_ref[:, cols] = (o * _silu(g_ref[:, cols].astype(F32))).astype(BF16)


def _retention(z, log_g, norm_w, bsz, seq, n_heads, col0):
    chunk = min(256, seq)
    nt = seq // chunk
    width = n_heads * HEAD_DIM
    c0 = col0 // width
    kern = functools.partial(_ret_kernel, chunk, n_heads)

    def sec(s):
        return pl.BlockSpec((chunk, width), lambda b, t, lg: (b * nt + t, c0 + s))

    grid_spec = pltpu.PrefetchScalarGridSpec(
        num_scalar_prefetch=1,
        grid=(bsz, nt),
        in_specs=[sec(0), sec(1), sec(2), sec(3), pl.BlockSpec((1, width), lambda b, t, lg: (0, 0))],
        out_specs=pl.BlockSpec((chunk, width), lambda b, t, lg: (b * nt + t, 0)),
        scratch_shapes=[pltpu.VMEM((n_heads, HEAD_DIM, HEAD_DIM), F32),
                        pltpu.VMEM((n_heads, chunk, chunk), F32)],
    )
    return pl.pallas_call(
        kern,
        grid_spec=grid_spec,
        out_shape=jax.ShapeDtypeStruct((bsz * seq, width), BF16),
        compiler_params=_params("arbitrary", "arbitrary"),
    )(log_g, z, z, z, z, norm_w)


def _outproj_kernel(oa_ref, ob_ref, wa_ref, wb_ref, x_ref, g1_ref, nw_ref, sc_ref, sh_ref, wr_ref, br_ref,
                    x1_ref, hp_ref, lg_ref):
    mix = jnp.dot(oa_ref[...], wa_ref[...], preferred_element_type=F32)
    mix = mix + jnp.dot(ob_ref[...], wb_ref[...], preferred_element_type=F32)
    x1 = x_ref[...] + g1_ref[0] * mix
    x1_ref[...] = x1
    ms = jnp.mean(x1 * x1, axis=-1, keepdims=True)
    h2 = x1 * lax.rsqrt(ms + EPS) * nw_ref[...] * (1.0 + sc_ref[0]) + sh_ref[0]
    hp_ref[...] = _rows_to_tiles(_pack_halves(h2))
    h_hi = h2.astype(BF16)
    h_lo = (h2 - h_hi.astype(F32)).astype(BF16)
    both = jnp.dot(h_hi, wr_ref[...], preferred_element_type=F32)
    cross = jnp.dot(h_lo, wr_ref[:, :LANES], preferred_element_type=F32)
    lg_ref[...] = both[:, :LANES] + both[:, LANES:] + cross + br_ref[...]


def _output_projection(o_a, o_b, w_out, x2d, seq, g1, norm_w, sc2, sh2, w_router, b_router):
    n, d = x2d.shape
    da = o_a.shape[1]
    tm = min(512, seq)
    tiles_per_seq = seq // tm
    bsel = lambda i: (i // tiles_per_seq, 0, 0)
    return pl.pallas_call(
        _outproj_kernel,
        grid=(n // tm,),
        in_specs=[pl.BlockSpec((tm, da), lambda i: (i, 0)),
                  pl.BlockSpec((tm, da), lambda i: (i, 0)),
                  pl.BlockSpec((da, d), lambda i: (0, 0)),
                  pl.BlockSpec((da, d), lambda i: (1, 0)),
                  pl.BlockSpec((tm, d), lambda i: (i, 0)),
                  pl.BlockSpec((1, 1, d), bsel),
                  pl.BlockSpec((1, d), lambda i: (0, 0)),
                  pl.BlockSpec((1, 1, d), bsel),
                  pl.BlockSpec((1, 1, d), bsel),
                  pl.BlockSpec((d, 2 * LANES), lambda i: (0, 0)),
                  pl.BlockSpec((1, LANES), lambda i: (0, 0))],
        out_specs=[pl.BlockSpec((tm, d), lambda i: (i, 0)),
                   pl.BlockSpec((tm, d // 2 // LANES, LANES), lambda i: (i, 0, 0)),
                   pl.BlockSpec((tm, LANES), lambda i: (i, 0))],
        out_shape=[jax.ShapeDtypeStruct((n, d), F32),
                   jax.ShapeDtypeStruct((n, d // 2 // LANES, LANES), U32),
                   jax.ShapeDtypeStruct((n, LANES), F32)],
        compiler_params=_params("arbitrary"),
    )(o_a, o_b, w_out, w_out, x2d, g1, norm_w, sc2, sh2, w_router, b_router)


def _route_kernel(blk, n_blocks, lg_ref, gate_ref, ids_ref, plan_ref, run_ref):
    i = pl.program_id(0)

    @pl.when(i == 0)
    def _():
        run_ref[...] = jnp.zeros(run_ref.shape, F32)

    lg = lg_ref[...]
    tt = lg.shape[0]
    lane = lax.broadcasted_iota(jnp.int32, lg.shape, 1).astype(F32)
    big = 1e6

    def rmax(v):
        return jnp.max(v, axis=-1, keepdims=True)

    def rmin(v):
        return jnp.min(v, axis=-1, keepdims=True)

    def rsum(v):
        return jnp.sum(v, axis=-1, keepdims=True)

    cmask = lane < N_GROUPS
    cm = jnp.where(cmask, lg, NEG_BIG)
    ce = jnp.where(cmask, jnp.exp(cm - rmax(cm)), 0.0)
    pgrp = ce / rsum(ce)
    p_g = rmax(pgrp)
    g_sel = rmin(jnp.where(cmask & (pgrp == p_g), lane, big))

    lo = N_GROUPS + EXPERTS_PER_GROUP * g_sel
    fmask = (lane >= lo) & (lane < lo + EXPERTS_PER_GROUP)
    fm = jnp.where(fmask, lg, NEG_BIG)
    fe = jnp.where(fmask, jnp.exp(fm - rmax(fm)), 0.0)
    fp = fe / rsum(fe)
    fp = jnp.where(fmask, fp, -1.0)
    p1 = rmax(fp)
    i1 = rmin(jnp.where(fp == p1, lane, big))
    fp2 = jnp.where(lane == i1, -1.0, fp)
    p2 = rmax(fp2)
    i2 = rmin(jnp.where(fp2 == p2, lane, big))
    denom = p1 + p2
    w1 = p_g * p1 / denom
    w2 = p_g * p2 / denom
    e1 = i1 - N_GROUPS
    e2 = i2 - N_GROUPS

    gate_ref[...] = jnp.where(lane == 0, w1, jnp.where(lane == 1, w2, 0.0))

    oh1 = (lane == e1).astype(F32)
    oh2 = (lane == e2).astype(F32)
    both = oh1 + oh2
    ra = lax.broadcasted_iota(jnp.int32, (tt, tt), 0)
    rb = lax.broadcasted_iota(jnp.int32, (tt, tt), 1)
    strict = (rb < ra).astype(BF16)
    prefix = jnp.dot(strict, both.astype(BF16), preferred_element_type=F32) + run_ref[...]
    r1 = rsum(prefix * oh1)
    r2 = rsum(prefix * oh2)
    run_ref[...] = run_ref[...] + jnp.sum(both, axis=0, keepdims=True)

    packed = jnp.where(lane == 0, e1, jnp.where(lane == 1, e2, jnp.where(lane == 2, r1,
                                                                        jnp.where(lane == 3, r2, 0.0))))
    ids_ref[...] = jnp.transpose(packed)[:8, :].astype(jnp.int32)

    @pl.when(i == pl.num_programs(0) - 1)
    def _():
        cnt = jnp.broadcast_to(run_ref[...], (8, LANES))
        lane8 = lax.broadcasted_iota(jnp.int32, (8, LANES), 1)
        padded = jnp.floor((cnt + (blk - 1.0)) * (1.0 / blk)) * blk
        pend = padded
        for sh in (1, 2, 4, 8, 16, 32, 64):
            pend = pend + jnp.where(lane8 >= sh, pltpu.roll(pend, sh, 1), 0.0)
        pstart = pend - padded
        total = jnp.max(pend, axis=-1, keepdims=True)
        tail = total + (lane8 - N_EXPERTS).astype(F32) * blk
        fill = jnp.where(lane8 < N_EXPERTS, jnp.where(padded > 0, pend - blk, -1.0),
                         jnp.where((lane8 < 2 * N_EXPERTS) & (tail < n_blocks * blk), tail, -1.0))
        row8 = lax.broadcasted_iota(jnp.int32, (8, LANES), 0)
        plan_ref[...] = jnp.where(row8 == 0, pstart, jnp.where(row8 == 1, fill,
                                                               jnp.where(row8 == 2, cnt, 0.0))).astype(jnp.int32)


def _route(logits, blk, n_blocks):
    n = logits.shape[0]
    tt = min(512, n)
    blkspec = lambda: pl.BlockSpec((tt, LANES), lambda i: (i, 0))
    return pl.pallas_call(
        functools.partial(_route_kernel, blk, n_blocks),
        grid=(n // tt,),
        in_specs=[blkspec()],
        out_specs=[blkspec(),
                   pl.BlockSpec((8, tt), lambda i: (0, i)),
                   pl.BlockSpec((8, LANES), lambda i: (0, 0))],
        out_shape=[jax.ShapeDtypeStruct((n, LANES), F32),
                   jax.ShapeDtypeStruct((8, n), jnp.int32),
                   jax.ShapeDtypeStruct((8, LANES), jnp.int32)],
        scratch_shapes=[pltpu.VMEM((1, LANES), F32)],
        compiler_params=_params("arbitrary"),
    )(logits)


def _dispatch_kernel(tt, blk, n_fill, dest_ref, fill_ref, h_ref, xs_ref, zero_ref, sem, zsem):
    i = pl.program_id(0)
    base = i * (tt * TOP_K)

    @pl.when(i == 0)
    def _():
        zero_ref[...] = jnp.zeros(zero_ref.shape, U32)

        def zcopy(z):
            row = pl.multiple_of(jnp.maximum(fill_ref[z], 0), blk)
            return pltpu.make_async_copy(zero_ref, xs_ref.at[pl.ds(row, blk)], zsem)

        def zissue(z, carry):
            @pl.when(fill_ref[z] >= 0)
            def _():
                zcopy(z).start()
            return carry

        def zdrain(z, carry):
            @pl.when(fill_ref[z] >= 0)
            def _():
                zcopy(z).wait()
            return carry

        lax.fori_loop(0, n_fill, zissue, 0)
        lax.fori_loop(0, n_fill, zdrain, 0)

    def copy(r, kk):
        d = dest_ref[base + r * TOP_K + kk]
        return pltpu.make_async_copy(h_ref.at[r], xs_ref.at[d], sem)

    def issue(r, carry):
        for kk in range(TOP_K):
            copy(r, kk).start()
        return carry

    lax.fori_loop(0, tt, issue, 0, unroll=8)
    for _ in range(TOP_K):
        pltpu.make_async_copy(h_ref, xs_ref.at[pl.ds(0, tt)], sem).wait()


def _dispatch(h_packed, dest_flat, fill_rows, n_slots, blk):
    n = h_packed.shape[0]
    tile = h_packed.shape[1:]
    tt = min(512, n)
    n_fill = fill_rows.shape[0]
    grid_spec = pltpu.PrefetchScalarGridSpec(
        num_scalar_prefetch=2,
        grid=(n // tt,),
        in_specs=[pl.BlockSpec((tt,) + tile, lambda i, d, f: (i, 0, 0))],
        out_specs=pl.BlockSpec(memory_space=pl.ANY),
        scratch_shapes=[pltpu.VMEM((blk,) + tile, U32), pltpu.SemaphoreType.DMA(()), pltpu.SemaphoreType.DMA(())],
    )
    return pl.pallas_call(
        functools.partial(_dispatch_kernel, tt, blk, n_fill),
        grid_spec=grid_spec,
        out_shape=jax.ShapeDtypeStruct((n_slots,) + tile, U32),
        compiler_params=_params("arbitrary"),
    )(dest_flat, fill_rows, h_packed)


def _expert_kernel(blk, ahead, cnt_ref, pstart_ref, fill_ref, xs_ref, w1_ref, w3_ref, w2_ref, y_ref,
                   w1f, w3f, w2f, w1b, w3b, w2b, xbuf, ybuf, w_sem, in_sem, out_sem):
    e = pl.program_id(0)
    n_exp = pl.num_programs(0)
    wslot = e % 2
    n_blk = (cnt_ref[e] + (blk - 1)) // blk
    base = pstart_ref[e]
    n_x = xbuf.shape[0]

    def weight_copies(ex, slot):
        return [pltpu.make_async_copy(src.at[ex], dst.at[slot], w_sem.at[slot])
                for src, dst in ((w1_ref, w1f), (w3_ref, w3f), (w2_ref, w2f))]

    def rows(b):
        return pl.ds(pl.multiple_of(base + b * blk, blk), blk)

    def in_copy(b, slot):
        return pltpu.make_async_copy(xs_ref.at[rows(b)], xbuf.at[slot], in_sem.at[slot])

    def out_copy(b, slot):
        return pltpu.make_async_copy(ybuf.at[slot], y_ref.at[rows(b)], out_sem.at[slot])

    @pl.when(e == 0)
    def _():
        for c in weight_copies(0, 0):
            c.start()

    for p in range(ahead):
        @pl.when(p < n_blk)
        def _():
            in_copy(p, p).start()

    @pl.when(e + 1 < n_exp)
    def _():
        for c in weight_copies(e + 1, 1 - wslot):
            c.start()

    for c in weight_copies(e, wslot):
        c.wait()
    w1b[...] = w1f[wslot].astype(BF16)
    w3b[...] = w3f[wslot].astype(BF16)
    w2b[...] = w2f[wslot].astype(BF16)

    def body(b, carry):
        slot = b % 2

        @pl.when(b + ahead < n_blk)
        def _():
            in_copy(b + ahead, (b + ahead) % n_x).start()

        in_copy(b, b % n_x).wait()

        @pl.when(b >= 2)
        def _():
            out_copy(b - 2, slot).wait()

        lo, hi = _unpack_halves(_tiles_to_rows(xbuf[b % n_x]))
        lo = lo.astype(BF16)
        hi = hi.astype(BF16)
        half = lo.shape[1]
        a = jnp.dot(lo, w1b[:half, :], preferred_element_type=F32)
        a = a + jnp.dot(hi, w1b[half:, :], preferred_element_type=F32)
        g = jnp.dot(lo, w3b[:half, :], preferred_element_type=F32)
        g = g + jnp.dot(hi, w3b[half:, :], preferred_element_type=F32)
        mid = (_silu(a) * g).astype(BF16)
        ybuf[slot] = _rows_to_tiles(_pack_halves(jnp.dot(mid, w2b[...], preferred_element_type=F32)))
        out_copy(b, slot).start(priority=1)
        return carry

    lax.fori_loop(0, n_blk, body, 0)

    @pl.when(n_blk >= 2)
    def _():
        out_copy(n_blk - 2, n_blk % 2).wait()

    @pl.when(n_blk >= 1)
    def _():
        out_copy(n_blk - 1, (n_blk - 1) % 2).wait()

    @pl.when(e == pl.num_programs(0) - 1)
    def _():
        ybuf[0] = jnp.zeros(ybuf.shape[1:], U32)

        def zcopy(t):
            row = pl.multiple_of(jnp.maximum(fill_ref[N_EXPERTS + t], 0), blk)
            return pltpu.make_async_copy(ybuf.at[0], y_ref.at[pl.ds(row, blk)], out_sem.at[0])

        def zissue(t, carry):
            @pl.when(fill_ref[N_EXPERTS + t] >= 0)
            def _():
                zcopy(t).start()
            return carry

        def zdrain(t, carry):
            @pl.when(fill_ref[N_EXPERTS + t] >= 0)
            def _():
                zcopy(t).wait()
            return carry

        lax.fori_loop(0, N_EXPERTS, zissue, 0)
        lax.fori_loop(0, N_EXPERTS, zdrain, 0)


def _expert_blocks(xs, counts, pstart, fill_rows, w1, w3, w2, blk):
    n_slots = xs.shape[0]
    tile = xs.shape[1:]
    n_exp, d, de = w1.shape
    ahead = 3
    hbm = pl.BlockSpec(memory_space=pl.ANY)
    grid_spec = pltpu.PrefetchScalarGridSpec(
        num_scalar_prefetch=3,
        grid=(n_exp,),
        in_specs=[hbm, hbm, hbm, hbm],
        out_specs=hbm,
        scratch_shapes=[pltpu.VMEM((2, d, de), F32), pltpu.VMEM((2, d, de), F32), pltpu.VMEM((2, de, d), F32),
                        pltpu.VMEM((d, de), BF16), pltpu.VMEM((d, de), BF16), pltpu.VMEM((de, d), BF16),
                        pltpu.VMEM((ahead + 1, blk) + tile, U32), pltpu.VMEM((2, blk) + tile, U32),
                        pltpu.SemaphoreType.DMA((2,)), pltpu.SemaphoreType.DMA((ahead + 1,)),
                        pltpu.SemaphoreType.DMA((2,))],
    )
    return pl.pallas_call(
        functools.partial(_expert_kernel, blk, ahead),
        grid_spec=grid_spec,
        out_shape=jax.ShapeDtypeStruct((n_slots,) + tile, U32),
        compiler_params=_params("arbitrary"),
    )(counts, pstart, fill_rows, xs, w1, w3, w2)


def _combine_kernel(tt, n_tiles, dest_ref, x1_ref, g2_ref, gate_ref, yb_ref, o_ref, buf, sems):
    i = pl.program_id(0)

    def copy(tile, slot, r, kk):
        d = dest_ref[(tile * tt + r) * TOP_K + kk]
        return pltpu.make_async_copy(yb_ref.at[d], buf.at[slot, kk, r], sems.at[slot])

    def issue_tile(tile, slot):
        def body(r, carry):
            for kk in range(TOP_K):
                copy(tile, slot, r, kk).start()
            return carry
        lax.fori_loop(0, tt, body, 0, unroll=8)

    def wait_tile(tile, slot):
        for kk in range(TOP_K):
            pltpu.make_async_copy(yb_ref.at[pl.ds(0, tt)], buf.at[slot, kk], sems.at[slot]).wait()

    slot = i % 2

    @pl.when(i == 0)
    def _():
        issue_tile(0, 0)

    @pl.when(i + 1 < n_tiles)
    def _():
        issue_tile(i + 1, 1 - slot)

    wait_tile(i, slot)

    gate = gate_ref[...]
    wa = gate[:, 0:1]
    wb = gate[:, 1:2]
    lo_a, hi_a = _unpack_halves(_tiles_to_rows(buf[slot, 0]))
    lo_b, hi_b = _unpack_halves(_tiles_to_rows(buf[slot, 1]))
    y = jnp.concatenate([wa * lo_a + wb * lo_b, wa * hi_a + wb * hi_b], axis=-1)
    o_ref[...] = x1_ref[...] + g2_ref[0] * y


def _combine(x1, seq, g2, gates, dest_flat, yb):
    n, d = x1.shape
    tile = yb.shape[1:]
    tt = min(512, seq)
    n_tiles = n // tt
    tiles_per_seq = seq // tt
    grid_spec = pltpu.PrefetchScalarGridSpec(
        num_scalar_prefetch=1,
        grid=(n_tiles,),
        in_specs=[pl.BlockSpec((tt, d), lambda i, dr: (i, 0)),
                  pl.BlockSpec((1, 1, d), lambda i, dr: (i // tiles_per_seq, 0, 0)),
                  pl.BlockSpec((tt, LANES), lambda i, dr: (i, 0)),
                  pl.BlockSpec(memory_space=pl.ANY)],
        out_specs=pl.BlockSpec((tt, d), lambda i, dr: (i, 0)),
        scratch_shapes=[pltpu.VMEM((2, TOP_K, tt) + tile, U32), pltpu.SemaphoreType.DMA((2,))],
    )
    return pl.pallas_call(
        functools.partial(_combine_kernel, tt, n_tiles),
        grid_spec=grid_spec,
        out_shape=jax.ShapeDtypeStruct((n, d), F32),
        compiler_params=_params("arbitrary"),
    )(dest_flat, x1, g2, gates, yb)


def _rotation_tables(seq):
    half = HEAD_DIM // 2
    theta = ROPE_BASE ** (-np.arange(half, dtype=np.float64) / half)
    ang = np.arange(seq, dtype=np.float64)[:, None] * theta[None, :]
    cos_t = np.concatenate([np.cos(ang), np.cos(ang)], axis=-1).astype(np.float32)
    sin_t = np.concatenate([-np.sin(ang), np.sin(ang)], axis=-1).astype(np.float32)
    return jnp.asarray(cos_t), jnp.asarray(sin_t)


def _layer(x, c, w_ada, b_ada, norm1_w, w_in, forget_bias, q_norm_w, k_norm_w, ret_norm_w, w_out, norm2_w,
           w_coarse, b_coarse, w_fine, b_fine, w1, w3, w2):
    bsz, seq, d = x.shape
    n = bsz * seq
    d_fox = d // 2
    d_ret = d // 2
    n_heads = d_fox // HEAD_DIM

    mod = _ada_modulation(c, w_ada, b_ada)
    sh1, sc1, g1, sh2, sc2, g2 = [m.reshape(bsz, 1, d) for m in jnp.split(mod, 6, axis=-1)]

    f0 = 3 * d_fox
    w_fox = w_in[:, :f0].astype(BF16)
    w_ret = w_in[:, f0 + n_heads:].astype(BF16)
    w_f = jnp.zeros((d, LANES), BF16).at[:, :n_heads].set(w_in[:, f0:f0 + n_heads].astype(BF16))
    fb = jnp.zeros((1, LANES), F32).at[0, :n_heads].set(forget_bias)

    cos_t, sin_t = _rotation_tables(seq)

    x2d = x.reshape(n, d)
    z, log_f = _input_projection(x2d, seq, norm1_w.reshape(1, d), sc1, sh1, w_fox, w_ret, w_f, cos_t, sin_t,
                                 q_norm_w.reshape(1, HEAD_DIM), k_norm_w.reshape(1, HEAD_DIM), fb)

    lf = log_f[:, :n_heads].reshape(bsz, seq, n_heads).transpose(0, 2, 1).reshape(bsz * n_heads, seq)
    cum = _cumsum_rows(lf)

    o_a = _fox_attention(z, cum, bsz, seq, n_heads)
    log_g = jnp.log(1.0 - 2.0 ** (-5.0 - jnp.arange(n_heads, dtype=F32)))
    o_b = _retention(z, log_g, ret_norm_w.reshape(1, d_ret), bsz, seq, n_heads, 3 * d_fox)

    w_router = jnp.zeros((d, LANES), F32)
    w_router = w_router.at[:, :N_GROUPS].set(w_coarse)
    w_router = w_router.at[:, N_GROUPS:N_GROUPS + N_EXPERTS].set(
        w_fine.transpose(1, 0, 2).reshape(d, N_EXPERTS))
    b_router = jnp.zeros((1, LANES), F32)
    b_router = b_router.at[0, :N_GROUPS].set(b_coarse)
    b_router = b_router.at[0, N_GROUPS:N_GROUPS + N_EXPERTS].set(b_fine.reshape(N_EXPERTS))

    wr_hi = w_router.astype(BF16)
    wr_lo = (w_router - wr_hi.astype(F32)).astype(BF16)
    x1, h_packed, logits = _output_projection(o_a, o_b, w_out.astype(BF16), x2d, seq, g1,
                                              norm2_w.reshape(1, d), sc2, sh2,
                                              jnp.concatenate([wr_hi, wr_lo], axis=1), b_router)

    blk = 256
    nk = n * TOP_K
    n_blocks = nk // blk + N_EXPERTS
    gates, ids, plan = _route(logits, blk, n_blocks)
    pstart = plan[0, :N_EXPERTS]
    fill_rows = plan[1, :2 * N_EXPERTS]
    counts = plan[2, :N_EXPERTS]
    eid = ids[0:TOP_K]
    hit = eid[None] == jnp.arange(N_EXPERTS, dtype=jnp.int32)[:, None, None]
    dest = (jnp.sum(jnp.where(hit, pstart[:, None, None], 0), axis=0) + ids[TOP_K:2 * TOP_K]).T.reshape(nk)

    xs = _dispatch(h_packed, dest, fill_rows, n_blocks * blk, blk)
    yb = _expert_blocks(xs, counts, pstart, fill_rows, w1, w3, w2, blk)
    out = _combine(x1, seq, g2, gates, dest, yb)
    return out.reshape(bsz, seq, d)


def kernel(x, c, w_ada, b_ada, norm1_w, w_in, forget_bias, q_norm_w, k_norm_w, ret_norm_w, w_out, norm2_w,
           w_coarse, b_coarse, w_fine, b_fine, w1, w3, w2):
    c_in = c
    for l in range(w_ada.shape[0]):
        x = _layer(x, c_in, w_ada[l], b_ada[l], norm1_w[l], w_in[l], forget_bias[l], q_norm_w[l],
                   k_norm_w[l], ret_norm_w[l], w_out[l], norm2_w[l], w_coarse[l], b_coarse[l],
                   w_fine[l], b_fine[l], w1[l], w3[l], w2[l])
    return x
```

```python
import functools

import jax
import jax.numpy as jnp
import numpy as np
from jax import lax
from jax.experimental import pallas as pl
from jax.experimental.pallas import tpu as pltpu

HEAD_DIM = 128
N_GROUPS = 4
EXPERTS_PER_GROUP = 8
N_EXPERTS = N_GROUPS * EXPERTS_PER_GROUP
TOP_K = 2
ROPE_BASE = 10000.0
EPS = 1e-6

LANES = 128
VMEM_LIMIT = 56 * 1024 * 1024
NEG_BIG = -1e30
LOG2E = 1.4426950408889634
UNDERFLOW_LOG2 = 160.0

F32 = jnp.float32
BF16 = jnp.bfloat16
U32 = jnp.uint32


def _params(*sem):
    return pltpu.CompilerParams(dimension_semantics=sem, vmem_limit_bytes=VMEM_LIMIT)


def _silu(v):
    return v * (1.0 / (1.0 + jnp.exp(-v)))


def _pack_halves(y):
    w = y.shape[1] // 2
    lo = pltpu.bitcast(y[:, :w].astype(BF16).astype(F32), U32)
    hi = pltpu.bitcast(y[:, w:].astype(BF16).astype(F32), U32)
    return (hi & jnp.uint32(0xFFFF0000)) | (lo >> 16)


def _rows_to_tiles(p):
    return pltpu.einshape("m(ck)->mck", p, c=8, k=LANES)


def _tiles_to_rows(t):
    return pltpu.einshape("mck->m(ck)", t)


def _unpack_halves(p):
    lo = pltpu.bitcast(p << 16, F32)
    hi = pltpu.bitcast(p & jnp.uint32(0xFFFF0000), F32)
    return lo, hi


def _ada_kernel(ct_ref, w_ref, b_ref, o_ref):
    w = w_ref[...]
    rows = []
    for b in range(o_ref.shape[0]):
        if b < 2:
            cb = _silu(ct_ref[:, b:b + 1])
            rows.append(jnp.sum(cb * w, axis=0, keepdims=True) + b_ref[...])
        else:
            rows.append(jnp.zeros_like(b_ref[...]))
    o_ref[...] = jnp.concatenate(rows, axis=0)


def _ada_modulation(c, w_ada, b_ada):
    bsz, d = c.shape
    n = w_ada.shape[1]
    tn = 1024
    ct = jnp.zeros((d, LANES), F32).at[:, :bsz].set(c.T)
    out = pl.pallas_call(
        _ada_kernel,
        grid=(n // tn,),
        in_specs=[pl.BlockSpec((d, LANES), lambda j: (0, 0)),
                  pl.BlockSpec((d, tn), lambda j: (0, j)),
                  pl.BlockSpec((1, tn), lambda j: (0, j))],
        out_specs=pl.BlockSpec((8, tn), lambda j: (0, j)),
        out_shape=jax.ShapeDtypeStruct((8, n), F32),
        compiler_params=_params("arbitrary"),
    )(ct, w_ada, b_ada.reshape(1, n))
    return out[:bsz]


def _inproj_kernel(q_t, r_t, x_ref, nw_ref, sc_ref, sh_ref, wa_ref, wb_ref, wf_ref, cos_ref, sin_ref,
                   qw_ref, kw_ref, fb_ref, z_ref, f_ref, h_ref):
    j = pl.program_id(1)
    r0 = 3 * q_t

    @pl.when(j == 0)
    def _():
        x = x_ref[...]
        ms = jnp.mean(x * x, axis=-1, keepdims=True)
        y = x * lax.rsqrt(ms + EPS) * nw_ref[...]
        h = (y * (1.0 + sc_ref[0]) + sh_ref[0]).astype(BF16)
        h_ref[...] = h
        t = jnp.dot(h, wf_ref[...], preferred_element_type=F32) + fb_ref[...]
        f_ref[...] = jnp.minimum(t, 0.0) - jnp.log(1.0 + jnp.exp(-jnp.abs(t)))

    def heads_of(acc):
        return [acc[:, hh * HEAD_DIM:(hh + 1) * HEAD_DIM] for hh in range(acc.shape[1] // HEAD_DIM)]

    def head_norm(acc, w_row):
        outs = []
        for a in heads_of(acc):
            ms = jnp.mean(a * a, axis=-1, keepdims=True)
            outs.append(a * lax.rsqrt(ms + EPS) * w_row)
        return jnp.concatenate(outs, axis=-1).astype(BF16)

    def rotate(acc, scale):
        cs = cos_ref[...] * scale
        sn = sin_ref[...] * scale
        outs = [a * cs + pltpu.roll(a, HEAD_DIM // 2, 1) * sn for a in heads_of(acc)]
        return jnp.concatenate(outs, axis=-1).astype(BF16)

    def fox():
        return jnp.dot(h_ref[...], wa_ref[...], preferred_element_type=F32)

    def ret():
        return jnp.dot(h_ref[...], wb_ref[...], preferred_element_type=F32)

    @pl.when(j < q_t)
    def _():
        z_ref[...] = head_norm(fox(), qw_ref[...] * (LOG2E * HEAD_DIM ** -0.5))

    @pl.when((j >= q_t) & (j < 2 * q_t))
    def _():
        z_ref[...] = head_norm(fox(), kw_ref[...])

    @pl.when((j >= 2 * q_t) & (j < r0))
    def _():
        z_ref[...] = fox().astype(BF16)

    @pl.when((j >= r0) & (j < r0 + r_t))
    def _():
        z_ref[...] = rotate(ret(), 1.0)

    @pl.when((j >= r0 + r_t) & (j < r0 + 2 * r_t))
    def _():
        z_ref[...] = rotate(ret(), HEAD_DIM ** -0.5)

    @pl.when(j >= r0 + 2 * r_t)
    def _():
        z_ref[...] = ret().astype(BF16)


def _input_projection(x2d, seq, norm_w, sc1, sh1, w_fox, w_ret, w_f, cos_t, sin_t, qw, kw, fb):
    n, d = x2d.shape
    tm, tn = min(1024, seq), 1024
    fox_tiles = w_fox.shape[1] // tn
    ret_tiles = w_ret.shape[1] // tn
    tiles_per_seq = seq // tm
    kern = functools.partial(_inproj_kernel, fox_tiles // 3, ret_tiles // 4)
    bsel = lambda i, j: (i // tiles_per_seq, 0, 0)
    const = lambda i, j: (0, 0)
    return pl.pallas_call(
        kern,
        grid=(n // tm, fox_tiles + ret_tiles),
        in_specs=[pl.BlockSpec((tm, d), lambda i, j: (i, 0)),
                  pl.BlockSpec((1, d), const),
                  pl.BlockSpec((1, 1, d), bsel),
                  pl.BlockSpec((1, 1, d), bsel),
                  pl.BlockSpec((d, tn), lambda i, j: (0, jnp.minimum(j, fox_tiles - 1))),
                  pl.BlockSpec((d, tn), lambda i, j: (0, jnp.maximum(j - fox_tiles, 0))),
                  pl.BlockSpec((d, LANES), const),
                  pl.BlockSpec((tm, HEAD_DIM), lambda i, j: (i % tiles_per_seq, 0)),
                  pl.BlockSpec((tm, HEAD_DIM), lambda i, j: (i % tiles_per_seq, 0)),
                  pl.BlockSpec((1, HEAD_DIM), const),
                  pl.BlockSpec((1, HEAD_DIM), const),
                  pl.BlockSpec((1, LANES), const)],
        out_specs=[pl.BlockSpec((tm, tn), lambda i, j: (i, j)),
                   pl.BlockSpec((tm, LANES), lambda i, j: (i, 0))],
        out_shape=[jax.ShapeDtypeStruct((n, w_fox.shape[1] + w_ret.shape[1]), BF16),
                   jax.ShapeDtypeStruct((n, LANES), F32)],
        scratch_shapes=[pltpu.VMEM((tm, d), BF16)],
        compiler_params=_params("arbitrary", "arbitrary"),
    )(x2d, norm_w, sc1, sh1, w_fox, w_ret, w_f, cos_t, sin_t, qw, kw, fb)


def _cumsum_kernel(x_ref, o_ref):
    x = x_ref[0]
    r = x.shape[0]
    a = lax.broadcasted_iota(jnp.int32, (LANES, LANES), 0)
    b = lax.broadcasted_iota(jnp.int32, (LANES, LANES), 1)
    upper = (a <= b).astype(F32)
    within = jnp.dot(x, upper, precision=lax.Precision.HIGHEST, preferred_element_type=F32)
    tot = jnp.broadcast_to(within[:, LANES - 1:LANES], (r, LANES))
    ra = lax.broadcasted_iota(jnp.int32, (r, r), 0)
    rb = lax.broadcasted_iota(jnp.int32, (r, r), 1)
    strict = (rb < ra).astype(F32)
    before = jnp.dot(strict, tot, precision=lax.Precision.HIGHEST, preferred_element_type=F32)
    o_ref[0] = within + before


def _cumsum_rows(x):
    g, s = x.shape
    r = s // LANES
    out = pl.pallas_call(
        _cumsum_kernel,
        grid=(g,),
        in_specs=[pl.BlockSpec((1, r, LANES), lambda i: (i, 0, 0))],
        out_specs=pl.BlockSpec((1, r, LANES), lambda i: (i, 0, 0)),
        out_shape=jax.ShapeDtypeStruct((g, r, LANES), F32),
        compiler_params=_params("arbitrary"),
    )(x.reshape(g, r, LANES))
    return out.reshape(g, 1, s)


def _fox_kernel(tq, first_ref, q_ref, k_ref, v_ref, cum_ref, o_ref, sa_ref, sb_ref, m_ref, l_ref, acc_ref):
    qi = pl.program_id(2)
    q_start = pl.multiple_of(qi * tq, tq)
    c0 = cum_ref[0, :, pl.ds(q_start, LANES)][:, 0:1]

    m_ref[...] = jnp.full(m_ref.shape, NEG_BIG, F32)
    l_ref[...] = jnp.zeros(l_ref.shape, F32)
    acc_ref[...] = jnp.zeros(acc_ref.shape, F32)
    n_slabs = tq // LANES

    def scores(kb, s_ref):
        start = pl.multiple_of(kb * tq, tq)
        k = k_ref[pl.ds(start, tq), :]
        bias = (c0 - cum_ref[0, :, pl.ds(start, tq)]) * LOG2E
        s_ref[...] = lax.dot_general(q_ref[...], k, (((1,), (1,)), ((), ())),
                                     preferred_element_type=F32) + bias

    def softmax_pv(kb, s_ref, masked):
        start = pl.multiple_of(kb * tq, tq)
        v = v_ref[pl.ds(start, tq), :]
        slabs = []
        for j in range(n_slabs):
            t = s_ref[:, j * LANES:(j + 1) * LANES]
            if masked:
                row = lax.broadcasted_iota(jnp.int32, t.shape, 0)
                col = lax.broadcasted_iota(jnp.int32, t.shape, 1) + j * LANES
                t = jnp.where(col <= row, t, NEG_BIG)
            slabs.append(t)
        mx = slabs[0]
        for t in slabs[1:]:
            mx = jnp.maximum(mx, t)
        m_prev = m_ref[...]
        m_new = jnp.maximum(m_prev, jnp.max(mx, axis=-1, keepdims=True))
        alpha = jnp.exp2(m_prev - m_new)
        probs = [jnp.exp2(t - m_new) for t in slabs]
        psum = probs[0]
        for t in probs[1:]:
            psum = psum + t
        l_ref[...] = alpha * l_ref[...] + psum
        p = jnp.concatenate([t.astype(BF16) for t in probs], axis=-1)
        acc_ref[...] = alpha * acc_ref[...] + jnp.dot(p, v, preferred_element_type=F32)
        m_ref[...] = m_new

    first = first_ref[(pl.program_id(0) * pl.num_programs(1) + pl.program_id(1)) * pl.num_programs(2) + qi]
    n_off = qi - first
    scores(first, sa_ref)

    def pair(kb):
        scores(kb + 1, sb_ref)
        softmax_pv(kb, sa_ref, False)
        scores(kb + 2, sa_ref)
        softmax_pv(kb + 1, sb_ref, False)

    def body4(i, carry):
        pair(first + 4 * i)
        pair(first + 4 * i + 2)
        return carry

    def body2(i, carry):
        pair(first + 2 * i)
        return carry

    n4 = n_off // 4
    lax.fori_loop(0, n4, body4, 0)
    lax.fori_loop(2 * n4, n_off // 2, body2, 0)

    @pl.when(n_off % 2 == 0)
    def _():
        softmax_pv(qi, sa_ref, True)

    @pl.when(n_off % 2 == 1)
    def _():
        scores(qi, sb_ref)
        softmax_pv(qi - 1, sa_ref, False)
        softmax_pv(qi, sb_ref, True)

    o_ref[...] = (acc_ref[...] / jnp.sum(l_ref[...], axis=-1, keepdims=True)).astype(BF16)


def _first_live_block(cum, tq, qk_bound):
    c0 = cum[:, 0, ::tq]
    cend = cum[:, 0, tq - 1::tq]
    gap = (c0[:, :, None] - cend[:, None, :]) * LOG2E + 2.0 * qk_bound
    nq = c0.shape[1]
    earlier = jnp.arange(nq)[None, :] < jnp.arange(nq)[:, None]
    return jnp.sum((gap < -UNDERFLOW_LOG2) & earlier[None], axis=-1).astype(jnp.int32).reshape(-1)


def _fox_attention(z, cum, qk_bound, bsz, seq, n_heads):
    tq = min(512, seq)
    nq = seq // tq
    kern = functools.partial(_fox_kernel, tq)
    grid_spec = pltpu.PrefetchScalarGridSpec(
        num_scalar_prefetch=1,
        grid=(bsz, n_heads, nq),
        in_specs=[pl.BlockSpec((tq, HEAD_DIM), lambda b, h, i, f: (b * nq + i, h)),
                  pl.BlockSpec((seq, HEAD_DIM), lambda b, h, i, f: (b, n_heads + h)),
                  pl.BlockSpec((seq, HEAD_DIM), lambda b, h, i, f: (b, 2 * n_heads + h)),
                  pl.BlockSpec((1, 1, seq), lambda b, h, i, f: (b * n_heads + h, 0, 0))],
        out_specs=pl.BlockSpec((tq, HEAD_DIM), lambda b, h, i, f: (b * nq + i, h)),
        scratch_shapes=[pltpu.VMEM((tq, tq), F32), pltpu.VMEM((tq, tq), F32),
                        pltpu.VMEM((tq, LANES), F32), pltpu.VMEM((tq, LANES), F32),
                        pltpu.VMEM((tq, HEAD_DIM), F32)],
    )
    return pl.pallas_call(
        kern,
        grid_spec=grid_spec,
        out_shape=jax.ShapeDtypeStruct((bsz * seq, n_heads * HEAD_DIM), BF16),
        compiler_params=_params("arbitrary", "arbitrary", "arbitrary"),
    )(_first_live_block(cum, tq, qk_bound), z, z, z, cum)


def _ret_kernel(chunk, n_heads, lg_ref, q_ref, k_ref, v_ref, g_ref, nw_ref, o_ref, state_ref, decay_ref):
    first = (pl.program_id(0) == 0) & (pl.program_id(1) == 0)

    @pl.when(first)
    def _():
        i = lax.broadcasted_iota(jnp.int32, (chunk, chunk), 0)
        jj = lax.broadcasted_iota(jnp.int32, (chunk, chunk), 1)
        diff = (i - jj).astype(F32)
        for h in range(n_heads):
            decay_ref[h] = jnp.where(diff >= 0, jnp.exp(lg_ref[h] * jnp.maximum(diff, 0.0)), 0.0)

    @pl.when(pl.program_id(1) == 0)
    def _():
        state_ref[...] = jnp.zeros(state_ref.shape, F32)

    pos = lax.broadcasted_iota(jnp.int32, (chunk, HEAD_DIM), 0).astype(F32)
    for h in range(n_heads):
        log_g = lg_ref[h]
        cols = slice(h * HEAD_DIM, (h + 1) * HEAD_DIM)
        q = q_ref[:, cols]
        k = k_ref[:, cols]
        v = v_ref[:, cols]
        scores = lax.dot_general(q, k, (((1,), (1,)), ((), ())), preferred_element_type=F32)
        scores = scores * decay_ref[h]
        intra = jnp.dot(scores.astype(BF16), v, preferred_element_type=F32)
        state = state_ref[h]
        inter = jnp.dot(q, state.astype(BF16), preferred_element_type=F32) * jnp.exp(log_g * (pos + 1.0))
        kd = (k.astype(F32) * jnp.exp(log_g * (chunk - 1.0 - pos))).astype(BF16)
        kv = lax.dot_general(kd, v, (((0,), (0,)), ((), ())), preferred_element_type=F32)
        state_ref[h] = state * jnp.exp(jnp.full((1, HEAD_DIM), chunk, F32) * log_g) + kv
        o = intra + inter
        ms = jnp.mean(o * o, axis=-1, keepdims=True)
        o = o * lax.rsqrt(ms + EPS) * nw_ref[:, cols]
        o_ref[:, cols] = (o * _silu(g_ref[:, cols].astype(F32))).astype(BF16)


def _retention(z, log_g, norm_w, bsz, seq, n_heads, col0):
    chunk = min(256, seq)
    nt = seq // chunk
    width = n_heads * HEAD_DIM
    c0 = col0 // width
    kern = functools.partial(_ret_kernel, chunk, n_heads)

    def sec(s):
        return pl.BlockSpec((chunk, width), lambda b, t, lg: (b * nt + t, c0 + s))

    grid_spec = pltpu.PrefetchScalarGridSpec(
        num_scalar_prefetch=1,
        grid=(bsz, nt),
        in_specs=[sec(0), sec(1), sec(2), sec(3), pl.BlockSpec((1, width), lambda b, t, lg: (0, 0))],
        out_specs=pl.BlockSpec((chunk, width), lambda b, t, lg: (b * nt + t, 0)),
        scratch_shapes=[pltpu.VMEM((n_heads, HEAD_DIM, HEAD_DIM), F32),
                        pltpu.VMEM((n_heads, chunk, chunk), F32)],
    )
    return pl.pallas_call(
        kern,
        grid_spec=grid_spec,
        out_shape=jax.ShapeDtypeStruct((bsz * seq, width), BF16),
        compiler_params=_params("arbitrary", "arbitrary"),
    )(log_g, z, z, z, z, norm_w)


def _outproj_kernel(oa_ref, ob_ref, wa_ref, wb_ref, x_ref, g1_ref, nw_ref, sc_ref, sh_ref, wr_ref, br_ref,
                    x1_ref, hp_ref, lg_ref):
    mix = jnp.dot(oa_ref[...], wa_ref[...], preferred_element_type=F32)
    mix = mix + jnp.dot(ob_ref[...], wb_ref[...], preferred_element_type=F32)
    x1 = x_ref[...] + g1_ref[0] * mix
    x1_ref[...] = x1
    ms = jnp.mean(x1 * x1, axis=-1, keepdims=True)
    h2 = x1 * lax.rsqrt(ms + EPS) * nw_ref[...] * (1.0 + sc_ref[0]) + sh_ref[0]
    hp_ref[...] = _rows_to_tiles(_pack_halves(h2))
    h_hi = h2.astype(BF16)
    h_lo = (h2 - h_hi.astype(F32)).astype(BF16)
    both = jnp.dot(h_hi, wr_ref[...], preferred_element_type=F32)
    cross = jnp.dot(h_lo, wr_ref[:, :LANES], preferred_element_type=F32)
    lg_ref[...] = both[:, :LANES] + both[:, LANES:] + cross + br_ref[...]


def _output_projection(o_a, o_b, w_out, x2d, seq, g1, norm_w, sc2, sh2, w_router, b_router):
    n, d = x2d.shape
    da = o_a.shape[1]
    tm = min(512, seq)
    tiles_per_seq = seq // tm
    bsel = lambda i: (i // tiles_per_seq, 0, 0)
    return pl.pallas_call(
        _outproj_kernel,
        grid=(n // tm,),
        in_specs=[pl.BlockSpec((tm, da), lambda i: (i, 0)),
                  pl.BlockSpec((tm, da), lambda i: (i, 0)),
                  pl.BlockSpec((da, d), lambda i: (0, 0)),
                  pl.BlockSpec((da, d), lambda i: (1, 0)),
                  pl.BlockSpec((tm, d), lambda i: (i, 0)),
                  pl.BlockSpec((1, 1, d), bsel),
                  pl.BlockSpec((1, d), lambda i: (0, 0)),
                  pl.BlockSpec((1, 1, d), bsel),
                  pl.BlockSpec((1, 1, d), bsel),
                  pl.BlockSpec((d, 2 * LANES), lambda i: (0, 0)),
                  pl.BlockSpec((1, LANES), lambda i: (0, 0))],
        out_specs=[pl.BlockSpec((tm, d), lambda i: (i, 0)),
                   pl.BlockSpec((tm, d // 2 // LANES, LANES), lambda i: (i, 0, 0)),
                   pl.BlockSpec((tm, LANES), lambda i: (i, 0))],
        out_shape=[jax.ShapeDtypeStruct((n, d), F32),
                   jax.ShapeDtypeStruct((n, d // 2 // LANES, LANES), U32),
                   jax.ShapeDtypeStruct((n, LANES), F32)],
        compiler_params=_params("arbitrary"),
    )(o_a, o_b, w_out, w_out, x2d, g1, norm_w, sc2, sh2, w_router, b_router)


def _route_kernel(blk, n_blocks, lg_ref, gate_ref, ids_ref, plan_ref, run_ref):
    i = pl.program_id(0)

    @pl.when(i == 0)
    def _():
        run_ref[...] = jnp.zeros(run_ref.shape, F32)

    lg = lg_ref[...]
    tt = lg.shape[0]
    lane = lax.broadcasted_iota(jnp.int32, lg.shape, 1).astype(F32)
    big = 1e6

    def rmax(v):
        return jnp.max(v, axis=-1, keepdims=True)

    def rmin(v):
        return jnp.min(v, axis=-1, keepdims=True)

    def rsum(v):
        return jnp.sum(v, axis=-1, keepdims=True)

    cmask = lane < N_GROUPS
    cm = jnp.where(cmask, lg, NEG_BIG)
    ce = jnp.where(cmask, jnp.exp(cm - rmax(cm)), 0.0)
    pgrp = ce / rsum(ce)
    p_g = rmax(pgrp)
    g_sel = rmin(jnp.where(cmask & (pgrp == p_g), lane, big))

    lo = N_GROUPS + EXPERTS_PER_GROUP * g_sel
    fmask = (lane >= lo) & (lane < lo + EXPERTS_PER_GROUP)
    fm = jnp.where(fmask, lg, NEG_BIG)
    fe = jnp.where(fmask, jnp.exp(fm - rmax(fm)), 0.0)
    fp = fe / rsum(fe)
    fp = jnp.where(fmask, fp, -1.0)
    p1 = rmax(fp)
    i1 = rmin(jnp.where(fp == p1, lane, big))
    fp2 = jnp.where(lane == i1, -1.0, fp)
    p2 = rmax(fp2)
    i2 = rmin(jnp.where(fp2 == p2, lane, big))
    denom = p1 + p2
    w1 = p_g * p1 / denom
    w2 = p_g * p2 / denom
    e1 = i1 - N_GROUPS
    e2 = i2 - N_GROUPS

    gate_ref[...] = jnp.where(lane == 0, w1, jnp.where(lane == 1, w2, 0.0))

    oh1 = (lane == e1).astype(F32)
    oh2 = (lane == e2).astype(F32)
    both = oh1 + oh2
    ra = lax.broadcasted_iota(jnp.int32, (tt, tt), 0)
    rb = lax.broadcasted_iota(jnp.int32, (tt, tt), 1)
    strict = (rb < ra).astype(BF16)
    prefix = jnp.dot(strict, both.astype(BF16), preferred_element_type=F32) + run_ref[...]
    r1 = rsum(prefix * oh1)
    r2 = rsum(prefix * oh2)
    run_ref[...] = run_ref[...] + jnp.sum(both, axis=0, keepdims=True)

    packed = jnp.where(lane == 0, e1, jnp.where(lane == 1, e2, jnp.where(lane == 2, r1,
                                                                        jnp.where(lane == 3, r2, 0.0))))
    ids_ref[...] = jnp.transpose(packed)[:8, :].astype(jnp.int32)

    @pl.when(i == pl.num_programs(0) - 1)
    def _():
        cnt = jnp.broadcast_to(run_ref[...], (8, LANES))
        lane8 = lax.broadcasted_iota(jnp.int32, (8, LANES), 1)
        padded = jnp.floor((cnt + (blk - 1.0)) * (1.0 / blk)) * blk
        pend = padded
        for sh in (1, 2, 4, 8, 16, 32, 64):
            pend = pend + jnp.where(lane8 >= sh, pltpu.roll(pend, sh, 1), 0.0)
        pstart = pend - padded
        total = jnp.max(pend, axis=-1, keepdims=True)
        tail = total + (lane8 - N_EXPERTS).astype(F32) * blk
        fill = jnp.where(lane8 < N_EXPERTS, jnp.where(padded > 0, pend - blk, -1.0),
                         jnp.where((lane8 < 2 * N_EXPERTS) & (tail < n_blocks * blk), tail, -1.0))
        row8 = lax.broadcasted_iota(jnp.int32, (8, LANES), 0)
        plan_ref[...] = jnp.where(row8 == 0, pstart, jnp.where(row8 == 1, fill,
                                                               jnp.where(row8 == 2, cnt, 0.0))).astype(jnp.int32)


def _route(logits, blk, n_blocks):
    n = logits.shape[0]
    tt = min(512, n)
    blkspec = lambda: pl.BlockSpec((tt, LANES), lambda i: (i, 0))
    return pl.pallas_call(
        functools.partial(_route_kernel, blk, n_blocks),
        grid=(n // tt,),
        in_specs=[blkspec()],
        out_specs=[blkspec(),
                   pl.BlockSpec((8, tt), lambda i: (0, i)),
                   pl.BlockSpec((8, LANES), lambda i: (0, 0))],
        out_shape=[jax.ShapeDtypeStruct((n, LANES), F32),
                   jax.ShapeDtypeStruct((8, n), jnp.int32),
                   jax.ShapeDtypeStruct((8, LANES), jnp.int32)],
        scratch_shapes=[pltpu.VMEM((1, LANES), F32)],
        compiler_params=_params("arbitrary"),
    )(logits)


def _dispatch_kernel(tt, blk, n_fill, dest_ref, fill_ref, h_ref, xs_ref, zero_ref, sem, zsem):
    i = pl.program_id(0)
    base = i * (tt * TOP_K)

    @pl.when(i == 0)
    def _():
        zero_ref[...] = jnp.zeros(zero_ref.shape, U32)

        def zcopy(z):
            row = pl.multiple_of(jnp.maximum(fill_ref[z], 0), blk)
            return pltpu.make_async_copy(zero_ref, xs_ref.at[pl.ds(row, blk)], zsem)

        def zissue(z, carry):
            @pl.when(fill_ref[z] >= 0)
            def _():
                zcopy(z).start()
            return carry

        def zdrain(z, carry):
            @pl.when(fill_ref[z] >= 0)
            def _():
                zcopy(z).wait()
            return carry

        lax.fori_loop(0, n_fill, zissue, 0)
        lax.fori_loop(0, n_fill, zdrain, 0)

    def copy(r, kk):
        d = dest_ref[base + r * TOP_K + kk]
        return pltpu.make_async_copy(h_ref.at[r], xs_ref.at[d], sem)

    def issue(r, carry):
        for kk in range(TOP_K):
            copy(r, kk).start()
        return carry

    lax.fori_loop(0, tt, issue, 0, unroll=8)
    for _ in range(TOP_K):
        pltpu.make_async_copy(h_ref, xs_ref.at[pl.ds(0, tt)], sem).wait()


def _dispatch(h_packed, dest_flat, fill_rows, n_slots, blk):
    n = h_packed.shape[0]
    tile = h_packed.shape[1:]
    tt = min(512, n)
    n_fill = fill_rows.shape[0]
    grid_spec = pltpu.PrefetchScalarGridSpec(
        num_scalar_prefetch=2,
        grid=(n // tt,),
        in_specs=[pl.BlockSpec((tt,) + tile, lambda i, d, f: (i, 0, 0))],
        out_specs=pl.BlockSpec(memory_space=pl.ANY),
        scratch_shapes=[pltpu.VMEM((blk,) + tile, U32), pltpu.SemaphoreType.DMA(()), pltpu.SemaphoreType.DMA(())],
    )
    return pl.pallas_call(
        functools.partial(_dispatch_kernel, tt, blk, n_fill),
        grid_spec=grid_spec,
        out_shape=jax.ShapeDtypeStruct((n_slots,) + tile, U32),
        compiler_params=_params("arbitrary"),
    )(dest_flat, fill_rows, h_packed)


def _expert_kernel(blk, ahead, cnt_ref, pstart_ref, fill_ref, xs_ref, w1_ref, w3_ref, w2_ref, y_ref,
                   w1f, w3f, w2f, w1b, w3b, w2b, xbuf, ybuf, w_sem, in_sem, out_sem):
    e = pl.program_id(0)
    n_exp = pl.num_programs(0)
    wslot = e % 2
    n_blk = (cnt_ref[e] + (blk - 1)) // blk
    base = pstart_ref[e]
    n_x = xbuf.shape[0]

    def weight_copies(ex, slot):
        return [pltpu.make_async_copy(src.at[ex], dst.at[slot], w_sem.at[slot])
                for src, dst in ((w1_ref, w1f), (w3_ref, w3f), (w2_ref, w2f))]

    def rows(b):
        return pl.ds(pl.multiple_of(base + b * blk, blk), blk)

    def in_copy(b, slot):
        return pltpu.make_async_copy(xs_ref.at[rows(b)], xbuf.at[slot], in_sem.at[slot])

    def out_copy(b, slot):
        return pltpu.make_async_copy(ybuf.at[slot], y_ref.at[rows(b)], out_sem.at[slot])

    @pl.when(e == 0)
    def _():
        for c in weight_copies(0, 0):
            c.start()

    for p in range(ahead):
        @pl.when(p < n_blk)
        def _():
            in_copy(p, p).start()

    @pl.when(e + 1 < n_exp)
    def _():
        for c in weight_copies(e + 1, 1 - wslot):
            c.start()

    for c in weight_copies(e, wslot):
        c.wait()
    w1b[...] = w1f[wslot].astype(BF16)
    w3b[...] = w3f[wslot].astype(BF16)
    w2b[...] = w2f[wslot].astype(BF16)

    def body(b, carry):
        slot = b % 2

        @pl.when(b + ahead < n_blk)
        def _():
            in_copy(b + ahead, (b + ahead) % n_x).start()

        in_copy(b, b % n_x).wait()

        @pl.when(b >= 2)
        def _():
            out_copy(b - 2, slot).wait()

        lo, hi = _unpack_halves(_tiles_to_rows(xbuf[b % n_x]))
        lo = lo.astype(BF16)
        hi = hi.astype(BF16)
        half = lo.shape[1]
        a = jnp.dot(lo, w1b[:half, :], preferred_element_type=F32)
        a = a + jnp.dot(hi, w1b[half:, :], preferred_element_type=F32)
        g = jnp.dot(lo, w3b[:half, :], preferred_element_type=F32)
        g = g + jnp.dot(hi, w3b[half:, :], preferred_element_type=F32)
        mid = (_silu(a) * g).astype(BF16)
        ybuf[slot] = _rows_to_tiles(_pack_halves(jnp.dot(mid, w2b[...], preferred_element_type=F32)))
        out_copy(b, slot).start(priority=1)
        return carry

    lax.fori_loop(0, n_blk, body, 0)

    @pl.when(n_blk >= 2)
    def _():
        out_copy(n_blk - 2, n_blk % 2).wait()

    @pl.when(n_blk >= 1)
    def _():
        out_copy(n_blk - 1, (n_blk - 1) % 2).wait()

    @pl.when(e == pl.num_programs(0) - 1)
    def _():
        ybuf[0] = jnp.zeros(ybuf.shape[1:], U32)

        def zcopy(t):
            row = pl.multiple_of(jnp.maximum(fill_ref[N_EXPERTS + t], 0), blk)
            return pltpu.make_async_copy(ybuf.at[0], y_ref.at[pl.ds(row, blk)], out_sem.at[0])

        def zissue(t, carry):
            @pl.when(fill_ref[N_EXPERTS + t] >= 0)
            def _():
                zcopy(t).start()
            return carry

        def zdrain(t, carry):
            @pl.when(fill_ref[N_EXPERTS + t] >= 0)
            def _():
                zcopy(t).wait()
            return carry

        lax.fori_loop(0, N_EXPERTS, zissue, 0)
        lax.fori_loop(0, N_EXPERTS, zdrain, 0)


def _expert_blocks(xs, counts, pstart, fill_rows, w1, w3, w2, blk):
    n_slots = xs.shape[0]
    tile = xs.shape[1:]
    n_exp, d, de = w1.shape
    ahead = 3
    hbm = pl.BlockSpec(memory_space=pl.ANY)
    grid_spec = pltpu.PrefetchScalarGridSpec(
        num_scalar_prefetch=3,
        grid=(n_exp,),
        in_specs=[hbm, hbm, hbm, hbm],
        out_specs=hbm,
        scratch_shapes=[pltpu.VMEM((2, d, de), F32), pltpu.VMEM((2, d, de), F32), pltpu.VMEM((2, de, d), F32),
                        pltpu.VMEM((d, de), BF16), pltpu.VMEM((d, de), BF16), pltpu.VMEM((de, d), BF16),
                        pltpu.VMEM((ahead + 1, blk) + tile, U32), pltpu.VMEM((2, blk) + tile, U32),
                        pltpu.SemaphoreType.DMA((2,)), pltpu.SemaphoreType.DMA((ahead + 1,)),
                        pltpu.SemaphoreType.DMA((2,))],
    )
    return pl.pallas_call(
        functools.partial(_expert_kernel, blk, ahead),
        grid_spec=grid_spec,
        out_shape=jax.ShapeDtypeStruct((n_slots,) + tile, U32),
        compiler_params=_params("arbitrary"),
    )(counts, pstart, fill_rows, xs, w1, w3, w2)


def _combine_kernel(tt, n_tiles, dest_ref, x1_ref, g2_ref, gate_ref, yb_ref, o_ref, buf, sems):
    i = pl.program_id(0)

    def copy(tile, slot, r, kk):
        d = dest_ref[(tile * tt + r) * TOP_K + kk]
        return pltpu.make_async_copy(yb_ref.at[d], buf.at[slot, kk, r], sems.at[slot])

    def issue_tile(tile, slot):
        def body(r, carry):
            for kk in range(TOP_K):
                copy(tile, slot, r, kk).start()
            return carry
        lax.fori_loop(0, tt, body, 0, unroll=8)

    def wait_tile(tile, slot):
        for kk in range(TOP_K):
            pltpu.make_async_copy(yb_ref.at[pl.ds(0, tt)], buf.at[slot, kk], sems.at[slot]).wait()

    slot = i % 2

    @pl.when(i == 0)
    def _():
        issue_tile(0, 0)

    @pl.when(i + 1 < n_tiles)
    def _():
        issue_tile(i + 1, 1 - slot)

    wait_tile(i, slot)

    gate = gate_ref[...]
    wa = gate[:, 0:1]
    wb = gate[:, 1:2]
    lo_a, hi_a = _unpack_halves(_tiles_to_rows(buf[slot, 0]))
    lo_b, hi_b = _unpack_halves(_tiles_to_rows(buf[slot, 1]))
    y = jnp.concatenate([wa * lo_a + wb * lo_b, wa * hi_a + wb * hi_b], axis=-1)
    o_ref[...] = x1_ref[...] + g2_ref[0] * y


def _combine(x1, seq, g2, gates, dest_flat, yb):
    n, d = x1.shape
    tile = yb.shape[1:]
    tt = min(512, seq)
    n_tiles = n // tt
    tiles_per_seq = seq // tt
    grid_spec = pltpu.PrefetchScalarGridSpec(
        num_scalar_prefetch=1,
        grid=(n_tiles,),
        in_specs=[pl.BlockSpec((tt, d), lambda i, dr: (i, 0)),
                  pl.BlockSpec((1, 1, d), lambda i, dr: (i // tiles_per_seq, 0, 0)),
                  pl.BlockSpec((tt, LANES), lambda i, dr: (i, 0)),
                  pl.BlockSpec(memory_space=pl.ANY)],
        out_specs=pl.BlockSpec((tt, d), lambda i, dr: (i, 0)),
        scratch_shapes=[pltpu.VMEM((2, TOP_K, tt) + tile, U32), pltpu.SemaphoreType.DMA((2,))],
    )
    return pl.pallas_call(
        functools.partial(_combine_kernel, tt, n_tiles),
        grid_spec=grid_spec,
        out_shape=jax.ShapeDtypeStruct((n, d), F32),
        compiler_params=_params("arbitrary"),
    )(dest_flat, x1, g2, gates, yb)


def _rotation_tables(seq):
    half = HEAD_DIM // 2
    theta = ROPE_BASE ** (-np.arange(half, dtype=np.float64) / half)
    ang = np.arange(seq, dtype=np.float64)[:, None] * theta[None, :]
    cos_t = np.concatenate([np.cos(ang), np.cos(ang)], axis=-1).astype(np.float32)
    sin_t = np.concatenate([-np.sin(ang), np.sin(ang)], axis=-1).astype(np.float32)
    return jnp.asarray(cos_t), jnp.asarray(sin_t)


def _layer(x, c, w_ada, b_ada, norm1_w, w_in, forget_bias, q_norm_w, k_norm_w, ret_norm_w, w_out, norm2_w,
           w_coarse, b_coarse, w_fine, b_fine, w1, w3, w2):
    bsz, seq, d = x.shape
    n = bsz * seq
    d_fox = d // 2
    d_ret = d // 2
    n_heads = d_fox // HEAD_DIM

    mod = _ada_modulation(c, w_ada, b_ada)
    sh1, sc1, g1, sh2, sc2, g2 = [m.reshape(bsz, 1, d) for m in jnp.split(mod, 6, axis=-1)]

    f0 = 3 * d_fox
    w_fox = w_in[:, :f0].astype(BF16)
    w_ret = w_in[:, f0 + n_heads:].astype(BF16)
    w_f = jnp.zeros((d, LANES), BF16).at[:, :n_heads].set(w_in[:, f0:f0 + n_heads].astype(BF16))
    fb = jnp.zeros((1, LANES), F32).at[0, :n_heads].set(forget_bias)

    cos_t, sin_t = _rotation_tables(seq)

    x2d = x.reshape(n, d)
    z, log_f = _input_projection(x2d, seq, norm1_w.reshape(1, d), sc1, sh1, w_fox, w_ret, w_f, cos_t, sin_t,
                                 q_norm_w.reshape(1, HEAD_DIM), k_norm_w.reshape(1, HEAD_DIM), fb)

    lf = log_f[:, :n_heads].reshape(bsz, seq, n_heads).transpose(0, 2, 1).reshape(bsz * n_heads, seq)
    cum = _cumsum_rows(lf)

    qk_bound = 1.02 * LOG2E * HEAD_DIM ** 0.5 * jnp.max(jnp.abs(q_norm_w)) * jnp.max(jnp.abs(k_norm_w))
    o_a = _fox_attention(z, cum, qk_bound, bsz, seq, n_heads)
    log_g = jnp.log(1.0 - 2.0 ** (-5.0 - jnp.arange(n_heads, dtype=F32)))
    o_b = _retention(z, log_g, ret_norm_w.reshape(1, d_ret), bsz, seq, n_heads, 3 * d_fox)

    w_router = jnp.zeros((d, LANES), F32)
    w_router = w_router.at[:, :N_GROUPS].set(w_coarse)
    w_router = w_router.at[:, N_GROUPS:N_GROUPS + N_EXPERTS].set(
        w_fine.transpose(1, 0, 2).reshape(d, N_EXPERTS))
    b_router = jnp.zeros((1, LANES), F32)
    b_router = b_router.at[0, :N_GROUPS].set(b_coarse)
    b_router = b_router.at[0, N_GROUPS:N_GROUPS + N_EXPERTS].set(b_fine.reshape(N_EXPERTS))

    wr_hi = w_router.astype(BF16)
    wr_lo = (w_router - wr_hi.astype(F32)).astype(BF16)
    x1, h_packed, logits = _output_projection(o_a, o_b, w_out.astype(BF16), x2d, seq, g1,
                                              norm2_w.reshape(1, d), sc2, sh2,
                                              jnp.concatenate([wr_hi, wr_lo], axis=1), b_router)

    blk = 256
    nk = n * TOP_K
    n_blocks = nk // blk + N_EXPERTS
    gates, ids, plan = _route(logits, blk, n_blocks)
    pstart = plan[0, :N_EXPERTS]
    fill_rows = plan[1, :2 * N_EXPERTS]
    counts = plan[2, :N_EXPERTS]
    eid = ids[0:TOP_K]
    hit = eid[None] == jnp.arange(N_EXPERTS, dtype=jnp.int32)[:, None, None]
    dest = (jnp.sum(jnp.where(hit, pstart[:, None, None], 0), axis=0) + ids[TOP_K:2 * TOP_K]).T.reshape(nk)

    xs = _dispatch(h_packed, dest, fill_rows, n_blocks * blk, blk)
    yb = _expert_blocks(xs, counts, pstart, fill_rows, w1, w3, w2, blk)
    out = _combine(x1, seq, g2, gates, dest, yb)
    return out.reshape(bsz, seq, d)


def kernel(x, c, w_ada, b_ada, norm1_w, w_in, forget_bias, q_norm_w, k_norm_w, ret_norm_w, w_out, norm2_w,
           w_coarse, b_coarse, w_fine, b_fine, w1, w3, w2):
    c_in = c
    for l in range(w_ada.shape[0]):
        x = _layer(x, c_in, w_ada[l], b_ada[l], norm1_w[l], w_in[l], forget_bias[l], q_norm_w[l],
                   k_norm_w[l], ret_norm_w[l], w_out[l], norm2_w[l], w_coarse[l], b_coarse[l],
                   w_fine[l], b_fine[l], w1[l], w3[l], w2[l])
    return x
```

```python
import functools

import jax
import jax.numpy as jnp
import numpy as np
from jax import lax
from jax.experimental import pallas as pl
from jax.experimental.pallas import tpu as pltpu

HEAD_DIM = 128
N_GROUPS = 4
EXPERTS_PER_GROUP = 8
N_EXPERTS = N_GROUPS * EXPERTS_PER_GROUP
TOP_K = 2
ROPE_BASE = 10000.0
EPS = 1e-6

LANES = 128
VMEM_LIMIT = 56 * 1024 * 1024
NEG_BIG = -1e30
LOG2E = 1.4426950408889634
UNDERFLOW_LOG2 = 160.0

F32 = jnp.float32
BF16 = jnp.bfloat16
U32 = jnp.uint32


def _params(*sem):
    return pltpu.CompilerParams(dimension_semantics=sem, vmem_limit_bytes=VMEM_LIMIT)


def _silu(v):
    return v * (1.0 / (1.0 + jnp.exp(-v)))


def _pack_halves(y):
    w = y.shape[1] // 2
    lo = pltpu.bitcast(y[:, :w].astype(BF16).astype(F32), U32)
    hi = pltpu.bitcast(y[:, w:].astype(BF16).astype(F32), U32)
    return (hi & jnp.uint32(0xFFFF0000)) | (lo >> 16)


def _rows_to_tiles(p):
    return pltpu.einshape("m(ck)->mck", p, c=8, k=LANES)


def _tiles_to_rows(t):
    return pltpu.einshape("mck->m(ck)", t)


def _unpack_halves(p):
    lo = pltpu.bitcast(p << 16, F32)
    hi = pltpu.bitcast(p & jnp.uint32(0xFFFF0000), F32)
    return lo, hi


def _ada_kernel(ct_ref, w_ref, b_ref, o_ref):
    w = w_ref[...]
    rows = []
    for b in range(o_ref.shape[0]):
        if b < 2:
            cb = _silu(ct_ref[:, b:b + 1])
            rows.append(jnp.sum(cb * w, axis=0, keepdims=True) + b_ref[...])
        else:
            rows.append(jnp.zeros_like(b_ref[...]))
    o_ref[...] = jnp.concatenate(rows, axis=0)


def _ada_modulation(c, w_ada, b_ada):
    bsz, d = c.shape
    n = w_ada.shape[1]
    tn = 1024
    ct = jnp.zeros((d, LANES), F32).at[:, :bsz].set(c.T)
    out = pl.pallas_call(
        _ada_kernel,
        grid=(n // tn,),
        in_specs=[pl.BlockSpec((d, LANES), lambda j: (0, 0)),
                  pl.BlockSpec((d, tn), lambda j: (0, j)),
                  pl.BlockSpec((1, tn), lambda j: (0, j))],
        out_specs=pl.BlockSpec((8, tn), lambda j: (0, j)),
        out_shape=jax.ShapeDtypeStruct((8, n), F32),
        compiler_params=_params("arbitrary"),
    )(ct, w_ada, b_ada.reshape(1, n))
    return out[:bsz]


def _inproj_kernel(q_t, r_t, x_ref, nw_ref, sc_ref, sh_ref, wa_ref, wb_ref, wf_ref, cos_ref, sin_ref,
                   qw_ref, kw_ref, fb_ref, z_ref, f_ref, h_ref):
    j = pl.program_id(1)
    r0 = 3 * q_t

    @pl.when(j == 0)
    def _():
        x = x_ref[...]
        ms = jnp.mean(x * x, axis=-1, keepdims=True)
        y = x * lax.rsqrt(ms + EPS) * nw_ref[...]
        h = (y * (1.0 + sc_ref[0]) + sh_ref[0]).astype(BF16)
        h_ref[...] = h
        t = jnp.dot(h, wf_ref[...], preferred_element_type=F32) + fb_ref[...]
        f_ref[...] = jnp.minimum(t, 0.0) - jnp.log(1.0 + jnp.exp(-jnp.abs(t)))

    def heads_of(acc):
        return [acc[:, hh * HEAD_DIM:(hh + 1) * HEAD_DIM] for hh in range(acc.shape[1] // HEAD_DIM)]

    def head_norm(acc, w_row):
        outs = []
        for a in heads_of(acc):
            ms = jnp.mean(a * a, axis=-1, keepdims=True)
            outs.append(a * lax.rsqrt(ms + EPS) * w_row)
        return jnp.concatenate(outs, axis=-1).astype(BF16)

    def rotate(acc, scale):
        cs = cos_ref[...] * scale
        sn = sin_ref[...] * scale
        outs = [a * cs + pltpu.roll(a, HEAD_DIM // 2, 1) * sn for a in heads_of(acc)]
        return jnp.concatenate(outs, axis=-1).astype(BF16)

    def fox():
        return jnp.dot(h_ref[...], wa_ref[...], preferred_element_type=F32)

    def ret():
        return jnp.dot(h_ref[...], wb_ref[...], preferred_element_type=F32)

    @pl.when(j < q_t)
    def _():
        z_ref[...] = head_norm(fox(), qw_ref[...] * (LOG2E * HEAD_DIM ** -0.5))

    @pl.when((j >= q_t) & (j < 2 * q_t))
    def _():
        z_ref[...] = head_norm(fox(), kw_ref[...])

    @pl.when((j >= 2 * q_t) & (j < r0))
    def _():
        z_ref[...] = fox().astype(BF16)

    @pl.when((j >= r0) & (j < r0 + r_t))
    def _():
        z_ref[...] = rotate(ret(), 1.0)

    @pl.when((j >= r0 + r_t) & (j < r0 + 2 * r_t))
    def _():
        z_ref[...] = rotate(ret(), HEAD_DIM ** -0.5)

    @pl.when(j >= r0 + 2 * r_t)
    def _():
        z_ref[...] = ret().astype(BF16)


def _input_projection(x2d, seq, norm_w, sc1, sh1, w_fox, w_ret, w_f, cos_t, sin_t, qw, kw, fb):
    n, d = x2d.shape
    tm, tn = min(1024, seq), 1024
    fox_tiles = w_fox.shape[1] // tn
    ret_tiles = w_ret.shape[1] // tn
    tiles_per_seq = seq // tm
    kern = functools.partial(_inproj_kernel, fox_tiles // 3, ret_tiles // 4)
    bsel = lambda i, j: (i // tiles_per_seq, 0, 0)
    const = lambda i, j: (0, 0)
    return pl.pallas_call(
        kern,
        grid=(n // tm, fox_tiles + ret_tiles),
        in_specs=[pl.BlockSpec((tm, d), lambda i, j: (i, 0)),
                  pl.BlockSpec((1, d), const),
                  pl.BlockSpec((1, 1, d), bsel),
                  pl.BlockSpec((1, 1, d), bsel),
                  pl.BlockSpec((d, tn), lambda i, j: (0, jnp.minimum(j, fox_tiles - 1))),
                  pl.BlockSpec((d, tn), lambda i, j: (0, jnp.maximum(j - fox_tiles, 0))),
                  pl.BlockSpec((d, LANES), const),
                  pl.BlockSpec((tm, HEAD_DIM), lambda i, j: (i % tiles_per_seq, 0)),
                  pl.BlockSpec((tm, HEAD_DIM), lambda i, j: (i % tiles_per_seq, 0)),
                  pl.BlockSpec((1, HEAD_DIM), const),
                  pl.BlockSpec((1, HEAD_DIM), const),
                  pl.BlockSpec((1, LANES), const)],
        out_specs=[pl.BlockSpec((tm, tn), lambda i, j: (i, j)),
                   pl.BlockSpec((tm, LANES), lambda i, j: (i, 0))],
        out_shape=[jax.ShapeDtypeStruct((n, w_fox.shape[1] + w_ret.shape[1]), BF16),
                   jax.ShapeDtypeStruct((n, LANES), F32)],
        scratch_shapes=[pltpu.VMEM((tm, d), BF16)],
        compiler_params=_params("arbitrary", "arbitrary"),
    )(x2d, norm_w, sc1, sh1, w_fox, w_ret, w_f, cos_t, sin_t, qw, kw, fb)


def _cumsum_kernel(x_ref, o_ref):
    x = x_ref[0]
    r = x.shape[0]
    a = lax.broadcasted_iota(jnp.int32, (LANES, LANES), 0)
    b = lax.broadcasted_iota(jnp.int32, (LANES, LANES), 1)
    upper = (a <= b).astype(F32)
    within = jnp.dot(x, upper, precision=lax.Precision.HIGHEST, preferred_element_type=F32)
    tot = jnp.broadcast_to(within[:, LANES - 1:LANES], (r, LANES))
    ra = lax.broadcasted_iota(jnp.int32, (r, r), 0)
    rb = lax.broadcasted_iota(jnp.int32, (r, r), 1)
    strict = (rb < ra).astype(F32)
    before = jnp.dot(strict, tot, precision=lax.Precision.HIGHEST, preferred_element_type=F32)
    o_ref[0] = within + before


def _cumsum_rows(x):
    g, s = x.shape
    r = s // LANES
    out = pl.pallas_call(
        _cumsum_kernel,
        grid=(g,),
        in_specs=[pl.BlockSpec((1, r, LANES), lambda i: (i, 0, 0))],
        out_specs=pl.BlockSpec((1, r, LANES), lambda i: (i, 0, 0)),
        out_shape=jax.ShapeDtypeStruct((g, r, LANES), F32),
        compiler_params=_params("arbitrary"),
    )(x.reshape(g, r, LANES))
    return out.reshape(g, 1, s)


def _fox_kernel(tq, first_ref, q_ref, k_ref, v_ref, cum_ref, o_ref, s_refs, m_ref, l_ref, acc_ref):
    pair_id = pl.program_id(2)
    n_pairs = pl.num_programs(2)
    head = pl.program_id(0) * pl.num_programs(1) + pl.program_id(1)
    n_slabs = tq // LANES

    m_ref[...] = jnp.full(m_ref.shape, NEG_BIG, F32)
    l_ref[...] = jnp.zeros(l_ref.shape, F32)
    acc_ref[...] = jnp.zeros(acc_ref.shape, F32)

    class Sub:
        def __init__(self, idx):
            self.rows = slice(idx * tq, (idx + 1) * tq)
            self.qi = 2 * pair_id + idx
            self.sa, self.sb = s_refs[2 * idx], s_refs[2 * idx + 1]
            q_start = pl.multiple_of(self.qi * tq, tq)
            self.c0 = cum_ref[0, :, pl.ds(q_start, LANES)][:, 0:1]
            self.first = first_ref[(head * n_pairs + pair_id) * 2 + idx]
            self.n_off = self.qi - self.first

    def scores(sub, kb, s_ref):
        start = pl.multiple_of(kb * tq, tq)
        k = k_ref[pl.ds(start, tq), :]
        bias = (sub.c0 - cum_ref[0, :, pl.ds(start, tq)]) * LOG2E
        s_ref[...] = lax.dot_general(q_ref[sub.rows, :], k, (((1,), (1,)), ((), ())),
                                     preferred_element_type=F32) + bias

    def softmax_pv(sub, kb, s_ref, masked):
        rs = sub.rows
        start = pl.multiple_of(kb * tq, tq)
        v = v_ref[pl.ds(start, tq), :]
        slabs = []
        for j in range(n_slabs):
            t = s_ref[:, j * LANES:(j + 1) * LANES]
            if masked:
                row = lax.broadcasted_iota(jnp.int32, t.shape, 0)
                col = lax.broadcasted_iota(jnp.int32, t.shape, 1) + j * LANES
                t = jnp.where(col <= row, t, NEG_BIG)
            slabs.append(t)
        mx = slabs[0]
        for t in slabs[1:]:
            mx = jnp.maximum(mx, t)
        m_prev = m_ref[rs, :]
        m_new = jnp.maximum(m_prev, jnp.max(mx, axis=-1, keepdims=True))
        alpha = jnp.exp2(m_prev - m_new)
        probs = [jnp.exp2(t - m_new) for t in slabs]
        psum = probs[0]
        for t in probs[1:]:
            psum = psum + t
        l_ref[rs, :] = alpha * l_ref[rs, :] + psum
        p = jnp.concatenate([t.astype(BF16) for t in probs], axis=-1)
        acc_ref[rs, :] = alpha * acc_ref[rs, :] + jnp.dot(p, v, preferred_element_type=F32)
        m_ref[rs, :] = m_new

    def sweep(sub, then):
        def pair(kb):
            scores(sub, kb + 1, sub.sb)
            softmax_pv(sub, kb, sub.sa, False)
            scores(sub, kb + 2, sub.sa)
            softmax_pv(sub, kb + 1, sub.sb, False)

        def body4(i, carry):
            pair(sub.first + 4 * i)
            pair(sub.first + 4 * i + 2)
            return carry

        def body2(i, carry):
            pair(sub.first + 2 * i)
            return carry

        n4 = sub.n_off // 4
        lax.fori_loop(0, n4, body4, 0)
        lax.fori_loop(2 * n4, sub.n_off // 2, body2, 0)

        @pl.when(sub.n_off % 2 == 0)
        def _():
            then()
            softmax_pv(sub, sub.qi, sub.sa, True)

        @pl.when(sub.n_off % 2 == 1)
        def _():
            scores(sub, sub.qi, sub.sb)
            softmax_pv(sub, sub.qi - 1, sub.sa, False)
            then()
            softmax_pv(sub, sub.qi, sub.sb, True)

    first_q, second_q = Sub(0), Sub(1)
    scores(first_q, first_q.first, first_q.sa)
    sweep(first_q, lambda: scores(second_q, second_q.first, second_q.sa))
    sweep(second_q, lambda: None)

    o_ref[...] = (acc_ref[...] / jnp.sum(l_ref[...], axis=-1, keepdims=True)).astype(BF16)


def _first_live_block(cum, tq, qk_bound):
    c0 = cum[:, 0, ::tq]
    cend = cum[:, 0, tq - 1::tq]
    gap = (c0[:, :, None] - cend[:, None, :]) * LOG2E + 2.0 * qk_bound
    nq = c0.shape[1]
    earlier = jnp.arange(nq)[None, :] < jnp.arange(nq)[:, None]
    return jnp.sum((gap < -UNDERFLOW_LOG2) & earlier[None], axis=-1).astype(jnp.int32).reshape(-1)


def _fox_attention(z, cum, qk_bound, bsz, seq, n_heads):
    tq = min(512, seq // 2)
    nq = seq // tq
    n_pairs = nq // 2
    kern = functools.partial(_fox_kernel, tq)
    grid_spec = pltpu.PrefetchScalarGridSpec(
        num_scalar_prefetch=1,
        grid=(bsz, n_heads, n_pairs),
        in_specs=[pl.BlockSpec((2 * tq, HEAD_DIM), lambda b, h, i, f: (b * n_pairs + i, h)),
                  pl.BlockSpec((seq, HEAD_DIM), lambda b, h, i, f: (b, n_heads + h)),
                  pl.BlockSpec((seq, HEAD_DIM), lambda b, h, i, f: (b, 2 * n_heads + h)),
                  pl.BlockSpec((1, 1, seq), lambda b, h, i, f: (b * n_heads + h, 0, 0))],
        out_specs=pl.BlockSpec((2 * tq, HEAD_DIM), lambda b, h, i, f: (b * n_pairs + i, h)),
        scratch_shapes=[[pltpu.VMEM((tq, tq), F32)] * 4,
                        pltpu.VMEM((2 * tq, LANES), F32), pltpu.VMEM((2 * tq, LANES), F32),
                        pltpu.VMEM((2 * tq, HEAD_DIM), F32)],
    )
    return pl.pallas_call(
        kern,
        grid_spec=grid_spec,
        out_shape=jax.ShapeDtypeStruct((bsz * seq, n_heads * HEAD_DIM), BF16),
        compiler_params=_params("arbitrary", "arbitrary", "arbitrary"),
    )(_first_live_block(cum, tq, qk_bound), z, z, z, cum)


def _ret_kernel(chunk, n_heads, lg_ref, q_ref, k_ref, v_ref, g_ref, nw_ref, o_ref, state_ref, decay_ref):
    first = (pl.program_id(0) == 0) & (pl.program_id(1) == 0)

    @pl.when(first)
    def _():
        i = lax.broadcasted_iota(jnp.int32, (chunk, chunk), 0)
        jj = lax.broadcasted_iota(jnp.int32, (chunk, chunk), 1)
        diff = (i - jj).astype(F32)
        for h in range(n_heads):
            decay_ref[h] = jnp.where(diff >= 0, jnp.exp(lg_ref[h] * jnp.maximum(diff, 0.0)), 0.0)

    @pl.when(pl.program_id(1) == 0)
    def _():
        state_ref[...] = jnp.zeros(state_ref.shape, F32)

    pos = lax.broadcasted_iota(jnp.int32, (chunk, HEAD_DIM), 0).astype(F32)
    for h in range(n_heads):
        log_g = lg_ref[h]
        cols = slice(h * HEAD_DIM, (h + 1) * HEAD_DIM)
        q = q_ref[:, cols]
        k = k_ref[:, cols]
        v = v_ref[:, cols]
        scores = lax.dot_general(q, k, (((1,), (1,)), ((), ())), preferred_element_type=F32)
        scores = scores * decay_ref[h]
        intra = jnp.dot(scores.astype(BF16), v, preferred_element_type=F32)
        state = state_ref[h]
        inter = jnp.dot(q, state.astype(BF16), preferred_element_type=F32) * jnp.exp(log_g * (pos + 1.0))
        kd = (k.astype(F32) * jnp.exp(log_g * (chunk - 1.0 - pos))).astype(BF16)
        kv = lax.dot_general(kd, v, (((0,), (0,)), ((), ())), preferred_element_type=F32)
        state_ref[h] = state * jnp.exp(jnp.full((1, HEAD_DIM), chunk, F32) * log_g) + kv
        o = intra + inter
        ms = jnp.mean(o * o, axis=-1, keepdims=True)
        o = o * lax.rsqrt(ms + EPS) * nw_ref[:, cols]
        o_ref[:, cols] = (o * _silu(g_ref[:, cols].astype(F32))).astype(BF16)


def _retention(z, log_g, norm_w, bsz, seq, n_heads, col0):
    chunk = min(256, seq)
    nt = seq // chunk
    width = n_heads * HEAD_DIM
    c0 = col0 // width
    kern = functools.partial(_ret_kernel, chunk, n_heads)

    def sec(s):
        return pl.BlockSpec((chunk, width), lambda b, t, lg: (b * nt + t, c0 + s))

    grid_spec = pltpu.PrefetchScalarGridSpec(
        num_scalar_prefetch=1,
        grid=(bsz, nt),
        in_specs=[sec(0), sec(1), sec(2), sec(3), pl.BlockSpec((1, width), lambda b, t, lg: (0, 0))],
        out_specs=pl.BlockSpec((chunk, width), lambda b, t, lg: (b * nt + t, 0)),
        scratch_shapes=[pltpu.VMEM((n_heads, HEAD_DIM, HEAD_DIM), F32),
                        pltpu.VMEM((n_heads, chunk, chunk), F32)],
    )
    return pl.pallas_call(
        kern,
        grid_spec=grid_spec,
        out_shape=jax.ShapeDtypeStruct((bsz * seq, width), BF16),
        compiler_params=_params("arbitrary", "arbitrary"),
    )(log_g, z, z, z, z, norm_w)


def _outproj_kernel(oa_ref, ob_ref, wa_ref, wb_ref, x_ref, g1_ref, nw_ref, sc_ref, sh_ref, wr_ref, br_ref,
                    x1_ref, hp_ref, lg_ref):
    mix = jnp.dot(oa_ref[...], wa_ref[...], preferred_element_type=F32)
    mix = mix + jnp.dot(ob_ref[...], wb_ref[...], preferred_element_type=F32)
    x1 = x_ref[...] + g1_ref[0] * mix
    x1_ref[...] = x1
    ms = jnp.mean(x1 * x1, axis=-1, keepdims=True)
    h2 = x1 * lax.rsqrt(ms + EPS) * nw_ref[...] * (1.0 + sc_ref[0]) + sh_ref[0]
    hp_ref[...] = _rows_to_tiles(_pack_halves(h2))
    h_hi = h2.astype(BF16)
    h_lo = (h2 - h_hi.astype(F32)).astype(BF16)
    both = jnp.dot(h_hi, wr_ref[...], preferred_element_type=F32)
    cross = jnp.dot(h_lo, wr_ref[:, :LANES], preferred_element_type=F32)
    lg_ref[...] = both[:, :LANES] + both[:, LANES:] + cross + br_ref[...]


def _output_projection(o_a, o_b, w_out, x2d, seq, g1, norm_w, sc2, sh2, w_router, b_router):
    n, d = x2d.shape
    da = o_a.shape[1]
    tm = min(512, seq)
    tiles_per_seq = seq // tm
    bsel = lambda i: (i // tiles_per_seq, 0, 0)
    return pl.pallas_call(
        _outproj_kernel,
        grid=(n // tm,),
        in_specs=[pl.BlockSpec((tm, da), lambda i: (i, 0)),
                  pl.BlockSpec((tm, da), lambda i: (i, 0)),
                  pl.BlockSpec((da, d), lambda i: (0, 0)),
                  pl.BlockSpec((da, d), lambda i: (1, 0)),
                  pl.BlockSpec((tm, d), lambda i: (i, 0)),
                  pl.BlockSpec((1, 1, d), bsel),
                  pl.BlockSpec((1, d), lambda i: (0, 0)),
                  pl.BlockSpec((1, 1, d), bsel),
                  pl.BlockSpec((1, 1, d), bsel),
                  pl.BlockSpec((d, 2 * LANES), lambda i: (0, 0)),
                  pl.BlockSpec((1, LANES), lambda i: (0, 0))],
        out_specs=[pl.BlockSpec((tm, d), lambda i: (i, 0)),
                   pl.BlockSpec((tm, d // 2 // LANES, LANES), lambda i: (i, 0, 0)),
                   pl.BlockSpec((tm, LANES), lambda i: (i, 0))],
        out_shape=[jax.ShapeDtypeStruct((n, d), F32),
                   jax.ShapeDtypeStruct((n, d // 2 // LANES, LANES), U32),
                   jax.ShapeDtypeStruct((n, LANES), F32)],
        compiler_params=_params("arbitrary"),
    )(o_a, o_b, w_out, w_out, x2d, g1, norm_w, sc2, sh2, w_router, b_router)


def _route_kernel(blk, n_blocks, lg_ref, gate_ref, ids_ref, plan_ref, run_ref):
    i = pl.program_id(0)

    @pl.when(i == 0)
    def _():
        run_ref[...] = jnp.zeros(run_ref.shape, F32)

    lg = lg_ref[...]
    tt = lg.shape[0]
    lane = lax.broadcasted_iota(jnp.int32, lg.shape, 1).astype(F32)
    big = 1e6

    def rmax(v):
        return jnp.max(v, axis=-1, keepdims=True)

    def rmin(v):
        return jnp.min(v, axis=-1, keepdims=True)

    def rsum(v):
        return jnp.sum(v, axis=-1, keepdims=True)

    cmask = lane < N_GROUPS
    cm = jnp.where(cmask, lg, NEG_BIG)
    ce = jnp.where(cmask, jnp.exp(cm - rmax(cm)), 0.0)
    pgrp = ce / rsum(ce)
    p_g = rmax(pgrp)
    g_sel = rmin(jnp.where(cmask & (pgrp == p_g), lane, big))

    lo = N_GROUPS + EXPERTS_PER_GROUP * g_sel
    fmask = (lane >= lo) & (lane < lo + EXPERTS_PER_GROUP)
    fm = jnp.where(fmask, lg, NEG_BIG)
    fe = jnp.where(fmask, jnp.exp(fm - rmax(fm)), 0.0)
    fp = fe / rsum(fe)
    fp = jnp.where(fmask, fp, -1.0)
    p1 = rmax(fp)
    i1 = rmin(jnp.where(fp == p1, lane, big))
    fp2 = jnp.where(lane == i1, -1.0, fp)
    p2 = rmax(fp2)
    i2 = rmin(jnp.where(fp2 == p2, lane, big))
    denom = p1 + p2
    w1 = p_g * p1 / denom
    w2 = p_g * p2 / denom
    e1 = i1 - N_GROUPS
    e2 = i2 - N_GROUPS

    gate_ref[...] = jnp.where(lane == 0, w1, jnp.where(lane == 1, w2, 0.0))

    oh1 = (lane == e1).astype(F32)
    oh2 = (lane == e2).astype(F32)
    both = oh1 + oh2
    ra = lax.broadcasted_iota(jnp.int32, (tt, tt), 0)
    rb = lax.broadcasted_iota(jnp.int32, (tt, tt), 1)
    strict = (rb < ra).astype(BF16)
    prefix = jnp.dot(strict, both.astype(BF16), preferred_element_type=F32) + run_ref[...]
    r1 = rsum(prefix * oh1)
    r2 = rsum(prefix * oh2)
    run_ref[...] = run_ref[...] + jnp.sum(both, axis=0, keepdims=True)

    packed = jnp.where(lane == 0, e1, jnp.where(lane == 1, e2, jnp.where(lane == 2, r1,
                                                                        jnp.where(lane == 3, r2, 0.0))))
    ids_ref[...] = jnp.transpose(packed)[:8, :].astype(jnp.int32)

    @pl.when(i == pl.num_programs(0) - 1)
    def _():
        cnt = jnp.broadcast_to(run_ref[...], (8, LANES))
        lane8 = lax.broadcasted_iota(jnp.int32, (8, LANES), 1)
        padded = jnp.floor((cnt + (blk - 1.0)) * (1.0 / blk)) * blk
        pend = padded
        for sh in (1, 2, 4, 8, 16, 32, 64):
            pend = pend + jnp.where(lane8 >= sh, pltpu.roll(pend, sh, 1), 0.0)
        pstart = pend - padded
        total = jnp.max(pend, axis=-1, keepdims=True)
        tail = total + (lane8 - N_EXPERTS).astype(F32) * blk
        fill = jnp.where(lane8 < N_EXPERTS, jnp.where(padded > 0, pend - blk, -1.0),
                         jnp.where((lane8 < 2 * N_EXPERTS) & (tail < n_blocks * blk), tail, -1.0))
        row8 = lax.broadcasted_iota(jnp.int32, (8, LANES), 0)
        plan_ref[...] = jnp.where(row8 == 0, pstart, jnp.where(row8 == 1, fill,
                                                               jnp.where(row8 == 2, cnt, 0.0))).astype(jnp.int32)


def _route(logits, blk, n_blocks):
    n = logits.shape[0]
    tt = min(512, n)
    blkspec = lambda: pl.BlockSpec((tt, LANES), lambda i: (i, 0))
    return pl.pallas_call(
        functools.partial(_route_kernel, blk, n_blocks),
        grid=(n // tt,),
        in_specs=[blkspec()],
        out_specs=[blkspec(),
                   pl.BlockSpec((8, tt), lambda i: (0, i)),
                   pl.BlockSpec((8, LANES), lambda i: (0, 0))],
        out_shape=[jax.ShapeDtypeStruct((n, LANES), F32),
                   jax.ShapeDtypeStruct((8, n), jnp.int32),
                   jax.ShapeDtypeStruct((8, LANES), jnp.int32)],
        scratch_shapes=[pltpu.VMEM((1, LANES), F32)],
        compiler_params=_params("arbitrary"),
    )(logits)


def _dispatch_kernel(tt, blk, n_fill, dest_ref, fill_ref, h_ref, xs_ref, zero_ref, sem, zsem):
    i = pl.program_id(0)
    base = i * (tt * TOP_K)

    @pl.when(i == 0)
    def _():
        zero_ref[...] = jnp.zeros(zero_ref.shape, U32)

        def zcopy(z):
            row = pl.multiple_of(jnp.maximum(fill_ref[z], 0), blk)
            return pltpu.make_async_copy(zero_ref, xs_ref.at[pl.ds(row, blk)], zsem)

        def zissue(z, carry):
            @pl.when(fill_ref[z] >= 0)
            def _():
                zcopy(z).start()
            return carry

        def zdrain(z, carry):
            @pl.when(fill_ref[z] >= 0)
            def _():
                zcopy(z).wait()
            return carry

        lax.fori_loop(0, n_fill, zissue, 0)
        lax.fori_loop(0, n_fill, zdrain, 0)

    def copy(r, kk):
        d = dest_ref[base + r * TOP_K + kk]
        return pltpu.make_async_copy(h_ref.at[r], xs_ref.at[d], sem)

    def issue(r, carry):
        for kk in range(TOP_K):
            copy(r, kk).start()
        return carry

    lax.fori_loop(0, tt, issue, 0, unroll=8)
    for _ in range(TOP_K):
        pltpu.make_async_copy(h_ref, xs_ref.at[pl.ds(0, tt)], sem).wait()


def _dispatch(h_packed, dest_flat, fill_rows, n_slots, blk):
    n = h_packed.shape[0]
    tile = h_packed.shape[1:]
    tt = min(512, n)
    n_fill = fill_rows.shape[0]
    grid_spec = pltpu.PrefetchScalarGridSpec(
        num_scalar_prefetch=2,
        grid=(n // tt,),
        in_specs=[pl.BlockSpec((tt,) + tile, lambda i, d, f: (i, 0, 0))],
        out_specs=pl.BlockSpec(memory_space=pl.ANY),
        scratch_shapes=[pltpu.VMEM((blk,) + tile, U32), pltpu.SemaphoreType.DMA(()), pltpu.SemaphoreType.DMA(())],
    )
    return pl.pallas_call(
        functools.partial(_dispatch_kernel, tt, blk, n_fill),
        grid_spec=grid_spec,
        out_shape=jax.ShapeDtypeStruct((n_slots,) + tile, U32),
        compiler_params=_params("arbitrary"),
    )(dest_flat, fill_rows, h_packed)


def _expert_kernel(blk, ahead, cnt_ref, pstart_ref, fill_ref, xs_ref, w1_ref, w3_ref, w2_ref, y_ref,
                   w1f, w3f, w2f, w1b, w3b, w2b, xbuf, ybuf, w_sem, in_sem, out_sem):
    e = pl.program_id(0)
    n_exp = pl.num_programs(0)
    wslot = e % 2
    n_blk = (cnt_ref[e] + (blk - 1)) // blk
    base = pstart_ref[e]
    n_x = xbuf.shape[0]

    def weight_copies(ex, slot):
        return [pltpu.make_async_copy(src.at[ex], dst.at[slot], w_sem.at[slot])
                for src, dst in ((w1_ref, w1f), (w3_ref, w3f), (w2_ref, w2f))]

    def rows(b):
        return pl.ds(pl.multiple_of(base + b * blk, blk), blk)

    def in_copy(b, slot):
        return pltpu.make_async_copy(xs_ref.at[rows(b)], xbuf.at[slot], in_sem.at[slot])

    def out_copy(b, slot):
        return pltpu.make_async_copy(ybuf.at[slot], y_ref.at[rows(b)], out_sem.at[slot])

    @pl.when(e == 0)
    def _():
        for c in weight_copies(0, 0):
            c.start()

    for p in range(ahead):
        @pl.when(p < n_blk)
        def _():
            in_copy(p, p).start()

    @pl.when(e + 1 < n_exp)
    def _():
        for c in weight_copies(e + 1, 1 - wslot):
            c.start()

    for c in weight_copies(e, wslot):
        c.wait()
    w1b[...] = w1f[wslot].astype(BF16)
    w3b[...] = w3f[wslot].astype(BF16)
    w2b[...] = w2f[wslot].astype(BF16)

    def body(b, carry):
        slot = b % 2

        @pl.when(b + ahead < n_blk)
        def _():
            in_copy(b + ahead, (b + ahead) % n_x).start()

        in_copy(b, b % n_x).wait()

        @pl.when(b >= 2)
        def _():
            out_copy(b - 2, slot).wait()

        lo, hi = _unpack_halves(_tiles_to_rows(xbuf[b % n_x]))
        lo = lo.astype(BF16)
        hi = hi.astype(BF16)
        half = lo.shape[1]
        a = jnp.dot(lo, w1b[:half, :], preferred_element_type=F32)
        a = a + jnp.dot(hi, w1b[half:, :], preferred_element_type=F32)
        g = jnp.dot(lo, w3b[:half, :], preferred_element_type=F32)
        g = g + jnp.dot(hi, w3b[half:, :], preferred_element_type=F32)
        mid = (_silu(a) * g).astype(BF16)
        ybuf[slot] = _rows_to_tiles(_pack_halves(jnp.dot(mid, w2b[...], preferred_element_type=F32)))
        out_copy(b, slot).start(priority=1)
        return carry

    lax.fori_loop(0, n_blk, body, 0)

    @pl.when(n_blk >= 2)
    def _():
        out_copy(n_blk - 2, n_blk % 2).wait()

    @pl.when(n_blk >= 1)
    def _():
        out_copy(n_blk - 1, (n_blk - 1) % 2).wait()

    @pl.when(e == pl.num_programs(0) - 1)
    def _():
        ybuf[0] = jnp.zeros(ybuf.shape[1:], U32)

        def zcopy(t):
            row = pl.multiple_of(jnp.maximum(fill_ref[N_EXPERTS + t], 0), blk)
            return pltpu.make_async_copy(ybuf.at[0], y_ref.at[pl.ds(row, blk)], out_sem.at[0])

        def zissue(t, carry):
            @pl.when(fill_ref[N_EXPERTS + t] >= 0)
            def _():
                zcopy(t).start()
            return carry

        def zdrain(t, carry):
            @pl.when(fill_ref[N_EXPERTS + t] >= 0)
            def _():
                zcopy(t).wait()
            return carry

        lax.fori_loop(0, N_EXPERTS, zissue, 0)
        lax.fori_loop(0, N_EXPERTS, zdrain, 0)


def _expert_blocks(xs, counts, pstart, fill_rows, w1, w3, w2, blk):
    n_slots = xs.shape[0]
    tile = xs.shape[1:]
    n_exp, d, de = w1.shape
    ahead = 3
    hbm = pl.BlockSpec(memory_space=pl.ANY)
    grid_spec = pltpu.PrefetchScalarGridSpec(
        num_scalar_prefetch=3,
        grid=(n_exp,),
        in_specs=[hbm, hbm, hbm, hbm],
        out_specs=hbm,
        scratch_shapes=[pltpu.VMEM((2, d, de), F32), pltpu.VMEM((2, d, de), F32), pltpu.VMEM((2, de, d), F32),
                        pltpu.VMEM((d, de), BF16), pltpu.VMEM((d, de), BF16), pltpu.VMEM((de, d), BF16),
                        pltpu.VMEM((ahead + 1, blk) + tile, U32), pltpu.VMEM((2, blk) + tile, U32),
                        pltpu.SemaphoreType.DMA((2,)), pltpu.SemaphoreType.DMA((ahead + 1,)),
                        pltpu.SemaphoreType.DMA((2,))],
    )
    return pl.pallas_call(
        functools.partial(_expert_kernel, blk, ahead),
        grid_spec=grid_spec,
        out_shape=jax.ShapeDtypeStruct((n_slots,) + tile, U32),
        compiler_params=_params("arbitrary"),
    )(counts, pstart, fill_rows, xs, w1, w3, w2)


def _combine_kernel(tt, n_tiles, dest_ref, x1_ref, g2_ref, gate_ref, yb_ref, o_ref, buf, sems):
    i = pl.program_id(0)

    def copy(tile, slot, r, kk):
        d = dest_ref[(tile * tt + r) * TOP_K + kk]
        return pltpu.make_async_copy(yb_ref.at[d], buf.at[slot, kk, r], sems.at[slot])

    def issue_tile(tile, slot):
        def body(r, carry):
            for kk in range(TOP_K):
                copy(tile, slot, r, kk).start()
            return carry
        lax.fori_loop(0, tt, body, 0, unroll=8)

    def wait_tile(tile, slot):
        for kk in range(TOP_K):
            pltpu.make_async_copy(yb_ref.at[pl.ds(0, tt)], buf.at[slot, kk], sems.at[slot]).wait()

    slot = i % 2

    @pl.when(i == 0)
    def _():
        issue_tile(0, 0)

    @pl.when(i + 1 < n_tiles)
    def _():
        issue_tile(i + 1, 1 - slot)

    wait_tile(i, slot)

    gate = gate_ref[...]
    wa = gate[:, 0:1]
    wb = gate[:, 1:2]
    lo_a, hi_a = _unpack_halves(_tiles_to_rows(buf[slot, 0]))
    lo_b, hi_b = _unpack_halves(_tiles_to_rows(buf[slot, 1]))
    y = jnp.concatenate([wa * lo_a + wb * lo_b, wa * hi_a + wb * hi_b], axis=-1)
    o_ref[...] = x1_ref[...] + g2_ref[0] * y


def _combine(x1, seq, g2, gates, dest_flat, yb):
    n, d = x1.shape
    tile = yb.shape[1:]
    tt = min(512, seq)
    n_tiles = n // tt
    tiles_per_seq = seq // tt
    grid_spec = pltpu.PrefetchScalarGridSpec(
        num_scalar_prefetch=1,
        grid=(n_tiles,),
        in_specs=[pl.BlockSpec((tt, d), lambda i, dr: (i, 0)),
                  pl.BlockSpec((1, 1, d), lambda i, dr: (i // tiles_per_seq, 0, 0)),
                  pl.BlockSpec((tt, LANES), lambda i, dr: (i, 0)),
                  pl.BlockSpec(memory_space=pl.ANY)],
        out_specs=pl.BlockSpec((tt, d), lambda i, dr: (i, 0)),
        scratch_shapes=[pltpu.VMEM((2, TOP_K, tt) + tile, U32), pltpu.SemaphoreType.DMA((2,))],
    )
    return pl.pallas_call(
        functools.partial(_combine_kernel, tt, n_tiles),
        grid_spec=grid_spec,
        out_shape=jax.ShapeDtypeStruct((n, d), F32),
        compiler_params=_params("arbitrary"),
    )(dest_flat, x1, g2, gates, yb)


def _rotation_tables(seq):
    half = HEAD_DIM // 2
    theta = ROPE_BASE ** (-np.arange(half, dtype=np.float64) / half)
    ang = np.arange(seq, dtype=np.float64)[:, None] * theta[None, :]
    cos_t = np.concatenate([np.cos(ang), np.cos(ang)], axis=-1).astype(np.float32)
    sin_t = np.concatenate([-np.sin(ang), np.sin(ang)], axis=-1).astype(np.float32)
    return jnp.asarray(cos_t), jnp.asarray(sin_t)


def _layer(x, c, w_ada, b_ada, norm1_w, w_in, forget_bias, q_norm_w, k_norm_w, ret_norm_w, w_out, norm2_w,
           w_coarse, b_coarse, w_fine, b_fine, w1, w3, w2):
    bsz, seq, d = x.shape
    n = bsz * seq
    d_fox = d // 2
    d_ret = d // 2
    n_heads = d_fox // HEAD_DIM

    mod = _ada_modulation(c, w_ada, b_ada)
    sh1, sc1, g1, sh2, sc2, g2 = [m.reshape(bsz, 1, d) for m in jnp.split(mod, 6, axis=-1)]

    f0 = 3 * d_fox
    w_fox = w_in[:, :f0].astype(BF16)
    w_ret = w_in[:, f0 + n_heads:].astype(BF16)
    w_f = jnp.zeros((d, LANES), BF16).at[:, :n_heads].set(w_in[:, f0:f0 + n_heads].astype(BF16))
    fb = jnp.zeros((1, LANES), F32).at[0, :n_heads].set(forget_bias)

    cos_t, sin_t = _rotation_tables(seq)

    x2d = x.reshape(n, d)
    z, log_f = _input_projection(x2d, seq, norm1_w.reshape(1, d), sc1, sh1, w_fox, w_ret, w_f, cos_t, sin_t,
                                 q_norm_w.reshape(1, HEAD_DIM), k_norm_w.reshape(1, HEAD_DIM), fb)

    lf = log_f[:, :n_heads].reshape(bsz, seq, n_heads).transpose(0, 2, 1).reshape(bsz * n_heads, seq)
    cum = _cumsum_rows(lf)

    qk_bound = 1.02 * LOG2E * HEAD_DIM ** 0.5 * jnp.max(jnp.abs(q_norm_w)) * jnp.max(jnp.abs(k_norm_w))
    o_a = _fox_attention(z, cum, qk_bound, bsz, seq, n_heads)
    log_g = jnp.log(1.0 - 2.0 ** (-5.0 - jnp.arange(n_heads, dtype=F32)))
    o_b = _retention(z, log_g, ret_norm_w.reshape(1, d_ret), bsz, seq, n_heads, 3 * d_fox)

    w_router = jnp.zeros((d, LANES), F32)
    w_router = w_router.at[:, :N_GROUPS].set(w_coarse)
    w_router = w_router.at[:, N_GROUPS:N_GROUPS + N_EXPERTS].set(
        w_fine.transpose(1, 0, 2).reshape(d, N_EXPERTS))
    b_router = jnp.zeros((1, LANES), F32)
    b_router = b_router.at[0, :N_GROUPS].set(b_coarse)
    b_router = b_router.at[0, N_GROUPS:N_GROUPS + N_EXPERTS].set(b_fine.reshape(N_EXPERTS))

    wr_hi = w_router.astype(BF16)
    wr_lo = (w_router - wr_hi.astype(F32)).astype(BF16)
    x1, h_packed, logits = _output_projection(o_a, o_b, w_out.astype(BF16), x2d, seq, g1,
                                              norm2_w.reshape(1, d), sc2, sh2,
                                              jnp.concatenate([wr_hi, wr_lo], axis=1), b_router)

    blk = 256
    nk = n * TOP_K
    n_blocks = nk // blk + N_EXPERTS
    gates, ids, plan = _route(logits, blk, n_blocks)
    pstart = plan[0, :N_EXPERTS]
    fill_rows = plan[1, :2 * N_EXPERTS]
    counts = plan[2, :N_EXPERTS]
    eid = ids[0:TOP_K]
    hit = eid[None] == jnp.arange(N_EXPERTS, dtype=jnp.int32)[:, None, None]
    dest = (jnp.sum(jnp.where(hit, pstart[:, None, None], 0), axis=0) + ids[TOP_K:2 * TOP_K]).T.reshape(nk)

    xs = _dispatch(h_packed, dest, fill_rows, n_blocks * blk, blk)
    yb = _expert_blocks(xs, counts, pstart, fill_rows, w1, w3, w2, blk)
    out = _combine(x1, seq, g2, gates, dest, yb)
    return out.reshape(bsz, seq, d)


def kernel(x, c, w_ada, b_ada, norm1_w, w_in, forget_bias, q_norm_w, k_norm_w, ret_norm_w, w_out, norm2_w,
           w_coarse, b_coarse, w_fine, b_fine, w1, w3, w2):
    c_in = c
    for l in range(w_ada.shape[0]):
        x = _layer(x, c_in, w_ada[l], b_ada[l], norm1_w[l], w_in[l], forget_bias[l], q_norm_w[l],
                   k_norm_w[l], ret_norm_w[l], w_out[l], norm2_w[l], w_coarse[l], b_coarse[l],
                   w_fine[l], b_fine[l], w1[l], w3[l], w2[l])
    return x
```

```python
import functools

import jax
import jax.numpy as jnp
import numpy as np
from jax import lax
from jax.experimental import pallas as pl
from jax.experimental.pallas import tpu as pltpu

HEAD_DIM = 128
N_GROUPS = 4
EXPERTS_PER_GROUP = 8
N_EXPERTS = N_GROUPS * EXPERTS_PER_GROUP
TOP_K = 2
ROPE_BASE = 10000.0
EPS = 1e-6

LANES = 128
VMEM_LIMIT = 56 * 1024 * 1024
NEG_BIG = -1e30
LOG2E = 1.4426950408889634
UNDERFLOW_LOG2 = 160.0

F32 = jnp.float32
BF16 = jnp.bfloat16
U32 = jnp.uint32


def _params(*sem):
    return pltpu.CompilerParams(dimension_semantics=sem, vmem_limit_bytes=VMEM_LIMIT)


def _silu(v):
    return v * (1.0 / (1.0 + jnp.exp(-v)))


def _pack_halves(y):
    w = y.shape[1] // 2
    lo = pltpu.bitcast(y[:, :w].astype(BF16).astype(F32), U32)
    hi = pltpu.bitcast(y[:, w:].astype(BF16).astype(F32), U32)
    return (hi & jnp.uint32(0xFFFF0000)) | (lo >> 16)


def _rows_to_tiles(p):
    return pltpu.einshape("m(ck)->mck", p, c=8, k=LANES)


def _tiles_to_rows(t):
    return pltpu.einshape("mck->m(ck)", t)


def _unpack_halves(p):
    lo = pltpu.bitcast(p << 16, F32)
    hi = pltpu.bitcast(p & jnp.uint32(0xFFFF0000), F32)
    return lo, hi


def _ada_kernel(ct_ref, w_ref, b_ref, o_ref):
    w = w_ref[...]
    rows = []
    for b in range(o_ref.shape[0]):
        if b < 2:
            cb = _silu(ct_ref[:, b:b + 1])
            rows.append(jnp.sum(cb * w, axis=0, keepdims=True) + b_ref[...])
        else:
            rows.append(jnp.zeros_like(b_ref[...]))
    o_ref[...] = jnp.concatenate(rows, axis=0)


def _ada_modulation(c, w_ada, b_ada):
    bsz, d = c.shape
    n = w_ada.shape[1]
    tn = 1024
    ct = jnp.zeros((d, LANES), F32).at[:, :bsz].set(c.T)
    out = pl.pallas_call(
        _ada_kernel,
        grid=(n // tn,),
        in_specs=[pl.BlockSpec((d, LANES), lambda j: (0, 0)),
                  pl.BlockSpec((d, tn), lambda j: (0, j)),
                  pl.BlockSpec((1, tn), lambda j: (0, j))],
        out_specs=pl.BlockSpec((8, tn), lambda j: (0, j)),
        out_shape=jax.ShapeDtypeStruct((8, n), F32),
        compiler_params=_params("arbitrary"),
    )(ct, w_ada, b_ada.reshape(1, n))
    return out[:bsz]


def _inproj_kernel(q_t, r_t, x_ref, nw_ref, sc_ref, sh_ref, wa_ref, wb_ref, wf_ref, cos_ref, sin_ref,
                   qw_ref, kw_ref, fb_ref, z_ref, f_ref, h_ref):
    j = pl.program_id(1)
    r0 = 3 * q_t

    @pl.when(j == 0)
    def _():
        x = x_ref[...]
        ms = jnp.mean(x * x, axis=-1, keepdims=True)
        y = x * lax.rsqrt(ms + EPS) * nw_ref[...]
        h = (y * (1.0 + sc_ref[0]) + sh_ref[0]).astype(BF16)
        h_ref[...] = h
        t = jnp.dot(h, wf_ref[...], preferred_element_type=F32) + fb_ref[...]
        f_ref[...] = jnp.minimum(t, 0.0) - jnp.log(1.0 + jnp.exp(-jnp.abs(t)))

    def heads_of(acc):
        return [acc[:, hh * HEAD_DIM:(hh + 1) * HEAD_DIM] for hh in range(acc.shape[1] // HEAD_DIM)]

    def head_norm(acc, w_row):
        outs = []
        for a in heads_of(acc):
            ms = jnp.mean(a * a, axis=-1, keepdims=True)
            outs.append(a * lax.rsqrt(ms + EPS) * w_row)
        return jnp.concatenate(outs, axis=-1).astype(BF16)

    def rotate(acc, scale):
        cs = cos_ref[...] * scale
        sn = sin_ref[...] * scale
        outs = [a * cs + pltpu.roll(a, HEAD_DIM // 2, 1) * sn for a in heads_of(acc)]
        return jnp.concatenate(outs, axis=-1).astype(BF16)

    def fox():
        return jnp.dot(h_ref[...], wa_ref[...], preferred_element_type=F32)

    def ret():
        return jnp.dot(h_ref[...], wb_ref[...], preferred_element_type=F32)

    @pl.when(j < q_t)
    def _():
        z_ref[...] = head_norm(fox(), qw_ref[...] * (LOG2E * HEAD_DIM ** -0.5))

    @pl.when((j >= q_t) & (j < 2 * q_t))
    def _():
        z_ref[...] = head_norm(fox(), kw_ref[...])

    @pl.when((j >= 2 * q_t) & (j < r0))
    def _():
        z_ref[...] = fox().astype(BF16)

    @pl.when((j >= r0) & (j < r0 + r_t))
    def _():
        z_ref[...] = rotate(ret(), 1.0)

    @pl.when((j >= r0 + r_t) & (j < r0 + 2 * r_t))
    def _():
        z_ref[...] = rotate(ret(), HEAD_DIM ** -0.5)

    @pl.when(j >= r0 + 2 * r_t)
    def _():
        z_ref[...] = ret().astype(BF16)


def _input_projection(x2d, seq, norm_w, sc1, sh1, w_fox, w_ret, w_f, cos_t, sin_t, qw, kw, fb):
    n, d = x2d.shape
    tm, tn = min(1024, seq), 1024
    fox_tiles = w_fox.shape[1] // tn
    ret_tiles = w_ret.shape[1] // tn
    tiles_per_seq = seq // tm
    kern = functools.partial(_inproj_kernel, fox_tiles // 3, ret_tiles // 4)
    bsel = lambda i, j: (i // tiles_per_seq, 0, 0)
    const = lambda i, j: (0, 0)
    return pl.pallas_call(
        kern,
        grid=(n // tm, fox_tiles + ret_tiles),
        in_specs=[pl.BlockSpec((tm, d), lambda i, j: (i, 0)),
                  pl.BlockSpec((1, d), const),
                  pl.BlockSpec((1, 1, d), bsel),
                  pl.BlockSpec((1, 1, d), bsel),
                  pl.BlockSpec((d, tn), lambda i, j: (0, jnp.minimum(j, fox_tiles - 1))),
                  pl.BlockSpec((d, tn), lambda i, j: (0, jnp.maximum(j - fox_tiles, 0))),
                  pl.BlockSpec((d, LANES), const),
                  pl.BlockSpec((tm, HEAD_DIM), lambda i, j: (i % tiles_per_seq, 0)),
                  pl.BlockSpec((tm, HEAD_DIM), lambda i, j: (i % tiles_per_seq, 0)),
                  pl.BlockSpec((1, HEAD_DIM), const),
                  pl.BlockSpec((1, HEAD_DIM), const),
                  pl.BlockSpec((1, LANES), const)],
        out_specs=[pl.BlockSpec((tm, tn), lambda i, j: (i, j)),
                   pl.BlockSpec((tm, LANES), lambda i, j: (i, 0))],
        out_shape=[jax.ShapeDtypeStruct((n, w_fox.shape[1] + w_ret.shape[1]), BF16),
                   jax.ShapeDtypeStruct((n, LANES), F32)],
        scratch_shapes=[pltpu.VMEM((tm, d), BF16)],
        compiler_params=_params("arbitrary", "arbitrary"),
    )(x2d, norm_w, sc1, sh1, w_fox, w_ret, w_f, cos_t, sin_t, qw, kw, fb)


def _cumsum_kernel(x_ref, o_ref):
    x = x_ref[0]
    r = x.shape[0]
    a = lax.broadcasted_iota(jnp.int32, (LANES, LANES), 0)
    b = lax.broadcasted_iota(jnp.int32, (LANES, LANES), 1)
    upper = (a <= b).astype(F32)
    within = jnp.dot(x, upper, precision=lax.Precision.HIGHEST, preferred_element_type=F32)
    tot = jnp.broadcast_to(within[:, LANES - 1:LANES], (r, LANES))
    ra = lax.broadcasted_iota(jnp.int32, (r, r), 0)
    rb = lax.broadcasted_iota(jnp.int32, (r, r), 1)
    strict = (rb < ra).astype(F32)
    before = jnp.dot(strict, tot, precision=lax.Precision.HIGHEST, preferred_element_type=F32)
    o_ref[0] = within + before


def _cumsum_rows(x):
    g, s = x.shape
    r = s // LANES
    out = pl.pallas_call(
        _cumsum_kernel,
        grid=(g,),
        in_specs=[pl.BlockSpec((1, r, LANES), lambda i: (i, 0, 0))],
        out_specs=pl.BlockSpec((1, r, LANES), lambda i: (i, 0, 0)),
        out_shape=jax.ShapeDtypeStruct((g, r, LANES), F32),
        compiler_params=_params("arbitrary"),
    )(x.reshape(g, r, LANES))
    return out.reshape(g, 1, s)


def _fox_kernel(tq, first_ref, q_ref, k_ref, v_ref, cum_ref, o_ref, s_refs, m_ref, l_ref, acc_ref):
    pair_id = pl.program_id(2)
    n_pairs = pl.num_programs(2)
    head = pl.program_id(0) * pl.num_programs(1) + pl.program_id(1)
    n_slabs = tq // LANES

    m_ref[...] = jnp.full(m_ref.shape, NEG_BIG, F32)
    l_ref[...] = jnp.zeros(l_ref.shape, F32)
    acc_ref[...] = jnp.zeros(acc_ref.shape, F32)

    class Sub:
        def __init__(self, idx):
            self.rows = slice(idx * tq, (idx + 1) * tq)
            self.qi = 2 * pair_id + idx
            self.sa, self.sb = s_refs[2 * idx], s_refs[2 * idx + 1]
            q_start = pl.multiple_of(self.qi * tq, tq)
            self.c0 = cum_ref[0, :, pl.ds(q_start, LANES)][:, 0:1]
            self.first = first_ref[(head * n_pairs + pair_id) * 2 + idx]
            self.n_off = self.qi - self.first

    def scores(sub, kb, s_ref):
        start = pl.multiple_of(kb * tq, tq)
        k = k_ref[pl.ds(start, tq), :]
        bias = (sub.c0 - cum_ref[0, :, pl.ds(start, tq)]) * LOG2E
        s_ref[...] = lax.dot_general(q_ref[sub.rows, :], k, (((1,), (1,)), ((), ())),
                                     preferred_element_type=F32) + bias

    def softmax_pv(sub, kb, s_ref, masked):
        rs = sub.rows
        start = pl.multiple_of(kb * tq, tq)
        v = v_ref[pl.ds(start, tq), :]
        slabs = []
        for j in range(n_slabs):
            t = s_ref[:, j * LANES:(j + 1) * LANES]
            if masked:
                row = lax.broadcasted_iota(jnp.int32, t.shape, 0)
                col = lax.broadcasted_iota(jnp.int32, t.shape, 1) + j * LANES
                t = jnp.where(col <= row, t, NEG_BIG)
            slabs.append(t)
        mx = slabs[0]
        for t in slabs[1:]:
            mx = jnp.maximum(mx, t)
        m_prev = m_ref[rs, :]
        m_new = jnp.maximum(m_prev, jnp.max(mx, axis=-1, keepdims=True))
        alpha = jnp.exp2(m_prev - m_new)
        probs = [jnp.exp2(t - m_new) for t in slabs]
        psum = probs[0]
        for t in probs[1:]:
            psum = psum + t
        l_ref[rs, :] = alpha * l_ref[rs, :] + psum
        p = jnp.concatenate([t.astype(BF16) for t in probs], axis=-1)
        acc_ref[rs, :] = alpha * acc_ref[rs, :] + jnp.dot(p, v, preferred_element_type=F32)
        m_ref[rs, :] = m_new

    def sweep(sub, then):
        def pair(kb):
            scores(sub, kb + 1, sub.sb)
            softmax_pv(sub, kb, sub.sa, False)
            scores(sub, kb + 2, sub.sa)
            softmax_pv(sub, kb + 1, sub.sb, False)

        def body4(i, carry):
            pair(sub.first + 4 * i)
            pair(sub.first + 4 * i + 2)
            return carry

        def body2(i, carry):
            pair(sub.first + 2 * i)
            return carry

        n4 = sub.n_off // 4
        lax.fori_loop(0, n4, body4, 0)
        lax.fori_loop(2 * n4, sub.n_off // 2, body2, 0)

        @pl.when(sub.n_off % 2 == 0)
        def _():
            then()
            softmax_pv(sub, sub.qi, sub.sa, True)

        @pl.when(sub.n_off % 2 == 1)
        def _():
            scores(sub, sub.qi, sub.sb)
            softmax_pv(sub, sub.qi - 1, sub.sa, False)
            then()
            softmax_pv(sub, sub.qi, sub.sb, True)

    first_q, second_q = Sub(0), Sub(1)
    scores(first_q, first_q.first, first_q.sa)
    sweep(first_q, lambda: scores(second_q, second_q.first, second_q.sa))
    sweep(second_q, lambda: None)

    o_ref[...] = (acc_ref[...] / jnp.sum(l_ref[...], axis=-1, keepdims=True)).astype(BF16)


def _first_live_block(cum, tq, qk_bound):
    c0 = cum[:, 0, ::tq]
    cend = cum[:, 0, tq - 1::tq]
    gap = (c0[:, :, None] - cend[:, None, :]) * LOG2E + 2.0 * qk_bound
    nq = c0.shape[1]
    earlier = jnp.arange(nq)[None, :] < jnp.arange(nq)[:, None]
    return jnp.sum((gap < -UNDERFLOW_LOG2) & earlier[None], axis=-1).astype(jnp.int32).reshape(-1)


def _fox_attention(z, cum, qk_bound, bsz, seq, n_heads):
    tq = min(512, seq // 2)
    nq = seq // tq
    n_pairs = nq // 2
    kern = functools.partial(_fox_kernel, tq)
    grid_spec = pltpu.PrefetchScalarGridSpec(
        num_scalar_prefetch=1,
        grid=(bsz, n_heads, n_pairs),
        in_specs=[pl.BlockSpec((2 * tq, HEAD_DIM), lambda b, h, i, f: (b * n_pairs + i, h)),
                  pl.BlockSpec((seq, HEAD_DIM), lambda b, h, i, f: (b, n_heads + h)),
                  pl.BlockSpec((seq, HEAD_DIM), lambda b, h, i, f: (b, 2 * n_heads + h)),
                  pl.BlockSpec((1, 1, seq), lambda b, h, i, f: (b * n_heads + h, 0, 0))],
        out_specs=pl.BlockSpec((2 * tq, HEAD_DIM), lambda b, h, i, f: (b * n_pairs + i, h)),
        scratch_shapes=[[pltpu.VMEM((tq, tq), F32)] * 4,
                        pltpu.VMEM((2 * tq, LANES), F32), pltpu.VMEM((2 * tq, LANES), F32),
                        pltpu.VMEM((2 * tq, HEAD_DIM), F32)],
    )
    return pl.pallas_call(
        kern,
        grid_spec=grid_spec,
        out_shape=jax.ShapeDtypeStruct((bsz * seq, n_heads * HEAD_DIM), BF16),
        compiler_params=_params("arbitrary", "arbitrary", "arbitrary"),
    )(_first_live_block(cum, tq, qk_bound), z, z, z, cum)


def _ret_kernel(chunk, n_heads, lg_ref, q_ref, k_ref, v_ref, g_ref, nw_ref, o_ref, state_ref, decay_ref,
                qdec_ref, kdec_ref):
    first = (pl.program_id(0) == 0) & (pl.program_id(1) == 0)

    @pl.when(first)
    def _():
        i = lax.broadcasted_iota(jnp.int32, (chunk, chunk), 0)
        jj = lax.broadcasted_iota(jnp.int32, (chunk, chunk), 1)
        diff = (i - jj).astype(F32)
        pos = lax.broadcasted_iota(jnp.int32, (chunk, HEAD_DIM), 0).astype(F32)
        for h in range(n_heads):
            decay_ref[h] = jnp.where(diff >= 0, jnp.exp(lg_ref[h] * jnp.maximum(diff, 0.0)), 0.0)
            qdec_ref[h] = jnp.exp(lg_ref[h] * (pos + 1.0))
            kdec_ref[h] = jnp.exp(lg_ref[h] * (chunk - 1.0 - pos))

    @pl.when(pl.program_id(1) == 0)
    def _():
        state_ref[...] = jnp.zeros(state_ref.shape, F32)

    for h in range(n_heads):
        log_g = lg_ref[h]
        cols = slice(h * HEAD_DIM, (h + 1) * HEAD_DIM)
        q = q_ref[:, cols]
        k = k_ref[:, cols]
        v = v_ref[:, cols]
        scores = lax.dot_general(q, k, (((1,), (1,)), ((), ())), preferred_element_type=F32)
        scores = scores * decay_ref[h]
        intra = jnp.dot(scores.astype(BF16), v, preferred_element_type=F32)
        state = state_ref[h]
        inter = jnp.dot(q, state.astype(BF16), preferred_element_type=F32) * qdec_ref[h]
        kd = (k.astype(F32) * kdec_ref[h]).astype(BF16)
        kv = lax.dot_general(kd, v, (((0,), (0,)), ((), ())), preferred_element_type=F32)
        state_ref[h] = state * jnp.exp(jnp.full((1, HEAD_DIM), chunk, F32) * log_g) + kv
        o = intra + inter
        ms = jnp.mean(o * o, axis=-1, keepdims=True)
        o = o * lax.rsqrt(ms + EPS) * nw_ref[:, cols]
        o_ref[:, cols] = (o * _silu(g_ref[:, cols].astype(F32))).astype(BF16)


def _retention(z, log_g, norm_w, bsz, seq, n_heads, col0):
    chunk = min(256, seq)
    nt = seq // chunk
    width = n_heads * HEAD_DIM
    c0 = col0 // width
    kern = functools.partial(_ret_kernel, chunk, n_heads)

    def sec(s):
        return pl.BlockSpec((chunk, width), lambda b, t, lg: (b * nt + t, c0 + s))

    grid_spec = pltpu.PrefetchScalarGridSpec(
        num_scalar_prefetch=1,
        grid=(bsz, nt),
        in_specs=[sec(0), sec(1), sec(2), sec(3), pl.BlockSpec((1, width), lambda b, t, lg: (0, 0))],
        out_specs=pl.BlockSpec((chunk, width), lambda b, t, lg: (b * nt + t, 0)),
        scratch_shapes=[pltpu.VMEM((n_heads, HEAD_DIM, HEAD_DIM), F32),
                        pltpu.VMEM((n_heads, chunk, chunk), F32),
                        pltpu.VMEM((n_heads, chunk, HEAD_DIM), F32),
                        pltpu.VMEM((n_heads, chunk, HEAD_DIM), F32)],
    )
    return pl.pallas_call(
        kern,
        grid_spec=grid_spec,
        out_shape=jax.ShapeDtypeStruct((bsz * seq, width), BF16),
        compiler_params=_params("arbitrary", "arbitrary"),
    )(log_g, z, z, z, z, norm_w)


def _outproj_kernel(oa_ref, ob_ref, wa_ref, wb_ref, x_ref, g1_ref, nw_ref, sc_ref, sh_ref, wr_ref, br_ref,
                    x1_ref, hp_ref, lg_ref):
    mix = jnp.dot(oa_ref[...], wa_ref[...], preferred_element_type=F32)
    mix = mix + jnp.dot(ob_ref[...], wb_ref[...], preferred_element_type=F32)
    x1 = x_ref[...] + g1_ref[0] * mix
    x1_ref[...] = x1
    ms = jnp.mean(x1 * x1, axis=-1, keepdims=True)
    h2 = x1 * lax.rsqrt(ms + EPS) * nw_ref[...] * (1.0 + sc_ref[0]) + sh_ref[0]
    hp_ref[...] = _rows_to_tiles(_pack_halves(h2))
    h_hi = h2.astype(BF16)
    h_lo = (h2 - h_hi.astype(F32)).astype(BF16)
    both = jnp.dot(h_hi, wr_ref[...], preferred_element_type=F32)
    cross = jnp.dot(h_lo, wr_ref[:, :LANES], preferred_element_type=F32)
    lg_ref[...] = both[:, :LANES] + both[:, LANES:] + cross + br_ref[...]


def _output_projection(o_a, o_b, w_out, x2d, seq, g1, norm_w, sc2, sh2, w_router, b_router):
    n, d = x2d.shape
    da = o_a.shape[1]
    tm = min(512, seq)
    tiles_per_seq = seq // tm
    bsel = lambda i: (i // tiles_per_seq, 0, 0)
    return pl.pallas_call(
        _outproj_kernel,
        grid=(n // tm,),
        in_specs=[pl.BlockSpec((tm, da), lambda i: (i, 0)),
                  pl.BlockSpec((tm, da), lambda i: (i, 0)),
                  pl.BlockSpec((da, d), lambda i: (0, 0)),
                  pl.BlockSpec((da, d), lambda i: (1, 0)),
                  pl.BlockSpec((tm, d), lambda i: (i, 0)),
                  pl.BlockSpec((1, 1, d), bsel),
                  pl.BlockSpec((1, d), lambda i: (0, 0)),
                  pl.BlockSpec((1, 1, d), bsel),
                  pl.BlockSpec((1, 1, d), bsel),
                  pl.BlockSpec((d, 2 * LANES), lambda i: (0, 0)),
                  pl.BlockSpec((1, LANES), lambda i: (0, 0))],
        out_specs=[pl.BlockSpec((tm, d), lambda i: (i, 0)),
                   pl.BlockSpec((tm, d // 2 // LANES, LANES), lambda i: (i, 0, 0)),
                   pl.BlockSpec((tm, LANES), lambda i: (i, 0))],
        out_shape=[jax.ShapeDtypeStruct((n, d), F32),
                   jax.ShapeDtypeStruct((n, d // 2 // LANES, LANES), U32),
                   jax.ShapeDtypeStruct((n, LANES), F32)],
        compiler_params=_params("arbitrary"),
    )(o_a, o_b, w_out, w_out, x2d, g1, norm_w, sc2, sh2, w_router, b_router)


def _route_kernel(blk, n_blocks, lg_ref, gate_ref, ids_ref, plan_ref, run_ref):
    i = pl.program_id(0)

    @pl.when(i == 0)
    def _():
        run_ref[...] = jnp.zeros(run_ref.shape, F32)

    lg = lg_ref[...]
    tt = lg.shape[0]
    lane = lax.broadcasted_iota(jnp.int32, lg.shape, 1).astype(F32)
    big = 1e6

    def rmax(v):
        return jnp.max(v, axis=-1, keepdims=True)

    def rmin(v):
        return jnp.min(v, axis=-1, keepdims=True)

    def rsum(v):
        return jnp.sum(v, axis=-1, keepdims=True)

    cmask = lane < N_GROUPS
    cm = jnp.where(cmask, lg, NEG_BIG)
    ce = jnp.where(cmask, jnp.exp(cm - rmax(cm)), 0.0)
    pgrp = ce / rsum(ce)
    p_g = rmax(pgrp)
    g_sel = rmin(jnp.where(cmask & (pgrp == p_g), lane, big))

    lo = N_GROUPS + EXPERTS_PER_GROUP * g_sel
    fmask = (lane >= lo) & (lane < lo + EXPERTS_PER_GROUP)
    fm = jnp.where(fmask, lg, NEG_BIG)
    fe = jnp.where(fmask, jnp.exp(fm - rmax(fm)), 0.0)
    fp = fe / rsum(fe)
    fp = jnp.where(fmask, fp, -1.0)
    p1 = rmax(fp)
    i1 = rmin(jnp.where(fp == p1, lane, big))
    fp2 = jnp.where(lane == i1, -1.0, fp)
    p2 = rmax(fp2)
    i2 = rmin(jnp.where(fp2 == p2, lane, big))
    denom = p1 + p2
    w1 = p_g * p1 / denom
    w2 = p_g * p2 / denom
    e1 = i1 - N_GROUPS
    e2 = i2 - N_GROUPS

    gate_ref[...] = jnp.where(lane == 0, w1, jnp.where(lane == 1, w2, 0.0))

    oh1 = (lane == e1).astype(F32)
    oh2 = (lane == e2).astype(F32)
    both = oh1 + oh2
    ra = lax.broadcasted_iota(jnp.int32, (tt, tt), 0)
    rb = lax.broadcasted_iota(jnp.int32, (tt, tt), 1)
    strict = (rb < ra).astype(BF16)
    prefix = jnp.dot(strict, both.astype(BF16), preferred_element_type=F32) + run_ref[...]
    r1 = rsum(prefix * oh1)
    r2 = rsum(prefix * oh2)
    run_ref[...] = run_ref[...] + jnp.sum(both, axis=0, keepdims=True)

    packed = jnp.where(lane == 0, e1, jnp.where(lane == 1, e2, jnp.where(lane == 2, r1,
                                                                        jnp.where(lane == 3, r2, 0.0))))
    ids_ref[...] = jnp.transpose(packed)[:8, :].astype(jnp.int32)

    @pl.when(i == pl.num_programs(0) - 1)
    def _():
        cnt = jnp.broadcast_to(run_ref[...], (8, LANES))
        lane8 = lax.broadcasted_iota(jnp.int32, (8, LANES), 1)
        padded = jnp.floor((cnt + (blk - 1.0)) * (1.0 / blk)) * blk
        pend = padded
        for sh in (1, 2, 4, 8, 16, 32, 64):
            pend = pend + jnp.where(lane8 >= sh, pltpu.roll(pend, sh, 1), 0.0)
        pstart = pend - padded
        total = jnp.max(pend, axis=-1, keepdims=True)
        tail = total + (lane8 - N_EXPERTS).astype(F32) * blk
        fill = jnp.where(lane8 < N_EXPERTS, jnp.where(padded > 0, pend - blk, -1.0),
                         jnp.where((lane8 < 2 * N_EXPERTS) & (tail < n_blocks * blk), tail, -1.0))
        row8 = lax.broadcasted_iota(jnp.int32, (8, LANES), 0)
        plan_ref[...] = jnp.where(row8 == 0, pstart, jnp.where(row8 == 1, fill,
                                                               jnp.where(row8 == 2, cnt, 0.0))).astype(jnp.int32)


def _route(logits, blk, n_blocks):
    n = logits.shape[0]
    tt = min(512, n)
    blkspec = lambda: pl.BlockSpec((tt, LANES), lambda i: (i, 0))
    return pl.pallas_call(
        functools.partial(_route_kernel, blk, n_blocks),
        grid=(n // tt,),
        in_specs=[blkspec()],
        out_specs=[blkspec(),
                   pl.BlockSpec((8, tt), lambda i: (0, i)),
                   pl.BlockSpec((8, LANES), lambda i: (0, 0))],
        out_shape=[jax.ShapeDtypeStruct((n, LANES), F32),
                   jax.ShapeDtypeStruct((8, n), jnp.int32),
                   jax.ShapeDtypeStruct((8, LANES), jnp.int32)],
        scratch_shapes=[pltpu.VMEM((1, LANES), F32)],
        compiler_params=_params("arbitrary"),
    )(logits)


def _dispatch_kernel(tt, blk, n_fill, dest_ref, fill_ref, h_ref, xs_ref, zero_ref, sem, zsem):
    i = pl.program_id(0)
    base = i * (tt * TOP_K)

    @pl.when(i == 0)
    def _():
        zero_ref[...] = jnp.zeros(zero_ref.shape, U32)

        def zcopy(z):
            row = pl.multiple_of(jnp.maximum(fill_ref[z], 0), blk)
            return pltpu.make_async_copy(zero_ref, xs_ref.at[pl.ds(row, blk)], zsem)

        def zissue(z, carry):
            @pl.when(fill_ref[z] >= 0)
            def _():
                zcopy(z).start()
            return carry

        def zdrain(z, carry):
            @pl.when(fill_ref[z] >= 0)
            def _():
                zcopy(z).wait()
            return carry

        lax.fori_loop(0, n_fill, zissue, 0)
        lax.fori_loop(0, n_fill, zdrain, 0)

    def copy(r, kk):
        d = dest_ref[base + r * TOP_K + kk]
        return pltpu.make_async_copy(h_ref.at[r], xs_ref.at[d], sem)

    def issue(r, carry):
        for kk in range(TOP_K):
            copy(r, kk).start()
        return carry

    lax.fori_loop(0, tt, issue, 0, unroll=8)
    for _ in range(TOP_K):
        pltpu.make_async_copy(h_ref, xs_ref.at[pl.ds(0, tt)], sem).wait()


def _dispatch(h_packed, dest_flat, fill_rows, n_slots, blk):
    n = h_packed.shape[0]
    tile = h_packed.shape[1:]
    tt = min(512, n)
    n_fill = fill_rows.shape[0]
    grid_spec = pltpu.PrefetchScalarGridSpec(
        num_scalar_prefetch=2,
        grid=(n // tt,),
        in_specs=[pl.BlockSpec((tt,) + tile, lambda i, d, f: (i, 0, 0))],
        out_specs=pl.BlockSpec(memory_space=pl.ANY),
        scratch_shapes=[pltpu.VMEM((blk,) + tile, U32), pltpu.SemaphoreType.DMA(()), pltpu.SemaphoreType.DMA(())],
    )
    return pl.pallas_call(
        functools.partial(_dispatch_kernel, tt, blk, n_fill),
        grid_spec=grid_spec,
        out_shape=jax.ShapeDtypeStruct((n_slots,) + tile, U32),
        compiler_params=_params("arbitrary"),
    )(dest_flat, fill_rows, h_packed)


def _expert_kernel(blk, ahead, cnt_ref, pstart_ref, fill_ref, xs_ref, w1_ref, w3_ref, w2_ref, y_ref,
                   w1f, w3f, w2f, w1b, w3b, w2b, xbuf, ybuf, done_ref, w_sem, in_sem, out_sem):
    e = pl.program_id(0)
    n_exp = pl.num_programs(0)
    wslot = e % 2
    n_blk = (cnt_ref[e] + (blk - 1)) // blk
    base = pstart_ref[e]
    n_x = xbuf.shape[0]

    def weight_copies(ex, slot):
        return [pltpu.make_async_copy(src.at[ex], dst.at[slot], w_sem.at[slot])
                for src, dst in ((w1_ref, w1f), (w3_ref, w3f), (w2_ref, w2f))]

    def rows(b):
        return pl.ds(pl.multiple_of(base + b * blk, blk), blk)

    def in_copy(b, slot):
        return pltpu.make_async_copy(xs_ref.at[rows(b)], xbuf.at[slot], in_sem.at[slot])

    def out_copy(b, slot):
        return pltpu.make_async_copy(ybuf.at[slot], y_ref.at[rows(b)], out_sem.at[slot])

    @pl.when(e == 0)
    def _():
        for c in weight_copies(0, 0):
            c.start()

    for p in range(ahead):
        @pl.when(p < n_blk)
        def _():
            in_copy(p, p).start()

    @pl.when(e + 1 < n_exp)
    def _():
        for c in weight_copies(e + 1, 1 - wslot):
            c.start()

    for c in weight_copies(e, wslot):
        c.wait()
    w1b[...] = w1f[wslot].astype(BF16)
    w3b[...] = w3f[wslot].astype(BF16)
    w2b[...] = w2f[wslot].astype(BF16)

    @pl.when(e == 0)
    def _():
        done_ref[0] = 0

    done = done_ref[0]

    def out_wait(slot):
        pltpu.make_async_copy(ybuf.at[slot], y_ref.at[pl.ds(0, blk)], out_sem.at[slot]).wait()

    def body(b, carry):
        slot = (done + b) % 2

        @pl.when(b + ahead < n_blk)
        def _():
            in_copy(b + ahead, (b + ahead) % n_x).start()

        in_copy(b, b % n_x).wait()

        @pl.when(done + b >= 2)
        def _():
            out_wait(slot)

        lo, hi = _unpack_halves(_tiles_to_rows(xbuf[b % n_x]))
        lo = lo.astype(BF16)
        hi = hi.astype(BF16)
        half = lo.shape[1]
        a = jnp.dot(lo, w1b[:half, :], preferred_element_type=F32)
        a = a + jnp.dot(hi, w1b[half:, :], preferred_element_type=F32)
        g = jnp.dot(lo, w3b[:half, :], preferred_element_type=F32)
        g = g + jnp.dot(hi, w3b[half:, :], preferred_element_type=F32)
        mid = (_silu(a) * g).astype(BF16)
        ybuf[slot] = _rows_to_tiles(_pack_halves(jnp.dot(mid, w2b[...], preferred_element_type=F32)))
        out_copy(b, slot).start(priority=1)
        return carry

    lax.fori_loop(0, n_blk, body, 0)
    total = done + n_blk
    done_ref[0] = total

    @pl.when(e == n_exp - 1)
    def _():
        @pl.when(total >= 2)
        def _():
            out_wait(total % 2)

        @pl.when(total >= 1)
        def _():
            out_wait((total - 1) % 2)

        ybuf[0] = jnp.zeros(ybuf.shape[1:], U32)

        def zcopy(t):
            row = pl.multiple_of(jnp.maximum(fill_ref[N_EXPERTS + t], 0), blk)
            return pltpu.make_async_copy(ybuf.at[0], y_ref.at[pl.ds(row, blk)], out_sem.at[0])

        def zissue(t, carry):
            @pl.when(fill_ref[N_EXPERTS + t] >= 0)
            def _():
                zcopy(t).start()
            return carry

        def zdrain(t, carry):
            @pl.when(fill_ref[N_EXPERTS + t] >= 0)
            def _():
                zcopy(t).wait()
            return carry

        lax.fori_loop(0, N_EXPERTS, zissue, 0)
        lax.fori_loop(0, N_EXPERTS, zdrain, 0)


def _expert_blocks(xs, counts, pstart, fill_rows, w1, w3, w2, blk):
    n_slots = xs.shape[0]
    tile = xs.shape[1:]
    n_exp, d, de = w1.shape
    ahead = 3
    hbm = pl.BlockSpec(memory_space=pl.ANY)
    grid_spec = pltpu.PrefetchScalarGridSpec(
        num_scalar_prefetch=3,
        grid=(n_exp,),
        in_specs=[hbm, hbm, hbm, hbm],
        out_specs=hbm,
        scratch_shapes=[pltpu.VMEM((2, d, de), F32), pltpu.VMEM((2, d, de), F32), pltpu.VMEM((2, de, d), F32),
                        pltpu.VMEM((d, de), BF16), pltpu.VMEM((d, de), BF16), pltpu.VMEM((de, d), BF16),
                        pltpu.VMEM((ahead + 1, blk) + tile, U32), pltpu.VMEM((2, blk) + tile, U32),
                        pltpu.SMEM((1,), jnp.int32),
                        pltpu.SemaphoreType.DMA((2,)), pltpu.SemaphoreType.DMA((ahead + 1,)),
                        pltpu.SemaphoreType.DMA((2,))],
    )
    return pl.pallas_call(
        functools.partial(_expert_kernel, blk, ahead),
        grid_spec=grid_spec,
        out_shape=jax.ShapeDtypeStruct((n_slots,) + tile, U32),
        compiler_params=_params("arbitrary"),
    )(counts, pstart, fill_rows, xs, w1, w3, w2)


def _combine_kernel(tt, n_tiles, dest_ref, x1_ref, g2_ref, gate_ref, yb_ref, o_ref, buf, sems):
    i = pl.program_id(0)

    def copy(tile, slot, r, kk):
        d = dest_ref[(tile * tt + r) * TOP_K + kk]
        return pltpu.make_async_copy(yb_ref.at[d], buf.at[slot, kk, r], sems.at[slot])

    def issue_tile(tile, slot):
        def body(r, carry):
            for kk in range(TOP_K):
                copy(tile, slot, r, kk).start()
            return carry
        lax.fori_loop(0, tt, body, 0, unroll=8)

    def wait_tile(tile, slot):
        for kk in range(TOP_K):
            pltpu.make_async_copy(yb_ref.at[pl.ds(0, tt)], buf.at[slot, kk], sems.at[slot]).wait()

    slot = i % 2

    @pl.when(i == 0)
    def _():
        issue_tile(0, 0)

    @pl.when(i + 1 < n_tiles)
    def _():
        issue_tile(i + 1, 1 - slot)

    wait_tile(i, slot)

    gate = gate_ref[...]
    wa = gate[:, 0:1]
    wb = gate[:, 1:2]
    lo_a, hi_a = _unpack_halves(_tiles_to_rows(buf[slot, 0]))
    lo_b, hi_b = _unpack_halves(_tiles_to_rows(buf[slot, 1]))
    y = jnp.concatenate([wa * lo_a + wb * lo_b, wa * hi_a + wb * hi_b], axis=-1)
    o_ref[...] = x1_ref[...] + g2_ref[0] * y


def _combine(x1, seq, g2, gates, dest_flat, yb):
    n, d = x1.shape
    tile = yb.shape[1:]
    tt = min(512, seq)
    n_tiles = n // tt
    tiles_per_seq = seq // tt
    grid_spec = pltpu.PrefetchScalarGridSpec(
        num_scalar_prefetch=1,
        grid=(n_tiles,),
        in_specs=[pl.BlockSpec((tt, d), lambda i, dr: (i, 0)),
                  pl.BlockSpec((1, 1, d), lambda i, dr: (i // tiles_per_seq, 0, 0)),
                  pl.BlockSpec((tt, LANES), lambda i, dr: (i, 0)),
                  pl.BlockSpec(memory_space=pl.ANY)],
        out_specs=pl.BlockSpec((tt, d), lambda i, dr: (i, 0)),
        scratch_shapes=[pltpu.VMEM((2, TOP_K, tt) + tile, U32), pltpu.SemaphoreType.DMA((2,))],
    )
    return pl.pallas_call(
        functools.partial(_combine_kernel, tt, n_tiles),
        grid_spec=grid_spec,
        out_shape=jax.ShapeDtypeStruct((n, d), F32),
        compiler_params=_params("arbitrary"),
    )(dest_flat, x1, g2, gates, yb)


def _rotation_tables(seq):
    half = HEAD_DIM // 2
    theta = ROPE_BASE ** (-np.arange(half, dtype=np.float64) / half)
    ang = np.arange(seq, dtype=np.float64)[:, None] * theta[None, :]
    cos_t = np.concatenate([np.cos(ang), np.cos(ang)], axis=-1).astype(np.float32)
    sin_t = np.concatenate([-np.sin(ang), np.sin(ang)], axis=-1).astype(np.float32)
    return jnp.asarray(cos_t), jnp.asarray(sin_t)


def _layer(x, c, w_ada, b_ada, norm1_w, w_in, forget_bias, q_norm_w, k_norm_w, ret_norm_w, w_out, norm2_w,
           w_coarse, b_coarse, w_fine, b_fine, w1, w3, w2):
    bsz, seq, d = x.shape
    n = bsz * seq
    d_fox = d // 2
    d_ret = d // 2
    n_heads = d_fox // HEAD_DIM

    mod = _ada_modulation(c, w_ada, b_ada)
    sh1, sc1, g1, sh2, sc2, g2 = [m.reshape(bsz, 1, d) for m in jnp.split(mod, 6, axis=-1)]

    f0 = 3 * d_fox
    w_fox = w_in[:, :f0].astype(BF16)
    w_ret = w_in[:, f0 + n_heads:].astype(BF16)
    w_f = jnp.zeros((d, LANES), BF16).at[:, :n_heads].set(w_in[:, f0:f0 + n_heads].astype(BF16))
    fb = jnp.zeros((1, LANES), F32).at[0, :n_heads].set(forget_bias)

    cos_t, sin_t = _rotation_tables(seq)

    x2d = x.reshape(n, d)
    z, log_f = _input_projection(x2d, seq, norm1_w.reshape(1, d), sc1, sh1, w_fox, w_ret, w_f, cos_t, sin_t,
                                 q_norm_w.reshape(1, HEAD_DIM), k_norm_w.reshape(1, HEAD_DIM), fb)

    lf = log_f[:, :n_heads].reshape(bsz, seq, n_heads).transpose(0, 2, 1).reshape(bsz * n_heads, seq)
    cum = _cumsum_rows(lf)

    qk_bound = 1.02 * LOG2E * HEAD_DIM ** 0.5 * jnp.max(jnp.abs(q_norm_w)) * jnp.max(jnp.abs(k_norm_w))
    o_a = _fox_attention(z, cum, qk_bound, bsz, seq, n_heads)
    log_g = jnp.log(1.0 - 2.0 ** (-5.0 - jnp.arange(n_heads, dtype=F32)))
    o_b = _retention(z, log_g, ret_norm_w.reshape(1, d_ret), bsz, seq, n_heads, 3 * d_fox)

    w_router = jnp.zeros((d, LANES), F32)
    w_router = w_router.at[:, :N_GROUPS].set(w_coarse)
    w_router = w_router.at[:, N_GROUPS:N_GROUPS + N_EXPERTS].set(
        w_fine.transpose(1, 0, 2).reshape(d, N_EXPERTS))
    b_router = jnp.zeros((1, LANES), F32)
    b_router = b_router.at[0, :N_GROUPS].set(b_coarse)
    b_router = b_router.at[0, N_GROUPS:N_GROUPS + N_EXPERTS].set(b_fine.reshape(N_EXPERTS))

    wr_hi = w_router.astype(BF16)
    wr_lo = (w_router - wr_hi.astype(F32)).astype(BF16)
    x1, h_packed, logits = _output_projection(o_a, o_b, w_out.astype(BF16), x2d, seq, g1,
                                              norm2_w.reshape(1, d), sc2, sh2,
                                              jnp.concatenate([wr_hi, wr_lo], axis=1), b_router)

    blk = 256
    nk = n * TOP_K
    n_blocks = nk // blk + N_EXPERTS
    gates, ids, plan = _route(logits, blk, n_blocks)
    pstart = plan[0, :N_EXPERTS]
    fill_rows = plan[1, :2 * N_EXPERTS]
    counts = plan[2, :N_EXPERTS]
    eid = ids[0:TOP_K]
    hit = eid[None] == jnp.arange(N_EXPERTS, dtype=jnp.int32)[:, None, None]
    dest = (jnp.sum(jnp.where(hit, pstart[:, None, None], 0), axis=0) + ids[TOP_K:2 * TOP_K]).T.reshape(nk)

    xs = _dispatch(h_packed, dest, fill_rows, n_blocks * blk, blk)
    yb = _expert_blocks(xs, counts, pstart, fill_rows, w1, w3, w2, blk)
    out = _combine(x1, seq, g2, gates, dest, yb)
    return out.reshape(bsz, seq, d)


def kernel(x, c, w_ada, b_ada, norm1_w, w_in, forget_bias, q_norm_w, k_norm_w, ret_norm_w, w_out, norm2_w,
           w_coarse, b_coarse, w_fine, b_fine, w1, w3, w2):
    c_in = c
    for l in range(w_ada.shape[0]):
        x = _layer(x, c_in, w_ada[l], b_ada[l], norm1_w[l], w_in[l], forget_bias[l], q_norm_w[l],
                   k_norm_w[l], ret_norm_w[l], w_out[l], norm2_w[l], w_coarse[l], b_coarse[l],
                   w_fine[l], b_fine[l], w1[l], w3[l], w2[l])
    return x
```

```python
import functools

import jax
import jax.numpy as jnp
import numpy as np
from jax import lax
from jax.experimental import pallas as pl
from jax.experimental.pallas import tpu as pltpu

HEAD_DIM = 128
N_GROUPS = 4
EXPERTS_PER_GROUP = 8
N_EXPERTS = N_GROUPS * EXPERTS_PER_GROUP
TOP_K = 2
ROPE_BASE = 10000.0
EPS = 1e-6

LANES = 128
VMEM_LIMIT = 56 * 1024 * 1024
NEG_BIG = -1e30
LOG2E = 1.4426950408889634
UNDERFLOW_LOG2 = 160.0

F32 = jnp.float32
BF16 = jnp.bfloat16
U32 = jnp.uint32


def _params(*sem):
    return pltpu.CompilerParams(dimension_semantics=sem, vmem_limit_bytes=VMEM_LIMIT)


def _silu(v):
    return v * (1.0 / (1.0 + jnp.exp(-v)))


def _pack_halves(y):
    w = y.shape[1] // 2
    lo = pltpu.bitcast(y[:, :w].astype(BF16).astype(F32), U32)
    hi = pltpu.bitcast(y[:, w:].astype(BF16).astype(F32), U32)
    return (hi & jnp.uint32(0xFFFF0000)) | (lo >> 16)


def _rows_to_tiles(p):
    return pltpu.einshape("m(ck)->mck", p, c=8, k=LANES)


def _tiles_to_rows(t):
    return pltpu.einshape("mck->m(ck)", t)


def _unpack_halves(p):
    lo = pltpu.bitcast(p << 16, F32)
    hi = pltpu.bitcast(p & jnp.uint32(0xFFFF0000), F32)
    return lo, hi


def _ada_kernel(ct_ref, w_ref, b_ref, o_ref):
    w = w_ref[...]
    rows = []
    for b in range(o_ref.shape[0]):
        if b < 2:
            cb = _silu(ct_ref[:, b:b + 1])
            rows.append(jnp.sum(cb * w, axis=0, keepdims=True) + b_ref[...])
        else:
            rows.append(jnp.zeros_like(b_ref[...]))
    o_ref[...] = jnp.concatenate(rows, axis=0)


def _ada_modulation(c, w_ada, b_ada):
    bsz, d = c.shape
    n = w_ada.shape[1]
    tn = 1024
    ct = jnp.zeros((d, LANES), F32).at[:, :bsz].set(c.T)
    out = pl.pallas_call(
        _ada_kernel,
        grid=(n // tn,),
        in_specs=[pl.BlockSpec((d, LANES), lambda j: (0, 0)),
                  pl.BlockSpec((d, tn), lambda j: (0, j)),
                  pl.BlockSpec((1, tn), lambda j: (0, j))],
        out_specs=pl.BlockSpec((8, tn), lambda j: (0, j)),
        out_shape=jax.ShapeDtypeStruct((8, n), F32),
        compiler_params=_params("arbitrary"),
    )(ct, w_ada, b_ada.reshape(1, n))
    return out[:bsz]


def _inproj_kernel(q_t, r_t, x_ref, nw_ref, sc_ref, sh_ref, wa_ref, wb_ref, wf_ref, cos_ref, sin_ref,
                   qw_ref, kw_ref, fb_ref, z_ref, f_ref, h_ref):
    j = pl.program_id(1)
    r0 = 3 * q_t

    @pl.when(j == 0)
    def _():
        x = x_ref[...]
        ms = jnp.mean(x * x, axis=-1, keepdims=True)
        y = x * lax.rsqrt(ms + EPS) * nw_ref[...]
        h = (y * (1.0 + sc_ref[0]) + sh_ref[0]).astype(BF16)
        h_ref[...] = h
        t = jnp.dot(h, wf_ref[...], preferred_element_type=F32) + fb_ref[...]
        f_ref[...] = jnp.minimum(t, 0.0) - jnp.log(1.0 + jnp.exp(-jnp.abs(t)))

    def heads_of(acc):
        return [acc[:, hh * HEAD_DIM:(hh + 1) * HEAD_DIM] for hh in range(acc.shape[1] // HEAD_DIM)]

    def head_norm(acc, w_row):
        outs = []
        for a in heads_of(acc):
            ms = jnp.mean(a * a, axis=-1, keepdims=True)
            outs.append(a * lax.rsqrt(ms + EPS) * w_row)
        return jnp.concatenate(outs, axis=-1).astype(BF16)

    def rotate(acc, scale):
        cs = cos_ref[...] * scale
        sn = sin_ref[...] * scale
        outs = [a * cs + pltpu.roll(a, HEAD_DIM // 2, 1) * sn for a in heads_of(acc)]
        return jnp.concatenate(outs, axis=-1).astype(BF16)

    def fox():
        return jnp.dot(h_ref[...], wa_ref[...], preferred_element_type=F32)

    def ret():
        return jnp.dot(h_ref[...], wb_ref[...], preferred_element_type=F32)

    @pl.when(j < q_t)
    def _():
        z_ref[...] = head_norm(fox(), qw_ref[...] * (LOG2E * HEAD_DIM ** -0.5))

    @pl.when((j >= q_t) & (j < 2 * q_t))
    def _():
        z_ref[...] = head_norm(fox(), kw_ref[...])

    @pl.when((j >= 2 * q_t) & (j < r0))
    def _():
        z_ref[...] = fox().astype(BF16)

    @pl.when((j >= r0) & (j < r0 + r_t))
    def _():
        z_ref[...] = rotate(ret(), 1.0)

    @pl.when((j >= r0 + r_t) & (j < r0 + 2 * r_t))
    def _():
        z_ref[...] = rotate(ret(), HEAD_DIM ** -0.5)

    @pl.when(j >= r0 + 2 * r_t)
    def _():
        z_ref[...] = ret().astype(BF16)


def _input_projection(x2d, seq, norm_w, sc1, sh1, w_fox, w_ret, w_f, cos_t, sin_t, qw, kw, fb):
    n, d = x2d.shape
    tm, tn = min(1024, seq), 1024
    fox_tiles = w_fox.shape[1] // tn
    ret_tiles = w_ret.shape[1] // tn
    tiles_per_seq = seq // tm
    kern = functools.partial(_inproj_kernel, fox_tiles // 3, ret_tiles // 4)
    bsel = lambda i, j: (i // tiles_per_seq, 0, 0)
    const = lambda i, j: (0, 0)
    return pl.pallas_call(
        kern,
        grid=(n // tm, fox_tiles + ret_tiles),
        in_specs=[pl.BlockSpec((tm, d), lambda i, j: (i, 0)),
                  pl.BlockSpec((1, d), const),
                  pl.BlockSpec((1, 1, d), bsel),
                  pl.BlockSpec((1, 1, d), bsel),
                  pl.BlockSpec((d, tn), lambda i, j: (0, jnp.minimum(j, fox_tiles - 1))),
                  pl.BlockSpec((d, tn), lambda i, j: (0, jnp.maximum(j - fox_tiles, 0))),
                  pl.BlockSpec((d, LANES), const),
                  pl.BlockSpec((tm, HEAD_DIM), lambda i, j: (i % tiles_per_seq, 0)),
                  pl.BlockSpec((tm, HEAD_DIM), lambda i, j: (i % tiles_per_seq, 0)),
                  pl.BlockSpec((1, HEAD_DIM), const),
                  pl.BlockSpec((1, HEAD_DIM), const),
                  pl.BlockSpec((1, LANES), const)],
        out_specs=[pl.BlockSpec((tm, tn), lambda i, j: (i, j)),
                   pl.BlockSpec((tm, LANES), lambda i, j: (i, 0))],
        out_shape=[jax.ShapeDtypeStruct((n, w_fox.shape[1] + w_ret.shape[1]), BF16),
                   jax.ShapeDtypeStruct((n, LANES), F32)],
        scratch_shapes=[pltpu.VMEM((tm, d), BF16)],
        compiler_params=_params("arbitrary", "arbitrary"),
    )(x2d, norm_w, sc1, sh1, w_fox, w_ret, w_f, cos_t, sin_t, qw, kw, fb)


def _cumsum_kernel(x_ref, o_ref):
    x = x_ref[0]
    r = x.shape[0]
    a = lax.broadcasted_iota(jnp.int32, (LANES, LANES), 0)
    b = lax.broadcasted_iota(jnp.int32, (LANES, LANES), 1)
    upper = (a <= b).astype(F32)
    within = jnp.dot(x, upper, precision=lax.Precision.HIGHEST, preferred_element_type=F32)
    tot = jnp.broadcast_to(within[:, LANES - 1:LANES], (r, LANES))
    ra = lax.broadcasted_iota(jnp.int32, (r, r), 0)
    rb = lax.broadcasted_iota(jnp.int32, (r, r), 1)
    strict = (rb < ra).astype(F32)
    before = jnp.dot(strict, tot, precision=lax.Precision.HIGHEST, preferred_element_type=F32)
    o_ref[0] = within + before


def _cumsum_rows(x):
    g, s = x.shape
    r = s // LANES
    out = pl.pallas_call(
        _cumsum_kernel,
        grid=(g,),
        in_specs=[pl.BlockSpec((1, r, LANES), lambda i: (i, 0, 0))],
        out_specs=pl.BlockSpec((1, r, LANES), lambda i: (i, 0, 0)),
        out_shape=jax.ShapeDtypeStruct((g, r, LANES), F32),
        compiler_params=_params("arbitrary"),
    )(x.reshape(g, r, LANES))
    return out.reshape(g, 1, s)


def _fox_kernel(tq, n_sub, first_ref, q_ref, k_ref, v_ref, cum_ref, o_ref, s_refs, m_ref, l_ref, acc_ref):
    group_id = pl.program_id(2)
    n_groups = pl.num_programs(2)
    head = pl.program_id(0) * pl.num_programs(1) + pl.program_id(1)
    n_slabs = tq // LANES

    m_ref[...] = jnp.full(m_ref.shape, NEG_BIG, F32)
    l_ref[...] = jnp.zeros(l_ref.shape, F32)
    acc_ref[...] = jnp.zeros(acc_ref.shape, F32)

    class Sub:
        def __init__(self, idx):
            self.rows = slice(idx * tq, (idx + 1) * tq)
            self.qi = n_sub * group_id + idx
            self.sa, self.sb = s_refs[2 * idx], s_refs[2 * idx + 1]
            q_start = pl.multiple_of(self.qi * tq, tq)
            self.c0 = cum_ref[0, :, pl.ds(q_start, LANES)][:, 0:1]
            self.first = first_ref[(head * n_groups + group_id) * n_sub + idx]
            self.n_off = self.qi - self.first

    def scores(sub, kb, s_ref):
        start = pl.multiple_of(kb * tq, tq)
        k = k_ref[pl.ds(start, tq), :]
        bias = (sub.c0 - cum_ref[0, :, pl.ds(start, tq)]) * LOG2E
        s_ref[...] = lax.dot_general(q_ref[sub.rows, :], k, (((1,), (1,)), ((), ())),
                                     preferred_element_type=F32) + bias

    def softmax_pv(sub, kb, s_ref, masked):
        rs = sub.rows
        start = pl.multiple_of(kb * tq, tq)
        v = v_ref[pl.ds(start, tq), :]
        slabs = []
        for j in range(n_slabs):
            t = s_ref[:, j * LANES:(j + 1) * LANES]
            if masked:
                row = lax.broadcasted_iota(jnp.int32, t.shape, 0)
                col = lax.broadcasted_iota(jnp.int32, t.shape, 1) + j * LANES
                t = jnp.where(col <= row, t, NEG_BIG)
            slabs.append(t)
        mx = slabs[0]
        for t in slabs[1:]:
            mx = jnp.maximum(mx, t)
        m_prev = m_ref[rs, :]
        m_new = jnp.maximum(m_prev, jnp.max(mx, axis=-1, keepdims=True))
        alpha = jnp.exp2(m_prev - m_new)
        probs = [jnp.exp2(t - m_new) for t in slabs]
        psum = probs[0]
        for t in probs[1:]:
            psum = psum + t
        l_ref[rs, :] = alpha * l_ref[rs, :] + psum
        p = jnp.concatenate([t.astype(BF16) for t in probs], axis=-1)
        acc_ref[rs, :] = alpha * acc_ref[rs, :] + jnp.dot(p, v, preferred_element_type=F32)
        m_ref[rs, :] = m_new

    def sweep(sub, then):
        def pair(kb):
            scores(sub, kb + 1, sub.sb)
            softmax_pv(sub, kb, sub.sa, False)
            scores(sub, kb + 2, sub.sa)
            softmax_pv(sub, kb + 1, sub.sb, False)

        def body4(i, carry):
            pair(sub.first + 4 * i)
            pair(sub.first + 4 * i + 2)
            return carry

        def body2(i, carry):
            pair(sub.first + 2 * i)
            return carry

        n4 = sub.n_off // 4
        lax.fori_loop(0, n4, body4, 0)
        lax.fori_loop(2 * n4, sub.n_off // 2, body2, 0)

        @pl.when(sub.n_off % 2 == 0)
        def _():
            then()
            softmax_pv(sub, sub.qi, sub.sa, True)

        @pl.when(sub.n_off % 2 == 1)
        def _():
            scores(sub, sub.qi, sub.sb)
            softmax_pv(sub, sub.qi - 1, sub.sa, False)
            then()
            softmax_pv(sub, sub.qi, sub.sb, True)

    subs = [Sub(idx) for idx in range(n_sub)]
    scores(subs[0], subs[0].first, subs[0].sa)
    for sub, nxt in zip(subs, subs[1:] + [None]):
        sweep(sub, (lambda: None) if nxt is None else functools.partial(scores, nxt, nxt.first, nxt.sa))

    o_ref[...] = (acc_ref[...] / jnp.sum(l_ref[...], axis=-1, keepdims=True)).astype(BF16)


def _first_live_block(cum, tq, qk_bound):
    c0 = cum[:, 0, ::tq]
    cend = cum[:, 0, tq - 1::tq]
    gap = (c0[:, :, None] - cend[:, None, :]) * LOG2E + 2.0 * qk_bound
    nq = c0.shape[1]
    earlier = jnp.arange(nq)[None, :] < jnp.arange(nq)[:, None]
    return jnp.sum((gap < -UNDERFLOW_LOG2) & earlier[None], axis=-1).astype(jnp.int32).reshape(-1)


def _fox_attention(z, cum, qk_bound, bsz, seq, n_heads):
    n_sub = 4
    tq = min(512, seq // n_sub)
    nq = seq // tq
    n_groups = nq // n_sub
    rows = n_sub * tq
    kern = functools.partial(_fox_kernel, tq, n_sub)
    grid_spec = pltpu.PrefetchScalarGridSpec(
        num_scalar_prefetch=1,
        grid=(bsz, n_heads, n_groups),
        in_specs=[pl.BlockSpec((rows, HEAD_DIM), lambda b, h, i, f: (b * n_groups + i, h)),
                  pl.BlockSpec((seq, HEAD_DIM), lambda b, h, i, f: (b, n_heads + h)),
                  pl.BlockSpec((seq, HEAD_DIM), lambda b, h, i, f: (b, 2 * n_heads + h)),
                  pl.BlockSpec((1, 1, seq), lambda b, h, i, f: (b * n_heads + h, 0, 0))],
        out_specs=pl.BlockSpec((rows, HEAD_DIM), lambda b, h, i, f: (b * n_groups + i, h)),
        scratch_shapes=[[pltpu.VMEM((tq, tq), F32)] * (2 * n_sub),
                        pltpu.VMEM((rows, LANES), F32), pltpu.VMEM((rows, LANES), F32),
                        pltpu.VMEM((rows, HEAD_DIM), F32)],
    )
    return pl.pallas_call(
        kern,
        grid_spec=grid_spec,
        out_shape=jax.ShapeDtypeStruct((bsz * seq, n_heads * HEAD_DIM), BF16),
        compiler_params=_params("arbitrary", "arbitrary", "arbitrary"),
    )(_first_live_block(cum, tq, qk_bound), z, z, z, cum)


def _ret_kernel(chunk, n_heads, lg_ref, q_ref, k_ref, v_ref, g_ref, nw_ref, o_ref, state_ref, decay_ref,
                qdec_ref, kdec_ref):
    first = (pl.program_id(0) == 0) & (pl.program_id(1) == 0)

    @pl.when(first)
    def _():
        i = lax.broadcasted_iota(jnp.int32, (chunk, chunk), 0)
        jj = lax.broadcasted_iota(jnp.int32, (chunk, chunk), 1)
        diff = (i - jj).astype(F32)
        pos = lax.broadcasted_iota(jnp.int32, (chunk, HEAD_DIM), 0).astype(F32)
        for h in range(n_heads):
            decay_ref[h] = jnp.where(diff >= 0, jnp.exp(lg_ref[h] * jnp.maximum(diff, 0.0)), 0.0)
            qdec_ref[h] = jnp.exp(lg_ref[h] * (pos + 1.0))
            kdec_ref[h] = jnp.exp(lg_ref[h] * (chunk - 1.0 - pos))

    @pl.when(pl.program_id(1) == 0)
    def _():
        state_ref[...] = jnp.zeros(state_ref.shape, F32)

    for h in range(n_heads):
        log_g = lg_ref[h]
        cols = slice(h * HEAD_DIM, (h + 1) * HEAD_DIM)
        q = q_ref[:, cols]
        k = k_ref[:, cols]
        v = v_ref[:, cols]
        scores = lax.dot_general(q, k, (((1,), (1,)), ((), ())), preferred_element_type=F32)
        scores = scores * decay_ref[h]
        intra = jnp.dot(scores.astype(BF16), v, preferred_element_type=F32)
        state = state_ref[h]
        inter = jnp.dot(q, state.astype(BF16), preferred_element_type=F32) * qdec_ref[h]
        kd = (k.astype(F32) * kdec_ref[h]).astype(BF16)
        kv = lax.dot_general(kd, v, (((0,), (0,)), ((), ())), preferred_element_type=F32)
        state_ref[h] = state * jnp.exp(jnp.full((1, HEAD_DIM), chunk, F32) * log_g) + kv
        o = intra + inter
        ms = jnp.mean(o * o, axis=-1, keepdims=True)
        o = o * lax.rsqrt(ms + EPS) * nw_ref[:, cols]
        o_ref[:, cols] = (o * _silu(g_ref[:, cols].astype(F32))).astype(BF16)


def _retention(z, log_g, norm_w, bsz, seq, n_heads, col0):
    chunk = min(256, seq)
    nt = seq // chunk
    width = n_heads * HEAD_DIM
    c0 = col0 // width
    kern = functools.partial(_ret_kernel, chunk, n_heads)

    def sec(s):
        return pl.BlockSpec((chunk, width), lambda b, t, lg: (b * nt + t, c0 + s))

    grid_spec = pltpu.PrefetchScalarGridSpec(
        num_scalar_prefetch=1,
        grid=(bsz, nt),
        in_specs=[sec(0), sec(1), sec(2), sec(3), pl.BlockSpec((1, width), lambda b, t, lg: (0, 0))],
        out_specs=pl.BlockSpec((chunk, width), lambda b, t, lg: (b * nt + t, 0)),
        scratch_shapes=[pltpu.VMEM((n_heads, HEAD_DIM, HEAD_DIM), F32),
                        pltpu.VMEM((n_heads, chunk, chunk), F32),
                        pltpu.VMEM((n_heads, chunk, HEAD_DIM), F32),
                        pltpu.VMEM((n_heads, chunk, HEAD_DIM), F32)],
    )
    return pl.pallas_call(
        kern,
        grid_spec=grid_spec,
        out_shape=jax.ShapeDtypeStruct((bsz * seq, width), BF16),
        compiler_params=_params("arbitrary", "arbitrary"),
    )(log_g, z, z, z, z, norm_w)


def _outproj_kernel(oa_ref, ob_ref, wa_ref, wb_ref, x_ref, g1_ref, nw_ref, sc_ref, sh_ref, wr_ref, br_ref,
                    x1_ref, hp_ref, lg_ref):
    mix = jnp.dot(oa_ref[...], wa_ref[...], preferred_element_type=F32)
    mix = mix + jnp.dot(ob_ref[...], wb_ref[...], preferred_element_type=F32)
    x1 = x_ref[...] + g1_ref[0] * mix
    x1_ref[...] = x1
    ms = jnp.mean(x1 * x1, axis=-1, keepdims=True)
    h2 = x1 * lax.rsqrt(ms + EPS) * nw_ref[...] * (1.0 + sc_ref[0]) + sh_ref[0]
    hp_ref[...] = _rows_to_tiles(_pack_halves(h2))
    h_hi = h2.astype(BF16)
    h_lo = (h2 - h_hi.astype(F32)).astype(BF16)
    both = jnp.dot(h_hi, wr_ref[...], preferred_element_type=F32)
    cross = jnp.dot(h_lo, wr_ref[:, :LANES], preferred_element_type=F32)
    lg_ref[...] = both[:, :LANES] + both[:, LANES:] + cross + br_ref[...]


def _output_projection(o_a, o_b, w_out, x2d, seq, g1, norm_w, sc2, sh2, w_router, b_router):
    n, d = x2d.shape
    da = o_a.shape[1]
    tm = min(512, seq)
    tiles_per_seq = seq // tm
    bsel = lambda i: (i // tiles_per_seq, 0, 0)
    return pl.pallas_call(
        _outproj_kernel,
        grid=(n // tm,),
        in_specs=[pl.BlockSpec((tm, da), lambda i: (i, 0)),
                  pl.BlockSpec((tm, da), lambda i: (i, 0)),
                  pl.BlockSpec((da, d), lambda i: (0, 0)),
                  pl.BlockSpec((da, d), lambda i: (1, 0)),
                  pl.BlockSpec((tm, d), lambda i: (i, 0)),
                  pl.BlockSpec((1, 1, d), bsel),
                  pl.BlockSpec((1, d), lambda i: (0, 0)),
                  pl.BlockSpec((1, 1, d), bsel),
                  pl.BlockSpec((1, 1, d), bsel),
                  pl.BlockSpec((d, 2 * LANES), lambda i: (0, 0)),
                  pl.BlockSpec((1, LANES), lambda i: (0, 0))],
        out_specs=[pl.BlockSpec((tm, d), lambda i: (i, 0)),
                   pl.BlockSpec((tm, d // 2 // LANES, LANES), lambda i: (i, 0, 0)),
                   pl.BlockSpec((tm, LANES), lambda i: (i, 0))],
        out_shape=[jax.ShapeDtypeStruct((n, d), F32),
                   jax.ShapeDtypeStruct((n, d // 2 // LANES, LANES), U32),
                   jax.ShapeDtypeStruct((n, LANES), F32)],
        compiler_params=_params("arbitrary"),
    )(o_a, o_b, w_out, w_out, x2d, g1, norm_w, sc2, sh2, w_router, b_router)


def _route_kernel(blk, n_blocks, lg_ref, gate_ref, ids_ref, plan_ref, run_ref):
    i = pl.program_id(0)

    @pl.when(i == 0)
    def _():
        run_ref[...] = jnp.zeros(run_ref.shape, F32)

    lg = lg_ref[...]
    tt = lg.shape[0]
    lane = lax.broadcasted_iota(jnp.int32, lg.shape, 1).astype(F32)
    big = 1e6

    def rmax(v):
        return jnp.max(v, axis=-1, keepdims=True)

    def rmin(v):
        return jnp.min(v, axis=-1, keepdims=True)

    def rsum(v):
        return jnp.sum(v, axis=-1, keepdims=True)

    cmask = lane < N_GROUPS
    cm = jnp.where(cmask, lg, NEG_BIG)
    ce = jnp.where(cmask, jnp.exp(cm - rmax(cm)), 0.0)
    pgrp = ce / rsum(ce)
    p_g = rmax(pgrp)
    g_sel = rmin(jnp.where(cmask & (pgrp == p_g), lane, big))

    lo = N_GROUPS + EXPERTS_PER_GROUP * g_sel
    fmask = (lane >= lo) & (lane < lo + EXPERTS_PER_GROUP)
    fm = jnp.where(fmask, lg, NEG_BIG)
    fe = jnp.where(fmask, jnp.exp(fm - rmax(fm)), 0.0)
    fp = fe / rsum(fe)
    fp = jnp.where(fmask, fp, -1.0)
    p1 = rmax(fp)
    i1 = rmin(jnp.where(fp == p1, lane, big))
    fp2 = jnp.where(lane == i1, -1.0, fp)
    p2 = rmax(fp2)
    i2 = rmin(jnp.where(fp2 == p2, lane, big))
    denom = p1 + p2
    w1 = p_g * p1 / denom
    w2 = p_g * p2 / denom
    e1 = i1 - N_GROUPS
    e2 = i2 - N_GROUPS

    gate_ref[...] = jnp.where(lane == 0, w1, jnp.where(lane == 1, w2, 0.0))

    oh1 = (lane == e1).astype(F32)
    oh2 = (lane == e2).astype(F32)
    both = oh1 + oh2
    ra = lax.broadcasted_iota(jnp.int32, (tt, tt), 0)
    rb = lax.broadcasted_iota(jnp.int32, (tt, tt), 1)
    strict = (rb < ra).astype(BF16)
    prefix = jnp.dot(strict, both.astype(BF16), preferred_element_type=F32) + run_ref[...]
    r1 = rsum(prefix * oh1)
    r2 = rsum(prefix * oh2)
    run_ref[...] = run_ref[...] + jnp.sum(both, axis=0, keepdims=True)

    packed = jnp.where(lane == 0, e1, jnp.where(lane == 1, e2, jnp.where(lane == 2, r1,
                                                                        jnp.where(lane == 3, r2, 0.0))))
    ids_ref[...] = jnp.transpose(packed)[:8, :].astype(jnp.int32)

    @pl.when(i == pl.num_programs(0) - 1)
    def _():
        cnt = jnp.broadcast_to(run_ref[...], (8, LANES))
        lane8 = lax.broadcasted_iota(jnp.int32, (8, LANES), 1)
        padded = jnp.floor((cnt + (blk - 1.0)) * (1.0 / blk)) * blk
        pend = padded
        for sh in (1, 2, 4, 8, 16, 32, 64):
            pend = pend + jnp.where(lane8 >= sh, pltpu.roll(pend, sh, 1), 0.0)
        pstart = pend - padded
        total = jnp.max(pend, axis=-1, keepdims=True)
        tail = total + (lane8 - N_EXPERTS).astype(F32) * blk
        fill = jnp.where(lane8 < N_EXPERTS, jnp.where(padded > 0, pend - blk, -1.0),
                         jnp.where((lane8 < 2 * N_EXPERTS) & (tail < n_blocks * blk), tail, -1.0))
        row8 = lax.broadcasted_iota(jnp.int32, (8, LANES), 0)
        plan_ref[...] = jnp.where(row8 == 0, pstart, jnp.where(row8 == 1, fill,
                                                               jnp.where(row8 == 2, cnt, 0.0))).astype(jnp.int32)


def _route(logits, blk, n_blocks):
    n = logits.shape[0]
    tt = min(512, n)
    blkspec = lambda: pl.BlockSpec((tt, LANES), lambda i: (i, 0))
    return pl.pallas_call(
        functools.partial(_route_kernel, blk, n_blocks),
        grid=(n // tt,),
        in_specs=[blkspec()],
        out_specs=[blkspec(),
                   pl.BlockSpec((8, tt), lambda i: (0, i)),
                   pl.BlockSpec((8, LANES), lambda i: (0, 0))],
        out_shape=[jax.ShapeDtypeStruct((n, LANES), F32),
                   jax.ShapeDtypeStruct((8, n), jnp.int32),
                   jax.ShapeDtypeStruct((8, LANES), jnp.int32)],
        scratch_shapes=[pltpu.VMEM((1, LANES), F32)],
        compiler_params=_params("arbitrary"),
    )(logits)


def _dispatch_kernel(tt, blk, n_fill, dest_ref, fill_ref, h_ref, xs_ref, zero_ref, sem, zsem):
    i = pl.program_id(0)
    base = i * (tt * TOP_K)

    @pl.when(i == 0)
    def _():
        zero_ref[...] = jnp.zeros(zero_ref.shape, U32)

        def zcopy(z):
            row = pl.multiple_of(jnp.maximum(fill_ref[z], 0), blk)
            return pltpu.make_async_copy(zero_ref, xs_ref.at[pl.ds(row, blk)], zsem)

        def zissue(z, carry):
            @pl.when(fill_ref[z] >= 0)
            def _():
                zcopy(z).start()
            return carry

        def zdrain(z, carry):
            @pl.when(fill_ref[z] >= 0)
            def _():
                zcopy(z).wait()
            return carry

        lax.fori_loop(0, n_fill, zissue, 0)
        lax.fori_loop(0, n_fill, zdrain, 0)

    def copy(r, kk):
        d = dest_ref[base + r * TOP_K + kk]
        return pltpu.make_async_copy(h_ref.at[r], xs_ref.at[d], sem)

    def issue(r, carry):
        for kk in range(TOP_K):
            copy(r, kk).start()
        return carry

    lax.fori_loop(0, tt, issue, 0, unroll=8)
    for _ in range(TOP_K):
        pltpu.make_async_copy(h_ref, xs_ref.at[pl.ds(0, tt)], sem).wait()


def _dispatch(h_packed, dest_flat, fill_rows, n_slots, blk):
    n = h_packed.shape[0]
    tile = h_packed.shape[1:]
    tt = min(512, n)
    n_fill = fill_rows.shape[0]
    grid_spec = pltpu.PrefetchScalarGridSpec(
        num_scalar_prefetch=2,
        grid=(n // tt,),
        in_specs=[pl.BlockSpec((tt,) + tile, lambda i, d, f: (i, 0, 0))],
        out_specs=pl.BlockSpec(memory_space=pl.ANY),
        scratch_shapes=[pltpu.VMEM((blk,) + tile, U32), pltpu.SemaphoreType.DMA(()), pltpu.SemaphoreType.DMA(())],
    )
    return pl.pallas_call(
        functools.partial(_dispatch_kernel, tt, blk, n_fill),
        grid_spec=grid_spec,
        out_shape=jax.ShapeDtypeStruct((n_slots,) + tile, U32),
        compiler_params=_params("arbitrary"),
    )(dest_flat, fill_rows, h_packed)


def _expert_kernel(blk, ahead, cnt_ref, pstart_ref, fill_ref, xs_ref, w1_ref, w3_ref, w2_ref, y_ref,
                   w1f, w3f, w2f, w1b, w3b, w2b, xbuf, ybuf, done_ref, w_sem, in_sem, out_sem):
    e = pl.program_id(0)
    n_exp = pl.num_programs(0)
    wslot = e % 2
    n_blk = (cnt_ref[e] + (blk - 1)) // blk
    base = pstart_ref[e]
    n_x = xbuf.shape[0]

    def weight_copies(ex, slot):
        return [pltpu.make_async_copy(src.at[ex], dst.at[slot], w_sem.at[slot])
                for src, dst in ((w1_ref, w1f), (w3_ref, w3f), (w2_ref, w2f))]

    def rows(b):
        return pl.ds(pl.multiple_of(base + b * blk, blk), blk)

    def in_copy(b, slot):
        return pltpu.make_async_copy(xs_ref.at[rows(b)], xbuf.at[slot], in_sem.at[slot])

    def out_copy(b, slot):
        return pltpu.make_async_copy(ybuf.at[slot], y_ref.at[rows(b)], out_sem.at[slot])

    @pl.when(e == 0)
    def _():
        for c in weight_copies(0, 0):
            c.start()

    for p in range(ahead):
        @pl.when(p < n_blk)
        def _():
            in_copy(p, p).start()

    @pl.when(e + 1 < n_exp)
    def _():
        for c in weight_copies(e + 1, 1 - wslot):
            c.start()

    for c in weight_copies(e, wslot):
        c.wait()
    w1b[...] = w1f[wslot].astype(BF16)
    w3b[...] = w3f[wslot].astype(BF16)
    w2b[...] = w2f[wslot].astype(BF16)

    @pl.when(e == 0)
    def _():
        done_ref[0] = 0

    done = done_ref[0]

    def out_wait(slot):
        pltpu.make_async_copy(ybuf.at[slot], y_ref.at[pl.ds(0, blk)], out_sem.at[slot]).wait()

    def body(b, carry):
        slot = (done + b) % 2

        @pl.when(b + ahead < n_blk)
        def _():
            in_copy(b + ahead, (b + ahead) % n_x).start()

        in_copy(b, b % n_x).wait()

        @pl.when(done + b >= 2)
        def _():
            out_wait(slot)

        lo, hi = _unpack_halves(_tiles_to_rows(xbuf[b % n_x]))
        lo = lo.astype(BF16)
        hi = hi.astype(BF16)
        half = lo.shape[1]
        a = jnp.dot(lo, w1b[:half, :], preferred_element_type=F32)
        a = a + jnp.dot(hi, w1b[half:, :], preferred_element_type=F32)
        g = jnp.dot(lo, w3b[:half, :], preferred_element_type=F32)
        g = g + jnp.dot(hi, w3b[half:, :], preferred_element_type=F32)
        mid = (_silu(a) * g).astype(BF16)
        ybuf[slot] = _rows_to_tiles(_pack_halves(jnp.dot(mid, w2b[...], preferred_element_type=F32)))
        out_copy(b, slot).start(priority=1)
        return carry

    lax.fori_loop(0, n_blk, body, 0)
    total = done + n_blk
    done_ref[0] = total

    @pl.when(e == n_exp - 1)
    def _():
        @pl.when(total >= 2)
        def _():
            out_wait(total % 2)

        @pl.when(total >= 1)
        def _():
            out_wait((total - 1) % 2)

        ybuf[0] = jnp.zeros(ybuf.shape[1:], U32)

        def zcopy(t):
            row = pl.multiple_of(jnp.maximum(fill_ref[N_EXPERTS + t], 0), blk)
            return pltpu.make_async_copy(ybuf.at[0], y_ref.at[pl.ds(row, blk)], out_sem.at[0])

        def zissue(t, carry):
            @pl.when(fill_ref[N_EXPERTS + t] >= 0)
            def _():
                zcopy(t).start()
            return carry

        def zdrain(t, carry):
            @pl.when(fill_ref[N_EXPERTS + t] >= 0)
            def _():
                zcopy(t).wait()
            return carry

        lax.fori_loop(0, N_EXPERTS, zissue, 0)
        lax.fori_loop(0, N_EXPERTS, zdrain, 0)


def _expert_blocks(xs, counts, pstart, fill_rows, w1, w3, w2, blk):
    n_slots = xs.shape[0]
    tile = xs.shape[1:]
    n_exp, d, de = w1.shape
    ahead = 3
    hbm = pl.BlockSpec(memory_space=pl.ANY)
    grid_spec = pltpu.PrefetchScalarGridSpec(
        num_scalar_prefetch=3,
        grid=(n_exp,),
        in_specs=[hbm, hbm, hbm, hbm],
        out_specs=hbm,
        scratch_shapes=[pltpu.VMEM((2, d, de), F32), pltpu.VMEM((2, d, de), F32), pltpu.VMEM((2, de, d), F32),
                        pltpu.VMEM((d, de), BF16), pltpu.VMEM((d, de), BF16), pltpu.VMEM((de, d), BF16),
                        pltpu.VMEM((ahead + 1, blk) + tile, U32), pltpu.VMEM((2, blk) + tile, U32),
                        pltpu.SMEM((1,), jnp.int32),
                        pltpu.SemaphoreType.DMA((2,)), pltpu.SemaphoreType.DMA((ahead + 1,)),
                        pltpu.SemaphoreType.DMA((2,))],
    )
    return pl.pallas_call(
        functools.partial(_expert_kernel, blk, ahead),
        grid_spec=grid_spec,
        out_shape=jax.ShapeDtypeStruct((n_slots,) + tile, U32),
        compiler_params=_params("arbitrary"),
    )(counts, pstart, fill_rows, xs, w1, w3, w2)


def _combine_kernel(tt, n_tiles, dest_ref, x1_ref, g2_ref, gate_ref, yb_ref, o_ref, buf, sems):
    i = pl.program_id(0)

    def copy(tile, slot, r, kk):
        d = dest_ref[(tile * tt + r) * TOP_K + kk]
        return pltpu.make_async_copy(yb_ref.at[d], buf.at[slot, kk, r], sems.at[slot])

    def issue_tile(tile, slot):
        def body(r, carry):
            for kk in range(TOP_K):
                copy(tile, slot, r, kk).start()
            return carry
        lax.fori_loop(0, tt, body, 0, unroll=8)

    def wait_tile(tile, slot):
        for kk in range(TOP_K):
            pltpu.make_async_copy(yb_ref.at[pl.ds(0, tt)], buf.at[slot, kk], sems.at[slot]).wait()

    slot = i % 2

    @pl.when(i == 0)
    def _():
        issue_tile(0, 0)

    @pl.when(i + 1 < n_tiles)
    def _():
        issue_tile(i + 1, 1 - slot)

    wait_tile(i, slot)

    gate = gate_ref[...]
    wa = gate[:, 0:1]
    wb = gate[:, 1:2]
    lo_a, hi_a = _unpack_halves(_tiles_to_rows(buf[slot, 0]))
    lo_b, hi_b = _unpack_halves(_tiles_to_rows(buf[slot, 1]))
    y = jnp.concatenate([wa * lo_a + wb * lo_b, wa * hi_a + wb * hi_b], axis=-1)
    o_ref[...] = x1_ref[...] + g2_ref[0] * y


def _combine(x1, seq, g2, gates, dest_flat, yb):
    n, d = x1.shape
    tile = yb.shape[1:]
    tt = min(512, seq)
    n_tiles = n // tt
    tiles_per_seq = seq // tt
    grid_spec = pltpu.PrefetchScalarGridSpec(
        num_scalar_prefetch=1,
        grid=(n_tiles,),
        in_specs=[pl.BlockSpec((tt, d), lambda i, dr: (i, 0)),
                  pl.BlockSpec((1, 1, d), lambda i, dr: (i // tiles_per_seq, 0, 0)),
                  pl.BlockSpec((tt, LANES), lambda i, dr: (i, 0)),
                  pl.BlockSpec(memory_space=pl.ANY)],
        out_specs=pl.BlockSpec((tt, d), lambda i, dr: (i, 0)),
        scratch_shapes=[pltpu.VMEM((2, TOP_K, tt) + tile, U32), pltpu.SemaphoreType.DMA((2,))],
    )
    return pl.pallas_call(
        functools.partial(_combine_kernel, tt, n_tiles),
        grid_spec=grid_spec,
        out_shape=jax.ShapeDtypeStruct((n, d), F32),
        compiler_params=_params("arbitrary"),
    )(dest_flat, x1, g2, gates, yb)


def _rotation_tables(seq):
    half = HEAD_DIM // 2
    theta = ROPE_BASE ** (-np.arange(half, dtype=np.float64) / half)
    ang = np.arange(seq, dtype=np.float64)[:, None] * theta[None, :]
    cos_t = np.concatenate([np.cos(ang), np.cos(ang)], axis=-1).astype(np.float32)
    sin_t = np.concatenate([-np.sin(ang), np.sin(ang)], axis=-1).astype(np.float32)
    return jnp.asarray(cos_t), jnp.asarray(sin_t)


def _layer(x, c, w_ada, b_ada, norm1_w, w_in, forget_bias, q_norm_w, k_norm_w, ret_norm_w, w_out, norm2_w,
           w_coarse, b_coarse, w_fine, b_fine, w1, w3, w2):
    bsz, seq, d = x.shape
    n = bsz * seq
    d_fox = d // 2
    d_ret = d // 2
    n_heads = d_fox // HEAD_DIM

    mod = _ada_modulation(c, w_ada, b_ada)
    sh1, sc1, g1, sh2, sc2, g2 = [m.reshape(bsz, 1, d) for m in jnp.split(mod, 6, axis=-1)]

    f0 = 3 * d_fox
    w_fox = w_in[:, :f0].astype(BF16)
    w_ret = w_in[:, f0 + n_heads:].astype(BF16)
    w_f = jnp.zeros((d, LANES), BF16).at[:, :n_heads].set(w_in[:, f0:f0 + n_heads].astype(BF16))
    fb = jnp.zeros((1, LANES), F32).at[0, :n_heads].set(forget_bias)

    cos_t, sin_t = _rotation_tables(seq)

    x2d = x.reshape(n, d)
    z, log_f = _input_projection(x2d, seq, norm1_w.reshape(1, d), sc1, sh1, w_fox, w_ret, w_f, cos_t, sin_t,
                                 q_norm_w.reshape(1, HEAD_DIM), k_norm_w.reshape(1, HEAD_DIM), fb)

    lf = log_f[:, :n_heads].reshape(bsz, seq, n_heads).transpose(0, 2, 1).reshape(bsz * n_heads, seq)
    cum = _cumsum_rows(lf)

    qk_bound = 1.02 * LOG2E * HEAD_DIM ** 0.5 * jnp.max(jnp.abs(q_norm_w)) * jnp.max(jnp.abs(k_norm_w))
    o_a = _fox_attention(z, cum, qk_bound, bsz, seq, n_heads)
    log_g = jnp.log(1.0 - 2.0 ** (-5.0 - jnp.arange(n_heads, dtype=F32)))
    o_b = _retention(z, log_g, ret_norm_w.reshape(1, d_ret), bsz, seq, n_heads, 3 * d_fox)

    w_router = jnp.zeros((d, LANES), F32)
    w_router = w_router.at[:, :N_GROUPS].set(w_coarse)
    w_router = w_router.at[:, N_GROUPS:N_GROUPS + N_EXPERTS].set(
        w_fine.transpose(1, 0, 2).reshape(d, N_EXPERTS))
    b_router = jnp.zeros((1, LANES), F32)
    b_router = b_router.at[0, :N_GROUPS].set(b_coarse)
    b_router = b_router.at[0, N_GROUPS:N_GROUPS + N_EXPERTS].set(b_fine.reshape(N_EXPERTS))

    wr_hi = w_router.astype(BF16)
    wr_lo = (w_router - wr_hi.astype(F32)).astype(BF16)
    x1, h_packed, logits = _output_projection(o_a, o_b, w_out.astype(BF16), x2d, seq, g1,
                                              norm2_w.reshape(1, d), sc2, sh2,
                                              jnp.concatenate([wr_hi, wr_lo], axis=1), b_router)

    blk = 256
    nk = n * TOP_K
    n_blocks = nk // blk + N_EXPERTS
    gates, ids, plan = _route(logits, blk, n_blocks)
    pstart = plan[0, :N_EXPERTS]
    fill_rows = plan[1, :2 * N_EXPERTS]
    counts = plan[2, :N_EXPERTS]
    eid = ids[0:TOP_K]
    hit = eid[None] == jnp.arange(N_EXPERTS, dtype=jnp.int32)[:, None, None]
    dest = (jnp.sum(jnp.where(hit, pstart[:, None, None], 0), axis=0) + ids[TOP_K:2 * TOP_K]).T.reshape(nk)

    xs = _dispatch(h_packed, dest, fill_rows, n_blocks * blk, blk)
    yb = _expert_blocks(xs, counts, pstart, fill_rows, w1, w3, w2, blk)
    out = _combine(x1, seq, g2, gates, dest, yb)
    return out.reshape(bsz, seq, d)


def kernel(x, c, w_ada, b_ada, norm1_w, w_in, forget_bias, q_norm_w, k_norm_w, ret_norm_w, w_out, norm2_w,
           w_coarse, b_coarse, w_fine, b_fine, w1, w3, w2):
    c_in = c
    for l in range(w_ada.shape[0]):
        x = _layer(x, c_in, w_ada[l], b_ada[l], norm1_w[l], w_in[l], forget_bias[l], q_norm_w[l],
                   k_norm_w[l], ret_norm_w[l], w_out[l], norm2_w[l], w_coarse[l], b_coarse[l],
                   w_fine[l], b_fine[l], w1[l], w3[l], w2[l])
    return x
```

```python
import functools

import jax
import jax.numpy as jnp
import numpy as np
from jax import lax
from jax.experimental import pallas as pl
from jax.experimental.pallas import tpu as pltpu

HEAD_DIM = 128
N_GROUPS = 4
EXPERTS_PER_GROUP = 8
N_EXPERTS = N_GROUPS * EXPERTS_PER_GROUP
TOP_K = 2
ROPE_BASE = 10000.0
EPS = 1e-6

LANES = 128
VMEM_LIMIT = 56 * 1024 * 1024
NEG_BIG = -1e30
LOG2E = 1.4426950408889634
UNDERFLOW_LOG2 = 160.0

ADA_COLS = 1024
INPROJ_ROWS = 1024
INPROJ_COLS = 1024
ATT_BLOCK = 512
ATT_BLOCKS_PER_STEP = 4
RET_CHUNK = 256
OUTPROJ_ROWS = 512
ROUTE_ROWS = 512
MOE_TOKENS_PER_STEP = 512
EXPERT_BLOCK = 256
EXPERT_AHEAD = 3

F32 = jnp.float32
BF16 = jnp.bfloat16
U32 = jnp.uint32


def _params(*sem):
    return pltpu.CompilerParams(dimension_semantics=sem, vmem_limit_bytes=VMEM_LIMIT)


def _silu(v):
    return v * (1.0 / (1.0 + jnp.exp(-v)))


def _pack_halves(y):
    w = y.shape[1] // 2
    lo = pltpu.bitcast(y[:, :w].astype(BF16).astype(F32), U32)
    hi = pltpu.bitcast(y[:, w:].astype(BF16).astype(F32), U32)
    return (hi & jnp.uint32(0xFFFF0000)) | (lo >> 16)


def _rows_to_tiles(p):
    return pltpu.einshape("m(ck)->mck", p, c=8, k=LANES)


def _tiles_to_rows(t):
    return pltpu.einshape("mck->m(ck)", t)


def _unpack_halves(p):
    lo = pltpu.bitcast(p << 16, F32)
    hi = pltpu.bitcast(p & jnp.uint32(0xFFFF0000), F32)
    return lo, hi


def _ada_kernel(ct_ref, w_ref, b_ref, o_ref):
    w = w_ref[...]
    rows = []
    for b in range(o_ref.shape[0]):
        if b < 2:
            cb = _silu(ct_ref[:, b:b + 1])
            rows.append(jnp.sum(cb * w, axis=0, keepdims=True) + b_ref[...])
        else:
            rows.append(jnp.zeros_like(b_ref[...]))
    o_ref[...] = jnp.concatenate(rows, axis=0)


def _ada_modulation(c, w_ada, b_ada):
    bsz, d = c.shape
    n = w_ada.shape[1]
    tn = ADA_COLS
    ct = jnp.zeros((d, LANES), F32).at[:, :bsz].set(c.T)
    out = pl.pallas_call(
        _ada_kernel,
        grid=(n // tn,),
        in_specs=[pl.BlockSpec((d, LANES), lambda j: (0, 0)),
                  pl.BlockSpec((d, tn), lambda j: (0, j)),
                  pl.BlockSpec((1, tn), lambda j: (0, j))],
        out_specs=pl.BlockSpec((8, tn), lambda j: (0, j)),
        out_shape=jax.ShapeDtypeStruct((8, n), F32),
        compiler_params=_params("arbitrary"),
    )(ct, w_ada, b_ada.reshape(1, n))
    return out[:bsz]


def _inproj_kernel(q_t, r_t, x_ref, nw_ref, sc_ref, sh_ref, wa_ref, wb_ref, wf_ref, cos_ref, sin_ref,
                   qw_ref, kw_ref, fb_ref, z_ref, f_ref, h_ref):
    j = pl.program_id(1)
    r0 = 3 * q_t

    @pl.when(j == 0)
    def _():
        x = x_ref[...]
        ms = jnp.mean(x * x, axis=-1, keepdims=True)
        y = x * lax.rsqrt(ms + EPS) * nw_ref[...]
        h = (y * (1.0 + sc_ref[0]) + sh_ref[0]).astype(BF16)
        h_ref[...] = h
        t = jnp.dot(h, wf_ref[...], preferred_element_type=F32) + fb_ref[...]
        f_ref[...] = jnp.minimum(t, 0.0) - jnp.log(1.0 + jnp.exp(-jnp.abs(t)))

    def heads_of(acc):
        return [acc[:, hh * HEAD_DIM:(hh + 1) * HEAD_DIM] for hh in range(acc.shape[1] // HEAD_DIM)]

    def head_norm(acc, w_row):
        outs = []
        for a in heads_of(acc):
            ms = jnp.mean(a * a, axis=-1, keepdims=True)
            outs.append(a * lax.rsqrt(ms + EPS) * w_row)
        return jnp.concatenate(outs, axis=-1).astype(BF16)

    def rotate(acc, scale):
        cs = cos_ref[...] * scale
        sn = sin_ref[...] * scale
        outs = [a * cs + pltpu.roll(a, HEAD_DIM // 2, 1) * sn for a in heads_of(acc)]
        return jnp.concatenate(outs, axis=-1).astype(BF16)

    def fox():
        return jnp.dot(h_ref[...], wa_ref[...], preferred_element_type=F32)

    def ret():
        return jnp.dot(h_ref[...], wb_ref[...], preferred_element_type=F32)

    @pl.when(j < q_t)
    def _():
        z_ref[...] = head_norm(fox(), qw_ref[...] * (LOG2E * HEAD_DIM ** -0.5))

    @pl.when((j >= q_t) & (j < 2 * q_t))
    def _():
        z_ref[...] = head_norm(fox(), kw_ref[...])

    @pl.when((j >= 2 * q_t) & (j < r0))
    def _():
        z_ref[...] = fox().astype(BF16)

    @pl.when((j >= r0) & (j < r0 + r_t))
    def _():
        z_ref[...] = rotate(ret(), 1.0)

    @pl.when((j >= r0 + r_t) & (j < r0 + 2 * r_t))
    def _():
        z_ref[...] = rotate(ret(), HEAD_DIM ** -0.5)

    @pl.when(j >= r0 + 2 * r_t)
    def _():
        z_ref[...] = ret().astype(BF16)


def _input_projection(x2d, seq, norm_w, sc1, sh1, w_fox, w_ret, w_f, cos_t, sin_t, qw, kw, fb):
    n, d = x2d.shape
    tm, tn = min(INPROJ_ROWS, seq), INPROJ_COLS
    fox_tiles = w_fox.shape[1] // tn
    ret_tiles = w_ret.shape[1] // tn
    tiles_per_seq = seq // tm
    kern = functools.partial(_inproj_kernel, fox_tiles // 3, ret_tiles // 4)
    bsel = lambda i, j: (i // tiles_per_seq, 0, 0)
    const = lambda i, j: (0, 0)
    return pl.pallas_call(
        kern,
        grid=(n // tm, fox_tiles + ret_tiles),
        in_specs=[pl.BlockSpec((tm, d), lambda i, j: (i, 0)),
                  pl.BlockSpec((1, d), const),
                  pl.BlockSpec((1, 1, d), bsel),
                  pl.BlockSpec((1, 1, d), bsel),
                  pl.BlockSpec((d, tn), lambda i, j: (0, jnp.minimum(j, fox_tiles - 1))),
                  pl.BlockSpec((d, tn), lambda i, j: (0, jnp.maximum(j - fox_tiles, 0))),
                  pl.BlockSpec((d, LANES), const),
                  pl.BlockSpec((tm, HEAD_DIM), lambda i, j: (i % tiles_per_seq, 0)),
                  pl.BlockSpec((tm, HEAD_DIM), lambda i, j: (i % tiles_per_seq, 0)),
                  pl.BlockSpec((1, HEAD_DIM), const),
                  pl.BlockSpec((1, HEAD_DIM), const),
                  pl.BlockSpec((1, LANES), const)],
        out_specs=[pl.BlockSpec((tm, tn), lambda i, j: (i, j)),
                   pl.BlockSpec((tm, LANES), lambda i, j: (i, 0))],
        out_shape=[jax.ShapeDtypeStruct((n, w_fox.shape[1] + w_ret.shape[1]), BF16),
                   jax.ShapeDtypeStruct((n, LANES), F32)],
        scratch_shapes=[pltpu.VMEM((tm, d), BF16)],
        compiler_params=_params("arbitrary", "arbitrary"),
    )(x2d, norm_w, sc1, sh1, w_fox, w_ret, w_f, cos_t, sin_t, qw, kw, fb)


def _cumsum_kernel(x_ref, o_ref):
    x = x_ref[0]
    r = x.shape[0]
    a = lax.broadcasted_iota(jnp.int32, (LANES, LANES), 0)
    b = lax.broadcasted_iota(jnp.int32, (LANES, LANES), 1)
    upper = (a <= b).astype(F32)
    within = jnp.dot(x, upper, precision=lax.Precision.HIGHEST, preferred_element_type=F32)
    tot = jnp.broadcast_to(within[:, LANES - 1:LANES], (r, LANES))
    ra = lax.broadcasted_iota(jnp.int32, (r, r), 0)
    rb = lax.broadcasted_iota(jnp.int32, (r, r), 1)
    strict = (rb < ra).astype(F32)
    before = jnp.dot(strict, tot, precision=lax.Precision.HIGHEST, preferred_element_type=F32)
    o_ref[0] = within + before


def _cumsum_rows(x):
    g, s = x.shape
    r = s // LANES
    out = pl.pallas_call(
        _cumsum_kernel,
        grid=(g,),
        in_specs=[pl.BlockSpec((1, r, LANES), lambda i: (i, 0, 0))],
        out_specs=pl.BlockSpec((1, r, LANES), lambda i: (i, 0, 0)),
        out_shape=jax.ShapeDtypeStruct((g, r, LANES), F32),
        compiler_params=_params("arbitrary"),
    )(x.reshape(g, r, LANES))
    return out.reshape(g, 1, s)


def _fox_kernel(tq, n_sub, first_ref, q_ref, k_ref, v_ref, cum_ref, o_ref, s_refs, m_ref, l_ref, acc_ref):
    group_id = pl.program_id(2)
    n_groups = pl.num_programs(2)
    head = pl.program_id(0) * pl.num_programs(1) + pl.program_id(1)
    n_slabs = tq // LANES

    m_ref[...] = jnp.full(m_ref.shape, NEG_BIG, F32)
    l_ref[...] = jnp.zeros(l_ref.shape, F32)
    acc_ref[...] = jnp.zeros(acc_ref.shape, F32)

    class Sub:
        def __init__(self, idx):
            self.rows = slice(idx * tq, (idx + 1) * tq)
            self.qi = n_sub * group_id + idx
            self.sa, self.sb = s_refs[2 * idx], s_refs[2 * idx + 1]
            q_start = pl.multiple_of(self.qi * tq, tq)
            self.c0 = cum_ref[0, :, pl.ds(q_start, LANES)][:, 0:1]
            self.first = first_ref[(head * n_groups + group_id) * n_sub + idx]
            self.n_off = self.qi - self.first

    def scores(sub, kb, s_ref):
        start = pl.multiple_of(kb * tq, tq)
        k = k_ref[pl.ds(start, tq), :]
        bias = (sub.c0 - cum_ref[0, :, pl.ds(start, tq)]) * LOG2E
        s_ref[...] = lax.dot_general(q_ref[sub.rows, :], k, (((1,), (1,)), ((), ())),
                                     preferred_element_type=F32) + bias

    def softmax_pv(sub, kb, s_ref, masked):
        rs = sub.rows
        start = pl.multiple_of(kb * tq, tq)
        v = v_ref[pl.ds(start, tq), :]
        slabs = []
        for j in range(n_slabs):
            t = s_ref[:, j * LANES:(j + 1) * LANES]
            if masked:
                row = lax.broadcasted_iota(jnp.int32, t.shape, 0)
                col = lax.broadcasted_iota(jnp.int32, t.shape, 1) + j * LANES
                t = jnp.where(col <= row, t, NEG_BIG)
            slabs.append(t)
        mx = slabs[0]
        for t in slabs[1:]:
            mx = jnp.maximum(mx, t)
        m_prev = m_ref[rs, :]
        m_new = jnp.maximum(m_prev, jnp.max(mx, axis=-1, keepdims=True))
        alpha = jnp.exp2(m_prev - m_new)
        probs = [jnp.exp2(t - m_new) for t in slabs]
        psum = probs[0]
        for t in probs[1:]:
            psum = psum + t
        l_ref[rs, :] = alpha * l_ref[rs, :] + psum
        p = jnp.concatenate([t.astype(BF16) for t in probs], axis=-1)
        acc_ref[rs, :] = alpha * acc_ref[rs, :] + jnp.dot(p, v, preferred_element_type=F32)
        m_ref[rs, :] = m_new

    def sweep(sub, then):
        def pair(kb):
            scores(sub, kb + 1, sub.sb)
            softmax_pv(sub, kb, sub.sa, False)
            scores(sub, kb + 2, sub.sa)
            softmax_pv(sub, kb + 1, sub.sb, False)

        def body4(i, carry):
            pair(sub.first + 4 * i)
            pair(sub.first + 4 * i + 2)
            return carry

        def body2(i, carry):
            pair(sub.first + 2 * i)
            return carry

        n4 = sub.n_off // 4
        lax.fori_loop(0, n4, body4, 0)
        lax.fori_loop(2 * n4, sub.n_off // 2, body2, 0)

        @pl.when(sub.n_off % 2 == 0)
        def _():
            then()
            softmax_pv(sub, sub.qi, sub.sa, True)

        @pl.when(sub.n_off % 2 == 1)
        def _():
            scores(sub, sub.qi, sub.sb)
            softmax_pv(sub, sub.qi - 1, sub.sa, False)
            then()
            softmax_pv(sub, sub.qi, sub.sb, True)

    subs = [Sub(idx) for idx in range(n_sub)]
    scores(subs[0], subs[0].first, subs[0].sa)
    for sub, nxt in zip(subs, subs[1:] + [None]):
        sweep(sub, (lambda: None) if nxt is None else functools.partial(scores, nxt, nxt.first, nxt.sa))

    o_ref[...] = (acc_ref[...] / jnp.sum(l_ref[...], axis=-1, keepdims=True)).astype(BF16)


def _first_live_block(cum, tq, qk_bound):
    c0 = cum[:, 0, ::tq]
    cend = cum[:, 0, tq - 1::tq]
    gap = (c0[:, :, None] - cend[:, None, :]) * LOG2E + 2.0 * qk_bound
    nq = c0.shape[1]
    earlier = jnp.arange(nq)[None, :] < jnp.arange(nq)[:, None]
    return jnp.sum((gap < -UNDERFLOW_LOG2) & earlier[None], axis=-1).astype(jnp.int32).reshape(-1)


def _fox_attention(z, cum, qk_bound, bsz, seq, n_heads):
    n_sub = ATT_BLOCKS_PER_STEP
    tq = min(ATT_BLOCK, seq // n_sub)
    nq = seq // tq
    n_groups = nq // n_sub
    rows = n_sub * tq
    kern = functools.partial(_fox_kernel, tq, n_sub)
    grid_spec = pltpu.PrefetchScalarGridSpec(
        num_scalar_prefetch=1,
        grid=(bsz, n_heads, n_groups),
        in_specs=[pl.BlockSpec((rows, HEAD_DIM), lambda b, h, i, f: (b * n_groups + i, h)),
                  pl.BlockSpec((seq, HEAD_DIM), lambda b, h, i, f: (b, n_heads + h)),
                  pl.BlockSpec((seq, HEAD_DIM), lambda b, h, i, f: (b, 2 * n_heads + h)),
                  pl.BlockSpec((1, 1, seq), lambda b, h, i, f: (b * n_heads + h, 0, 0))],
        out_specs=pl.BlockSpec((rows, HEAD_DIM), lambda b, h, i, f: (b * n_groups + i, h)),
        scratch_shapes=[[pltpu.VMEM((tq, tq), F32)] * (2 * n_sub),
                        pltpu.VMEM((rows, LANES), F32), pltpu.VMEM((rows, LANES), F32),
                        pltpu.VMEM((rows, HEAD_DIM), F32)],
    )
    return pl.pallas_call(
        kern,
        grid_spec=grid_spec,
        out_shape=jax.ShapeDtypeStruct((bsz * seq, n_heads * HEAD_DIM), BF16),
        compiler_params=_params("arbitrary", "arbitrary", "arbitrary"),
    )(_first_live_block(cum, tq, qk_bound), z, z, z, cum)


def _ret_kernel(chunk, n_heads, lg_ref, q_ref, k_ref, v_ref, g_ref, nw_ref, o_ref, state_ref, decay_ref,
                qdec_ref, kdec_ref):
    first = (pl.program_id(0) == 0) & (pl.program_id(1) == 0)

    @pl.when(first)
    def _():
        i = lax.broadcasted_iota(jnp.int32, (chunk, chunk), 0)
        jj = lax.broadcasted_iota(jnp.int32, (chunk, chunk), 1)
        diff = (i - jj).astype(F32)
        pos = lax.broadcasted_iota(jnp.int32, (chunk, HEAD_DIM), 0).astype(F32)
        for h in range(n_heads):
            decay_ref[h] = jnp.where(diff >= 0, jnp.exp(lg_ref[h] * jnp.maximum(diff, 0.0)), 0.0)
            qdec_ref[h] = jnp.exp(lg_ref[h] * (pos + 1.0))
            kdec_ref[h] = jnp.exp(lg_ref[h] * (chunk - 1.0 - pos))

    @pl.when(pl.program_id(1) == 0)
    def _():
        state_ref[...] = jnp.zeros(state_ref.shape, F32)

    for h in range(n_heads):
        log_g = lg_ref[h]
        cols = slice(h * HEAD_DIM, (h + 1) * HEAD_DIM)
        q = q_ref[:, cols]
        k = k_ref[:, cols]
        v = v_ref[:, cols]
        scores = lax.dot_general(q, k, (((1,), (1,)), ((), ())), preferred_element_type=F32)
        scores = scores * decay_ref[h]
        intra = jnp.dot(scores.astype(BF16), v, preferred_element_type=F32)
        state = state_ref[h]
        inter = jnp.dot(q, state.astype(BF16), preferred_element_type=F32) * qdec_ref[h]
        kd = (k.astype(F32) * kdec_ref[h]).astype(BF16)
        kv = lax.dot_general(kd, v, (((0,), (0,)), ((), ())), preferred_element_type=F32)
        state_ref[h] = state * jnp.exp(jnp.full((1, HEAD_DIM), chunk, F32) * log_g) + kv
        o = intra + inter
        ms = jnp.mean(o * o, axis=-1, keepdims=True)
        o = o * lax.rsqrt(ms + EPS) * nw_ref[:, cols]
        o_ref[:, cols] = (o * _silu(g_ref[:, cols].astype(F32))).astype(BF16)


def _retention(z, log_g, norm_w, bsz, seq, n_heads, col0):
    chunk = min(RET_CHUNK, seq)
    nt = seq // chunk
    width = n_heads * HEAD_DIM
    c0 = col0 // width
    kern = functools.partial(_ret_kernel, chunk, n_heads)

    def sec(s):
        return pl.BlockSpec((chunk, width), lambda b, t, lg: (b * nt + t, c0 + s))

    grid_spec = pltpu.PrefetchScalarGridSpec(
        num_scalar_prefetch=1,
        grid=(bsz, nt),
        in_specs=[sec(0), sec(1), sec(2), sec(3), pl.BlockSpec((1, width), lambda b, t, lg: (0, 0))],
        out_specs=pl.BlockSpec((chunk, width), lambda b, t, lg: (b * nt + t, 0)),
        scratch_shapes=[pltpu.VMEM((n_heads, HEAD_DIM, HEAD_DIM), F32),
                        pltpu.VMEM((n_heads, chunk, chunk), F32),
                        pltpu.VMEM((n_heads, chunk, HEAD_DIM), F32),
                        pltpu.VMEM((n_heads, chunk, HEAD_DIM), F32)],
    )
    return pl.pallas_call(
        kern,
        grid_spec=grid_spec,
        out_shape=jax.ShapeDtypeStruct((bsz * seq, width), BF16),
        compiler_params=_params("arbitrary", "arbitrary"),
    )(log_g, z, z, z, z, norm_w)


def _outproj_kernel(oa_ref, ob_ref, wa_ref, wb_ref, x_ref, g1_ref, nw_ref, sc_ref, sh_ref, wr_ref, br_ref,
                    x1_ref, hp_ref, lg_ref):
    mix = jnp.dot(oa_ref[...], wa_ref[...], preferred_element_type=F32)
    mix = mix + jnp.dot(ob_ref[...], wb_ref[...], preferred_element_type=F32)
    x1 = x_ref[...] + g1_ref[0] * mix
    x1_ref[...] = x1
    ms = jnp.mean(x1 * x1, axis=-1, keepdims=True)
    h2 = x1 * lax.rsqrt(ms + EPS) * nw_ref[...] * (1.0 + sc_ref[0]) + sh_ref[0]
    hp_ref[...] = _rows_to_tiles(_pack_halves(h2))
    h_hi = h2.astype(BF16)
    h_lo = (h2 - h_hi.astype(F32)).astype(BF16)
    both = jnp.dot(h_hi, wr_ref[...], preferred_element_type=F32)
    cross = jnp.dot(h_lo, wr_ref[:, :LANES], preferred_element_type=F32)
    lg_ref[...] = both[:, :LANES] + both[:, LANES:] + cross + br_ref[...]


def _output_projection(o_a, o_b, w_out, x2d, seq, g1, norm_w, sc2, sh2, w_router, b_router):
    n, d = x2d.shape
    da = o_a.shape[1]
    tm = min(OUTPROJ_ROWS, seq)
    tiles_per_seq = seq // tm
    bsel = lambda i: (i // tiles_per_seq, 0, 0)
    return pl.pallas_call(
        _outproj_kernel,
        grid=(n // tm,),
        in_specs=[pl.BlockSpec((tm, da), lambda i: (i, 0)),
                  pl.BlockSpec((tm, da), lambda i: (i, 0)),
                  pl.BlockSpec((da, d), lambda i: (0, 0)),
                  pl.BlockSpec((da, d), lambda i: (1, 0)),
                  pl.BlockSpec((tm, d), lambda i: (i, 0)),
                  pl.BlockSpec((1, 1, d), bsel),
                  pl.BlockSpec((1, d), lambda i: (0, 0)),
                  pl.BlockSpec((1, 1, d), bsel),
                  pl.BlockSpec((1, 1, d), bsel),
                  pl.BlockSpec((d, 2 * LANES), lambda i: (0, 0)),
                  pl.BlockSpec((1, LANES), lambda i: (0, 0))],
        out_specs=[pl.BlockSpec((tm, d), lambda i: (i, 0)),
                   pl.BlockSpec((tm, d // 2 // LANES, LANES), lambda i: (i, 0, 0)),
                   pl.BlockSpec((tm, LANES), lambda i: (i, 0))],
        out_shape=[jax.ShapeDtypeStruct((n, d), F32),
                   jax.ShapeDtypeStruct((n, d // 2 // LANES, LANES), U32),
                   jax.ShapeDtypeStruct((n, LANES), F32)],
        compiler_params=_params("arbitrary"),
    )(o_a, o_b, w_out, w_out, x2d, g1, norm_w, sc2, sh2, w_router, b_router)


def _route_kernel(blk, n_blocks, lg_ref, gate_ref, ids_ref, plan_ref, run_ref):
    i = pl.program_id(0)

    @pl.when(i == 0)
    def _():
        run_ref[...] = jnp.zeros(run_ref.shape, F32)

    lg = lg_ref[...]
    tt = lg.shape[0]
    lane = lax.broadcasted_iota(jnp.int32, lg.shape, 1).astype(F32)
    big = 1e6

    def rmax(v):
        return jnp.max(v, axis=-1, keepdims=True)

    def rmin(v):
        return jnp.min(v, axis=-1, keepdims=True)

    def rsum(v):
        return jnp.sum(v, axis=-1, keepdims=True)

    cmask = lane < N_GROUPS
    cm = jnp.where(cmask, lg, NEG_BIG)
    ce = jnp.where(cmask, jnp.exp(cm - rmax(cm)), 0.0)
    pgrp = ce / rsum(ce)
    p_g = rmax(pgrp)
    g_sel = rmin(jnp.where(cmask & (pgrp == p_g), lane, big))

    lo = N_GROUPS + EXPERTS_PER_GROUP * g_sel
    fmask = (lane >= lo) & (lane < lo + EXPERTS_PER_GROUP)
    fm = jnp.where(fmask, lg, NEG_BIG)
    fe = jnp.where(fmask, jnp.exp(fm - rmax(fm)), 0.0)
    fp = fe / rsum(fe)
    fp = jnp.where(fmask, fp, -1.0)
    p1 = rmax(fp)
    i1 = rmin(jnp.where(fp == p1, lane, big))
    fp2 = jnp.where(lane == i1, -1.0, fp)
    p2 = rmax(fp2)
    i2 = rmin(jnp.where(fp2 == p2, lane, big))
    denom = p1 + p2
    w1 = p_g * p1 / denom
    w2 = p_g * p2 / denom
    e1 = i1 - N_GROUPS
    e2 = i2 - N_GROUPS

    gate_ref[...] = jnp.where(lane == 0, w1, jnp.where(lane == 1, w2, 0.0))

    oh1 = (lane == e1).astype(F32)
    oh2 = (lane == e2).astype(F32)
    both = oh1 + oh2
    ra = lax.broadcasted_iota(jnp.int32, (tt, tt), 0)
    rb = lax.broadcasted_iota(jnp.int32, (tt, tt), 1)
    strict = (rb < ra).astype(BF16)
    prefix = jnp.dot(strict, both.astype(BF16), preferred_element_type=F32) + run_ref[...]
    r1 = rsum(prefix * oh1)
    r2 = rsum(prefix * oh2)
    run_ref[...] = run_ref[...] + jnp.sum(both, axis=0, keepdims=True)

    packed = jnp.where(lane == 0, e1, jnp.where(lane == 1, e2, jnp.where(lane == 2, r1,
                                                                        jnp.where(lane == 3, r2, 0.0))))
    ids_ref[...] = jnp.transpose(packed)[:8, :].astype(jnp.int32)

    @pl.when(i == pl.num_programs(0) - 1)
    def _():
        cnt = jnp.broadcast_to(run_ref[...], (8, LANES))
        lane8 = lax.broadcasted_iota(jnp.int32, (8, LANES), 1)
        padded = jnp.floor((cnt + (blk - 1.0)) * (1.0 / blk)) * blk
        pend = padded
        for sh in (1, 2, 4, 8, 16, 32, 64):
            pend = pend + jnp.where(lane8 >= sh, pltpu.roll(pend, sh, 1), 0.0)
        pstart = pend - padded
        total = jnp.max(pend, axis=-1, keepdims=True)
        tail = total + (lane8 - N_EXPERTS).astype(F32) * blk
        fill = jnp.where(lane8 < N_EXPERTS, jnp.where(padded > 0, pend - blk, -1.0),
                         jnp.where((lane8 < 2 * N_EXPERTS) & (tail < n_blocks * blk), tail, -1.0))
        row8 = lax.broadcasted_iota(jnp.int32, (8, LANES), 0)
        plan_ref[...] = jnp.where(row8 == 0, pstart, jnp.where(row8 == 1, fill,
                                                               jnp.where(row8 == 2, cnt, 0.0))).astype(jnp.int32)


def _route(logits, blk, n_blocks):
    n = logits.shape[0]
    tt = min(ROUTE_ROWS, n)
    blkspec = lambda: pl.BlockSpec((tt, LANES), lambda i: (i, 0))
    return pl.pallas_call(
        functools.partial(_route_kernel, blk, n_blocks),
        grid=(n // tt,),
        in_specs=[blkspec()],
        out_specs=[blkspec(),
                   pl.BlockSpec((8, tt), lambda i: (0, i)),
                   pl.BlockSpec((8, LANES), lambda i: (0, 0))],
        out_shape=[jax.ShapeDtypeStruct((n, LANES), F32),
                   jax.ShapeDtypeStruct((8, n), jnp.int32),
                   jax.ShapeDtypeStruct((8, LANES), jnp.int32)],
        scratch_shapes=[pltpu.VMEM((1, LANES), F32)],
        compiler_params=_params("arbitrary"),
    )(logits)


def _dispatch_kernel(tt, blk, n_fill, dest_ref, fill_ref, h_ref, xs_ref, zero_ref, sem, zsem):
    i = pl.program_id(0)
    base = i * (tt * TOP_K)

    @pl.when(i == 0)
    def _():
        zero_ref[...] = jnp.zeros(zero_ref.shape, U32)

        def zcopy(z):
            row = pl.multiple_of(jnp.maximum(fill_ref[z], 0), blk)
            return pltpu.make_async_copy(zero_ref, xs_ref.at[pl.ds(row, blk)], zsem)

        def zissue(z, carry):
            @pl.when(fill_ref[z] >= 0)
            def _():
                zcopy(z).start()
            return carry

        def zdrain(z, carry):
            @pl.when(fill_ref[z] >= 0)
            def _():
                zcopy(z).wait()
            return carry

        lax.fori_loop(0, n_fill, zissue, 0)
        lax.fori_loop(0, n_fill, zdrain, 0)

    def copy(r, kk):
        d = dest_ref[base + r * TOP_K + kk]
        return pltpu.make_async_copy(h_ref.at[r], xs_ref.at[d], sem)

    def issue(r, carry):
        for kk in range(TOP_K):
            copy(r, kk).start(priority=kk % 2)
        return carry

    lax.fori_loop(0, tt, issue, 0, unroll=8)
    for _ in range(TOP_K):
        pltpu.make_async_copy(h_ref, xs_ref.at[pl.ds(0, tt)], sem).wait()


def _dispatch(h_packed, dest_flat, fill_rows, n_slots, blk):
    n = h_packed.shape[0]
    tile = h_packed.shape[1:]
    tt = min(MOE_TOKENS_PER_STEP, n)
    n_fill = fill_rows.shape[0]
    grid_spec = pltpu.PrefetchScalarGridSpec(
        num_scalar_prefetch=2,
        grid=(n // tt,),
        in_specs=[pl.BlockSpec((tt,) + tile, lambda i, d, f: (i, 0, 0))],
        out_specs=pl.BlockSpec(memory_space=pl.ANY),
        scratch_shapes=[pltpu.VMEM((blk,) + tile, U32), pltpu.SemaphoreType.DMA(()), pltpu.SemaphoreType.DMA(())],
    )
    return pl.pallas_call(
        functools.partial(_dispatch_kernel, tt, blk, n_fill),
        grid_spec=grid_spec,
        out_shape=jax.ShapeDtypeStruct((n_slots,) + tile, U32),
        compiler_params=_params("arbitrary"),
    )(dest_flat, fill_rows, h_packed)


def _expert_kernel(blk, ahead, cnt_ref, pstart_ref, fill_ref, xs_ref, w1_ref, w3_ref, w2_ref, y_ref,
                   w1f, w3f, w2f, w1b, w3b, w2b, xbuf, ybuf, done_ref, w_sem, in_sem, out_sem):
    e = pl.program_id(0)
    n_exp = pl.num_programs(0)
    wslot = e % 2
    n_blk = (cnt_ref[e] + (blk - 1)) // blk
    base = pstart_ref[e]
    n_x = xbuf.shape[0]

    def weight_copies(ex, slot):
        return [pltpu.make_async_copy(src.at[ex], dst.at[slot], w_sem.at[slot])
                for src, dst in ((w1_ref, w1f), (w3_ref, w3f), (w2_ref, w2f))]

    def rows(b):
        return pl.ds(pl.multiple_of(base + b * blk, blk), blk)

    def in_copy(b, slot):
        return pltpu.make_async_copy(xs_ref.at[rows(b)], xbuf.at[slot], in_sem.at[slot])

    def out_copy(b, slot):
        return pltpu.make_async_copy(ybuf.at[slot], y_ref.at[rows(b)], out_sem.at[slot])

    @pl.when(e == 0)
    def _():
        for c in weight_copies(0, 0):
            c.start()

    for p in range(ahead):
        @pl.when(p < n_blk)
        def _():
            in_copy(p, p).start()

    @pl.when(e + 1 < n_exp)
    def _():
        for c in weight_copies(e + 1, 1 - wslot):
            c.start()

    for c in weight_copies(e, wslot):
        c.wait()
    w1b[...] = w1f[wslot].astype(BF16)
    w3b[...] = w3f[wslot].astype(BF16)
    w2b[...] = w2f[wslot].astype(BF16)

    @pl.when(e == 0)
    def _():
        done_ref[0] = 0

    done = done_ref[0]

    def out_wait(slot):
        pltpu.make_async_copy(ybuf.at[slot], y_ref.at[pl.ds(0, blk)], out_sem.at[slot]).wait()

    def body(b, carry):
        slot = (done + b) % 2

        @pl.when(b + ahead < n_blk)
        def _():
            in_copy(b + ahead, (b + ahead) % n_x).start()

        in_copy(b, b % n_x).wait()

        @pl.when(done + b >= 2)
        def _():
            out_wait(slot)

        lo, hi = _unpack_halves(_tiles_to_rows(xbuf[b % n_x]))
        lo = lo.astype(BF16)
        hi = hi.astype(BF16)
        half = lo.shape[1]
        a = jnp.dot(lo, w1b[:half, :], preferred_element_type=F32)
        a = a + jnp.dot(hi, w1b[half:, :], preferred_element_type=F32)
        g = jnp.dot(lo, w3b[:half, :], preferred_element_type=F32)
        g = g + jnp.dot(hi, w3b[half:, :], preferred_element_type=F32)
        mid = (_silu(a) * g).astype(BF16)
        ybuf[slot] = _rows_to_tiles(_pack_halves(jnp.dot(mid, w2b[...], preferred_element_type=F32)))
        out_copy(b, slot).start(priority=1)
        return carry

    lax.fori_loop(0, n_blk, body, 0)
    total = done + n_blk
    done_ref[0] = total

    @pl.when(e == n_exp - 1)
    def _():
        @pl.when(total >= 2)
        def _():
            out_wait(total % 2)

        @pl.when(total >= 1)
        def _():
            out_wait((total - 1) % 2)

        ybuf[0] = jnp.zeros(ybuf.shape[1:], U32)

        def zcopy(t):
            row = pl.multiple_of(jnp.maximum(fill_ref[N_EXPERTS + t], 0), blk)
            return pltpu.make_async_copy(ybuf.at[0], y_ref.at[pl.ds(row, blk)], out_sem.at[0])

        def zissue(t, carry):
            @pl.when(fill_ref[N_EXPERTS + t] >= 0)
            def _():
                zcopy(t).start()
            return carry

        def zdrain(t, carry):
            @pl.when(fill_ref[N_EXPERTS + t] >= 0)
            def _():
                zcopy(t).wait()
            return carry

        lax.fori_loop(0, N_EXPERTS, zissue, 0)
        lax.fori_loop(0, N_EXPERTS, zdrain, 0)


def _expert_blocks(xs, counts, pstart, fill_rows, w1, w3, w2, blk):
    n_slots = xs.shape[0]
    tile = xs.shape[1:]
    n_exp, d, de = w1.shape
    ahead = EXPERT_AHEAD
    hbm = pl.BlockSpec(memory_space=pl.ANY)
    grid_spec = pltpu.PrefetchScalarGridSpec(
        num_scalar_prefetch=3,
        grid=(n_exp,),
        in_specs=[hbm, hbm, hbm, hbm],
        out_specs=hbm,
        scratch_shapes=[pltpu.VMEM((2, d, de), F32), pltpu.VMEM((2, d, de), F32), pltpu.VMEM((2, de, d), F32),
                        pltpu.VMEM((d, de), BF16), pltpu.VMEM((d, de), BF16), pltpu.VMEM((de, d), BF16),
                        pltpu.VMEM((ahead + 1, blk) + tile, U32), pltpu.VMEM((2, blk) + tile, U32),
                        pltpu.SMEM((1,), jnp.int32),
                        pltpu.SemaphoreType.DMA((2,)), pltpu.SemaphoreType.DMA((ahead + 1,)),
                        pltpu.SemaphoreType.DMA((2,))],
    )
    return pl.pallas_call(
        functools.partial(_expert_kernel, blk, ahead),
        grid_spec=grid_spec,
        out_shape=jax.ShapeDtypeStruct((n_slots,) + tile, U32),
        compiler_params=_params("arbitrary"),
    )(counts, pstart, fill_rows, xs, w1, w3, w2)


def _combine_kernel(tt, n_tiles, dest_ref, x1_ref, g2_ref, gate_ref, yb_ref, o_ref, buf, sems):
    i = pl.program_id(0)

    def copy(tile, slot, r, kk):
        d = dest_ref[(tile * tt + r) * TOP_K + kk]
        return pltpu.make_async_copy(yb_ref.at[d], buf.at[slot, kk, r], sems.at[slot])

    def issue_tile(tile, slot):
        def body(r, carry):
            for kk in range(TOP_K):
                copy(tile, slot, r, kk).start(priority=kk % 2)
            return carry
        lax.fori_loop(0, tt, body, 0, unroll=8)

    def wait_tile(tile, slot):
        for kk in range(TOP_K):
            pltpu.make_async_copy(yb_ref.at[pl.ds(0, tt)], buf.at[slot, kk], sems.at[slot]).wait()

    slot = i % 2

    @pl.when(i == 0)
    def _():
        issue_tile(0, 0)

    @pl.when(i + 1 < n_tiles)
    def _():
        issue_tile(i + 1, 1 - slot)

    wait_tile(i, slot)

    gate = gate_ref[...]
    wa = gate[:, 0:1]
    wb = gate[:, 1:2]
    lo_a, hi_a = _unpack_halves(_tiles_to_rows(buf[slot, 0]))
    lo_b, hi_b = _unpack_halves(_tiles_to_rows(buf[slot, 1]))
    y = jnp.concatenate([wa * lo_a + wb * lo_b, wa * hi_a + wb * hi_b], axis=-1)
    o_ref[...] = x1_ref[...] + g2_ref[0] * y


def _combine(x1, seq, g2, gates, dest_flat, yb):
    n, d = x1.shape
    tile = yb.shape[1:]
    tt = min(MOE_TOKENS_PER_STEP, seq)
    n_tiles = n // tt
    tiles_per_seq = seq // tt
    grid_spec = pltpu.PrefetchScalarGridSpec(
        num_scalar_prefetch=1,
        grid=(n_tiles,),
        in_specs=[pl.BlockSpec((tt, d), lambda i, dr: (i, 0)),
                  pl.BlockSpec((1, 1, d), lambda i, dr: (i // tiles_per_seq, 0, 0)),
                  pl.BlockSpec((tt, LANES), lambda i, dr: (i, 0)),
                  pl.BlockSpec(memory_space=pl.ANY)],
        out_specs=pl.BlockSpec((tt, d), lambda i, dr: (i, 0)),
        scratch_shapes=[pltpu.VMEM((2, TOP_K, tt) + tile, U32), pltpu.SemaphoreType.DMA((2,))],
    )
    return pl.pallas_call(
        functools.partial(_combine_kernel, tt, n_tiles),
        grid_spec=grid_spec,
        out_shape=jax.ShapeDtypeStruct((n, d), F32),
        compiler_params=_params("arbitrary"),
    )(dest_flat, x1, g2, gates, yb)


def _rotation_tables(seq):
    half = HEAD_DIM // 2
    theta = ROPE_BASE ** (-np.arange(half, dtype=np.float64) / half)
    ang = np.arange(seq, dtype=np.float64)[:, None] * theta[None, :]
    cos_t = np.concatenate([np.cos(ang), np.cos(ang)], axis=-1).astype(np.float32)
    sin_t = np.concatenate([-np.sin(ang), np.sin(ang)], axis=-1).astype(np.float32)
    return jnp.asarray(cos_t), jnp.asarray(sin_t)


def _layer(x, c, w_ada, b_ada, norm1_w, w_in, forget_bias, q_norm_w, k_norm_w, ret_norm_w, w_out, norm2_w,
           w_coarse, b_coarse, w_fine, b_fine, w1, w3, w2):
    bsz, seq, d = x.shape
    n = bsz * seq
    d_fox = d // 2
    d_ret = d // 2
    n_heads = d_fox // HEAD_DIM

    mod = _ada_modulation(c, w_ada, b_ada)
    sh1, sc1, g1, sh2, sc2, g2 = [m.reshape(bsz, 1, d) for m in jnp.split(mod, 6, axis=-1)]

    f0 = 3 * d_fox
    w_fox = w_in[:, :f0].astype(BF16)
    w_ret = w_in[:, f0 + n_heads:].astype(BF16)
    w_f = jnp.zeros((d, LANES), BF16).at[:, :n_heads].set(w_in[:, f0:f0 + n_heads].astype(BF16))
    fb = jnp.zeros((1, LANES), F32).at[0, :n_heads].set(forget_bias)

    cos_t, sin_t = _rotation_tables(seq)

    x2d = x.reshape(n, d)
    z, log_f = _input_projection(x2d, seq, norm1_w.reshape(1, d), sc1, sh1, w_fox, w_ret, w_f, cos_t, sin_t,
                                 q_norm_w.reshape(1, HEAD_DIM), k_norm_w.reshape(1, HEAD_DIM), fb)

    lf = log_f[:, :n_heads].reshape(bsz, seq, n_heads).transpose(0, 2, 1).reshape(bsz * n_heads, seq)
    cum = _cumsum_rows(lf)

    qk_bound = 1.02 * LOG2E * HEAD_DIM ** 0.5 * jnp.max(jnp.abs(q_norm_w)) * jnp.max(jnp.abs(k_norm_w))
    o_a = _fox_attention(z, cum, qk_bound, bsz, seq, n_heads)
    log_g = jnp.log(1.0 - 2.0 ** (-5.0 - jnp.arange(n_heads, dtype=F32)))
    o_b = _retention(z, log_g, ret_norm_w.reshape(1, d_ret), bsz, seq, n_heads, 3 * d_fox)

    w_router = jnp.zeros((d, LANES), F32)
    w_router = w_router.at[:, :N_GROUPS].set(w_coarse)
    w_router = w_router.at[:, N_GROUPS:N_GROUPS + N_EXPERTS].set(
        w_fine.transpose(1, 0, 2).reshape(d, N_EXPERTS))
    b_router = jnp.zeros((1, LANES), F32)
    b_router = b_router.at[0, :N_GROUPS].set(b_coarse)
    b_router = b_router.at[0, N_GROUPS:N_GROUPS + N_EXPERTS].set(b_fine.reshape(N_EXPERTS))

    wr_hi = w_router.astype(BF16)
    wr_lo = (w_router - wr_hi.astype(F32)).astype(BF16)
    x1, h_packed, logits = _output_projection(o_a, o_b, w_out.astype(BF16), x2d, seq, g1,
                                              norm2_w.reshape(1, d), sc2, sh2,
                                              jnp.concatenate([wr_hi, wr_lo], axis=1), b_router)

    blk = EXPERT_BLOCK
    nk = n * TOP_K
    n_blocks = nk // blk + N_EXPERTS
    gates, ids, plan = _route(logits, blk, n_blocks)
    pstart = plan[0, :N_EXPERTS]
    fill_rows = plan[1, :2 * N_EXPERTS]
    counts = plan[2, :N_EXPERTS]
    eid = ids[0:TOP_K]
    hit = eid[None] == jnp.arange(N_EXPERTS, dtype=jnp.int32)[:, None, None]
    dest = (jnp.sum(jnp.where(hit, pstart[:, None, None], 0), axis=0) + ids[TOP_K:2 * TOP_K]).T.reshape(nk)

    xs = _dispatch(h_packed, dest, fill_rows, n_blocks * blk, blk)
    yb = _expert_blocks(xs, counts, pstart, fill_rows, w1, w3, w2, blk)
    out = _combine(x1, seq, g2, gates, dest, yb)
    return out.reshape(bsz, seq, d)


def kernel(x, c, w_ada, b_ada, norm1_w, w_in, forget_bias, q_norm_w, k_norm_w, ret_norm_w, w_out, norm2_w,
           w_coarse, b_coarse, w_fine, b_fine, w1, w3, w2):
    c_in = c
    for l in range(w_ada.shape[0]):
        x = _layer(x, c_in, w_ada[l], b_ada[l], norm1_w[l], w_in[l], forget_bias[l], q_norm_w[l],
                   k_norm_w[l], ret_norm_w[l], w_out[l], norm2_w[l], w_coarse[l], b_coarse[l],
                   w_fine[l], b_fine[l], w1[l], w3[l], w2[l])
    return x
```

```python
import functools

import jax
import jax.numpy as jnp
import numpy as np
from jax import lax
from jax.experimental import pallas as pl
from jax.experimental.pallas import tpu as pltpu

HEAD_DIM = 128
N_GROUPS = 4
EXPERTS_PER_GROUP = 8
N_EXPERTS = N_GROUPS * EXPERTS_PER_GROUP
TOP_K = 2
ROPE_BASE = 10000.0
EPS = 1e-6

LANES = 128
VMEM_LIMIT = 56 * 1024 * 1024
NEG_BIG = -1e30
LOG2E = 1.4426950408889634
UNDERFLOW_LOG2 = 160.0

ADA_COLS = 1024
INPROJ_ROWS = 1024
INPROJ_COLS = 1024
ATT_BLOCK = 512
ATT_BLOCKS_PER_STEP = 4
RET_CHUNK = 256
OUTPROJ_ROWS = 512
ROUTE_ROWS = 512
MOE_TOKENS_PER_STEP = 512
EXPERT_BLOCK = 256
EXPERT_AHEAD = 3

F32 = jnp.float32
BF16 = jnp.bfloat16
U32 = jnp.uint32


def _params(*sem):
    return pltpu.CompilerParams(dimension_semantics=sem, vmem_limit_bytes=VMEM_LIMIT)


def _silu(v):
    return v * (1.0 / (1.0 + jnp.exp(-v)))


def _pack_halves(y):
    w = y.shape[1] // 2
    lo = pltpu.bitcast(y[:, :w].astype(BF16).astype(F32), U32)
    hi = pltpu.bitcast(y[:, w:].astype(BF16).astype(F32), U32)
    return (hi & jnp.uint32(0xFFFF0000)) | (lo >> 16)


def _rows_to_tiles(p):
    return pltpu.einshape("m(ck)->mck", p, c=8, k=LANES)


def _tiles_to_rows(t):
    return pltpu.einshape("mck->m(ck)", t)


def _unpack_halves(p):
    lo = pltpu.bitcast(p << 16, F32)
    hi = pltpu.bitcast(p & jnp.uint32(0xFFFF0000), F32)
    return lo, hi


def _ada_kernel(ct_ref, w_ref, b_ref, o_ref):
    w = w_ref[...]
    rows = []
    for b in range(o_ref.shape[0]):
        if b < 2:
            cb = _silu(ct_ref[:, b:b + 1])
            rows.append(jnp.sum(cb * w, axis=0, keepdims=True) + b_ref[...])
        else:
            rows.append(jnp.zeros_like(b_ref[...]))
    o_ref[...] = jnp.concatenate(rows, axis=0)


def _ada_modulation(c, w_ada, b_ada):
    bsz, d = c.shape
    n = w_ada.shape[1]
    tn = ADA_COLS
    ct = jnp.zeros((d, LANES), F32).at[:, :bsz].set(c.T)
    out = pl.pallas_call(
        _ada_kernel,
        grid=(n // tn,),
        in_specs=[pl.BlockSpec((d, LANES), lambda j: (0, 0)),
                  pl.BlockSpec((d, tn), lambda j: (0, j)),
                  pl.BlockSpec((1, tn), lambda j: (0, j))],
        out_specs=pl.BlockSpec((8, tn), lambda j: (0, j)),
        out_shape=jax.ShapeDtypeStruct((8, n), F32),
        compiler_params=_params("arbitrary"),
    )(ct, w_ada, b_ada.reshape(1, n))
    return out[:bsz]


def _inproj_kernel(q_t, r_t, x_ref, nw_ref, sc_ref, sh_ref, wa_ref, wb_ref, wf_ref, cos_ref, sin_ref,
                   qw_ref, kw_ref, fb_ref, z_ref, f_ref, h_ref):
    j = pl.program_id(1)
    r0 = 3 * q_t

    @pl.when(j == 0)
    def _():
        x = x_ref[...]
        ms = jnp.mean(x * x, axis=-1, keepdims=True)
        y = x * lax.rsqrt(ms + EPS) * nw_ref[...]
        h = (y * (1.0 + sc_ref[0]) + sh_ref[0]).astype(BF16)
        h_ref[...] = h
        t = jnp.dot(h, wf_ref[...], preferred_element_type=F32) + fb_ref[...]
        f_ref[...] = jnp.minimum(t, 0.0) - jnp.log(1.0 + jnp.exp(-jnp.abs(t)))

    def heads_of(acc):
        return [acc[:, hh * HEAD_DIM:(hh + 1) * HEAD_DIM] for hh in range(acc.shape[1] // HEAD_DIM)]

    def head_norm(acc, w_row):
        outs = []
        for a in heads_of(acc):
            ms = jnp.mean(a * a, axis=-1, keepdims=True)
            outs.append(a * lax.rsqrt(ms + EPS) * w_row)
        return jnp.concatenate(outs, axis=-1).astype(BF16)

    def rotate(acc, scale):
        cs = cos_ref[...] * scale
        sn = sin_ref[...] * scale
        outs = [a * cs + pltpu.roll(a, HEAD_DIM // 2, 1) * sn for a in heads_of(acc)]
        return jnp.concatenate(outs, axis=-1).astype(BF16)

    def fox():
        return jnp.dot(h_ref[...], wa_ref[...], preferred_element_type=F32)

    def ret():
        return jnp.dot(h_ref[...], wb_ref[...], preferred_element_type=F32)

    @pl.when(j < q_t)
    def _():
        z_ref[...] = head_norm(fox(), qw_ref[...] * (LOG2E * HEAD_DIM ** -0.5))

    @pl.when((j >= q_t) & (j < 2 * q_t))
    def _():
        z_ref[...] = head_norm(fox(), kw_ref[...])

    @pl.when((j >= 2 * q_t) & (j < r0))
    def _():
        z_ref[...] = fox().astype(BF16)

    @pl.when((j >= r0) & (j < r0 + r_t))
    def _():
        z_ref[...] = rotate(ret(), 1.0)

    @pl.when((j >= r0 + r_t) & (j < r0 + 2 * r_t))
    def _():
        z_ref[...] = rotate(ret(), HEAD_DIM ** -0.5)

    @pl.when(j >= r0 + 2 * r_t)
    def _():
        z_ref[...] = ret().astype(BF16)


def _input_projection(x2d, seq, norm_w, sc1, sh1, w_fox, w_ret, w_f, cos_t, sin_t, qw, kw, fb):
    n, d = x2d.shape
    tm, tn = min(INPROJ_ROWS, seq), INPROJ_COLS
    fox_tiles = w_fox.shape[1] // tn
    ret_tiles = w_ret.shape[1] // tn
    tiles_per_seq = seq // tm
    kern = functools.partial(_inproj_kernel, fox_tiles // 3, ret_tiles // 4)
    bsel = lambda i, j: (i // tiles_per_seq, 0, 0)
    const = lambda i, j: (0, 0)
    return pl.pallas_call(
        kern,
        grid=(n // tm, fox_tiles + ret_tiles),
        in_specs=[pl.BlockSpec((tm, d), lambda i, j: (i, 0)),
                  pl.BlockSpec((1, d), const),
                  pl.BlockSpec((1, 1, d), bsel),
                  pl.BlockSpec((1, 1, d), bsel),
                  pl.BlockSpec((d, tn), lambda i, j: (0, jnp.minimum(j, fox_tiles - 1))),
                  pl.BlockSpec((d, tn), lambda i, j: (0, jnp.maximum(j - fox_tiles, 0))),
                  pl.BlockSpec((d, LANES), const),
                  pl.BlockSpec((tm, HEAD_DIM), lambda i, j: (i % tiles_per_seq, 0)),
                  pl.BlockSpec((tm, HEAD_DIM), lambda i, j: (i % tiles_per_seq, 0)),
                  pl.BlockSpec((1, HEAD_DIM), const),
                  pl.BlockSpec((1, HEAD_DIM), const),
                  pl.BlockSpec((1, LANES), const)],
        out_specs=[pl.BlockSpec((tm, tn), lambda i, j: (i, j)),
                   pl.BlockSpec((tm, LANES), lambda i, j: (i, 0))],
        out_shape=[jax.ShapeDtypeStruct((n, w_fox.shape[1] + w_ret.shape[1]), BF16),
                   jax.ShapeDtypeStruct((n, LANES), F32)],
        scratch_shapes=[pltpu.VMEM((tm, d), BF16)],
        compiler_params=_params("arbitrary", "arbitrary"),
    )(x2d, norm_w, sc1, sh1, w_fox, w_ret, w_f, cos_t, sin_t, qw, kw, fb)


def _cumsum_kernel(x_ref, o_ref):
    x = x_ref[0]
    r = x.shape[0]
    a = lax.broadcasted_iota(jnp.int32, (LANES, LANES), 0)
    b = lax.broadcasted_iota(jnp.int32, (LANES, LANES), 1)
    upper = (a <= b).astype(F32)
    within = jnp.dot(x, upper, precision=lax.Precision.HIGHEST, preferred_element_type=F32)
    tot = jnp.broadcast_to(within[:, LANES - 1:LANES], (r, LANES))
    ra = lax.broadcasted_iota(jnp.int32, (r, r), 0)
    rb = lax.broadcasted_iota(jnp.int32, (r, r), 1)
    strict = (rb < ra).astype(F32)
    before = jnp.dot(strict, tot, precision=lax.Precision.HIGHEST, preferred_element_type=F32)
    o_ref[0] = within + before


def _cumsum_rows(x):
    g, s = x.shape
    r = s // LANES
    out = pl.pallas_call(
        _cumsum_kernel,
        grid=(g,),
        in_specs=[pl.BlockSpec((1, r, LANES), lambda i: (i, 0, 0))],
        out_specs=pl.BlockSpec((1, r, LANES), lambda i: (i, 0, 0)),
        out_shape=jax.ShapeDtypeStruct((g, r, LANES), F32),
        compiler_params=_params("arbitrary"),
    )(x.reshape(g, r, LANES))
    return out.reshape(g, 1, s)


def _fox_kernel(tq, n_sub, first_ref, q_ref, k_ref, v_ref, cum_ref, o_ref, s_refs, m_ref, l_ref, acc_ref):
    group_id = pl.program_id(2)
    n_groups = pl.num_programs(2)
    head = pl.program_id(0) * pl.num_programs(1) + pl.program_id(1)
    n_slabs = tq // LANES

    m_ref[...] = jnp.full(m_ref.shape, NEG_BIG, F32)
    l_ref[...] = jnp.zeros(l_ref.shape, F32)
    acc_ref[...] = jnp.zeros(acc_ref.shape, F32)

    class Sub:
        def __init__(self, idx):
            self.rows = slice(idx * tq, (idx + 1) * tq)
            self.qi = n_sub * group_id + idx
            self.sa, self.sb = s_refs[2 * idx], s_refs[2 * idx + 1]
            q_start = pl.multiple_of(self.qi * tq, tq)
            self.c0 = cum_ref[0, :, pl.ds(q_start, LANES)][:, 0:1]
            self.first = first_ref[(head * n_groups + group_id) * n_sub + idx]
            self.n_off = self.qi - self.first

    def scores(sub, kb, s_ref):
        start = pl.multiple_of(kb * tq, tq)
        k = k_ref[pl.ds(start, tq), :]
        bias = (sub.c0 - cum_ref[0, :, pl.ds(start, tq)]) * LOG2E
        s_ref[...] = lax.dot_general(q_ref[sub.rows, :], k, (((1,), (1,)), ((), ())),
                                     preferred_element_type=F32) + bias

    def softmax_pv(sub, kb, s_ref, masked):
        start = pl.multiple_of(kb * tq, tq)
        if masked:
            row = lax.broadcasted_iota(jnp.int32, (LANES, LANES), 0)
            col = lax.broadcasted_iota(jnp.int32, (LANES, LANES), 1)
            for g in range(n_slabs):
                r0 = g * LANES
                slabs = [s_ref[r0:r0 + LANES, j * LANES:(j + 1) * LANES] for j in range(g + 1)]
                slabs[g] = jnp.where(col <= row, slabs[g], NEG_BIG)
                rs = slice(sub.rows.start + r0, sub.rows.start + r0 + LANES)
                update(rs, slabs, v_ref[pl.ds(start, (g + 1) * LANES), :])
        else:
            slabs = [s_ref[:, j * LANES:(j + 1) * LANES] for j in range(n_slabs)]
            update(sub.rows, slabs, v_ref[pl.ds(start, tq), :])

    def update(rs, slabs, v):
        mx = slabs[0]
        for t in slabs[1:]:
            mx = jnp.maximum(mx, t)
        m_prev = m_ref[rs, :]
        m_new = jnp.maximum(m_prev, jnp.max(mx, axis=-1, keepdims=True))
        alpha = jnp.exp2(m_prev - m_new)
        probs = [jnp.exp2(t - m_new) for t in slabs]
        psum = probs[0]
        for t in probs[1:]:
            psum = psum + t
        l_ref[rs, :] = alpha * l_ref[rs, :] + psum
        p = jnp.concatenate([t.astype(BF16) for t in probs], axis=-1)
        acc_ref[rs, :] = alpha * acc_ref[rs, :] + jnp.dot(p, v, preferred_element_type=F32)
        m_ref[rs, :] = m_new

    def sweep(sub, then):
        def pair(kb):
            scores(sub, kb + 1, sub.sb)
            softmax_pv(sub, kb, sub.sa, False)
            scores(sub, kb + 2, sub.sa)
            softmax_pv(sub, kb + 1, sub.sb, False)

        def body4(i, carry):
            pair(sub.first + 4 * i)
            pair(sub.first + 4 * i + 2)
            return carry

        def body2(i, carry):
            pair(sub.first + 2 * i)
            return carry

        n4 = sub.n_off // 4
        lax.fori_loop(0, n4, body4, 0)
        lax.fori_loop(2 * n4, sub.n_off // 2, body2, 0)

        @pl.when(sub.n_off % 2 == 0)
        def _():
            then()
            softmax_pv(sub, sub.qi, sub.sa, True)

        @pl.when(sub.n_off % 2 == 1)
        def _():
            scores(sub, sub.qi, sub.sb)
            softmax_pv(sub, sub.qi - 1, sub.sa, False)
            then()
            softmax_pv(sub, sub.qi, sub.sb, True)

    subs = [Sub(idx) for idx in range(n_sub)]
    scores(subs[0], subs[0].first, subs[0].sa)
    for sub, nxt in zip(subs, subs[1:] + [None]):
        sweep(sub, (lambda: None) if nxt is None else functools.partial(scores, nxt, nxt.first, nxt.sa))

    o_ref[...] = (acc_ref[...] / jnp.sum(l_ref[...], axis=-1, keepdims=True)).astype(BF16)


def _first_live_block(cum, tq, qk_bound):
    c0 = cum[:, 0, ::tq]
    cend = cum[:, 0, tq - 1::tq]
    gap = (c0[:, :, None] - cend[:, None, :]) * LOG2E + 2.0 * qk_bound
    nq = c0.shape[1]
    earlier = jnp.arange(nq)[None, :] < jnp.arange(nq)[:, None]
    return jnp.sum((gap < -UNDERFLOW_LOG2) & earlier[None], axis=-1).astype(jnp.int32).reshape(-1)


def _fox_attention(z, cum, qk_bound, bsz, seq, n_heads):
    n_sub = ATT_BLOCKS_PER_STEP
    tq = min(ATT_BLOCK, seq // n_sub)
    nq = seq // tq
    n_groups = nq // n_sub
    rows = n_sub * tq
    kern = functools.partial(_fox_kernel, tq, n_sub)
    grid_spec = pltpu.PrefetchScalarGridSpec(
        num_scalar_prefetch=1,
        grid=(bsz, n_heads, n_groups),
        in_specs=[pl.BlockSpec((rows, HEAD_DIM), lambda b, h, i, f: (b * n_groups + i, h)),
                  pl.BlockSpec((seq, HEAD_DIM), lambda b, h, i, f: (b, n_heads + h)),
                  pl.BlockSpec((seq, HEAD_DIM), lambda b, h, i, f: (b, 2 * n_heads + h)),
                  pl.BlockSpec((1, 1, seq), lambda b, h, i, f: (b * n_heads + h, 0, 0))],
        out_specs=pl.BlockSpec((rows, HEAD_DIM), lambda b, h, i, f: (b * n_groups + i, h)),
        scratch_shapes=[[pltpu.VMEM((tq, tq), F32)] * (2 * n_sub),
                        pltpu.VMEM((rows, LANES), F32), pltpu.VMEM((rows, LANES), F32),
                        pltpu.VMEM((rows, HEAD_DIM), F32)],
    )
    return pl.pallas_call(
        kern,
        grid_spec=grid_spec,
        out_shape=jax.ShapeDtypeStruct((bsz * seq, n_heads * HEAD_DIM), BF16),
        compiler_params=_params("arbitrary", "arbitrary", "arbitrary"),
    )(_first_live_block(cum, tq, qk_bound), z, z, z, cum)


def _ret_kernel(chunk, n_heads, lg_ref, q_ref, k_ref, v_ref, g_ref, nw_ref, o_ref, state_ref, decay_ref,
                qdec_ref, kdec_ref):
    first = (pl.program_id(0) == 0) & (pl.program_id(1) == 0)

    @pl.when(first)
    def _():
        i = lax.broadcasted_iota(jnp.int32, (chunk, chunk), 0)
        jj = lax.broadcasted_iota(jnp.int32, (chunk, chunk), 1)
        diff = (i - jj).astype(F32)
        pos = lax.broadcasted_iota(jnp.int32, (chunk, HEAD_DIM), 0).astype(F32)
        for h in range(n_heads):
            decay_ref[h] = jnp.where(diff >= 0, jnp.exp(lg_ref[h] * jnp.maximum(diff, 0.0)), 0.0)
            qdec_ref[h] = jnp.exp(lg_ref[h] * (pos + 1.0))
            kdec_ref[h] = jnp.exp(lg_ref[h] * (chunk - 1.0 - pos))

    @pl.when(pl.program_id(1) == 0)
    def _():
        state_ref[...] = jnp.zeros(state_ref.shape, F32)

    for h in range(n_heads):
        log_g = lg_ref[h]
        cols = slice(h * HEAD_DIM, (h + 1) * HEAD_DIM)
        q = q_ref[:, cols]
        k = k_ref[:, cols]
        v = v_ref[:, cols]
        scores = lax.dot_general(q, k, (((1,), (1,)), ((), ())), preferred_element_type=F32)
        scores = scores * decay_ref[h]
        intra = jnp.dot(scores.astype(BF16), v, preferred_element_type=F32)
        state = state_ref[h]
        inter = jnp.dot(q, state.astype(BF16), preferred_element_type=F32) * qdec_ref[h]
        kd = (k.astype(F32) * kdec_ref[h]).astype(BF16)
        kv = lax.dot_general(kd, v, (((0,), (0,)), ((), ())), preferred_element_type=F32)
        state_ref[h] = state * jnp.exp(jnp.full((1, HEAD_DIM), chunk, F32) * log_g) + kv
        o = intra + inter
        ms = jnp.mean(o * o, axis=-1, keepdims=True)
        o = o * lax.rsqrt(ms + EPS) * nw_ref[:, cols]
        o_ref[:, cols] = (o * _silu(g_ref[:, cols].astype(F32))).astype(BF16)


def _retention(z, log_g, norm_w, bsz, seq, n_heads, col0):
    chunk = min(RET_CHUNK, seq)
    nt = seq // chunk
    width = n_heads * HEAD_DIM
    c0 = col0 // width
    kern = functools.partial(_ret_kernel, chunk, n_heads)

    def sec(s):
        return pl.BlockSpec((chunk, width), lambda b, t, lg: (b * nt + t, c0 + s))

    grid_spec = pltpu.PrefetchScalarGridSpec(
        num_scalar_prefetch=1,
        grid=(bsz, nt),
        in_specs=[sec(0), sec(1), sec(2), sec(3), pl.BlockSpec((1, width), lambda b, t, lg: (0, 0))],
        out_specs=pl.BlockSpec((chunk, width), lambda b, t, lg: (b * nt + t, 0)),
        scratch_shapes=[pltpu.VMEM((n_heads, HEAD_DIM, HEAD_DIM), F32),
                        pltpu.VMEM((n_heads, chunk, chunk), F32),
                        pltpu.VMEM((n_heads, chunk, HEAD_DIM), F32),
                        pltpu.VMEM((n_heads, chunk, HEAD_DIM), F32)],
    )
    return pl.pallas_call(
        kern,
        grid_spec=grid_spec,
        out_shape=jax.ShapeDtypeStruct((bsz * seq, width), BF16),
        compiler_params=_params("arbitrary", "arbitrary"),
    )(log_g, z, z, z, z, norm_w)


def _outproj_kernel(oa_ref, ob_ref, wa_ref, wb_ref, x_ref, g1_ref, nw_ref, sc_ref, sh_ref, wr_ref, br_ref,
                    x1_ref, hp_ref, lg_ref):
    mix = jnp.dot(oa_ref[...], wa_ref[...], preferred_element_type=F32)
    mix = mix + jnp.dot(ob_ref[...], wb_ref[...], preferred_element_type=F32)
    x1 = x_ref[...] + g1_ref[0] * mix
    x1_ref[...] = x1
    ms = jnp.mean(x1 * x1, axis=-1, keepdims=True)
    h2 = x1 * lax.rsqrt(ms + EPS) * nw_ref[...] * (1.0 + sc_ref[0]) + sh_ref[0]
    hp_ref[...] = _rows_to_tiles(_pack_halves(h2))
    h_hi = h2.astype(BF16)
    h_lo = (h2 - h_hi.astype(F32)).astype(BF16)
    both = jnp.dot(h_hi, wr_ref[...], preferred_element_type=F32)
    cross = jnp.dot(h_lo, wr_ref[:, :LANES], preferred_element_type=F32)
    lg_ref[...] = both[:, :LANES] + both[:, LANES:] + cross + br_ref[...]


def _output_projection(o_a, o_b, w_out, x2d, seq, g1, norm_w, sc2, sh2, w_router, b_router):
    n, d = x2d.shape
    da = o_a.shape[1]
    tm = min(OUTPROJ_ROWS, seq)
    tiles_per_seq = seq // tm
    bsel = lambda i: (i // tiles_per_seq, 0, 0)
    return pl.pallas_call(
        _outproj_kernel,
        grid=(n // tm,),
        in_specs=[pl.BlockSpec((tm, da), lambda i: (i, 0)),
                  pl.BlockSpec((tm, da), lambda i: (i, 0)),
                  pl.BlockSpec((da, d), lambda i: (0, 0)),
                  pl.BlockSpec((da, d), lambda i: (1, 0)),
                  pl.BlockSpec((tm, d), lambda i: (i, 0)),
                  pl.BlockSpec((1, 1, d), bsel),
                  pl.BlockSpec((1, d), lambda i: (0, 0)),
                  pl.BlockSpec((1, 1, d), bsel),
                  pl.BlockSpec((1, 1, d), bsel),
                  pl.BlockSpec((d, 2 * LANES), lambda i: (0, 0)),
                  pl.BlockSpec((1, LANES), lambda i: (0, 0))],
        out_specs=[pl.BlockSpec((tm, d), lambda i: (i, 0)),
                   pl.BlockSpec((tm, d // 2 // LANES, LANES), lambda i: (i, 0, 0)),
                   pl.BlockSpec((tm, LANES), lambda i: (i, 0))],
        out_shape=[jax.ShapeDtypeStruct((n, d), F32),
                   jax.ShapeDtypeStruct((n, d // 2 // LANES, LANES), U32),
                   jax.ShapeDtypeStruct((n, LANES), F32)],
        compiler_params=_params("arbitrary"),
    )(o_a, o_b, w_out, w_out, x2d, g1, norm_w, sc2, sh2, w_router, b_router)


def _route_kernel(blk, n_blocks, lg_ref, gate_ref, ids_ref, plan_ref, run_ref):
    i = pl.program_id(0)

    @pl.when(i == 0)
    def _():
        run_ref[...] = jnp.zeros(run_ref.shape, F32)

    lg = lg_ref[...]
    tt = lg.shape[0]
    lane = lax.broadcasted_iota(jnp.int32, lg.shape, 1).astype(F32)
    big = 1e6

    def rmax(v):
        return jnp.max(v, axis=-1, keepdims=True)

    def rmin(v):
        return jnp.min(v, axis=-1, keepdims=True)

    def rsum(v):
        return jnp.sum(v, axis=-1, keepdims=True)

    cmask = lane < N_GROUPS
    cm = jnp.where(cmask, lg, NEG_BIG)
    ce = jnp.where(cmask, jnp.exp(cm - rmax(cm)), 0.0)
    pgrp = ce / rsum(ce)
    p_g = rmax(pgrp)
    g_sel = rmin(jnp.where(cmask & (pgrp == p_g), lane, big))

    lo = N_GROUPS + EXPERTS_PER_GROUP * g_sel
    fmask = (lane >= lo) & (lane < lo + EXPERTS_PER_GROUP)
    fm = jnp.where(fmask, lg, NEG_BIG)
    fe = jnp.where(fmask, jnp.exp(fm - rmax(fm)), 0.0)
    fp = fe / rsum(fe)
    fp = jnp.where(fmask, fp, -1.0)
    p1 = rmax(fp)
    i1 = rmin(jnp.where(fp == p1, lane, big))
    fp2 = jnp.where(lane == i1, -1.0, fp)
    p2 = rmax(fp2)
    i2 = rmin(jnp.where(fp2 == p2, lane, big))
    denom = p1 + p2
    w1 = p_g * p1 / denom
    w2 = p_g * p2 / denom
    e1 = i1 - N_GROUPS
    e2 = i2 - N_GROUPS

    gate_ref[...] = jnp.where(lane == 0, w1, jnp.where(lane == 1, w2, 0.0))

    oh1 = (lane == e1).astype(F32)
    oh2 = (lane == e2).astype(F32)
    both = oh1 + oh2
    ra = lax.broadcasted_iota(jnp.int32, (tt, tt), 0)
    rb = lax.broadcasted_iota(jnp.int32, (tt, tt), 1)
    strict = (rb < ra).astype(BF16)
    prefix = jnp.dot(strict, both.astype(BF16), preferred_element_type=F32) + run_ref[...]
    r1 = rsum(prefix * oh1)
    r2 = rsum(prefix * oh2)
    run_ref[...] = run_ref[...] + jnp.sum(both, axis=0, keepdims=True)

    packed = jnp.where(lane == 0, e1, jnp.where(lane == 1, e2, jnp.where(lane == 2, r1,
                                                                        jnp.where(lane == 3, r2, 0.0))))
    ids_ref[...] = jnp.transpose(packed)[:8, :].astype(jnp.int32)

    @pl.when(i == pl.num_programs(0) - 1)
    def _():
        cnt = jnp.broadcast_to(run_ref[...], (8, LANES))
        lane8 = lax.broadcasted_iota(jnp.int32, (8, LANES), 1)
        padded = jnp.floor((cnt + (blk - 1.0)) * (1.0 / blk)) * blk
        pend = padded
        for sh in (1, 2, 4, 8, 16, 32, 64):
            pend = pend + jnp.where(lane8 >= sh, pltpu.roll(pend, sh, 1), 0.0)
        pstart = pend - padded
        total = jnp.max(pend, axis=-1, keepdims=True)
        tail = total + (lane8 - N_EXPERTS).astype(F32) * blk
        fill = jnp.where(lane8 < N_EXPERTS, jnp.where(padded > 0, pend - blk, -1.0),
                         jnp.where((lane8 < 2 * N_EXPERTS) & (tail < n_blocks * blk), tail, -1.0))
        row8 = lax.broadcasted_iota(jnp.int32, (8, LANES), 0)
        plan_ref[...] = jnp.where(row8 == 0, pstart, jnp.where(row8 == 1, fill,
                                                               jnp.where(row8 == 2, cnt, 0.0))).astype(jnp.int32)


def _route(logits, blk, n_blocks):
    n = logits.shape[0]
    tt = min(ROUTE_ROWS, n)
    blkspec = lambda: pl.BlockSpec((tt, LANES), lambda i: (i, 0))
    return pl.pallas_call(
        functools.partial(_route_kernel, blk, n_blocks),
        grid=(n // tt,),
        in_specs=[blkspec()],
        out_specs=[blkspec(),
                   pl.BlockSpec((8, tt), lambda i: (0, i)),
                   pl.BlockSpec((8, LANES), lambda i: (0, 0))],
        out_shape=[jax.ShapeDtypeStruct((n, LANES), F32),
                   jax.ShapeDtypeStruct((8, n), jnp.int32),
                   jax.ShapeDtypeStruct((8, LANES), jnp.int32)],
        scratch_shapes=[pltpu.VMEM((1, LANES), F32)],
        compiler_params=_params("arbitrary"),
    )(logits)


def _dispatch_kernel(tt, blk, n_fill, dest_ref, fill_ref, h_ref, xs_ref, zero_ref, sem, zsem):
    i = pl.program_id(0)
    base = i * (tt * TOP_K)

    @pl.when(i == 0)
    def _():
        zero_ref[...] = jnp.zeros(zero_ref.shape, U32)

        def zcopy(z):
            row = pl.multiple_of(jnp.maximum(fill_ref[z], 0), blk)
            return pltpu.make_async_copy(zero_ref, xs_ref.at[pl.ds(row, blk)], zsem)

        def zissue(z, carry):
            @pl.when(fill_ref[z] >= 0)
            def _():
                zcopy(z).start()
            return carry

        def zdrain(z, carry):
            @pl.when(fill_ref[z] >= 0)
            def _():
                zcopy(z).wait()
            return carry

        lax.fori_loop(0, n_fill, zissue, 0)
        lax.fori_loop(0, n_fill, zdrain, 0)

    def copy(r, kk):
        d = dest_ref[base + r * TOP_K + kk]
        return pltpu.make_async_copy(h_ref.at[r], xs_ref.at[d], sem)

    def issue(r, carry):
        for kk in range(TOP_K):
            copy(r, kk).start(priority=kk % 2)
        return carry

    lax.fori_loop(0, tt, issue, 0, unroll=8)
    for _ in range(TOP_K):
        pltpu.make_async_copy(h_ref, xs_ref.at[pl.ds(0, tt)], sem).wait()


def _dispatch(h_packed, dest_flat, fill_rows, n_slots, blk):
    n = h_packed.shape[0]
    tile = h_packed.shape[1:]
    tt = min(MOE_TOKENS_PER_STEP, n)
    n_fill = fill_rows.shape[0]
    grid_spec = pltpu.PrefetchScalarGridSpec(
        num_scalar_prefetch=2,
        grid=(n // tt,),
        in_specs=[pl.BlockSpec((tt,) + tile, lambda i, d, f: (i, 0, 0))],
        out_specs=pl.BlockSpec(memory_space=pl.ANY),
        scratch_shapes=[pltpu.VMEM((blk,) + tile, U32), pltpu.SemaphoreType.DMA(()), pltpu.SemaphoreType.DMA(())],
    )
    return pl.pallas_call(
        functools.partial(_dispatch_kernel, tt, blk, n_fill),
        grid_spec=grid_spec,
        out_shape=jax.ShapeDtypeStruct((n_slots,) + tile, U32),
        compiler_params=_params("arbitrary"),
    )(dest_flat, fill_rows, h_packed)


def _expert_kernel(blk, ahead, cnt_ref, pstart_ref, fill_ref, xs_ref, w1_ref, w3_ref, w2_ref, y_ref,
                   w1f, w3f, w2f, w1b, w3b, w2b, xbuf, ybuf, done_ref, w_sem, in_sem, out_sem):
    e = pl.program_id(0)
    n_exp = pl.num_programs(0)
    wslot = e % 2
    n_blk = (cnt_ref[e] + (blk - 1)) // blk
    base = pstart_ref[e]
    n_x = xbuf.shape[0]

    def weight_copies(ex, slot):
        return [pltpu.make_async_copy(src.at[ex], dst.at[slot], w_sem.at[slot])
                for src, dst in ((w1_ref, w1f), (w3_ref, w3f), (w2_ref, w2f))]

    def rows(b):
        return pl.ds(pl.multiple_of(base + b * blk, blk), blk)

    def in_copy(b, slot):
        return pltpu.make_async_copy(xs_ref.at[rows(b)], xbuf.at[slot], in_sem.at[slot])

    def out_copy(b, slot):
        return pltpu.make_async_copy(ybuf.at[slot], y_ref.at[rows(b)], out_sem.at[slot])

    @pl.when(e == 0)
    def _():
        for c in weight_copies(0, 0):
            c.start()

    for p in range(ahead):
        @pl.when(p < n_blk)
        def _():
            in_copy(p, p).start()

    @pl.when(e + 1 < n_exp)
    def _():
        for c in weight_copies(e + 1, 1 - wslot):
            c.start()

    for c in weight_copies(e, wslot):
        c.wait()
    w1b[...] = w1f[wslot].astype(BF16)
    w3b[...] = w3f[wslot].astype(BF16)
    w2b[...] = w2f[wslot].astype(BF16)

    @pl.when(e == 0)
    def _():
        done_ref[0] = 0

    done = done_ref[0]

    def out_wait(slot):
        pltpu.make_async_copy(ybuf.at[slot], y_ref.at[pl.ds(0, blk)], out_sem.at[slot]).wait()

    def body(b, carry):
        slot = (done + b) % 2

        @pl.when(b + ahead < n_blk)
        def _():
            in_copy(b + ahead, (b + ahead) % n_x).start()

        in_copy(b, b % n_x).wait()

        @pl.when(done + b >= 2)
        def _():
            out_wait(slot)

        lo, hi = _unpack_halves(_tiles_to_rows(xbuf[b % n_x]))
        lo = lo.astype(BF16)
        hi = hi.astype(BF16)
        half = lo.shape[1]
        a = jnp.dot(lo, w1b[:half, :], preferred_element_type=F32)
        a = a + jnp.dot(hi, w1b[half:, :], preferred_element_type=F32)
        g = jnp.dot(lo, w3b[:half, :], preferred_element_type=F32)
        g = g + jnp.dot(hi, w3b[half:, :], preferred_element_type=F32)
        mid = (_silu(a) * g).astype(BF16)
        ybuf[slot] = _rows_to_tiles(_pack_halves(jnp.dot(mid, w2b[...], preferred_element_type=F32)))
        out_copy(b, slot).start(priority=1)
        return carry

    lax.fori_loop(0, n_blk, body, 0)
    total = done + n_blk
    done_ref[0] = total

    @pl.when(e == n_exp - 1)
    def _():
        @pl.when(total >= 2)
        def _():
            out_wait(total % 2)

        @pl.when(total >= 1)
        def _():
            out_wait((total - 1) % 2)

        ybuf[0] = jnp.zeros(ybuf.shape[1:], U32)

        def zcopy(t):
            row = pl.multiple_of(jnp.maximum(fill_ref[N_EXPERTS + t], 0), blk)
            return pltpu.make_async_copy(ybuf.at[0], y_ref.at[pl.ds(row, blk)], out_sem.at[0])

        def zissue(t, carry):
            @pl.when(fill_ref[N_EXPERTS + t] >= 0)
            def _():
                zcopy(t).start()
            return carry

        def zdrain(t, carry):
            @pl.when(fill_ref[N_EXPERTS + t] >= 0)
            def _():
                zcopy(t).wait()
            return carry

        lax.fori_loop(0, N_EXPERTS, zissue, 0)
        lax.fori_loop(0, N_EXPERTS, zdrain, 0)


def _expert_blocks(xs, counts, pstart, fill_rows, w1, w3, w2, blk):
    n_slots = xs.shape[0]
    tile = xs.shape[1:]
    n_exp, d, de = w1.shape
    ahead = EXPERT_AHEAD
    hbm = pl.BlockSpec(memory_space=pl.ANY)
    grid_spec = pltpu.PrefetchScalarGridSpec(
        num_scalar_prefetch=3,
        grid=(n_exp,),
        in_specs=[hbm, hbm, hbm, hbm],
        out_specs=hbm,
        scratch_shapes=[pltpu.VMEM((2, d, de), F32), pltpu.VMEM((2, d, de), F32), pltpu.VMEM((2, de, d), F32),
                        pltpu.VMEM((d, de), BF16), pltpu.VMEM((d, de), BF16), pltpu.VMEM((de, d), BF16),
                        pltpu.VMEM((ahead + 1, blk) + tile, U32), pltpu.VMEM((2, blk) + tile, U32),
                        pltpu.SMEM((1,), jnp.int32),
                        pltpu.SemaphoreType.DMA((2,)), pltpu.SemaphoreType.DMA((ahead + 1,)),
                        pltpu.SemaphoreType.DMA((2,))],
    )
    return pl.pallas_call(
        functools.partial(_expert_kernel, blk, ahead),
        grid_spec=grid_spec,
        out_shape=jax.ShapeDtypeStruct((n_slots,) + tile, U32),
        compiler_params=_params("arbitrary"),
    )(counts, pstart, fill_rows, xs, w1, w3, w2)


def _combine_kernel(tt, n_tiles, dest_ref, x1_ref, g2_ref, gate_ref, yb_ref, o_ref, buf, sems):
    i = pl.program_id(0)

    def copy(tile, slot, r, kk):
        d = dest_ref[(tile * tt + r) * TOP_K + kk]
        return pltpu.make_async_copy(yb_ref.at[d], buf.at[slot, kk, r], sems.at[slot])

    def issue_tile(tile, slot):
        def body(r, carry):
            for kk in range(TOP_K):
                copy(tile, slot, r, kk).start(priority=kk % 2)
            return carry
        lax.fori_loop(0, tt, body, 0, unroll=8)

    def wait_tile(tile, slot):
        for kk in range(TOP_K):
            pltpu.make_async_copy(yb_ref.at[pl.ds(0, tt)], buf.at[slot, kk], sems.at[slot]).wait()

    slot = i % 2

    @pl.when(i == 0)
    def _():
        issue_tile(0, 0)

    @pl.when(i + 1 < n_tiles)
    def _():
        issue_tile(i + 1, 1 - slot)

    wait_tile(i, slot)

    gate = gate_ref[...]
    wa = gate[:, 0:1]
    wb = gate[:, 1:2]
    lo_a, hi_a = _unpack_halves(_tiles_to_rows(buf[slot, 0]))
    lo_b, hi_b = _unpack_halves(_tiles_to_rows(buf[slot, 1]))
    y = jnp.concatenate([wa * lo_a + wb * lo_b, wa * hi_a + wb * hi_b], axis=-1)
    o_ref[...] = x1_ref[...] + g2_ref[0] * y


def _combine(x1, seq, g2, gates, dest_flat, yb):
    n, d = x1.shape
    tile = yb.shape[1:]
    tt = min(MOE_TOKENS_PER_STEP, seq)
    n_tiles = n // tt
    tiles_per_seq = seq // tt
    grid_spec = pltpu.PrefetchScalarGridSpec(
        num_scalar_prefetch=1,
        grid=(n_tiles,),
        in_specs=[pl.BlockSpec((tt, d), lambda i, dr: (i, 0)),
                  pl.BlockSpec((1, 1, d), lambda i, dr: (i // tiles_per_seq, 0, 0)),
                  pl.BlockSpec((tt, LANES), lambda i, dr: (i, 0)),
                  pl.BlockSpec(memory_space=pl.ANY)],
        out_specs=pl.BlockSpec((tt, d), lambda i, dr: (i, 0)),
        scratch_shapes=[pltpu.VMEM((2, TOP_K, tt) + tile, U32), pltpu.SemaphoreType.DMA((2,))],
    )
    return pl.pallas_call(
        functools.partial(_combine_kernel, tt, n_tiles),
        grid_spec=grid_spec,
        out_shape=jax.ShapeDtypeStruct((n, d), F32),
        compiler_params=_params("arbitrary"),
    )(dest_flat, x1, g2, gates, yb)


def _rotation_tables(seq):
    half = HEAD_DIM // 2
    theta = ROPE_BASE ** (-np.arange(half, dtype=np.float64) / half)
    ang = np.arange(seq, dtype=np.float64)[:, None] * theta[None, :]
    cos_t = np.concatenate([np.cos(ang), np.cos(ang)], axis=-1).astype(np.float32)
    sin_t = np.concatenate([-np.sin(ang), np.sin(ang)], axis=-1).astype(np.float32)
    return jnp.asarray(cos_t), jnp.asarray(sin_t)


def _layer(x, c, w_ada, b_ada, norm1_w, w_in, forget_bias, q_norm_w, k_norm_w, ret_norm_w, w_out, norm2_w,
           w_coarse, b_coarse, w_fine, b_fine, w1, w3, w2):
    bsz, seq, d = x.shape
    n = bsz * seq
    d_fox = d // 2
    d_ret = d // 2
    n_heads = d_fox // HEAD_DIM

    mod = _ada_modulation(c, w_ada, b_ada)
    sh1, sc1, g1, sh2, sc2, g2 = [m.reshape(bsz, 1, d) for m in jnp.split(mod, 6, axis=-1)]

    f0 = 3 * d_fox
    w_fox = w_in[:, :f0].astype(BF16)
    w_ret = w_in[:, f0 + n_heads:].astype(BF16)
    w_f = jnp.zeros((d, LANES), BF16).at[:, :n_heads].set(w_in[:, f0:f0 + n_heads].astype(BF16))
    fb = jnp.zeros((1, LANES), F32).at[0, :n_heads].set(forget_bias)

    cos_t, sin_t = _rotation_tables(seq)

    x2d = x.reshape(n, d)
    z, log_f = _input_projection(x2d, seq, norm1_w.reshape(1, d), sc1, sh1, w_fox, w_ret, w_f, cos_t, sin_t,
                                 q_norm_w.reshape(1, HEAD_DIM), k_norm_w.reshape(1, HEAD_DIM), fb)

    lf = log_f[:, :n_heads].reshape(bsz, seq, n_heads).transpose(0, 2, 1).reshape(bsz * n_heads, seq)
    cum = _cumsum_rows(lf)

    qk_bound = 1.02 * LOG2E * HEAD_DIM ** 0.5 * jnp.max(jnp.abs(q_norm_w)) * jnp.max(jnp.abs(k_norm_w))
    o_a = _fox_attention(z, cum, qk_bound, bsz, seq, n_heads)
    log_g = jnp.log(1.0 - 2.0 ** (-5.0 - jnp.arange(n_heads, dtype=F32)))
    o_b = _retention(z, log_g, ret_norm_w.reshape(1, d_ret), bsz, seq, n_heads, 3 * d_fox)

    w_router = jnp.zeros((d, LANES), F32)
    w_router = w_router.at[:, :N_GROUPS].set(w_coarse)
    w_router = w_router.at[:, N_GROUPS:N_GROUPS + N_EXPERTS].set(
        w_fine.transpose(1, 0, 2).reshape(d, N_EXPERTS))
    b_router = jnp.zeros((1, LANES), F32)
    b_router = b_router.at[0, :N_GROUPS].set(b_coarse)
    b_router = b_router.at[0, N_GROUPS:N_GROUPS + N_EXPERTS].set(b_fine.reshape(N_EXPERTS))

    wr_hi = w_router.astype(BF16)
    wr_lo = (w_router - wr_hi.astype(F32)).astype(BF16)
    x1, h_packed, logits = _output_projection(o_a, o_b, w_out.astype(BF16), x2d, seq, g1,
                                              norm2_w.reshape(1, d), sc2, sh2,
                                              jnp.concatenate([wr_hi, wr_lo], axis=1), b_router)

    blk = EXPERT_BLOCK
    nk = n * TOP_K
    n_blocks = nk // blk + N_EXPERTS
    gates, ids, plan = _route(logits, blk, n_blocks)
    pstart = plan[0, :N_EXPERTS]
    fill_rows = plan[1, :2 * N_EXPERTS]
    counts = plan[2, :N_EXPERTS]
    eid = ids[0:TOP_K]
    hit = eid[None] == jnp.arange(N_EXPERTS, dtype=jnp.int32)[:, None, None]
    dest = (jnp.sum(jnp.where(hit, pstart[:, None, None], 0), axis=0) + ids[TOP_K:2 * TOP_K]).T.reshape(nk)

    xs = _dispatch(h_packed, dest, fill_rows, n_blocks * blk, blk)
    yb = _expert_blocks(xs, counts, pstart, fill_rows, w1, w3, w2, blk)
    out = _combine(x1, seq, g2, gates, dest, yb)
    return out.reshape(bsz, seq, d)


def kernel(x, c, w_ada, b_ada, norm1_w, w_in, forget_bias, q_norm_w, k_norm_w, ret_norm_w, w_out, norm2_w,
           w_coarse, b_coarse, w_fine, b_fine, w1, w3, w2):
    c_in = c
    for l in range(w_ada.shape[0]):
        x = _layer(x, c_in, w_ada[l], b_ada[l], norm1_w[l], w_in[l], forget_bias[l], q_norm_w[l],
                   k_norm_w[l], ret_norm_w[l], w_out[l], norm2_w[l], w_coarse[l], b_coarse[l],
                   w_fine[l], b_fine[l], w1[l], w3[l], w2[l])
    return x
```

```python
import functools

import jax
import jax.numpy as jnp
import numpy as np
from jax import lax
from jax.experimental import pallas as pl
from jax.experimental.pallas import tpu as pltpu

HEAD_DIM = 128
N_GROUPS = 4
EXPERTS_PER_GROUP = 8
N_EXPERTS = N_GROUPS * EXPERTS_PER_GROUP
TOP_K = 2
ROPE_BASE = 10000.0
EPS = 1e-6

LANES = 128
VMEM_LIMIT = 56 * 1024 * 1024
NEG_BIG = -1e30
LOG2E = 1.4426950408889634
UNDERFLOW_LOG2 = 160.0

ADA_COLS = 1024
INPROJ_ROWS = 1024
INPROJ_COLS = 1024
ATT_BLOCK = 512
ATT_BLOCKS_PER_STEP = 4
RET_CHUNK = 256
OUTPROJ_ROWS = 512
ROUTE_ROWS = 512
DISPATCH_TOKENS = 1024
COMBINE_TOKENS = 512
EXPERT_BLOCK = 256
EXPERT_AHEAD = 3

F32 = jnp.float32
BF16 = jnp.bfloat16
U32 = jnp.uint32


def _params(*sem):
    return pltpu.CompilerParams(dimension_semantics=sem, vmem_limit_bytes=VMEM_LIMIT)


def _silu(v):
    return v * (1.0 / (1.0 + jnp.exp(-v)))


def _pack_halves(y):
    w = y.shape[1] // 2
    lo = pltpu.bitcast(y[:, :w].astype(BF16).astype(F32), U32)
    hi = pltpu.bitcast(y[:, w:].astype(BF16).astype(F32), U32)
    return (hi & jnp.uint32(0xFFFF0000)) | (lo >> 16)


def _rows_to_tiles(p):
    return pltpu.einshape("m(ck)->mck", p, c=8, k=LANES)


def _tiles_to_rows(t):
    return pltpu.einshape("mck->m(ck)", t)


def _unpack_halves(p):
    lo = pltpu.bitcast(p << 16, F32)
    hi = pltpu.bitcast(p & jnp.uint32(0xFFFF0000), F32)
    return lo, hi


def _ada_kernel(bsz, ct_ref, w_ref, b_ref, o_ref):
    w = w_ref[...]
    rows = []
    for b in range(o_ref.shape[0]):
        if b < bsz:
            cb = _silu(ct_ref[:, b:b + 1])
            rows.append(jnp.sum(cb * w, axis=0, keepdims=True) + b_ref[...])
        else:
            rows.append(jnp.zeros_like(b_ref[...]))
    o_ref[...] = jnp.concatenate(rows, axis=0)


def _ada_modulation(c, w_ada, b_ada):
    bsz, d = c.shape
    n = w_ada.shape[1]
    tn = ADA_COLS
    ct = jnp.zeros((d, LANES), F32).at[:, :bsz].set(c.T)
    assert bsz <= 8, "one sublane tile of modulation rows"
    out = pl.pallas_call(
        functools.partial(_ada_kernel, bsz),
        grid=(n // tn,),
        in_specs=[pl.BlockSpec((d, LANES), lambda j: (0, 0)),
                  pl.BlockSpec((d, tn), lambda j: (0, j)),
                  pl.BlockSpec((1, tn), lambda j: (0, j))],
        out_specs=pl.BlockSpec((8, tn), lambda j: (0, j)),
        out_shape=jax.ShapeDtypeStruct((8, n), F32),
        compiler_params=_params("arbitrary"),
    )(ct, w_ada, b_ada.reshape(1, n))
    return out[:bsz]


def _inproj_kernel(q_t, r_t, x_ref, nw_ref, sc_ref, sh_ref, wa_ref, wb_ref, wf_ref, cos_ref, sin_ref,
                   qw_ref, kw_ref, fb_ref, z_ref, f_ref, h_ref):
    j = pl.program_id(1)
    r0 = 3 * q_t

    @pl.when(j == 0)
    def _():
        x = x_ref[...]
        ms = jnp.mean(x * x, axis=-1, keepdims=True)
        y = x * lax.rsqrt(ms + EPS) * nw_ref[...]
        h = (y * (1.0 + sc_ref[0]) + sh_ref[0]).astype(BF16)
        h_ref[...] = h
        t = jnp.dot(h, wf_ref[...], preferred_element_type=F32) + fb_ref[...]
        f_ref[...] = jnp.minimum(t, 0.0) - jnp.log(1.0 + jnp.exp(-jnp.abs(t)))

    def heads_of(acc):
        return [acc[:, hh * HEAD_DIM:(hh + 1) * HEAD_DIM] for hh in range(acc.shape[1] // HEAD_DIM)]

    def head_norm(acc, w_row):
        outs = []
        for a in heads_of(acc):
            ms = jnp.mean(a * a, axis=-1, keepdims=True)
            outs.append(a * lax.rsqrt(ms + EPS) * w_row)
        return jnp.concatenate(outs, axis=-1).astype(BF16)

    def rotate(acc, scale):
        cs = cos_ref[...] * scale
        sn = sin_ref[...] * scale
        outs = [a * cs + pltpu.roll(a, HEAD_DIM // 2, 1) * sn for a in heads_of(acc)]
        return jnp.concatenate(outs, axis=-1).astype(BF16)

    def fox():
        return jnp.dot(h_ref[...], wa_ref[...], preferred_element_type=F32)

    def ret():
        return jnp.dot(h_ref[...], wb_ref[...], preferred_element_type=F32)

    @pl.when(j < q_t)
    def _():
        z_ref[...] = head_norm(fox(), qw_ref[...] * (LOG2E * HEAD_DIM ** -0.5))

    @pl.when((j >= q_t) & (j < 2 * q_t))
    def _():
        z_ref[...] = head_norm(fox(), kw_ref[...])

    @pl.when((j >= 2 * q_t) & (j < r0))
    def _():
        z_ref[...] = fox().astype(BF16)

    @pl.when((j >= r0) & (j < r0 + r_t))
    def _():
        z_ref[...] = rotate(ret(), 1.0)

    @pl.when((j >= r0 + r_t) & (j < r0 + 2 * r_t))
    def _():
        z_ref[...] = rotate(ret(), HEAD_DIM ** -0.5)

    @pl.when(j >= r0 + 2 * r_t)
    def _():
        z_ref[...] = ret().astype(BF16)


def _input_projection(x2d, seq, norm_w, sc1, sh1, w_fox, w_ret, w_f, cos_t, sin_t, qw, kw, fb):
    n, d = x2d.shape
    tm, tn = min(INPROJ_ROWS, seq), INPROJ_COLS
    fox_tiles = w_fox.shape[1] // tn
    ret_tiles = w_ret.shape[1] // tn
    tiles_per_seq = seq // tm
    kern = functools.partial(_inproj_kernel, fox_tiles // 3, ret_tiles // 4)
    bsel = lambda i, j: (i // tiles_per_seq, 0, 0)
    const = lambda i, j: (0, 0)
    return pl.pallas_call(
        kern,
        grid=(n // tm, fox_tiles + ret_tiles),
        in_specs=[pl.BlockSpec((tm, d), lambda i, j: (i, 0)),
                  pl.BlockSpec((1, d), const),
                  pl.BlockSpec((1, 1, d), bsel),
                  pl.BlockSpec((1, 1, d), bsel),
                  pl.BlockSpec((d, tn), lambda i, j: (0, jnp.minimum(j, fox_tiles - 1))),
                  pl.BlockSpec((d, tn), lambda i, j: (0, jnp.maximum(j - fox_tiles, 0))),
                  pl.BlockSpec((d, LANES), const),
                  pl.BlockSpec((tm, HEAD_DIM), lambda i, j: (i % tiles_per_seq, 0)),
                  pl.BlockSpec((tm, HEAD_DIM), lambda i, j: (i % tiles_per_seq, 0)),
                  pl.BlockSpec((1, HEAD_DIM), const),
                  pl.BlockSpec((1, HEAD_DIM), const),
                  pl.BlockSpec((1, LANES), const)],
        out_specs=[pl.BlockSpec((tm, tn), lambda i, j: (i, j)),
                   pl.BlockSpec((tm, LANES), lambda i, j: (i, 0))],
        out_shape=[jax.ShapeDtypeStruct((n, w_fox.shape[1] + w_ret.shape[1]), BF16),
                   jax.ShapeDtypeStruct((n, LANES), F32)],
        scratch_shapes=[pltpu.VMEM((tm, d), BF16)],
        compiler_params=_params("arbitrary", "arbitrary"),
    )(x2d, norm_w, sc1, sh1, w_fox, w_ret, w_f, cos_t, sin_t, qw, kw, fb)


def _cumsum_kernel(x_ref, o_ref):
    x = x_ref[0]
    r = x.shape[0]
    a = lax.broadcasted_iota(jnp.int32, (LANES, LANES), 0)
    b = lax.broadcasted_iota(jnp.int32, (LANES, LANES), 1)
    upper = (a <= b).astype(F32)
    within = jnp.dot(x, upper, precision=lax.Precision.HIGHEST, preferred_element_type=F32)
    tot = jnp.broadcast_to(within[:, LANES - 1:LANES], (r, LANES))
    ra = lax.broadcasted_iota(jnp.int32, (r, r), 0)
    rb = lax.broadcasted_iota(jnp.int32, (r, r), 1)
    strict = (rb < ra).astype(F32)
    before = jnp.dot(strict, tot, precision=lax.Precision.HIGHEST, preferred_element_type=F32)
    o_ref[0] = within + before


def _cumsum_rows(x):
    g, s = x.shape
    r = s // LANES
    out = pl.pallas_call(
        _cumsum_kernel,
        grid=(g,),
        in_specs=[pl.BlockSpec((1, r, LANES), lambda i: (i, 0, 0))],
        out_specs=pl.BlockSpec((1, r, LANES), lambda i: (i, 0, 0)),
        out_shape=jax.ShapeDtypeStruct((g, r, LANES), F32),
        compiler_params=_params("arbitrary"),
    )(x.reshape(g, r, LANES))
    return out.reshape(g, 1, s)


def _fox_kernel(tq, n_sub, first_ref, q_ref, k_ref, v_ref, cum_ref, o_ref, s_refs, m_ref, l_ref, acc_ref):
    group_id = pl.program_id(2)
    n_groups = pl.num_programs(2)
    head = pl.program_id(0) * pl.num_programs(1) + pl.program_id(1)
    n_slabs = tq // LANES

    m_ref[...] = jnp.full(m_ref.shape, NEG_BIG, F32)
    l_ref[...] = jnp.zeros(l_ref.shape, F32)
    acc_ref[...] = jnp.zeros(acc_ref.shape, F32)

    class Sub:
        def __init__(self, idx):
            self.rows = slice(idx * tq, (idx + 1) * tq)
            self.qi = n_sub * group_id + idx
            self.sa, self.sb = s_refs[2 * idx], s_refs[2 * idx + 1]
            q_start = pl.multiple_of(self.qi * tq, tq)
            self.c0 = cum_ref[0, :, pl.ds(q_start, LANES)][:, 0:1]
            self.first = first_ref[(head * n_groups + group_id) * n_sub + idx]
            self.n_off = self.qi - self.first

    def scores(sub, kb, s_ref):
        start = pl.multiple_of(kb * tq, tq)
        k = k_ref[pl.ds(start, tq), :]
        bias = (sub.c0 - cum_ref[0, :, pl.ds(start, tq)]) * LOG2E
        s_ref[...] = lax.dot_general(q_ref[sub.rows, :], k, (((1,), (1,)), ((), ())),
                                     preferred_element_type=F32) + bias

    def softmax_pv(sub, kb, s_ref, masked):
        start = pl.multiple_of(kb * tq, tq)
        if masked:
            row = lax.broadcasted_iota(jnp.int32, (LANES, LANES), 0)
            col = lax.broadcasted_iota(jnp.int32, (LANES, LANES), 1)
            for g in range(n_slabs):
                r0 = g * LANES
                slabs = [s_ref[r0:r0 + LANES, j * LANES:(j + 1) * LANES] for j in range(g + 1)]
                slabs[g] = jnp.where(col <= row, slabs[g], NEG_BIG)
                rs = slice(sub.rows.start + r0, sub.rows.start + r0 + LANES)
                update(rs, slabs, v_ref[pl.ds(start, (g + 1) * LANES), :])
        else:
            slabs = [s_ref[:, j * LANES:(j + 1) * LANES] for j in range(n_slabs)]
            update(sub.rows, slabs, v_ref[pl.ds(start, tq), :])

    def update(rs, slabs, v):
        mx = slabs[0]
        for t in slabs[1:]:
            mx = jnp.maximum(mx, t)
        m_prev = m_ref[rs, :]
        m_new = jnp.maximum(m_prev, jnp.max(mx, axis=-1, keepdims=True))
        alpha = jnp.exp2(m_prev - m_new)
        probs = [jnp.exp2(t - m_new) for t in slabs]
        psum = probs[0]
        for t in probs[1:]:
            psum = psum + t
        l_ref[rs, :] = alpha * l_ref[rs, :] + psum
        p = jnp.concatenate([t.astype(BF16) for t in probs], axis=-1)
        acc_ref[rs, :] = alpha * acc_ref[rs, :] + jnp.dot(p, v, preferred_element_type=F32)
        m_ref[rs, :] = m_new

    def sweep(sub, then):
        def pair(kb):
            scores(sub, kb + 1, sub.sb)
            softmax_pv(sub, kb, sub.sa, False)
            scores(sub, kb + 2, sub.sa)
            softmax_pv(sub, kb + 1, sub.sb, False)

        def body4(i, carry):
            pair(sub.first + 4 * i)
            pair(sub.first + 4 * i + 2)
            return carry

        def body2(i, carry):
            pair(sub.first + 2 * i)
            return carry

        n4 = sub.n_off // 4
        lax.fori_loop(0, n4, body4, 0)
        lax.fori_loop(2 * n4, sub.n_off // 2, body2, 0)

        @pl.when(sub.n_off % 2 == 0)
        def _():
            then()
            softmax_pv(sub, sub.qi, sub.sa, True)

        @pl.when(sub.n_off % 2 == 1)
        def _():
            scores(sub, sub.qi, sub.sb)
            softmax_pv(sub, sub.qi - 1, sub.sa, False)
            then()
            softmax_pv(sub, sub.qi, sub.sb, True)

    subs = [Sub(idx) for idx in range(n_sub)]
    scores(subs[0], subs[0].first, subs[0].sa)
    for sub, nxt in zip(subs, subs[1:] + [None]):
        sweep(sub, (lambda: None) if nxt is None else functools.partial(scores, nxt, nxt.first, nxt.sa))

    o_ref[...] = (acc_ref[...] / jnp.sum(l_ref[...], axis=-1, keepdims=True)).astype(BF16)


def _first_live_block(cum, tq, qk_bound):
    c0 = cum[:, 0, ::tq]
    cend = cum[:, 0, tq - 1::tq]
    gap = (c0[:, :, None] - cend[:, None, :]) * LOG2E + 2.0 * qk_bound
    nq = c0.shape[1]
    earlier = jnp.arange(nq)[None, :] < jnp.arange(nq)[:, None]
    return jnp.sum((gap < -UNDERFLOW_LOG2) & earlier[None], axis=-1).astype(jnp.int32).reshape(-1)


def _fox_attention(z, cum, qk_bound, bsz, seq, n_heads):
    n_sub = ATT_BLOCKS_PER_STEP
    tq = min(ATT_BLOCK, seq // n_sub)
    nq = seq // tq
    n_groups = nq // n_sub
    rows = n_sub * tq
    kern = functools.partial(_fox_kernel, tq, n_sub)
    grid_spec = pltpu.PrefetchScalarGridSpec(
        num_scalar_prefetch=1,
        grid=(bsz, n_heads, n_groups),
        in_specs=[pl.BlockSpec((rows, HEAD_DIM), lambda b, h, i, f: (b * n_groups + i, h)),
                  pl.BlockSpec((seq, HEAD_DIM), lambda b, h, i, f: (b, n_heads + h)),
                  pl.BlockSpec((seq, HEAD_DIM), lambda b, h, i, f: (b, 2 * n_heads + h)),
                  pl.BlockSpec((1, 1, seq), lambda b, h, i, f: (b * n_heads + h, 0, 0))],
        out_specs=pl.BlockSpec((rows, HEAD_DIM), lambda b, h, i, f: (b * n_groups + i, h)),
        scratch_shapes=[[pltpu.VMEM((tq, tq), F32)] * (2 * n_sub),
                        pltpu.VMEM((rows, LANES), F32), pltpu.VMEM((rows, LANES), F32),
                        pltpu.VMEM((rows, HEAD_DIM), F32)],
    )
    return pl.pallas_call(
        kern,
        grid_spec=grid_spec,
        out_shape=jax.ShapeDtypeStruct((bsz * seq, n_heads * HEAD_DIM), BF16),
        compiler_params=_params("arbitrary", "arbitrary", "arbitrary"),
    )(_first_live_block(cum, tq, qk_bound), z, z, z, cum)


def _ret_kernel(chunk, n_heads, lg_ref, q_ref, k_ref, v_ref, g_ref, nw_ref, o_ref, state_ref, decay_ref,
                qdec_ref, kdec_ref):
    first = (pl.program_id(0) == 0) & (pl.program_id(1) == 0)

    @pl.when(first)
    def _():
        i = lax.broadcasted_iota(jnp.int32, (chunk, chunk), 0)
        jj = lax.broadcasted_iota(jnp.int32, (chunk, chunk), 1)
        diff = (i - jj).astype(F32)
        pos = lax.broadcasted_iota(jnp.int32, (chunk, HEAD_DIM), 0).astype(F32)
        for h in range(n_heads):
            decay_ref[h] = jnp.where(diff >= 0, jnp.exp(lg_ref[h] * jnp.maximum(diff, 0.0)), 0.0)
            qdec_ref[h] = jnp.exp(lg_ref[h] * (pos + 1.0))
            kdec_ref[h] = jnp.exp(lg_ref[h] * (chunk - 1.0 - pos))

    @pl.when(pl.program_id(1) == 0)
    def _():
        state_ref[...] = jnp.zeros(state_ref.shape, F32)

    for h in range(n_heads):
        log_g = lg_ref[h]
        cols = slice(h * HEAD_DIM, (h + 1) * HEAD_DIM)
        q = q_ref[:, cols]
        k = k_ref[:, cols]
        v = v_ref[:, cols]
        scores = lax.dot_general(q, k, (((1,), (1,)), ((), ())), preferred_element_type=F32)
        scores = scores * decay_ref[h]
        intra = jnp.dot(scores.astype(BF16), v, preferred_element_type=F32)
        state = state_ref[h]
        inter = jnp.dot(q, state.astype(BF16), preferred_element_type=F32) * qdec_ref[h]
        kd = (k.astype(F32) * kdec_ref[h]).astype(BF16)
        kv = lax.dot_general(kd, v, (((0,), (0,)), ((), ())), preferred_element_type=F32)
        state_ref[h] = state * jnp.exp(jnp.full((1, HEAD_DIM), chunk, F32) * log_g) + kv
        o = intra + inter
        ms = jnp.mean(o * o, axis=-1, keepdims=True)
        o = o * lax.rsqrt(ms + EPS) * nw_ref[:, cols]
        o_ref[:, cols] = (o * _silu(g_ref[:, cols].astype(F32))).astype(BF16)


def _retention(z, log_g, norm_w, bsz, seq, n_heads, col0):
    chunk = min(RET_CHUNK, seq)
    nt = seq // chunk
    width = n_heads * HEAD_DIM
    c0 = col0 // width
    kern = functools.partial(_ret_kernel, chunk, n_heads)

    def sec(s):
        return pl.BlockSpec((chunk, width), lambda b, t, lg: (b * nt + t, c0 + s))

    grid_spec = pltpu.PrefetchScalarGridSpec(
        num_scalar_prefetch=1,
        grid=(bsz, nt),
        in_specs=[sec(0), sec(1), sec(2), sec(3), pl.BlockSpec((1, width), lambda b, t, lg: (0, 0))],
        out_specs=pl.BlockSpec((chunk, width), lambda b, t, lg: (b * nt + t, 0)),
        scratch_shapes=[pltpu.VMEM((n_heads, HEAD_DIM, HEAD_DIM), F32),
                        pltpu.VMEM((n_heads, chunk, chunk), F32),
                        pltpu.VMEM((n_heads, chunk, HEAD_DIM), F32),
                        pltpu.VMEM((n_heads, chunk, HEAD_DIM), F32)],
    )
    return pl.pallas_call(
        kern,
        grid_spec=grid_spec,
        out_shape=jax.ShapeDtypeStruct((bsz * seq, width), BF16),
        compiler_params=_params("arbitrary", "arbitrary"),
    )(log_g, z, z, z, z, norm_w)


def _outproj_kernel(oa_ref, ob_ref, wa_ref, wb_ref, x_ref, g1_ref, nw_ref, sc_ref, sh_ref, wr_ref, br_ref,
                    x1_ref, hp_ref, lg_ref):
    mix = jnp.dot(oa_ref[...], wa_ref[...], preferred_element_type=F32)
    mix = mix + jnp.dot(ob_ref[...], wb_ref[...], preferred_element_type=F32)
    x1 = x_ref[...] + g1_ref[0] * mix
    x1_ref[...] = x1
    ms = jnp.mean(x1 * x1, axis=-1, keepdims=True)
    h2 = x1 * lax.rsqrt(ms + EPS) * nw_ref[...] * (1.0 + sc_ref[0]) + sh_ref[0]
    hp_ref[...] = _rows_to_tiles(_pack_halves(h2))
    h_hi = h2.astype(BF16)
    h_lo = (h2 - h_hi.astype(F32)).astype(BF16)
    both = jnp.dot(h_hi, wr_ref[...], preferred_element_type=F32)
    cross = jnp.dot(h_lo, wr_ref[:, :LANES], preferred_element_type=F32)
    lg_ref[...] = both[:, :LANES] + both[:, LANES:] + cross + br_ref[...]


def _output_projection(o_a, o_b, w_out, x2d, seq, g1, norm_w, sc2, sh2, w_router, b_router):
    n, d = x2d.shape
    da = o_a.shape[1]
    tm = min(OUTPROJ_ROWS, seq)
    tiles_per_seq = seq // tm
    bsel = lambda i: (i // tiles_per_seq, 0, 0)
    return pl.pallas_call(
        _outproj_kernel,
        grid=(n // tm,),
        in_specs=[pl.BlockSpec((tm, da), lambda i: (i, 0)),
                  pl.BlockSpec((tm, da), lambda i: (i, 0)),
                  pl.BlockSpec((da, d), lambda i: (0, 0)),
                  pl.BlockSpec((da, d), lambda i: (1, 0)),
                  pl.BlockSpec((tm, d), lambda i: (i, 0)),
                  pl.BlockSpec((1, 1, d), bsel),
                  pl.BlockSpec((1, d), lambda i: (0, 0)),
                  pl.BlockSpec((1, 1, d), bsel),
                  pl.BlockSpec((1, 1, d), bsel),
                  pl.BlockSpec((d, 2 * LANES), lambda i: (0, 0)),
                  pl.BlockSpec((1, LANES), lambda i: (0, 0))],
        out_specs=[pl.BlockSpec((tm, d), lambda i: (i, 0)),
                   pl.BlockSpec((tm, d // 2 // LANES, LANES), lambda i: (i, 0, 0)),
                   pl.BlockSpec((tm, LANES), lambda i: (i, 0))],
        out_shape=[jax.ShapeDtypeStruct((n, d), F32),
                   jax.ShapeDtypeStruct((n, d // 2 // LANES, LANES), U32),
                   jax.ShapeDtypeStruct((n, LANES), F32)],
        compiler_params=_params("arbitrary"),
    )(o_a, o_b, w_out, w_out, x2d, g1, norm_w, sc2, sh2, w_router, b_router)


def _route_kernel(blk, n_blocks, lg_ref, gate_ref, ids_ref, plan_ref, run_ref):
    i = pl.program_id(0)

    @pl.when(i == 0)
    def _():
        run_ref[...] = jnp.zeros(run_ref.shape, F32)

    lg = lg_ref[...]
    tt = lg.shape[0]
    lane = lax.broadcasted_iota(jnp.int32, lg.shape, 1).astype(F32)
    big = 1e6

    def rmax(v):
        return jnp.max(v, axis=-1, keepdims=True)

    def rmin(v):
        return jnp.min(v, axis=-1, keepdims=True)

    def rsum(v):
        return jnp.sum(v, axis=-1, keepdims=True)

    cmask = lane < N_GROUPS
    cm = jnp.where(cmask, lg, NEG_BIG)
    ce = jnp.where(cmask, jnp.exp(cm - rmax(cm)), 0.0)
    pgrp = ce / rsum(ce)
    p_g = rmax(pgrp)
    g_sel = rmin(jnp.where(cmask & (pgrp == p_g), lane, big))

    lo = N_GROUPS + EXPERTS_PER_GROUP * g_sel
    fmask = (lane >= lo) & (lane < lo + EXPERTS_PER_GROUP)
    fm = jnp.where(fmask, lg, NEG_BIG)
    fe = jnp.where(fmask, jnp.exp(fm - rmax(fm)), 0.0)
    fp = fe / rsum(fe)
    fp = jnp.where(fmask, fp, -1.0)
    p1 = rmax(fp)
    i1 = rmin(jnp.where(fp == p1, lane, big))
    fp2 = jnp.where(lane == i1, -1.0, fp)
    p2 = rmax(fp2)
    i2 = rmin(jnp.where(fp2 == p2, lane, big))
    denom = p1 + p2
    w1 = p_g * p1 / denom
    w2 = p_g * p2 / denom
    e1 = i1 - N_GROUPS
    e2 = i2 - N_GROUPS

    gate_ref[...] = jnp.where(lane == 0, w1, jnp.where(lane == 1, w2, 0.0))

    oh1 = (lane == e1).astype(F32)
    oh2 = (lane == e2).astype(F32)
    both = oh1 + oh2
    ra = lax.broadcasted_iota(jnp.int32, (tt, tt), 0)
    rb = lax.broadcasted_iota(jnp.int32, (tt, tt), 1)
    strict = (rb < ra).astype(BF16)
    prefix = jnp.dot(strict, both.astype(BF16), preferred_element_type=F32) + run_ref[...]
    r1 = rsum(prefix * oh1)
    r2 = rsum(prefix * oh2)
    run_ref[...] = run_ref[...] + jnp.sum(both, axis=0, keepdims=True)

    packed = jnp.where(lane == 0, e1, jnp.where(lane == 1, e2, jnp.where(lane == 2, r1,
                                                                        jnp.where(lane == 3, r2, 0.0))))
    ids_ref[...] = jnp.transpose(packed)[:8, :].astype(jnp.int32)

    @pl.when(i == pl.num_programs(0) - 1)
    def _():
        cnt = jnp.broadcast_to(run_ref[...], (8, LANES))
        lane8 = lax.broadcasted_iota(jnp.int32, (8, LANES), 1)
        padded = jnp.floor((cnt + (blk - 1.0)) * (1.0 / blk)) * blk
        pend = padded
        for sh in (1, 2, 4, 8, 16, 32, 64):
            pend = pend + jnp.where(lane8 >= sh, pltpu.roll(pend, sh, 1), 0.0)
        pstart = pend - padded
        total = jnp.max(pend, axis=-1, keepdims=True)
        tail = total + (lane8 - N_EXPERTS).astype(F32) * blk
        fill = jnp.where(lane8 < N_EXPERTS, jnp.where(padded > 0, pend - blk, -1.0),
                         jnp.where((lane8 < 2 * N_EXPERTS) & (tail < n_blocks * blk), tail, -1.0))
        row8 = lax.broadcasted_iota(jnp.int32, (8, LANES), 0)
        plan_ref[...] = jnp.where(row8 == 0, pstart, jnp.where(row8 == 1, fill,
                                                               jnp.where(row8 == 2, cnt, 0.0))).astype(jnp.int32)


def _route(logits, blk, n_blocks):
    n = logits.shape[0]
    tt = min(ROUTE_ROWS, n)
    blkspec = lambda: pl.BlockSpec((tt, LANES), lambda i: (i, 0))
    return pl.pallas_call(
        functools.partial(_route_kernel, blk, n_blocks),
        grid=(n // tt,),
        in_specs=[blkspec()],
        out_specs=[blkspec(),
                   pl.BlockSpec((8, tt), lambda i: (0, i)),
                   pl.BlockSpec((8, LANES), lambda i: (0, 0))],
        out_shape=[jax.ShapeDtypeStruct((n, LANES), F32),
                   jax.ShapeDtypeStruct((8, n), jnp.int32),
                   jax.ShapeDtypeStruct((8, LANES), jnp.int32)],
        scratch_shapes=[pltpu.VMEM((1, LANES), F32)],
        compiler_params=_params("arbitrary"),
    )(logits)


def _dispatch_kernel(tt, blk, n_fill, dest_ref, fill_ref, h_ref, xs_ref, zero_ref, sem, zsem):
    i = pl.program_id(0)
    base = i * (tt * TOP_K)

    @pl.when(i == 0)
    def _():
        zero_ref[...] = jnp.zeros(zero_ref.shape, U32)

        def zcopy(z):
            row = pl.multiple_of(jnp.maximum(fill_ref[z], 0), blk)
            return pltpu.make_async_copy(zero_ref, xs_ref.at[pl.ds(row, blk)], zsem)

        def zissue(z, carry):
            @pl.when(fill_ref[z] >= 0)
            def _():
                zcopy(z).start()
            return carry

        def zdrain(z, carry):
            @pl.when(fill_ref[z] >= 0)
            def _():
                zcopy(z).wait()
            return carry

        lax.fori_loop(0, n_fill, zissue, 0)
        lax.fori_loop(0, n_fill, zdrain, 0)

    def copy(r, kk):
        d = dest_ref[base + r * TOP_K + kk]
        return pltpu.make_async_copy(h_ref.at[r], xs_ref.at[d], sem)

    def issue(r, carry):
        for kk in range(TOP_K):
            copy(r, kk).start(priority=kk % 2)
        return carry

    lax.fori_loop(0, tt, issue, 0, unroll=8)
    for _ in range(TOP_K):
        pltpu.make_async_copy(h_ref, xs_ref.at[pl.ds(0, tt)], sem).wait()


def _dispatch(h_packed, dest_flat, fill_rows, n_slots, blk):
    n = h_packed.shape[0]
    tile = h_packed.shape[1:]
    tt = min(DISPATCH_TOKENS, n)
    n_fill = fill_rows.shape[0]
    grid_spec = pltpu.PrefetchScalarGridSpec(
        num_scalar_prefetch=2,
        grid=(n // tt,),
        in_specs=[pl.BlockSpec((tt,) + tile, lambda i, d, f: (i, 0, 0))],
        out_specs=pl.BlockSpec(memory_space=pl.ANY),
        scratch_shapes=[pltpu.VMEM((blk,) + tile, U32), pltpu.SemaphoreType.DMA(()), pltpu.SemaphoreType.DMA(())],
    )
    return pl.pallas_call(
        functools.partial(_dispatch_kernel, tt, blk, n_fill),
        grid_spec=grid_spec,
        out_shape=jax.ShapeDtypeStruct((n_slots,) + tile, U32),
        compiler_params=_params("arbitrary"),
    )(dest_flat, fill_rows, h_packed)


def _expert_kernel(blk, ahead, cnt_ref, pstart_ref, fill_ref, xs_ref, w1_ref, w3_ref, w2_ref, y_ref,
                   w1f, w3f, w2f, w1b, w3b, w2b, xbuf, ybuf, done_ref, w_sem, in_sem, out_sem):
    e = pl.program_id(0)
    n_exp = pl.num_programs(0)
    wslot = e % 2
    n_blk = (cnt_ref[e] + (blk - 1)) // blk
    base = pstart_ref[e]
    n_x = xbuf.shape[0]

    def weight_copies(ex, slot):
        return [pltpu.make_async_copy(src.at[ex], dst.at[slot], w_sem.at[slot])
                for src, dst in ((w1_ref, w1f), (w3_ref, w3f), (w2_ref, w2f))]

    def rows(b):
        return pl.ds(pl.multiple_of(base + b * blk, blk), blk)

    def in_copy(b, slot):
        return pltpu.make_async_copy(xs_ref.at[rows(b)], xbuf.at[slot], in_sem.at[slot])

    def out_copy(b, slot):
        return pltpu.make_async_copy(ybuf.at[slot], y_ref.at[rows(b)], out_sem.at[slot])

    @pl.when(e == 0)
    def _():
        for c in weight_copies(0, 0):
            c.start()

    for p in range(ahead):
        @pl.when(p < n_blk)
        def _():
            in_copy(p, p).start()

    @pl.when(e + 1 < n_exp)
    def _():
        for c in weight_copies(e + 1, 1 - wslot):
            c.start()

    for c in weight_copies(e, wslot):
        c.wait()
    w1b[...] = w1f[wslot].astype(BF16)
    w3b[...] = w3f[wslot].astype(BF16)
    w2b[...] = w2f[wslot].astype(BF16)

    @pl.when(e == 0)
    def _():
        done_ref[0] = 0

    done = done_ref[0]

    def out_wait(slot):
        pltpu.make_async_copy(ybuf.at[slot], y_ref.at[pl.ds(0, blk)], out_sem.at[slot]).wait()

    def body(b, carry):
        slot = (done + b) % 2

        @pl.when(b + ahead < n_blk)
        def _():
            in_copy(b + ahead, (b + ahead) % n_x).start()

        in_copy(b, b % n_x).wait()

        @pl.when(done + b >= 2)
        def _():
            out_wait(slot)

        lo, hi = _unpack_halves(_tiles_to_rows(xbuf[b % n_x]))
        lo = lo.astype(BF16)
        hi = hi.astype(BF16)
        half = lo.shape[1]
        a = jnp.dot(lo, w1b[:half, :], preferred_element_type=F32)
        a = a + jnp.dot(hi, w1b[half:, :], preferred_element_type=F32)
        g = jnp.dot(lo, w3b[:half, :], preferred_element_type=F32)
        g = g + jnp.dot(hi, w3b[half:, :], preferred_element_type=F32)
        mid = (_silu(a) * g).astype(BF16)
        ybuf[slot] = _rows_to_tiles(_pack_halves(jnp.dot(mid, w2b[...], preferred_element_type=F32)))
        out_copy(b, slot).start(priority=1)
        return carry

    lax.fori_loop(0, n_blk, body, 0)
    total = done + n_blk
    done_ref[0] = total

    @pl.when(e == n_exp - 1)
    def _():
        @pl.when(total >= 2)
        def _():
            out_wait(total % 2)

        @pl.when(total >= 1)
        def _():
            out_wait((total - 1) % 2)

        ybuf[0] = jnp.zeros(ybuf.shape[1:], U32)

        def zcopy(t):
            row = pl.multiple_of(jnp.maximum(fill_ref[N_EXPERTS + t], 0), blk)
            return pltpu.make_async_copy(ybuf.at[0], y_ref.at[pl.ds(row, blk)], out_sem.at[0])

        def zissue(t, carry):
            @pl.when(fill_ref[N_EXPERTS + t] >= 0)
            def _():
                zcopy(t).start()
            return carry

        def zdrain(t, carry):
            @pl.when(fill_ref[N_EXPERTS + t] >= 0)
            def _():
                zcopy(t).wait()
            return carry

        lax.fori_loop(0, N_EXPERTS, zissue, 0)
        lax.fori_loop(0, N_EXPERTS, zdrain, 0)


def _expert_blocks(xs, counts, pstart, fill_rows, w1, w3, w2, blk):
    n_slots = xs.shape[0]
    tile = xs.shape[1:]
    n_exp, d, de = w1.shape
    ahead = EXPERT_AHEAD
    hbm = pl.BlockSpec(memory_space=pl.ANY)
    grid_spec = pltpu.PrefetchScalarGridSpec(
        num_scalar_prefetch=3,
        grid=(n_exp,),
        in_specs=[hbm, hbm, hbm, hbm],
        out_specs=hbm,
        scratch_shapes=[pltpu.VMEM((2, d, de), F32), pltpu.VMEM((2, d, de), F32), pltpu.VMEM((2, de, d), F32),
                        pltpu.VMEM((d, de), BF16), pltpu.VMEM((d, de), BF16), pltpu.VMEM((de, d), BF16),
                        pltpu.VMEM((ahead + 1, blk) + tile, U32), pltpu.VMEM((2, blk) + tile, U32),
                        pltpu.SMEM((1,), jnp.int32),
                        pltpu.SemaphoreType.DMA((2,)), pltpu.SemaphoreType.DMA((ahead + 1,)),
                        pltpu.SemaphoreType.DMA((2,))],
    )
    return pl.pallas_call(
        functools.partial(_expert_kernel, blk, ahead),
        grid_spec=grid_spec,
        out_shape=jax.ShapeDtypeStruct((n_slots,) + tile, U32),
        compiler_params=_params("arbitrary"),
    )(counts, pstart, fill_rows, xs, w1, w3, w2)


def _combine_kernel(tt, n_tiles, dest_ref, x1_ref, g2_ref, gate_ref, yb_ref, o_ref, buf, sems):
    i = pl.program_id(0)

    def copy(tile, slot, r, kk):
        d = dest_ref[(tile * tt + r) * TOP_K + kk]
        return pltpu.make_async_copy(yb_ref.at[d], buf.at[slot, kk, r], sems.at[slot])

    def issue_tile(tile, slot):
        def body(r, carry):
            for kk in range(TOP_K):
                copy(tile, slot, r, kk).start(priority=kk % 2)
            return carry
        lax.fori_loop(0, tt, body, 0, unroll=8)

    def wait_tile(tile, slot):
        for kk in range(TOP_K):
            pltpu.make_async_copy(yb_ref.at[pl.ds(0, tt)], buf.at[slot, kk], sems.at[slot]).wait()

    slot = i % 2

    @pl.when(i == 0)
    def _():
        issue_tile(0, 0)

    @pl.when(i + 1 < n_tiles)
    def _():
        issue_tile(i + 1, 1 - slot)

    wait_tile(i, slot)

    gate = gate_ref[...]
    wa = gate[:, 0:1]
    wb = gate[:, 1:2]
    lo_a, hi_a = _unpack_halves(_tiles_to_rows(buf[slot, 0]))
    lo_b, hi_b = _unpack_halves(_tiles_to_rows(buf[slot, 1]))
    y = jnp.concatenate([wa * lo_a + wb * lo_b, wa * hi_a + wb * hi_b], axis=-1)
    o_ref[...] = x1_ref[...] + g2_ref[0] * y


def _combine(x1, seq, g2, gates, dest_flat, yb):
    n, d = x1.shape
    tile = yb.shape[1:]
    tt = min(COMBINE_TOKENS, seq)
    n_tiles = n // tt
    tiles_per_seq = seq // tt
    grid_spec = pltpu.PrefetchScalarGridSpec(
        num_scalar_prefetch=1,
        grid=(n_tiles,),
        in_specs=[pl.BlockSpec((tt, d), lambda i, dr: (i, 0)),
                  pl.BlockSpec((1, 1, d), lambda i, dr: (i // tiles_per_seq, 0, 0)),
                  pl.BlockSpec((tt, LANES), lambda i, dr: (i, 0)),
                  pl.BlockSpec(memory_space=pl.ANY)],
        out_specs=pl.BlockSpec((tt, d), lambda i, dr: (i, 0)),
        scratch_shapes=[pltpu.VMEM((2, TOP_K, tt) + tile, U32), pltpu.SemaphoreType.DMA((2,))],
    )
    return pl.pallas_call(
        functools.partial(_combine_kernel, tt, n_tiles),
        grid_spec=grid_spec,
        out_shape=jax.ShapeDtypeStruct((n, d), F32),
        compiler_params=_params("arbitrary"),
    )(dest_flat, x1, g2, gates, yb)


def _rotation_tables(seq):
    half = HEAD_DIM // 2
    theta = ROPE_BASE ** (-np.arange(half, dtype=np.float64) / half)
    ang = np.arange(seq, dtype=np.float64)[:, None] * theta[None, :]
    cos_t = np.concatenate([np.cos(ang), np.cos(ang)], axis=-1).astype(np.float32)
    sin_t = np.concatenate([-np.sin(ang), np.sin(ang)], axis=-1).astype(np.float32)
    return jnp.asarray(cos_t), jnp.asarray(sin_t)


def _layer(x, c, w_ada, b_ada, norm1_w, w_in, forget_bias, q_norm_w, k_norm_w, ret_norm_w, w_out, norm2_w,
           w_coarse, b_coarse, w_fine, b_fine, w1, w3, w2):
    bsz, seq, d = x.shape
    n = bsz * seq
    d_fox = d // 2
    d_ret = d // 2
    n_heads = d_fox // HEAD_DIM

    mod = _ada_modulation(c, w_ada, b_ada)
    sh1, sc1, g1, sh2, sc2, g2 = [m.reshape(bsz, 1, d) for m in jnp.split(mod, 6, axis=-1)]

    f0 = 3 * d_fox
    w_fox = w_in[:, :f0].astype(BF16)
    w_ret = w_in[:, f0 + n_heads:].astype(BF16)
    w_f = jnp.zeros((d, LANES), BF16).at[:, :n_heads].set(w_in[:, f0:f0 + n_heads].astype(BF16))
    fb = jnp.zeros((1, LANES), F32).at[0, :n_heads].set(forget_bias)

    cos_t, sin_t = _rotation_tables(seq)

    x2d = x.reshape(n, d)
    z, log_f = _input_projection(x2d, seq, norm1_w.reshape(1, d), sc1, sh1, w_fox, w_ret, w_f, cos_t, sin_t,
                                 q_norm_w.reshape(1, HEAD_DIM), k_norm_w.reshape(1, HEAD_DIM), fb)

    lf = log_f[:, :n_heads].reshape(bsz, seq, n_heads).transpose(0, 2, 1).reshape(bsz * n_heads, seq)
    cum = _cumsum_rows(lf)

    qk_bound = 1.02 * LOG2E * HEAD_DIM ** 0.5 * jnp.max(jnp.abs(q_norm_w)) * jnp.max(jnp.abs(k_norm_w))
    o_a = _fox_attention(z, cum, qk_bound, bsz, seq, n_heads)
    log_g = jnp.log(1.0 - 2.0 ** (-5.0 - jnp.arange(n_heads, dtype=F32)))
    o_b = _retention(z, log_g, ret_norm_w.reshape(1, d_ret), bsz, seq, n_heads, 3 * d_fox)

    w_router = jnp.zeros((d, LANES), F32)
    w_router = w_router.at[:, :N_GROUPS].set(w_coarse)
    w_router = w_router.at[:, N_GROUPS:N_GROUPS + N_EXPERTS].set(
        w_fine.transpose(1, 0, 2).reshape(d, N_EXPERTS))
    b_router = jnp.zeros((1, LANES), F32)
    b_router = b_router.at[0, :N_GROUPS].set(b_coarse)
    b_router = b_router.at[0, N_GROUPS:N_GROUPS + N_EXPERTS].set(b_fine.reshape(N_EXPERTS))

    wr_hi = w_router.astype(BF16)
    wr_lo = (w_router - wr_hi.astype(F32)).astype(BF16)
    x1, h_packed, logits = _output_projection(o_a, o_b, w_out.astype(BF16), x2d, seq, g1,
                                              norm2_w.reshape(1, d), sc2, sh2,
                                              jnp.concatenate([wr_hi, wr_lo], axis=1), b_router)

    blk = EXPERT_BLOCK
    nk = n * TOP_K
    n_blocks = nk // blk + N_EXPERTS
    gates, ids, plan = _route(logits, blk, n_blocks)
    pstart = plan[0, :N_EXPERTS]
    fill_rows = plan[1, :2 * N_EXPERTS]
    counts = plan[2, :N_EXPERTS]
    eid = ids[0:TOP_K]
    hit = eid[None] == jnp.arange(N_EXPERTS, dtype=jnp.int32)[:, None, None]
    dest = (jnp.sum(jnp.where(hit, pstart[:, None, None], 0), axis=0) + ids[TOP_K:2 * TOP_K]).T.reshape(nk)

    xs = _dispatch(h_packed, dest, fill_rows, n_blocks * blk, blk)
    yb = _expert_blocks(xs, counts, pstart, fill_rows, w1, w3, w2, blk)
    out = _combine(x1, seq, g2, gates, dest, yb)
    return out.reshape(bsz, seq, d)


def kernel(x, c, w_ada, b_ada, norm1_w, w_in, forget_bias, q_norm_w, k_norm_w, ret_norm_w, w_out, norm2_w,
           w_coarse, b_coarse, w_fine, b_fine, w1, w3, w2):
    c_in = c
    for l in range(w_ada.shape[0]):
        x = _layer(x, c_in, w_ada[l], b_ada[l], norm1_w[l], w_in[l], forget_bias[l], q_norm_w[l],
                   k_norm_w[l], ret_norm_w[l], w_out[l], norm2_w[l], w_coarse[l], b_coarse[l],
                   w_fine[l], b_fine[l], w1[l], w3[l], w2[l])
    return x
```

```python
import functools

import jax
import jax.numpy as jnp
import numpy as np
from jax import lax
from jax.experimental import pallas as pl
from jax.experimental.pallas import tpu as pltpu

HEAD_DIM = 128
N_GROUPS = 4
EXPERTS_PER_GROUP = 8
N_EXPERTS = N_GROUPS * EXPERTS_PER_GROUP
TOP_K = 2
ROPE_BASE = 10000.0
EPS = 1e-6

LANES = 128
VMEM_LIMIT = 56 * 1024 * 1024
NEG_BIG = -1e30
LOG2E = 1.4426950408889634
UNDERFLOW_LOG2 = 160.0

ADA_COLS = 1024
INPROJ_ROWS = 1024
INPROJ_COLS = 1024
ATT_BLOCK = 512
ATT_BLOCKS_PER_STEP = 4
RET_CHUNK = 256
OUTPROJ_ROWS = 512
ROUTE_ROWS = 512
DISPATCH_TOKENS = 1024
COMBINE_TOKENS = 512
EXPERT_BLOCK = 256
EXPERT_AHEAD = 3

F32 = jnp.float32
BF16 = jnp.bfloat16
U32 = jnp.uint32


def _params(*sem):
    return pltpu.CompilerParams(dimension_semantics=sem, vmem_limit_bytes=VMEM_LIMIT)


def _silu(v):
    return v * (1.0 / (1.0 + jnp.exp(-v)))


def _pack_halves(y):
    w = y.shape[1] // 2
    lo = pltpu.bitcast(y[:, :w].astype(BF16).astype(F32), U32)
    hi = pltpu.bitcast(y[:, w:].astype(BF16).astype(F32), U32)
    return (hi & jnp.uint32(0xFFFF0000)) | (lo >> 16)


def _rows_to_tiles(p):
    return pltpu.einshape("m(ck)->mck", p, c=8, k=LANES)


def _tiles_to_rows(t):
    return pltpu.einshape("mck->m(ck)", t)


def _unpack_halves(p):
    lo = pltpu.bitcast(p << 16, F32)
    hi = pltpu.bitcast(p & jnp.uint32(0xFFFF0000), F32)
    return lo, hi


def _ada_kernel(bsz, ct_ref, w_ref, b_ref, o_ref):
    w = w_ref[...]
    rows = []
    for b in range(o_ref.shape[0]):
        if b < bsz:
            cb = _silu(ct_ref[:, b:b + 1])
            rows.append(jnp.sum(cb * w, axis=0, keepdims=True) + b_ref[...])
        else:
            rows.append(jnp.zeros_like(b_ref[...]))
    o_ref[...] = jnp.concatenate(rows, axis=0)


def _ada_modulation(c, w_ada, b_ada):
    bsz, d = c.shape
    n = w_ada.shape[1]
    tn = ADA_COLS
    ct = jnp.zeros((d, LANES), F32).at[:, :bsz].set(c.T)
    assert bsz <= 8, "one sublane tile of modulation rows"
    out = pl.pallas_call(
        functools.partial(_ada_kernel, bsz),
        grid=(n // tn,),
        in_specs=[pl.BlockSpec((d, LANES), lambda j: (0, 0)),
                  pl.BlockSpec((d, tn), lambda j: (0, j)),
                  pl.BlockSpec((1, tn), lambda j: (0, j))],
        out_specs=pl.BlockSpec((8, tn), lambda j: (0, j)),
        out_shape=jax.ShapeDtypeStruct((8, n), F32),
        compiler_params=_params("arbitrary"),
    )(ct, w_ada, b_ada.reshape(1, n))
    return out[:bsz]


def _inproj_kernel(q_t, r_t, x_ref, nw_ref, sc_ref, sh_ref, wa_ref, wb_ref, wf_ref, cos_ref, sin_ref,
                   qw_ref, kw_ref, fb_ref, z_ref, f_ref, h_ref):
    j = pl.program_id(1)
    r0 = 3 * q_t

    @pl.when(j == 0)
    def _():
        x = x_ref[...]
        ms = jnp.mean(x * x, axis=-1, keepdims=True)
        y = x * lax.rsqrt(ms + EPS) * nw_ref[...]
        h = (y * (1.0 + sc_ref[0]) + sh_ref[0]).astype(BF16)
        h_ref[...] = h
        t = jnp.dot(h, wf_ref[...], preferred_element_type=F32) + fb_ref[...]
        f_ref[...] = jnp.minimum(t, 0.0) - jnp.log(1.0 + jnp.exp(-jnp.abs(t)))

    def heads_of(acc):
        return [acc[:, hh * HEAD_DIM:(hh + 1) * HEAD_DIM] for hh in range(acc.shape[1] // HEAD_DIM)]

    def head_norm(acc, w_row):
        outs = []
        for a in heads_of(acc):
            ms = jnp.mean(a * a, axis=-1, keepdims=True)
            outs.append(a * lax.rsqrt(ms + EPS) * w_row)
        return jnp.concatenate(outs, axis=-1).astype(BF16)

    def rotate(acc, scale):
        cs = cos_ref[...] * scale
        sn = sin_ref[...] * scale
        outs = [a * cs + pltpu.roll(a, HEAD_DIM // 2, 1) * sn for a in heads_of(acc)]
        return jnp.concatenate(outs, axis=-1).astype(BF16)

    def fox():
        return jnp.dot(h_ref[...], wa_ref[...], preferred_element_type=F32)

    def ret():
        return jnp.dot(h_ref[...], wb_ref[...], preferred_element_type=F32)

    @pl.when(j < q_t)
    def _():
        z_ref[...] = head_norm(fox(), qw_ref[...] * (LOG2E * HEAD_DIM ** -0.5))

    @pl.when((j >= q_t) & (j < 2 * q_t))
    def _():
        z_ref[...] = head_norm(fox(), kw_ref[...])

    @pl.when((j >= 2 * q_t) & (j < r0))
    def _():
        z_ref[...] = fox().astype(BF16)

    @pl.when((j >= r0) & (j < r0 + r_t))
    def _():
        z_ref[...] = rotate(ret(), 1.0)

    @pl.when((j >= r0 + r_t) & (j < r0 + 2 * r_t))
    def _():
        z_ref[...] = rotate(ret(), HEAD_DIM ** -0.5)

    @pl.when(j >= r0 + 2 * r_t)
    def _():
        z_ref[...] = ret().astype(BF16)


def _input_projection(x2d, seq, norm_w, sc1, sh1, w_fox, w_ret, w_f, cos_t, sin_t, qw, kw, fb):
    n, d = x2d.shape
    tm, tn = min(INPROJ_ROWS, seq), INPROJ_COLS
    fox_tiles = w_fox.shape[1] // tn
    ret_tiles = w_ret.shape[1] // tn
    tiles_per_seq = seq // tm
    kern = functools.partial(_inproj_kernel, fox_tiles // 3, ret_tiles // 4)
    bsel = lambda i, j: (i // tiles_per_seq, 0, 0)
    const = lambda i, j: (0, 0)
    return pl.pallas_call(
        kern,
        grid=(n // tm, fox_tiles + ret_tiles),
        in_specs=[pl.BlockSpec((tm, d), lambda i, j: (i, 0)),
                  pl.BlockSpec((1, d), const),
                  pl.BlockSpec((1, 1, d), bsel),
                  pl.BlockSpec((1, 1, d), bsel),
                  pl.BlockSpec((d, tn), lambda i, j: (0, jnp.minimum(j, fox_tiles - 1))),
                  pl.BlockSpec((d, tn), lambda i, j: (0, jnp.maximum(j - fox_tiles, 0))),
                  pl.BlockSpec((d, LANES), const),
                  pl.BlockSpec((tm, HEAD_DIM), lambda i, j: (i % tiles_per_seq, 0)),
                  pl.BlockSpec((tm, HEAD_DIM), lambda i, j: (i % tiles_per_seq, 0)),
                  pl.BlockSpec((1, HEAD_DIM), const),
                  pl.BlockSpec((1, HEAD_DIM), const),
                  pl.BlockSpec((1, LANES), const)],
        out_specs=[pl.BlockSpec((tm, tn), lambda i, j: (i, j)),
                   pl.BlockSpec((tm, LANES), lambda i, j: (i, 0))],
        out_shape=[jax.ShapeDtypeStruct((n, w_fox.shape[1] + w_ret.shape[1]), BF16),
                   jax.ShapeDtypeStruct((n, LANES), F32)],
        scratch_shapes=[pltpu.VMEM((tm, d), BF16)],
        compiler_params=_params("arbitrary", "arbitrary"),
    )(x2d, norm_w, sc1, sh1, w_fox, w_ret, w_f, cos_t, sin_t, qw, kw, fb)


def _cumsum_kernel(x_ref, o_ref):
    x = x_ref[0]
    r = x.shape[0]
    a = lax.broadcasted_iota(jnp.int32, (LANES, LANES), 0)
    b = lax.broadcasted_iota(jnp.int32, (LANES, LANES), 1)
    upper = (a <= b).astype(F32)
    within = jnp.dot(x, upper, precision=lax.Precision.HIGHEST, preferred_element_type=F32)
    tot = jnp.broadcast_to(within[:, LANES - 1:LANES], (r, LANES))
    ra = lax.broadcasted_iota(jnp.int32, (r, r), 0)
    rb = lax.broadcasted_iota(jnp.int32, (r, r), 1)
    strict = (rb < ra).astype(F32)
    before = jnp.dot(strict, tot, precision=lax.Precision.HIGHEST, preferred_element_type=F32)
    o_ref[0] = within + before


def _cumsum_rows(x):
    g, s = x.shape
    r = s // LANES
    out = pl.pallas_call(
        _cumsum_kernel,
        grid=(g,),
        in_specs=[pl.BlockSpec((1, r, LANES), lambda i: (i, 0, 0))],
        out_specs=pl.BlockSpec((1, r, LANES), lambda i: (i, 0, 0)),
        out_shape=jax.ShapeDtypeStruct((g, r, LANES), F32),
        compiler_params=_params("arbitrary"),
    )(x.reshape(g, r, LANES))
    return out.reshape(g, 1, s)


def _fox_kernel(tq, n_sub, first_ref, q_ref, k_ref, v_ref, cum_ref, o_ref, s_refs, m_ref, l_ref, acc_ref):
    group_id = pl.program_id(2)
    n_groups = pl.num_programs(2)
    head = pl.program_id(0) * pl.num_programs(1) + pl.program_id(1)
    n_slabs = tq // LANES

    m_ref[...] = jnp.full(m_ref.shape, NEG_BIG, F32)
    l_ref[...] = jnp.zeros(l_ref.shape, F32)
    acc_ref[...] = jnp.zeros(acc_ref.shape, F32)

    class Sub:
        def __init__(self, idx):
            self.rows = slice(idx * tq, (idx + 1) * tq)
            self.qi = n_sub * group_id + idx
            self.sa, self.sb = s_refs[2 * idx], s_refs[2 * idx + 1]
            q_start = pl.multiple_of(self.qi * tq, tq)
            self.c0 = cum_ref[0, :, pl.ds(q_start, LANES)][:, 0:1]
            self.first = first_ref[(head * n_groups + group_id) * n_sub + idx]
            self.n_off = self.qi - self.first

    def scores(sub, kb, s_ref):
        start = pl.multiple_of(kb * tq, tq)
        k = k_ref[pl.ds(start, tq), :]
        bias = (sub.c0 - cum_ref[0, :, pl.ds(start, tq)]) * LOG2E
        s_ref[...] = lax.dot_general(q_ref[sub.rows, :], k, (((1,), (1,)), ((), ())),
                                     preferred_element_type=F32) + bias

    def softmax_pv(sub, kb, s_ref, masked):
        start = pl.multiple_of(kb * tq, tq)
        if masked:
            row = lax.broadcasted_iota(jnp.int32, (LANES, LANES), 0)
            col = lax.broadcasted_iota(jnp.int32, (LANES, LANES), 1)
            for g in range(n_slabs):
                r0 = g * LANES
                slabs = [s_ref[r0:r0 + LANES, j * LANES:(j + 1) * LANES] for j in range(g + 1)]
                slabs[g] = jnp.where(col <= row, slabs[g], NEG_BIG)
                rs = slice(sub.rows.start + r0, sub.rows.start + r0 + LANES)
                update(rs, slabs, v_ref[pl.ds(start, (g + 1) * LANES), :])
        else:
            slabs = [s_ref[:, j * LANES:(j + 1) * LANES] for j in range(n_slabs)]
            update(sub.rows, slabs, v_ref[pl.ds(start, tq), :])

    def update(rs, slabs, v):
        mx = slabs[0]
        for t in slabs[1:]:
            mx = jnp.maximum(mx, t)
        m_prev = m_ref[rs, :]
        m_new = jnp.maximum(m_prev, jnp.max(mx, axis=-1, keepdims=True))
        alpha = jnp.exp2(m_prev - m_new)
        probs = [jnp.exp2(t - m_new) for t in slabs]
        psum = probs[0]
        for t in probs[1:]:
            psum = psum + t
        l_ref[rs, :] = alpha * l_ref[rs, :] + psum
        p = jnp.concatenate([t.astype(BF16) for t in probs], axis=-1)
        acc_ref[rs, :] = alpha * acc_ref[rs, :] + jnp.dot(p, v, preferred_element_type=F32)
        m_ref[rs, :] = m_new

    def sweep(sub, then):
        def pair(kb):
            scores(sub, kb + 1, sub.sb)
            softmax_pv(sub, kb, sub.sa, False)
            scores(sub, kb + 2, sub.sa)
            softmax_pv(sub, kb + 1, sub.sb, False)

        def body4(i, carry):
            pair(sub.first + 4 * i)
            pair(sub.first + 4 * i + 2)
            return carry

        def body2(i, carry):
            pair(sub.first + 2 * i)
            return carry

        n4 = sub.n_off // 4
        lax.fori_loop(0, n4, body4, 0)
        lax.fori_loop(2 * n4, sub.n_off // 2, body2, 0)

        @pl.when(sub.n_off % 2 == 0)
        def _():
            then()
            softmax_pv(sub, sub.qi, sub.sa, True)

        @pl.when(sub.n_off % 2 == 1)
        def _():
            scores(sub, sub.qi, sub.sb)
            softmax_pv(sub, sub.qi - 1, sub.sa, False)
            then()
            softmax_pv(sub, sub.qi, sub.sb, True)

    subs = [Sub(idx) for idx in range(n_sub)]
    scores(subs[0], subs[0].first, subs[0].sa)
    for sub, nxt in zip(subs, subs[1:] + [None]):
        sweep(sub, (lambda: None) if nxt is None else functools.partial(scores, nxt, nxt.first, nxt.sa))

    o_ref[...] = (acc_ref[...] / jnp.sum(l_ref[...], axis=-1, keepdims=True)).astype(BF16)


def _first_live_block(cum, tq, qk_bound):
    c0 = cum[:, 0, ::tq]
    cend = cum[:, 0, tq - 1::tq]
    gap = (c0[:, :, None] - cend[:, None, :]) * LOG2E + 2.0 * qk_bound
    nq = c0.shape[1]
    earlier = jnp.arange(nq)[None, :] < jnp.arange(nq)[:, None]
    return jnp.sum((gap < -UNDERFLOW_LOG2) & earlier[None], axis=-1).astype(jnp.int32).reshape(-1)


def _fox_attention(z, cum, qk_bound, bsz, seq, n_heads):
    n_sub = ATT_BLOCKS_PER_STEP
    tq = min(ATT_BLOCK, seq // n_sub)
    nq = seq // tq
    n_groups = nq // n_sub
    rows = n_sub * tq
    kern = functools.partial(_fox_kernel, tq, n_sub)
    grid_spec = pltpu.PrefetchScalarGridSpec(
        num_scalar_prefetch=1,
        grid=(bsz, n_heads, n_groups),
        in_specs=[pl.BlockSpec((rows, HEAD_DIM), lambda b, h, i, f: (b * n_groups + i, h)),
                  pl.BlockSpec((seq, HEAD_DIM), lambda b, h, i, f: (b, n_heads + h)),
                  pl.BlockSpec((seq, HEAD_DIM), lambda b, h, i, f: (b, 2 * n_heads + h)),
                  pl.BlockSpec((1, 1, seq), lambda b, h, i, f: (b * n_heads + h, 0, 0))],
        out_specs=pl.BlockSpec((rows, HEAD_DIM), lambda b, h, i, f: (b * n_groups + i, h)),
        scratch_shapes=[[pltpu.VMEM((tq, tq), F32)] * (2 * n_sub),
                        pltpu.VMEM((rows, LANES), F32), pltpu.VMEM((rows, LANES), F32),
                        pltpu.VMEM((rows, HEAD_DIM), F32)],
    )
    return pl.pallas_call(
        kern,
        grid_spec=grid_spec,
        out_shape=jax.ShapeDtypeStruct((bsz * seq, n_heads * HEAD_DIM), BF16),
        compiler_params=_params("arbitrary", "arbitrary", "arbitrary"),
    )(_first_live_block(cum, tq, qk_bound), z, z, z, cum)


def _ret_kernel(chunk, n_heads, lg_ref, q_ref, k_ref, v_ref, g_ref, nw_ref, o_ref, state_ref, decay_ref,
                qdec_ref, kdec_ref):
    first = (pl.program_id(0) == 0) & (pl.program_id(1) == 0)

    @pl.when(first)
    def _():
        i = lax.broadcasted_iota(jnp.int32, (chunk, chunk), 0)
        jj = lax.broadcasted_iota(jnp.int32, (chunk, chunk), 1)
        diff = (i - jj).astype(F32)
        pos = lax.broadcasted_iota(jnp.int32, (chunk, HEAD_DIM), 0).astype(F32)
        for h in range(n_heads):
            decay_ref[h] = jnp.where(diff >= 0, jnp.exp(lg_ref[h] * jnp.maximum(diff, 0.0)), 0.0)
            qdec_ref[h] = jnp.exp(lg_ref[h] * (pos + 1.0))
            kdec_ref[h] = jnp.exp(lg_ref[h] * (chunk - 1.0 - pos))

    @pl.when(pl.program_id(1) == 0)
    def _():
        state_ref[...] = jnp.zeros(state_ref.shape, F32)

    for h in range(n_heads):
        log_g = lg_ref[h]
        cols = slice(h * HEAD_DIM, (h + 1) * HEAD_DIM)
        q = q_ref[:, cols]
        k = k_ref[:, cols]
        v = v_ref[:, cols]
        scores = lax.dot_general(q, k, (((1,), (1,)), ((), ())), preferred_element_type=F32)
        scores = scores * decay_ref[h]
        intra = jnp.dot(scores.astype(BF16), v, preferred_element_type=F32)
        state = state_ref[h]
        inter = jnp.dot(q, state.astype(BF16), preferred_element_type=F32) * qdec_ref[h]
        kd = (k.astype(F32) * kdec_ref[h]).astype(BF16)
        kv = lax.dot_general(kd, v, (((0,), (0,)), ((), ())), preferred_element_type=F32)
        state_ref[h] = state * jnp.exp(jnp.full((1, HEAD_DIM), chunk, F32) * log_g) + kv
        o = intra + inter
        ms = jnp.mean(o * o, axis=-1, keepdims=True)
        o = o * lax.rsqrt(ms + EPS) * nw_ref[:, cols]
        o_ref[:, cols] = (o * _silu(g_ref[:, cols].astype(F32))).astype(BF16)


def _retention(z, log_g, norm_w, bsz, seq, n_heads, col0):
    chunk = min(RET_CHUNK, seq)
    nt = seq // chunk
    width = n_heads * HEAD_DIM
    c0 = col0 // width
    kern = functools.partial(_ret_kernel, chunk, n_heads)

    def sec(s):
        return pl.BlockSpec((chunk, width), lambda b, t, lg: (b * nt + t, c0 + s))

    grid_spec = pltpu.PrefetchScalarGridSpec(
        num_scalar_prefetch=1,
        grid=(bsz, nt),
        in_specs=[sec(0), sec(1), sec(2), sec(3), pl.BlockSpec((1, width), lambda b, t, lg: (0, 0))],
        out_specs=pl.BlockSpec((chunk, width), lambda b, t, lg: (b * nt + t, 0)),
        scratch_shapes=[pltpu.VMEM((n_heads, HEAD_DIM, HEAD_DIM), F32),
                        pltpu.VMEM((n_heads, chunk, chunk), F32),
                        pltpu.VMEM((n_heads, chunk, HEAD_DIM), F32),
                        pltpu.VMEM((n_heads, chunk, HEAD_DIM), F32)],
    )
    return pl.pallas_call(
        kern,
        grid_spec=grid_spec,
        out_shape=jax.ShapeDtypeStruct((bsz * seq, width), BF16),
        compiler_params=_params("arbitrary", "arbitrary"),
    )(log_g, z, z, z, z, norm_w)


def _outproj_kernel(oa_ref, ob_ref, wa_ref, wb_ref, x_ref, g1_ref, nw_ref, sc_ref, sh_ref, wr_ref, br_ref,
                    x1_ref, hp_ref, lg_ref):
    mix = jnp.dot(oa_ref[...], wa_ref[...], preferred_element_type=F32)
    mix = mix + jnp.dot(ob_ref[...], wb_ref[...], preferred_element_type=F32)
    x1 = x_ref[...] + g1_ref[0] * mix
    x1_ref[...] = x1
    ms = jnp.mean(x1 * x1, axis=-1, keepdims=True)
    h2 = x1 * lax.rsqrt(ms + EPS) * nw_ref[...] * (1.0 + sc_ref[0]) + sh_ref[0]
    hp_ref[...] = _rows_to_tiles(_pack_halves(h2))
    h_hi = h2.astype(BF16)
    h_lo = (h2 - h_hi.astype(F32)).astype(BF16)
    both = jnp.dot(h_hi, wr_ref[...], preferred_element_type=F32)
    cross = jnp.dot(h_lo, wr_ref[:, :LANES], preferred_element_type=F32)
    lg_ref[...] = both[:, :LANES] + both[:, LANES:] + cross + br_ref[...]


def _output_projection(o_a, o_b, w_out, x2d, seq, g1, norm_w, sc2, sh2, w_router, b_router):
    n, d = x2d.shape
    da = o_a.shape[1]
    tm = min(OUTPROJ_ROWS, seq)
    tiles_per_seq = seq // tm
    bsel = lambda i: (i // tiles_per_seq, 0, 0)
    return pl.pallas_call(
        _outproj_kernel,
        grid=(n // tm,),
        in_specs=[pl.BlockSpec((tm, da), lambda i: (i, 0)),
                  pl.BlockSpec((tm, da), lambda i: (i, 0)),
                  pl.BlockSpec((da, d), lambda i: (0, 0)),
                  pl.BlockSpec((da, d), lambda i: (1, 0)),
                  pl.BlockSpec((tm, d), lambda i: (i, 0)),
                  pl.BlockSpec((1, 1, d), bsel),
                  pl.BlockSpec((1, d), lambda i: (0, 0)),
                  pl.BlockSpec((1, 1, d), bsel),
                  pl.BlockSpec((1, 1, d), bsel),
                  pl.BlockSpec((d, 2 * LANES), lambda i: (0, 0)),
                  pl.BlockSpec((1, LANES), lambda i: (0, 0))],
        out_specs=[pl.BlockSpec((tm, d), lambda i: (i, 0)),
                   pl.BlockSpec((tm, d // 2 // LANES, LANES), lambda i: (i, 0, 0)),
                   pl.BlockSpec((tm, LANES), lambda i: (i, 0))],
        out_shape=[jax.ShapeDtypeStruct((n, d), F32),
                   jax.ShapeDtypeStruct((n, d // 2 // LANES, LANES), U32),
                   jax.ShapeDtypeStruct((n, LANES), F32)],
        compiler_params=_params("arbitrary"),
    )(o_a, o_b, w_out, w_out, x2d, g1, norm_w, sc2, sh2, w_router, b_router)


def _route_kernel(blk, n_blocks, lg_ref, gate_ref, ids_ref, plan_ref, run_ref):
    i = pl.program_id(0)

    @pl.when(i == 0)
    def _():
        run_ref[...] = jnp.zeros(run_ref.shape, F32)

    lg = lg_ref[...]
    tt = lg.shape[0]
    lane = lax.broadcasted_iota(jnp.int32, lg.shape, 1).astype(F32)
    big = 1e6

    def rmax(v):
        return jnp.max(v, axis=-1, keepdims=True)

    def rmin(v):
        return jnp.min(v, axis=-1, keepdims=True)

    def rsum(v):
        return jnp.sum(v, axis=-1, keepdims=True)

    cmask = lane < N_GROUPS
    cm = jnp.where(cmask, lg, NEG_BIG)
    ce = jnp.where(cmask, jnp.exp(cm - rmax(cm)), 0.0)
    pgrp = ce / rsum(ce)
    p_g = rmax(pgrp)
    g_sel = rmin(jnp.where(cmask & (pgrp == p_g), lane, big))

    lo = N_GROUPS + EXPERTS_PER_GROUP * g_sel
    fmask = (lane >= lo) & (lane < lo + EXPERTS_PER_GROUP)
    fm = jnp.where(fmask, lg, NEG_BIG)
    fe = jnp.where(fmask, jnp.exp(fm - rmax(fm)), 0.0)
    fp = fe / rsum(fe)
    fp = jnp.where(fmask, fp, -1.0)
    p1 = rmax(fp)
    i1 = rmin(jnp.where(fp == p1, lane, big))
    fp2 = jnp.where(lane == i1, -1.0, fp)
    p2 = rmax(fp2)
    i2 = rmin(jnp.where(fp2 == p2, lane, big))
    denom = p1 + p2
    w1 = p_g * p1 / denom
    w2 = p_g * p2 / denom
    e1 = i1 - N_GROUPS
    e2 = i2 - N_GROUPS

    gate_ref[...] = jnp.where(lane == 0, w1, jnp.where(lane == 1, w2, 0.0))

    oh1 = (lane == e1).astype(F32)
    oh2 = (lane == e2).astype(F32)
    both = oh1 + oh2
    ra = lax.broadcasted_iota(jnp.int32, (tt, tt), 0)
    rb = lax.broadcasted_iota(jnp.int32, (tt, tt), 1)
    strict = (rb < ra).astype(BF16)
    prefix = jnp.dot(strict, both.astype(BF16), preferred_element_type=F32) + run_ref[...]
    r1 = rsum(prefix * oh1)
    r2 = rsum(prefix * oh2)
    run_ref[...] = run_ref[...] + jnp.sum(both, axis=0, keepdims=True)

    packed = jnp.where(lane == 0, e1, jnp.where(lane == 1, e2, jnp.where(lane == 2, r1,
                                                                        jnp.where(lane == 3, r2, 0.0))))
    ids_ref[...] = jnp.transpose(packed)[:8, :].astype(jnp.int32)

    @pl.when(i == pl.num_programs(0) - 1)
    def _():
        cnt = jnp.broadcast_to(run_ref[...], (8, LANES))
        lane8 = lax.broadcasted_iota(jnp.int32, (8, LANES), 1)
        padded = jnp.floor((cnt + (blk - 1.0)) * (1.0 / blk)) * blk
        pend = padded
        for sh in (1, 2, 4, 8, 16, 32, 64):
            pend = pend + jnp.where(lane8 >= sh, pltpu.roll(pend, sh, 1), 0.0)
        pstart = pend - padded
        total = jnp.max(pend, axis=-1, keepdims=True)
        tail = total + (lane8 - N_EXPERTS).astype(F32) * blk
        fill = jnp.where(lane8 < N_EXPERTS, jnp.where(padded > 0, pend - blk, -1.0),
                         jnp.where((lane8 < 2 * N_EXPERTS) & (tail < n_blocks * blk), tail, -1.0))
        row8 = lax.broadcasted_iota(jnp.int32, (8, LANES), 0)
        plan_ref[...] = jnp.where(row8 == 0, pstart, jnp.where(row8 == 1, fill,
                                                               jnp.where(row8 == 2, cnt, 0.0))).astype(jnp.int32)


def _route(logits, blk, n_blocks):
    n = logits.shape[0]
    tt = min(ROUTE_ROWS, n)
    blkspec = lambda: pl.BlockSpec((tt, LANES), lambda i: (i, 0))
    return pl.pallas_call(
        functools.partial(_route_kernel, blk, n_blocks),
        grid=(n // tt,),
        in_specs=[blkspec()],
        out_specs=[blkspec(),
                   pl.BlockSpec((8, tt), lambda i: (0, i)),
                   pl.BlockSpec((8, LANES), lambda i: (0, 0))],
        out_shape=[jax.ShapeDtypeStruct((n, LANES), F32),
                   jax.ShapeDtypeStruct((8, n), jnp.int32),
                   jax.ShapeDtypeStruct((8, LANES), jnp.int32)],
        scratch_shapes=[pltpu.VMEM((1, LANES), F32)],
        compiler_params=_params("arbitrary"),
    )(logits)


def _dispatch_kernel(tt, blk, n_fill, dest_ref, fill_ref, h_ref, xs_ref, zero_ref, sem, zsem):
    i = pl.program_id(0)
    n_tok = pl.num_programs(0) * tt

    @pl.when(i == 0)
    def _():
        zero_ref[...] = jnp.zeros(zero_ref.shape, U32)

        def zcopy(z):
            row = pl.multiple_of(jnp.maximum(fill_ref[z], 0), blk)
            return pltpu.make_async_copy(zero_ref, xs_ref.at[pl.ds(row, blk)], zsem)

        def zissue(z, carry):
            @pl.when(fill_ref[z] >= 0)
            def _():
                zcopy(z).start()
            return carry

        def zdrain(z, carry):
            @pl.when(fill_ref[z] >= 0)
            def _():
                zcopy(z).wait()
            return carry

        lax.fori_loop(0, n_fill, zissue, 0)
        lax.fori_loop(0, n_fill, zdrain, 0)

    def copy(r, kk):
        d = dest_ref[kk * n_tok + i * tt + r]
        return pltpu.make_async_copy(h_ref.at[r], xs_ref.at[d], sem)

    def issue(r, carry):
        for kk in range(TOP_K):
            copy(r, kk).start(priority=kk % 2)
        return carry

    lax.fori_loop(0, tt, issue, 0, unroll=8)
    for _ in range(TOP_K):
        pltpu.make_async_copy(h_ref, xs_ref.at[pl.ds(0, tt)], sem).wait()


def _dispatch(h_packed, dest_flat, fill_rows, n_slots, blk):
    n = h_packed.shape[0]
    tile = h_packed.shape[1:]
    tt = min(DISPATCH_TOKENS, n)
    n_fill = fill_rows.shape[0]
    grid_spec = pltpu.PrefetchScalarGridSpec(
        num_scalar_prefetch=2,
        grid=(n // tt,),
        in_specs=[pl.BlockSpec((tt,) + tile, lambda i, d, f: (i, 0, 0))],
        out_specs=pl.BlockSpec(memory_space=pl.ANY),
        scratch_shapes=[pltpu.VMEM((blk,) + tile, U32), pltpu.SemaphoreType.DMA(()), pltpu.SemaphoreType.DMA(())],
    )
    return pl.pallas_call(
        functools.partial(_dispatch_kernel, tt, blk, n_fill),
        grid_spec=grid_spec,
        out_shape=jax.ShapeDtypeStruct((n_slots,) + tile, U32),
        compiler_params=_params("arbitrary"),
    )(dest_flat, fill_rows, h_packed)


def _expert_kernel(blk, ahead, cnt_ref, pstart_ref, fill_ref, xs_ref, w1_ref, w3_ref, w2_ref, y_ref,
                   w1f, w3f, w2f, w1b, w3b, w2b, xbuf, ybuf, done_ref, w_sem, in_sem, out_sem):
    e = pl.program_id(0)
    n_exp = pl.num_programs(0)
    wslot = e % 2
    n_blk = (cnt_ref[e] + (blk - 1)) // blk
    base = pstart_ref[e]
    n_x = xbuf.shape[0]

    def weight_copies(ex, slot):
        return [pltpu.make_async_copy(src.at[ex], dst.at[slot], w_sem.at[slot])
                for src, dst in ((w1_ref, w1f), (w3_ref, w3f), (w2_ref, w2f))]

    def rows(b):
        return pl.ds(pl.multiple_of(base + b * blk, blk), blk)

    def in_copy(b, slot):
        return pltpu.make_async_copy(xs_ref.at[rows(b)], xbuf.at[slot], in_sem.at[slot])

    def out_copy(b, slot):
        return pltpu.make_async_copy(ybuf.at[slot], y_ref.at[rows(b)], out_sem.at[slot])

    @pl.when(e == 0)
    def _():
        for c in weight_copies(0, 0):
            c.start()

    for p in range(ahead):
        @pl.when(p < n_blk)
        def _():
            in_copy(p, p).start()

    @pl.when(e + 1 < n_exp)
    def _():
        for c in weight_copies(e + 1, 1 - wslot):
            c.start()

    for c in weight_copies(e, wslot):
        c.wait()
    w1b[...] = w1f[wslot].astype(BF16)
    w3b[...] = w3f[wslot].astype(BF16)
    w2b[...] = w2f[wslot].astype(BF16)

    @pl.when(e == 0)
    def _():
        done_ref[0] = 0

    done = done_ref[0]

    def out_wait(slot):
        pltpu.make_async_copy(ybuf.at[slot], y_ref.at[pl.ds(0, blk)], out_sem.at[slot]).wait()

    def body(b, carry):
        slot = (done + b) % 2

        @pl.when(b + ahead < n_blk)
        def _():
            in_copy(b + ahead, (b + ahead) % n_x).start()

        in_copy(b, b % n_x).wait()

        @pl.when(done + b >= 2)
        def _():
            out_wait(slot)

        lo, hi = _unpack_halves(_tiles_to_rows(xbuf[b % n_x]))
        lo = lo.astype(BF16)
        hi = hi.astype(BF16)
        half = lo.shape[1]
        a = jnp.dot(lo, w1b[:half, :], preferred_element_type=F32)
        a = a + jnp.dot(hi, w1b[half:, :], preferred_element_type=F32)
        g = jnp.dot(lo, w3b[:half, :], preferred_element_type=F32)
        g = g + jnp.dot(hi, w3b[half:, :], preferred_element_type=F32)
        mid = (_silu(a) * g).astype(BF16)
        ybuf[slot] = _rows_to_tiles(_pack_halves(jnp.dot(mid, w2b[...], preferred_element_type=F32)))
        out_copy(b, slot).start(priority=1)
        return carry

    lax.fori_loop(0, n_blk, body, 0)
    total = done + n_blk
    done_ref[0] = total

    @pl.when(e == n_exp - 1)
    def _():
        @pl.when(total >= 2)
        def _():
            out_wait(total % 2)

        @pl.when(total >= 1)
        def _():
            out_wait((total - 1) % 2)

        ybuf[0] = jnp.zeros(ybuf.shape[1:], U32)

        def zcopy(t):
            row = pl.multiple_of(jnp.maximum(fill_ref[N_EXPERTS + t], 0), blk)
            return pltpu.make_async_copy(ybuf.at[0], y_ref.at[pl.ds(row, blk)], out_sem.at[0])

        def zissue(t, carry):
            @pl.when(fill_ref[N_EXPERTS + t] >= 0)
            def _():
                zcopy(t).start()
            return carry

        def zdrain(t, carry):
            @pl.when(fill_ref[N_EXPERTS + t] >= 0)
            def _():
                zcopy(t).wait()
            return carry

        lax.fori_loop(0, N_EXPERTS, zissue, 0)
        lax.fori_loop(0, N_EXPERTS, zdrain, 0)


def _expert_blocks(xs, counts, pstart, fill_rows, w1, w3, w2, blk):
    n_slots = xs.shape[0]
    tile = xs.shape[1:]
    n_exp, d, de = w1.shape
    ahead = EXPERT_AHEAD
    hbm = pl.BlockSpec(memory_space=pl.ANY)
    grid_spec = pltpu.PrefetchScalarGridSpec(
        num_scalar_prefetch=3,
        grid=(n_exp,),
        in_specs=[hbm, hbm, hbm, hbm],
        out_specs=hbm,
        scratch_shapes=[pltpu.VMEM((2, d, de), F32), pltpu.VMEM((2, d, de), F32), pltpu.VMEM((2, de, d), F32),
                        pltpu.VMEM((d, de), BF16), pltpu.VMEM((d, de), BF16), pltpu.VMEM((de, d), BF16),
                        pltpu.VMEM((ahead + 1, blk) + tile, U32), pltpu.VMEM((2, blk) + tile, U32),
                        pltpu.SMEM((1,), jnp.int32),
                        pltpu.SemaphoreType.DMA((2,)), pltpu.SemaphoreType.DMA((ahead + 1,)),
                        pltpu.SemaphoreType.DMA((2,))],
    )
    return pl.pallas_call(
        functools.partial(_expert_kernel, blk, ahead),
        grid_spec=grid_spec,
        out_shape=jax.ShapeDtypeStruct((n_slots,) + tile, U32),
        compiler_params=_params("arbitrary"),
    )(counts, pstart, fill_rows, xs, w1, w3, w2)


def _combine_kernel(tt, n_tiles, dest_ref, x1_ref, g2_ref, gate_ref, yb_ref, o_ref, buf, sems):
    i = pl.program_id(0)

    def copy(tile, slot, r, kk):
        d = dest_ref[kk * (n_tiles * tt) + tile * tt + r]
        return pltpu.make_async_copy(yb_ref.at[d], buf.at[slot, kk, r], sems.at[slot])

    def issue_tile(tile, slot):
        def body(r, carry):
            for kk in range(TOP_K):
                copy(tile, slot, r, kk).start(priority=kk % 2)
            return carry
        lax.fori_loop(0, tt, body, 0, unroll=8)

    def wait_tile(tile, slot):
        for kk in range(TOP_K):
            pltpu.make_async_copy(yb_ref.at[pl.ds(0, tt)], buf.at[slot, kk], sems.at[slot]).wait()

    slot = i % 2

    @pl.when(i == 0)
    def _():
        issue_tile(0, 0)

    @pl.when(i + 1 < n_tiles)
    def _():
        issue_tile(i + 1, 1 - slot)

    wait_tile(i, slot)

    gate = gate_ref[...]
    wa = gate[:, 0:1]
    wb = gate[:, 1:2]
    lo_a, hi_a = _unpack_halves(_tiles_to_rows(buf[slot, 0]))
    lo_b, hi_b = _unpack_halves(_tiles_to_rows(buf[slot, 1]))
    y = jnp.concatenate([wa * lo_a + wb * lo_b, wa * hi_a + wb * hi_b], axis=-1)
    o_ref[...] = x1_ref[...] + g2_ref[0] * y


def _combine(x1, seq, g2, gates, dest_flat, yb):
    n, d = x1.shape
    tile = yb.shape[1:]
    tt = min(COMBINE_TOKENS, seq)
    n_tiles = n // tt
    tiles_per_seq = seq // tt
    grid_spec = pltpu.PrefetchScalarGridSpec(
        num_scalar_prefetch=1,
        grid=(n_tiles,),
        in_specs=[pl.BlockSpec((tt, d), lambda i, dr: (i, 0)),
                  pl.BlockSpec((1, 1, d), lambda i, dr: (i // tiles_per_seq, 0, 0)),
                  pl.BlockSpec((tt, LANES), lambda i, dr: (i, 0)),
                  pl.BlockSpec(memory_space=pl.ANY)],
        out_specs=pl.BlockSpec((tt, d), lambda i, dr: (i, 0)),
        scratch_shapes=[pltpu.VMEM((2, TOP_K, tt) + tile, U32), pltpu.SemaphoreType.DMA((2,))],
    )
    return pl.pallas_call(
        functools.partial(_combine_kernel, tt, n_tiles),
        grid_spec=grid_spec,
        out_shape=jax.ShapeDtypeStruct((n, d), F32),
        compiler_params=_params("arbitrary"),
    )(dest_flat, x1, g2, gates, yb)


def _rotation_tables(seq):
    half = HEAD_DIM // 2
    theta = ROPE_BASE ** (-np.arange(half, dtype=np.float64) / half)
    ang = np.arange(seq, dtype=np.float64)[:, None] * theta[None, :]
    cos_t = np.concatenate([np.cos(ang), np.cos(ang)], axis=-1).astype(np.float32)
    sin_t = np.concatenate([-np.sin(ang), np.sin(ang)], axis=-1).astype(np.float32)
    return jnp.asarray(cos_t), jnp.asarray(sin_t)


def _layer(x, c, w_ada, b_ada, norm1_w, w_in, forget_bias, q_norm_w, k_norm_w, ret_norm_w, w_out, norm2_w,
           w_coarse, b_coarse, w_fine, b_fine, w1, w3, w2):
    bsz, seq, d = x.shape
    n = bsz * seq
    d_fox = d // 2
    d_ret = d // 2
    n_heads = d_fox // HEAD_DIM

    mod = _ada_modulation(c, w_ada, b_ada)
    sh1, sc1, g1, sh2, sc2, g2 = [m.reshape(bsz, 1, d) for m in jnp.split(mod, 6, axis=-1)]

    f0 = 3 * d_fox
    w_fox = w_in[:, :f0].astype(BF16)
    w_ret = w_in[:, f0 + n_heads:].astype(BF16)
    w_f = jnp.zeros((d, LANES), BF16).at[:, :n_heads].set(w_in[:, f0:f0 + n_heads].astype(BF16))
    fb = jnp.zeros((1, LANES), F32).at[0, :n_heads].set(forget_bias)

    cos_t, sin_t = _rotation_tables(seq)

    x2d = x.reshape(n, d)
    z, log_f = _input_projection(x2d, seq, norm1_w.reshape(1, d), sc1, sh1, w_fox, w_ret, w_f, cos_t, sin_t,
                                 q_norm_w.reshape(1, HEAD_DIM), k_norm_w.reshape(1, HEAD_DIM), fb)

    lf = log_f[:, :n_heads].reshape(bsz, seq, n_heads).transpose(0, 2, 1).reshape(bsz * n_heads, seq)
    cum = _cumsum_rows(lf)

    qk_bound = 1.02 * LOG2E * HEAD_DIM ** 0.5 * jnp.max(jnp.abs(q_norm_w)) * jnp.max(jnp.abs(k_norm_w))
    o_a = _fox_attention(z, cum, qk_bound, bsz, seq, n_heads)
    log_g = jnp.log(1.0 - 2.0 ** (-5.0 - jnp.arange(n_heads, dtype=F32)))
    o_b = _retention(z, log_g, ret_norm_w.reshape(1, d_ret), bsz, seq, n_heads, 3 * d_fox)

    pad = LANES - N_GROUPS - N_EXPERTS
    w_router = jnp.concatenate([w_coarse, w_fine.transpose(1, 0, 2).reshape(d, N_EXPERTS),
                                jnp.zeros((d, pad), F32)], axis=1)
    b_router = jnp.concatenate([b_coarse, b_fine.reshape(N_EXPERTS), jnp.zeros((pad,), F32)]).reshape(1, LANES)

    wr_hi = w_router.astype(BF16)
    wr_lo = (w_router - wr_hi.astype(F32)).astype(BF16)
    x1, h_packed, logits = _output_projection(o_a, o_b, w_out.astype(BF16), x2d, seq, g1,
                                              norm2_w.reshape(1, d), sc2, sh2,
                                              jnp.concatenate([wr_hi, wr_lo], axis=1), b_router)

    blk = EXPERT_BLOCK
    nk = n * TOP_K
    n_blocks = nk // blk + N_EXPERTS
    gates, ids, plan = _route(logits, blk, n_blocks)
    pstart = plan[0, :N_EXPERTS]
    fill_rows = plan[1, :2 * N_EXPERTS]
    counts = plan[2, :N_EXPERTS]
    eid = ids[0:TOP_K]
    hit = eid[None] == jnp.arange(N_EXPERTS, dtype=jnp.int32)[:, None, None]
    dest = (jnp.sum(jnp.where(hit, pstart[:, None, None], 0), axis=0) + ids[TOP_K:2 * TOP_K]).reshape(nk)

    xs = _dispatch(h_packed, dest, fill_rows, n_blocks * blk, blk)
    yb = _expert_blocks(xs, counts, pstart, fill_rows, w1, w3, w2, blk)
    out = _combine(x1, seq, g2, gates, dest, yb)
    return out.reshape(bsz, seq, d)


def kernel(x, c, w_ada, b_ada, norm1_w, w_in, forget_bias, q_norm_w, k_norm_w, ret_norm_w, w_out, norm2_w,
           w_coarse, b_coarse, w_fine, b_fine, w1, w3, w2):
    c_in = c
    for l in range(w_ada.shape[0]):
        x = _layer(x, c_in, w_ada[l], b_ada[l], norm1_w[l], w_in[l], forget_bias[l], q_norm_w[l],
                   k_norm_w[l], ret_norm_w[l], w_out[l], norm2_w[l], w_coarse[l], b_coarse[l],
                   w_fine[l], b_fine[l], w1[l], w3[l], w2[l])
    return x
```

```python
import functools

import jax
import jax.numpy as jnp
import numpy as np
from jax import lax
from jax.experimental import pallas as pl
from jax.experimental.pallas import tpu as pltpu

HEAD_DIM = 128
N_GROUPS = 4
EXPERTS_PER_GROUP = 8
N_EXPERTS = N_GROUPS * EXPERTS_PER_GROUP
TOP_K = 2
ROPE_BASE = 10000.0
EPS = 1e-6

LANES = 128
VMEM_LIMIT = 56 * 1024 * 1024
NEG_BIG = -1e30
LOG2E = 1.4426950408889634
UNDERFLOW_LOG2 = 160.0

ADA_COLS = 1024
INPROJ_ROWS = 1024
INPROJ_COLS = 1024
ATT_BLOCK = 512
ATT_BLOCKS_PER_STEP = 4
RET_CHUNK = 256
OUTPROJ_ROWS = 512
ROUTE_ROWS = 512
DISPATCH_TOKENS = 1024
COMBINE_TOKENS = 512
EXPERT_BLOCK = 256
EXPERT_AHEAD = 3

F32 = jnp.float32
BF16 = jnp.bfloat16
U32 = jnp.uint32


def _params(*sem):
    return pltpu.CompilerParams(dimension_semantics=sem, vmem_limit_bytes=VMEM_LIMIT)


def _silu(v):
    return v * (1.0 / (1.0 + jnp.exp(-v)))


def _pack_halves(y):
    w = y.shape[1] // 2
    lo = pltpu.bitcast(y[:, :w].astype(BF16).astype(F32), U32)
    hi = pltpu.bitcast(y[:, w:].astype(BF16).astype(F32), U32)
    return (hi & jnp.uint32(0xFFFF0000)) | (lo >> 16)


def _rows_to_tiles(p):
    return pltpu.einshape("m(ck)->mck", p, c=8, k=LANES)


def _tiles_to_rows(t):
    return pltpu.einshape("mck->m(ck)", t)


def _unpack_halves(p):
    lo = pltpu.bitcast(p << 16, F32)
    hi = pltpu.bitcast(p & jnp.uint32(0xFFFF0000), F32)
    return lo, hi


def _ada_kernel(bsz, ct_ref, w_ref, b_ref, o_ref):
    w = w_ref[...]
    rows = []
    for b in range(o_ref.shape[0]):
        if b < bsz:
            cb = _silu(ct_ref[:, b:b + 1])
            rows.append(jnp.sum(cb * w, axis=0, keepdims=True) + b_ref[...])
        else:
            rows.append(jnp.zeros_like(b_ref[...]))
    o_ref[...] = jnp.concatenate(rows, axis=0)


def _ada_modulation(c, w_ada, b_ada):
    bsz, d = c.shape
    n = w_ada.shape[1]
    tn = ADA_COLS
    ct = jnp.zeros((d, LANES), F32).at[:, :bsz].set(c.T)
    assert bsz <= 8, "one sublane tile of modulation rows"
    out = pl.pallas_call(
        functools.partial(_ada_kernel, bsz),
        grid=(n // tn,),
        in_specs=[pl.BlockSpec((d, LANES), lambda j: (0, 0)),
                  pl.BlockSpec((d, tn), lambda j: (0, j)),
                  pl.BlockSpec((1, tn), lambda j: (0, j))],
        out_specs=pl.BlockSpec((8, tn), lambda j: (0, j)),
        out_shape=jax.ShapeDtypeStruct((8, n), F32),
        compiler_params=_params("arbitrary"),
    )(ct, w_ada, b_ada.reshape(1, n))
    return out[:bsz]


def _inproj_kernel(q_t, r_t, x_ref, nw_ref, sc_ref, sh_ref, wa_ref, wb_ref, wf_ref, cos_ref, sin_ref,
                   qw_ref, kw_ref, fb_ref, z_ref, f_ref, h_ref):
    j = pl.program_id(1)
    r0 = 3 * q_t

    @pl.when(j == 0)
    def _():
        x = x_ref[...]
        ms = jnp.mean(x * x, axis=-1, keepdims=True)
        y = x * lax.rsqrt(ms + EPS) * nw_ref[...]
        h = (y * (1.0 + sc_ref[0]) + sh_ref[0]).astype(BF16)
        h_ref[...] = h
        t = jnp.dot(h, wf_ref[...], preferred_element_type=F32) + fb_ref[...]
        f_ref[...] = jnp.minimum(t, 0.0) - jnp.log(1.0 + jnp.exp(-jnp.abs(t)))

    def heads_of(acc):
        return [acc[:, hh * HEAD_DIM:(hh + 1) * HEAD_DIM] for hh in range(acc.shape[1] // HEAD_DIM)]

    def head_norm(acc, w_row):
        outs = []
        for a in heads_of(acc):
            ms = jnp.mean(a * a, axis=-1, keepdims=True)
            outs.append(a * lax.rsqrt(ms + EPS) * w_row)
        return jnp.concatenate(outs, axis=-1).astype(BF16)

    def rotate(acc, scale):
        cs = cos_ref[...] * scale
        sn = sin_ref[...] * scale
        outs = [a * cs + pltpu.roll(a, HEAD_DIM // 2, 1) * sn for a in heads_of(acc)]
        return jnp.concatenate(outs, axis=-1).astype(BF16)

    def fox():
        return jnp.dot(h_ref[...], wa_ref[...], preferred_element_type=F32)

    def ret():
        return jnp.dot(h_ref[...], wb_ref[...], preferred_element_type=F32)

    @pl.when(j < q_t)
    def _():
        z_ref[...] = head_norm(fox(), qw_ref[...] * (LOG2E * HEAD_DIM ** -0.5))

    @pl.when((j >= q_t) & (j < 2 * q_t))
    def _():
        z_ref[...] = head_norm(fox(), kw_ref[...])

    @pl.when((j >= 2 * q_t) & (j < r0))
    def _():
        z_ref[...] = fox().astype(BF16)

    @pl.when((j >= r0) & (j < r0 + r_t))
    def _():
        z_ref[...] = rotate(ret(), 1.0)

    @pl.when((j >= r0 + r_t) & (j < r0 + 2 * r_t))
    def _():
        z_ref[...] = rotate(ret(), HEAD_DIM ** -0.5)

    @pl.when(j >= r0 + 2 * r_t)
    def _():
        z_ref[...] = ret().astype(BF16)


def _input_projection(x2d, seq, norm_w, sc1, sh1, w_all, w_ret, fox_cols, cos_t, sin_t, qw, kw, fb):
    n, d = x2d.shape
    tm, tn = min(INPROJ_ROWS, seq), INPROJ_COLS
    fox_tiles = fox_cols // tn
    ret_tiles = w_ret.shape[1] // tn
    tiles_per_seq = seq // tm
    kern = functools.partial(_inproj_kernel, fox_tiles // 3, ret_tiles // 4)
    bsel = lambda i, j: (i // tiles_per_seq, 0, 0)
    const = lambda i, j: (0, 0)
    return pl.pallas_call(
        kern,
        grid=(n // tm, fox_tiles + ret_tiles),
        in_specs=[pl.BlockSpec((tm, d), lambda i, j: (i, 0)),
                  pl.BlockSpec((1, d), const),
                  pl.BlockSpec((1, 1, d), bsel),
                  pl.BlockSpec((1, 1, d), bsel),
                  pl.BlockSpec((d, tn), lambda i, j: (0, jnp.minimum(j, fox_tiles - 1))),
                  pl.BlockSpec((d, tn), lambda i, j: (0, jnp.maximum(j - fox_tiles, 0))),
                  pl.BlockSpec((d, LANES), lambda i, j: (0, fox_cols // LANES)),
                  pl.BlockSpec((tm, HEAD_DIM), lambda i, j: (i % tiles_per_seq, 0)),
                  pl.BlockSpec((tm, HEAD_DIM), lambda i, j: (i % tiles_per_seq, 0)),
                  pl.BlockSpec((1, HEAD_DIM), const),
                  pl.BlockSpec((1, HEAD_DIM), const),
                  pl.BlockSpec((1, LANES), const)],
        out_specs=[pl.BlockSpec((tm, tn), lambda i, j: (i, j)),
                   pl.BlockSpec((tm, LANES), lambda i, j: (i, 0))],
        out_shape=[jax.ShapeDtypeStruct((n, fox_cols + w_ret.shape[1]), BF16),
                   jax.ShapeDtypeStruct((n, LANES), F32)],
        scratch_shapes=[pltpu.VMEM((tm, d), BF16)],
        compiler_params=_params("arbitrary", "arbitrary"),
    )(x2d, norm_w, sc1, sh1, w_all, w_ret, w_all, cos_t, sin_t, qw, kw, fb)


def _cumsum_kernel(x_ref, o_ref):
    x = x_ref[0]
    r = x.shape[0]
    a = lax.broadcasted_iota(jnp.int32, (LANES, LANES), 0)
    b = lax.broadcasted_iota(jnp.int32, (LANES, LANES), 1)
    upper = (a <= b).astype(F32)
    within = jnp.dot(x, upper, precision=lax.Precision.HIGHEST, preferred_element_type=F32)
    tot = jnp.broadcast_to(within[:, LANES - 1:LANES], (r, LANES))
    ra = lax.broadcasted_iota(jnp.int32, (r, r), 0)
    rb = lax.broadcasted_iota(jnp.int32, (r, r), 1)
    strict = (rb < ra).astype(F32)
    before = jnp.dot(strict, tot, precision=lax.Precision.HIGHEST, preferred_element_type=F32)
    o_ref[0] = within + before


def _cumsum_rows(x):
    g, s = x.shape
    r = s // LANES
    out = pl.pallas_call(
        _cumsum_kernel,
        grid=(g,),
        in_specs=[pl.BlockSpec((1, r, LANES), lambda i: (i, 0, 0))],
        out_specs=pl.BlockSpec((1, r, LANES), lambda i: (i, 0, 0)),
        out_shape=jax.ShapeDtypeStruct((g, r, LANES), F32),
        compiler_params=_params("arbitrary"),
    )(x.reshape(g, r, LANES))
    return out.reshape(g, 1, s)


def _fox_kernel(tq, n_sub, first_ref, q_ref, k_ref, v_ref, cum_ref, o_ref, s_refs, m_ref, l_ref, acc_ref):
    group_id = pl.program_id(2)
    n_groups = pl.num_programs(2)
    head = pl.program_id(0) * pl.num_programs(1) + pl.program_id(1)
    n_slabs = tq // LANES

    m_ref[...] = jnp.full(m_ref.shape, NEG_BIG, F32)
    l_ref[...] = jnp.zeros(l_ref.shape, F32)
    acc_ref[...] = jnp.zeros(acc_ref.shape, F32)

    class Sub:
        def __init__(self, idx):
            self.rows = slice(idx * tq, (idx + 1) * tq)
            self.qi = n_sub * group_id + idx
            self.sa, self.sb = s_refs[2 * idx], s_refs[2 * idx + 1]
            q_start = pl.multiple_of(self.qi * tq, tq)
            self.c0 = cum_ref[0, :, pl.ds(q_start, LANES)][:, 0:1]
            self.first = first_ref[(head * n_groups + group_id) * n_sub + idx]
            self.n_off = self.qi - self.first

    def scores(sub, kb, s_ref):
        start = pl.multiple_of(kb * tq, tq)
        k = k_ref[pl.ds(start, tq), :]
        bias = (sub.c0 - cum_ref[0, :, pl.ds(start, tq)]) * LOG2E
        s_ref[...] = lax.dot_general(q_ref[sub.rows, :], k, (((1,), (1,)), ((), ())),
                                     preferred_element_type=F32) + bias

    def softmax_pv(sub, kb, s_ref, masked):
        start = pl.multiple_of(kb * tq, tq)
        if masked:
            row = lax.broadcasted_iota(jnp.int32, (LANES, LANES), 0)
            col = lax.broadcasted_iota(jnp.int32, (LANES, LANES), 1)
            for g in range(n_slabs):
                r0 = g * LANES
                slabs = [s_ref[r0:r0 + LANES, j * LANES:(j + 1) * LANES] for j in range(g + 1)]
                slabs[g] = jnp.where(col <= row, slabs[g], NEG_BIG)
                rs = slice(sub.rows.start + r0, sub.rows.start + r0 + LANES)
                update(rs, slabs, v_ref[pl.ds(start, (g + 1) * LANES), :])
        else:
            slabs = [s_ref[:, j * LANES:(j + 1) * LANES] for j in range(n_slabs)]
            update(sub.rows, slabs, v_ref[pl.ds(start, tq), :])

    def update(rs, slabs, v):
        mx = slabs[0]
        for t in slabs[1:]:
            mx = jnp.maximum(mx, t)
        m_prev = m_ref[rs, :]
        m_new = jnp.maximum(m_prev, jnp.max(mx, axis=-1, keepdims=True))
        alpha = jnp.exp2(m_prev - m_new)
        probs = [jnp.exp2(t - m_new) for t in slabs]
        psum = probs[0]
        for t in probs[1:]:
            psum = psum + t
        l_ref[rs, :] = alpha * l_ref[rs, :] + psum
        p = jnp.concatenate([t.astype(BF16) for t in probs], axis=-1)
        acc_ref[rs, :] = alpha * acc_ref[rs, :] + jnp.dot(p, v, preferred_element_type=F32)
        m_ref[rs, :] = m_new

    def sweep(sub, then):
        def pair(kb):
            scores(sub, kb + 1, sub.sb)
            softmax_pv(sub, kb, sub.sa, False)
            scores(sub, kb + 2, sub.sa)
            softmax_pv(sub, kb + 1, sub.sb, False)

        def body4(i, carry):
            pair(sub.first + 4 * i)
            pair(sub.first + 4 * i + 2)
            return carry

        def body2(i, carry):
            pair(sub.first + 2 * i)
            return carry

        n4 = sub.n_off // 4
        lax.fori_loop(0, n4, body4, 0)
        lax.fori_loop(2 * n4, sub.n_off // 2, body2, 0)

        @pl.when(sub.n_off % 2 == 0)
        def _():
            then()
            softmax_pv(sub, sub.qi, sub.sa, True)

        @pl.when(sub.n_off % 2 == 1)
        def _():
            scores(sub, sub.qi, sub.sb)
            softmax_pv(sub, sub.qi - 1, sub.sa, False)
            then()
            softmax_pv(sub, sub.qi, sub.sb, True)

    subs = [Sub(idx) for idx in range(n_sub)]
    scores(subs[0], subs[0].first, subs[0].sa)
    for sub, nxt in zip(subs, subs[1:] + [None]):
        sweep(sub, (lambda: None) if nxt is None else functools.partial(scores, nxt, nxt.first, nxt.sa))

    o_ref[...] = (acc_ref[...] / jnp.sum(l_ref[...], axis=-1, keepdims=True)).astype(BF16)


def _first_live_block(cum, tq, qk_bound):
    c0 = cum[:, 0, ::tq]
    cend = cum[:, 0, tq - 1::tq]
    gap = (c0[:, :, None] - cend[:, None, :]) * LOG2E + 2.0 * qk_bound
    nq = c0.shape[1]
    earlier = jnp.arange(nq)[None, :] < jnp.arange(nq)[:, None]
    return jnp.sum((gap < -UNDERFLOW_LOG2) & earlier[None], axis=-1).astype(jnp.int32).reshape(-1)


def _fox_attention(z, cum, qk_bound, bsz, seq, n_heads):
    n_sub = ATT_BLOCKS_PER_STEP
    tq = min(ATT_BLOCK, seq // n_sub)
    nq = seq // tq
    n_groups = nq // n_sub
    rows = n_sub * tq
    kern = functools.partial(_fox_kernel, tq, n_sub)
    grid_spec = pltpu.PrefetchScalarGridSpec(
        num_scalar_prefetch=1,
        grid=(bsz, n_heads, n_groups),
        in_specs=[pl.BlockSpec((rows, HEAD_DIM), lambda b, h, i, f: (b * n_groups + i, h)),
                  pl.BlockSpec((seq, HEAD_DIM), lambda b, h, i, f: (b, n_heads + h)),
                  pl.BlockSpec((seq, HEAD_DIM), lambda b, h, i, f: (b, 2 * n_heads + h)),
                  pl.BlockSpec((1, 1, seq), lambda b, h, i, f: (b * n_heads + h, 0, 0))],
        out_specs=pl.BlockSpec((rows, HEAD_DIM), lambda b, h, i, f: (b * n_groups + i, h)),
        scratch_shapes=[[pltpu.VMEM((tq, tq), F32)] * (2 * n_sub),
                        pltpu.VMEM((rows, LANES), F32), pltpu.VMEM((rows, LANES), F32),
                        pltpu.VMEM((rows, HEAD_DIM), F32)],
    )
    return pl.pallas_call(
        kern,
        grid_spec=grid_spec,
        out_shape=jax.ShapeDtypeStruct((bsz * seq, n_heads * HEAD_DIM), BF16),
        compiler_params=_params("arbitrary", "arbitrary", "arbitrary"),
    )(_first_live_block(cum, tq, qk_bound), z, z, z, cum)


def _ret_kernel(chunk, n_heads, lg_ref, q_ref, k_ref, v_ref, g_ref, nw_ref, o_ref, state_ref, decay_ref,
                qdec_ref, kdec_ref):
    first = (pl.program_id(0) == 0) & (pl.program_id(1) == 0)

    @pl.when(first)
    def _():
        i = lax.broadcasted_iota(jnp.int32, (chunk, chunk), 0)
        jj = lax.broadcasted_iota(jnp.int32, (chunk, chunk), 1)
        diff = (i - jj).astype(F32)
        pos = lax.broadcasted_iota(jnp.int32, (chunk, HEAD_DIM), 0).astype(F32)
        for h in range(n_heads):
            decay_ref[h] = jnp.where(diff >= 0, jnp.exp(lg_ref[h] * jnp.maximum(diff, 0.0)), 0.0)
            qdec_ref[h] = jnp.exp(lg_ref[h] * (pos + 1.0))
            kdec_ref[h] = jnp.exp(lg_ref[h] * (chunk - 1.0 - pos))

    @pl.when(pl.program_id(1) == 0)
    def _():
        state_ref[...] = jnp.zeros(state_ref.shape, F32)

    for h in range(n_heads):
        log_g = lg_ref[h]
        cols = slice(h * HEAD_DIM, (h + 1) * HEAD_DIM)
        q = q_ref[:, cols]
        k = k_ref[:, cols]
        v = v_ref[:, cols]
        scores = lax.dot_general(q, k, (((1,), (1,)), ((), ())), preferred_element_type=F32)
        scores = scores * decay_ref[h]
        intra = jnp.dot(scores.astype(BF16), v, preferred_element_type=F32)
        state = state_ref[h]
        inter = jnp.dot(q, state.astype(BF16), preferred_element_type=F32) * qdec_ref[h]
        kd = (k.astype(F32) * kdec_ref[h]).astype(BF16)
        kv = lax.dot_general(kd, v, (((0,), (0,)), ((), ())), preferred_element_type=F32)
        state_ref[h] = state * jnp.exp(jnp.full((1, HEAD_DIM), chunk, F32) * log_g) + kv
        o = intra + inter
        ms = jnp.mean(o * o, axis=-1, keepdims=True)
        o = o * lax.rsqrt(ms + EPS) * nw_ref[:, cols]
        o_ref[:, cols] = (o * _silu(g_ref[:, cols].astype(F32))).astype(BF16)


def _retention(z, log_g, norm_w, bsz, seq, n_heads, col0):
    chunk = min(RET_CHUNK, seq)
    nt = seq // chunk
    width = n_heads * HEAD_DIM
    c0 = col0 // width
    kern = functools.partial(_ret_kernel, chunk, n_heads)

    def sec(s):
        return pl.BlockSpec((chunk, width), lambda b, t, lg: (b * nt + t, c0 + s))

    grid_spec = pltpu.PrefetchScalarGridSpec(
        num_scalar_prefetch=1,
        grid=(bsz, nt),
        in_specs=[sec(0), sec(1), sec(2), sec(3), pl.BlockSpec((1, width), lambda b, t, lg: (0, 0))],
        out_specs=pl.BlockSpec((chunk, width), lambda b, t, lg: (b * nt + t, 0)),
        scratch_shapes=[pltpu.VMEM((n_heads, HEAD_DIM, HEAD_DIM), F32),
                        pltpu.VMEM((n_heads, chunk, chunk), F32),
                        pltpu.VMEM((n_heads, chunk, HEAD_DIM), F32),
                        pltpu.VMEM((n_heads, chunk, HEAD_DIM), F32)],
    )
    return pl.pallas_call(
        kern,
        grid_spec=grid_spec,
        out_shape=jax.ShapeDtypeStruct((bsz * seq, width), BF16),
        compiler_params=_params("arbitrary", "arbitrary"),
    )(log_g, z, z, z, z, norm_w)


def _outproj_kernel(oa_ref, ob_ref, wa_ref, wb_ref, x_ref, g1_ref, nw_ref, sc_ref, sh_ref, wr_ref, br_ref,
                    x1_ref, hp_ref, lg_ref):
    mix = jnp.dot(oa_ref[...], wa_ref[...], preferred_element_type=F32)
    mix = mix + jnp.dot(ob_ref[...], wb_ref[...], preferred_element_type=F32)
    x1 = x_ref[...] + g1_ref[0] * mix
    x1_ref[...] = x1
    ms = jnp.mean(x1 * x1, axis=-1, keepdims=True)
    h2 = x1 * lax.rsqrt(ms + EPS) * nw_ref[...] * (1.0 + sc_ref[0]) + sh_ref[0]
    hp_ref[...] = _rows_to_tiles(_pack_halves(h2))
    h_hi = h2.astype(BF16)
    h_lo = (h2 - h_hi.astype(F32)).astype(BF16)
    both = jnp.dot(h_hi, wr_ref[...], preferred_element_type=F32)
    cross = jnp.dot(h_lo, wr_ref[:, :LANES], preferred_element_type=F32)
    lg_ref[...] = both[:, :LANES] + both[:, LANES:] + cross + br_ref[...]


def _output_projection(o_a, o_b, w_out, x2d, seq, g1, norm_w, sc2, sh2, w_router, b_router):
    n, d = x2d.shape
    da = o_a.shape[1]
    tm = min(OUTPROJ_ROWS, seq)
    tiles_per_seq = seq // tm
    bsel = lambda i: (i // tiles_per_seq, 0, 0)
    return pl.pallas_call(
        _outproj_kernel,
        grid=(n // tm,),
        in_specs=[pl.BlockSpec((tm, da), lambda i: (i, 0)),
                  pl.BlockSpec((tm, da), lambda i: (i, 0)),
                  pl.BlockSpec((da, d), lambda i: (0, 0)),
                  pl.BlockSpec((da, d), lambda i: (1, 0)),
                  pl.BlockSpec((tm, d), lambda i: (i, 0)),
                  pl.BlockSpec((1, 1, d), bsel),
                  pl.BlockSpec((1, d), lambda i: (0, 0)),
                  pl.BlockSpec((1, 1, d), bsel),
                  pl.BlockSpec((1, 1, d), bsel),
                  pl.BlockSpec((d, 2 * LANES), lambda i: (0, 0)),
                  pl.BlockSpec((1, LANES), lambda i: (0, 0))],
        out_specs=[pl.BlockSpec((tm, d), lambda i: (i, 0)),
                   pl.BlockSpec((tm, d // 2 // LANES, LANES), lambda i: (i, 0, 0)),
                   pl.BlockSpec((tm, LANES), lambda i: (i, 0))],
        out_shape=[jax.ShapeDtypeStruct((n, d), F32),
                   jax.ShapeDtypeStruct((n, d // 2 // LANES, LANES), U32),
                   jax.ShapeDtypeStruct((n, LANES), F32)],
        compiler_params=_params("arbitrary"),
    )(o_a, o_b, w_out, w_out, x2d, g1, norm_w, sc2, sh2, w_router, b_router)


def _route_kernel(blk, n_blocks, lg_ref, gate_ref, ids_ref, plan_ref, run_ref):
    i = pl.program_id(0)

    @pl.when(i == 0)
    def _():
        run_ref[...] = jnp.zeros(run_ref.shape, F32)

    lg = lg_ref[...]
    tt = lg.shape[0]
    lane = lax.broadcasted_iota(jnp.int32, lg.shape, 1).astype(F32)
    big = 1e6

    def rmax(v):
        return jnp.max(v, axis=-1, keepdims=True)

    def rmin(v):
        return jnp.min(v, axis=-1, keepdims=True)

    def rsum(v):
        return jnp.sum(v, axis=-1, keepdims=True)

    cmask = lane < N_GROUPS
    cm = jnp.where(cmask, lg, NEG_BIG)
    ce = jnp.where(cmask, jnp.exp(cm - rmax(cm)), 0.0)
    pgrp = ce / rsum(ce)
    p_g = rmax(pgrp)
    g_sel = rmin(jnp.where(cmask & (pgrp == p_g), lane, big))

    lo = N_GROUPS + EXPERTS_PER_GROUP * g_sel
    fmask = (lane >= lo) & (lane < lo + EXPERTS_PER_GROUP)
    fm = jnp.where(fmask, lg, NEG_BIG)
    fe = jnp.where(fmask, jnp.exp(fm - rmax(fm)), 0.0)
    fp = fe / rsum(fe)
    fp = jnp.where(fmask, fp, -1.0)
    p1 = rmax(fp)
    i1 = rmin(jnp.where(fp == p1, lane, big))
    fp2 = jnp.where(lane == i1, -1.0, fp)
    p2 = rmax(fp2)
    i2 = rmin(jnp.where(fp2 == p2, lane, big))
    denom = p1 + p2
    w1 = p_g * p1 / denom
    w2 = p_g * p2 / denom
    e1 = i1 - N_GROUPS
    e2 = i2 - N_GROUPS

    gate_ref[...] = jnp.where(lane == 0, w1, jnp.where(lane == 1, w2, 0.0))

    oh1 = (lane == e1).astype(F32)
    oh2 = (lane == e2).astype(F32)
    both = oh1 + oh2
    ra = lax.broadcasted_iota(jnp.int32, (tt, tt), 0)
    rb = lax.broadcasted_iota(jnp.int32, (tt, tt), 1)
    strict = (rb < ra).astype(BF16)
    prefix = jnp.dot(strict, both.astype(BF16), preferred_element_type=F32) + run_ref[...]
    r1 = rsum(prefix * oh1)
    r2 = rsum(prefix * oh2)
    run_ref[...] = run_ref[...] + jnp.sum(both, axis=0, keepdims=True)

    packed = jnp.where(lane == 0, e1, jnp.where(lane == 1, e2, jnp.where(lane == 2, r1,
                                                                        jnp.where(lane == 3, r2, 0.0))))
    ids_ref[...] = jnp.transpose(packed)[:8, :].astype(jnp.int32)

    @pl.when(i == pl.num_programs(0) - 1)
    def _():
        cnt = jnp.broadcast_to(run_ref[...], (8, LANES))
        lane8 = lax.broadcasted_iota(jnp.int32, (8, LANES), 1)
        padded = jnp.floor((cnt + (blk - 1.0)) * (1.0 / blk)) * blk
        pend = padded
        for sh in (1, 2, 4, 8, 16, 32, 64):
            pend = pend + jnp.where(lane8 >= sh, pltpu.roll(pend, sh, 1), 0.0)
        pstart = pend - padded
        total = jnp.max(pend, axis=-1, keepdims=True)
        tail = total + (lane8 - N_EXPERTS).astype(F32) * blk
        fill = jnp.where(lane8 < N_EXPERTS, jnp.where(padded > 0, pend - blk, -1.0),
                         jnp.where((lane8 < 2 * N_EXPERTS) & (tail < n_blocks * blk), tail, -1.0))
        row8 = lax.broadcasted_iota(jnp.int32, (8, LANES), 0)
        plan_ref[...] = jnp.where(row8 == 0, pstart, jnp.where(row8 == 1, fill,
                                                               jnp.where(row8 == 2, cnt, 0.0))).astype(jnp.int32)


def _route(logits, blk, n_blocks):
    n = logits.shape[0]
    tt = min(ROUTE_ROWS, n)
    blkspec = lambda: pl.BlockSpec((tt, LANES), lambda i: (i, 0))
    return pl.pallas_call(
        functools.partial(_route_kernel, blk, n_blocks),
        grid=(n // tt,),
        in_specs=[blkspec()],
        out_specs=[blkspec(),
                   pl.BlockSpec((8, tt), lambda i: (0, i)),
                   pl.BlockSpec((8, LANES), lambda i: (0, 0))],
        out_shape=[jax.ShapeDtypeStruct((n, LANES), F32),
                   jax.ShapeDtypeStruct((8, n), jnp.int32),
                   jax.ShapeDtypeStruct((8, LANES), jnp.int32)],
        scratch_shapes=[pltpu.VMEM((1, LANES), F32)],
        compiler_params=_params("arbitrary"),
    )(logits)


def _dispatch_kernel(tt, blk, n_fill, dest_ref, fill_ref, h_ref, xs_ref, zero_ref, sem, zsem):
    i = pl.program_id(0)
    n_tok = pl.num_programs(0) * tt

    @pl.when(i == 0)
    def _():
        zero_ref[...] = jnp.zeros(zero_ref.shape, U32)

        def zcopy(z):
            row = pl.multiple_of(jnp.maximum(fill_ref[z], 0), blk)
            return pltpu.make_async_copy(zero_ref, xs_ref.at[pl.ds(row, blk)], zsem)

        def zissue(z, carry):
            @pl.when(fill_ref[z] >= 0)
            def _():
                zcopy(z).start()
            return carry

        def zdrain(z, carry):
            @pl.when(fill_ref[z] >= 0)
            def _():
                zcopy(z).wait()
            return carry

        lax.fori_loop(0, n_fill, zissue, 0)
        lax.fori_loop(0, n_fill, zdrain, 0)

    def copy(r, kk):
        d = dest_ref[kk * n_tok + i * tt + r]
        return pltpu.make_async_copy(h_ref.at[r], xs_ref.at[d], sem)

    def issue(r, carry):
        for kk in range(TOP_K):
            copy(r, kk).start(priority=kk % 2)
        return carry

    lax.fori_loop(0, tt, issue, 0, unroll=8)
    for _ in range(TOP_K):
        pltpu.make_async_copy(h_ref, xs_ref.at[pl.ds(0, tt)], sem).wait()


def _dispatch(h_packed, dest_flat, fill_rows, n_slots, blk):
    n = h_packed.shape[0]
    tile = h_packed.shape[1:]
    tt = min(DISPATCH_TOKENS, n)
    n_fill = fill_rows.shape[0]
    grid_spec = pltpu.PrefetchScalarGridSpec(
        num_scalar_prefetch=2,
        grid=(n // tt,),
        in_specs=[pl.BlockSpec((tt,) + tile, lambda i, d, f: (i, 0, 0))],
        out_specs=pl.BlockSpec(memory_space=pl.ANY),
        scratch_shapes=[pltpu.VMEM((blk,) + tile, U32), pltpu.SemaphoreType.DMA(()), pltpu.SemaphoreType.DMA(())],
    )
    return pl.pallas_call(
        functools.partial(_dispatch_kernel, tt, blk, n_fill),
        grid_spec=grid_spec,
        out_shape=jax.ShapeDtypeStruct((n_slots,) + tile, U32),
        compiler_params=_params("arbitrary"),
    )(dest_flat, fill_rows, h_packed)


def _expert_kernel(blk, ahead, cnt_ref, pstart_ref, fill_ref, xs_ref, w1_ref, w3_ref, w2_ref, y_ref,
                   w1f, w3f, w2f, w1b, w3b, w2b, xbuf, ybuf, done_ref, w_sem, in_sem, out_sem):
    e = pl.program_id(0)
    n_exp = pl.num_programs(0)
    wslot = e % 2
    n_blk = (cnt_ref[e] + (blk - 1)) // blk
    base = pstart_ref[e]
    n_x = xbuf.shape[0]

    def weight_copies(ex, slot):
        return [pltpu.make_async_copy(src.at[ex], dst.at[slot], w_sem.at[slot])
                for src, dst in ((w1_ref, w1f), (w3_ref, w3f), (w2_ref, w2f))]

    def rows(b):
        return pl.ds(pl.multiple_of(base + b * blk, blk), blk)

    def in_copy(b, slot):
        return pltpu.make_async_copy(xs_ref.at[rows(b)], xbuf.at[slot], in_sem.at[slot])

    def out_copy(b, slot):
        return pltpu.make_async_copy(ybuf.at[slot], y_ref.at[rows(b)], out_sem.at[slot])

    @pl.when(e == 0)
    def _():
        for c in weight_copies(0, 0):
            c.start()

    for p in range(ahead):
        @pl.when(p < n_blk)
        def _():
            in_copy(p, p).start()

    @pl.when(e + 1 < n_exp)
    def _():
        for c in weight_copies(e + 1, 1 - wslot):
            c.start()

    for c in weight_copies(e, wslot):
        c.wait()
    w1b[...] = w1f[wslot].astype(BF16)
    w3b[...] = w3f[wslot].astype(BF16)
    w2b[...] = w2f[wslot].astype(BF16)

    @pl.when(e == 0)
    def _():
        done_ref[0] = 0

    done = done_ref[0]

    def out_wait(slot):
        pltpu.make_async_copy(ybuf.at[slot], y_ref.at[pl.ds(0, blk)], out_sem.at[slot]).wait()

    def body(b, carry):
        slot = (done + b) % 2

        @pl.when(b + ahead < n_blk)
        def _():
            in_copy(b + ahead, (b + ahead) % n_x).start()

        in_copy(b, b % n_x).wait()

        @pl.when(done + b >= 2)
        def _():
            out_wait(slot)

        lo, hi = _unpack_halves(_tiles_to_rows(xbuf[b % n_x]))
        lo = lo.astype(BF16)
        hi = hi.astype(BF16)
        half = lo.shape[1]
        a = jnp.dot(lo, w1b[:half, :], preferred_element_type=F32)
        a = a + jnp.dot(hi, w1b[half:, :], preferred_element_type=F32)
        g = jnp.dot(lo, w3b[:half, :], preferred_element_type=F32)
        g = g + jnp.dot(hi, w3b[half:, :], preferred_element_type=F32)
        mid = (_silu(a) * g).astype(BF16)
        ybuf[slot] = _rows_to_tiles(_pack_halves(jnp.dot(mid, w2b[...], preferred_element_type=F32)))
        out_copy(b, slot).start(priority=1)
        return carry

    lax.fori_loop(0, n_blk, body, 0)
    total = done + n_blk
    done_ref[0] = total

    @pl.when(e == n_exp - 1)
    def _():
        @pl.when(total >= 2)
        def _():
            out_wait(total % 2)

        @pl.when(total >= 1)
        def _():
            out_wait((total - 1) % 2)

        ybuf[0] = jnp.zeros(ybuf.shape[1:], U32)

        def zcopy(t):
            row = pl.multiple_of(jnp.maximum(fill_ref[N_EXPERTS + t], 0), blk)
            return pltpu.make_async_copy(ybuf.at[0], y_ref.at[pl.ds(row, blk)], out_sem.at[0])

        def zissue(t, carry):
            @pl.when(fill_ref[N_EXPERTS + t] >= 0)
            def _():
                zcopy(t).start()
            return carry

        def zdrain(t, carry):
            @pl.when(fill_ref[N_EXPERTS + t] >= 0)
            def _():
                zcopy(t).wait()
            return carry

        lax.fori_loop(0, N_EXPERTS, zissue, 0)
        lax.fori_loop(0, N_EXPERTS, zdrain, 0)


def _expert_blocks(xs, counts, pstart, fill_rows, w1, w3, w2, blk):
    n_slots = xs.shape[0]
    tile = xs.shape[1:]
    n_exp, d, de = w1.shape
    ahead = EXPERT_AHEAD
    hbm = pl.BlockSpec(memory_space=pl.ANY)
    grid_spec = pltpu.PrefetchScalarGridSpec(
        num_scalar_prefetch=3,
        grid=(n_exp,),
        in_specs=[hbm, hbm, hbm, hbm],
        out_specs=hbm,
        scratch_shapes=[pltpu.VMEM((2, d, de), F32), pltpu.VMEM((2, d, de), F32), pltpu.VMEM((2, de, d), F32),
                        pltpu.VMEM((d, de), BF16), pltpu.VMEM((d, de), BF16), pltpu.VMEM((de, d), BF16),
                        pltpu.VMEM((ahead + 1, blk) + tile, U32), pltpu.VMEM((2, blk) + tile, U32),
                        pltpu.SMEM((1,), jnp.int32),
                        pltpu.SemaphoreType.DMA((2,)), pltpu.SemaphoreType.DMA((ahead + 1,)),
                        pltpu.SemaphoreType.DMA((2,))],
    )
    return pl.pallas_call(
        functools.partial(_expert_kernel, blk, ahead),
        grid_spec=grid_spec,
        out_shape=jax.ShapeDtypeStruct((n_slots,) + tile, U32),
        compiler_params=_params("arbitrary"),
    )(counts, pstart, fill_rows, xs, w1, w3, w2)


def _combine_kernel(tt, n_tiles, dest_ref, x1_ref, g2_ref, gate_ref, yb_ref, o_ref, buf, sems):
    i = pl.program_id(0)

    def copy(tile, slot, r, kk):
        d = dest_ref[kk * (n_tiles * tt) + tile * tt + r]
        return pltpu.make_async_copy(yb_ref.at[d], buf.at[slot, kk, r], sems.at[slot])

    def issue_tile(tile, slot):
        def body(r, carry):
            for kk in range(TOP_K):
                copy(tile, slot, r, kk).start(priority=kk % 2)
            return carry
        lax.fori_loop(0, tt, body, 0, unroll=8)

    def wait_tile(tile, slot):
        for kk in range(TOP_K):
            pltpu.make_async_copy(yb_ref.at[pl.ds(0, tt)], buf.at[slot, kk], sems.at[slot]).wait()

    slot = i % 2

    @pl.when(i == 0)
    def _():
        issue_tile(0, 0)

    @pl.when(i + 1 < n_tiles)
    def _():
        issue_tile(i + 1, 1 - slot)

    wait_tile(i, slot)

    gate = gate_ref[...]
    wa = gate[:, 0:1]
    wb = gate[:, 1:2]
    lo_a, hi_a = _unpack_halves(_tiles_to_rows(buf[slot, 0]))
    lo_b, hi_b = _unpack_halves(_tiles_to_rows(buf[slot, 1]))
    y = jnp.concatenate([wa * lo_a + wb * lo_b, wa * hi_a + wb * hi_b], axis=-1)
    o_ref[...] = x1_ref[...] + g2_ref[0] * y


def _combine(x1, seq, g2, gates, dest_flat, yb):
    n, d = x1.shape
    tile = yb.shape[1:]
    tt = min(COMBINE_TOKENS, seq)
    n_tiles = n // tt
    tiles_per_seq = seq // tt
    grid_spec = pltpu.PrefetchScalarGridSpec(
        num_scalar_prefetch=1,
        grid=(n_tiles,),
        in_specs=[pl.BlockSpec((tt, d), lambda i, dr: (i, 0)),
                  pl.BlockSpec((1, 1, d), lambda i, dr: (i // tiles_per_seq, 0, 0)),
                  pl.BlockSpec((tt, LANES), lambda i, dr: (i, 0)),
                  pl.BlockSpec(memory_space=pl.ANY)],
        out_specs=pl.BlockSpec((tt, d), lambda i, dr: (i, 0)),
        scratch_shapes=[pltpu.VMEM((2, TOP_K, tt) + tile, U32), pltpu.SemaphoreType.DMA((2,))],
    )
    return pl.pallas_call(
        functools.partial(_combine_kernel, tt, n_tiles),
        grid_spec=grid_spec,
        out_shape=jax.ShapeDtypeStruct((n, d), F32),
        compiler_params=_params("arbitrary"),
    )(dest_flat, x1, g2, gates, yb)


def _rotation_tables(seq):
    half = HEAD_DIM // 2
    theta = ROPE_BASE ** (-np.arange(half, dtype=np.float64) / half)
    ang = np.arange(seq, dtype=np.float64)[:, None] * theta[None, :]
    cos_t = np.concatenate([np.cos(ang), np.cos(ang)], axis=-1).astype(np.float32)
    sin_t = np.concatenate([-np.sin(ang), np.sin(ang)], axis=-1).astype(np.float32)
    return jnp.asarray(cos_t), jnp.asarray(sin_t)


def _layer(x, c, w_ada, b_ada, norm1_w, w_in, forget_bias, q_norm_w, k_norm_w, ret_norm_w, w_out, norm2_w,
           w_coarse, b_coarse, w_fine, b_fine, w1, w3, w2):
    bsz, seq, d = x.shape
    n = bsz * seq
    d_fox = d // 2
    d_ret = d // 2
    n_heads = d_fox // HEAD_DIM

    mod = _ada_modulation(c, w_ada, b_ada)
    sh1, sc1, g1, sh2, sc2, g2 = [m.reshape(bsz, 1, d) for m in jnp.split(mod, 6, axis=-1)]

    f0 = 3 * d_fox
    w_all = w_in.astype(BF16)
    w_ret = w_all[:, f0 + n_heads:]
    fb = jnp.zeros((1, LANES), F32).at[0, :n_heads].set(forget_bias)

    cos_t, sin_t = _rotation_tables(seq)

    x2d = x.reshape(n, d)
    z, log_f = _input_projection(x2d, seq, norm1_w.reshape(1, d), sc1, sh1, w_all, w_ret, f0, cos_t, sin_t,
                                 q_norm_w.reshape(1, HEAD_DIM), k_norm_w.reshape(1, HEAD_DIM), fb)

    lf = log_f[:, :n_heads].reshape(bsz, seq, n_heads).transpose(0, 2, 1).reshape(bsz * n_heads, seq)
    cum = _cumsum_rows(lf)

    qk_bound = 1.02 * LOG2E * HEAD_DIM ** 0.5 * jnp.max(jnp.abs(q_norm_w)) * jnp.max(jnp.abs(k_norm_w))
    o_a = _fox_attention(z, cum, qk_bound, bsz, seq, n_heads)
    log_g = jnp.log(1.0 - 2.0 ** (-5.0 - jnp.arange(n_heads, dtype=F32)))
    o_b = _retention(z, log_g, ret_norm_w.reshape(1, d_ret), bsz, seq, n_heads, 3 * d_fox)

    pad = LANES - N_GROUPS - N_EXPERTS
    w_router = jnp.concatenate([w_coarse, w_fine.transpose(1, 0, 2).reshape(d, N_EXPERTS),
                                jnp.zeros((d, pad), F32)], axis=1)
    b_router = jnp.concatenate([b_coarse, b_fine.reshape(N_EXPERTS), jnp.zeros((pad,), F32)]).reshape(1, LANES)

    wr_hi = w_router.astype(BF16)
    wr_lo = (w_router - wr_hi.astype(F32)).astype(BF16)
    x1, h_packed, logits = _output_projection(o_a, o_b, w_out.astype(BF16), x2d, seq, g1,
                                              norm2_w.reshape(1, d), sc2, sh2,
                                              jnp.concatenate([wr_hi, wr_lo], axis=1), b_router)

    blk = EXPERT_BLOCK
    nk = n * TOP_K
    n_blocks = nk // blk + N_EXPERTS
    gates, ids, plan = _route(logits, blk, n_blocks)
    pstart = plan[0, :N_EXPERTS]
    fill_rows = plan[1, :2 * N_EXPERTS]
    counts = plan[2, :N_EXPERTS]
    eid = ids[0:TOP_K]
    hit = eid[None] == jnp.arange(N_EXPERTS, dtype=jnp.int32)[:, None, None]
    dest = (jnp.sum(jnp.where(hit, pstart[:, None, None], 0), axis=0) + ids[TOP_K:2 * TOP_K]).reshape(nk)

    xs = _dispatch(h_packed, dest, fill_rows, n_blocks * blk, blk)
    yb = _expert_blocks(xs, counts, pstart, fill_rows, w1, w3, w2, blk)
    out = _combine(x1, seq, g2, gates, dest, yb)
    return out.reshape(bsz, seq, d)


def kernel(x, c, w_ada, b_ada, norm1_w, w_in, forget_bias, q_norm_w, k_norm_w, ret_norm_w, w_out, norm2_w,
           w_coarse, b_coarse, w_fine, b_fine, w1, w3, w2):
    c_in = c
    for l in range(w_ada.shape[0]):
        x = _layer(x, c_in, w_ada[l], b_ada[l], norm1_w[l], w_in[l], forget_bias[l], q_norm_w[l],
                   k_norm_w[l], ret_norm_w[l], w_out[l], norm2_w[l], w_coarse[l], b_coarse[l],
                   w_fine[l], b_fine[l], w1[l], w3[l], w2[l])
    return x
```

```python
import functools

import jax
import jax.numpy as jnp
import numpy as np
from jax import lax
from jax.experimental import pallas as pl
from jax.experimental.pallas import tpu as pltpu

HEAD_DIM = 128
N_GROUPS = 4
EXPERTS_PER_GROUP = 8
N_EXPERTS = N_GROUPS * EXPERTS_PER_GROUP
TOP_K = 2
ROPE_BASE = 10000.0
EPS = 1e-6

LANES = 128
VMEM_LIMIT = 56 * 1024 * 1024
NEG_BIG = -1e30
LOG2E = 1.4426950408889634
UNDERFLOW_LOG2 = 160.0

ADA_COLS = 1024
INPROJ_ROWS = 1024
INPROJ_COLS = 1024
ATT_BLOCK = 512
ATT_BLOCKS_PER_STEP = 4
RET_CHUNK = 256
OUTPROJ_ROWS = 512
DISPATCH_TOKENS = 1024
COMBINE_TOKENS = 512
EXPERT_BLOCK = 256
EXPERT_AHEAD = 3

F32 = jnp.float32
BF16 = jnp.bfloat16
U32 = jnp.uint32


def _params(*sem):
    return pltpu.CompilerParams(dimension_semantics=sem, vmem_limit_bytes=VMEM_LIMIT)


def _silu(v):
    return v * (1.0 / (1.0 + jnp.exp(-v)))


def _pack_halves(y):
    w = y.shape[1] // 2
    lo = pltpu.bitcast(y[:, :w].astype(BF16).astype(F32), U32)
    hi = pltpu.bitcast(y[:, w:].astype(BF16).astype(F32), U32)
    return (hi & jnp.uint32(0xFFFF0000)) | (lo >> 16)


def _rows_to_tiles(p):
    return pltpu.einshape("m(ck)->mck", p, c=8, k=LANES)


def _tiles_to_rows(t):
    return pltpu.einshape("mck->m(ck)", t)


def _unpack_halves(p):
    lo = pltpu.bitcast(p << 16, F32)
    hi = pltpu.bitcast(p & jnp.uint32(0xFFFF0000), F32)
    return lo, hi


def _ada_kernel(bsz, ct_ref, w_ref, b_ref, o_ref):
    w = w_ref[...]
    rows = []
    for b in range(o_ref.shape[0]):
        if b < bsz:
            cb = _silu(ct_ref[:, b:b + 1])
            rows.append(jnp.sum(cb * w, axis=0, keepdims=True) + b_ref[...])
        else:
            rows.append(jnp.zeros_like(b_ref[...]))
    o_ref[...] = jnp.concatenate(rows, axis=0)


def _ada_modulation(c, w_ada, b_ada):
    bsz, d = c.shape
    n = w_ada.shape[1]
    tn = ADA_COLS
    ct = jnp.zeros((d, LANES), F32).at[:, :bsz].set(c.T)
    assert bsz <= 8, "one sublane tile of modulation rows"
    out = pl.pallas_call(
        functools.partial(_ada_kernel, bsz),
        grid=(n // tn,),
        in_specs=[pl.BlockSpec((d, LANES), lambda j: (0, 0)),
                  pl.BlockSpec((d, tn), lambda j: (0, j)),
                  pl.BlockSpec((1, tn), lambda j: (0, j))],
        out_specs=pl.BlockSpec((8, tn), lambda j: (0, j)),
        out_shape=jax.ShapeDtypeStruct((8, n), F32),
        compiler_params=_params("arbitrary"),
    )(ct, w_ada, b_ada.reshape(1, n))
    return out[:bsz]


def _inproj_kernel(q_t, r_t, x_ref, nw_ref, sc_ref, sh_ref, wa_ref, wb_ref, wf_ref, cos_ref, sin_ref,
                   qw_ref, kw_ref, fb_ref, z_ref, f_ref, h_ref):
    j = pl.program_id(1)
    r0 = 3 * q_t

    @pl.when(j == 0)
    def _():
        x = x_ref[...]
        ms = jnp.mean(x * x, axis=-1, keepdims=True)
        y = x * lax.rsqrt(ms + EPS) * nw_ref[...]
        h = (y * (1.0 + sc_ref[0]) + sh_ref[0]).astype(BF16)
        h_ref[...] = h
        t = jnp.dot(h, wf_ref[...], preferred_element_type=F32) + fb_ref[...]
        f_ref[...] = jnp.minimum(t, 0.0) - jnp.log(1.0 + jnp.exp(-jnp.abs(t)))

    def heads_of(acc):
        return [acc[:, hh * HEAD_DIM:(hh + 1) * HEAD_DIM] for hh in range(acc.shape[1] // HEAD_DIM)]

    def head_norm(acc, w_row):
        outs = []
        for a in heads_of(acc):
            ms = jnp.mean(a * a, axis=-1, keepdims=True)
            outs.append(a * lax.rsqrt(ms + EPS) * w_row)
        return jnp.concatenate(outs, axis=-1).astype(BF16)

    def rotate(acc, scale):
        cs = cos_ref[...] * scale
        sn = sin_ref[...] * scale
        outs = [a * cs + pltpu.roll(a, HEAD_DIM // 2, 1) * sn for a in heads_of(acc)]
        return jnp.concatenate(outs, axis=-1).astype(BF16)

    def fox():
        return jnp.dot(h_ref[...], wa_ref[...], preferred_element_type=F32)

    def ret():
        return jnp.dot(h_ref[...], wb_ref[...], preferred_element_type=F32)

    @pl.when(j < q_t)
    def _():
        z_ref[...] = head_norm(fox(), qw_ref[...] * (LOG2E * HEAD_DIM ** -0.5))

    @pl.when((j >= q_t) & (j < 2 * q_t))
    def _():
        z_ref[...] = head_norm(fox(), kw_ref[...])

    @pl.when((j >= 2 * q_t) & (j < r0))
    def _():
        z_ref[...] = fox().astype(BF16)

    @pl.when((j >= r0) & (j < r0 + r_t))
    def _():
        z_ref[...] = rotate(ret(), 1.0)

    @pl.when((j >= r0 + r_t) & (j < r0 + 2 * r_t))
    def _():
        z_ref[...] = rotate(ret(), HEAD_DIM ** -0.5)

    @pl.when(j >= r0 + 2 * r_t)
    def _():
        z_ref[...] = ret().astype(BF16)


def _input_projection(x2d, seq, norm_w, sc1, sh1, w_all, w_ret, fox_cols, cos_t, sin_t, qw, kw, fb):
    n, d = x2d.shape
    tm, tn = min(INPROJ_ROWS, seq), INPROJ_COLS
    fox_tiles = fox_cols // tn
    ret_tiles = w_ret.shape[1] // tn
    tiles_per_seq = seq // tm
    kern = functools.partial(_inproj_kernel, fox_tiles // 3, ret_tiles // 4)
    bsel = lambda i, j: (i // tiles_per_seq, 0, 0)
    const = lambda i, j: (0, 0)
    return pl.pallas_call(
        kern,
        grid=(n // tm, fox_tiles + ret_tiles),
        in_specs=[pl.BlockSpec((tm, d), lambda i, j: (i, 0)),
                  pl.BlockSpec((1, d), const),
                  pl.BlockSpec((1, 1, d), bsel),
                  pl.BlockSpec((1, 1, d), bsel),
                  pl.BlockSpec((d, tn), lambda i, j: (0, jnp.minimum(j, fox_tiles - 1))),
                  pl.BlockSpec((d, tn), lambda i, j: (0, jnp.maximum(j - fox_tiles, 0))),
                  pl.BlockSpec((d, LANES), lambda i, j: (0, fox_cols // LANES)),
                  pl.BlockSpec((tm, HEAD_DIM), lambda i, j: (i % tiles_per_seq, 0)),
                  pl.BlockSpec((tm, HEAD_DIM), lambda i, j: (i % tiles_per_seq, 0)),
                  pl.BlockSpec((1, HEAD_DIM), const),
                  pl.BlockSpec((1, HEAD_DIM), const),
                  pl.BlockSpec((1, LANES), const)],
        out_specs=[pl.BlockSpec((tm, tn), lambda i, j: (i, j)),
                   pl.BlockSpec((tm, LANES), lambda i, j: (i, 0))],
        out_shape=[jax.ShapeDtypeStruct((n, fox_cols + w_ret.shape[1]), BF16),
                   jax.ShapeDtypeStruct((n, LANES), F32)],
        scratch_shapes=[pltpu.VMEM((tm, d), BF16)],
        compiler_params=_params("arbitrary", "arbitrary"),
    )(x2d, norm_w, sc1, sh1, w_all, w_ret, w_all, cos_t, sin_t, qw, kw, fb)


def _cumsum_kernel(x_ref, o_ref):
    x = x_ref[0]
    r = x.shape[0]
    a = lax.broadcasted_iota(jnp.int32, (LANES, LANES), 0)
    b = lax.broadcasted_iota(jnp.int32, (LANES, LANES), 1)
    upper = (a <= b).astype(F32)
    within = jnp.dot(x, upper, precision=lax.Precision.HIGHEST, preferred_element_type=F32)
    tot = jnp.broadcast_to(within[:, LANES - 1:LANES], (r, LANES))
    ra = lax.broadcasted_iota(jnp.int32, (r, r), 0)
    rb = lax.broadcasted_iota(jnp.int32, (r, r), 1)
    strict = (rb < ra).astype(F32)
    before = jnp.dot(strict, tot, precision=lax.Precision.HIGHEST, preferred_element_type=F32)
    o_ref[0] = within + before


def _cumsum_rows(x):
    g, s = x.shape
    r = s // LANES
    out = pl.pallas_call(
        _cumsum_kernel,
        grid=(g,),
        in_specs=[pl.BlockSpec((1, r, LANES), lambda i: (i, 0, 0))],
        out_specs=pl.BlockSpec((1, r, LANES), lambda i: (i, 0, 0)),
        out_shape=jax.ShapeDtypeStruct((g, r, LANES), F32),
        compiler_params=_params("arbitrary"),
    )(x.reshape(g, r, LANES))
    return out.reshape(g, 1, s)


def _fox_kernel(tq, n_sub, first_ref, q_ref, k_ref, v_ref, cum_ref, o_ref, s_refs, m_ref, l_ref, acc_ref):
    group_id = pl.program_id(2)
    n_groups = pl.num_programs(2)
    head = pl.program_id(0) * pl.num_programs(1) + pl.program_id(1)
    n_slabs = tq // LANES

    m_ref[...] = jnp.full(m_ref.shape, NEG_BIG, F32)
    l_ref[...] = jnp.zeros(l_ref.shape, F32)
    acc_ref[...] = jnp.zeros(acc_ref.shape, F32)

    class Sub:
        def __init__(self, idx):
            self.rows = slice(idx * tq, (idx + 1) * tq)
            self.qi = n_sub * group_id + idx
            self.sa, self.sb = s_refs[2 * idx], s_refs[2 * idx + 1]
            q_start = pl.multiple_of(self.qi * tq, tq)
            self.c0 = cum_ref[0, :, pl.ds(q_start, LANES)][:, 0:1]
            self.first = first_ref[(head * n_groups + group_id) * n_sub + idx]
            self.n_off = self.qi - self.first

    def scores(sub, kb, s_ref):
        start = pl.multiple_of(kb * tq, tq)
        k = k_ref[pl.ds(start, tq), :]
        bias = (sub.c0 - cum_ref[0, :, pl.ds(start, tq)]) * LOG2E
        s_ref[...] = lax.dot_general(q_ref[sub.rows, :], k, (((1,), (1,)), ((), ())),
                                     preferred_element_type=F32) + bias

    def softmax_pv(sub, kb, s_ref, masked):
        start = pl.multiple_of(kb * tq, tq)
        if masked:
            row = lax.broadcasted_iota(jnp.int32, (LANES, LANES), 0)
            col = lax.broadcasted_iota(jnp.int32, (LANES, LANES), 1)
            for g in range(n_slabs):
                r0 = g * LANES
                slabs = [s_ref[r0:r0 + LANES, j * LANES:(j + 1) * LANES] for j in range(g + 1)]
                slabs[g] = jnp.where(col <= row, slabs[g], NEG_BIG)
                rs = slice(sub.rows.start + r0, sub.rows.start + r0 + LANES)
                update(rs, slabs, v_ref[pl.ds(start, (g + 1) * LANES), :])
        else:
            slabs = [s_ref[:, j * LANES:(j + 1) * LANES] for j in range(n_slabs)]
            update(sub.rows, slabs, v_ref[pl.ds(start, tq), :])

    def update(rs, slabs, v):
        mx = slabs[0]
        for t in slabs[1:]:
            mx = jnp.maximum(mx, t)
        m_prev = m_ref[rs, :]
        m_new = jnp.maximum(m_prev, jnp.max(mx, axis=-1, keepdims=True))
        alpha = jnp.exp2(m_prev - m_new)
        probs = [jnp.exp2(t - m_new) for t in slabs]
        psum = probs[0]
        for t in probs[1:]:
            psum = psum + t
        l_ref[rs, :] = alpha * l_ref[rs, :] + psum
        p = jnp.concatenate([t.astype(BF16) for t in probs], axis=-1)
        acc_ref[rs, :] = alpha * acc_ref[rs, :] + jnp.dot(p, v, preferred_element_type=F32)
        m_ref[rs, :] = m_new

    def sweep(sub, then):
        def pair(kb):
            scores(sub, kb + 1, sub.sb)
            softmax_pv(sub, kb, sub.sa, False)
            scores(sub, kb + 2, sub.sa)
            softmax_pv(sub, kb + 1, sub.sb, False)

        def body4(i, carry):
            pair(sub.first + 4 * i)
            pair(sub.first + 4 * i + 2)
            return carry

        def body2(i, carry):
            pair(sub.first + 2 * i)
            return carry

        n4 = sub.n_off // 4
        lax.fori_loop(0, n4, body4, 0)
        lax.fori_loop(2 * n4, sub.n_off // 2, body2, 0)

        @pl.when(sub.n_off % 2 == 0)
        def _():
            then()
            softmax_pv(sub, sub.qi, sub.sa, True)

        @pl.when(sub.n_off % 2 == 1)
        def _():
            scores(sub, sub.qi, sub.sb)
            softmax_pv(sub, sub.qi - 1, sub.sa, False)
            then()
            softmax_pv(sub, sub.qi, sub.sb, True)

    subs = [Sub(idx) for idx in range(n_sub)]
    scores(subs[0], subs[0].first, subs[0].sa)
    for sub, nxt in zip(subs, subs[1:] + [None]):
        sweep(sub, (lambda: None) if nxt is None else functools.partial(scores, nxt, nxt.first, nxt.sa))

    o_ref[...] = (acc_ref[...] / jnp.sum(l_ref[...], axis=-1, keepdims=True)).astype(BF16)


def _first_live_block(cum, tq, qk_bound):
    c0 = cum[:, 0, ::tq]
    cend = cum[:, 0, tq - 1::tq]
    gap = (c0[:, :, None] - cend[:, None, :]) * LOG2E + 2.0 * qk_bound
    nq = c0.shape[1]
    earlier = jnp.arange(nq)[None, :] < jnp.arange(nq)[:, None]
    return jnp.sum((gap < -UNDERFLOW_LOG2) & earlier[None], axis=-1).astype(jnp.int32).reshape(-1)


def _fox_attention(z, cum, qk_bound, bsz, seq, n_heads):
    n_sub = ATT_BLOCKS_PER_STEP
    tq = min(ATT_BLOCK, seq // n_sub)
    nq = seq // tq
    n_groups = nq // n_sub
    rows = n_sub * tq
    kern = functools.partial(_fox_kernel, tq, n_sub)
    grid_spec = pltpu.PrefetchScalarGridSpec(
        num_scalar_prefetch=1,
        grid=(bsz, n_heads, n_groups),
        in_specs=[pl.BlockSpec((rows, HEAD_DIM), lambda b, h, i, f: (b * n_groups + i, h)),
                  pl.BlockSpec((seq, HEAD_DIM), lambda b, h, i, f: (b, n_heads + h)),
                  pl.BlockSpec((seq, HEAD_DIM), lambda b, h, i, f: (b, 2 * n_heads + h)),
                  pl.BlockSpec((1, 1, seq), lambda b, h, i, f: (b * n_heads + h, 0, 0))],
        out_specs=pl.BlockSpec((rows, HEAD_DIM), lambda b, h, i, f: (b * n_groups + i, h)),
        scratch_shapes=[[pltpu.VMEM((tq, tq), F32)] * (2 * n_sub),
                        pltpu.VMEM((rows, LANES), F32), pltpu.VMEM((rows, LANES), F32),
                        pltpu.VMEM((rows, HEAD_DIM), F32)],
    )
    return pl.pallas_call(
        kern,
        grid_spec=grid_spec,
        out_shape=jax.ShapeDtypeStruct((bsz * seq, n_heads * HEAD_DIM), BF16),
        compiler_params=_params("arbitrary", "arbitrary", "arbitrary"),
    )(_first_live_block(cum, tq, qk_bound), z, z, z, cum)


def _ret_kernel(chunk, n_heads, lg_ref, q_ref, k_ref, v_ref, g_ref, nw_ref, o_ref, state_ref, decay_ref,
                qdec_ref, kdec_ref):
    first = (pl.program_id(0) == 0) & (pl.program_id(1) == 0)

    @pl.when(first)
    def _():
        i = lax.broadcasted_iota(jnp.int32, (chunk, chunk), 0)
        jj = lax.broadcasted_iota(jnp.int32, (chunk, chunk), 1)
        diff = (i - jj).astype(F32)
        pos = lax.broadcasted_iota(jnp.int32, (chunk, HEAD_DIM), 0).astype(F32)
        for h in range(n_heads):
            decay_ref[h] = jnp.where(diff >= 0, jnp.exp(lg_ref[h] * jnp.maximum(diff, 0.0)), 0.0)
            qdec_ref[h] = jnp.exp(lg_ref[h] * (pos + 1.0))
            kdec_ref[h] = jnp.exp(lg_ref[h] * (chunk - 1.0 - pos))

    @pl.when(pl.program_id(1) == 0)
    def _():
        state_ref[...] = jnp.zeros(state_ref.shape, F32)

    for h in range(n_heads):
        log_g = lg_ref[h]
        cols = slice(h * HEAD_DIM, (h + 1) * HEAD_DIM)
        q = q_ref[:, cols]
        k = k_ref[:, cols]
        v = v_ref[:, cols]
        scores = lax.dot_general(q, k, (((1,), (1,)), ((), ())), preferred_element_type=F32)
        scores = scores * decay_ref[h]
        intra = jnp.dot(scores.astype(BF16), v, preferred_element_type=F32)
        state = state_ref[h]
        inter = jnp.dot(q, state.astype(BF16), preferred_element_type=F32) * qdec_ref[h]
        kd = (k.astype(F32) * kdec_ref[h]).astype(BF16)
        kv = lax.dot_general(kd, v, (((0,), (0,)), ((), ())), preferred_element_type=F32)
        state_ref[h] = state * jnp.exp(jnp.full((1, HEAD_DIM), chunk, F32) * log_g) + kv
        o = intra + inter
        ms = jnp.mean(o * o, axis=-1, keepdims=True)
        o = o * lax.rsqrt(ms + EPS) * nw_ref[:, cols]
        o_ref[:, cols] = (o * _silu(g_ref[:, cols].astype(F32))).astype(BF16)


def _retention(z, log_g, norm_w, bsz, seq, n_heads, col0):
    chunk = min(RET_CHUNK, seq)
    nt = seq // chunk
    width = n_heads * HEAD_DIM
    c0 = col0 // width
    kern = functools.partial(_ret_kernel, chunk, n_heads)

    def sec(s):
        return pl.BlockSpec((chunk, width), lambda b, t, lg: (b * nt + t, c0 + s))

    grid_spec = pltpu.PrefetchScalarGridSpec(
        num_scalar_prefetch=1,
        grid=(bsz, nt),
        in_specs=[sec(0), sec(1), sec(2), sec(3), pl.BlockSpec((1, width), lambda b, t, lg: (0, 0))],
        out_specs=pl.BlockSpec((chunk, width), lambda b, t, lg: (b * nt + t, 0)),
        scratch_shapes=[pltpu.VMEM((n_heads, HEAD_DIM, HEAD_DIM), F32),
                        pltpu.VMEM((n_heads, chunk, chunk), F32),
                        pltpu.VMEM((n_heads, chunk, HEAD_DIM), F32),
                        pltpu.VMEM((n_heads, chunk, HEAD_DIM), F32)],
    )
    return pl.pallas_call(
        kern,
        grid_spec=grid_spec,
        out_shape=jax.ShapeDtypeStruct((bsz * seq, width), BF16),
        compiler_params=_params("arbitrary", "arbitrary"),
    )(log_g, z, z, z, z, norm_w)


def _outproj_kernel(blk, n_blocks, oa_ref, ob_ref, wa_ref, wb_ref, x_ref, g1_ref, nw_ref, sc_ref, sh_ref,
                    wr_ref, br_ref, x1_ref, hp_ref, gate_ref, ids_ref, plan_ref, lg_ref, run_ref):
    i = pl.program_id(0)

    @pl.when(i == 0)
    def _():
        run_ref[...] = jnp.zeros(run_ref.shape, F32)
        lg_ref[...] = jnp.zeros(lg_ref.shape, F32)

    lg_prev = lg_ref[...]

    mix = jnp.dot(oa_ref[...], wa_ref[...], preferred_element_type=F32)
    mix = mix + jnp.dot(ob_ref[...], wb_ref[...], preferred_element_type=F32)
    x1 = x_ref[...] + g1_ref[0] * mix
    x1_ref[...] = x1
    ms = jnp.mean(x1 * x1, axis=-1, keepdims=True)
    h2 = x1 * lax.rsqrt(ms + EPS) * nw_ref[...] * (1.0 + sc_ref[0]) + sh_ref[0]
    hp_ref[...] = _rows_to_tiles(_pack_halves(h2))
    h_hi = h2.astype(BF16)
    h_lo = (h2 - h_hi.astype(F32)).astype(BF16)
    both = jnp.dot(h_hi, wr_ref[...], preferred_element_type=F32)
    cross = jnp.dot(h_lo, wr_ref[:, :LANES], preferred_element_type=F32)
    lg_ref[...] = both[:, :LANES] + both[:, LANES:] + cross + br_ref[...]

    gate, ids = _route_tile(lg_prev, run_ref, jnp.where(i > 0, 1.0, 0.0))
    gate_ref[...] = gate
    ids_ref[...] = ids

    @pl.when(i == pl.num_programs(0) - 1)
    def _():
        _slot_plan(run_ref, plan_ref, blk, n_blocks)


def _output_projection(o_a, o_b, w_out, x2d, seq, g1, norm_w, sc2, sh2, w_router, b_router, blk, n_blocks):
    n, d = x2d.shape
    da = o_a.shape[1]
    tm = min(OUTPROJ_ROWS, seq)
    n_tiles = n // tm
    tiles_per_seq = seq // tm
    cur = lambda i: jnp.minimum(i, n_tiles - 1)
    prev = lambda i: jnp.maximum(i - 1, 0)
    bsel = lambda i: (cur(i) // tiles_per_seq, 0, 0)
    return pl.pallas_call(
        functools.partial(_outproj_kernel, blk, n_blocks),
        grid=(n_tiles + 1,),
        in_specs=[pl.BlockSpec((tm, da), lambda i: (cur(i), 0)),
                  pl.BlockSpec((tm, da), lambda i: (cur(i), 0)),
                  pl.BlockSpec((da, d), lambda i: (0, 0)),
                  pl.BlockSpec((da, d), lambda i: (1, 0)),
                  pl.BlockSpec((tm, d), lambda i: (cur(i), 0)),
                  pl.BlockSpec((1, 1, d), bsel),
                  pl.BlockSpec((1, d), lambda i: (0, 0)),
                  pl.BlockSpec((1, 1, d), bsel),
                  pl.BlockSpec((1, 1, d), bsel),
                  pl.BlockSpec((d, 2 * LANES), lambda i: (0, 0)),
                  pl.BlockSpec((1, LANES), lambda i: (0, 0))],
        out_specs=[pl.BlockSpec((tm, d), lambda i: (cur(i), 0)),
                   pl.BlockSpec((tm, d // 2 // LANES, LANES), lambda i: (cur(i), 0, 0)),
                   pl.BlockSpec((tm, LANES), lambda i: (prev(i), 0)),
                   pl.BlockSpec((8, tm), lambda i: (0, prev(i))),
                   pl.BlockSpec((8, LANES), lambda i: (0, 0))],
        out_shape=[jax.ShapeDtypeStruct((n, d), F32),
                   jax.ShapeDtypeStruct((n, d // 2 // LANES, LANES), U32),
                   jax.ShapeDtypeStruct((n, LANES), F32),
                   jax.ShapeDtypeStruct((8, n), jnp.int32),
                   jax.ShapeDtypeStruct((8, LANES), jnp.int32)],
        scratch_shapes=[pltpu.VMEM((tm, LANES), F32), pltpu.VMEM((1, LANES), F32)],
        compiler_params=_params("arbitrary"),
    )(o_a, o_b, w_out, w_out, x2d, g1, norm_w, sc2, sh2, w_router, b_router)


def _route_tile(lg, run_ref, live):
    tt = lg.shape[0]
    lane = lax.broadcasted_iota(jnp.int32, lg.shape, 1).astype(F32)
    big = 1e6

    def rmax(v):
        return jnp.max(v, axis=-1, keepdims=True)

    def rmin(v):
        return jnp.min(v, axis=-1, keepdims=True)

    def rsum(v):
        return jnp.sum(v, axis=-1, keepdims=True)

    cmask = lane < N_GROUPS
    cm = jnp.where(cmask, lg, NEG_BIG)
    ce = jnp.where(cmask, jnp.exp(cm - rmax(cm)), 0.0)
    pgrp = ce / rsum(ce)
    p_g = rmax(pgrp)
    g_sel = rmin(jnp.where(cmask & (pgrp == p_g), lane, big))

    lo = N_GROUPS + EXPERTS_PER_GROUP * g_sel
    fmask = (lane >= lo) & (lane < lo + EXPERTS_PER_GROUP)
    fm = jnp.where(fmask, lg, NEG_BIG)
    fe = jnp.where(fmask, jnp.exp(fm - rmax(fm)), 0.0)
    fp = fe / rsum(fe)
    fp = jnp.where(fmask, fp, -1.0)
    p1 = rmax(fp)
    i1 = rmin(jnp.where(fp == p1, lane, big))
    fp2 = jnp.where(lane == i1, -1.0, fp)
    p2 = rmax(fp2)
    i2 = rmin(jnp.where(fp2 == p2, lane, big))
    denom = p1 + p2
    w1 = p_g * p1 / denom
    w2 = p_g * p2 / denom
    e1 = i1 - N_GROUPS
    e2 = i2 - N_GROUPS

    gate = jnp.where(lane == 0, w1, jnp.where(lane == 1, w2, 0.0))

    oh1 = (lane == e1).astype(F32)
    oh2 = (lane == e2).astype(F32)
    both = oh1 + oh2
    ra = lax.broadcasted_iota(jnp.int32, (tt, tt), 0)
    rb = lax.broadcasted_iota(jnp.int32, (tt, tt), 1)
    strict = (rb < ra).astype(BF16)
    prefix = jnp.dot(strict, both.astype(BF16), preferred_element_type=F32) + run_ref[...]
    r1 = rsum(prefix * oh1)
    r2 = rsum(prefix * oh2)
    run_ref[...] = run_ref[...] + live * jnp.sum(both, axis=0, keepdims=True)

    packed = jnp.where(lane == 0, e1, jnp.where(lane == 1, e2, jnp.where(lane == 2, r1,
                                                                        jnp.where(lane == 3, r2, 0.0))))
    return gate, jnp.transpose(packed)[:8, :].astype(jnp.int32)


def _slot_plan(run_ref, plan_ref, blk, n_blocks):
    cnt = jnp.broadcast_to(run_ref[...], (8, LANES))
    lane8 = lax.broadcasted_iota(jnp.int32, (8, LANES), 1)
    padded = jnp.floor((cnt + (blk - 1.0)) * (1.0 / blk)) * blk
    pend = padded
    for sh in (1, 2, 4, 8, 16, 32, 64):
        pend = pend + jnp.where(lane8 >= sh, pltpu.roll(pend, sh, 1), 0.0)
    pstart = pend - padded
    total = jnp.max(pend, axis=-1, keepdims=True)
    tail = total + (lane8 - N_EXPERTS).astype(F32) * blk
    fill = jnp.where(lane8 < N_EXPERTS, jnp.where(padded > 0, pend - blk, -1.0),
                     jnp.where((lane8 < 2 * N_EXPERTS) & (tail < n_blocks * blk), tail, -1.0))
    row8 = lax.broadcasted_iota(jnp.int32, (8, LANES), 0)
    plan_ref[...] = jnp.where(row8 == 0, pstart, jnp.where(row8 == 1, fill,
                                                           jnp.where(row8 == 2, cnt, 0.0))).astype(jnp.int32)


def _dispatch_kernel(tt, blk, n_fill, dest_ref, fill_ref, h_ref, xs_ref, zero_ref, sem, zsem):
    i = pl.program_id(0)
    n_tok = pl.num_programs(0) * tt

    @pl.when(i == 0)
    def _():
        zero_ref[...] = jnp.zeros(zero_ref.shape, U32)

        def zcopy(z):
            row = pl.multiple_of(jnp.maximum(fill_ref[z], 0), blk)
            return pltpu.make_async_copy(zero_ref, xs_ref.at[pl.ds(row, blk)], zsem)

        def zissue(z, carry):
            @pl.when(fill_ref[z] >= 0)
            def _():
                zcopy(z).start()
            return carry

        def zdrain(z, carry):
            @pl.when(fill_ref[z] >= 0)
            def _():
                zcopy(z).wait()
            return carry

        lax.fori_loop(0, n_fill, zissue, 0)
        lax.fori_loop(0, n_fill, zdrain, 0)

    def copy(r, kk):
        d = dest_ref[kk * n_tok + i * tt + r]
        return pltpu.make_async_copy(h_ref.at[r], xs_ref.at[d], sem)

    def issue(r, carry):
        for kk in range(TOP_K):
            copy(r, kk).start(priority=kk % 2)
        return carry

    lax.fori_loop(0, tt, issue, 0, unroll=8)
    for _ in range(TOP_K):
        pltpu.make_async_copy(h_ref, xs_ref.at[pl.ds(0, tt)], sem).wait()


def _dispatch(h_packed, dest_flat, fill_rows, n_slots, blk):
    n = h_packed.shape[0]
    tile = h_packed.shape[1:]
    tt = min(DISPATCH_TOKENS, n)
    n_fill = fill_rows.shape[0]
    grid_spec = pltpu.PrefetchScalarGridSpec(
        num_scalar_prefetch=2,
        grid=(n // tt,),
        in_specs=[pl.BlockSpec((tt,) + tile, lambda i, d, f: (i, 0, 0))],
        out_specs=pl.BlockSpec(memory_space=pl.ANY),
        scratch_shapes=[pltpu.VMEM((blk,) + tile, U32), pltpu.SemaphoreType.DMA(()), pltpu.SemaphoreType.DMA(())],
    )
    return pl.pallas_call(
        functools.partial(_dispatch_kernel, tt, blk, n_fill),
        grid_spec=grid_spec,
        out_shape=jax.ShapeDtypeStruct((n_slots,) + tile, U32),
        compiler_params=_params("arbitrary"),
    )(dest_flat, fill_rows, h_packed)


def _expert_kernel(blk, ahead, cnt_ref, pstart_ref, fill_ref, xs_ref, w1_ref, w3_ref, w2_ref, y_ref,
                   w1f, w3f, w2f, w1b, w3b, w2b, xbuf, ybuf, done_ref, w_sem, in_sem, out_sem):
    e = pl.program_id(0)
    n_exp = pl.num_programs(0)
    wslot = e % 2
    n_blk = (cnt_ref[e] + (blk - 1)) // blk
    base = pstart_ref[e]
    n_x = xbuf.shape[0]

    def weight_copies(ex, slot):
        return [pltpu.make_async_copy(src.at[ex], dst.at[slot], w_sem.at[slot])
                for src, dst in ((w1_ref, w1f), (w3_ref, w3f), (w2_ref, w2f))]

    def rows(b):
        return pl.ds(pl.multiple_of(base + b * blk, blk), blk)

    def in_copy(b, slot):
        return pltpu.make_async_copy(xs_ref.at[rows(b)], xbuf.at[slot], in_sem.at[slot])

    def out_copy(b, slot):
        return pltpu.make_async_copy(ybuf.at[slot], y_ref.at[rows(b)], out_sem.at[slot])

    @pl.when(e == 0)
    def _():
        for c in weight_copies(0, 0):
            c.start()

    for p in range(ahead):
        @pl.when(p < n_blk)
        def _():
            in_copy(p, p).start()

    @pl.when(e + 1 < n_exp)
    def _():
        for c in weight_copies(e + 1, 1 - wslot):
            c.start()

    for c in weight_copies(e, wslot):
        c.wait()
    w1b[...] = w1f[wslot].astype(BF16)
    w3b[...] = w3f[wslot].astype(BF16)
    w2b[...] = w2f[wslot].astype(BF16)

    @pl.when(e == 0)
    def _():
        done_ref[0] = 0

    done = done_ref[0]

    def out_wait(slot):
        pltpu.make_async_copy(ybuf.at[slot], y_ref.at[pl.ds(0, blk)], out_sem.at[slot]).wait()

    def body(b, carry):
        slot = (done + b) % 2

        @pl.when(b + ahead < n_blk)
        def _():
            in_copy(b + ahead, (b + ahead) % n_x).start()

        in_copy(b, b % n_x).wait()

        @pl.when(done + b >= 2)
        def _():
            out_wait(slot)

        lo, hi = _unpack_halves(_tiles_to_rows(xbuf[b % n_x]))
        lo = lo.astype(BF16)
        hi = hi.astype(BF16)
        half = lo.shape[1]
        a = jnp.dot(lo, w1b[:half, :], preferred_element_type=F32)
        a = a + jnp.dot(hi, w1b[half:, :], preferred_element_type=F32)
        g = jnp.dot(lo, w3b[:half, :], preferred_element_type=F32)
        g = g + jnp.dot(hi, w3b[half:, :], preferred_element_type=F32)
        mid = (_silu(a) * g).astype(BF16)
        ybuf[slot] = _rows_to_tiles(_pack_halves(jnp.dot(mid, w2b[...], preferred_element_type=F32)))
        out_copy(b, slot).start(priority=1)
        return carry

    lax.fori_loop(0, n_blk, body, 0)
    total = done + n_blk
    done_ref[0] = total

    @pl.when(e == n_exp - 1)
    def _():
        @pl.when(total >= 2)
        def _():
            out_wait(total % 2)

        @pl.when(total >= 1)
        def _():
            out_wait((total - 1) % 2)

        ybuf[0] = jnp.zeros(ybuf.shape[1:], U32)

        def zcopy(t):
            row = pl.multiple_of(jnp.maximum(fill_ref[N_EXPERTS + t], 0), blk)
            return pltpu.make_async_copy(ybuf.at[0], y_ref.at[pl.ds(row, blk)], out_sem.at[0])

        def zissue(t, carry):
            @pl.when(fill_ref[N_EXPERTS + t] >= 0)
            def _():
                zcopy(t).start()
            return carry

        def zdrain(t, carry):
            @pl.when(fill_ref[N_EXPERTS + t] >= 0)
            def _():
                zcopy(t).wait()
            return carry

        lax.fori_loop(0, N_EXPERTS, zissue, 0)
        lax.fori_loop(0, N_EXPERTS, zdrain, 0)


def _expert_blocks(xs, counts, pstart, fill_rows, w1, w3, w2, blk):
    n_slots = xs.shape[0]
    tile = xs.shape[1:]
    n_exp, d, de = w1.shape
    ahead = EXPERT_AHEAD
    hbm = pl.BlockSpec(memory_space=pl.ANY)
    grid_spec = pltpu.PrefetchScalarGridSpec(
        num_scalar_prefetch=3,
        grid=(n_exp,),
        in_specs=[hbm, hbm, hbm, hbm],
        out_specs=hbm,
        scratch_shapes=[pltpu.VMEM((2, d, de), F32), pltpu.VMEM((2, d, de), F32), pltpu.VMEM((2, de, d), F32),
                        pltpu.VMEM((d, de), BF16), pltpu.VMEM((d, de), BF16), pltpu.VMEM((de, d), BF16),
                        pltpu.VMEM((ahead + 1, blk) + tile, U32), pltpu.VMEM((2, blk) + tile, U32),
                        pltpu.SMEM((1,), jnp.int32),
                        pltpu.SemaphoreType.DMA((2,)), pltpu.SemaphoreType.DMA((ahead + 1,)),
                        pltpu.SemaphoreType.DMA((2,))],
    )
    return pl.pallas_call(
        functools.partial(_expert_kernel, blk, ahead),
        grid_spec=grid_spec,
        out_shape=jax.ShapeDtypeStruct((n_slots,) + tile, U32),
        compiler_params=_params("arbitrary"),
    )(counts, pstart, fill_rows, xs, w1, w3, w2)


def _combine_kernel(tt, n_tiles, dest_ref, x1_ref, g2_ref, gate_ref, yb_ref, o_ref, buf, sems):
    i = pl.program_id(0)

    def copy(tile, slot, r, kk):
        d = dest_ref[kk * (n_tiles * tt) + tile * tt + r]
        return pltpu.make_async_copy(yb_ref.at[d], buf.at[slot, kk, r], sems.at[slot])

    def issue_tile(tile, slot):
        def body(r, carry):
            for kk in range(TOP_K):
                copy(tile, slot, r, kk).start(priority=kk % 2)
            return carry
        lax.fori_loop(0, tt, body, 0, unroll=8)

    def wait_tile(tile, slot):
        for kk in range(TOP_K):
            pltpu.make_async_copy(yb_ref.at[pl.ds(0, tt)], buf.at[slot, kk], sems.at[slot]).wait()

    slot = i % 2

    @pl.when(i == 0)
    def _():
        issue_tile(0, 0)

    @pl.when(i + 1 < n_tiles)
    def _():
        issue_tile(i + 1, 1 - slot)

    wait_tile(i, slot)

    gate = gate_ref[...]
    wa = gate[:, 0:1]
    wb = gate[:, 1:2]
    lo_a, hi_a = _unpack_halves(_tiles_to_rows(buf[slot, 0]))
    lo_b, hi_b = _unpack_halves(_tiles_to_rows(buf[slot, 1]))
    y = jnp.concatenate([wa * lo_a + wb * lo_b, wa * hi_a + wb * hi_b], axis=-1)
    o_ref[...] = x1_ref[...] + g2_ref[0] * y


def _combine(x1, seq, g2, gates, dest_flat, yb):
    n, d = x1.shape
    tile = yb.shape[1:]
    tt = min(COMBINE_TOKENS, seq)
    n_tiles = n // tt
    tiles_per_seq = seq // tt
    grid_spec = pltpu.PrefetchScalarGridSpec(
        num_scalar_prefetch=1,
        grid=(n_tiles,),
        in_specs=[pl.BlockSpec((tt, d), lambda i, dr: (i, 0)),
                  pl.BlockSpec((1, 1, d), lambda i, dr: (i // tiles_per_seq, 0, 0)),
                  pl.BlockSpec((tt, LANES), lambda i, dr: (i, 0)),
                  pl.BlockSpec(memory_space=pl.ANY)],
        out_specs=pl.BlockSpec((tt, d), lambda i, dr: (i, 0)),
        scratch_shapes=[pltpu.VMEM((2, TOP_K, tt) + tile, U32), pltpu.SemaphoreType.DMA((2,))],
    )
    return pl.pallas_call(
        functools.partial(_combine_kernel, tt, n_tiles),
        grid_spec=grid_spec,
        out_shape=jax.ShapeDtypeStruct((n, d), F32),
        compiler_params=_params("arbitrary"),
    )(dest_flat, x1, g2, gates, yb)


def _rotation_tables(seq):
    half = HEAD_DIM // 2
    theta = ROPE_BASE ** (-np.arange(half, dtype=np.float64) / half)
    ang = np.arange(seq, dtype=np.float64)[:, None] * theta[None, :]
    cos_t = np.concatenate([np.cos(ang), np.cos(ang)], axis=-1).astype(np.float32)
    sin_t = np.concatenate([-np.sin(ang), np.sin(ang)], axis=-1).astype(np.float32)
    return jnp.asarray(cos_t), jnp.asarray(sin_t)


def _layer(x, c, w_ada, b_ada, norm1_w, w_in, forget_bias, q_norm_w, k_norm_w, ret_norm_w, w_out, norm2_w,
           w_coarse, b_coarse, w_fine, b_fine, w1, w3, w2):
    bsz, seq, d = x.shape
    n = bsz * seq
    d_fox = d // 2
    d_ret = d // 2
    n_heads = d_fox // HEAD_DIM

    mod = _ada_modulation(c, w_ada, b_ada)
    sh1, sc1, g1, sh2, sc2, g2 = [m.reshape(bsz, 1, d) for m in jnp.split(mod, 6, axis=-1)]

    f0 = 3 * d_fox
    w_all = w_in.astype(BF16)
    w_ret = w_all[:, f0 + n_heads:]
    fb = jnp.zeros((1, LANES), F32).at[0, :n_heads].set(forget_bias)

    cos_t, sin_t = _rotation_tables(seq)

    x2d = x.reshape(n, d)
    z, log_f = _input_projection(x2d, seq, norm1_w.reshape(1, d), sc1, sh1, w_all, w_ret, f0, cos_t, sin_t,
                                 q_norm_w.reshape(1, HEAD_DIM), k_norm_w.reshape(1, HEAD_DIM), fb)

    lf = log_f[:, :n_heads].reshape(bsz, seq, n_heads).transpose(0, 2, 1).reshape(bsz * n_heads, seq)
    cum = _cumsum_rows(lf)

    qk_bound = 1.02 * LOG2E * HEAD_DIM ** 0.5 * jnp.max(jnp.abs(q_norm_w)) * jnp.max(jnp.abs(k_norm_w))
    o_a = _fox_attention(z, cum, qk_bound, bsz, seq, n_heads)
    log_g = jnp.log(1.0 - 2.0 ** (-5.0 - jnp.arange(n_heads, dtype=F32)))
    o_b = _retention(z, log_g, ret_norm_w.reshape(1, d_ret), bsz, seq, n_heads, 3 * d_fox)

    pad = LANES - N_GROUPS - N_EXPERTS
    w_router = jnp.concatenate([w_coarse, w_fine.transpose(1, 0, 2).reshape(d, N_EXPERTS),
                                jnp.zeros((d, pad), F32)], axis=1)
    b_router = jnp.concatenate([b_coarse, b_fine.reshape(N_EXPERTS), jnp.zeros((pad,), F32)]).reshape(1, LANES)

    wr_hi = w_router.astype(BF16)
    wr_lo = (w_router - wr_hi.astype(F32)).astype(BF16)
    blk = EXPERT_BLOCK
    nk = n * TOP_K
    n_blocks = nk // blk + N_EXPERTS
    x1, h_packed, gates, ids, plan = _output_projection(
        o_a, o_b, w_out.astype(BF16), x2d, seq, g1, norm2_w.reshape(1, d), sc2, sh2,
        jnp.concatenate([wr_hi, wr_lo], axis=1), b_router, blk, n_blocks)
    pstart = plan[0, :N_EXPERTS]
    fill_rows = plan[1, :2 * N_EXPERTS]
    counts = plan[2, :N_EXPERTS]
    eid = ids[0:TOP_K]
    hit = eid[None] == jnp.arange(N_EXPERTS, dtype=jnp.int32)[:, None, None]
    dest = (jnp.sum(jnp.where(hit, pstart[:, None, None], 0), axis=0) + ids[TOP_K:2 * TOP_K]).reshape(nk)

    xs = _dispatch(h_packed, dest, fill_rows, n_blocks * blk, blk)
    yb = _expert_blocks(xs, counts, pstart, fill_rows, w1, w3, w2, blk)
    out = _combine(x1, seq, g2, gates, dest, yb)
    return out.reshape(bsz, seq, d)


def kernel(x, c, w_ada, b_ada, norm1_w, w_in, forget_bias, q_norm_w, k_norm_w, ret_norm_w, w_out, norm2_w,
           w_coarse, b_coarse, w_fine, b_fine, w1, w3, w2):
    c_in = c
    for l in range(w_ada.shape[0]):
        x = _layer(x, c_in, w_ada[l], b_ada[l], norm1_w[l], w_in[l], forget_bias[l], q_norm_w[l],
                   k_norm_w[l], ret_norm_w[l], w_out[l], norm2_w[l], w_coarse[l], b_coarse[l],
                   w_fine[l], b_fine[l], w1[l], w3[l], w2[l])
    return x
```

```python
import functools

import jax
import jax.numpy as jnp
import numpy as np
from jax import lax
from jax.experimental import pallas as pl
from jax.experimental.pallas import tpu as pltpu

HEAD_DIM = 128
N_GROUPS = 4
EXPERTS_PER_GROUP = 8
N_EXPERTS = N_GROUPS * EXPERTS_PER_GROUP
TOP_K = 2
ROPE_BASE = 10000.0
EPS = 1e-6

LANES = 128
VMEM_LIMIT = 56 * 1024 * 1024
NEG_BIG = -1e30
LOG2E = 1.4426950408889634
UNDERFLOW_LOG2 = 160.0

ADA_COLS = 1024
INPROJ_ROWS = 1024
INPROJ_COLS = 1024
ATT_BLOCK = 512
ATT_BLOCKS_PER_STEP = 8
RET_CHUNK = 256
OUTPROJ_ROWS = 512
ROUTE_ROWS = 512
DISPATCH_TOKENS = 2048
COMBINE_TOKENS = 512
EXPERT_BLOCK = 256
EXPERT_AHEAD = 3

F32 = jnp.float32
BF16 = jnp.bfloat16
U32 = jnp.uint32


def _params(*sem):
    return pltpu.CompilerParams(dimension_semantics=sem, vmem_limit_bytes=VMEM_LIMIT)


def _silu(v):
    return v * (1.0 / (1.0 + jnp.exp(-v)))


def _pack_halves(y):
    w = y.shape[1] // 2
    lo = pltpu.bitcast(y[:, :w].astype(BF16).astype(F32), U32)
    hi = pltpu.bitcast(y[:, w:].astype(BF16).astype(F32), U32)
    return (hi & jnp.uint32(0xFFFF0000)) | (lo >> 16)


def _rows_to_tiles(p):
    return pltpu.einshape("m(ck)->mck", p, c=8, k=LANES)


def _tiles_to_rows(t):
    return pltpu.einshape("mck->m(ck)", t)


def _unpack_halves(p):
    lo = pltpu.bitcast(p << 16, F32)
    hi = pltpu.bitcast(p & jnp.uint32(0xFFFF0000), F32)
    return lo, hi


def _ada_kernel(bsz, ct_ref, w_ref, b_ref, o_ref):
    w = w_ref[...]
    rows = []
    for b in range(o_ref.shape[0]):
        if b < bsz:
            cb = _silu(ct_ref[:, b:b + 1])
            rows.append(jnp.sum(cb * w, axis=0, keepdims=True) + b_ref[...])
        else:
            rows.append(jnp.zeros_like(b_ref[...]))
    o_ref[...] = jnp.concatenate(rows, axis=0)


def _ada_modulation(c, w_ada, b_ada):
    bsz, d = c.shape
    n = w_ada.shape[1]
    tn = ADA_COLS
    ct = jnp.zeros((d, LANES), F32).at[:, :bsz].set(c.T)
    assert bsz <= 8, "one sublane tile of modulation rows"
    out = pl.pallas_call(
        functools.partial(_ada_kernel, bsz),
        grid=(n // tn,),
        in_specs=[pl.BlockSpec((d, LANES), lambda j: (0, 0)),
                  pl.BlockSpec((d, tn), lambda j: (0, j)),
                  pl.BlockSpec((1, tn), lambda j: (0, j))],
        out_specs=pl.BlockSpec((8, tn), lambda j: (0, j)),
        out_shape=jax.ShapeDtypeStruct((8, n), F32),
        compiler_params=_params("arbitrary"),
    )(ct, w_ada, b_ada.reshape(1, n))
    return out[:bsz]


def _inproj_kernel(q_t, r_t, x_ref, nw_ref, sc_ref, sh_ref, wa_ref, wb_ref, wf_ref, cos_ref, sin_ref,
                   qw_ref, kw_ref, fb_ref, z_ref, f_ref, h_ref):
    j = pl.program_id(1)
    r0 = 3 * q_t

    @pl.when(j == 0)
    def _():
        x = x_ref[...]
        ms = jnp.mean(x * x, axis=-1, keepdims=True)
        y = x * lax.rsqrt(ms + EPS) * nw_ref[...]
        h = (y * (1.0 + sc_ref[0]) + sh_ref[0]).astype(BF16)
        h_ref[...] = h
        t = jnp.dot(h, wf_ref[...], preferred_element_type=F32) + fb_ref[...]
        f_ref[...] = jnp.minimum(t, 0.0) - jnp.log(1.0 + jnp.exp(-jnp.abs(t)))

    def heads_of(acc):
        return [acc[:, hh * HEAD_DIM:(hh + 1) * HEAD_DIM] for hh in range(acc.shape[1] // HEAD_DIM)]

    def head_norm(acc, w_row):
        outs = []
        for a in heads_of(acc):
            ms = jnp.mean(a * a, axis=-1, keepdims=True)
            outs.append(a * lax.rsqrt(ms + EPS) * w_row)
        return jnp.concatenate(outs, axis=-1).astype(BF16)

    def rotate(acc, scale):
        cs = cos_ref[...] * scale
        sn = sin_ref[...] * scale
        outs = [a * cs + pltpu.roll(a, HEAD_DIM // 2, 1) * sn for a in heads_of(acc)]
        return jnp.concatenate(outs, axis=-1).astype(BF16)

    def fox():
        return jnp.dot(h_ref[...], wa_ref[...], preferred_element_type=F32)

    def ret():
        return jnp.dot(h_ref[...], wb_ref[...], preferred_element_type=F32)

    @pl.when(j < q_t)
    def _():
        z_ref[...] = head_norm(fox(), qw_ref[...] * (LOG2E * HEAD_DIM ** -0.5))

    @pl.when((j >= q_t) & (j < 2 * q_t))
    def _():
        z_ref[...] = head_norm(fox(), kw_ref[...])

    @pl.when((j >= 2 * q_t) & (j < r0))
    def _():
        z_ref[...] = fox().astype(BF16)

    @pl.when((j >= r0) & (j < r0 + r_t))
    def _():
        z_ref[...] = rotate(ret(), 1.0)

    @pl.when((j >= r0 + r_t) & (j < r0 + 2 * r_t))
    def _():
        z_ref[...] = rotate(ret(), HEAD_DIM ** -0.5)

    @pl.when(j >= r0 + 2 * r_t)
    def _():
        z_ref[...] = ret().astype(BF16)


def _input_projection(x2d, seq, norm_w, sc1, sh1, w_all, w_ret, fox_cols, cos_t, sin_t, qw, kw, fb):
    n, d = x2d.shape
    tm, tn = min(INPROJ_ROWS, seq), INPROJ_COLS
    fox_tiles = fox_cols // tn
    ret_tiles = w_ret.shape[1] // tn
    tiles_per_seq = seq // tm
    kern = functools.partial(_inproj_kernel, fox_tiles // 3, ret_tiles // 4)
    bsel = lambda i, j: (i // tiles_per_seq, 0, 0)
    const = lambda i, j: (0, 0)
    return pl.pallas_call(
        kern,
        grid=(n // tm, fox_tiles + ret_tiles),
        in_specs=[pl.BlockSpec((tm, d), lambda i, j: (i, 0)),
                  pl.BlockSpec((1, d), const),
                  pl.BlockSpec((1, 1, d), bsel),
                  pl.BlockSpec((1, 1, d), bsel),
                  pl.BlockSpec((d, tn), lambda i, j: (0, jnp.minimum(j, fox_tiles - 1))),
                  pl.BlockSpec((d, tn), lambda i, j: (0, jnp.maximum(j - fox_tiles, 0))),
                  pl.BlockSpec((d, LANES), lambda i, j: (0, fox_cols // LANES)),
                  pl.BlockSpec((tm, HEAD_DIM), lambda i, j: (i % tiles_per_seq, 0)),
                  pl.BlockSpec((tm, HEAD_DIM), lambda i, j: (i % tiles_per_seq, 0)),
                  pl.BlockSpec((1, HEAD_DIM), const),
                  pl.BlockSpec((1, HEAD_DIM), const),
                  pl.BlockSpec((1, LANES), const)],
        out_specs=[pl.BlockSpec((tm, tn), lambda i, j: (i, j)),
                   pl.BlockSpec((tm, LANES), lambda i, j: (i, 0))],
        out_shape=[jax.ShapeDtypeStruct((n, fox_cols + w_ret.shape[1]), BF16),
                   jax.ShapeDtypeStruct((n, LANES), F32)],
        scratch_shapes=[pltpu.VMEM((tm, d), BF16)],
        compiler_params=_params("arbitrary", "arbitrary"),
    )(x2d, norm_w, sc1, sh1, w_all, w_ret, w_all, cos_t, sin_t, qw, kw, fb)


def _cumsum_kernel(x_ref, o_ref):
    x = x_ref[0]
    r = x.shape[0]
    a = lax.broadcasted_iota(jnp.int32, (LANES, LANES), 0)
    b = lax.broadcasted_iota(jnp.int32, (LANES, LANES), 1)
    upper = (a <= b).astype(F32)
    within = jnp.dot(x, upper, precision=lax.Precision.HIGHEST, preferred_element_type=F32)
    tot = jnp.broadcast_to(within[:, LANES - 1:LANES], (r, LANES))
    ra = lax.broadcasted_iota(jnp.int32, (r, r), 0)
    rb = lax.broadcasted_iota(jnp.int32, (r, r), 1)
    strict = (rb < ra).astype(F32)
    before = jnp.dot(strict, tot, precision=lax.Precision.HIGHEST, preferred_element_type=F32)
    o_ref[0] = within + before


def _cumsum_rows(x):
    g, s = x.shape
    r = s // LANES
    out = pl.pallas_call(
        _cumsum_kernel,
        grid=(g,),
        in_specs=[pl.BlockSpec((1, r, LANES), lambda i: (i, 0, 0))],
        out_specs=pl.BlockSpec((1, r, LANES), lambda i: (i, 0, 0)),
        out_shape=jax.ShapeDtypeStruct((g, r, LANES), F32),
        compiler_params=_params("arbitrary"),
    )(x.reshape(g, r, LANES))
    return out.reshape(g, 1, s)


def _fox_kernel(tq, n_sub, first_ref, q_ref, k_ref, v_ref, cum_ref, o_ref, s_refs, m_ref, l_ref, acc_ref):
    group_id = pl.program_id(2)
    n_groups = pl.num_programs(2)
    head = pl.program_id(0) * pl.num_programs(1) + pl.program_id(1)
    n_slabs = tq // LANES

    m_ref[...] = jnp.full(m_ref.shape, NEG_BIG, F32)
    l_ref[...] = jnp.zeros(l_ref.shape, F32)
    acc_ref[...] = jnp.zeros(acc_ref.shape, F32)

    class Sub:
        def __init__(self, idx):
            self.rows = slice(idx * tq, (idx + 1) * tq)
            self.qi = n_sub * group_id + idx
            self.sa, self.sb = s_refs[2 * idx], s_refs[2 * idx + 1]
            q_start = pl.multiple_of(self.qi * tq, tq)
            self.c0 = cum_ref[0, :, pl.ds(q_start, LANES)][:, 0:1]
            self.first = first_ref[(head * n_groups + group_id) * n_sub + idx]
            self.n_off = self.qi - self.first

    def scores(sub, kb, s_ref):
        start = pl.multiple_of(kb * tq, tq)
        k = k_ref[pl.ds(start, tq), :]
        bias = (sub.c0 - cum_ref[0, :, pl.ds(start, tq)]) * LOG2E
        s_ref[...] = lax.dot_general(q_ref[sub.rows, :], k, (((1,), (1,)), ((), ())),
                                     preferred_element_type=F32) + bias

    def softmax_pv(sub, kb, s_ref, masked):
        start = pl.multiple_of(kb * tq, tq)
        if masked:
            row = lax.broadcasted_iota(jnp.int32, (LANES, LANES), 0)
            col = lax.broadcasted_iota(jnp.int32, (LANES, LANES), 1)
            for g in range(n_slabs):
                r0 = g * LANES
                slabs = [s_ref[r0:r0 + LANES, j * LANES:(j + 1) * LANES] for j in range(g + 1)]
                slabs[g] = jnp.where(col <= row, slabs[g], NEG_BIG)
                rs = slice(sub.rows.start + r0, sub.rows.start + r0 + LANES)
                update(rs, slabs, v_ref[pl.ds(start, (g + 1) * LANES), :])
        else:
            slabs = [s_ref[:, j * LANES:(j + 1) * LANES] for j in range(n_slabs)]
            update(sub.rows, slabs, v_ref[pl.ds(start, tq), :])

    def update(rs, slabs, v):
        mx = slabs[0]
        for t in slabs[1:]:
            mx = jnp.maximum(mx, t)
        m_prev = m_ref[rs, :]
        m_new = jnp.maximum(m_prev, jnp.max(mx, axis=-1, keepdims=True))
        alpha = jnp.exp2(m_prev - m_new)
        probs = [jnp.exp2(t - m_new) for t in slabs]
        psum = probs[0]
        for t in probs[1:]:
            psum = psum + t
        l_ref[rs, :] = alpha * l_ref[rs, :] + psum
        p = jnp.concatenate([t.astype(BF16) for t in probs], axis=-1)
        acc_ref[rs, :] = alpha * acc_ref[rs, :] + jnp.dot(p, v, preferred_element_type=F32)
        m_ref[rs, :] = m_new

    def sweep(sub, then):
        def pair(kb):
            scores(sub, kb + 1, sub.sb)
            softmax_pv(sub, kb, sub.sa, False)
            scores(sub, kb + 2, sub.sa)
            softmax_pv(sub, kb + 1, sub.sb, False)

        def body4(i, carry):
            pair(sub.first + 4 * i)
            pair(sub.first + 4 * i + 2)
            return carry

        def body2(i, carry):
            pair(sub.first + 2 * i)
            return carry

        n4 = sub.n_off // 4
        lax.fori_loop(0, n4, body4, 0)
        lax.fori_loop(2 * n4, sub.n_off // 2, body2, 0)

        @pl.when(sub.n_off % 2 == 0)
        def _():
            then()
            softmax_pv(sub, sub.qi, sub.sa, True)

        @pl.when(sub.n_off % 2 == 1)
        def _():
            scores(sub, sub.qi, sub.sb)
            softmax_pv(sub, sub.qi - 1, sub.sa, False)
            then()
            softmax_pv(sub, sub.qi, sub.sb, True)

    subs = [Sub(idx) for idx in range(n_sub)]
    scores(subs[0], subs[0].first, subs[0].sa)
    for sub, nxt in zip(subs, subs[1:] + [None]):
        sweep(sub, (lambda: None) if nxt is None else functools.partial(scores, nxt, nxt.first, nxt.sa))

    o_ref[...] = (acc_ref[...] / jnp.sum(l_ref[...], axis=-1, keepdims=True)).astype(BF16)


def _first_live_block(cum, tq, qk_bound):
    c0 = cum[:, 0, ::tq]
    cend = cum[:, 0, tq - 1::tq]
    gap = (c0[:, :, None] - cend[:, None, :]) * LOG2E + 2.0 * qk_bound
    nq = c0.shape[1]
    earlier = jnp.arange(nq)[None, :] < jnp.arange(nq)[:, None]
    return jnp.sum((gap < -UNDERFLOW_LOG2) & earlier[None], axis=-1).astype(jnp.int32).reshape(-1)


def _fox_attention(z, cum, qk_bound, bsz, seq, n_heads):
    n_sub = ATT_BLOCKS_PER_STEP
    tq = min(ATT_BLOCK, seq // n_sub)
    nq = seq // tq
    n_groups = nq // n_sub
    rows = n_sub * tq
    kern = functools.partial(_fox_kernel, tq, n_sub)
    grid_spec = pltpu.PrefetchScalarGridSpec(
        num_scalar_prefetch=1,
        grid=(bsz, n_heads, n_groups),
        in_specs=[pl.BlockSpec((rows, HEAD_DIM), lambda b, h, i, f: (b * n_groups + i, h)),
                  pl.BlockSpec((seq, HEAD_DIM), lambda b, h, i, f: (b, n_heads + h)),
                  pl.BlockSpec((seq, HEAD_DIM), lambda b, h, i, f: (b, 2 * n_heads + h)),
                  pl.BlockSpec((1, 1, seq), lambda b, h, i, f: (b * n_heads + h, 0, 0))],
        out_specs=pl.BlockSpec((rows, HEAD_DIM), lambda b, h, i, f: (b * n_groups + i, h)),
        scratch_shapes=[[pltpu.VMEM((tq, tq), F32)] * (2 * n_sub),
                        pltpu.VMEM((rows, LANES), F32), pltpu.VMEM((rows, LANES), F32),
                        pltpu.VMEM((rows, HEAD_DIM), F32)],
    )
    return pl.pallas_call(
        kern,
        grid_spec=grid_spec,
        out_shape=jax.ShapeDtypeStruct((bsz * seq, n_heads * HEAD_DIM), BF16),
        compiler_params=_params("arbitrary", "arbitrary", "arbitrary"),
    )(_first_live_block(cum, tq, qk_bound), z, z, z, cum)


def _ret_kernel(chunk, n_heads, lg_ref, q_ref, k_ref, v_ref, g_ref, nw_ref, o_ref, state_ref, decay_ref,
                qdec_ref, kdec_ref):
    first = (pl.program_id(0) == 0) & (pl.program_id(1) == 0)

    @pl.when(first)
    def _():
        i = lax.broadcasted_iota(jnp.int32, (chunk, chunk), 0)
        jj = lax.broadcasted_iota(jnp.int32, (chunk, chunk), 1)
        diff = (i - jj).astype(F32)
        pos = lax.broadcasted_iota(jnp.int32, (chunk, HEAD_DIM), 0).astype(F32)
        for h in range(n_heads):
            decay_ref[h] = jnp.where(diff >= 0, jnp.exp(lg_ref[h] * jnp.maximum(diff, 0.0)), 0.0)
            qdec_ref[h] = jnp.exp(lg_ref[h] * (pos + 1.0))
            kdec_ref[h] = jnp.exp(lg_ref[h] * (chunk - 1.0 - pos))

    @pl.when(pl.program_id(1) == 0)
    def _():
        state_ref[...] = jnp.zeros(state_ref.shape, F32)

    for h in range(n_heads):
        log_g = lg_ref[h]
        cols = slice(h * HEAD_DIM, (h + 1) * HEAD_DIM)
        q = q_ref[:, cols]
        k = k_ref[:, cols]
        v = v_ref[:, cols]
        scores = lax.dot_general(q, k, (((1,), (1,)), ((), ())), preferred_element_type=F32)
        scores = scores * decay_ref[h]
        intra = jnp.dot(scores.astype(BF16), v, preferred_element_type=F32)
        state = state_ref[h]
        inter = jnp.dot(q, state.astype(BF16), preferred_element_type=F32) * qdec_ref[h]
        kd = (k.astype(F32) * kdec_ref[h]).astype(BF16)
        kv = lax.dot_general(kd, v, (((0,), (0,)), ((), ())), preferred_element_type=F32)
        state_ref[h] = state * jnp.exp(jnp.full((1, HEAD_DIM), chunk, F32) * log_g) + kv
        o = intra + inter
        ms = jnp.mean(o * o, axis=-1, keepdims=True)
        o = o * lax.rsqrt(ms + EPS) * nw_ref[:, cols]
        o_ref[:, cols] = (o * _silu(g_ref[:, cols].astype(F32))).astype(BF16)


def _retention(z, log_g, norm_w, bsz, seq, n_heads, col0):
    chunk = min(RET_CHUNK, seq)
    nt = seq // chunk
    width = n_heads * HEAD_DIM
    c0 = col0 // width
    kern = functools.partial(_ret_kernel, chunk, n_heads)

    def sec(s):
        return pl.BlockSpec((chunk, width), lambda b, t, lg: (b * nt + t, c0 + s))

    grid_spec = pltpu.PrefetchScalarGridSpec(
        num_scalar_prefetch=1,
        grid=(bsz, nt),
        in_specs=[sec(0), sec(1), sec(2), sec(3), pl.BlockSpec((1, width), lambda b, t, lg: (0, 0))],
        out_specs=pl.BlockSpec((chunk, width), lambda b, t, lg: (b * nt + t, 0)),
        scratch_shapes=[pltpu.VMEM((n_heads, HEAD_DIM, HEAD_DIM), F32),
                        pltpu.VMEM((n_heads, chunk, chunk), F32),
                        pltpu.VMEM((n_heads, chunk, HEAD_DIM), F32),
                        pltpu.VMEM((n_heads, chunk, HEAD_DIM), F32)],
    )
    return pl.pallas_call(
        kern,
        grid_spec=grid_spec,
        out_shape=jax.ShapeDtypeStruct((bsz * seq, width), BF16),
        compiler_params=_params("arbitrary", "arbitrary"),
    )(log_g, z, z, z, z, norm_w)


def _outproj_kernel(oa_ref, ob_ref, wa_ref, wb_ref, x_ref, g1_ref, nw_ref, sc_ref, sh_ref, wr_ref, br_ref,
                    x1_ref, hp_ref, lg_ref):
    mix = jnp.dot(oa_ref[...], wa_ref[...], preferred_element_type=F32)
    mix = mix + jnp.dot(ob_ref[...], wb_ref[...], preferred_element_type=F32)
    x1 = x_ref[...] + g1_ref[0] * mix
    x1_ref[...] = x1
    ms = jnp.mean(x1 * x1, axis=-1, keepdims=True)
    h2 = x1 * lax.rsqrt(ms + EPS) * nw_ref[...] * (1.0 + sc_ref[0]) + sh_ref[0]
    hp_ref[...] = _rows_to_tiles(_pack_halves(h2))
    h_hi = h2.astype(BF16)
    h_lo = (h2 - h_hi.astype(F32)).astype(BF16)
    both = jnp.dot(h_hi, wr_ref[...], preferred_element_type=F32)
    cross = jnp.dot(h_lo, wr_ref[:, :LANES], preferred_element_type=F32)
    lg_ref[...] = both[:, :LANES] + both[:, LANES:] + cross + br_ref[...]


def _output_projection(o_a, o_b, w_out, x2d, seq, g1, norm_w, sc2, sh2, w_router, b_router):
    n, d = x2d.shape
    da = o_a.shape[1]
    tm = min(OUTPROJ_ROWS, seq)
    tiles_per_seq = seq // tm
    bsel = lambda i: (i // tiles_per_seq, 0, 0)
    return pl.pallas_call(
        _outproj_kernel,
        grid=(n // tm,),
        in_specs=[pl.BlockSpec((tm, da), lambda i: (i, 0)),
                  pl.BlockSpec((tm, da), lambda i: (i, 0)),
                  pl.BlockSpec((da, d), lambda i: (0, 0)),
                  pl.BlockSpec((da, d), lambda i: (1, 0)),
                  pl.BlockSpec((tm, d), lambda i: (i, 0)),
                  pl.BlockSpec((1, 1, d), bsel),
                  pl.BlockSpec((1, d), lambda i: (0, 0)),
                  pl.BlockSpec((1, 1, d), bsel),
                  pl.BlockSpec((1, 1, d), bsel),
                  pl.BlockSpec((d, 2 * LANES), lambda i: (0, 0)),
                  pl.BlockSpec((1, LANES), lambda i: (0, 0))],
        out_specs=[pl.BlockSpec((tm, d), lambda i: (i, 0)),
                   pl.BlockSpec((tm, d // 2 // LANES, LANES), lambda i: (i, 0, 0)),
                   pl.BlockSpec((tm, LANES), lambda i: (i, 0))],
        out_shape=[jax.ShapeDtypeStruct((n, d), F32),
                   jax.ShapeDtypeStruct((n, d // 2 // LANES, LANES), U32),
                   jax.ShapeDtypeStruct((n, LANES), F32)],
        compiler_params=_params("arbitrary"),
    )(o_a, o_b, w_out, w_out, x2d, g1, norm_w, sc2, sh2, w_router, b_router)


def _route_kernel(blk, n_blocks, lg_ref, gate_ref, ids_ref, plan_ref, run_ref):
    i = pl.program_id(0)

    @pl.when(i == 0)
    def _():
        run_ref[...] = jnp.zeros(run_ref.shape, F32)

    lg = lg_ref[...]
    tt = lg.shape[0]
    lane = lax.broadcasted_iota(jnp.int32, lg.shape, 1).astype(F32)
    big = 1e6

    def rmax(v):
        return jnp.max(v, axis=-1, keepdims=True)

    def rmin(v):
        return jnp.min(v, axis=-1, keepdims=True)

    def rsum(v):
        return jnp.sum(v, axis=-1, keepdims=True)

    cmask = lane < N_GROUPS
    cm = jnp.where(cmask, lg, NEG_BIG)
    ce = jnp.where(cmask, jnp.exp(cm - rmax(cm)), 0.0)
    pgrp = ce / rsum(ce)
    p_g = rmax(pgrp)
    g_sel = rmin(jnp.where(cmask & (pgrp == p_g), lane, big))

    lo = N_GROUPS + EXPERTS_PER_GROUP * g_sel
    fmask = (lane >= lo) & (lane < lo + EXPERTS_PER_GROUP)
    fm = jnp.where(fmask, lg, NEG_BIG)
    fe = jnp.where(fmask, jnp.exp(fm - rmax(fm)), 0.0)
    fp = fe / rsum(fe)
    fp = jnp.where(fmask, fp, -1.0)
    p1 = rmax(fp)
    i1 = rmin(jnp.where(fp == p1, lane, big))
    fp2 = jnp.where(lane == i1, -1.0, fp)
    p2 = rmax(fp2)
    i2 = rmin(jnp.where(fp2 == p2, lane, big))
    denom = p1 + p2
    w1 = p_g * p1 / denom
    w2 = p_g * p2 / denom
    e1 = i1 - N_GROUPS
    e2 = i2 - N_GROUPS

    gate_ref[...] = jnp.where(lane == 0, w1, jnp.where(lane == 1, w2, 0.0))

    oh1 = (lane == e1).astype(F32)
    oh2 = (lane == e2).astype(F32)
    both = oh1 + oh2
    ra = lax.broadcasted_iota(jnp.int32, (tt, tt), 0)
    rb = lax.broadcasted_iota(jnp.int32, (tt, tt), 1)
    strict = (rb < ra).astype(BF16)
    prefix = jnp.dot(strict, both.astype(BF16), preferred_element_type=F32) + run_ref[...]
    r1 = rsum(prefix * oh1)
    r2 = rsum(prefix * oh2)
    run_ref[...] = run_ref[...] + jnp.sum(both, axis=0, keepdims=True)

    packed = jnp.where(lane == 0, e1, jnp.where(lane == 1, e2, jnp.where(lane == 2, r1,
                                                                        jnp.where(lane == 3, r2, 0.0))))
    ids_ref[...] = jnp.transpose(packed)[:8, :].astype(jnp.int32)

    @pl.when(i == pl.num_programs(0) - 1)
    def _():
        cnt = jnp.broadcast_to(run_ref[...], (8, LANES))
        lane8 = lax.broadcasted_iota(jnp.int32, (8, LANES), 1)
        padded = jnp.floor((cnt + (blk - 1.0)) * (1.0 / blk)) * blk
        pend = padded
        for sh in (1, 2, 4, 8, 16, 32, 64):
            pend = pend + jnp.where(lane8 >= sh, pltpu.roll(pend, sh, 1), 0.0)
        pstart = pend - padded
        total = jnp.max(pend, axis=-1, keepdims=True)
        tail = total + (lane8 - N_EXPERTS).astype(F32) * blk
        fill = jnp.where(lane8 < N_EXPERTS, jnp.where(padded > 0, pend - blk, -1.0),
                         jnp.where((lane8 < 2 * N_EXPERTS) & (tail < n_blocks * blk), tail, -1.0))
        row8 = lax.broadcasted_iota(jnp.int32, (8, LANES), 0)
        plan_ref[...] = jnp.where(row8 == 0, pstart, jnp.where(row8 == 1, fill,
                                                               jnp.where(row8 == 2, cnt, 0.0))).astype(jnp.int32)


def _route(logits, blk, n_blocks):
    n = logits.shape[0]
    tt = min(ROUTE_ROWS, n)
    blkspec = lambda: pl.BlockSpec((tt, LANES), lambda i: (i, 0))
    return pl.pallas_call(
        functools.partial(_route_kernel, blk, n_blocks),
        grid=(n // tt,),
        in_specs=[blkspec()],
        out_specs=[blkspec(),
                   pl.BlockSpec((8, tt), lambda i: (0, i)),
                   pl.BlockSpec((8, LANES), lambda i: (0, 0))],
        out_shape=[jax.ShapeDtypeStruct((n, LANES), F32),
                   jax.ShapeDtypeStruct((8, n), jnp.int32),
                   jax.ShapeDtypeStruct((8, LANES), jnp.int32)],
        scratch_shapes=[pltpu.VMEM((1, LANES), F32)],
        compiler_params=_params("arbitrary"),
    )(logits)


def _dispatch_kernel(tt, blk, n_fill, dest_ref, fill_ref, h_ref, xs_ref, zero_ref, sem, zsem):
    i = pl.program_id(0)
    n_tok = pl.num_programs(0) * tt

    @pl.when(i == 0)
    def _():
        zero_ref[...] = jnp.zeros(zero_ref.shape, U32)

        def zcopy(z):
            row = pl.multiple_of(jnp.maximum(fill_ref[z], 0), blk)
            return pltpu.make_async_copy(zero_ref, xs_ref.at[pl.ds(row, blk)], zsem)

        def zissue(z, carry):
            @pl.when(fill_ref[z] >= 0)
            def _():
                zcopy(z).start()
            return carry

        def zdrain(z, carry):
            @pl.when(fill_ref[z] >= 0)
            def _():
                zcopy(z).wait()
            return carry

        lax.fori_loop(0, n_fill, zissue, 0)
        lax.fori_loop(0, n_fill, zdrain, 0)

    def copy(r, kk):
        d = dest_ref[kk * n_tok + i * tt + r]
        return pltpu.make_async_copy(h_ref.at[r], xs_ref.at[d], sem)

    def issue(r, carry):
        for kk in range(TOP_K):
            copy(r, kk).start(priority=kk % 2)
        return carry

    lax.fori_loop(0, tt, issue, 0, unroll=8)
    for _ in range(TOP_K):
        pltpu.make_async_copy(h_ref, xs_ref.at[pl.ds(0, tt)], sem).wait()


def _dispatch(h_packed, dest_flat, fill_rows, n_slots, blk):
    n = h_packed.shape[0]
    tile = h_packed.shape[1:]
    tt = min(DISPATCH_TOKENS, n)
    n_fill = fill_rows.shape[0]
    grid_spec = pltpu.PrefetchScalarGridSpec(
        num_scalar_prefetch=2,
        grid=(n // tt,),
        in_specs=[pl.BlockSpec((tt,) + tile, lambda i, d, f: (i, 0, 0))],
        out_specs=pl.BlockSpec(memory_space=pl.ANY),
        scratch_shapes=[pltpu.VMEM((blk,) + tile, U32), pltpu.SemaphoreType.DMA(()), pltpu.SemaphoreType.DMA(())],
    )
    return pl.pallas_call(
        functools.partial(_dispatch_kernel, tt, blk, n_fill),
        grid_spec=grid_spec,
        out_shape=jax.ShapeDtypeStruct((n_slots,) + tile, U32),
        compiler_params=_params("arbitrary"),
    )(dest_flat, fill_rows, h_packed)


def _expert_kernel(blk, ahead, cnt_ref, pstart_ref, fill_ref, xs_ref, w1_ref, w3_ref, w2_ref, y_ref,
                   w1f, w3f, w2f, w1b, w3b, w2b, xbuf, ybuf, done_ref, w_sem, in_sem, out_sem):
    e = pl.program_id(0)
    n_exp = pl.num_programs(0)
    wslot = e % 2
    n_blk = (cnt_ref[e] + (blk - 1)) // blk
    base = pstart_ref[e]
    n_x = xbuf.shape[0]

    def weight_copies(ex, slot):
        return [pltpu.make_async_copy(src.at[ex], dst.at[slot], w_sem.at[slot])
                for src, dst in ((w1_ref, w1f), (w3_ref, w3f), (w2_ref, w2f))]

    def rows(b):
        return pl.ds(pl.multiple_of(base + b * blk, blk), blk)

    def in_copy(b, slot):
        return pltpu.make_async_copy(xs_ref.at[rows(b)], xbuf.at[slot], in_sem.at[slot])

    def out_copy(b, slot):
        return pltpu.make_async_copy(ybuf.at[slot], y_ref.at[rows(b)], out_sem.at[slot])

    @pl.when(e == 0)
    def _():
        for c in weight_copies(0, 0):
            c.start()

    for p in range(ahead):
        @pl.when(p < n_blk)
        def _():
            in_copy(p, p).start()

    @pl.when(e + 1 < n_exp)
    def _():
        for c in weight_copies(e + 1, 1 - wslot):
            c.start()

    for c in weight_copies(e, wslot):
        c.wait()
    w1b[...] = w1f[wslot].astype(BF16)
    w3b[...] = w3f[wslot].astype(BF16)
    w2b[...] = w2f[wslot].astype(BF16)

    @pl.when(e == 0)
    def _():
        done_ref[0] = 0

    done = done_ref[0]

    def out_wait(slot):
        pltpu.make_async_copy(ybuf.at[slot], y_ref.at[pl.ds(0, blk)], out_sem.at[slot]).wait()

    def body(b, carry):
        slot = (done + b) % 2

        @pl.when(b + ahead < n_blk)
        def _():
            in_copy(b + ahead, (b + ahead) % n_x).start()

        in_copy(b, b % n_x).wait()

        @pl.when(done + b >= 2)
        def _():
            out_wait(slot)

        lo, hi = _unpack_halves(_tiles_to_rows(xbuf[b % n_x]))
        lo = lo.astype(BF16)
        hi = hi.astype(BF16)
        half = lo.shape[1]
        a = jnp.dot(lo, w1b[:half, :], preferred_element_type=F32)
        a = a + jnp.dot(hi, w1b[half:, :], preferred_element_type=F32)
        g = jnp.dot(lo, w3b[:half, :], preferred_element_type=F32)
        g = g + jnp.dot(hi, w3b[half:, :], preferred_element_type=F32)
        mid = (_silu(a) * g).astype(BF16)
        ybuf[slot] = _rows_to_tiles(_pack_halves(jnp.dot(mid, w2b[...], preferred_element_type=F32)))
        out_copy(b, slot).start(priority=1)
        return carry

    lax.fori_loop(0, n_blk, body, 0)
    total = done + n_blk
    done_ref[0] = total

    @pl.when(e == n_exp - 1)
    def _():
        @pl.when(total >= 2)
        def _():
            out_wait(total % 2)

        @pl.when(total >= 1)
        def _():
            out_wait((total - 1) % 2)

        ybuf[0] = jnp.zeros(ybuf.shape[1:], U32)

        def zcopy(t):
            row = pl.multiple_of(jnp.maximum(fill_ref[N_EXPERTS + t], 0), blk)
            return pltpu.make_async_copy(ybuf.at[0], y_ref.at[pl.ds(row, blk)], out_sem.at[0])

        def zissue(t, carry):
            @pl.when(fill_ref[N_EXPERTS + t] >= 0)
            def _():
                zcopy(t).start()
            return carry

        def zdrain(t, carry):
            @pl.when(fill_ref[N_EXPERTS + t] >= 0)
            def _():
                zcopy(t).wait()
            return carry

        lax.fori_loop(0, N_EXPERTS, zissue, 0)
        lax.fori_loop(0, N_EXPERTS, zdrain, 0)


def _expert_blocks(xs, counts, pstart, fill_rows, w1, w3, w2, blk):
    n_slots = xs.shape[0]
    tile = xs.shape[1:]
    n_exp, d, de = w1.shape
    ahead = EXPERT_AHEAD
    hbm = pl.BlockSpec(memory_space=pl.ANY)
    grid_spec = pltpu.PrefetchScalarGridSpec(
        num_scalar_prefetch=3,
        grid=(n_exp,),
        in_specs=[hbm, hbm, hbm, hbm],
        out_specs=hbm,
        scratch_shapes=[pltpu.VMEM((2, d, de), F32), pltpu.VMEM((2, d, de), F32), pltpu.VMEM((2, de, d), F32),
                        pltpu.VMEM((d, de), BF16), pltpu.VMEM((d, de), BF16), pltpu.VMEM((de, d), BF16),
                        pltpu.VMEM((ahead + 1, blk) + tile, U32), pltpu.VMEM((2, blk) + tile, U32),
                        pltpu.SMEM((1,), jnp.int32),
                        pltpu.SemaphoreType.DMA((2,)), pltpu.SemaphoreType.DMA((ahead + 1,)),
                        pltpu.SemaphoreType.DMA((2,))],
    )
    return pl.pallas_call(
        functools.partial(_expert_kernel, blk, ahead),
        grid_spec=grid_spec,
        out_shape=jax.ShapeDtypeStruct((n_slots,) + tile, U32),
        compiler_params=_params("arbitrary"),
    )(counts, pstart, fill_rows, xs, w1, w3, w2)


def _combine_kernel(tt, n_tiles, dest_ref, x1_ref, g2_ref, gate_ref, yb_ref, o_ref, buf, sems):
    i = pl.program_id(0)

    def copy(tile, slot, r, kk):
        d = dest_ref[kk * (n_tiles * tt) + tile * tt + r]
        return pltpu.make_async_copy(yb_ref.at[d], buf.at[slot, kk, r], sems.at[slot])

    def issue_tile(tile, slot):
        def body(r, carry):
            for kk in range(TOP_K):
                copy(tile, slot, r, kk).start(priority=kk % 2)
            return carry
        lax.fori_loop(0, tt, body, 0, unroll=8)

    def wait_tile(tile, slot):
        for kk in range(TOP_K):
            pltpu.make_async_copy(yb_ref.at[pl.ds(0, tt)], buf.at[slot, kk], sems.at[slot]).wait()

    slot = i % 2

    @pl.when(i == 0)
    def _():
        issue_tile(0, 0)

    @pl.when(i + 1 < n_tiles)
    def _():
        issue_tile(i + 1, 1 - slot)

    wait_tile(i, slot)

    gate = gate_ref[...]
    wa = gate[:, 0:1]
    wb = gate[:, 1:2]
    lo_a, hi_a = _unpack_halves(_tiles_to_rows(buf[slot, 0]))
    lo_b, hi_b = _unpack_halves(_tiles_to_rows(buf[slot, 1]))
    y = jnp.concatenate([wa * lo_a + wb * lo_b, wa * hi_a + wb * hi_b], axis=-1)
    o_ref[...] = x1_ref[...] + g2_ref[0] * y


def _combine(x1, seq, g2, gates, dest_flat, yb):
    n, d = x1.shape
    tile = yb.shape[1:]
    tt = min(COMBINE_TOKENS, seq)
    n_tiles = n // tt
    tiles_per_seq = seq // tt
    grid_spec = pltpu.PrefetchScalarGridSpec(
        num_scalar_prefetch=1,
        grid=(n_tiles,),
        in_specs=[pl.BlockSpec((tt, d), lambda i, dr: (i, 0)),
                  pl.BlockSpec((1, 1, d), lambda i, dr: (i // tiles_per_seq, 0, 0)),
                  pl.BlockSpec((tt, LANES), lambda i, dr: (i, 0)),
                  pl.BlockSpec(memory_space=pl.ANY)],
        out_specs=pl.BlockSpec((tt, d), lambda i, dr: (i, 0)),
        scratch_shapes=[pltpu.VMEM((2, TOP_K, tt) + tile, U32), pltpu.SemaphoreType.DMA((2,))],
    )
    return pl.pallas_call(
        functools.partial(_combine_kernel, tt, n_tiles),
        grid_spec=grid_spec,
        out_shape=jax.ShapeDtypeStruct((n, d), F32),
        compiler_params=_params("arbitrary"),
    )(dest_flat, x1, g2, gates, yb)


def _rotation_tables(seq):
    half = HEAD_DIM // 2
    theta = ROPE_BASE ** (-np.arange(half, dtype=np.float64) / half)
    ang = np.arange(seq, dtype=np.float64)[:, None] * theta[None, :]
    cos_t = np.concatenate([np.cos(ang), np.cos(ang)], axis=-1).astype(np.float32)
    sin_t = np.concatenate([-np.sin(ang), np.sin(ang)], axis=-1).astype(np.float32)
    return jnp.asarray(cos_t), jnp.asarray(sin_t)


def _layer(x, c, w_ada, b_ada, norm1_w, w_in, forget_bias, q_norm_w, k_norm_w, ret_norm_w, w_out, norm2_w,
           w_coarse, b_coarse, w_fine, b_fine, w1, w3, w2):
    bsz, seq, d = x.shape
    n = bsz * seq
    d_fox = d // 2
    d_ret = d // 2
    n_heads = d_fox // HEAD_DIM

    mod = _ada_modulation(c, w_ada, b_ada)
    sh1, sc1, g1, sh2, sc2, g2 = [m.reshape(bsz, 1, d) for m in jnp.split(mod, 6, axis=-1)]

    f0 = 3 * d_fox
    w_all = w_in.astype(BF16)
    w_ret = w_all[:, f0 + n_heads:]
    fb = jnp.zeros((1, LANES), F32).at[0, :n_heads].set(forget_bias)

    cos_t, sin_t = _rotation_tables(seq)

    x2d = x.reshape(n, d)
    z, log_f = _input_projection(x2d, seq, norm1_w.reshape(1, d), sc1, sh1, w_all, w_ret, f0, cos_t, sin_t,
                                 q_norm_w.reshape(1, HEAD_DIM), k_norm_w.reshape(1, HEAD_DIM), fb)

    lf = log_f[:, :n_heads].reshape(bsz, seq, n_heads).transpose(0, 2, 1).reshape(bsz * n_heads, seq)
    cum = _cumsum_rows(lf)

    qk_bound = 1.02 * LOG2E * HEAD_DIM ** 0.5 * jnp.max(jnp.abs(q_norm_w)) * jnp.max(jnp.abs(k_norm_w))
    o_a = _fox_attention(z, cum, qk_bound, bsz, seq, n_heads)
    log_g = jnp.log(1.0 - 2.0 ** (-5.0 - jnp.arange(n_heads, dtype=F32)))
    o_b = _retention(z, log_g, ret_norm_w.reshape(1, d_ret), bsz, seq, n_heads, 3 * d_fox)

    pad = LANES - N_GROUPS - N_EXPERTS
    w_router = jnp.concatenate([w_coarse, w_fine.transpose(1, 0, 2).reshape(d, N_EXPERTS),
                                jnp.zeros((d, pad), F32)], axis=1)
    b_router = jnp.concatenate([b_coarse, b_fine.reshape(N_EXPERTS), jnp.zeros((pad,), F32)]).reshape(1, LANES)

    wr_hi = w_router.astype(BF16)
    wr_lo = (w_router - wr_hi.astype(F32)).astype(BF16)
    x1, h_packed, logits = _output_projection(o_a, o_b, w_out.astype(BF16), x2d, seq, g1,
                                              norm2_w.reshape(1, d), sc2, sh2,
                                              jnp.concatenate([wr_hi, wr_lo], axis=1), b_router)

    blk = EXPERT_BLOCK
    nk = n * TOP_K
    n_blocks = nk // blk + N_EXPERTS
    gates, ids, plan = _route(logits, blk, n_blocks)
    pstart = plan[0, :N_EXPERTS]
    fill_rows = plan[1, :2 * N_EXPERTS]
    counts = plan[2, :N_EXPERTS]
    eid = ids[0:TOP_K]
    hit = eid[None] == jnp.arange(N_EXPERTS, dtype=jnp.int32)[:, None, None]
    dest = (jnp.sum(jnp.where(hit, pstart[:, None, None], 0), axis=0) + ids[TOP_K:2 * TOP_K]).reshape(nk)

    xs = _dispatch(h_packed, dest, fill_rows, n_blocks * blk, blk)
    yb = _expert_blocks(xs, counts, pstart, fill_rows, w1, w3, w2, blk)
    out = _combine(x1, seq, g2, gates, dest, yb)
    return out.reshape(bsz, seq, d)


def kernel(x, c, w_ada, b_ada, norm1_w, w_in, forget_bias, q_norm_w, k_norm_w, ret_norm_w, w_out, norm2_w,
           w_coarse, b_coarse, w_fine, b_fine, w1, w3, w2):
    c_in = c
    for l in range(w_ada.shape[0]):
        x = _layer(x, c_in, w_ada[l], b_ada[l], norm1_w[l], w_in[l], forget_bias[l], q_norm_w[l],
                   k_norm_w[l], ret_norm_w[l], w_out[l], norm2_w[l], w_coarse[l], b_coarse[l],
                   w_fine[l], b_fine[l], w1[l], w3[l], w2[l])
    return x
```

```python
import functools

import jax
import jax.numpy as jnp
import numpy as np
from jax import lax
from jax.experimental import pallas as pl
from jax.experimental.pallas import tpu as pltpu

HEAD_DIM = 128
N_GROUPS = 4
EXPERTS_PER_GROUP = 8
N_EXPERTS = N_GROUPS * EXPERTS_PER_GROUP
TOP_K = 2
ROPE_BASE = 10000.0
EPS = 1e-6

LANES = 128
VMEM_LIMIT = 56 * 1024 * 1024
NEG_BIG = -1e30
LOG2E = 1.4426950408889634
UNDERFLOW_LOG2 = 160.0

ADA_COLS = 1024
INPROJ_ROWS = 1024
INPROJ_COLS = 1024
ATT_BLOCK = 512
ATT_BLOCKS_PER_STEP = 4
RET_CHUNK = 256
OUTPROJ_ROWS = 512
ROUTE_ROWS = 512
DISPATCH_TOKENS = 1024
COMBINE_TOKENS = 512
EXPERT_BLOCK = 256
EXPERT_AHEAD = 3

F32 = jnp.float32
BF16 = jnp.bfloat16
U32 = jnp.uint32


def _params(*sem):
    return pltpu.CompilerParams(dimension_semantics=sem, vmem_limit_bytes=VMEM_LIMIT)


def _silu(v):
    return v * (1.0 / (1.0 + jnp.exp(-v)))


def _pack_halves(y):
    w = y.shape[1] // 2
    lo = pltpu.bitcast(y[:, :w].astype(BF16).astype(F32), U32)
    hi = pltpu.bitcast(y[:, w:].astype(BF16).astype(F32), U32)
    return (hi & jnp.uint32(0xFFFF0000)) | (lo >> 16)


def _rows_to_tiles(p):
    return pltpu.einshape("m(ck)->mck", p, c=8, k=LANES)


def _tiles_to_rows(t):
    return pltpu.einshape("mck->m(ck)", t)


def _unpack_halves(p):
    lo = pltpu.bitcast(p << 16, F32)
    hi = pltpu.bitcast(p & jnp.uint32(0xFFFF0000), F32)
    return lo, hi


def _ada_kernel(bsz, ct_ref, w_ref, b_ref, o_ref):
    w = w_ref[...]
    rows = []
    for b in range(o_ref.shape[0]):
        if b < bsz:
            cb = _silu(ct_ref[:, b:b + 1])
            rows.append(jnp.sum(cb * w, axis=0, keepdims=True) + b_ref[...])
        else:
            rows.append(jnp.zeros_like(b_ref[...]))
    o_ref[...] = jnp.concatenate(rows, axis=0)


def _ada_modulation(c, w_ada, b_ada):
    bsz, d = c.shape
    n = w_ada.shape[1]
    tn = ADA_COLS
    ct = jnp.zeros((d, LANES), F32).at[:, :bsz].set(c.T)
    assert bsz <= 8, "one sublane tile of modulation rows"
    out = pl.pallas_call(
        functools.partial(_ada_kernel, bsz),
        grid=(n // tn,),
        in_specs=[pl.BlockSpec((d, LANES), lambda j: (0, 0)),
                  pl.BlockSpec((d, tn), lambda j: (0, j)),
                  pl.BlockSpec((1, tn), lambda j: (0, j))],
        out_specs=pl.BlockSpec((8, tn), lambda j: (0, j)),
        out_shape=jax.ShapeDtypeStruct((8, n), F32),
        compiler_params=_params("arbitrary"),
    )(ct, w_ada, b_ada.reshape(1, n))
    return out[:bsz]


def _inproj_kernel(q_t, r_t, x_ref, nw_ref, sc_ref, sh_ref, wa_ref, wb_ref, wf_ref, cos_ref, sin_ref,
                   qw_ref, kw_ref, fb_ref, z_ref, f_ref, h_ref):
    j = pl.program_id(1)
    r0 = 3 * q_t

    @pl.when(j == 0)
    def _():
        x = x_ref[...]
        ms = jnp.mean(x * x, axis=-1, keepdims=True)
        y = x * lax.rsqrt(ms + EPS) * nw_ref[...]
        h = (y * (1.0 + sc_ref[0]) + sh_ref[0]).astype(BF16)
        h_ref[...] = h
        t = lax.dot_general(h, wf_ref[...], (((1,), (1,)), ((), ())), preferred_element_type=F32) + fb_ref[...]
        f_ref[...] = jnp.minimum(t, 0.0) - jnp.log(1.0 + jnp.exp(-jnp.abs(t)))

    def heads_of(acc):
        return [acc[:, hh * HEAD_DIM:(hh + 1) * HEAD_DIM] for hh in range(acc.shape[1] // HEAD_DIM)]

    def head_norm(acc, w_row):
        outs = []
        for a in heads_of(acc):
            ms = jnp.mean(a * a, axis=-1, keepdims=True)
            outs.append(a * lax.rsqrt(ms + EPS) * w_row)
        return jnp.concatenate(outs, axis=-1).astype(BF16)

    def rotate(acc, scale):
        cs = cos_ref[...] * scale
        sn = sin_ref[...] * scale
        outs = [a * cs + pltpu.roll(a, HEAD_DIM // 2, 1) * sn for a in heads_of(acc)]
        return jnp.concatenate(outs, axis=-1).astype(BF16)

    def fox():
        return lax.dot_general(h_ref[...], wa_ref[...], (((1,), (1,)), ((), ())), preferred_element_type=F32)

    def ret():
        return lax.dot_general(h_ref[...], wb_ref[...], (((1,), (1,)), ((), ())), preferred_element_type=F32)

    @pl.when(j < q_t)
    def _():
        z_ref[...] = head_norm(fox(), qw_ref[...] * (LOG2E * HEAD_DIM ** -0.5))

    @pl.when((j >= q_t) & (j < 2 * q_t))
    def _():
        z_ref[...] = head_norm(fox(), kw_ref[...])

    @pl.when((j >= 2 * q_t) & (j < r0))
    def _():
        z_ref[...] = fox().astype(BF16)

    @pl.when((j >= r0) & (j < r0 + r_t))
    def _():
        z_ref[...] = rotate(ret(), 1.0)

    @pl.when((j >= r0 + r_t) & (j < r0 + 2 * r_t))
    def _():
        z_ref[...] = rotate(ret(), HEAD_DIM ** -0.5)

    @pl.when(j >= r0 + 2 * r_t)
    def _():
        z_ref[...] = ret().astype(BF16)


def _input_projection(x2d, seq, norm_w, sc1, sh1, w_all, w_ret, fox_cols, cos_t, sin_t, qw, kw, fb):
    n, d = x2d.shape
    tm, tn = min(INPROJ_ROWS, seq), INPROJ_COLS
    fox_tiles = fox_cols // tn
    ret_tiles = w_ret.shape[0] // tn
    tiles_per_seq = seq // tm
    kern = functools.partial(_inproj_kernel, fox_tiles // 3, ret_tiles // 4)
    bsel = lambda i, j: (i // tiles_per_seq, 0, 0)
    const = lambda i, j: (0, 0)
    return pl.pallas_call(
        kern,
        grid=(n // tm, fox_tiles + ret_tiles),
        in_specs=[pl.BlockSpec((tm, d), lambda i, j: (i, 0)),
                  pl.BlockSpec((1, d), const),
                  pl.BlockSpec((1, 1, d), bsel),
                  pl.BlockSpec((1, 1, d), bsel),
                  pl.BlockSpec((tn, d), lambda i, j: (jnp.minimum(j, fox_tiles - 1), 0)),
                  pl.BlockSpec((tn, d), lambda i, j: (jnp.maximum(j - fox_tiles, 0), 0)),
                  pl.BlockSpec((LANES, d), lambda i, j: (fox_cols // LANES, 0)),
                  pl.BlockSpec((tm, HEAD_DIM), lambda i, j: (i % tiles_per_seq, 0)),
                  pl.BlockSpec((tm, HEAD_DIM), lambda i, j: (i % tiles_per_seq, 0)),
                  pl.BlockSpec((1, HEAD_DIM), const),
                  pl.BlockSpec((1, HEAD_DIM), const),
                  pl.BlockSpec((1, LANES), const)],
        out_specs=[pl.BlockSpec((tm, tn), lambda i, j: (i, j)),
                   pl.BlockSpec((tm, LANES), lambda i, j: (i, 0))],
        out_shape=[jax.ShapeDtypeStruct((n, fox_cols + w_ret.shape[0]), BF16),
                   jax.ShapeDtypeStruct((n, LANES), F32)],
        scratch_shapes=[pltpu.VMEM((tm, d), BF16)],
        compiler_params=_params("arbitrary", "arbitrary"),
    )(x2d, norm_w, sc1, sh1, w_all, w_ret, w_all, cos_t, sin_t, qw, kw, fb)


def _cumsum_kernel(x_ref, o_ref):
    x = x_ref[0]
    r = x.shape[0]
    a = lax.broadcasted_iota(jnp.int32, (LANES, LANES), 0)
    b = lax.broadcasted_iota(jnp.int32, (LANES, LANES), 1)
    upper = (a <= b).astype(F32)
    within = jnp.dot(x, upper, precision=lax.Precision.HIGHEST, preferred_element_type=F32)
    tot = jnp.broadcast_to(within[:, LANES - 1:LANES], (r, LANES))
    ra = lax.broadcasted_iota(jnp.int32, (r, r), 0)
    rb = lax.broadcasted_iota(jnp.int32, (r, r), 1)
    strict = (rb < ra).astype(F32)
    before = jnp.dot(strict, tot, precision=lax.Precision.HIGHEST, preferred_element_type=F32)
    o_ref[0] = within + before


def _cumsum_rows(x):
    g, s = x.shape
    r = s // LANES
    out = pl.pallas_call(
        _cumsum_kernel,
        grid=(g,),
        in_specs=[pl.BlockSpec((1, r, LANES), lambda i: (i, 0, 0))],
        out_specs=pl.BlockSpec((1, r, LANES), lambda i: (i, 0, 0)),
        out_shape=jax.ShapeDtypeStruct((g, r, LANES), F32),
        compiler_params=_params("arbitrary"),
    )(x.reshape(g, r, LANES))
    return out.reshape(g, 1, s)


def _fox_kernel(tq, n_sub, first_ref, q_ref, k_ref, v_ref, cum_ref, o_ref, s_refs, m_ref, l_ref, acc_ref):
    group_id = pl.program_id(2)
    n_groups = pl.num_programs(2)
    head = pl.program_id(0) * pl.num_programs(1) + pl.program_id(1)
    n_slabs = tq // LANES

    m_ref[...] = jnp.full(m_ref.shape, NEG_BIG, F32)
    l_ref[...] = jnp.zeros(l_ref.shape, F32)
    acc_ref[...] = jnp.zeros(acc_ref.shape, F32)

    class Sub:
        def __init__(self, idx):
            self.rows = slice(idx * tq, (idx + 1) * tq)
            self.qi = n_sub * group_id + idx
            self.sa, self.sb = s_refs[2 * idx], s_refs[2 * idx + 1]
            q_start = pl.multiple_of(self.qi * tq, tq)
            self.c0 = cum_ref[0, :, pl.ds(q_start, LANES)][:, 0:1]
            self.first = first_ref[(head * n_groups + group_id) * n_sub + idx]
            self.n_off = self.qi - self.first

    def scores(sub, kb, s_ref):
        start = pl.multiple_of(kb * tq, tq)
        k = k_ref[pl.ds(start, tq), :]
        bias = (sub.c0 - cum_ref[0, :, pl.ds(start, tq)]) * LOG2E
        s_ref[...] = lax.dot_general(q_ref[sub.rows, :], k, (((1,), (1,)), ((), ())),
                                     preferred_element_type=F32) + bias

    def softmax_pv(sub, kb, s_ref, masked):
        start = pl.multiple_of(kb * tq, tq)
        if masked:
            row = lax.broadcasted_iota(jnp.int32, (LANES, LANES), 0)
            col = lax.broadcasted_iota(jnp.int32, (LANES, LANES), 1)
            for g in range(n_slabs):
                r0 = g * LANES
                slabs = [s_ref[r0:r0 + LANES, j * LANES:(j + 1) * LANES] for j in range(g + 1)]
                slabs[g] = jnp.where(col <= row, slabs[g], NEG_BIG)
                rs = slice(sub.rows.start + r0, sub.rows.start + r0 + LANES)
                update(rs, slabs, v_ref[pl.ds(start, (g + 1) * LANES), :])
        else:
            slabs = [s_ref[:, j * LANES:(j + 1) * LANES] for j in range(n_slabs)]
            update(sub.rows, slabs, v_ref[pl.ds(start, tq), :])

    def update(rs, slabs, v):
        mx = slabs[0]
        for t in slabs[1:]:
            mx = jnp.maximum(mx, t)
        m_prev = m_ref[rs, :]
        m_new = jnp.maximum(m_prev, jnp.max(mx, axis=-1, keepdims=True))
        alpha = jnp.exp2(m_prev - m_new)
        probs = [jnp.exp2(t - m_new) for t in slabs]
        psum = probs[0]
        for t in probs[1:]:
            psum = psum + t
        l_ref[rs, :] = alpha * l_ref[rs, :] + psum
        p = jnp.concatenate([t.astype(BF16) for t in probs], axis=-1)
        acc_ref[rs, :] = alpha * acc_ref[rs, :] + jnp.dot(p, v, preferred_element_type=F32)
        m_ref[rs, :] = m_new

    def sweep(sub, then):
        def pair(kb):
            scores(sub, kb + 1, sub.sb)
            softmax_pv(sub, kb, sub.sa, False)
            scores(sub, kb + 2, sub.sa)
            softmax_pv(sub, kb + 1, sub.sb, False)

        def body4(i, carry):
            pair(sub.first + 4 * i)
            pair(sub.first + 4 * i + 2)
            return carry

        def body2(i, carry):
            pair(sub.first + 2 * i)
            return carry

        n4 = sub.n_off // 4
        lax.fori_loop(0, n4, body4, 0)
        lax.fori_loop(2 * n4, sub.n_off // 2, body2, 0)

        @pl.when(sub.n_off % 2 == 0)
        def _():
            then()
            softmax_pv(sub, sub.qi, sub.sa, True)

        @pl.when(sub.n_off % 2 == 1)
        def _():
            scores(sub, sub.qi, sub.sb)
            softmax_pv(sub, sub.qi - 1, sub.sa, False)
            then()
            softmax_pv(sub, sub.qi, sub.sb, True)

    subs = [Sub(idx) for idx in range(n_sub)]
    scores(subs[0], subs[0].first, subs[0].sa)
    for sub, nxt in zip(subs, subs[1:] + [None]):
        sweep(sub, (lambda: None) if nxt is None else functools.partial(scores, nxt, nxt.first, nxt.sa))

    o_ref[...] = (acc_ref[...] / jnp.sum(l_ref[...], axis=-1, keepdims=True)).astype(BF16)


def _first_live_block(cum, tq, qk_bound):
    c0 = cum[:, 0, ::tq]
    cend = cum[:, 0, tq - 1::tq]
    gap = (c0[:, :, None] - cend[:, None, :]) * LOG2E + 2.0 * qk_bound
    nq = c0.shape[1]
    earlier = jnp.arange(nq)[None, :] < jnp.arange(nq)[:, None]
    return jnp.sum((gap < -UNDERFLOW_LOG2) & earlier[None], axis=-1).astype(jnp.int32).reshape(-1)


def _fox_attention(z, cum, qk_bound, bsz, seq, n_heads):
    n_sub = ATT_BLOCKS_PER_STEP
    tq = min(ATT_BLOCK, seq // n_sub)
    nq = seq // tq
    n_groups = nq // n_sub
    rows = n_sub * tq
    kern = functools.partial(_fox_kernel, tq, n_sub)
    grid_spec = pltpu.PrefetchScalarGridSpec(
        num_scalar_prefetch=1,
        grid=(bsz, n_heads, n_groups),
        in_specs=[pl.BlockSpec((rows, HEAD_DIM), lambda b, h, i, f: (b * n_groups + i, h)),
                  pl.BlockSpec((seq, HEAD_DIM), lambda b, h, i, f: (b, n_heads + h)),
                  pl.BlockSpec((seq, HEAD_DIM), lambda b, h, i, f: (b, 2 * n_heads + h)),
                  pl.BlockSpec((1, 1, seq), lambda b, h, i, f: (b * n_heads + h, 0, 0))],
        out_specs=pl.BlockSpec((rows, HEAD_DIM), lambda b, h, i, f: (b * n_groups + i, h)),
        scratch_shapes=[[pltpu.VMEM((tq, tq), F32)] * (2 * n_sub),
                        pltpu.VMEM((rows, LANES), F32), pltpu.VMEM((rows, LANES), F32),
                        pltpu.VMEM((rows, HEAD_DIM), F32)],
    )
    return pl.pallas_call(
        kern,
        grid_spec=grid_spec,
        out_shape=jax.ShapeDtypeStruct((bsz * seq, n_heads * HEAD_DIM), BF16),
        compiler_params=_params("arbitrary", "arbitrary", "arbitrary"),
    )(_first_live_block(cum, tq, qk_bound), z, z, z, cum)


def _ret_kernel(chunk, n_heads, lg_ref, q_ref, k_ref, v_ref, g_ref, nw_ref, o_ref, state_ref, decay_ref,
                qdec_ref, kdec_ref):
    first = (pl.program_id(0) == 0) & (pl.program_id(1) == 0)

    @pl.when(first)
    def _():
        i = lax.broadcasted_iota(jnp.int32, (chunk, chunk), 0)
        jj = lax.broadcasted_iota(jnp.int32, (chunk, chunk), 1)
        diff = (i - jj).astype(F32)
        pos = lax.broadcasted_iota(jnp.int32, (chunk, HEAD_DIM), 0).astype(F32)
        for h in range(n_heads):
            decay_ref[h] = jnp.where(diff >= 0, jnp.exp(lg_ref[h] * jnp.maximum(diff, 0.0)), 0.0)
            qdec_ref[h] = jnp.exp(lg_ref[h] * (pos + 1.0))
            kdec_ref[h] = jnp.exp(lg_ref[h] * (chunk - 1.0 - pos))

    @pl.when(pl.program_id(1) == 0)
    def _():
        state_ref[...] = jnp.zeros(state_ref.shape, F32)

    for h in range(n_heads):
        log_g = lg_ref[h]
        cols = slice(h * HEAD_DIM, (h + 1) * HEAD_DIM)
        q = q_ref[:, cols]
        k = k_ref[:, cols]
        v = v_ref[:, cols]
        scores = lax.dot_general(q, k, (((1,), (1,)), ((), ())), preferred_element_type=F32)
        scores = scores * decay_ref[h]
        intra = jnp.dot(scores.astype(BF16), v, preferred_element_type=F32)
        state = state_ref[h]
        inter = jnp.dot(q, state.astype(BF16), preferred_element_type=F32) * qdec_ref[h]
        kd = (k.astype(F32) * kdec_ref[h]).astype(BF16)
        kv = lax.dot_general(kd, v, (((0,), (0,)), ((), ())), preferred_element_type=F32)
        state_ref[h] = state * jnp.exp(jnp.full((1, HEAD_DIM), chunk, F32) * log_g) + kv
        o = intra + inter
        ms = jnp.mean(o * o, axis=-1, keepdims=True)
        o = o * lax.rsqrt(ms + EPS) * nw_ref[:, cols]
        o_ref[:, cols] = (o * _silu(g_ref[:, cols].astype(F32))).astype(BF16)


def _retention(z, log_g, norm_w, bsz, seq, n_heads, col0):
    chunk = min(RET_CHUNK, seq)
    nt = seq // chunk
    width = n_heads * HEAD_DIM
    c0 = col0 // width
    kern = functools.partial(_ret_kernel, chunk, n_heads)

    def sec(s):
        return pl.BlockSpec((chunk, width), lambda b, t, lg: (b * nt + t, c0 + s))

    grid_spec = pltpu.PrefetchScalarGridSpec(
        num_scalar_prefetch=1,
        grid=(bsz, nt),
        in_specs=[sec(0), sec(1), sec(2), sec(3), pl.BlockSpec((1, width), lambda b, t, lg: (0, 0))],
        out_specs=pl.BlockSpec((chunk, width), lambda b, t, lg: (b * nt + t, 0)),
        scratch_shapes=[pltpu.VMEM((n_heads, HEAD_DIM, HEAD_DIM), F32),
                        pltpu.VMEM((n_heads, chunk, chunk), F32),
                        pltpu.VMEM((n_heads, chunk, HEAD_DIM), F32),
                        pltpu.VMEM((n_heads, chunk, HEAD_DIM), F32)],
    )
    return pl.pallas_call(
        kern,
        grid_spec=grid_spec,
        out_shape=jax.ShapeDtypeStruct((bsz * seq, width), BF16),
        compiler_params=_params("arbitrary", "arbitrary"),
    )(log_g, z, z, z, z, norm_w)


def _outproj_kernel(oa_ref, ob_ref, wa_ref, wb_ref, x_ref, g1_ref, nw_ref, sc_ref, sh_ref, wr_ref, br_ref,
                    x1_ref, hp_ref, lg_ref):
    mix = jnp.dot(oa_ref[...], wa_ref[...], preferred_element_type=F32)
    mix = mix + jnp.dot(ob_ref[...], wb_ref[...], preferred_element_type=F32)
    x1 = x_ref[...] + g1_ref[0] * mix
    x1_ref[...] = x1
    ms = jnp.mean(x1 * x1, axis=-1, keepdims=True)
    h2 = x1 * lax.rsqrt(ms + EPS) * nw_ref[...] * (1.0 + sc_ref[0]) + sh_ref[0]
    hp_ref[...] = _rows_to_tiles(_pack_halves(h2))
    h_hi = h2.astype(BF16)
    h_lo = (h2 - h_hi.astype(F32)).astype(BF16)
    both = jnp.dot(h_hi, wr_ref[...], preferred_element_type=F32)
    cross = jnp.dot(h_lo, wr_ref[:, :LANES], preferred_element_type=F32)
    lg_ref[...] = both[:, :LANES] + both[:, LANES:] + cross + br_ref[...]


def _output_projection(o_a, o_b, w_out, x2d, seq, g1, norm_w, sc2, sh2, w_router, b_router):
    n, d = x2d.shape
    da = o_a.shape[1]
    tm = min(OUTPROJ_ROWS, seq)
    tiles_per_seq = seq // tm
    bsel = lambda i: (i // tiles_per_seq, 0, 0)
    return pl.pallas_call(
        _outproj_kernel,
        grid=(n // tm,),
        in_specs=[pl.BlockSpec((tm, da), lambda i: (i, 0)),
                  pl.BlockSpec((tm, da), lambda i: (i, 0)),
                  pl.BlockSpec((da, d), lambda i: (0, 0)),
                  pl.BlockSpec((da, d), lambda i: (1, 0)),
                  pl.BlockSpec((tm, d), lambda i: (i, 0)),
                  pl.BlockSpec((1, 1, d), bsel),
                  pl.BlockSpec((1, d), lambda i: (0, 0)),
                  pl.BlockSpec((1, 1, d), bsel),
                  pl.BlockSpec((1, 1, d), bsel),
                  pl.BlockSpec((d, 2 * LANES), lambda i: (0, 0)),
                  pl.BlockSpec((1, LANES), lambda i: (0, 0))],
        out_specs=[pl.BlockSpec((tm, d), lambda i: (i, 0)),
                   pl.BlockSpec((tm, d // 2 // LANES, LANES), lambda i: (i, 0, 0)),
                   pl.BlockSpec((tm, LANES), lambda i: (i, 0))],
        out_shape=[jax.ShapeDtypeStruct((n, d), F32),
                   jax.ShapeDtypeStruct((n, d // 2 // LANES, LANES), U32),
                   jax.ShapeDtypeStruct((n, LANES), F32)],
        compiler_params=_params("arbitrary"),
    )(o_a, o_b, w_out, w_out, x2d, g1, norm_w, sc2, sh2, w_router, b_router)


def _route_kernel(blk, n_blocks, lg_ref, gate_ref, ids_ref, plan_ref, run_ref):
    i = pl.program_id(0)

    @pl.when(i == 0)
    def _():
        run_ref[...] = jnp.zeros(run_ref.shape, F32)

    lg = lg_ref[...]
    tt = lg.shape[0]
    lane = lax.broadcasted_iota(jnp.int32, lg.shape, 1).astype(F32)
    big = 1e6

    def rmax(v):
        return jnp.max(v, axis=-1, keepdims=True)

    def rmin(v):
        return jnp.min(v, axis=-1, keepdims=True)

    def rsum(v):
        return jnp.sum(v, axis=-1, keepdims=True)

    cmask = lane < N_GROUPS
    cm = jnp.where(cmask, lg, NEG_BIG)
    ce = jnp.where(cmask, jnp.exp(cm - rmax(cm)), 0.0)
    pgrp = ce / rsum(ce)
    p_g = rmax(pgrp)
    g_sel = rmin(jnp.where(cmask & (pgrp == p_g), lane, big))

    lo = N_GROUPS + EXPERTS_PER_GROUP * g_sel
    fmask = (lane >= lo) & (lane < lo + EXPERTS_PER_GROUP)
    fm = jnp.where(fmask, lg, NEG_BIG)
    fe = jnp.where(fmask, jnp.exp(fm - rmax(fm)), 0.0)
    fp = fe / rsum(fe)
    fp = jnp.where(fmask, fp, -1.0)
    p1 = rmax(fp)
    i1 = rmin(jnp.where(fp == p1, lane, big))
    fp2 = jnp.where(lane == i1, -1.0, fp)
    p2 = rmax(fp2)
    i2 = rmin(jnp.where(fp2 == p2, lane, big))
    denom = p1 + p2
    w1 = p_g * p1 / denom
    w2 = p_g * p2 / denom
    e1 = i1 - N_GROUPS
    e2 = i2 - N_GROUPS

    gate_ref[...] = jnp.where(lane == 0, w1, jnp.where(lane == 1, w2, 0.0))

    oh1 = (lane == e1).astype(F32)
    oh2 = (lane == e2).astype(F32)
    both = oh1 + oh2
    ra = lax.broadcasted_iota(jnp.int32, (tt, tt), 0)
    rb = lax.broadcasted_iota(jnp.int32, (tt, tt), 1)
    strict = (rb < ra).astype(BF16)
    prefix = jnp.dot(strict, both.astype(BF16), preferred_element_type=F32) + run_ref[...]
    r1 = rsum(prefix * oh1)
    r2 = rsum(prefix * oh2)
    run_ref[...] = run_ref[...] + jnp.sum(both, axis=0, keepdims=True)

    packed = jnp.where(lane == 0, e1, jnp.where(lane == 1, e2, jnp.where(lane == 2, r1,
                                                                        jnp.where(lane == 3, r2, 0.0))))
    ids_ref[...] = jnp.transpose(packed)[:8, :].astype(jnp.int32)

    @pl.when(i == pl.num_programs(0) - 1)
    def _():
        cnt = jnp.broadcast_to(run_ref[...], (8, LANES))
        lane8 = lax.broadcasted_iota(jnp.int32, (8, LANES), 1)
        padded = jnp.floor((cnt + (blk - 1.0)) * (1.0 / blk)) * blk
        pend = padded
        for sh in (1, 2, 4, 8, 16, 32, 64):
            pend = pend + jnp.where(lane8 >= sh, pltpu.roll(pend, sh, 1), 0.0)
        pstart = pend - padded
        total = jnp.max(pend, axis=-1, keepdims=True)
        tail = total + (lane8 - N_EXPERTS).astype(F32) * blk
        fill = jnp.where(lane8 < N_EXPERTS, jnp.where(padded > 0, pend - blk, -1.0),
                         jnp.where((lane8 < 2 * N_EXPERTS) & (tail < n_blocks * blk), tail, -1.0))
        row8 = lax.broadcasted_iota(jnp.int32, (8, LANES), 0)
        plan_ref[...] = jnp.where(row8 == 0, pstart, jnp.where(row8 == 1, fill,
                                                               jnp.where(row8 == 2, cnt, 0.0))).astype(jnp.int32)


def _route(logits, blk, n_blocks):
    n = logits.shape[0]
    tt = min(ROUTE_ROWS, n)
    blkspec = lambda: pl.BlockSpec((tt, LANES), lambda i: (i, 0))
    return pl.pallas_call(
        functools.partial(_route_kernel, blk, n_blocks),
        grid=(n // tt,),
        in_specs=[blkspec()],
        out_specs=[blkspec(),
                   pl.BlockSpec((8, tt), lambda i: (0, i)),
                   pl.BlockSpec((8, LANES), lambda i: (0, 0))],
        out_shape=[jax.ShapeDtypeStruct((n, LANES), F32),
                   jax.ShapeDtypeStruct((8, n), jnp.int32),
                   jax.ShapeDtypeStruct((8, LANES), jnp.int32)],
        scratch_shapes=[pltpu.VMEM((1, LANES), F32)],
        compiler_params=_params("arbitrary"),
    )(logits)


def _dispatch_kernel(tt, blk, n_fill, dest_ref, fill_ref, h_ref, xs_ref, zero_ref, sem, zsem):
    i = pl.program_id(0)
    n_tok = pl.num_programs(0) * tt

    @pl.when(i == 0)
    def _():
        zero_ref[...] = jnp.zeros(zero_ref.shape, U32)

        def zcopy(z):
            row = pl.multiple_of(jnp.maximum(fill_ref[z], 0), blk)
            return pltpu.make_async_copy(zero_ref, xs_ref.at[pl.ds(row, blk)], zsem)

        def zissue(z, carry):
            @pl.when(fill_ref[z] >= 0)
            def _():
                zcopy(z).start()
            return carry

        def zdrain(z, carry):
            @pl.when(fill_ref[z] >= 0)
            def _():
                zcopy(z).wait()
            return carry

        lax.fori_loop(0, n_fill, zissue, 0)
        lax.fori_loop(0, n_fill, zdrain, 0)

    def copy(r, kk):
        d = dest_ref[kk * n_tok + i * tt + r]
        return pltpu.make_async_copy(h_ref.at[r], xs_ref.at[d], sem)

    def issue(r, carry):
        for kk in range(TOP_K):
            copy(r, kk).start(priority=kk % 2)
        return carry

    lax.fori_loop(0, tt, issue, 0, unroll=8)
    for _ in range(TOP_K):
        pltpu.make_async_copy(h_ref, xs_ref.at[pl.ds(0, tt)], sem).wait()


def _dispatch(h_packed, dest_flat, fill_rows, n_slots, blk):
    n = h_packed.shape[0]
    tile = h_packed.shape[1:]
    tt = min(DISPATCH_TOKENS, n)
    n_fill = fill_rows.shape[0]
    grid_spec = pltpu.PrefetchScalarGridSpec(
        num_scalar_prefetch=2,
        grid=(n // tt,),
        in_specs=[pl.BlockSpec((tt,) + tile, lambda i, d, f: (i, 0, 0))],
        out_specs=pl.BlockSpec(memory_space=pl.ANY),
        scratch_shapes=[pltpu.VMEM((blk,) + tile, U32), pltpu.SemaphoreType.DMA(()), pltpu.SemaphoreType.DMA(())],
    )
    return pl.pallas_call(
        functools.partial(_dispatch_kernel, tt, blk, n_fill),
        grid_spec=grid_spec,
        out_shape=jax.ShapeDtypeStruct((n_slots,) + tile, U32),
        compiler_params=_params("arbitrary"),
    )(dest_flat, fill_rows, h_packed)


def _expert_kernel(blk, ahead, cnt_ref, pstart_ref, fill_ref, xs_ref, w1_ref, w3_ref, w2_ref, y_ref,
                   w1f, w3f, w2f, w1b, w3b, w2b, xbuf, ybuf, done_ref, w_sem, in_sem, out_sem):
    e = pl.program_id(0)
    n_exp = pl.num_programs(0)
    wslot = e % 2
    n_blk = (cnt_ref[e] + (blk - 1)) // blk
    base = pstart_ref[e]
    n_x = xbuf.shape[0]

    def weight_copies(ex, slot):
        return [pltpu.make_async_copy(src.at[ex], dst.at[slot], w_sem.at[slot])
                for src, dst in ((w1_ref, w1f), (w3_ref, w3f), (w2_ref, w2f))]

    def rows(b):
        return pl.ds(pl.multiple_of(base + b * blk, blk), blk)

    def in_copy(b, slot):
        return pltpu.make_async_copy(xs_ref.at[rows(b)], xbuf.at[slot], in_sem.at[slot])

    def out_copy(b, slot):
        return pltpu.make_async_copy(ybuf.at[slot], y_ref.at[rows(b)], out_sem.at[slot])

    @pl.when(e == 0)
    def _():
        for c in weight_copies(0, 0):
            c.start()

    for p in range(ahead):
        @pl.when(p < n_blk)
        def _():
            in_copy(p, p).start()

    @pl.when(e + 1 < n_exp)
    def _():
        for c in weight_copies(e + 1, 1 - wslot):
            c.start()

    for c in weight_copies(e, wslot):
        c.wait()
    w1b[...] = w1f[wslot].astype(BF16)
    w3b[...] = w3f[wslot].astype(BF16)
    w2b[...] = w2f[wslot].astype(BF16)

    @pl.when(e == 0)
    def _():
        done_ref[0] = 0

    done = done_ref[0]

    def out_wait(slot):
        pltpu.make_async_copy(ybuf.at[slot], y_ref.at[pl.ds(0, blk)], out_sem.at[slot]).wait()

    def body(b, carry):
        slot = (done + b) % 2

        @pl.when(b + ahead < n_blk)
        def _():
            in_copy(b + ahead, (b + ahead) % n_x).start()

        in_copy(b, b % n_x).wait()

        @pl.when(done + b >= 2)
        def _():
            out_wait(slot)

        lo, hi = _unpack_halves(_tiles_to_rows(xbuf[b % n_x]))
        lo = lo.astype(BF16)
        hi = hi.astype(BF16)
        half = lo.shape[1]
        a = jnp.dot(lo, w1b[:half, :], preferred_element_type=F32)
        a = a + jnp.dot(hi, w1b[half:, :], preferred_element_type=F32)
        g = jnp.dot(lo, w3b[:half, :], preferred_element_type=F32)
        g = g + jnp.dot(hi, w3b[half:, :], preferred_element_type=F32)
        mid = (_silu(a) * g).astype(BF16)
        ybuf[slot] = _rows_to_tiles(_pack_halves(jnp.dot(mid, w2b[...], preferred_element_type=F32)))
        out_copy(b, slot).start(priority=1)
        return carry

    lax.fori_loop(0, n_blk, body, 0)
    total = done + n_blk
    done_ref[0] = total

    @pl.when(e == n_exp - 1)
    def _():
        @pl.when(total >= 2)
        def _():
            out_wait(total % 2)

        @pl.when(total >= 1)
        def _():
            out_wait((total - 1) % 2)

        ybuf[0] = jnp.zeros(ybuf.shape[1:], U32)

        def zcopy(t):
            row = pl.multiple_of(jnp.maximum(fill_ref[N_EXPERTS + t], 0), blk)
            return pltpu.make_async_copy(ybuf.at[0], y_ref.at[pl.ds(row, blk)], out_sem.at[0])

        def zissue(t, carry):
            @pl.when(fill_ref[N_EXPERTS + t] >= 0)
            def _():
                zcopy(t).start()
            return carry

        def zdrain(t, carry):
            @pl.when(fill_ref[N_EXPERTS + t] >= 0)
            def _():
                zcopy(t).wait()
            return carry

        lax.fori_loop(0, N_EXPERTS, zissue, 0)
        lax.fori_loop(0, N_EXPERTS, zdrain, 0)


def _expert_blocks(xs, counts, pstart, fill_rows, w1, w3, w2, blk):
    n_slots = xs.shape[0]
    tile = xs.shape[1:]
    n_exp, d, de = w1.shape
    ahead = EXPERT_AHEAD
    hbm = pl.BlockSpec(memory_space=pl.ANY)
    grid_spec = pltpu.PrefetchScalarGridSpec(
        num_scalar_prefetch=3,
        grid=(n_exp,),
        in_specs=[hbm, hbm, hbm, hbm],
        out_specs=hbm,
        scratch_shapes=[pltpu.VMEM((2, d, de), F32), pltpu.VMEM((2, d, de), F32), pltpu.VMEM((2, de, d), F32),
                        pltpu.VMEM((d, de), BF16), pltpu.VMEM((d, de), BF16), pltpu.VMEM((de, d), BF16),
                        pltpu.VMEM((ahead + 1, blk) + tile, U32), pltpu.VMEM((2, blk) + tile, U32),
                        pltpu.SMEM((1,), jnp.int32),
                        pltpu.SemaphoreType.DMA((2,)), pltpu.SemaphoreType.DMA((ahead + 1,)),
                        pltpu.SemaphoreType.DMA((2,))],
    )
    return pl.pallas_call(
        functools.partial(_expert_kernel, blk, ahead),
        grid_spec=grid_spec,
        out_shape=jax.ShapeDtypeStruct((n_slots,) + tile, U32),
        compiler_params=_params("arbitrary"),
    )(counts, pstart, fill_rows, xs, w1, w3, w2)


def _combine_kernel(tt, n_tiles, dest_ref, x1_ref, g2_ref, gate_ref, yb_ref, o_ref, buf, sems):
    i = pl.program_id(0)

    def copy(tile, slot, r, kk):
        d = dest_ref[kk * (n_tiles * tt) + tile * tt + r]
        return pltpu.make_async_copy(yb_ref.at[d], buf.at[slot, kk, r], sems.at[slot])

    def issue_tile(tile, slot):
        def body(r, carry):
            for kk in range(TOP_K):
                copy(tile, slot, r, kk).start(priority=kk % 2)
            return carry
        lax.fori_loop(0, tt, body, 0, unroll=8)

    def wait_tile(tile, slot):
        for kk in range(TOP_K):
            pltpu.make_async_copy(yb_ref.at[pl.ds(0, tt)], buf.at[slot, kk], sems.at[slot]).wait()

    slot = i % 2

    @pl.when(i == 0)
    def _():
        issue_tile(0, 0)

    @pl.when(i + 1 < n_tiles)
    def _():
        issue_tile(i + 1, 1 - slot)

    wait_tile(i, slot)

    gate = gate_ref[...]
    wa = gate[:, 0:1]
    wb = gate[:, 1:2]
    lo_a, hi_a = _unpack_halves(_tiles_to_rows(buf[slot, 0]))
    lo_b, hi_b = _unpack_halves(_tiles_to_rows(buf[slot, 1]))
    y = jnp.concatenate([wa * lo_a + wb * lo_b, wa * hi_a + wb * hi_b], axis=-1)
    o_ref[...] = x1_ref[...] + g2_ref[0] * y


def _combine(x1, seq, g2, gates, dest_flat, yb):
    n, d = x1.shape
    tile = yb.shape[1:]
    tt = min(COMBINE_TOKENS, seq)
    n_tiles = n // tt
    tiles_per_seq = seq // tt
    grid_spec = pltpu.PrefetchScalarGridSpec(
        num_scalar_prefetch=1,
        grid=(n_tiles,),
        in_specs=[pl.BlockSpec((tt, d), lambda i, dr: (i, 0)),
                  pl.BlockSpec((1, 1, d), lambda i, dr: (i // tiles_per_seq, 0, 0)),
                  pl.BlockSpec((tt, LANES), lambda i, dr: (i, 0)),
                  pl.BlockSpec(memory_space=pl.ANY)],
        out_specs=pl.BlockSpec((tt, d), lambda i, dr: (i, 0)),
        scratch_shapes=[pltpu.VMEM((2, TOP_K, tt) + tile, U32), pltpu.SemaphoreType.DMA((2,))],
    )
    return pl.pallas_call(
        functools.partial(_combine_kernel, tt, n_tiles),
        grid_spec=grid_spec,
        out_shape=jax.ShapeDtypeStruct((n, d), F32),
        compiler_params=_params("arbitrary"),
    )(dest_flat, x1, g2, gates, yb)


def _rotation_tables(seq):
    half = HEAD_DIM // 2
    theta = ROPE_BASE ** (-np.arange(half, dtype=np.float64) / half)
    ang = np.arange(seq, dtype=np.float64)[:, None] * theta[None, :]
    cos_t = np.concatenate([np.cos(ang), np.cos(ang)], axis=-1).astype(np.float32)
    sin_t = np.concatenate([-np.sin(ang), np.sin(ang)], axis=-1).astype(np.float32)
    return jnp.asarray(cos_t), jnp.asarray(sin_t)


def _layer(x, c, w_ada, b_ada, norm1_w, w_in, forget_bias, q_norm_w, k_norm_w, ret_norm_w, w_out, norm2_w,
           w_coarse, b_coarse, w_fine, b_fine, w1, w3, w2):
    bsz, seq, d = x.shape
    n = bsz * seq
    d_fox = d // 2
    d_ret = d // 2
    n_heads = d_fox // HEAD_DIM

    mod = _ada_modulation(c, w_ada, b_ada)
    sh1, sc1, g1, sh2, sc2, g2 = [m.reshape(bsz, 1, d) for m in jnp.split(mod, 6, axis=-1)]

    f0 = 3 * d_fox
    w_all = jnp.swapaxes(w_in, 0, 1).astype(BF16)
    w_ret = w_all[f0 + n_heads:]
    fb = jnp.zeros((1, LANES), F32).at[0, :n_heads].set(forget_bias)

    cos_t, sin_t = _rotation_tables(seq)

    x2d = x.reshape(n, d)
    z, log_f = _input_projection(x2d, seq, norm1_w.reshape(1, d), sc1, sh1, w_all, w_ret, f0, cos_t, sin_t,
                                 q_norm_w.reshape(1, HEAD_DIM), k_norm_w.reshape(1, HEAD_DIM), fb)

    lf = log_f[:, :n_heads].reshape(bsz, seq, n_heads).transpose(0, 2, 1).reshape(bsz * n_heads, seq)
    cum = _cumsum_rows(lf)

    qk_bound = 1.02 * LOG2E * HEAD_DIM ** 0.5 * jnp.max(jnp.abs(q_norm_w)) * jnp.max(jnp.abs(k_norm_w))
    o_a = _fox_attention(z, cum, qk_bound, bsz, seq, n_heads)
    log_g = jnp.log(1.0 - 2.0 ** (-5.0 - jnp.arange(n_heads, dtype=F32)))
    o_b = _retention(z, log_g, ret_norm_w.reshape(1, d_ret), bsz, seq, n_heads, 3 * d_fox)

    pad = LANES - N_GROUPS - N_EXPERTS
    w_router = jnp.concatenate([w_coarse, w_fine.transpose(1, 0, 2).reshape(d, N_EXPERTS),
                                jnp.zeros((d, pad), F32)], axis=1)
    b_router = jnp.concatenate([b_coarse, b_fine.reshape(N_EXPERTS), jnp.zeros((pad,), F32)]).reshape(1, LANES)

    wr_hi = w_router.astype(BF16)
    wr_lo = (w_router - wr_hi.astype(F32)).astype(BF16)
    x1, h_packed, logits = _output_projection(o_a, o_b, w_out.astype(BF16), x2d, seq, g1,
                                              norm2_w.reshape(1, d), sc2, sh2,
                                              jnp.concatenate([wr_hi, wr_lo], axis=1), b_router)

    blk = EXPERT_BLOCK
    nk = n * TOP_K
    n_blocks = nk // blk + N_EXPERTS
    gates, ids, plan = _route(logits, blk, n_blocks)
    pstart = plan[0, :N_EXPERTS]
    fill_rows = plan[1, :2 * N_EXPERTS]
    counts = plan[2, :N_EXPERTS]
    eid = ids[0:TOP_K]
    hit = eid[None] == jnp.arange(N_EXPERTS, dtype=jnp.int32)[:, None, None]
    dest = (jnp.sum(jnp.where(hit, pstart[:, None, None], 0), axis=0) + ids[TOP_K:2 * TOP_K]).reshape(nk)

    xs = _dispatch(h_packed, dest, fill_rows, n_blocks * blk, blk)
    yb = _expert_blocks(xs, counts, pstart, fill_rows, w1, w3, w2, blk)
    out = _combine(x1, seq, g2, gates, dest, yb)
    return out.reshape(bsz, seq, d)


def kernel(x, c, w_ada, b_ada, norm1_w, w_in, forget_bias, q_norm_w, k_norm_w, ret_norm_w, w_out, norm2_w,
           w_coarse, b_coarse, w_fine, b_fine, w1, w3, w2):
    c_in = c
    for l in range(w_ada.shape[0]):
        x = _layer(x, c_in, w_ada[l], b_ada[l], norm1_w[l], w_in[l], forget_bias[l], q_norm_w[l],
                   k_norm_w[l], ret_norm_w[l], w_out[l], norm2_w[l], w_coarse[l], b_coarse[l],
                   w_fine[l], b_fine[l], w1[l], w3[l], w2[l])
    return x
```

```python
import functools

import jax
import jax.numpy as jnp
import numpy as np
from jax import lax
from jax.experimental import pallas as pl
from jax.experimental.pallas import tpu as pltpu

HEAD_DIM = 128
N_GROUPS = 4
EXPERTS_PER_GROUP = 8
N_EXPERTS = N_GROUPS * EXPERTS_PER_GROUP
TOP_K = 2
ROPE_BASE = 10000.0
EPS = 1e-6

LANES = 128
VMEM_LIMIT = 56 * 1024 * 1024
NEG_BIG = -1e30
LOG2E = 1.4426950408889634
UNDERFLOW_LOG2 = 160.0

ADA_COLS = 1024
INPROJ_ROWS = 1024
INPROJ_COLS = 1024
ATT_BLOCK = 512
ATT_BLOCKS_PER_STEP = 4
RET_CHUNK = 256
OUTPROJ_ROWS = 512
ROUTE_ROWS = 512
DISPATCH_TOKENS = 1024
COMBINE_TOKENS = 512
EXPERT_BLOCK = 256
EXPERT_AHEAD = 3

F32 = jnp.float32
BF16 = jnp.bfloat16
U32 = jnp.uint32


def _params(*sem):
    return pltpu.CompilerParams(dimension_semantics=sem, vmem_limit_bytes=VMEM_LIMIT)


def _silu(v):
    return v * (1.0 / (1.0 + jnp.exp(-v)))


def _pack_halves(y):
    w = y.shape[1] // 2
    lo = pltpu.bitcast(y[:, :w].astype(BF16).astype(F32), U32)
    hi = pltpu.bitcast(y[:, w:].astype(BF16).astype(F32), U32)
    return (hi & jnp.uint32(0xFFFF0000)) | (lo >> 16)


def _rows_to_tiles(p):
    return pltpu.einshape("m(ck)->mck", p, c=8, k=LANES)


def _tiles_to_rows(t):
    return pltpu.einshape("mck->m(ck)", t)


def _unpack_halves(p):
    lo = pltpu.bitcast(p << 16, F32)
    hi = pltpu.bitcast(p & jnp.uint32(0xFFFF0000), F32)
    return lo, hi


def _ada_kernel(bsz, ct_ref, w_ref, b_ref, o_ref):
    w = w_ref[...]
    rows = []
    for b in range(o_ref.shape[0]):
        if b < bsz:
            cb = _silu(ct_ref[:, b:b + 1])
            rows.append(jnp.sum(cb * w, axis=0, keepdims=True) + b_ref[...])
        else:
            rows.append(jnp.zeros_like(b_ref[...]))
    o_ref[...] = jnp.concatenate(rows, axis=0)


def _ada_modulation(c, w_ada, b_ada):
    bsz, d = c.shape
    n = w_ada.shape[1]
    tn = ADA_COLS
    ct = jnp.zeros((d, LANES), F32).at[:, :bsz].set(c.T)
    assert bsz <= 8, "one sublane tile of modulation rows"
    out = pl.pallas_call(
        functools.partial(_ada_kernel, bsz),
        grid=(n // tn,),
        in_specs=[pl.BlockSpec((d, LANES), lambda j: (0, 0)),
                  pl.BlockSpec((d, tn), lambda j: (0, j)),
                  pl.BlockSpec((1, tn), lambda j: (0, j))],
        out_specs=pl.BlockSpec((8, tn), lambda j: (0, j)),
        out_shape=jax.ShapeDtypeStruct((8, n), F32),
        compiler_params=_params("arbitrary"),
    )(ct, w_ada, b_ada.reshape(1, n))
    return out[:bsz]


def _inproj_kernel(q_t, r_t, x_ref, nw_ref, sc_ref, sh_ref, wa_ref, wb_ref, wf_ref, cos_ref, sin_ref,
                   qw_ref, kw_ref, fb_ref, z_ref, f_ref, h_ref):
    j = pl.program_id(1)
    r0 = 3 * q_t

    @pl.when(j == 0)
    def _():
        x = x_ref[...]
        ms = jnp.mean(x * x, axis=-1, keepdims=True)
        y = x * lax.rsqrt(ms + EPS) * nw_ref[...]
        h = (y * (1.0 + sc_ref[0]) + sh_ref[0]).astype(BF16)
        h_ref[...] = h
        t = lax.dot_general(h, wf_ref[...], (((1,), (1,)), ((), ())), preferred_element_type=F32) + fb_ref[...]
        f_ref[...] = jnp.minimum(t, 0.0) - jnp.log(1.0 + jnp.exp(-jnp.abs(t)))

    def heads_of(acc):
        return [acc[:, hh * HEAD_DIM:(hh + 1) * HEAD_DIM] for hh in range(acc.shape[1] // HEAD_DIM)]

    def head_norm(acc, w_row):
        outs = []
        for a in heads_of(acc):
            ms = jnp.mean(a * a, axis=-1, keepdims=True)
            outs.append(a * lax.rsqrt(ms + EPS) * w_row)
        return jnp.concatenate(outs, axis=-1).astype(BF16)

    def rotate(acc, scale):
        cs = cos_ref[...] * scale
        sn = sin_ref[...] * scale
        outs = [a * cs + pltpu.roll(a, HEAD_DIM // 2, 1) * sn for a in heads_of(acc)]
        return jnp.concatenate(outs, axis=-1).astype(BF16)

    def fox():
        return lax.dot_general(h_ref[...], wa_ref[...], (((1,), (1,)), ((), ())), preferred_element_type=F32)

    def ret():
        return lax.dot_general(h_ref[...], wb_ref[...], (((1,), (1,)), ((), ())), preferred_element_type=F32)

    @pl.when(j < q_t)
    def _():
        z_ref[...] = head_norm(fox(), qw_ref[...] * (LOG2E * HEAD_DIM ** -0.5))

    @pl.when((j >= q_t) & (j < 2 * q_t))
    def _():
        z_ref[...] = head_norm(fox(), kw_ref[...])

    @pl.when((j >= 2 * q_t) & (j < r0))
    def _():
        z_ref[...] = fox().astype(BF16)

    @pl.when((j >= r0) & (j < r0 + r_t))
    def _():
        z_ref[...] = rotate(ret(), 1.0)

    @pl.when((j >= r0 + r_t) & (j < r0 + 2 * r_t))
    def _():
        z_ref[...] = rotate(ret(), HEAD_DIM ** -0.5)

    @pl.when(j >= r0 + 2 * r_t)
    def _():
        z_ref[...] = ret().astype(BF16)


def _input_projection(x2d, seq, norm_w, sc1, sh1, w_all, w_ret, fox_cols, cos_t, sin_t, qw, kw, fb):
    n, d = x2d.shape
    tm, tn = min(INPROJ_ROWS, seq), INPROJ_COLS
    fox_tiles = fox_cols // tn
    ret_tiles = w_ret.shape[0] // tn
    tiles_per_seq = seq // tm
    kern = functools.partial(_inproj_kernel, fox_tiles // 3, ret_tiles // 4)
    bsel = lambda i, j: (i // tiles_per_seq, 0, 0)
    const = lambda i, j: (0, 0)
    return pl.pallas_call(
        kern,
        grid=(n // tm, fox_tiles + ret_tiles),
        in_specs=[pl.BlockSpec((tm, d), lambda i, j: (i, 0)),
                  pl.BlockSpec((1, d), const),
                  pl.BlockSpec((1, 1, d), bsel),
                  pl.BlockSpec((1, 1, d), bsel),
                  pl.BlockSpec((tn, d), lambda i, j: (jnp.minimum(j, fox_tiles - 1), 0)),
                  pl.BlockSpec((tn, d), lambda i, j: (jnp.maximum(j - fox_tiles, 0), 0)),
                  pl.BlockSpec((LANES, d), lambda i, j: (fox_cols // LANES, 0)),
                  pl.BlockSpec((tm, HEAD_DIM), lambda i, j: (i % tiles_per_seq, 0)),
                  pl.BlockSpec((tm, HEAD_DIM), lambda i, j: (i % tiles_per_seq, 0)),
                  pl.BlockSpec((1, HEAD_DIM), const),
                  pl.BlockSpec((1, HEAD_DIM), const),
                  pl.BlockSpec((1, LANES), const)],
        out_specs=[pl.BlockSpec((tm, tn), lambda i, j: (i, j)),
                   pl.BlockSpec((tm, LANES), lambda i, j: (i, 0))],
        out_shape=[jax.ShapeDtypeStruct((n, fox_cols + w_ret.shape[0]), BF16),
                   jax.ShapeDtypeStruct((n, LANES), F32)],
        scratch_shapes=[pltpu.VMEM((tm, d), BF16)],
        compiler_params=_params("arbitrary", "arbitrary"),
    )(x2d, norm_w, sc1, sh1, w_all, w_ret, w_all, cos_t, sin_t, qw, kw, fb)


def _cumsum_kernel(r, x_ref, o_ref):
    x = x_ref[...]
    n = x.shape[0]
    a = lax.broadcasted_iota(jnp.int32, (LANES, LANES), 0)
    b = lax.broadcasted_iota(jnp.int32, (LANES, LANES), 1)
    upper = (a <= b).astype(F32)
    within = jnp.dot(x, upper, precision=lax.Precision.HIGHEST, preferred_element_type=F32)
    tot = jnp.broadcast_to(within[:, LANES - 1:LANES], (n, LANES))
    ra = lax.broadcasted_iota(jnp.int32, (n, n), 0)
    rb = lax.broadcasted_iota(jnp.int32, (n, n), 1)
    strict = ((rb < ra) & (jnp.bitwise_xor(ra, rb) < r)).astype(F32)
    before = jnp.dot(strict, tot, precision=lax.Precision.HIGHEST, preferred_element_type=F32)
    o_ref[...] = within + before


def _cumsum_rows(x):
    g, s = x.shape
    r = s // LANES
    assert r & (r - 1) == 0, "rows per sequence must be a power of two"
    out = pl.pallas_call(
        functools.partial(_cumsum_kernel, r),
        grid=(1,),
        in_specs=[pl.BlockSpec((g * r, LANES), lambda i: (0, 0))],
        out_specs=pl.BlockSpec((g * r, LANES), lambda i: (0, 0)),
        out_shape=jax.ShapeDtypeStruct((g * r, LANES), F32),
        compiler_params=_params("arbitrary"),
    )(x.reshape(g * r, LANES))
    return out.reshape(g, 1, s)


def _fox_kernel(tq, n_sub, first_ref, q_ref, k_ref, v_ref, cum_ref, o_ref, s_refs, m_ref, l_ref, acc_ref):
    group_id = pl.program_id(2)
    n_groups = pl.num_programs(2)
    head = pl.program_id(0) * pl.num_programs(1) + pl.program_id(1)
    n_slabs = tq // LANES

    m_ref[...] = jnp.full(m_ref.shape, NEG_BIG, F32)
    l_ref[...] = jnp.zeros(l_ref.shape, F32)
    acc_ref[...] = jnp.zeros(acc_ref.shape, F32)

    class Sub:
        def __init__(self, idx):
            self.rows = slice(idx * tq, (idx + 1) * tq)
            self.qi = n_sub * group_id + idx
            self.sa, self.sb = s_refs[2 * idx], s_refs[2 * idx + 1]
            q_start = pl.multiple_of(self.qi * tq, tq)
            self.c0 = cum_ref[0, :, pl.ds(q_start, LANES)][:, 0:1]
            self.first = first_ref[(head * n_groups + group_id) * n_sub + idx]
            self.n_off = self.qi - self.first

    def scores(sub, kb, s_ref):
        start = pl.multiple_of(kb * tq, tq)
        k = k_ref[pl.ds(start, tq), :]
        bias = (sub.c0 - cum_ref[0, :, pl.ds(start, tq)]) * LOG2E
        s_ref[...] = lax.dot_general(q_ref[sub.rows, :], k, (((1,), (1,)), ((), ())),
                                     preferred_element_type=F32) + bias

    def softmax_pv(sub, kb, s_ref, masked):
        start = pl.multiple_of(kb * tq, tq)
        if masked:
            row = lax.broadcasted_iota(jnp.int32, (LANES, LANES), 0)
            col = lax.broadcasted_iota(jnp.int32, (LANES, LANES), 1)
            for g in range(n_slabs):
                r0 = g * LANES
                slabs = [s_ref[r0:r0 + LANES, j * LANES:(j + 1) * LANES] for j in range(g + 1)]
                slabs[g] = jnp.where(col <= row, slabs[g], NEG_BIG)
                rs = slice(sub.rows.start + r0, sub.rows.start + r0 + LANES)
                update(rs, slabs, v_ref[pl.ds(start, (g + 1) * LANES), :])
        else:
            slabs = [s_ref[:, j * LANES:(j + 1) * LANES] for j in range(n_slabs)]
            update(sub.rows, slabs, v_ref[pl.ds(start, tq), :])

    def update(rs, slabs, v):
        mx = slabs[0]
        for t in slabs[1:]:
            mx = jnp.maximum(mx, t)
        m_prev = m_ref[rs, :]
        m_new = jnp.maximum(m_prev, jnp.max(mx, axis=-1, keepdims=True))
        alpha = jnp.exp2(m_prev - m_new)
        probs = [jnp.exp2(t - m_new) for t in slabs]
        psum = probs[0]
        for t in probs[1:]:
            psum = psum + t
        l_ref[rs, :] = alpha * l_ref[rs, :] + psum
        p = jnp.concatenate([t.astype(BF16) for t in probs], axis=-1)
        acc_ref[rs, :] = alpha * acc_ref[rs, :] + jnp.dot(p, v, preferred_element_type=F32)
        m_ref[rs, :] = m_new

    def sweep(sub, then):
        def pair(kb):
            scores(sub, kb + 1, sub.sb)
            softmax_pv(sub, kb, sub.sa, False)
            scores(sub, kb + 2, sub.sa)
            softmax_pv(sub, kb + 1, sub.sb, False)

        def body4(i, carry):
            pair(sub.first + 4 * i)
            pair(sub.first + 4 * i + 2)
            return carry

        def body2(i, carry):
            pair(sub.first + 2 * i)
            return carry

        n4 = sub.n_off // 4
        lax.fori_loop(0, n4, body4, 0)
        lax.fori_loop(2 * n4, sub.n_off // 2, body2, 0)

        @pl.when(sub.n_off % 2 == 0)
        def _():
            then()
            softmax_pv(sub, sub.qi, sub.sa, True)

        @pl.when(sub.n_off % 2 == 1)
        def _():
            scores(sub, sub.qi, sub.sb)
            softmax_pv(sub, sub.qi - 1, sub.sa, False)
            then()
            softmax_pv(sub, sub.qi, sub.sb, True)

    subs = [Sub(idx) for idx in range(n_sub)]
    scores(subs[0], subs[0].first, subs[0].sa)
    for sub, nxt in zip(subs, subs[1:] + [None]):
        sweep(sub, (lambda: None) if nxt is None else functools.partial(scores, nxt, nxt.first, nxt.sa))

    o_ref[...] = (acc_ref[...] / jnp.sum(l_ref[...], axis=-1, keepdims=True)).astype(BF16)


def _first_live_block(cum, tq, qk_bound):
    c0 = cum[:, 0, ::tq]
    cend = cum[:, 0, tq - 1::tq]
    gap = (c0[:, :, None] - cend[:, None, :]) * LOG2E + 2.0 * qk_bound
    nq = c0.shape[1]
    earlier = jnp.arange(nq)[None, :] < jnp.arange(nq)[:, None]
    return jnp.sum((gap < -UNDERFLOW_LOG2) & earlier[None], axis=-1).astype(jnp.int32).reshape(-1)


def _fox_attention(z, cum, qk_bound, bsz, seq, n_heads):
    n_sub = ATT_BLOCKS_PER_STEP
    tq = min(ATT_BLOCK, seq // n_sub)
    nq = seq // tq
    n_groups = nq // n_sub
    rows = n_sub * tq
    kern = functools.partial(_fox_kernel, tq, n_sub)
    grid_spec = pltpu.PrefetchScalarGridSpec(
        num_scalar_prefetch=1,
        grid=(bsz, n_heads, n_groups),
        in_specs=[pl.BlockSpec((rows, HEAD_DIM), lambda b, h, i, f: (b * n_groups + i, h)),
                  pl.BlockSpec((seq, HEAD_DIM), lambda b, h, i, f: (b, n_heads + h)),
                  pl.BlockSpec((seq, HEAD_DIM), lambda b, h, i, f: (b, 2 * n_heads + h)),
                  pl.BlockSpec((1, 1, seq), lambda b, h, i, f: (b * n_heads + h, 0, 0))],
        out_specs=pl.BlockSpec((rows, HEAD_DIM), lambda b, h, i, f: (b * n_groups + i, h)),
        scratch_shapes=[[pltpu.VMEM((tq, tq), F32)] * (2 * n_sub),
                        pltpu.VMEM((rows, LANES), F32), pltpu.VMEM((rows, LANES), F32),
                        pltpu.VMEM((rows, HEAD_DIM), F32)],
    )
    return pl.pallas_call(
        kern,
        grid_spec=grid_spec,
        out_shape=jax.ShapeDtypeStruct((bsz * seq, n_heads * HEAD_DIM), BF16),
        compiler_params=_params("arbitrary", "arbitrary", "arbitrary"),
    )(_first_live_block(cum, tq, qk_bound), z, z, z, cum)


def _ret_kernel(chunk, n_heads, lg_ref, q_ref, k_ref, v_ref, g_ref, nw_ref, o_ref, state_ref, decay_ref,
                qdec_ref, kdec_ref):
    first = (pl.program_id(0) == 0) & (pl.program_id(1) == 0)

    @pl.when(first)
    def _():
        i = lax.broadcasted_iota(jnp.int32, (chunk, chunk), 0)
        jj = lax.broadcasted_iota(jnp.int32, (chunk, chunk), 1)
        diff = (i - jj).astype(F32)
        pos = lax.broadcasted_iota(jnp.int32, (chunk, HEAD_DIM), 0).astype(F32)
        for h in range(n_heads):
            decay_ref[h] = jnp.where(diff >= 0, jnp.exp(lg_ref[h] * jnp.maximum(diff, 0.0)), 0.0)
            qdec_ref[h] = jnp.exp(lg_ref[h] * (pos + 1.0))
            kdec_ref[h] = jnp.exp(lg_ref[h] * (chunk - 1.0 - pos))

    @pl.when(pl.program_id(1) == 0)
    def _():
        state_ref[...] = jnp.zeros(state_ref.shape, F32)

    for h in range(n_heads):
        log_g = lg_ref[h]
        cols = slice(h * HEAD_DIM, (h + 1) * HEAD_DIM)
        q = q_ref[:, cols]
        k = k_ref[:, cols]
        v = v_ref[:, cols]
        scores = lax.dot_general(q, k, (((1,), (1,)), ((), ())), preferred_element_type=F32)
        scores = scores * decay_ref[h]
        intra = jnp.dot(scores.astype(BF16), v, preferred_element_type=F32)
        state = state_ref[h]
        inter = jnp.dot(q, state.astype(BF16), preferred_element_type=F32) * qdec_ref[h]
        kd = (k.astype(F32) * kdec_ref[h]).astype(BF16)
        kv = lax.dot_general(kd, v, (((0,), (0,)), ((), ())), preferred_element_type=F32)
        state_ref[h] = state * jnp.exp(jnp.full((1, HEAD_DIM), chunk, F32) * log_g) + kv
        o = intra + inter
        ms = jnp.mean(o * o, axis=-1, keepdims=True)
        o = o * lax.rsqrt(ms + EPS) * nw_ref[:, cols]
        o_ref[:, cols] = (o * _silu(g_ref[:, cols].astype(F32))).astype(BF16)


def _retention(z, log_g, norm_w, bsz, seq, n_heads, col0):
    chunk = min(RET_CHUNK, seq)
    nt = seq // chunk
    width = n_heads * HEAD_DIM
    c0 = col0 // width
    kern = functools.partial(_ret_kernel, chunk, n_heads)

    def sec(s):
        return pl.BlockSpec((chunk, width), lambda b, t, lg: (b * nt + t, c0 + s))

    grid_spec = pltpu.PrefetchScalarGridSpec(
        num_scalar_prefetch=1,
        grid=(bsz, nt),
        in_specs=[sec(0), sec(1), sec(2), sec(3), pl.BlockSpec((1, width), lambda b, t, lg: (0, 0))],
        out_specs=pl.BlockSpec((chunk, width), lambda b, t, lg: (b * nt + t, 0)),
        scratch_shapes=[pltpu.VMEM((n_heads, HEAD_DIM, HEAD_DIM), F32),
                        pltpu.VMEM((n_heads, chunk, chunk), F32),
                        pltpu.VMEM((n_heads, chunk, HEAD_DIM), F32),
                        pltpu.VMEM((n_heads, chunk, HEAD_DIM), F32)],
    )
    return pl.pallas_call(
        kern,
        grid_spec=grid_spec,
        out_shape=jax.ShapeDtypeStruct((bsz * seq, width), BF16),
        compiler_params=_params("arbitrary", "arbitrary"),
    )(log_g, z, z, z, z, norm_w)


def _outproj_kernel(oa_ref, ob_ref, wa_ref, wb_ref, x_ref, g1_ref, nw_ref, sc_ref, sh_ref, wr_ref, br_ref,
                    x1_ref, hp_ref, lg_ref):
    mix = jnp.dot(oa_ref[...], wa_ref[...], preferred_element_type=F32)
    mix = mix + jnp.dot(ob_ref[...], wb_ref[...], preferred_element_type=F32)
    x1 = x_ref[...] + g1_ref[0] * mix
    x1_ref[...] = x1
    ms = jnp.mean(x1 * x1, axis=-1, keepdims=True)
    h2 = x1 * lax.rsqrt(ms + EPS) * nw_ref[...] * (1.0 + sc_ref[0]) + sh_ref[0]
    hp_ref[...] = _rows_to_tiles(_pack_halves(h2))
    h_hi = h2.astype(BF16)
    h_lo = (h2 - h_hi.astype(F32)).astype(BF16)
    both = jnp.dot(h_hi, wr_ref[...], preferred_element_type=F32)
    cross = jnp.dot(h_lo, wr_ref[:, :LANES], preferred_element_type=F32)
    lg_ref[...] = both[:, :LANES] + both[:, LANES:] + cross + br_ref[...]


def _output_projection(o_a, o_b, w_out, x2d, seq, g1, norm_w, sc2, sh2, w_router, b_router):
    n, d = x2d.shape
    da = o_a.shape[1]
    tm = min(OUTPROJ_ROWS, seq)
    tiles_per_seq = seq // tm
    bsel = lambda i: (i // tiles_per_seq, 0, 0)
    return pl.pallas_call(
        _outproj_kernel,
        grid=(n // tm,),
        in_specs=[pl.BlockSpec((tm, da), lambda i: (i, 0)),
                  pl.BlockSpec((tm, da), lambda i: (i, 0)),
                  pl.BlockSpec((da, d), lambda i: (0, 0)),
                  pl.BlockSpec((da, d), lambda i: (1, 0)),
                  pl.BlockSpec((tm, d), lambda i: (i, 0)),
                  pl.BlockSpec((1, 1, d), bsel),
                  pl.BlockSpec((1, d), lambda i: (0, 0)),
                  pl.BlockSpec((1, 1, d), bsel),
                  pl.BlockSpec((1, 1, d), bsel),
                  pl.BlockSpec((d, 2 * LANES), lambda i: (0, 0)),
                  pl.BlockSpec((1, LANES), lambda i: (0, 0))],
        out_specs=[pl.BlockSpec((tm, d), lambda i: (i, 0)),
                   pl.BlockSpec((tm, d // 2 // LANES, LANES), lambda i: (i, 0, 0)),
                   pl.BlockSpec((tm, LANES), lambda i: (i, 0))],
        out_shape=[jax.ShapeDtypeStruct((n, d), F32),
                   jax.ShapeDtypeStruct((n, d // 2 // LANES, LANES), U32),
                   jax.ShapeDtypeStruct((n, LANES), F32)],
        compiler_params=_params("arbitrary"),
    )(o_a, o_b, w_out, w_out, x2d, g1, norm_w, sc2, sh2, w_router, b_router)


def _route_kernel(blk, n_blocks, lg_ref, gate_ref, ids_ref, plan_ref, run_ref):
    i = pl.program_id(0)

    @pl.when(i == 0)
    def _():
        run_ref[...] = jnp.zeros(run_ref.shape, F32)

    lg = lg_ref[...]
    tt = lg.shape[0]
    lane = lax.broadcasted_iota(jnp.int32, lg.shape, 1).astype(F32)
    big = 1e6

    def rmax(v):
        return jnp.max(v, axis=-1, keepdims=True)

    def rmin(v):
        return jnp.min(v, axis=-1, keepdims=True)

    def rsum(v):
        return jnp.sum(v, axis=-1, keepdims=True)

    cmask = lane < N_GROUPS
    cm = jnp.where(cmask, lg, NEG_BIG)
    ce = jnp.where(cmask, jnp.exp(cm - rmax(cm)), 0.0)
    pgrp = ce / rsum(ce)
    p_g = rmax(pgrp)
    g_sel = rmin(jnp.where(cmask & (pgrp == p_g), lane, big))

    lo = N_GROUPS + EXPERTS_PER_GROUP * g_sel
    fmask = (lane >= lo) & (lane < lo + EXPERTS_PER_GROUP)
    fm = jnp.where(fmask, lg, NEG_BIG)
    fe = jnp.where(fmask, jnp.exp(fm - rmax(fm)), 0.0)
    fp = fe / rsum(fe)
    fp = jnp.where(fmask, fp, -1.0)
    p1 = rmax(fp)
    i1 = rmin(jnp.where(fp == p1, lane, big))
    fp2 = jnp.where(lane == i1, -1.0, fp)
    p2 = rmax(fp2)
    i2 = rmin(jnp.where(fp2 == p2, lane, big))
    denom = p1 + p2
    w1 = p_g * p1 / denom
    w2 = p_g * p2 / denom
    e1 = i1 - N_GROUPS
    e2 = i2 - N_GROUPS

    gate_ref[...] = jnp.where(lane == 0, w1, jnp.where(lane == 1, w2, 0.0))

    oh1 = (lane == e1).astype(F32)
    oh2 = (lane == e2).astype(F32)
    both = oh1 + oh2
    ra = lax.broadcasted_iota(jnp.int32, (tt, tt), 0)
    rb = lax.broadcasted_iota(jnp.int32, (tt, tt), 1)
    strict = (rb < ra).astype(BF16)
    prefix = jnp.dot(strict, both.astype(BF16), preferred_element_type=F32) + run_ref[...]
    r1 = rsum(prefix * oh1)
    r2 = rsum(prefix * oh2)
    run_ref[...] = run_ref[...] + jnp.sum(both, axis=0, keepdims=True)

    packed = jnp.where(lane == 0, e1, jnp.where(lane == 1, e2, jnp.where(lane == 2, r1,
                                                                        jnp.where(lane == 3, r2, 0.0))))
    ids_ref[...] = jnp.transpose(packed)[:8, :].astype(jnp.int32)

    @pl.when(i == pl.num_programs(0) - 1)
    def _():
        cnt = jnp.broadcast_to(run_ref[...], (8, LANES))
        lane8 = lax.broadcasted_iota(jnp.int32, (8, LANES), 1)
        padded = jnp.floor((cnt + (blk - 1.0)) * (1.0 / blk)) * blk
        pend = padded
        for sh in (1, 2, 4, 8, 16, 32, 64):
            pend = pend + jnp.where(lane8 >= sh, pltpu.roll(pend, sh, 1), 0.0)
        pstart = pend - padded
        total = jnp.max(pend, axis=-1, keepdims=True)
        tail = total + (lane8 - N_EXPERTS).astype(F32) * blk
        fill = jnp.where(lane8 < N_EXPERTS, jnp.where(padded > 0, pend - blk, -1.0),
                         jnp.where((lane8 < 2 * N_EXPERTS) & (tail < n_blocks * blk), tail, -1.0))
        row8 = lax.broadcasted_iota(jnp.int32, (8, LANES), 0)
        plan_ref[...] = jnp.where(row8 == 0, pstart, jnp.where(row8 == 1, fill,
                                                               jnp.where(row8 == 2, cnt, 0.0))).astype(jnp.int32)


def _route(logits, blk, n_blocks):
    n = logits.shape[0]
    tt = min(ROUTE_ROWS, n)
    blkspec = lambda: pl.BlockSpec((tt, LANES), lambda i: (i, 0))
    return pl.pallas_call(
        functools.partial(_route_kernel, blk, n_blocks),
        grid=(n // tt,),
        in_specs=[blkspec()],
        out_specs=[blkspec(),
                   pl.BlockSpec((8, tt), lambda i: (0, i)),
                   pl.BlockSpec((8, LANES), lambda i: (0, 0))],
        out_shape=[jax.ShapeDtypeStruct((n, LANES), F32),
                   jax.ShapeDtypeStruct((8, n), jnp.int32),
                   jax.ShapeDtypeStruct((8, LANES), jnp.int32)],
        scratch_shapes=[pltpu.VMEM((1, LANES), F32)],
        compiler_params=_params("arbitrary"),
    )(logits)


def _dispatch_kernel(tt, blk, n_fill, dest_ref, fill_ref, h_ref, xs_ref, zero_ref, sem, zsem):
    i = pl.program_id(0)
    n_tok = pl.num_programs(0) * tt

    @pl.when(i == 0)
    def _():
        zero_ref[...] = jnp.zeros(zero_ref.shape, U32)

        def zcopy(z):
            row = pl.multiple_of(jnp.maximum(fill_ref[z], 0), blk)
            return pltpu.make_async_copy(zero_ref, xs_ref.at[pl.ds(row, blk)], zsem)

        def zissue(z, carry):
            @pl.when(fill_ref[z] >= 0)
            def _():
                zcopy(z).start()
            return carry

        def zdrain(z, carry):
            @pl.when(fill_ref[z] >= 0)
            def _():
                zcopy(z).wait()
            return carry

        lax.fori_loop(0, n_fill, zissue, 0)
        lax.fori_loop(0, n_fill, zdrain, 0)

    def copy(r, kk):
        d = dest_ref[kk * n_tok + i * tt + r]
        return pltpu.make_async_copy(h_ref.at[r], xs_ref.at[d], sem)

    def issue(r, carry):
        for kk in range(TOP_K):
            copy(r, kk).start(priority=kk % 2)
        return carry

    lax.fori_loop(0, tt, issue, 0, unroll=8)
    for _ in range(TOP_K):
        pltpu.make_async_copy(h_ref, xs_ref.at[pl.ds(0, tt)], sem).wait()


def _dispatch(h_packed, dest_flat, fill_rows, n_slots, blk):
    n = h_packed.shape[0]
    tile = h_packed.shape[1:]
    tt = min(DISPATCH_TOKENS, n)
    n_fill = fill_rows.shape[0]
    grid_spec = pltpu.PrefetchScalarGridSpec(
        num_scalar_prefetch=2,
        grid=(n // tt,),
        in_specs=[pl.BlockSpec((tt,) + tile, lambda i, d, f: (i, 0, 0))],
        out_specs=pl.BlockSpec(memory_space=pl.ANY),
        scratch_shapes=[pltpu.VMEM((blk,) + tile, U32), pltpu.SemaphoreType.DMA(()), pltpu.SemaphoreType.DMA(())],
    )
    return pl.pallas_call(
        functools.partial(_dispatch_kernel, tt, blk, n_fill),
        grid_spec=grid_spec,
        out_shape=jax.ShapeDtypeStruct((n_slots,) + tile, U32),
        compiler_params=_params("arbitrary"),
    )(dest_flat, fill_rows, h_packed)


def _expert_kernel(blk, ahead, cnt_ref, pstart_ref, fill_ref, xs_ref, w1_ref, w3_ref, w2_ref, y_ref,
                   w1f, w3f, w2f, w1b, w3b, w2b, xbuf, ybuf, done_ref, w_sem, in_sem, out_sem):
    e = pl.program_id(0)
    n_exp = pl.num_programs(0)
    wslot = e % 2
    n_blk = (cnt_ref[e] + (blk - 1)) // blk
    base = pstart_ref[e]
    n_x = xbuf.shape[0]

    def weight_copies(ex, slot):
        return [pltpu.make_async_copy(src.at[ex], dst.at[slot], w_sem.at[slot])
                for src, dst in ((w1_ref, w1f), (w3_ref, w3f), (w2_ref, w2f))]

    def rows(b):
        return pl.ds(pl.multiple_of(base + b * blk, blk), blk)

    def in_copy(b, slot):
        return pltpu.make_async_copy(xs_ref.at[rows(b)], xbuf.at[slot], in_sem.at[slot])

    def out_copy(b, slot):
        return pltpu.make_async_copy(ybuf.at[slot], y_ref.at[rows(b)], out_sem.at[slot])

    @pl.when(e == 0)
    def _():
        for c in weight_copies(0, 0):
            c.start()

    for p in range(ahead):
        @pl.when(p < n_blk)
        def _():
            in_copy(p, p).start()

    @pl.when(e + 1 < n_exp)
    def _():
        for c in weight_copies(e + 1, 1 - wslot):
            c.start()

    for c in weight_copies(e, wslot):
        c.wait()
    w1b[...] = w1f[wslot].astype(BF16)
    w3b[...] = w3f[wslot].astype(BF16)
    w2b[...] = w2f[wslot].astype(BF16)

    @pl.when(e == 0)
    def _():
        done_ref[0] = 0

    done = done_ref[0]

    def out_wait(slot):
        pltpu.make_async_copy(ybuf.at[slot], y_ref.at[pl.ds(0, blk)], out_sem.at[slot]).wait()

    def body(b, carry):
        slot = (done + b) % 2

        @pl.when(b + ahead < n_blk)
        def _():
            in_copy(b + ahead, (b + ahead) % n_x).start()

        in_copy(b, b % n_x).wait()

        @pl.when(done + b >= 2)
        def _():
            out_wait(slot)

        lo, hi = _unpack_halves(_tiles_to_rows(xbuf[b % n_x]))
        lo = lo.astype(BF16)
        hi = hi.astype(BF16)
        half = lo.shape[1]
        a = jnp.dot(lo, w1b[:half, :], preferred_element_type=F32)
        a = a + jnp.dot(hi, w1b[half:, :], preferred_element_type=F32)
        g = jnp.dot(lo, w3b[:half, :], preferred_element_type=F32)
        g = g + jnp.dot(hi, w3b[half:, :], preferred_element_type=F32)
        mid = (_silu(a) * g).astype(BF16)
        ybuf[slot] = _rows_to_tiles(_pack_halves(jnp.dot(mid, w2b[...], preferred_element_type=F32)))
        out_copy(b, slot).start(priority=1)
        return carry

    lax.fori_loop(0, n_blk, body, 0)
    total = done + n_blk
    done_ref[0] = total

    @pl.when(e == n_exp - 1)
    def _():
        @pl.when(total >= 2)
        def _():
            out_wait(total % 2)

        @pl.when(total >= 1)
        def _():
            out_wait((total - 1) % 2)

        ybuf[0] = jnp.zeros(ybuf.shape[1:], U32)

        def zcopy(t):
            row = pl.multiple_of(jnp.maximum(fill_ref[N_EXPERTS + t], 0), blk)
            return pltpu.make_async_copy(ybuf.at[0], y_ref.at[pl.ds(row, blk)], out_sem.at[0])

        def zissue(t, carry):
            @pl.when(fill_ref[N_EXPERTS + t] >= 0)
            def _():
                zcopy(t).start()
            return carry

        def zdrain(t, carry):
            @pl.when(fill_ref[N_EXPERTS + t] >= 0)
            def _():
                zcopy(t).wait()
            return carry

        lax.fori_loop(0, N_EXPERTS, zissue, 0)
        lax.fori_loop(0, N_EXPERTS, zdrain, 0)


def _expert_blocks(xs, counts, pstart, fill_rows, w1, w3, w2, blk):
    n_slots = xs.shape[0]
    tile = xs.shape[1:]
    n_exp, d, de = w1.shape
    ahead = EXPERT_AHEAD
    hbm = pl.BlockSpec(memory_space=pl.ANY)
    grid_spec = pltpu.PrefetchScalarGridSpec(
        num_scalar_prefetch=3,
        grid=(n_exp,),
        in_specs=[hbm, hbm, hbm, hbm],
        out_specs=hbm,
        scratch_shapes=[pltpu.VMEM((2, d, de), F32), pltpu.VMEM((2, d, de), F32), pltpu.VMEM((2, de, d), F32),
                        pltpu.VMEM((d, de), BF16), pltpu.VMEM((d, de), BF16), pltpu.VMEM((de, d), BF16),
                        pltpu.VMEM((ahead + 1, blk) + tile, U32), pltpu.VMEM((2, blk) + tile, U32),
                        pltpu.SMEM((1,), jnp.int32),
                        pltpu.SemaphoreType.DMA((2,)), pltpu.SemaphoreType.DMA((ahead + 1,)),
                        pltpu.SemaphoreType.DMA((2,))],
    )
    return pl.pallas_call(
        functools.partial(_expert_kernel, blk, ahead),
        grid_spec=grid_spec,
        out_shape=jax.ShapeDtypeStruct((n_slots,) + tile, U32),
        compiler_params=_params("arbitrary"),
    )(counts, pstart, fill_rows, xs, w1, w3, w2)


def _combine_kernel(tt, n_tiles, dest_ref, x1_ref, g2_ref, gate_ref, yb_ref, o_ref, buf, sems):
    i = pl.program_id(0)

    def copy(tile, slot, r, kk):
        d = dest_ref[kk * (n_tiles * tt) + tile * tt + r]
        return pltpu.make_async_copy(yb_ref.at[d], buf.at[slot, kk, r], sems.at[slot])

    def issue_tile(tile, slot):
        def body(r, carry):
            for kk in range(TOP_K):
                copy(tile, slot, r, kk).start(priority=kk % 2)
            return carry
        lax.fori_loop(0, tt, body, 0, unroll=8)

    def wait_tile(tile, slot):
        for kk in range(TOP_K):
            pltpu.make_async_copy(yb_ref.at[pl.ds(0, tt)], buf.at[slot, kk], sems.at[slot]).wait()

    slot = i % 2

    @pl.when(i == 0)
    def _():
        issue_tile(0, 0)

    @pl.when(i + 1 < n_tiles)
    def _():
        issue_tile(i + 1, 1 - slot)

    wait_tile(i, slot)

    gate = gate_ref[...]
    wa = gate[:, 0:1]
    wb = gate[:, 1:2]
    lo_a, hi_a = _unpack_halves(_tiles_to_rows(buf[slot, 0]))
    lo_b, hi_b = _unpack_halves(_tiles_to_rows(buf[slot, 1]))
    y = jnp.concatenate([wa * lo_a + wb * lo_b, wa * hi_a + wb * hi_b], axis=-1)
    o_ref[...] = x1_ref[...] + g2_ref[0] * y


def _combine(x1, seq, g2, gates, dest_flat, yb):
    n, d = x1.shape
    tile = yb.shape[1:]
    tt = min(COMBINE_TOKENS, seq)
    n_tiles = n // tt
    tiles_per_seq = seq // tt
    grid_spec = pltpu.PrefetchScalarGridSpec(
        num_scalar_prefetch=1,
        grid=(n_tiles,),
        in_specs=[pl.BlockSpec((tt, d), lambda i, dr: (i, 0)),
                  pl.BlockSpec((1, 1, d), lambda i, dr: (i // tiles_per_seq, 0, 0)),
                  pl.BlockSpec((tt, LANES), lambda i, dr: (i, 0)),
                  pl.BlockSpec(memory_space=pl.ANY)],
        out_specs=pl.BlockSpec((tt, d), lambda i, dr: (i, 0)),
        scratch_shapes=[pltpu.VMEM((2, TOP_K, tt) + tile, U32), pltpu.SemaphoreType.DMA((2,))],
    )
    return pl.pallas_call(
        functools.partial(_combine_kernel, tt, n_tiles),
        grid_spec=grid_spec,
        out_shape=jax.ShapeDtypeStruct((n, d), F32),
        compiler_params=_params("arbitrary"),
    )(dest_flat, x1, g2, gates, yb)


def _rotation_tables(seq):
    half = HEAD_DIM // 2
    theta = ROPE_BASE ** (-np.arange(half, dtype=np.float64) / half)
    ang = np.arange(seq, dtype=np.float64)[:, None] * theta[None, :]
    cos_t = np.concatenate([np.cos(ang), np.cos(ang)], axis=-1).astype(np.float32)
    sin_t = np.concatenate([-np.sin(ang), np.sin(ang)], axis=-1).astype(np.float32)
    return jnp.asarray(cos_t), jnp.asarray(sin_t)


def _layer(x, c, w_ada, b_ada, norm1_w, w_in, forget_bias, q_norm_w, k_norm_w, ret_norm_w, w_out, norm2_w,
           w_coarse, b_coarse, w_fine, b_fine, w1, w3, w2):
    bsz, seq, d = x.shape
    n = bsz * seq
    d_fox = d // 2
    d_ret = d // 2
    n_heads = d_fox // HEAD_DIM

    mod = _ada_modulation(c, w_ada, b_ada)
    sh1, sc1, g1, sh2, sc2, g2 = [m.reshape(bsz, 1, d) for m in jnp.split(mod, 6, axis=-1)]

    f0 = 3 * d_fox
    w_all = jnp.swapaxes(w_in, 0, 1).astype(BF16)
    w_ret = w_all[f0 + n_heads:]
    fb = jnp.zeros((1, LANES), F32).at[0, :n_heads].set(forget_bias)

    cos_t, sin_t = _rotation_tables(seq)

    x2d = x.reshape(n, d)
    z, log_f = _input_projection(x2d, seq, norm1_w.reshape(1, d), sc1, sh1, w_all, w_ret, f0, cos_t, sin_t,
                                 q_norm_w.reshape(1, HEAD_DIM), k_norm_w.reshape(1, HEAD_DIM), fb)

    lf = log_f[:, :n_heads].reshape(bsz, seq, n_heads).transpose(0, 2, 1).reshape(bsz * n_heads, seq)
    cum = _cumsum_rows(lf)

    qk_bound = 1.02 * LOG2E * HEAD_DIM ** 0.5 * jnp.max(jnp.abs(q_norm_w)) * jnp.max(jnp.abs(k_norm_w))
    o_a = _fox_attention(z, cum, qk_bound, bsz, seq, n_heads)
    log_g = jnp.log(1.0 - 2.0 ** (-5.0 - jnp.arange(n_heads, dtype=F32)))
    o_b = _retention(z, log_g, ret_norm_w.reshape(1, d_ret), bsz, seq, n_heads, 3 * d_fox)

    pad = LANES - N_GROUPS - N_EXPERTS
    w_router = jnp.concatenate([w_coarse, w_fine.transpose(1, 0, 2).reshape(d, N_EXPERTS),
                                jnp.zeros((d, pad), F32)], axis=1)
    b_router = jnp.concatenate([b_coarse, b_fine.reshape(N_EXPERTS), jnp.zeros((pad,), F32)]).reshape(1, LANES)

    wr_hi = w_router.astype(BF16)
    wr_lo = (w_router - wr_hi.astype(F32)).astype(BF16)
    x1, h_packed, logits = _output_projection(o_a, o_b, w_out.astype(BF16), x2d, seq, g1,
                                              norm2_w.reshape(1, d), sc2, sh2,
                                              jnp.concatenate([wr_hi, wr_lo], axis=1), b_router)

    blk = EXPERT_BLOCK
    nk = n * TOP_K
    n_blocks = nk // blk + N_EXPERTS
    gates, ids, plan = _route(logits, blk, n_blocks)
    pstart = plan[0, :N_EXPERTS]
    fill_rows = plan[1, :2 * N_EXPERTS]
    counts = plan[2, :N_EXPERTS]
    eid = ids[0:TOP_K]
    hit = eid[None] == jnp.arange(N_EXPERTS, dtype=jnp.int32)[:, None, None]
    dest = (jnp.sum(jnp.where(hit, pstart[:, None, None], 0), axis=0) + ids[TOP_K:2 * TOP_K]).reshape(nk)

    xs = _dispatch(h_packed, dest, fill_rows, n_blocks * blk, blk)
    yb = _expert_blocks(xs, counts, pstart, fill_rows, w1, w3, w2, blk)
    out = _combine(x1, seq, g2, gates, dest, yb)
    return out.reshape(bsz, seq, d)


def kernel(x, c, w_ada, b_ada, norm1_w, w_in, forget_bias, q_norm_w, k_norm_w, ret_norm_w, w_out, norm2_w,
           w_coarse, b_coarse, w_fine, b_fine, w1, w3, w2):
    c_in = c
    for l in range(w_ada.shape[0]):
        x = _layer(x, c_in, w_ada[l], b_ada[l], norm1_w[l], w_in[l], forget_bias[l], q_norm_w[l],
                   k_norm_w[l], ret_norm_w[l], w_out[l], norm2_w[l], w_coarse[l], b_coarse[l],
                   w_fine[l], b_fine[l], w1[l], w3[l], w2[l])
    return x
```
